```python
import math
import jax
import jax.numpy as jnp
from jax import lax
import numpy as np

D_MODEL = 1024
BATCH = 8
SEQ = 2048
DEPTH = 1

CHUNK = 64
N_META = 16
D_MIX = 2 * D_MODEL
SB_WIDTH = D_MIX // 2
SB_HEAD_DIM = 64
SB_HEADS = SB_WIDTH // SB_HEAD_DIM
SB_QBLOCK = 128
SSD_WIDTH = D_MIX - SB_WIDTH
SSD_HEAD_DIM = 64
SSD_HEADS = SSD_WIDTH // SSD_HEAD_DIM
SSD_GROUPS = 2
SSD_HEADS_PER_GROUP = SSD_HEADS // SSD_GROUPS
SSD_STATE = 128
SSD_CONV = 4
SSD_XBC = SSD_WIDTH + 2 * SSD_GROUPS * SSD_STATE
IN_WIDTHS = (SB_WIDTH, SB_WIDTH, SB_WIDTH, SB_WIDTH, SSD_WIDTH, SSD_XBC, SSD_HEADS)
D_IN = sum(IN_WIDTHS)
EPS = 1e-5
DT_MIN = 1e-3
DT_MAX = 1e-1

kernel_name = "hybrid_stickbreak_ssd_block"


def rms_norm(x, w):
    xf = x.astype(jnp.float32)
    y = xf * lax.rsqrt(jnp.mean(xf * xf, axis=-1, keepdims=True) + EPS)
    return (y * w.astype(jnp.float32)).astype(x.dtype)


def split_cols(proj):
    outs = []
    start = 0
    for width in IN_WIDTHS:
        outs.append(proj[..., start:start + width])
        start += width
    return outs


def causal_depthwise_conv(x, w, b):
    c = x.shape[-1]
    y = lax.conv_general_dilated(
        x, w[:, None, :].astype(x.dtype), window_strides=(1,),
        padding=[(SSD_CONV - 1, 0)], dimension_numbers=("NWC", "WIO", "NWC"),
        feature_group_count=c)
    return y + b.astype(x.dtype)


def stick_breaking_attention(q, k, v):
    seqlen = q.shape[1]
    scale = 1.0 / math.sqrt(q.shape[-1])
    outs = []
    for q0 in range(0, seqlen, SB_QBLOCK):
        q1 = min(q0 + SB_QBLOCK, seqlen)
        qb = q[:, q0:q1]
        kb = k[:, :q1]
        vb = v[:, :q1]
        z = jnp.einsum("bqhd,bkhd->bhqk", qb, kb).astype(jnp.float32) * scale
        t_pos = jnp.arange(q0, q1)[:, None]
        s_pos = jnp.arange(q1)[None, :]
        strict = s_pos < t_pos
        log_keep = jnp.where(strict, jax.nn.log_sigmoid(-z), 0.0)
        after = lax.cumsum(log_keep, axis=3, reverse=True) - log_keep
        weights = jnp.where(strict, jnp.exp(jax.nn.log_sigmoid(z) + after), 0.0)
        outs.append(jnp.einsum("bhqk,bkhd->bqhd", weights.astype(v.dtype), vb))
    return jnp.concatenate(outs, axis=1)


def ssd_chunked(xs, dt, a, bmat, cmat, d_skip):
    f32 = jnp.float32
    bsz, seqlen, _ = xs.shape
    pad = (-seqlen) % CHUNK
    G, R, P, N = SSD_GROUPS, SSD_HEADS_PER_GROUP, SSD_HEAD_DIM, SSD_STATE
    x = xs.astype(f32).reshape(bsz, seqlen, SSD_HEADS, P)
    bm = bmat.astype(f32).reshape(bsz, seqlen, G, N)
    cm = cmat.astype(f32).reshape(bsz, seqlen, G, N)

    def front(t):
        return jnp.pad(t, ((0, 0), (pad, 0)) + ((0, 0),) * (t.ndim - 2))

    x, dt, bm, cm = front(x), front(dt), front(bm), front(cm)
    lp = seqlen + pad
    nc = lp // CHUNK
    x = x.reshape(bsz, nc, CHUNK, G, R, P)
    dt = dt.reshape(bsz, nc, CHUNK, G, R)
    bm = bm.reshape(bsz, nc, CHUNK, G, N)
    cm = cm.reshape(bsz, nc, CHUNK, G, N)

    xdt = x * dt[..., None]
    a_cum = jnp.cumsum(dt * a.reshape(G, R), axis=2)
    seg = a_cum[:, :, :, None] - a_cum[:, :, None]
    causal = jnp.tril(jnp.ones((CHUNK, CHUNK), bool))[:, :, None, None]
    decay_ls = jnp.exp(jnp.where(causal, seg, -jnp.inf))
    cb = jnp.einsum("bclgn,bcsgn->bclsg", cm, bm)
    y_diag = jnp.einsum("bclsg,bclsgr,bcsgrp->bclgrp", cb, decay_ls, xdt)
    decay_to_end = jnp.exp(a_cum[:, :, -1:] - a_cum)
    states = jnp.einsum("bclgn,bclgr,bclgrp->bcgrpn", bm, decay_to_end, xdt)
    chunk_decay = jnp.exp(a_cum[:, :, -1])

    def step(h, inp):
        st, dec = inp
        return h * dec[..., None, None] + st, h

    h0 = jnp.zeros((bsz, G, R, P, N), f32)
    _, h_start = lax.scan(step, h0, (jnp.moveaxis(states, 1, 0), jnp.moveaxis(chunk_decay, 1, 0)))
    h_start = jnp.moveaxis(h_start, 0, 1)
    y_off = jnp.einsum("bclgn,bcgrpn,bclgr->bclgrp", cm, h_start, jnp.exp(a_cum))
    y = y_diag + y_off + x * d_skip.astype(f32).reshape(G, R)[:, :, None]
    y = y.reshape(bsz, lp, SSD_WIDTH)[:, pad:]
    return y.astype(xs.dtype)


def hybrid_layer(h, norm_w, w_in, conv_w, conv_b, dt_bias, a_log, d_skip, sb_norm_w, ssd_norm_w, w_out):
    bsz, seqlen, _ = h.shape
    u = rms_norm(h, norm_w)
    proj = jnp.einsum("bld,de->ble", u, w_in)
    q, k, v, sb_gate, ssd_z, xbc, dt_raw = split_cols(proj)
    hs = (bsz, seqlen, SB_HEADS, SB_HEAD_DIM)
    o_sb = stick_breaking_attention(q.reshape(hs), k.reshape(hs), v.reshape(hs))
    o_sb = o_sb.reshape(bsz, seqlen, SB_WIDTH)
    y_sb = rms_norm(o_sb * jax.nn.silu(sb_gate), sb_norm_w)
    xbc = jax.nn.silu(causal_depthwise_conv(xbc, conv_w, conv_b))
    xs = xbc[..., :SSD_WIDTH]
    bmat = xbc[..., SSD_WIDTH:SSD_WIDTH + SSD_GROUPS * SSD_STATE]
    cmat = xbc[..., SSD_WIDTH + SSD_GROUPS * SSD_STATE:]
    dt = jax.nn.softplus(dt_raw.astype(jnp.float32) + dt_bias.astype(jnp.float32))
    a = -jnp.exp(a_log.astype(jnp.float32))
    o_ssd = ssd_chunked(xs, dt, a, bmat, cmat, d_skip)
    y_ssd = rms_norm(o_ssd * jax.nn.silu(ssd_z), ssd_norm_w)
    y = jnp.concatenate([y_sb, y_ssd], axis=-1)
    return h + jnp.einsum("ble,ed->bld", y, w_out)


def _fwd_setup_inputs(seed: int = 0) -> dict:
    key = jax.random.key(seed)
    ks = jax.random.split(key, 13)
    nrm = jax.random.normal
    x = nrm(ks[0], (BATCH, SEQ, D_MODEL), jnp.float32)
    meta_tokens = nrm(ks[1], (N_META, D_MODEL), jnp.float32)
    norm_w = 1.0 + 0.02 * nrm(ks[2], (DEPTH, D_MODEL), jnp.float32)
    w_in = nrm(ks[3], (DEPTH, D_MODEL, D_IN), jnp.float32) * D_MODEL ** -0.5
    conv_w = nrm(ks[4], (DEPTH, SSD_CONV, SSD_XBC), jnp.float32) * SSD_CONV ** -0.5
    conv_b = 0.02 * nrm(ks[5], (DEPTH, SSD_XBC), jnp.float32)
    dt0 = jnp.exp(jax.random.uniform(ks[6], (DEPTH, SSD_HEADS), jnp.float32,
                                     minval=math.log(DT_MIN), maxval=math.log(DT_MAX)))
    dt_bias = dt0 + jnp.log(-jnp.expm1(-dt0))
    a_log = jnp.log(jax.random.uniform(ks[7], (DEPTH, SSD_HEADS), jnp.float32, minval=1.0, maxval=16.0))
    d_skip = 1.0 + 0.1 * nrm(ks[8], (DEPTH, SSD_HEADS), jnp.float32)
    sb_norm_w = 1.0 + 0.02 * nrm(ks[9], (DEPTH, SB_WIDTH), jnp.float32)
    ssd_norm_w = 1.0 + 0.02 * nrm(ks[10], (DEPTH, SSD_WIDTH), jnp.float32)
    w_out = nrm(ks[11], (DEPTH, D_MIX, D_MODEL), jnp.float32) * D_MIX ** -0.5
    final_norm_w = 1.0 + 0.02 * nrm(ks[12], (D_MODEL,), jnp.float32)
    return {"x": x, "meta_tokens": meta_tokens, "norm_w": norm_w, "w_in": w_in,
            "conv_w": conv_w, "conv_b": conv_b, "dt_bias": dt_bias, "a_log": a_log,
            "d_skip": d_skip, "sb_norm_w": sb_norm_w, "ssd_norm_w": ssd_norm_w,
            "w_out": w_out, "final_norm_w": final_norm_w}


def _fwd_reference(x, meta_tokens, norm_w, w_in, conv_w, conv_b, dt_bias, a_log, d_skip,
              sb_norm_w, ssd_norm_w, w_out, final_norm_w):
    bsz = x.shape[0]
    meta = jnp.broadcast_to(meta_tokens.astype(x.dtype)[None], (bsz, N_META, D_MODEL))
    h = jnp.concatenate([meta, x], axis=1)
    for layer in range(DEPTH):
        h = hybrid_layer(h, norm_w[layer], w_in[layer], conv_w[layer], conv_b[layer],
                         dt_bias[layer], a_log[layer], d_skip[layer], sb_norm_w[layer],
                         ssd_norm_w[layer], w_out[layer])
    h = rms_norm(h, final_norm_w)
    return h[:, N_META:]


import jax as _jax
import jax.numpy as _jnp

TWIN_FORMAT = 'train_step'
FWD_PARAMS = ['x', 'meta_tokens', 'norm_w', 'w_in', 'conv_w', 'conv_b', 'dt_bias', 'a_log', 'd_skip', 'sb_norm_w', 'ssd_norm_w', 'w_out', 'final_norm_w']
TWIN_WEIGHTS = ['meta_tokens', 'norm_w', 'w_in', 'conv_w', 'conv_b', 'dt_bias', 'a_log', 'd_skip', 'sb_norm_w', 'ssd_norm_w', 'w_out', 'final_norm_w']
TWIN_DIFF_INPUT = 'x'
TWIN_INPUTS = ['x', 'meta_tokens', 'norm_w', 'w_in', 'conv_w', 'conv_b', 'dt_bias', 'a_log', 'd_skip', 'sb_norm_w', 'ssd_norm_w', 'w_out', 'final_norm_w', 'loss_target', 'm_meta_tokens', 'm_norm_w', 'm_w_in', 'm_conv_w', 'm_conv_b', 'm_dt_bias', 'm_a_log', 'm_d_skip', 'm_sb_norm_w', 'm_ssd_norm_w', 'm_w_out', 'm_final_norm_w', 'v_meta_tokens', 'v_norm_w', 'v_w_in', 'v_conv_w', 'v_conv_b', 'v_dt_bias', 'v_a_log', 'v_d_skip', 'v_sb_norm_w', 'v_ssd_norm_w', 'v_w_out', 'v_final_norm_w']
TWIN_OUTPUTS = ['loss', 'grad_x', 'grad_meta_tokens', 'grad_norm_w', 'grad_w_in', 'grad_conv_w', 'grad_conv_b', 'grad_dt_bias', 'grad_a_log', 'grad_d_skip', 'grad_sb_norm_w', 'grad_ssd_norm_w', 'grad_w_out', 'grad_final_norm_w', 'delta_meta_tokens', 'delta_norm_w', 'delta_w_in', 'delta_conv_w', 'delta_conv_b', 'delta_dt_bias', 'delta_a_log', 'delta_d_skip', 'delta_sb_norm_w', 'delta_ssd_norm_w', 'delta_w_out', 'delta_final_norm_w', 'new_m_meta_tokens', 'new_m_norm_w', 'new_m_w_in', 'new_m_conv_w', 'new_m_conv_b', 'new_m_dt_bias', 'new_m_a_log', 'new_m_d_skip', 'new_m_sb_norm_w', 'new_m_ssd_norm_w', 'new_m_w_out', 'new_m_final_norm_w', 'new_v_meta_tokens', 'new_v_norm_w', 'new_v_w_in', 'new_v_conv_w', 'new_v_conv_b', 'new_v_dt_bias', 'new_v_a_log', 'new_v_d_skip', 'new_v_sb_norm_w', 'new_v_ssd_norm_w', 'new_v_w_out', 'new_v_final_norm_w']
TWIN_LEAF_KINDS = {'loss': 'loss', 'grad_x': 'grad_x', 'grad_meta_tokens': 'grad_w', 'grad_norm_w': 'grad_w', 'grad_w_in': 'grad_w', 'grad_conv_w': 'grad_w', 'grad_conv_b': 'grad_w', 'grad_dt_bias': 'grad_w', 'grad_a_log': 'grad_w', 'grad_d_skip': 'grad_w', 'grad_sb_norm_w': 'grad_w', 'grad_ssd_norm_w': 'grad_w', 'grad_w_out': 'grad_w', 'grad_final_norm_w': 'grad_w', 'delta_meta_tokens': 'delta_w', 'delta_norm_w': 'delta_w', 'delta_w_in': 'delta_w', 'delta_conv_w': 'delta_w', 'delta_conv_b': 'delta_w', 'delta_dt_bias': 'delta_w', 'delta_a_log': 'delta_w', 'delta_d_skip': 'delta_w', 'delta_sb_norm_w': 'delta_w', 'delta_ssd_norm_w': 'delta_w', 'delta_w_out': 'delta_w', 'delta_final_norm_w': 'delta_w', 'new_m_meta_tokens': 'new_m', 'new_m_norm_w': 'new_m', 'new_m_w_in': 'new_m', 'new_m_conv_w': 'new_m', 'new_m_conv_b': 'new_m', 'new_m_dt_bias': 'new_m', 'new_m_a_log': 'new_m', 'new_m_d_skip': 'new_m', 'new_m_sb_norm_w': 'new_m', 'new_m_ssd_norm_w': 'new_m', 'new_m_w_out': 'new_m', 'new_m_final_norm_w': 'new_m', 'new_v_meta_tokens': 'new_v', 'new_v_norm_w': 'new_v', 'new_v_w_in': 'new_v', 'new_v_conv_w': 'new_v', 'new_v_conv_b': 'new_v', 'new_v_dt_bias': 'new_v', 'new_v_a_log': 'new_v', 'new_v_d_skip': 'new_v', 'new_v_sb_norm_w': 'new_v', 'new_v_ssd_norm_w': 'new_v', 'new_v_w_out': 'new_v', 'new_v_final_norm_w': 'new_v'}


def _forward(args):
    return _fwd_reference(*[args[k] for k in FWD_PARAMS])


def _output_shape():
    out = _jax.eval_shape(lambda: _forward(_fwd_setup_inputs(0)))
    return out.shape, out.dtype

N_MICROBATCH = 1
ADAM_LR = 0.001
ADAM_B1 = 0.9
ADAM_B2 = 0.999
ADAM_EPS = 1e-08
ADAM_WD = 0.01
ADAM_STEP = 10
PER_EXAMPLE_BATCH_AXIS = {'x': 0, 'loss_target': 0}
SHARED_INPUTS = []
_WEIGHT_DTYPES = {'meta_tokens': _jnp.float32, 'norm_w': _jnp.float32, 'w_in': _jnp.float32, 'conv_w': _jnp.float32, 'conv_b': _jnp.float32, 'dt_bias': _jnp.float32, 'a_log': _jnp.float32, 'd_skip': _jnp.float32, 'sb_norm_w': _jnp.float32, 'ssd_norm_w': _jnp.float32, 'w_out': _jnp.float32, 'final_norm_w': _jnp.float32}
MOMENT_SCALE = {'meta_tokens': 2.156721e-03, 'norm_w': 1.371148e-01, 'w_in': 5.434007e-02, 'conv_w': 5.457849e-02, 'conv_b': 7.453415e-02, 'dt_bias': 2.415887e-01, 'a_log': 1.765726e-01, 'd_skip': 3.620159e-01, 'sb_norm_w': 6.386721e-02, 'ssd_norm_w': 6.191723e-02, 'w_out': 8.900140e-02, 'final_norm_w': 1.601016e+01}


def _to_microbatches(a, axis):
    t = _jnp.moveaxis(a, axis, 0)
    t = t.reshape((N_MICROBATCH, t.shape[0] // N_MICROBATCH) + t.shape[1:])
    return _jnp.moveaxis(t, 1, axis + 1)


def setup_inputs(seed: int = 0) -> dict:
    inp = _fwd_setup_inputs(seed)
    key = _jax.random.fold_in(_jax.random.key(seed), 7919)
    shape, _ = _output_shape()
    out = dict(inp)
    out["loss_target"] = _jax.random.normal(_jax.random.fold_in(key, 0), shape, _jnp.float32)
    for i, name in enumerate(TWIN_WEIGHTS):
        w = inp[name].astype(_jnp.float32)
        if MOMENT_SCALE is None:
            s = _jnp.sqrt(_jnp.mean(_jnp.square(w)) + 1e-30)
        else:
            s = MOMENT_SCALE[name]
        km, kv = _jax.random.split(_jax.random.fold_in(key, i + 1))
        out[name] = w
        out["m_" + name] = s * _jax.random.normal(km, w.shape, _jnp.float32)
        out["v_" + name] = (s * s) * _jax.random.uniform(kv, w.shape, _jnp.float32, 0.5, 1.5)
    if N_MICROBATCH > 1:
        for name, axis in PER_EXAMPLE_BATCH_AXIS.items():
            out[name] = _to_microbatches(out[name], axis)
    return {'x': out['x'], 'meta_tokens': out['meta_tokens'], 'norm_w': out['norm_w'], 'w_in': out['w_in'], 'conv_w': out['conv_w'], 'conv_b': out['conv_b'], 'dt_bias': out['dt_bias'], 'a_log': out['a_log'], 'd_skip': out['d_skip'], 'sb_norm_w': out['sb_norm_w'], 'ssd_norm_w': out['ssd_norm_w'], 'w_out': out['w_out'], 'final_norm_w': out['final_norm_w'], 'loss_target': out['loss_target'], 'm_meta_tokens': out['m_meta_tokens'], 'm_norm_w': out['m_norm_w'], 'm_w_in': out['m_w_in'], 'm_conv_w': out['m_conv_w'], 'm_conv_b': out['m_conv_b'], 'm_dt_bias': out['m_dt_bias'], 'm_a_log': out['m_a_log'], 'm_d_skip': out['m_d_skip'], 'm_sb_norm_w': out['m_sb_norm_w'], 'm_ssd_norm_w': out['m_ssd_norm_w'], 'm_w_out': out['m_w_out'], 'm_final_norm_w': out['m_final_norm_w'], 'v_meta_tokens': out['v_meta_tokens'], 'v_norm_w': out['v_norm_w'], 'v_w_in': out['v_w_in'], 'v_conv_w': out['v_conv_w'], 'v_conv_b': out['v_conv_b'], 'v_dt_bias': out['v_dt_bias'], 'v_a_log': out['v_a_log'], 'v_d_skip': out['v_d_skip'], 'v_sb_norm_w': out['v_sb_norm_w'], 'v_ssd_norm_w': out['v_ssd_norm_w'], 'v_w_out': out['v_w_out'], 'v_final_norm_w': out['v_final_norm_w']}


def _loss(weights, diff, rest, loss_target):
    with _jax.named_scope("forward"):
        args = {**rest, TWIN_DIFF_INPUT: diff, **{k: w.astype(_WEIGHT_DTYPES[k]) for k, w in weights.items()}}
        y = _forward(args)
    with _jax.named_scope("loss_head"):
        err = _jnp.square(y.astype(_jnp.float32) - loss_target)
        return 0.5 * _jnp.sum(_jnp.mean(err, axis=-1)) if err.ndim else 0.5 * err


def _adamw(w, g, m, v):
    m = ADAM_B1 * m + (1.0 - ADAM_B1) * g
    v = ADAM_B2 * v + (1.0 - ADAM_B2) * _jnp.square(g)
    m_hat = m / (1.0 - ADAM_B1 ** ADAM_STEP)
    v_hat = v / (1.0 - ADAM_B2 ** ADAM_STEP)
    delta = -ADAM_LR * (m_hat / (_jnp.sqrt(v_hat) + ADAM_EPS) + ADAM_WD * w)
    return delta, m, v


def reference(x, meta_tokens, norm_w, w_in, conv_w, conv_b, dt_bias, a_log, d_skip, sb_norm_w, ssd_norm_w, w_out, final_norm_w, loss_target, m_meta_tokens, m_norm_w, m_w_in, m_conv_w, m_conv_b, m_dt_bias, m_a_log, m_d_skip, m_sb_norm_w, m_ssd_norm_w, m_w_out, m_final_norm_w, v_meta_tokens, v_norm_w, v_w_in, v_conv_w, v_conv_b, v_dt_bias, v_a_log, v_d_skip, v_sb_norm_w, v_ssd_norm_w, v_w_out, v_final_norm_w):
    given = dict(x=x, meta_tokens=meta_tokens, norm_w=norm_w, w_in=w_in, conv_w=conv_w, conv_b=conv_b, dt_bias=dt_bias, a_log=a_log, d_skip=d_skip, sb_norm_w=sb_norm_w, ssd_norm_w=ssd_norm_w, w_out=w_out, final_norm_w=final_norm_w, loss_target=loss_target, m_meta_tokens=m_meta_tokens, m_norm_w=m_norm_w, m_w_in=m_w_in, m_conv_w=m_conv_w, m_conv_b=m_conv_b, m_dt_bias=m_dt_bias, m_a_log=m_a_log, m_d_skip=m_d_skip, m_sb_norm_w=m_sb_norm_w, m_ssd_norm_w=m_ssd_norm_w, m_w_out=m_w_out, m_final_norm_w=m_final_norm_w, v_meta_tokens=v_meta_tokens, v_norm_w=v_norm_w, v_w_in=v_w_in, v_conv_w=v_conv_w, v_conv_b=v_conv_b, v_dt_bias=v_dt_bias, v_a_log=v_a_log, v_d_skip=v_d_skip, v_sb_norm_w=v_sb_norm_w, v_ssd_norm_w=v_ssd_norm_w, v_w_out=v_w_out, v_final_norm_w=v_final_norm_w)
    weights = {n: given[n] for n in TWIN_WEIGHTS}
    shared = {n: given[n] for n in SHARED_INPUTS}
    per_example = {n: given[n] for n in ['x']}
    grad_fn = _jax.value_and_grad(_loss, argnums=(0, 1))

    def one_microbatch(ex, loss_target):
        ex = dict(ex)
        diff = ex.pop(TWIN_DIFF_INPUT)
        return grad_fn(weights, diff, {**shared, **ex}, loss_target)

    if N_MICROBATCH == 1:
        loss, (grad_w, grad_x) = one_microbatch(per_example, given["loss_target"])
    else:
        def body(carry, xs):
            loss_sum, grad_sum = carry
            l_k, (gw_k, gx_k) = one_microbatch(xs[0], xs[1])
            with _jax.named_scope("update"):
                return (loss_sum + l_k, _jax.tree.map(_jnp.add, grad_sum, gw_k)), gx_k

        init = (_jnp.zeros((), _jnp.float32), _jax.tree.map(_jnp.zeros_like, weights))
        (loss, grad_w), grad_x = _jax.lax.scan(body, init, (per_example, given["loss_target"]))
    with _jax.named_scope("update"):
        delta_w, new_m, new_v = {}, {}, {}
        for n in TWIN_WEIGHTS:
            delta_w[n], new_m[n], new_v[n] = _adamw(weights[n], grad_w[n], given["m_" + n], given["v_" + n])
    return (loss, grad_x, *[grad_w[n] for n in TWIN_WEIGHTS], *[delta_w[n] for n in TWIN_WEIGHTS],
            *[new_m[n] for n in TWIN_WEIGHTS], *[new_v[n] for n in TWIN_WEIGHTS])
```

```python
import functools

import jax
import jax.numpy as jnp
from jax import lax
from jax.experimental import pallas as pl
from jax.experimental.pallas import tpu as pltpu

F32 = jnp.float32
BF16 = jnp.bfloat16

D_MODEL = 1024
SEQ = 2048
N_META = 16
BLK = 128
PAD = BLK - N_META
LP = PAD + N_META + SEQ
NBLK = LP // BLK
N_HEADS = 16
HEAD_DIM = 64
N_GROUPS = 2
HEADS_PER_GROUP = 8
N_STATE = 128
SSD_W = 1024
XBC_W = 1536
D_MAIN = 6656
D_IN = 6672
COL_Q, COL_K, COL_V, COL_G, COL_Z, COL_XBC = 0, 1024, 2048, 3072, 4096, 5120
N_DEV = 8
EPS = 1e-5
SB_SCALE = 0.125

ADAM_LR = 0.001
ADAM_B1 = 0.9
ADAM_B2 = 0.999
ADAM_EPS = 1e-08
ADAM_WD = 0.01
ADAM_STEP = 10

VMEM_LIMIT = 48 * 1024 * 1024

NN = (((1,), (0,)), ((), ()))
NT = (((1,), (1,)), ((), ()))
TN = (((0,), (0,)), ((), ()))


def _dot(a, b, dims=NN):
    return lax.dot_general(a, b, dims, preferred_element_type=F32)


def _split(x, n):
    parts = []
    r = x
    for i in range(n):
        p = r.astype(BF16)
        parts.append(p)
        if i + 1 < n:
            r = r - p.astype(F32)
    return parts


def _dot_x_exact(x, m, dims=NN, n=3):
    out = None
    for p in _split(x, n):
        t = _dot(p, m, dims)
        out = t if out is None else out + t
    return out


def _dot_exact_x(m, x, dims=NN, n=3):
    out = None
    for p in _split(x, n):
        t = _dot(m, p, dims)
        out = t if out is None else out + t
    return out


def _iota(shape, dim):
    return lax.broadcasted_iota(jnp.int32, shape, dim)


def _softplus(x):
    return jnp.maximum(x, 0.0) + jnp.log(1.0 + jnp.exp(-jnp.abs(x)))


def _sigmoid(x):
    return 1.0 / (1.0 + jnp.exp(-x))


def _params(sem=None):
    return pltpu.CompilerParams(dimension_semantics=sem, vmem_limit_bytes=VMEM_LIMIT)


def _exchange(name, srcs, gather):
    n = len(srcs)
    out_shape = []
    for s in srcs:
        shp = ((N_DEV,) + s.shape) if gather else s.shape
        out_shape.append(jax.ShapeDtypeStruct(shp, s.dtype))

    def body(*refs):
        src_refs = refs[:n]
        out_refs = refs[n:2 * n]
        send_sems, recv_sems, loc_sems = refs[2 * n:]
        x = lax.axis_index("x")
        y = lax.axis_index("y")
        c = lax.axis_index("c")
        me = 4 * x + 2 * y + c
        copies = []
        for a in range(n):
            src_all = src_refs[a]
            own = src_all if gather else src_all.at[me]
            loc = pltpu.make_async_copy(own, out_refs[a].at[me], loc_sems.at[a])
            loc.start()
            copies.append(loc)
            for k in (1, 2, 4, 3, 5, 6, 7):
                pc = 1 - c if k & 1 else c
                py = 1 - y if k & 2 else y
                px = 1 - x if k & 4 else x
                peer = 4 * px + 2 * py + pc
                src = src_all if gather else src_all.at[peer]
                cp = pltpu.make_async_remote_copy(
                    src_ref=src, dst_ref=out_refs[a].at[me],
                    send_sem=send_sems.at[a * 7 + k - 1], recv_sem=recv_sems.at[a * 7 + k - 1],
                    device_id=(px, py, pc), device_id_type=pl.DeviceIdType.MESH)
                cp.start()
                copies.append(cp)
        for cp in copies:
            cp.wait()

    any_spec = pl.BlockSpec(memory_space=pl.ANY)
    return pl.pallas_call(
        body, name=name, out_shape=tuple(out_shape),
        in_specs=[any_spec] * n, out_specs=tuple([any_spec] * n),
        scratch_shapes=[pltpu.SemaphoreType.DMA((7 * n,)), pltpu.SemaphoreType.DMA((7 * n,)),
                        pltpu.SemaphoreType.DMA((n,))],
    )(*srcs)


def _matmul(name, a, b, kind, tm, tn, tk, out_dtype=F32):
    if kind == "nn":
        (m, kk), (_, nn_) = a.shape, b.shape
        a_spec = pl.BlockSpec((tm, tk), lambda i, j, k: (i, k))
        b_spec = pl.BlockSpec((tk, tn), lambda i, j, k: (k, j))
        dims = NN
    elif kind == "nt":
        (m, kk), (nn_, _) = a.shape, b.shape
        a_spec = pl.BlockSpec((tm, tk), lambda i, j, k: (i, k))
        b_spec = pl.BlockSpec((tn, tk), lambda i, j, k: (j, k))
        dims = NT
    else:
        (kk, m), (_, nn_) = a.shape, b.shape
        a_spec = pl.BlockSpec((tk, tm), lambda i, j, k: (k, i))
        b_spec = pl.BlockSpec((tk, tn), lambda i, j, k: (k, j))
        dims = TN
    assert m % tm == 0 and nn_ % tn == 0 and kk % tk == 0
    nk = kk // tk

    def body(a_ref, b_ref, o_ref, acc_ref):
        k = pl.program_id(2)
        part = _dot(a_ref[...], b_ref[...], dims)
        if nk == 1:
            o_ref[...] = part.astype(o_ref.dtype)
        else:
            @pl.when(k == 0)
            def _():
                acc_ref[...] = part

            @pl.when(k > 0)
            def _():
                acc_ref[...] += part

            @pl.when(k == nk - 1)
            def _():
                o_ref[...] = acc_ref[...].astype(o_ref.dtype)

    return pl.pallas_call(
        body, name=name, out_shape=jax.ShapeDtypeStruct((m, nn_), out_dtype),
        grid=(m // tm, nn_ // tn, nk),
        in_specs=[a_spec, b_spec], out_specs=pl.BlockSpec((tm, tn), lambda i, j, k: (i, j)),
        scratch_shapes=[pltpu.VMEM((tm, tn) if nk > 1 else (8, 128), F32)],
        compiler_params=_params(("parallel", "parallel", "arbitrary")),
    )(a, b)


def _row_spec(width, col=0):
    return pl.BlockSpec((BLK, width), lambda i: (i, col))


def _const_spec(shape):
    return pl.BlockSpec(shape, lambda i: tuple(0 for _ in shape))


def _prenorm(h_pad, norm_w, wdt, wdt_t):
    def body(h_ref, w_ref, wdt_ref, wdtt_ref, u_ref, dt_ref, dtt_ref):
        xv = h_ref[...]
        r = lax.rsqrt(jnp.mean(xv * xv, axis=-1, keepdims=True) + EPS)
        u = (xv * r * w_ref[...]).astype(BF16)
        u_ref[...] = u
        dt_ref[...] = _dot(u, wdt_ref[...], NN)
        dtt_ref[...] = _dot(wdtt_ref[...], u, NT)

    return pl.pallas_call(
        body, name="prenorm",
        out_shape=(jax.ShapeDtypeStruct((LP, D_MODEL), BF16), jax.ShapeDtypeStruct((LP, N_HEADS), F32),
                   jax.ShapeDtypeStruct((N_HEADS, LP), F32)),
        grid=(NBLK,),
        in_specs=[_row_spec(D_MODEL), _const_spec((1, D_MODEL)), _const_spec((D_MODEL, N_HEADS)),
                  _const_spec((N_HEADS, D_MODEL))],
        out_specs=(_row_spec(D_MODEL), _row_spec(N_HEADS), pl.BlockSpec((N_HEADS, BLK), lambda i: (0, i))),
        compiler_params=_params(("parallel",)),
    )(h_pad, norm_w, wdt, wdt_t)


def _gated_norm(o, g, w):
    a = o * (g * _sigmoid(g))
    r = lax.rsqrt(jnp.mean(a * a, axis=-1, keepdims=True) + EPS)
    return a * r * w


def _ycat(o_sb, proj, o_ssd, sb_w, ssd_w):
    def body(osb_ref, g_ref, ossd_ref, z_ref, sbw_ref, ssdw_ref, y_ref):
        y_ref[:, :SSD_W] = _gated_norm(osb_ref[...], g_ref[...], sbw_ref[...]).astype(BF16)
        y_ref[:, SSD_W:] = _gated_norm(ossd_ref[...], z_ref[...], ssdw_ref[...]).astype(BF16)

    return pl.pallas_call(
        body, name="ycat", out_shape=jax.ShapeDtypeStruct((LP, 2 * SSD_W), BF16), grid=(NBLK,),
        in_specs=[_row_spec(1024), _row_spec(1024, COL_G // 1024), _row_spec(1024),
                  _row_spec(1024, COL_Z // 1024), _const_spec((1, 1024)), _const_spec((1, 1024))],
        out_specs=_row_spec(2 * SSD_W),
        compiler_params=_params(("parallel",)),
    )(o_sb, proj, o_ssd, proj, sb_w, ssd_w)


def _loss_head(h_pad, yo, fnw, target):
    def body(h_ref, yo_ref, w_ref, t_ref, dh2_ref, dh2b_ref, loss_ref, dw_ref):
        i = pl.program_id(0)

        @pl.when(i == 0)
        def _():
            loss_ref[...] = jnp.zeros_like(loss_ref)
            dw_ref[...] = jnp.zeros_like(dw_ref)

        h2 = h_ref[...] + yo_ref[...]
        r = lax.rsqrt(jnp.mean(h2 * h2, axis=-1, keepdims=True) + EPS)
        nrm = h2 * r
        w = w_ref[...]
        live = i > 0
        err = jnp.where(live, nrm * w - t_ref[...], 0.0)
        dout = err * (1.0 / D_MODEL)
        loss_ref[...] += (0.5 / D_MODEL) * _fold_lanes(jnp.sum(err * err, axis=0, keepdims=True))
        dw_ref[...] += jnp.sum(dout * nrm, axis=0, keepdims=True)
        wd = dout * w
        dh2 = r * (wd - nrm * jnp.mean(wd * nrm, axis=-1, keepdims=True))
        dh2_ref[...] = dh2
        dh2b_ref[...] = dh2.astype(BF16)

    return pl.pallas_call(
        body, name="loss_head",
        out_shape=(jax.ShapeDtypeStruct((LP, D_MODEL), F32), jax.ShapeDtypeStruct((LP, D_MODEL), BF16),
                   jax.ShapeDtypeStruct((1, BLK), F32), jax.ShapeDtypeStruct((1, D_MODEL), F32)),
        grid=(NBLK,),
        in_specs=[_row_spec(D_MODEL), _row_spec(D_MODEL), _const_spec((1, D_MODEL)),
                  pl.BlockSpec((BLK, D_MODEL), lambda i: (jnp.maximum(i - 1, 0), 0))],
        out_specs=(_row_spec(D_MODEL), _row_spec(D_MODEL), _const_spec((1, BLK)), _const_spec((1, D_MODEL))),
        compiler_params=_params(("arbitrary",)),
    )(h_pad, yo, fnw, target)


def _fold_lanes(row):
    out = row[:, :BLK]
    for j in range(1, row.shape[1] // BLK):
        out = out + row[:, j * BLK:(j + 1) * BLK]
    return out


def _gated_norm_bwd(dy, o, g, w):
    s = _sigmoid(g)
    sg = g * s
    a = o * sg
    r = lax.rsqrt(jnp.mean(a * a, axis=-1, keepdims=True) + EPS)
    nrm = a * r
    dw = jnp.sum(dy * nrm, axis=0, keepdims=True)
    wd = dy * w
    da = r * (wd - nrm * jnp.mean(wd * nrm, axis=-1, keepdims=True))
    return da * sg, da * o * (s * (1.0 + g * (1.0 - s))), dw


def _ycat_bwd(dycat, o_sb, proj, o_ssd, sb_w, ssd_w):
    def body(dy_ref, osb_ref, g_ref, ossd_ref, z_ref, sbw_ref, ssdw_ref,
             dosb_ref, dg_ref, dossd_ref, dz_ref, dsbw_ref, dssdw_ref):
        @pl.when(pl.program_id(0) == 0)
        def _():
            dsbw_ref[...] = jnp.zeros_like(dsbw_ref)
            dssdw_ref[...] = jnp.zeros_like(dssdw_ref)

        do, dg, dw = _gated_norm_bwd(dy_ref[:, :SSD_W], osb_ref[...], g_ref[...], sbw_ref[...])
        dosb_ref[...] = do
        dg_ref[...] = dg
        dsbw_ref[...] += dw
        do, dg, dw = _gated_norm_bwd(dy_ref[:, SSD_W:], ossd_ref[...], z_ref[...], ssdw_ref[...])
        dossd_ref[...] = do
        dz_ref[...] = dg
        dssdw_ref[...] += dw

    act = jax.ShapeDtypeStruct((LP, 1024), F32)
    vec = jax.ShapeDtypeStruct((1, 1024), F32)
    return pl.pallas_call(
        body, name="ycat_bwd", out_shape=(act, act, act, act, vec, vec), grid=(NBLK,),
        in_specs=[_row_spec(2048), _row_spec(1024), _row_spec(1024, COL_G // 1024), _row_spec(1024),
                  _row_spec(1024, COL_Z // 1024), _const_spec((1, 1024)), _const_spec((1, 1024))],
        out_specs=(_row_spec(1024), _row_spec(1024), _row_spec(1024), _row_spec(1024),
                   _const_spec((1, 1024)), _const_spec((1, 1024))),
        compiler_params=_params(("arbitrary",)),
    )(dycat, o_sb, proj, o_ssd, proj, sb_w, ssd_w)


def _prenorm_bwd(du, ddt, wdt, h_pad, norm_w, dh2):
    def body(du_ref, ddt_ref, wdt_ref, h_ref, w_ref, dh2_ref, dh_ref, dw_ref, dwdt_ref):
        @pl.when(pl.program_id(0) == 0)
        def _():
            dw_ref[...] = jnp.zeros_like(dw_ref)
            dwdt_ref[...] = jnp.zeros_like(dwdt_ref)

        ddt_b = ddt_ref[...].astype(BF16)
        dut = du_ref[...] + _dot(ddt_b, wdt_ref[...], NT)
        xv = h_ref[...]
        r = lax.rsqrt(jnp.mean(xv * xv, axis=-1, keepdims=True) + EPS)
        nrm = xv * r
        w = w_ref[...]
        dw_ref[...] += jnp.sum(dut * nrm, axis=0, keepdims=True)
        wd = dut * w
        dh_ref[...] = dh2_ref[...] + r * (wd - nrm * jnp.mean(wd * nrm, axis=-1, keepdims=True))
        dwdt_ref[...] += _dot((nrm * w).astype(BF16), ddt_b, TN)

    return pl.pallas_call(
        body, name="prenorm_bwd",
        out_shape=(jax.ShapeDtypeStruct((LP, D_MODEL), F32), jax.ShapeDtypeStruct((1, D_MODEL), F32),
                   jax.ShapeDtypeStruct((D_MODEL, N_HEADS), F32)),
        grid=(NBLK,),
        in_specs=[_row_spec(D_MODEL), _row_spec(N_HEADS), _const_spec((D_MODEL, N_HEADS)), _row_spec(D_MODEL),
                  _const_spec((1, D_MODEL)), _row_spec(D_MODEL)],
        out_specs=(_row_spec(D_MODEL), _const_spec((1, D_MODEL)), _const_spec((D_MODEL, N_HEADS))),
        compiler_params=_params(("arbitrary",)),
    )(du, ddt, wdt, h_pad, norm_w, dh2)


def _suffix_sum(vals, tri):
    return _dot_x_exact(vals, tri, NN, n=3)


def _sb_tile(z, valid):
    t = jnp.exp(-jnp.abs(z))
    inv = 1.0 / (1.0 + t)
    sp = jnp.maximum(z, 0.0) + jnp.log(1.0 + t)
    sig = jnp.where(z >= 0, inv, t * inv)
    return sig, jnp.where(valid, -sp, 0.0), z - sp


def _sb_attention_fwd(proj):
    def body(q_ref, k_ref, v_ref, o_ref, ox_ref, kb_ref, vb_ref):
        qi = pl.program_id(1)

        @pl.when(qi == 0)
        def _():
            kb_ref[...] = k_ref[...].astype(BF16)
            vb_ref[...] = v_ref[...].astype(BF16)

        lane = _iota((BLK, BLK), 1)
        q = q_ref[...] * SB_SCALE
        qh = (jnp.where(lane < HEAD_DIM, q, 0.0).astype(BF16), jnp.where(lane >= HEAD_DIM, q, 0.0).astype(BF16))
        row = qi * BLK + _iota((BLK, BLK), 0)
        tri = (_iota((BLK, BLK), 0) > lane).astype(BF16)

        def step(j, carry):
            kblk = qi - j
            off = pl.multiple_of(kblk * BLK, BLK)
            ks = kb_ref[pl.ds(off, BLK), :]
            vs = vb_ref[pl.ds(off, BLK), :]
            col = kblk * BLK + lane
            valid = jnp.logical_and(col < row, col >= PAD)
            new = []
            for h in range(2):
                run, acc, rest = carry[3 * h], carry[3 * h + 1], carry[3 * h + 2]
                z = _dot(qh[h], ks, NT)
                _, lk, lb = _sb_tile(z, valid)
                after = run + _suffix_sum(lk, tri)
                a = jnp.where(valid, jnp.exp(lb + after), 0.0)
                a_hi = a.astype(BF16)
                acc = acc + _dot(a_hi, vs, NN)
                rest = rest + _dot((a - a_hi.astype(F32)).astype(BF16), vs, NN)
                run = run + jnp.sum(lk, axis=1, keepdims=True)
                new += [run, acc, rest]
            return tuple(new)

        zero_r = jnp.zeros((BLK, 1), F32)
        zero_a = jnp.zeros((BLK, BLK), F32)
        res = lax.fori_loop(0, qi + 1, step, (zero_r, zero_a, zero_a, zero_r, zero_a, zero_a))
        o = jnp.where(lane < HEAD_DIM, res[1], res[4])
        o_ref[...] = o
        ox_ref[...] = o + jnp.where(lane < HEAD_DIM, res[2], res[5])

    act = jax.ShapeDtypeStruct((LP, 1024), F32)
    return pl.pallas_call(
        body, name="sb_attn_fwd", out_shape=(act, act),
        grid=(N_HEADS // 2, NBLK),
        in_specs=[pl.BlockSpec((BLK, BLK), lambda hp, qi: (qi, COL_Q // BLK + hp)),
                  pl.BlockSpec((LP, BLK), lambda hp, qi: (0, COL_K // BLK + hp)),
                  pl.BlockSpec((LP, BLK), lambda hp, qi: (0, COL_V // BLK + hp))],
        out_specs=(pl.BlockSpec((BLK, BLK), lambda hp, qi: (qi, hp)),
                   pl.BlockSpec((BLK, BLK), lambda hp, qi: (qi, hp))),
        scratch_shapes=[pltpu.VMEM((LP, BLK), BF16), pltpu.VMEM((LP, BLK), BF16)],
        compiler_params=_params(("arbitrary", "arbitrary")),
    )(proj, proj, proj)


def _sb_attention_bwd(proj, o_sb, do_sb):
    def body(q_ref, k_ref, v_ref, o_ref, do_ref, dq_ref, dk_ref, dv_ref, kb_ref, vb_ref):
        qi = pl.program_id(1)

        @pl.when(qi == 0)
        def _():
            kb_ref[...] = k_ref[...].astype(BF16)
            vb_ref[...] = v_ref[...].astype(BF16)
            dk_ref[...] = jnp.zeros_like(dk_ref)
            dv_ref[...] = jnp.zeros_like(dv_ref)

        lane = _iota((BLK, BLK), 1)
        head0 = lane < HEAD_DIM
        q = q_ref[...] * SB_SCALE
        do = do_ref[...]
        qh = (jnp.where(head0, q, 0.0).astype(BF16), jnp.where(head0, 0.0, q).astype(BF16))
        doh = (jnp.where(head0, do, 0.0).astype(BF16), jnp.where(head0, 0.0, do).astype(BF16))
        prod = do.astype(BF16).astype(F32) * o_ref[...]
        dsum = (jnp.sum(jnp.where(head0, prod, 0.0), axis=1, keepdims=True),
                jnp.sum(jnp.where(head0, 0.0, prod), axis=1, keepdims=True))
        row = qi * BLK + _iota((BLK, BLK), 0)
        tri = (_iota((BLK, BLK), 0) > lane).astype(BF16)

        def step(j, carry):
            kblk = qi - j
            off = pl.multiple_of(kblk * BLK, BLK)
            ks = kb_ref[pl.ds(off, BLK), :]
            vs = vb_ref[pl.ds(off, BLK), :]
            col = kblk * BLK + lane
            valid = jnp.logical_and(col < row, col >= PAD)
            new = []
            dk_blk = jnp.zeros((BLK, BLK), F32)
            dv_blk = jnp.zeros((BLK, BLK), F32)
            for h in range(2):
                run, erun, dq = carry[3 * h], carry[3 * h + 1], carry[3 * h + 2]
                z = _dot(qh[h], ks, NT)
                sig, lk, lb = _sb_tile(z, valid)
                after = run + _suffix_sum(lk, tri)
                a = jnp.where(valid, jnp.exp(lb + after), 0.0)
                da = _dot(doh[h], vs, NT)
                e = a * da
                csum = dsum[h] - (erun + _suffix_sum(e, tri))
                dz = jnp.where(valid, e - sig * csum, 0.0).astype(BF16)
                dq = dq + _dot(dz, ks, NN)
                dk_blk = dk_blk + _dot(dz, qh[h], TN)
                dv_blk = dv_blk + _dot(a.astype(BF16), doh[h], TN)
                run = run + jnp.sum(lk, axis=1, keepdims=True)
                erun = erun + jnp.sum(e, axis=1, keepdims=True)
                new += [run, erun, dq]
            dk_ref[pl.ds(off, BLK), :] += dk_blk
            dv_ref[pl.ds(off, BLK), :] += dv_blk
            return tuple(new)

        zero_r = jnp.zeros((BLK, 1), F32)
        zero_a = jnp.zeros((BLK, BLK), F32)
        res = lax.fori_loop(0, qi + 1, step, (zero_r, zero_r, zero_a, zero_r, zero_r, zero_a))
        dq_ref[...] = jnp.where(head0, res[2], res[5]) * SB_SCALE

    act = jax.ShapeDtypeStruct((LP, 1024), F32)
    blk = lambda col: pl.BlockSpec((BLK, BLK), lambda hp, qi: (qi, col + hp))
    whole = lambda col: pl.BlockSpec((LP, BLK), lambda hp, qi: (0, col + hp))
    return pl.pallas_call(
        body, name="sb_attn_bwd", out_shape=(act, act, act), grid=(N_HEADS // 2, NBLK),
        in_specs=[blk(COL_Q // BLK), whole(COL_K // BLK), whole(COL_V // BLK), blk(0), blk(0)],
        out_specs=(blk(0), whole(0), whole(0)),
        scratch_shapes=[pltpu.VMEM((LP, BLK), BF16), pltpu.VMEM((LP, BLK), BF16)],
        compiler_params=_params(("arbitrary", "arbitrary")),
    )(proj, proj, proj, o_sb, do_sb)


def _conv_pre(x, w_ref, b_ref):
    acc = b_ref[...] + w_ref[3:4, :] * x
    for k in range(3):
        acc = acc + w_ref[k:k + 1, :] * pltpu.roll(x, 3 - k, 0)
    return acc


def _conv_fwd(proj, conv_w, conv_b):
    def body(x_ref, w_ref, b_ref, o_ref):
        xc = _conv_pre(x_ref[...], w_ref, b_ref)
        o_ref[...] = xc * _sigmoid(xc)

    nb = XBC_W // BLK
    return pl.pallas_call(
        body, name="conv_fwd", out_shape=jax.ShapeDtypeStruct((LP, XBC_W), F32), grid=(nb,),
        in_specs=[pl.BlockSpec((LP, BLK), lambda j: (0, COL_XBC // BLK + j)),
                  pl.BlockSpec((4, BLK), lambda j: (0, j)), pl.BlockSpec((1, BLK), lambda j: (0, j))],
        out_specs=pl.BlockSpec((LP, BLK), lambda j: (0, j)),
        compiler_params=_params(("parallel",)),
    )(proj, conv_w, conv_b)


def _conv_bwd(dxa, proj, conv_w, conv_b):
    def body(d_ref, x_ref, w_ref, b_ref, dx_ref, dw_ref, db_ref):
        x = x_ref[...]
        xc = _conv_pre(x, w_ref, b_ref)
        s = _sigmoid(xc)
        live = _iota((LP, BLK), 0) >= PAD
        dxc = jnp.where(live, d_ref[...] * (s * (1.0 + xc * (1.0 - s))), 0.0)
        db_ref[...] = jnp.sum(dxc, axis=0, keepdims=True)
        dx = w_ref[3:4, :] * dxc
        dw_ref[3:4, :] = jnp.sum(dxc * x, axis=0, keepdims=True)
        for k in range(3):
            dw_ref[k:k + 1, :] = jnp.sum(dxc * pltpu.roll(x, 3 - k, 0), axis=0, keepdims=True)
            dx = dx + w_ref[k:k + 1, :] * pltpu.roll(dxc, LP - (3 - k), 0)
        dx_ref[...] = dx

    nb = XBC_W // BLK
    col = lambda j: (0, j)
    return pl.pallas_call(
        body, name="conv_bwd",
        out_shape=(jax.ShapeDtypeStruct((LP, XBC_W), F32), jax.ShapeDtypeStruct((4, XBC_W), F32),
                   jax.ShapeDtypeStruct((1, XBC_W), F32)),
        grid=(nb,),
        in_specs=[pl.BlockSpec((LP, BLK), col), pl.BlockSpec((LP, BLK), lambda j: (0, COL_XBC // BLK + j)),
                  pl.BlockSpec((4, BLK), col), pl.BlockSpec((1, BLK), col)],
        out_specs=(pl.BlockSpec((LP, BLK), col), pl.BlockSpec((4, BLK), col), pl.BlockSpec((1, BLK), col)),
        compiler_params=_params(("parallel",)),
    )(dxa, proj, conv_w, conv_b)


def _ssd_prelude(c, dt_ref, dtt_ref, dtb_ref, dtbt_ref, alog_ref, alogt_ref):
    live = jnp.logical_or(c > 0, _iota((BLK, N_HEADS), 0) >= PAD)
    live_t = jnp.logical_or(c > 0, _iota((N_HEADS, BLK), 1) >= PAD)
    pre = dt_ref[...] + dtb_ref[...]
    pre_t = dtt_ref[...] + dtbt_ref[...]
    dt = jnp.where(live, _softplus(pre), 0.0)
    dt_t = jnp.where(live_t, _softplus(pre_t), 0.0)
    a = -jnp.exp(alog_ref[...])
    a_t = -jnp.exp(alogt_ref[...])
    li = _iota((BLK, BLK), 0)
    si = _iota((BLK, BLK), 1)
    lower = (si <= li).astype(BF16)
    upper = (li <= si).astype(BF16)
    acum = _dot_exact_x(lower, dt * a, NN)
    acum_t = _dot_x_exact(dt_t * a_t, upper, NN)
    return live, pre, dt, a, a_t, acum, acum_t


def _head_expand():
    return (_iota((N_HEADS, SSD_W), 1) // HEAD_DIM == _iota((N_HEADS, SSD_W), 0)).astype(BF16)


def _head_reduce_mat():
    return (_iota((SSD_W, N_HEADS), 0) // HEAD_DIM == _iota((SSD_W, N_HEADS), 1)).astype(BF16)


def _decay_mat(acum, acum_t, h, causal):
    seg = jnp.minimum(acum[:, h:h + 1] - acum_t[h:h + 1, :], 0.0)
    return jnp.where(causal, jnp.exp(seg), 0.0)


def _ssd_specs():
    chunk = lambda width, col=0: pl.BlockSpec((BLK, width), lambda c: (c, col))
    return chunk


def _ssd_fwd(xa, dt_raw, dt_raw_t, dt_bias, dt_bias_t, a_log, a_log_t, d_exp):
    def body(x_ref, b_ref, c_ref, dt_ref, dtt_ref, dtb_ref, dtbt_ref, alog_ref, alogt_ref, dexp_ref,
             y_ref, hs_ref, state_ref):
        c = pl.program_id(0)

        @pl.when(c == 0)
        def _():
            state_ref[...] = jnp.zeros_like(state_ref)

        _, _, dt, _, _, acum, acum_t = _ssd_prelude(c, dt_ref, dtt_ref, dtb_ref, dtbt_ref, alog_ref, alogt_ref)
        expand = _head_expand()
        x = x_ref[...]
        xdt = x * _dot_x_exact(dt, expand)
        exp_a = _dot_x_exact(jnp.exp(acum), expand)
        to_end = _dot_x_exact(jnp.exp(acum[BLK - 1:BLK, :] - acum), expand)
        xdt_b = xdt.astype(BF16)
        xd_b = (xdt * to_end).astype(BF16)
        chunk_decay = jnp.exp(acum_t[:, BLK - 1:BLK])
        hs_ref[0] = state_ref[...]
        lane = _iota((BLK, BLK), 1)
        causal = _iota((BLK, BLK), 0) >= lane
        gw = HEADS_PER_GROUP * HEAD_DIM
        for g in range(N_GROUPS):
            bg = b_ref[:, g * N_STATE:(g + 1) * N_STATE].astype(BF16)
            cg = c_ref[:, g * N_STATE:(g + 1) * N_STATE].astype(BF16)
            cb = _dot(cg, bg, NT)
            hg = state_ref[g * gw:(g + 1) * gw, :]
            ch = _dot(cg, hg.astype(BF16), NT)
            st = _dot(xd_b[:, g * gw:(g + 1) * gw], bg, TN)
            for p in range(HEADS_PER_GROUP // 2):
                h0 = g * HEADS_PER_GROUP + 2 * p
                lo = h0 * HEAD_DIM
                xp = xdt_b[:, lo:lo + BLK]
                w0 = (cb * _decay_mat(acum, acum_t, h0, causal)).astype(BF16)
                w1 = (cb * _decay_mat(acum, acum_t, h0 + 1, causal)).astype(BF16)
                yd = jnp.where(lane < HEAD_DIM, _dot(w0, xp), _dot(w1, xp))
                y_ref[:, lo:lo + BLK] = (yd + ch[:, lo - g * gw:lo - g * gw + BLK] * exp_a[:, lo:lo + BLK]
                                         + x[:, lo:lo + BLK] * dexp_ref[:, lo:lo + BLK])
            for r in range(HEADS_PER_GROUP):
                h = g * HEADS_PER_GROUP + r
                state_ref[h * HEAD_DIM:(h + 1) * HEAD_DIM, :] = (
                    hg[r * HEAD_DIM:(r + 1) * HEAD_DIM, :] * chunk_decay[h:h + 1, :]
                    + st[r * HEAD_DIM:(r + 1) * HEAD_DIM, :])

    chunk = _ssd_specs()
    return pl.pallas_call(
        body, name="ssd_fwd",
        out_shape=(jax.ShapeDtypeStruct((LP, SSD_W), F32), jax.ShapeDtypeStruct((NBLK, SSD_W, N_STATE), F32)),
        grid=(NBLK,),
        in_specs=[chunk(SSD_W), chunk(256, 4), chunk(256, 5), chunk(N_HEADS),
                  pl.BlockSpec((N_HEADS, BLK), lambda c: (0, c)), _const_spec((1, N_HEADS)),
                  _const_spec((N_HEADS, 1)), _const_spec((1, N_HEADS)), _const_spec((N_HEADS, 1)),
                  _const_spec((1, SSD_W))],
        out_specs=(chunk(SSD_W), pl.BlockSpec((1, SSD_W, N_STATE), lambda c: (c, 0, 0))),
        scratch_shapes=[pltpu.VMEM((SSD_W, N_STATE), F32)],
        compiler_params=_params(("arbitrary",)),
    )(xa, xa, xa, dt_raw, dt_raw_t, dt_bias, dt_bias_t, a_log, a_log_t, d_exp)


def _ssd_bwd(xa, dt_raw, dt_raw_t, dt_bias, dt_bias_t, a_log, a_log_t, d_exp, hstart, dy):
    def body(x_ref, b_ref, c_ref, dt_ref, dtt_ref, dtb_ref, dtbt_ref, alog_ref, alogt_ref, dexp_ref,
             hs_ref, dy_ref, dx_ref, db_ref, dc_ref, ddt_ref, dbias_ref, dalog_ref, dd_ref, dstate_ref):
        step = pl.program_id(0)
        c = NBLK - 1 - step

        @pl.when(step == 0)
        def _():
            dstate_ref[...] = jnp.zeros_like(dstate_ref)
            dbias_ref[...] = jnp.zeros_like(dbias_ref)
            dalog_ref[...] = jnp.zeros_like(dalog_ref)
            dd_ref[...] = jnp.zeros_like(dd_ref)

        live, pre, dt, a, a_t, acum, acum_t = _ssd_prelude(c, dt_ref, dtt_ref, dtb_ref, dtbt_ref,
                                                           alog_ref, alogt_ref)
        expand = _head_expand()
        reduce_m = _head_reduce_mat()
        x = x_ref[...]
        dyv = dy_ref[...]
        dt_e = _dot_x_exact(dt, expand)
        xdt = x * dt_e
        exp_acum = jnp.exp(acum)
        exp_a = _dot_x_exact(exp_acum, expand)
        dte = jnp.exp(acum[BLK - 1:BLK, :] - acum)
        to_end = _dot_x_exact(dte, expand)
        xdt_b = xdt.astype(BF16)
        xd_b = (xdt * to_end).astype(BF16)
        chunk_decay = jnp.exp(acum_t[:, BLK - 1:BLK])
        lane = _iota((BLK, BLK), 1)
        head0 = lane < HEAD_DIM
        causal = _iota((BLK, BLK), 0) >= lane
        gw = HEADS_PER_GROUP * HEAD_DIM
        dm = dyv * exp_a
        dm_b = dm.astype(BF16)
        onehot = lambda h: (_iota((1, N_HEADS), 1) == h).astype(F32)
        onehot_t = lambda h: (_iota((N_HEADS, 1), 0) == h).astype(F32)
        dacum = jnp.zeros((BLK, N_HEADS), F32)
        dacum_t = jnp.zeros((N_HEADS, BLK), F32)
        ddt_acc = jnp.zeros((BLK, N_HEADS), F32)
        ddte_acc = jnp.zeros((BLK, N_HEADS), F32)
        dexpa_acc = jnp.zeros((BLK, N_HEADS), F32)
        dskip_acc = jnp.zeros((BLK, N_HEADS), F32)
        head_sum = expand
        for g in range(N_GROUPS):
            bg = b_ref[:, g * N_STATE:(g + 1) * N_STATE].astype(BF16)
            cg = c_ref[:, g * N_STATE:(g + 1) * N_STATE].astype(BF16)
            cb = _dot(cg, bg, NT)
            hg = hs_ref[0, g * gw:(g + 1) * gw, :]
            hg_b = hg.astype(BF16)
            dhe = dstate_ref[g * gw:(g + 1) * gw, :]
            dhe_b = dhe.astype(BF16)
            ch = _dot(cg, hg_b, NT)
            dcg = _dot(dm_b[:, g * gw:(g + 1) * gw], hg_b, NN)
            dhs = _dot(dm_b[:, g * gw:(g + 1) * gw], cg, TN)
            dxd = _dot(bg, dhe_b, NT)
            dbg = _dot(xd_b[:, g * gw:(g + 1) * gw], dhe_b, NN)
            dcb = jnp.zeros((BLK, BLK), F32)
            for p in range(HEADS_PER_GROUP // 2):
                h0 = g * HEADS_PER_GROUP + 2 * p
                lo = h0 * HEAD_DIM
                xp = xdt_b[:, lo:lo + BLK]
                dyp = dyv[:, lo:lo + BLK]
                dyh = (jnp.where(head0, dyp, 0.0).astype(BF16), jnp.where(head0, 0.0, dyp).astype(BF16))
                dxdt_p = jnp.zeros((BLK, BLK), F32)
                for q in range(2):
                    h = h0 + q
                    dec = _decay_mat(acum, acum_t, h, causal)
                    w = cb * dec
                    dw = _dot(dyh[q], xp, NT)
                    t = dw * w
                    dacum = dacum + jnp.sum(t, axis=1, keepdims=True) * onehot(h)
                    dacum_t = dacum_t - jnp.sum(t, axis=0, keepdims=True) * onehot_t(h)
                    dcb = dcb + dw * dec
                    dxdt_p = dxdt_p + _dot(w.astype(BF16), dyh[q], TN)
                sl = slice(lo, lo + BLK)
                gl = slice(lo - g * gw, lo - g * gw + BLK)
                dxdt_p = dxdt_p + dxd[:, gl] * to_end[:, sl]
                dx_ref[:, sl] = dyp * dexp_ref[:, sl] + dxdt_p * dt_e[:, sl]
                red = reduce_m[lo:lo + BLK, :]
                ddt_acc = ddt_acc + _dot_x_exact(dxdt_p * x[:, sl], red)
                ddte_acc = ddte_acc + _dot_x_exact(dxd[:, gl] * xdt[:, sl], red)
                dexpa_acc = dexpa_acc + _dot_x_exact(dyp * ch[:, gl], red)
                dskip_acc = dskip_acc + _dot_x_exact(dyp * x[:, sl], red)
            dcb_b = dcb.astype(BF16)
            dc_ref[:, g * N_STATE:(g + 1) * N_STATE] = dcg + _dot(dcb_b, bg, NN)
            db_ref[:, g * N_STATE:(g + 1) * N_STATE] = dbg + _dot(dcb_b, cg, TN)
            prod = dhe * hg
            per_head = jnp.sum(_dot_exact_x(head_sum[:, g * gw:(g + 1) * gw], prod, NN), axis=1, keepdims=True)
            dacum_t = dacum_t + (per_head * chunk_decay) * (_iota((1, BLK), 1) == BLK - 1).astype(F32)
            for r in range(HEADS_PER_GROUP):
                h = g * HEADS_PER_GROUP + r
                rows = slice(h * HEAD_DIM, (h + 1) * HEAD_DIM)
                dstate_ref[rows, :] = (dhs[r * HEAD_DIM:(r + 1) * HEAD_DIM, :]
                                       + dhe[r * HEAD_DIM:(r + 1) * HEAD_DIM, :] * chunk_decay[h:h + 1, :])
        dacum = dacum + dexpa_acc * exp_acum - ddte_acc * dte
        last_row = (_iota((BLK, 1), 0) == BLK - 1).astype(F32)
        dacum = dacum + last_row * jnp.sum(ddte_acc * dte, axis=0, keepdims=True)
        li = _iota((BLK, BLK), 0)
        si = _iota((BLK, BLK), 1)
        upper = (li <= si).astype(BF16)
        lower = (si <= li).astype(BF16)
        dda = _dot_exact_x(upper, dacum, NN)
        dda_t = _dot_x_exact(dacum_t, lower, NN)
        eye = (_iota((N_HEADS, N_HEADS), 0) == _iota((N_HEADS, N_HEADS), 1)).astype(BF16)
        dda = dda + _dot_x_exact_tn(dda_t, eye)
        ddt = ddt_acc + dda * a
        dalog_ref[...] += jnp.sum(dda * dt, axis=0, keepdims=True) * a
        dd_ref[...] += jnp.sum(dskip_acc, axis=0, keepdims=True)
        ddt_raw = jnp.where(live, ddt * _sigmoid(pre), 0.0)
        ddt_ref[...] = ddt_raw
        dbias_ref[...] += jnp.sum(ddt_raw, axis=0, keepdims=True)

    rev = lambda width, col=0: pl.BlockSpec((BLK, width), lambda s: (NBLK - 1 - s, col))
    vec = jax.ShapeDtypeStruct((1, N_HEADS), F32)
    return pl.pallas_call(
        body, name="ssd_bwd",
        out_shape=(jax.ShapeDtypeStruct((LP, SSD_W), F32), jax.ShapeDtypeStruct((LP, 256), F32),
                   jax.ShapeDtypeStruct((LP, 256), F32), jax.ShapeDtypeStruct((LP, N_HEADS), F32), vec, vec, vec),
        grid=(NBLK,),
        in_specs=[rev(SSD_W), rev(256, 4), rev(256, 5), rev(N_HEADS),
                  pl.BlockSpec((N_HEADS, BLK), lambda s: (0, NBLK - 1 - s)), _const_spec((1, N_HEADS)),
                  _const_spec((N_HEADS, 1)), _const_spec((1, N_HEADS)), _const_spec((N_HEADS, 1)),
                  _const_spec((1, SSD_W)),
                  pl.BlockSpec((1, SSD_W, N_STATE), lambda s: (NBLK - 1 - s, 0, 0)), rev(SSD_W)],
        out_specs=(rev(SSD_W), rev(256), rev(256), rev(N_HEADS), _const_spec((1, N_HEADS)),
                   _const_spec((1, N_HEADS)), _const_spec((1, N_HEADS))),
        scratch_shapes=[pltpu.VMEM((SSD_W, N_STATE), F32)],
        compiler_params=_params(("arbitrary",)),
    )(xa, xa, xa, dt_raw, dt_raw_t, dt_bias, dt_bias_t, a_log, a_log_t, d_exp, hstart, dy)


def _dot_x_exact_tn(x_t, eye):
    out = None
    for p in _split(x_t, 3):
        t = _dot(p, eye, TN)
        out = t if out is None else out + t
    return out


def _adamw(name, parts, w, m, v, rows):
    r_all, cols = w.shape
    assert r_all % rows == 0
    c1 = 1.0 / (1.0 - ADAM_B1 ** ADAM_STEP)
    c2 = 1.0 / (1.0 - ADAM_B2 ** ADAM_STEP)

    def body(p_ref, w_ref, m_ref, v_ref, g_ref, d_ref, mo_ref, vo_ref):
        g = p_ref[0]
        for j in range(1, N_DEV):
            g = g + p_ref[j]
        mn = ADAM_B1 * m_ref[...] + (1.0 - ADAM_B1) * g
        vn = ADAM_B2 * v_ref[...] + (1.0 - ADAM_B2) * (g * g)
        g_ref[...] = g
        mo_ref[...] = mn
        vo_ref[...] = vn
        d_ref[...] = -ADAM_LR * ((mn * c1) / (jnp.sqrt(vn * c2) + ADAM_EPS) + ADAM_WD * w_ref[...])

    spec = pl.BlockSpec((rows, cols), lambda i: (i, 0))
    shp = jax.ShapeDtypeStruct((r_all, cols), F32)
    return pl.pallas_call(
        body, name=name, out_shape=(shp, shp, shp, shp), grid=(r_all // rows,),
        in_specs=[pl.BlockSpec((N_DEV, rows, cols), lambda i: (0, i, 0)), spec, spec, spec],
        out_specs=(spec, spec, spec, spec),
        compiler_params=_params(("parallel",)),
    )(parts, w, m, v)


_REP_ROWS = (("norm_w", 8), ("conv_b", 12), ("dt_bias", 1), ("a_log", 1), ("d_skip", 1),
             ("sb_norm_w", 8), ("ssd_norm_w", 8), ("final_norm_w", 8))
_LOSS_ROW = 47
_CONVW_ROW = 48
_META_ROW = 54
_PACK_ROWS = 72


def _rows128(v, rows):
    flat = v.reshape(-1)
    return jnp.pad(flat, (0, rows * BLK - flat.shape[0])).reshape(rows, BLK)


def _pack_small(rep, conv_w_shard, meta_shard, loss_row=None):
    parts = [_rows128(rep[name], rows) for name, rows in _REP_ROWS]
    parts.append(jnp.zeros((1, BLK), F32) if loss_row is None else loss_row)
    parts.append(conv_w_shard.reshape(6, BLK))
    parts.append(meta_shard)
    parts.append(jnp.zeros((_PACK_ROWS - _META_ROW - N_META, BLK), F32))
    return jnp.concatenate(parts, axis=0)


def _unpack_small(pack, shapes):
    out = {}
    r = 0
    for name, rows in _REP_ROWS:
        n = 1
        for s in shapes[name]:
            n *= s
        out[name] = pack[r:r + rows].reshape(-1)[:n].reshape(shapes[name])
        r += rows
    out["conv_w"] = pack[_CONVW_ROW:_CONVW_ROW + 6].reshape(1, 4, 192)
    out["meta_tokens"] = pack[_META_ROW:_META_ROW + N_META]
    return out


def _forward_backward(h_pad, target, norm_w, w_main, w_dt, w_out_full, conv_w_full, conv_b, dt_bias, a_log, d_skip,
                      sb_norm_w, ssd_norm_w, final_norm_w):
    w_dt_t = w_dt.T
    dt_bias_t = dt_bias.reshape(N_HEADS, 1)
    a_log_t = a_log.reshape(N_HEADS, 1)
    d_exp = jnp.repeat(d_skip, HEAD_DIM, axis=1)

    u, dt_raw, dt_raw_t = _prenorm(h_pad, norm_w, w_dt, w_dt_t)
    proj = _matmul("in_proj", u, w_main, "nn", 1088, 512, D_MODEL)
    o_sb, o_sb_exact = _sb_attention_fwd(proj)
    xa = _conv_fwd(proj, conv_w_full, conv_b)
    o_ssd, hstart = _ssd_fwd(xa, dt_raw, dt_raw_t, dt_bias, dt_bias_t, a_log, a_log_t, d_exp)
    ycat = _ycat(o_sb, proj, o_ssd, sb_norm_w, ssd_norm_w)
    yo = _matmul("out_proj", ycat, w_out_full, "nn", 1088, 512, 2 * SSD_W)
    dh2, dh2_b, loss_row, d_fnw = _loss_head(h_pad, yo, final_norm_w.reshape(1, D_MODEL), target)

    g_w_out = _matmul("d_w_out", ycat, dh2_b, "tn", 512, 512, LP)
    dycat = _matmul("d_ycat", dh2_b, w_out_full, "nt", 1088, 512, D_MODEL)
    do_sb, dg, do_ssd, dz, d_sbw, d_ssdw = _ycat_bwd(dycat, o_sb, proj, o_ssd, sb_norm_w, ssd_norm_w)
    dq, dk, dv = _sb_attention_bwd(proj, o_sb_exact, do_sb)
    dxs, dbm, dcm, ddt_raw, d_dtb, d_alog, d_dskip = _ssd_bwd(
        xa, dt_raw, dt_raw_t, dt_bias, dt_bias_t, a_log, a_log_t, d_exp, hstart, do_ssd)
    dxa = jnp.concatenate([dxs, dbm, dcm], axis=1)
    dxbc, d_convw, d_convb = _conv_bwd(dxa, proj, conv_w_full, conv_b)
    dproj = jnp.concatenate([t.astype(BF16) for t in (dq, dk, dv, dg, dz, dxbc)], axis=1)
    g_w_main = _matmul("d_w_in", u, dproj, "tn", 512, 512, LP)
    du = _matmul("d_u", dproj, w_main, "nt", 1088, 512, 1664)
    dh, d_nw, g_w_dt = _prenorm_bwd(du, ddt_raw, w_dt, h_pad, norm_w, dh2)
    return dict(loss_row=loss_row, dh=dh, w_main=g_w_main, w_dt=g_w_dt, w_out=g_w_out, conv_w=d_convw,
                norm_w=d_nw, conv_b=d_convb, dt_bias=d_dtb, a_log=d_alog, d_skip=d_dskip,
                sb_norm_w=d_sbw, ssd_norm_w=d_ssdw, final_norm_w=d_fnw)


def kernel(x, meta_tokens, norm_w, w_in, conv_w, conv_b, dt_bias, a_log, d_skip, sb_norm_w, ssd_norm_w, w_out, final_norm_w, loss_target, m_meta_tokens, m_norm_w, m_w_in, m_conv_w, m_conv_b, m_dt_bias, m_a_log, m_d_skip, m_sb_norm_w, m_ssd_norm_w, m_w_out, m_final_norm_w, v_meta_tokens, v_norm_w, v_w_in, v_conv_w, v_conv_b, v_dt_bias, v_a_log, v_d_skip, v_sb_norm_w, v_ssd_norm_w, v_w_out, v_final_norm_w):
    small_src = jnp.concatenate([conv_w[0].reshape(6, BLK), meta_tokens, jnp.zeros((2, BLK), F32)], axis=0)
    w_in_g, w_out_g, small_g = _exchange(
        "gather_weights", [w_in[0].astype(BF16), w_out[0].astype(BF16), small_src], gather=True)
    w_in_full = w_in_g.transpose(1, 0, 2).reshape(D_MODEL, D_IN)
    w_out_full = w_out_g.reshape(2 * SSD_W, D_MODEL)
    conv_w_full = small_g[:, :6].reshape(N_DEV, 4, 192).transpose(1, 0, 2).reshape(4, XBC_W)
    meta_full = small_g[:, 6:6 + N_META].transpose(1, 0, 2).reshape(N_META, D_MODEL)
    h_pad = jnp.concatenate([jnp.zeros((PAD, D_MODEL), F32), meta_full, x[0]], axis=0)

    grads = _forward_backward(h_pad, loss_target[0], norm_w, w_in_full[:, :D_MAIN], w_in_full[:, D_MAIN:],
                              w_out_full, conv_w_full, conv_b, dt_bias, a_log, d_skip, sb_norm_w, ssd_norm_w,
                              final_norm_w)
    dh = grads["dh"]

    g_w_in = jnp.concatenate([grads["w_main"], grads["w_dt"]], axis=1)
    g_w_in = g_w_in.reshape(D_MODEL, N_DEV, D_IN // N_DEV).transpose(1, 0, 2)
    g_w_out = grads["w_out"]
    d_convw = grads["conv_w"]
    d_meta = dh[PAD:PAD + N_META]
    packs = [_pack_small(grads, d_convw[:, 192 * j:192 * (j + 1)], d_meta[:, BLK * j:BLK * (j + 1)],
                         grads["loss_row"]) for j in range(N_DEV)]
    p_w_in, p_w_out, p_small = _exchange(
        "exchange_grads", [g_w_in, g_w_out.reshape(N_DEV, 256, D_MODEL), jnp.stack(packs)], gather=False)

    rep_w = {"norm_w": norm_w, "conv_b": conv_b, "dt_bias": dt_bias, "a_log": a_log, "d_skip": d_skip,
             "sb_norm_w": sb_norm_w, "ssd_norm_w": ssd_norm_w, "final_norm_w": final_norm_w}
    rep_m = {"norm_w": m_norm_w, "conv_b": m_conv_b, "dt_bias": m_dt_bias, "a_log": m_a_log, "d_skip": m_d_skip,
             "sb_norm_w": m_sb_norm_w, "ssd_norm_w": m_ssd_norm_w, "final_norm_w": m_final_norm_w}
    rep_v = {"norm_w": v_norm_w, "conv_b": v_conv_b, "dt_bias": v_dt_bias, "a_log": v_a_log, "d_skip": v_d_skip,
             "sb_norm_w": v_sb_norm_w, "ssd_norm_w": v_ssd_norm_w, "final_norm_w": v_final_norm_w}
    res_in = _adamw("adamw_w_in", p_w_in, w_in[0], m_w_in[0], v_w_in[0], 128)
    res_out = _adamw("adamw_w_out", p_w_out, w_out[0], m_w_out[0], v_w_out[0], 128)
    res_small = _adamw("adamw_small", p_small, _pack_small(rep_w, conv_w[0], meta_tokens),
                       _pack_small(rep_m, m_conv_w[0], m_meta_tokens),
                       _pack_small(rep_v, v_conv_w[0], v_meta_tokens), _PACK_ROWS)

    shapes = {name: rep_w[name].shape for name, _ in _REP_ROWS}
    loss = jnp.sum(res_small[0][_LOSS_ROW])
    order = ["meta_tokens", "norm_w", "w_in", "conv_w", "conv_b", "dt_bias", "a_log", "d_skip",
             "sb_norm_w", "ssd_norm_w", "w_out", "final_norm_w"]
    outs = [loss, dh[BLK:].reshape(1, SEQ, D_MODEL)]
    for kind in range(4):
        small = _unpack_small(res_small[kind], shapes)
        small["w_in"] = res_in[kind].reshape(1, D_MODEL, D_IN // N_DEV)
        small["w_out"] = res_out[kind].reshape(1, 256, D_MODEL)
        outs += [small[name] for name in order]
    return tuple(outs)
```

```python
import functools

import jax
import jax.numpy as jnp
from jax import lax
from jax.experimental import pallas as pl
from jax.experimental.pallas import tpu as pltpu

F32 = jnp.float32
BF16 = jnp.bfloat16

D_MODEL = 1024
SEQ = 2048
N_META = 16
BLK = 128
PAD = BLK - N_META
LP = PAD + N_META + SEQ
NBLK = LP // BLK
N_HEADS = 16
HEAD_DIM = 64
N_GROUPS = 2
HEADS_PER_GROUP = 8
N_STATE = 128
SSD_W = 1024
XBC_W = 1536
D_MAIN = 6656
D_IN = 6672
COL_Q, COL_K, COL_V, COL_G, COL_Z, COL_XBC = 0, 1024, 2048, 3072, 4096, 5120
N_DEV = 8
EPS = 1e-5
SB_SCALE = 0.125
SB_DEAD = -87.4

ADAM_LR = 0.001
ADAM_B1 = 0.9
ADAM_B2 = 0.999
ADAM_EPS = 1e-08
ADAM_WD = 0.01
ADAM_STEP = 10

VMEM_LIMIT = 48 * 1024 * 1024

NN = (((1,), (0,)), ((), ()))
NT = (((1,), (1,)), ((), ()))
TN = (((0,), (0,)), ((), ()))


def _dot(a, b, dims=NN):
    return lax.dot_general(a, b, dims, preferred_element_type=F32)


def _split(x, n):
    parts = []
    r = x
    for i in range(n):
        p = r.astype(BF16)
        parts.append(p)
        if i + 1 < n:
            r = r - p.astype(F32)
    return parts


def _dot_x_exact(x, m, dims=NN, n=3):
    out = None
    for p in _split(x, n):
        t = _dot(p, m, dims)
        out = t if out is None else out + t
    return out


def _dot_exact_x(m, x, dims=NN, n=3):
    out = None
    for p in _split(x, n):
        t = _dot(m, p, dims)
        out = t if out is None else out + t
    return out


def _iota(shape, dim):
    return lax.broadcasted_iota(jnp.int32, shape, dim)


def _softplus(x):
    return jnp.maximum(x, 0.0) + jnp.log(1.0 + jnp.exp(-jnp.abs(x)))


def _sigmoid(x):
    return 1.0 / (1.0 + jnp.exp(-x))


def _params(sem=None):
    return pltpu.CompilerParams(dimension_semantics=sem, vmem_limit_bytes=VMEM_LIMIT)


_ANY = pl.BlockSpec(memory_space=pl.ANY)
_MESH = pl.DeviceIdType.MESH


def _mesh_position():
    return lax.axis_index("x"), lax.axis_index("y"), lax.axis_index("c")


def _other_chips(x, y):
    return [(1 - x, y), (x, 1 - y), (1 - x, 1 - y)]


def _gather_weights(srcs):
    n = len(srcs)

    def body(*refs):
        src, out = refs[:n], refs[n:2 * n]
        send_sems, recv_sems, loc_sems = refs[2 * n:]
        x, y, c = _mesh_position()
        sibling = (x, y, 1 - c)
        chips = _other_chips(x, y)

        def copy(a, k, block, to, from_src=False):
            slot = out[a].at[4 * block[0] + 2 * block[1] + block[2]]
            return pltpu.make_async_remote_copy(
                src_ref=src[a] if from_src else slot, dst_ref=slot,
                send_sem=send_sems.at[7 * a + k], recv_sem=recv_sems.at[7 * a + k],
                device_id=to, device_id_type=_MESH)

        local, sends = [], []
        for a in range(n):
            mine = pltpu.make_async_copy(src[a], out[a].at[4 * x + 2 * y + c], loc_sems.at[a])
            mine.start()
            local.append(mine)
            first = [copy(a, 0, (x, y, c), sibling, True)]
            first += [copy(a, 1 + j, (x, y, c), (*chip, c), True) for j, chip in enumerate(chips)]
            for cp in first:
                cp.start()
            sends += first
        for a in range(n):
            for j, chip in enumerate(chips):
                copy(a, 1 + j, (*chip, c), (x, y, c)).wait_recv()
                passed = copy(a, 4 + j, (*chip, c), sibling)
                passed.start()
                sends.append(passed)
        for a in range(n):
            copy(a, 0, (x, y, 1 - c), (x, y, c)).wait_recv()
            for j, chip in enumerate(chips):
                copy(a, 4 + j, (*chip, 1 - c), (x, y, c)).wait_recv()
        for cp in sends:
            cp.wait_send()
        for cp in local:
            cp.wait()

    return pl.pallas_call(
        body, name="gather_weights",
        out_shape=tuple(jax.ShapeDtypeStruct((N_DEV,) + s.shape, s.dtype) for s in srcs),
        in_specs=[_ANY] * n, out_specs=tuple([_ANY] * n),
        scratch_shapes=[pltpu.SemaphoreType.DMA((7 * n,)), pltpu.SemaphoreType.DMA((7 * n,)),
                        pltpu.SemaphoreType.DMA((n,))],
    )(*srcs)


def _swap_with_sibling(srcs):
    n = len(srcs)

    def body(*refs):
        src, out = refs[:n], refs[n:2 * n]
        send_sems, recv_sems = refs[2 * n:]
        x, y, c = _mesh_position()
        copies = []
        for a in range(n):
            cp = pltpu.make_async_remote_copy(
                src_ref=src[a].at[1 - c], dst_ref=out[a], send_sem=send_sems.at[a], recv_sem=recv_sems.at[a],
                device_id=(x, y, 1 - c), device_id_type=_MESH)
            cp.start()
            copies.append(cp)
        for cp in copies:
            cp.wait()

    return pl.pallas_call(
        body, name="swap_with_sibling",
        out_shape=tuple(jax.ShapeDtypeStruct(s.shape[1:], s.dtype) for s in srcs),
        in_specs=[_ANY] * n, out_specs=tuple([_ANY] * n),
        scratch_shapes=[pltpu.SemaphoreType.DMA((n,)), pltpu.SemaphoreType.DMA((n,))],
    )(*srcs)


def _pair_sum(name, g, sib, rows):
    _, _, r_all, cols = g.shape
    assert r_all % rows == 0

    def body(g0_ref, g1_ref, s_ref, o_ref):
        c = lax.axis_index("c")
        mine = jnp.where(c == 0, g0_ref[0, 0].astype(F32), g1_ref[0, 0].astype(F32))
        o_ref[0] = (mine + s_ref[0].astype(F32)).astype(o_ref.dtype)

    return pl.pallas_call(
        body, name=name, out_shape=jax.ShapeDtypeStruct((4, r_all, cols), g.dtype), grid=(4, r_all // rows),
        in_specs=[pl.BlockSpec((1, 1, rows, cols), lambda k, i: (0, k, i, 0)),
                  pl.BlockSpec((1, 1, rows, cols), lambda k, i: (1, k, i, 0)),
                  pl.BlockSpec((1, rows, cols), lambda k, i: (k, i, 0))],
        out_specs=pl.BlockSpec((1, rows, cols), lambda k, i: (k, i, 0)),
        compiler_params=_params(("parallel", "parallel")),
    )(g, g, sib)


def _exchange_between_chips(srcs):
    n = len(srcs)

    def body(*refs):
        src, out = refs[:n], refs[n:2 * n]
        send_sems, recv_sems, loc_sems = refs[2 * n:]
        x, y, c = _mesh_position()
        here = 2 * x + y
        copies = []
        for a in range(n):
            loc = pltpu.make_async_copy(src[a].at[here], out[a].at[here], loc_sems.at[a])
            loc.start()
            copies.append(loc)
            for j, chip in enumerate(_other_chips(x, y)):
                cp = pltpu.make_async_remote_copy(
                    src_ref=src[a].at[2 * chip[0] + chip[1]], dst_ref=out[a].at[here],
                    send_sem=send_sems.at[3 * a + j], recv_sem=recv_sems.at[3 * a + j],
                    device_id=(*chip, c), device_id_type=_MESH)
                cp.start()
                copies.append(cp)
        for cp in copies:
            cp.wait()

    return pl.pallas_call(
        body, name="exchange_between_chips",
        out_shape=tuple(jax.ShapeDtypeStruct(s.shape, s.dtype) for s in srcs),
        in_specs=[_ANY] * n, out_specs=tuple([_ANY] * n),
        scratch_shapes=[pltpu.SemaphoreType.DMA((3 * n,)), pltpu.SemaphoreType.DMA((3 * n,)),
                        pltpu.SemaphoreType.DMA((n,))],
    )(*srcs)


def _matmul(name, a, b, kind, tm, tn, tk, out_dtype=F32):
    if kind == "nn":
        (m, kk), (_, nn_) = a.shape, b.shape
        a_spec = pl.BlockSpec((tm, tk), lambda i, j, k: (i, k))
        b_spec = pl.BlockSpec((tk, tn), lambda i, j, k: (k, j))
        dims = NN
    elif kind == "nt":
        (m, kk), (nn_, _) = a.shape, b.shape
        a_spec = pl.BlockSpec((tm, tk), lambda i, j, k: (i, k))
        b_spec = pl.BlockSpec((tn, tk), lambda i, j, k: (j, k))
        dims = NT
    else:
        (kk, m), (_, nn_) = a.shape, b.shape
        a_spec = pl.BlockSpec((tk, tm), lambda i, j, k: (k, i))
        b_spec = pl.BlockSpec((tk, tn), lambda i, j, k: (k, j))
        dims = TN
    assert m % tm == 0 and nn_ % tn == 0 and kk % tk == 0
    nk = kk // tk

    def body(a_ref, b_ref, o_ref, acc_ref):
        k = pl.program_id(2)
        part = _dot(a_ref[...], b_ref[...], dims)
        if nk == 1:
            o_ref[...] = part.astype(o_ref.dtype)
        else:
            @pl.when(k == 0)
            def _():
                acc_ref[...] = part

            @pl.when(k > 0)
            def _():
                acc_ref[...] += part

            @pl.when(k == nk - 1)
            def _():
                o_ref[...] = acc_ref[...].astype(o_ref.dtype)

    return pl.pallas_call(
        body, name=name, out_shape=jax.ShapeDtypeStruct((m, nn_), out_dtype),
        grid=(m // tm, nn_ // tn, nk),
        in_specs=[a_spec, b_spec], out_specs=pl.BlockSpec((tm, tn), lambda i, j, k: (i, j)),
        scratch_shapes=[pltpu.VMEM((tm, tn) if nk > 1 else (8, 128), F32)],
        compiler_params=_params(("parallel", "parallel", "arbitrary")),
    )(a, b)


def _row_spec(width, col=0):
    return pl.BlockSpec((BLK, width), lambda i: (i, col))


def _const_spec(shape):
    return pl.BlockSpec(shape, lambda i: tuple(0 for _ in shape))


def _prenorm(h_pad, norm_w, wdt, wdt_t):
    def body(h_ref, w_ref, wdt_ref, wdtt_ref, u_ref, dt_ref, dtt_ref):
        xv = h_ref[...]
        r = lax.rsqrt(jnp.mean(xv * xv, axis=-1, keepdims=True) + EPS)
        u = (xv * r * w_ref[...]).astype(BF16)
        u_ref[...] = u
        dt_ref[...] = _dot(u, wdt_ref[...], NN)
        dtt_ref[...] = _dot(wdtt_ref[...], u, NT)

    return pl.pallas_call(
        body, name="prenorm",
        out_shape=(jax.ShapeDtypeStruct((LP, D_MODEL), BF16), jax.ShapeDtypeStruct((LP, N_HEADS), F32),
                   jax.ShapeDtypeStruct((N_HEADS, LP), F32)),
        grid=(NBLK,),
        in_specs=[_row_spec(D_MODEL), _const_spec((1, D_MODEL)), _const_spec((D_MODEL, N_HEADS)),
                  _const_spec((N_HEADS, D_MODEL))],
        out_specs=(_row_spec(D_MODEL), _row_spec(N_HEADS), pl.BlockSpec((N_HEADS, BLK), lambda i: (0, i))),
        compiler_params=_params(("parallel",)),
    )(h_pad, norm_w, wdt, wdt_t)


def _gated_norm(o, g, w):
    a = o * (g * _sigmoid(g))
    r = lax.rsqrt(jnp.mean(a * a, axis=-1, keepdims=True) + EPS)
    return a * r * w


def _ycat(o_sb, proj, o_ssd, sb_w, ssd_w):
    def body(osb_ref, g_ref, ossd_ref, z_ref, sbw_ref, ssdw_ref, y_ref):
        y_ref[:, :SSD_W] = _gated_norm(osb_ref[...], g_ref[...], sbw_ref[...]).astype(BF16)
        y_ref[:, SSD_W:] = _gated_norm(ossd_ref[...], z_ref[...], ssdw_ref[...]).astype(BF16)

    return pl.pallas_call(
        body, name="ycat", out_shape=jax.ShapeDtypeStruct((LP, 2 * SSD_W), BF16), grid=(NBLK,),
        in_specs=[_row_spec(1024), _row_spec(1024, COL_G // 1024), _row_spec(1024),
                  _row_spec(1024, COL_Z // 1024), _const_spec((1, 1024)), _const_spec((1, 1024))],
        out_specs=_row_spec(2 * SSD_W),
        compiler_params=_params(("parallel",)),
    )(o_sb, proj, o_ssd, proj, sb_w, ssd_w)


def _loss_head(h_pad, yo, fnw, target):
    def body(h_ref, yo_ref, w_ref, t_ref, dh2_ref, dh2b_ref, loss_ref, dw_ref):
        i = pl.program_id(0)

        @pl.when(i == 0)
        def _():
            loss_ref[...] = jnp.zeros_like(loss_ref)
            dw_ref[...] = jnp.zeros_like(dw_ref)

        h2 = h_ref[...] + yo_ref[...]
        r = lax.rsqrt(jnp.mean(h2 * h2, axis=-1, keepdims=True) + EPS)
        nrm = h2 * r
        w = w_ref[...]
        live = i > 0
        err = jnp.where(live, nrm * w - t_ref[...], 0.0)
        dout = err * (1.0 / D_MODEL)
        loss_ref[...] += (0.5 / D_MODEL) * _fold_lanes(jnp.sum(err * err, axis=0, keepdims=True))
        dw_ref[...] += jnp.sum(dout * nrm, axis=0, keepdims=True)
        wd = dout * w
        dh2 = r * (wd - nrm * jnp.mean(wd * nrm, axis=-1, keepdims=True))
        dh2_ref[...] = dh2
        dh2b_ref[...] = dh2.astype(BF16)

    return pl.pallas_call(
        body, name="loss_head",
        out_shape=(jax.ShapeDtypeStruct((LP, D_MODEL), F32), jax.ShapeDtypeStruct((LP, D_MODEL), BF16),
                   jax.ShapeDtypeStruct((1, BLK), F32), jax.ShapeDtypeStruct((1, D_MODEL), F32)),
        grid=(NBLK,),
        in_specs=[_row_spec(D_MODEL), _row_spec(D_MODEL), _const_spec((1, D_MODEL)),
                  pl.BlockSpec((BLK, D_MODEL), lambda i: (jnp.maximum(i - 1, 0), 0))],
        out_specs=(_row_spec(D_MODEL), _row_spec(D_MODEL), _const_spec((1, BLK)), _const_spec((1, D_MODEL))),
        compiler_params=_params(("arbitrary",)),
    )(h_pad, yo, fnw, target)


def _fold_lanes(row):
    out = row[:, :BLK]
    for j in range(1, row.shape[1] // BLK):
        out = out + row[:, j * BLK:(j + 1) * BLK]
    return out


def _gated_norm_bwd(dy, o, g, w):
    s = _sigmoid(g)
    sg = g * s
    a = o * sg
    r = lax.rsqrt(jnp.mean(a * a, axis=-1, keepdims=True) + EPS)
    nrm = a * r
    dw = jnp.sum(dy * nrm, axis=0, keepdims=True)
    wd = dy * w
    da = r * (wd - nrm * jnp.mean(wd * nrm, axis=-1, keepdims=True))
    return da * sg, da * o * (s * (1.0 + g * (1.0 - s))), dw


def _ycat_bwd(dycat, o_sb, proj, o_ssd, sb_w, ssd_w):
    def body(dy_ref, osb_ref, g_ref, ossd_ref, z_ref, sbw_ref, ssdw_ref,
             dosb_ref, dg_ref, dossd_ref, dz_ref, dsbw_ref, dssdw_ref):
        @pl.when(pl.program_id(0) == 0)
        def _():
            dsbw_ref[...] = jnp.zeros_like(dsbw_ref)
            dssdw_ref[...] = jnp.zeros_like(dssdw_ref)

        do, dg, dw = _gated_norm_bwd(dy_ref[:, :SSD_W], osb_ref[...], g_ref[...], sbw_ref[...])
        dosb_ref[...] = do
        dg_ref[...] = dg
        dsbw_ref[...] += dw
        do, dg, dw = _gated_norm_bwd(dy_ref[:, SSD_W:], ossd_ref[...], z_ref[...], ssdw_ref[...])
        dossd_ref[...] = do
        dz_ref[...] = dg
        dssdw_ref[...] += dw

    act = jax.ShapeDtypeStruct((LP, 1024), F32)
    vec = jax.ShapeDtypeStruct((1, 1024), F32)
    return pl.pallas_call(
        body, name="ycat_bwd", out_shape=(act, act, act, act, vec, vec), grid=(NBLK,),
        in_specs=[_row_spec(2048), _row_spec(1024), _row_spec(1024, COL_G // 1024), _row_spec(1024),
                  _row_spec(1024, COL_Z // 1024), _const_spec((1, 1024)), _const_spec((1, 1024))],
        out_specs=(_row_spec(1024), _row_spec(1024), _row_spec(1024), _row_spec(1024),
                   _const_spec((1, 1024)), _const_spec((1, 1024))),
        compiler_params=_params(("arbitrary",)),
    )(dycat, o_sb, proj, o_ssd, proj, sb_w, ssd_w)


def _prenorm_bwd(du, ddt, wdt, h_pad, norm_w, dh2):
    def body(du_ref, ddt_ref, wdt_ref, h_ref, w_ref, dh2_ref, dh_ref, dw_ref, dwdt_ref):
        @pl.when(pl.program_id(0) == 0)
        def _():
            dw_ref[...] = jnp.zeros_like(dw_ref)
            dwdt_ref[...] = jnp.zeros_like(dwdt_ref)

        ddt_b = ddt_ref[...].astype(BF16)
        dut = du_ref[...] + _dot(ddt_b, wdt_ref[...], NT)
        xv = h_ref[...]
        r = lax.rsqrt(jnp.mean(xv * xv, axis=-1, keepdims=True) + EPS)
        nrm = xv * r
        w = w_ref[...]
        dw_ref[...] += jnp.sum(dut * nrm, axis=0, keepdims=True)
        wd = dut * w
        dh_ref[...] = dh2_ref[...] + r * (wd - nrm * jnp.mean(wd * nrm, axis=-1, keepdims=True))
        dwdt_ref[...] += _dot((nrm * w).astype(BF16), ddt_b, TN)

    return pl.pallas_call(
        body, name="prenorm_bwd",
        out_shape=(jax.ShapeDtypeStruct((LP, D_MODEL), F32), jax.ShapeDtypeStruct((1, D_MODEL), F32),
                   jax.ShapeDtypeStruct((D_MODEL, N_HEADS), F32)),
        grid=(NBLK,),
        in_specs=[_row_spec(D_MODEL), _row_spec(N_HEADS), _const_spec((D_MODEL, N_HEADS)), _row_spec(D_MODEL),
                  _const_spec((1, D_MODEL)), _row_spec(D_MODEL)],
        out_specs=(_row_spec(D_MODEL), _const_spec((1, D_MODEL)), _const_spec((D_MODEL, N_HEADS))),
        compiler_params=_params(("arbitrary",)),
    )(du, ddt, wdt, h_pad, norm_w, dh2)


def _suffix_sum(vals, tri):
    return _dot_x_exact(vals, tri, NN, n=3)


def _sb_tile(z, valid):
    t = jnp.exp(-jnp.abs(z))
    inv = 1.0 / (1.0 + t)
    sp = jnp.maximum(z, 0.0) + jnp.log(1.0 + t)
    sig = jnp.where(z >= 0, inv, t * inv)
    return sig, jnp.where(valid, -sp, 0.0), z - sp


def _sweep(qi, step, init, run_slots):
    def cond(carry):
        return jnp.logical_and(carry[0] <= qi, carry[1] > 0)

    def body(carry):
        state = step(carry[0], tuple(carry[2:]))
        top = state[run_slots[0]]
        for s in run_slots[1:]:
            top = jnp.maximum(top, state[s])
        alive = (jnp.max(top) > SB_DEAD).astype(jnp.int32)
        return (carry[0] + 1, alive) + tuple(state)

    return lax.while_loop(cond, body, (jnp.int32(0), jnp.int32(1)) + tuple(init))[2:]


def _sb_attention_fwd(proj):
    def body(q_ref, k_ref, v_ref, o_ref, ox_ref, kb_ref, vb_ref):
        qi = pl.program_id(1)

        @pl.when(qi == 0)
        def _():
            kb_ref[...] = k_ref[...].astype(BF16)
            vb_ref[...] = v_ref[...].astype(BF16)

        lane = _iota((BLK, BLK), 1)
        q = q_ref[...] * SB_SCALE
        qh = (jnp.where(lane < HEAD_DIM, q, 0.0).astype(BF16), jnp.where(lane >= HEAD_DIM, q, 0.0).astype(BF16))
        row = qi * BLK + _iota((BLK, BLK), 0)
        tri = (_iota((BLK, BLK), 0) > lane).astype(BF16)

        def step(j, carry):
            kblk = qi - j
            off = pl.multiple_of(kblk * BLK, BLK)
            ks = kb_ref[pl.ds(off, BLK), :]
            vs = vb_ref[pl.ds(off, BLK), :]
            col = kblk * BLK + lane
            valid = jnp.logical_and(col < row, col >= PAD)
            new = []
            for h in range(2):
                run, acc, rest = carry[3 * h], carry[3 * h + 1], carry[3 * h + 2]
                z = _dot(qh[h], ks, NT)
                _, lk, lb = _sb_tile(z, valid)
                after = run + _suffix_sum(lk, tri)
                a = jnp.where(valid, jnp.exp(lb + after), 0.0)
                a_hi = a.astype(BF16)
                acc = acc + _dot(a_hi, vs, NN)
                rest = rest + _dot((a - a_hi.astype(F32)).astype(BF16), vs, NN)
                run = run + jnp.sum(lk, axis=1, keepdims=True)
                new += [run, acc, rest]
            return tuple(new)

        zero_r = jnp.zeros((BLK, 1), F32)
        zero_a = jnp.zeros((BLK, BLK), F32)
        res = _sweep(qi, step, (zero_r, zero_a, zero_a, zero_r, zero_a, zero_a), (0, 3))
        o = jnp.where(lane < HEAD_DIM, res[1], res[4])
        o_ref[...] = o
        ox_ref[...] = o + jnp.where(lane < HEAD_DIM, res[2], res[5])

    act = jax.ShapeDtypeStruct((LP, 1024), F32)
    return pl.pallas_call(
        body, name="sb_attn_fwd", out_shape=(act, act),
        grid=(N_HEADS // 2, NBLK),
        in_specs=[pl.BlockSpec((BLK, BLK), lambda hp, qi: (qi, COL_Q // BLK + hp)),
                  pl.BlockSpec((LP, BLK), lambda hp, qi: (0, COL_K // BLK + hp)),
                  pl.BlockSpec((LP, BLK), lambda hp, qi: (0, COL_V // BLK + hp))],
        out_specs=(pl.BlockSpec((BLK, BLK), lambda hp, qi: (qi, hp)),
                   pl.BlockSpec((BLK, BLK), lambda hp, qi: (qi, hp))),
        scratch_shapes=[pltpu.VMEM((LP, BLK), BF16), pltpu.VMEM((LP, BLK), BF16)],
        compiler_params=_params(("arbitrary", "arbitrary")),
    )(proj, proj, proj)


def _sb_attention_bwd(proj, o_sb, do_sb):
    def body(q_ref, k_ref, v_ref, o_ref, do_ref, dq_ref, dk_ref, dv_ref, kb_ref, vb_ref):
        qi = pl.program_id(1)

        @pl.when(qi == 0)
        def _():
            kb_ref[...] = k_ref[...].astype(BF16)
            vb_ref[...] = v_ref[...].astype(BF16)
            dk_ref[...] = jnp.zeros_like(dk_ref)
            dv_ref[...] = jnp.zeros_like(dv_ref)

        lane = _iota((BLK, BLK), 1)
        head0 = lane < HEAD_DIM
        q = q_ref[...] * SB_SCALE
        do = do_ref[...]
        qh = (jnp.where(head0, q, 0.0).astype(BF16), jnp.where(head0, 0.0, q).astype(BF16))
        doh = (jnp.where(head0, do, 0.0).astype(BF16), jnp.where(head0, 0.0, do).astype(BF16))
        prod = do.astype(BF16).astype(F32) * o_ref[...]
        dsum = (jnp.sum(jnp.where(head0, prod, 0.0), axis=1, keepdims=True),
                jnp.sum(jnp.where(head0, 0.0, prod), axis=1, keepdims=True))
        row = qi * BLK + _iota((BLK, BLK), 0)
        tri = (_iota((BLK, BLK), 0) > lane).astype(BF16)

        def step(j, carry):
            kblk = qi - j
            off = pl.multiple_of(kblk * BLK, BLK)
            ks = kb_ref[pl.ds(off, BLK), :]
            vs = vb_ref[pl.ds(off, BLK), :]
            col = kblk * BLK + lane
            valid = jnp.logical_and(col < row, col >= PAD)
            new = []
            dk_blk = jnp.zeros((BLK, BLK), F32)
            dv_blk = jnp.zeros((BLK, BLK), F32)
            for h in range(2):
                run, erun, dq = carry[3 * h], carry[3 * h + 1], carry[3 * h + 2]
                z = _dot(qh[h], ks, NT)
                sig, lk, lb = _sb_tile(z, valid)
                after = run + _suffix_sum(lk, tri)
                a = jnp.where(valid, jnp.exp(lb + after), 0.0)
                da = _dot(doh[h], vs, NT)
                e = a * da
                csum = dsum[h] - (erun + _suffix_sum(e, tri))
                dz = jnp.where(valid, e - sig * csum, 0.0).astype(BF16)
                dq = dq + _dot(dz, ks, NN)
                dk_blk = dk_blk + _dot(dz, qh[h], TN)
                dv_blk = dv_blk + _dot(a.astype(BF16), doh[h], TN)
                run = run + jnp.sum(lk, axis=1, keepdims=True)
                erun = erun + jnp.sum(e, axis=1, keepdims=True)
                new += [run, erun, dq]
            dk_ref[pl.ds(off, BLK), :] += dk_blk
            dv_ref[pl.ds(off, BLK), :] += dv_blk
            return tuple(new)

        zero_r = jnp.zeros((BLK, 1), F32)
        zero_a = jnp.zeros((BLK, BLK), F32)
        res = _sweep(qi, step, (zero_r, zero_r, zero_a, zero_r, zero_r, zero_a), (0, 3))
        dq_ref[...] = jnp.where(head0, res[2], res[5]) * SB_SCALE

    act = jax.ShapeDtypeStruct((LP, 1024), F32)
    blk = lambda col: pl.BlockSpec((BLK, BLK), lambda hp, qi: (qi, col + hp))
    whole = lambda col: pl.BlockSpec((LP, BLK), lambda hp, qi: (0, col + hp))
    return pl.pallas_call(
        body, name="sb_attn_bwd", out_shape=(act, act, act), grid=(N_HEADS // 2, NBLK),
        in_specs=[blk(COL_Q // BLK), whole(COL_K // BLK), whole(COL_V // BLK), blk(0), blk(0)],
        out_specs=(blk(0), whole(0), whole(0)),
        scratch_shapes=[pltpu.VMEM((LP, BLK), BF16), pltpu.VMEM((LP, BLK), BF16)],
        compiler_params=_params(("arbitrary", "arbitrary")),
    )(proj, proj, proj, o_sb, do_sb)


def _conv_pre(x, w_ref, b_ref):
    acc = b_ref[...] + w_ref[3:4, :] * x
    for k in range(3):
        acc = acc + w_ref[k:k + 1, :] * pltpu.roll(x, 3 - k, 0)
    return acc


def _conv_fwd(proj, conv_w, conv_b):
    def body(x_ref, w_ref, b_ref, o_ref):
        xc = _conv_pre(x_ref[...], w_ref, b_ref)
        o_ref[...] = xc * _sigmoid(xc)

    nb = XBC_W // BLK
    return pl.pallas_call(
        body, name="conv_fwd", out_shape=jax.ShapeDtypeStruct((LP, XBC_W), F32), grid=(nb,),
        in_specs=[pl.BlockSpec((LP, BLK), lambda j: (0, COL_XBC // BLK + j)),
                  pl.BlockSpec((4, BLK), lambda j: (0, j)), pl.BlockSpec((1, BLK), lambda j: (0, j))],
        out_specs=pl.BlockSpec((LP, BLK), lambda j: (0, j)),
        compiler_params=_params(("parallel",)),
    )(proj, conv_w, conv_b)


def _conv_bwd(dxa, proj, conv_w, conv_b):
    def body(d_ref, x_ref, w_ref, b_ref, dx_ref, dw_ref, db_ref):
        x = x_ref[...]
        xc = _conv_pre(x, w_ref, b_ref)
        s = _sigmoid(xc)
        live = _iota((LP, BLK), 0) >= PAD
        dxc = jnp.where(live, d_ref[...] * (s * (1.0 + xc * (1.0 - s))), 0.0)
        db_ref[...] = jnp.sum(dxc, axis=0, keepdims=True)
        dx = w_ref[3:4, :] * dxc
        dw_ref[3:4, :] = jnp.sum(dxc * x, axis=0, keepdims=True)
        for k in range(3):
            dw_ref[k:k + 1, :] = jnp.sum(dxc * pltpu.roll(x, 3 - k, 0), axis=0, keepdims=True)
            dx = dx + w_ref[k:k + 1, :] * pltpu.roll(dxc, LP - (3 - k), 0)
        dx_ref[...] = dx

    nb = XBC_W // BLK
    col = lambda j: (0, j)
    return pl.pallas_call(
        body, name="conv_bwd",
        out_shape=(jax.ShapeDtypeStruct((LP, XBC_W), F32), jax.ShapeDtypeStruct((4, XBC_W), F32),
                   jax.ShapeDtypeStruct((1, XBC_W), F32)),
        grid=(nb,),
        in_specs=[pl.BlockSpec((LP, BLK), col), pl.BlockSpec((LP, BLK), lambda j: (0, COL_XBC // BLK + j)),
                  pl.BlockSpec((4, BLK), col), pl.BlockSpec((1, BLK), col)],
        out_specs=(pl.BlockSpec((LP, BLK), col), pl.BlockSpec((4, BLK), col), pl.BlockSpec((1, BLK), col)),
        compiler_params=_params(("parallel",)),
    )(dxa, proj, conv_w, conv_b)


def _ssd_prelude(c, dt_ref, dtt_ref, dtb_ref, dtbt_ref, alog_ref, alogt_ref):
    live = jnp.logical_or(c > 0, _iota((BLK, N_HEADS), 0) >= PAD)
    live_t = jnp.logical_or(c > 0, _iota((N_HEADS, BLK), 1) >= PAD)
    pre = dt_ref[...] + dtb_ref[...]
    pre_t = dtt_ref[...] + dtbt_ref[...]
    dt = jnp.where(live, _softplus(pre), 0.0)
    dt_t = jnp.where(live_t, _softplus(pre_t), 0.0)
    a = -jnp.exp(alog_ref[...])
    a_t = -jnp.exp(alogt_ref[...])
    li = _iota((BLK, BLK), 0)
    si = _iota((BLK, BLK), 1)
    lower = (si <= li).astype(BF16)
    upper = (li <= si).astype(BF16)
    acum = _dot_exact_x(lower, dt * a, NN)
    acum_t = _dot_x_exact(dt_t * a_t, upper, NN)
    return live, pre, dt, a, a_t, acum, acum_t


def _head_expand():
    return (_iota((N_HEADS, SSD_W), 1) // HEAD_DIM == _iota((N_HEADS, SSD_W), 0)).astype(BF16)


def _head_reduce_mat():
    return (_iota((SSD_W, N_HEADS), 0) // HEAD_DIM == _iota((SSD_W, N_HEADS), 1)).astype(BF16)


def _decay_mat(acum, acum_t, h, causal):
    seg = jnp.minimum(acum[:, h:h + 1] - acum_t[h:h + 1, :], 0.0)
    return jnp.where(causal, jnp.exp(seg), 0.0)


def _ssd_specs():
    chunk = lambda width, col=0: pl.BlockSpec((BLK, width), lambda c: (c, col))
    return chunk


def _ssd_fwd(xa, dt_raw, dt_raw_t, dt_bias, dt_bias_t, a_log, a_log_t, d_exp):
    def body(x_ref, b_ref, c_ref, dt_ref, dtt_ref, dtb_ref, dtbt_ref, alog_ref, alogt_ref, dexp_ref,
             y_ref, hs_ref, state_ref):
        c = pl.program_id(0)

        @pl.when(c == 0)
        def _():
            state_ref[...] = jnp.zeros_like(state_ref)

        _, _, dt, _, _, acum, acum_t = _ssd_prelude(c, dt_ref, dtt_ref, dtb_ref, dtbt_ref, alog_ref, alogt_ref)
        expand = _head_expand()
        x = x_ref[...]
        xdt = x * _dot_x_exact(dt, expand)
        exp_a = _dot_x_exact(jnp.exp(acum), expand)
        to_end = _dot_x_exact(jnp.exp(acum[BLK - 1:BLK, :] - acum), expand)
        xdt_b = xdt.astype(BF16)
        xd_b = (xdt * to_end).astype(BF16)
        chunk_decay = jnp.exp(acum_t[:, BLK - 1:BLK])
        hs_ref[0] = state_ref[...]
        lane = _iota((BLK, BLK), 1)
        causal = _iota((BLK, BLK), 0) >= lane
        gw = HEADS_PER_GROUP * HEAD_DIM
        for g in range(N_GROUPS):
            bg = b_ref[:, g * N_STATE:(g + 1) * N_STATE].astype(BF16)
            cg = c_ref[:, g * N_STATE:(g + 1) * N_STATE].astype(BF16)
            cb = _dot(cg, bg, NT)
            hg = state_ref[g * gw:(g + 1) * gw, :]
            ch = _dot(cg, hg.astype(BF16), NT)
            st = _dot(xd_b[:, g * gw:(g + 1) * gw], bg, TN)
            for p in range(HEADS_PER_GROUP // 2):
                h0 = g * HEADS_PER_GROUP + 2 * p
                lo = h0 * HEAD_DIM
                xp = xdt_b[:, lo:lo + BLK]
                w0 = (cb * _decay_mat(acum, acum_t, h0, causal)).astype(BF16)
                w1 = (cb * _decay_mat(acum, acum_t, h0 + 1, causal)).astype(BF16)
                yd = jnp.where(lane < HEAD_DIM, _dot(w0, xp), _dot(w1, xp))
                y_ref[:, lo:lo + BLK] = (yd + ch[:, lo - g * gw:lo - g * gw + BLK] * exp_a[:, lo:lo + BLK]
                                         + x[:, lo:lo + BLK] * dexp_ref[:, lo:lo + BLK])
            for r in range(HEADS_PER_GROUP):
                h = g * HEADS_PER_GROUP + r
                state_ref[h * HEAD_DIM:(h + 1) * HEAD_DIM, :] = (
                    hg[r * HEAD_DIM:(r + 1) * HEAD_DIM, :] * chunk_decay[h:h + 1, :]
                    + st[r * HEAD_DIM:(r + 1) * HEAD_DIM, :])

    chunk = _ssd_specs()
    return pl.pallas_call(
        body, name="ssd_fwd",
        out_shape=(jax.ShapeDtypeStruct((LP, SSD_W), F32), jax.ShapeDtypeStruct((NBLK, SSD_W, N_STATE), F32)),
        grid=(NBLK,),
        in_specs=[chunk(SSD_W), chunk(256, 4), chunk(256, 5), chunk(N_HEADS),
                  pl.BlockSpec((N_HEADS, BLK), lambda c: (0, c)), _const_spec((1, N_HEADS)),
                  _const_spec((N_HEADS, 1)), _const_spec((1, N_HEADS)), _const_spec((N_HEADS, 1)),
                  _const_spec((1, SSD_W))],
        out_specs=(chunk(SSD_W), pl.BlockSpec((1, SSD_W, N_STATE), lambda c: (c, 0, 0))),
        scratch_shapes=[pltpu.VMEM((SSD_W, N_STATE), F32)],
        compiler_params=_params(("arbitrary",)),
    )(xa, xa, xa, dt_raw, dt_raw_t, dt_bias, dt_bias_t, a_log, a_log_t, d_exp)


def _ssd_bwd(xa, dt_raw, dt_raw_t, dt_bias, dt_bias_t, a_log, a_log_t, d_exp, hstart, dy):
    def body(x_ref, b_ref, c_ref, dt_ref, dtt_ref, dtb_ref, dtbt_ref, alog_ref, alogt_ref, dexp_ref,
             hs_ref, dy_ref, dx_ref, db_ref, dc_ref, ddt_ref, dbias_ref, dalog_ref, dd_ref, dstate_ref):
        step = pl.program_id(0)
        c = NBLK - 1 - step

        @pl.when(step == 0)
        def _():
            dstate_ref[...] = jnp.zeros_like(dstate_ref)
            dbias_ref[...] = jnp.zeros_like(dbias_ref)
            dalog_ref[...] = jnp.zeros_like(dalog_ref)
            dd_ref[...] = jnp.zeros_like(dd_ref)

        live, pre, dt, a, a_t, acum, acum_t = _ssd_prelude(c, dt_ref, dtt_ref, dtb_ref, dtbt_ref,
                                                           alog_ref, alogt_ref)
        expand = _head_expand()
        reduce_m = _head_reduce_mat()
        x = x_ref[...]
        dyv = dy_ref[...]
        dt_e = _dot_x_exact(dt, expand)
        xdt = x * dt_e
        exp_acum = jnp.exp(acum)
        exp_a = _dot_x_exact(exp_acum, expand)
        dte = jnp.exp(acum[BLK - 1:BLK, :] - acum)
        to_end = _dot_x_exact(dte, expand)
        xdt_b = xdt.astype(BF16)
        xd_b = (xdt * to_end).astype(BF16)
        chunk_decay = jnp.exp(acum_t[:, BLK - 1:BLK])
        lane = _iota((BLK, BLK), 1)
        head0 = lane < HEAD_DIM
        causal = _iota((BLK, BLK), 0) >= lane
        gw = HEADS_PER_GROUP * HEAD_DIM
        dm = dyv * exp_a
        dm_b = dm.astype(BF16)
        onehot = lambda h: (_iota((1, N_HEADS), 1) == h).astype(F32)
        onehot_t = lambda h: (_iota((N_HEADS, 1), 0) == h).astype(F32)
        dacum = jnp.zeros((BLK, N_HEADS), F32)
        dacum_t = jnp.zeros((N_HEADS, BLK), F32)
        ddt_acc = jnp.zeros((BLK, N_HEADS), F32)
        ddte_acc = jnp.zeros((BLK, N_HEADS), F32)
        dexpa_acc = jnp.zeros((BLK, N_HEADS), F32)
        dskip_acc = jnp.zeros((BLK, N_HEADS), F32)
        head_sum = expand
        for g in range(N_GROUPS):
            bg = b_ref[:, g * N_STATE:(g + 1) * N_STATE].astype(BF16)
            cg = c_ref[:, g * N_STATE:(g + 1) * N_STATE].astype(BF16)
            cb = _dot(cg, bg, NT)
            hg = hs_ref[0, g * gw:(g + 1) * gw, :]
            hg_b = hg.astype(BF16)
            dhe = dstate_ref[g * gw:(g + 1) * gw, :]
            dhe_b = dhe.astype(BF16)
            ch = _dot(cg, hg_b, NT)
            dcg = _dot(dm_b[:, g * gw:(g + 1) * gw], hg_b, NN)
            dhs = _dot(dm_b[:, g * gw:(g + 1) * gw], cg, TN)
            dxd = _dot(bg, dhe_b, NT)
            dbg = _dot(xd_b[:, g * gw:(g + 1) * gw], dhe_b, NN)
            dcb = jnp.zeros((BLK, BLK), F32)
            for p in range(HEADS_PER_GROUP // 2):
                h0 = g * HEADS_PER_GROUP + 2 * p
                lo = h0 * HEAD_DIM
                xp = xdt_b[:, lo:lo + BLK]
                dyp = dyv[:, lo:lo + BLK]
                dyh = (jnp.where(head0, dyp, 0.0).astype(BF16), jnp.where(head0, 0.0, dyp).astype(BF16))
                dxdt_p = jnp.zeros((BLK, BLK), F32)
                for q in range(2):
                    h = h0 + q
                    dec = _decay_mat(acum, acum_t, h, causal)
                    w = cb * dec
                    dw = _dot(dyh[q], xp, NT)
                    t = dw * w
                    dacum = dacum + jnp.sum(t, axis=1, keepdims=True) * onehot(h)
                    dacum_t = dacum_t - jnp.sum(t, axis=0, keepdims=True) * onehot_t(h)
                    dcb = dcb + dw * dec
                    dxdt_p = dxdt_p + _dot(w.astype(BF16), dyh[q], TN)
                sl = slice(lo, lo + BLK)
                gl = slice(lo - g * gw, lo - g * gw + BLK)
                dxdt_p = dxdt_p + dxd[:, gl] * to_end[:, sl]
                dx_ref[:, sl] = dyp * dexp_ref[:, sl] + dxdt_p * dt_e[:, sl]
                red = reduce_m[lo:lo + BLK, :]
                ddt_acc = ddt_acc + _dot_x_exact(dxdt_p * x[:, sl], red)
                ddte_acc = ddte_acc + _dot_x_exact(dxd[:, gl] * xdt[:, sl], red)
                dexpa_acc = dexpa_acc + _dot_x_exact(dyp * ch[:, gl], red)
                dskip_acc = dskip_acc + _dot_x_exact(dyp * x[:, sl], red)
            dcb_b = dcb.astype(BF16)
            dc_ref[:, g * N_STATE:(g + 1) * N_STATE] = dcg + _dot(dcb_b, bg, NN)
            db_ref[:, g * N_STATE:(g + 1) * N_STATE] = dbg + _dot(dcb_b, cg, TN)
            prod = dhe * hg
            per_head = jnp.sum(_dot_exact_x(head_sum[:, g * gw:(g + 1) * gw], prod, NN), axis=1, keepdims=True)
            dacum_t = dacum_t + (per_head * chunk_decay) * (_iota((1, BLK), 1) == BLK - 1).astype(F32)
            for r in range(HEADS_PER_GROUP):
                h = g * HEADS_PER_GROUP + r
                rows = slice(h * HEAD_DIM, (h + 1) * HEAD_DIM)
                dstate_ref[rows, :] = (dhs[r * HEAD_DIM:(r + 1) * HEAD_DIM, :]
                                       + dhe[r * HEAD_DIM:(r + 1) * HEAD_DIM, :] * chunk_decay[h:h + 1, :])
        dacum = dacum + dexpa_acc * exp_acum - ddte_acc * dte
        last_row = (_iota((BLK, 1), 0) == BLK - 1).astype(F32)
        dacum = dacum + last_row * jnp.sum(ddte_acc * dte, axis=0, keepdims=True)
        li = _iota((BLK, BLK), 0)
        si = _iota((BLK, BLK), 1)
        upper = (li <= si).astype(BF16)
        lower = (si <= li).astype(BF16)
        dda = _dot_exact_x(upper, dacum, NN)
        dda_t = _dot_x_exact(dacum_t, lower, NN)
        eye = (_iota((N_HEADS, N_HEADS), 0) == _iota((N_HEADS, N_HEADS), 1)).astype(BF16)
        dda = dda + _dot_x_exact_tn(dda_t, eye)
        ddt = ddt_acc + dda * a
        dalog_ref[...] += jnp.sum(dda * dt, axis=0, keepdims=True) * a
        dd_ref[...] += jnp.sum(dskip_acc, axis=0, keepdims=True)
        ddt_raw = jnp.where(live, ddt * _sigmoid(pre), 0.0)
        ddt_ref[...] = ddt_raw
        dbias_ref[...] += jnp.sum(ddt_raw, axis=0, keepdims=True)

    rev = lambda width, col=0: pl.BlockSpec((BLK, width), lambda s: (NBLK - 1 - s, col))
    vec = jax.ShapeDtypeStruct((1, N_HEADS), F32)
    return pl.pallas_call(
        body, name="ssd_bwd",
        out_shape=(jax.ShapeDtypeStruct((LP, SSD_W), F32), jax.ShapeDtypeStruct((LP, 256), F32),
                   jax.ShapeDtypeStruct((LP, 256), F32), jax.ShapeDtypeStruct((LP, N_HEADS), F32), vec, vec, vec),
        grid=(NBLK,),
        in_specs=[rev(SSD_W), rev(256, 4), rev(256, 5), rev(N_HEADS),
                  pl.BlockSpec((N_HEADS, BLK), lambda s: (0, NBLK - 1 - s)), _const_spec((1, N_HEADS)),
                  _const_spec((N_HEADS, 1)), _const_spec((1, N_HEADS)), _const_spec((N_HEADS, 1)),
                  _const_spec((1, SSD_W)),
                  pl.BlockSpec((1, SSD_W, N_STATE), lambda s: (NBLK - 1 - s, 0, 0)), rev(SSD_W)],
        out_specs=(rev(SSD_W), rev(256), rev(256), rev(N_HEADS), _const_spec((1, N_HEADS)),
                   _const_spec((1, N_HEADS)), _const_spec((1, N_HEADS))),
        scratch_shapes=[pltpu.VMEM((SSD_W, N_STATE), F32)],
        compiler_params=_params(("arbitrary",)),
    )(xa, xa, xa, dt_raw, dt_raw_t, dt_bias, dt_bias_t, a_log, a_log_t, d_exp, hstart, dy)


def _dot_x_exact_tn(x_t, eye):
    out = None
    for p in _split(x_t, 3):
        t = _dot(p, eye, TN)
        out = t if out is None else out + t
    return out


def _adamw(name, parts, w, m, v, rows):
    r_all, cols = w.shape
    assert r_all % rows == 0
    c1 = 1.0 / (1.0 - ADAM_B1 ** ADAM_STEP)
    c2 = 1.0 / (1.0 - ADAM_B2 ** ADAM_STEP)

    def body(p_ref, w_ref, m_ref, v_ref, g_ref, d_ref, mo_ref, vo_ref):
        g = p_ref[0].astype(F32)
        for j in range(1, parts.shape[0]):
            g = g + p_ref[j].astype(F32)
        mn = ADAM_B1 * m_ref[...] + (1.0 - ADAM_B1) * g
        vn = ADAM_B2 * v_ref[...] + (1.0 - ADAM_B2) * (g * g)
        g_ref[...] = g
        mo_ref[...] = mn
        vo_ref[...] = vn
        d_ref[...] = -ADAM_LR * ((mn * c1) / (jnp.sqrt(vn * c2) + ADAM_EPS) + ADAM_WD * w_ref[...])

    spec = pl.BlockSpec((rows, cols), lambda i: (i, 0))
    shp = jax.ShapeDtypeStruct((r_all, cols), F32)
    return pl.pallas_call(
        body, name=name, out_shape=(shp, shp, shp, shp), grid=(r_all // rows,),
        in_specs=[pl.BlockSpec((parts.shape[0], rows, cols), lambda i: (0, i, 0)), spec, spec, spec],
        out_specs=(spec, spec, spec, spec),
        compiler_params=_params(("parallel",)),
    )(parts, w, m, v)


_REP_ROWS = (("norm_w", 8), ("conv_b", 12), ("dt_bias", 1), ("a_log", 1), ("d_skip", 1),
             ("sb_norm_w", 8), ("ssd_norm_w", 8), ("final_norm_w", 8))
_LOSS_ROW = 47
_CONVW_ROW = 48
_META_ROW = 54
_PACK_ROWS = 72


def _rows128(v, rows):
    flat = v.reshape(-1)
    return jnp.pad(flat, (0, rows * BLK - flat.shape[0])).reshape(rows, BLK)


def _pack_small(rep, conv_w_shard, meta_shard, loss_row=None):
    parts = [_rows128(rep[name], rows) for name, rows in _REP_ROWS]
    parts.append(jnp.zeros((1, BLK), F32) if loss_row is None else loss_row)
    parts.append(conv_w_shard.reshape(6, BLK))
    parts.append(meta_shard)
    parts.append(jnp.zeros((_PACK_ROWS - _META_ROW - N_META, BLK), F32))
    return jnp.concatenate(parts, axis=0)


def _unpack_small(pack, shapes):
    out = {}
    r = 0
    for name, rows in _REP_ROWS:
        n = 1
        for s in shapes[name]:
            n *= s
        out[name] = pack[r:r + rows].reshape(-1)[:n].reshape(shapes[name])
        r += rows
    out["conv_w"] = pack[_CONVW_ROW:_CONVW_ROW + 6].reshape(1, 4, 192)
    out["meta_tokens"] = pack[_META_ROW:_META_ROW + N_META]
    return out


def _forward_backward(h_pad, target, norm_w, w_main, w_dt, w_out_full, conv_w_full, conv_b, dt_bias, a_log, d_skip,
                      sb_norm_w, ssd_norm_w, final_norm_w):
    w_dt_t = w_dt.T
    dt_bias_t = dt_bias.reshape(N_HEADS, 1)
    a_log_t = a_log.reshape(N_HEADS, 1)
    d_exp = jnp.repeat(d_skip, HEAD_DIM, axis=1)

    u, dt_raw, dt_raw_t = _prenorm(h_pad, norm_w, w_dt, w_dt_t)
    proj = _matmul("in_proj", u, w_main, "nn", 1088, 512, D_MODEL)
    o_sb, o_sb_exact = _sb_attention_fwd(proj)
    xa = _conv_fwd(proj, conv_w_full, conv_b)
    o_ssd, hstart = _ssd_fwd(xa, dt_raw, dt_raw_t, dt_bias, dt_bias_t, a_log, a_log_t, d_exp)
    ycat = _ycat(o_sb, proj, o_ssd, sb_norm_w, ssd_norm_w)
    yo = _matmul("out_proj", ycat, w_out_full, "nn", 1088, 512, 2 * SSD_W)
    dh2, dh2_b, loss_row, d_fnw = _loss_head(h_pad, yo, final_norm_w.reshape(1, D_MODEL), target)

    g_w_out = _matmul("d_w_out", ycat, dh2_b, "tn", 512, 512, LP, BF16)
    dycat = _matmul("d_ycat", dh2_b, w_out_full, "nt", 1088, 512, D_MODEL)
    do_sb, dg, do_ssd, dz, d_sbw, d_ssdw = _ycat_bwd(dycat, o_sb, proj, o_ssd, sb_norm_w, ssd_norm_w)
    dq, dk, dv = _sb_attention_bwd(proj, o_sb_exact, do_sb)
    dxs, dbm, dcm, ddt_raw, d_dtb, d_alog, d_dskip = _ssd_bwd(
        xa, dt_raw, dt_raw_t, dt_bias, dt_bias_t, a_log, a_log_t, d_exp, hstart, do_ssd)
    dxa = jnp.concatenate([dxs, dbm, dcm], axis=1)
    dxbc, d_convw, d_convb = _conv_bwd(dxa, proj, conv_w_full, conv_b)
    dproj = jnp.concatenate([t.astype(BF16) for t in (dq, dk, dv, dg, dz, dxbc)], axis=1)
    g_w_main = _matmul("d_w_in", u, dproj, "tn", 512, 512, LP, BF16)
    du = _matmul("d_u", dproj, w_main, "nt", 1088, 512, 1664)
    dh, d_nw, g_w_dt = _prenorm_bwd(du, ddt_raw, w_dt, h_pad, norm_w, dh2)
    return dict(loss_row=loss_row, dh=dh, w_main=g_w_main, w_dt=g_w_dt, w_out=g_w_out, conv_w=d_convw,
                norm_w=d_nw, conv_b=d_convb, dt_bias=d_dtb, a_log=d_alog, d_skip=d_dskip,
                sb_norm_w=d_sbw, ssd_norm_w=d_ssdw, final_norm_w=d_fnw)


def kernel(x, meta_tokens, norm_w, w_in, conv_w, conv_b, dt_bias, a_log, d_skip, sb_norm_w, ssd_norm_w, w_out, final_norm_w, loss_target, m_meta_tokens, m_norm_w, m_w_in, m_conv_w, m_conv_b, m_dt_bias, m_a_log, m_d_skip, m_sb_norm_w, m_ssd_norm_w, m_w_out, m_final_norm_w, v_meta_tokens, v_norm_w, v_w_in, v_conv_w, v_conv_b, v_dt_bias, v_a_log, v_d_skip, v_sb_norm_w, v_ssd_norm_w, v_w_out, v_final_norm_w):
    small_src = jnp.concatenate([conv_w[0].reshape(6, BLK), meta_tokens, jnp.zeros((2, BLK), F32)], axis=0)
    small_g, w_out_g, w_in_g = _gather_weights([small_src, w_out[0].astype(BF16), w_in[0].astype(BF16)])
    w_in_full = w_in_g.transpose(1, 0, 2).reshape(D_MODEL, D_IN)
    w_out_full = w_out_g.reshape(2 * SSD_W, D_MODEL)
    conv_w_full = small_g[:, :6].reshape(N_DEV, 4, 192).transpose(1, 0, 2).reshape(4, XBC_W)
    meta_full = small_g[:, 6:6 + N_META].transpose(1, 0, 2).reshape(N_META, D_MODEL)
    h_pad = jnp.concatenate([jnp.zeros((PAD, D_MODEL), F32), meta_full, x[0]], axis=0)

    grads = _forward_backward(h_pad, loss_target[0], norm_w, w_in_full[:, :D_MAIN], w_in_full[:, D_MAIN:],
                              w_out_full, conv_w_full, conv_b, dt_bias, a_log, d_skip, sb_norm_w, ssd_norm_w,
                              final_norm_w)
    dh = grads["dh"]

    g_w_in = jnp.concatenate([grads["w_main"], grads["w_dt"].astype(BF16)], axis=1)
    g_w_in = g_w_in.reshape(D_MODEL, 4, 2, D_IN // N_DEV).transpose(2, 1, 0, 3)
    g_w_out = grads["w_out"].reshape(4, 2, 256, D_MODEL).transpose(1, 0, 2, 3)
    d_convw = grads["conv_w"]
    d_meta = dh[PAD:PAD + N_META]
    packs = [_pack_small(grads, d_convw[:, 192 * j:192 * (j + 1)], d_meta[:, BLK * j:BLK * (j + 1)],
                         grads["loss_row"]) for j in range(N_DEV)]
    g_small = jnp.stack(packs).reshape(4, 2, _PACK_ROWS, BLK).transpose(1, 0, 2, 3)
    sib = _swap_with_sibling([g_small, g_w_out, g_w_in])
    sums = [_pair_sum("pair_sum_small", g_small, sib[0], _PACK_ROWS),
            _pair_sum("pair_sum_w_out", g_w_out, sib[1], 256),
            _pair_sum("pair_sum_w_in", g_w_in, sib[2], 256)]
    p_small, p_w_out, p_w_in = _exchange_between_chips(sums)

    rep_w = {"norm_w": norm_w, "conv_b": conv_b, "dt_bias": dt_bias, "a_log": a_log, "d_skip": d_skip,
             "sb_norm_w": sb_norm_w, "ssd_norm_w": ssd_norm_w, "final_norm_w": final_norm_w}
    rep_m = {"norm_w": m_norm_w, "conv_b": m_conv_b, "dt_bias": m_dt_bias, "a_log": m_a_log, "d_skip": m_d_skip,
             "sb_norm_w": m_sb_norm_w, "ssd_norm_w": m_ssd_norm_w, "final_norm_w": m_final_norm_w}
    rep_v = {"norm_w": v_norm_w, "conv_b": v_conv_b, "dt_bias": v_dt_bias, "a_log": v_a_log, "d_skip": v_d_skip,
             "sb_norm_w": v_sb_norm_w, "ssd_norm_w": v_ssd_norm_w, "final_norm_w": v_final_norm_w}
    res_in = _adamw("adamw_w_in", p_w_in, w_in[0], m_w_in[0], v_w_in[0], 128)
    res_out = _adamw("adamw_w_out", p_w_out, w_out[0], m_w_out[0], v_w_out[0], 128)
    res_small = _adamw("adamw_small", p_small, _pack_small(rep_w, conv_w[0], meta_tokens),
                       _pack_small(rep_m, m_conv_w[0], m_meta_tokens),
                       _pack_small(rep_v, v_conv_w[0], v_meta_tokens), _PACK_ROWS)

    shapes = {name: rep_w[name].shape for name, _ in _REP_ROWS}
    loss = jnp.sum(res_small[0][_LOSS_ROW])
    order = ["meta_tokens", "norm_w", "w_in", "conv_w", "conv_b", "dt_bias", "a_log", "d_skip",
             "sb_norm_w", "ssd_norm_w", "w_out", "final_norm_w"]
    outs = [loss, dh[BLK:].reshape(1, SEQ, D_MODEL)]
    for kind in range(4):
        small = _unpack_small(res_small[kind], shapes)
        small["w_in"] = res_in[kind].reshape(1, D_MODEL, D_IN // N_DEV)
        small["w_out"] = res_out[kind].reshape(1, 256, D_MODEL)
        outs += [small[name] for name in order]
    return tuple(outs)
```

```python
import functools

import jax
import jax.numpy as jnp
from jax import lax
from jax.experimental import pallas as pl
from jax.experimental.pallas import tpu as pltpu

F32 = jnp.float32
BF16 = jnp.bfloat16

D_MODEL = 1024
SEQ = 2048
N_META = 16
BLK = 128
PAD = BLK - N_META
LP = PAD + N_META + SEQ
NBLK = LP // BLK
N_HEADS = 16
HEAD_DIM = 64
N_GROUPS = 2
HEADS_PER_GROUP = 8
N_STATE = 128
SSD_W = 1024
XBC_W = 1536
D_MAIN = 6656
D_IN = 6672
COL_Q, COL_K, COL_V, COL_G, COL_Z, COL_XBC = 0, 1024, 2048, 3072, 4096, 5120
N_DEV = 8
EPS = 1e-5
SB_SCALE = 0.125
SB_DEAD = -87.4

ADAM_LR = 0.001
ADAM_B1 = 0.9
ADAM_B2 = 0.999
ADAM_EPS = 1e-08
ADAM_WD = 0.01
ADAM_STEP = 10

VMEM_LIMIT = 48 * 1024 * 1024

NN = (((1,), (0,)), ((), ()))
NT = (((1,), (1,)), ((), ()))
TN = (((0,), (0,)), ((), ()))


def _dot(a, b, dims=NN):
    return lax.dot_general(a, b, dims, preferred_element_type=F32)


def _split(x, n):
    parts = []
    r = x
    for i in range(n):
        p = r.astype(BF16)
        parts.append(p)
        if i + 1 < n:
            r = r - p.astype(F32)
    return parts


def _dot_x_exact(x, m, dims=NN, n=3):
    out = None
    for p in _split(x, n):
        t = _dot(p, m, dims)
        out = t if out is None else out + t
    return out


def _dot_exact_x(m, x, dims=NN, n=3):
    out = None
    for p in _split(x, n):
        t = _dot(m, p, dims)
        out = t if out is None else out + t
    return out


def _iota(shape, dim):
    return lax.broadcasted_iota(jnp.int32, shape, dim)


def _softplus(x):
    return jnp.maximum(x, 0.0) + jnp.log(1.0 + jnp.exp(-jnp.abs(x)))


def _sigmoid(x):
    return 1.0 / (1.0 + jnp.exp(-x))


def _params(sem=None):
    return pltpu.CompilerParams(dimension_semantics=sem, vmem_limit_bytes=VMEM_LIMIT)


_ANY = pl.BlockSpec(memory_space=pl.ANY)
_MESH = pl.DeviceIdType.MESH


def _mesh_position():
    return lax.axis_index("x"), lax.axis_index("y"), lax.axis_index("c")


def _other_chips(x, y):
    return [(1 - x, y), (x, 1 - y), (1 - x, 1 - y)]


def _gather_weights(srcs):
    n = len(srcs)

    def body(*refs):
        src, out = refs[:n], refs[n:2 * n]
        send_sems, recv_sems, loc_sems = refs[2 * n:]
        x, y, c = _mesh_position()
        sibling = (x, y, 1 - c)
        chips = _other_chips(x, y)

        def copy(a, k, block, to, from_src=False):
            slot = out[a].at[4 * block[0] + 2 * block[1] + block[2]]
            return pltpu.make_async_remote_copy(
                src_ref=src[a] if from_src else slot, dst_ref=slot,
                send_sem=send_sems.at[7 * a + k], recv_sem=recv_sems.at[7 * a + k],
                device_id=to, device_id_type=_MESH)

        local, sends = [], []
        for a in range(n):
            mine = pltpu.make_async_copy(src[a], out[a].at[4 * x + 2 * y + c], loc_sems.at[a])
            mine.start()
            local.append(mine)
            first = [copy(a, 0, (x, y, c), sibling, True)]
            first += [copy(a, 1 + j, (x, y, c), (*chip, c), True) for j, chip in enumerate(chips)]
            for cp in first:
                cp.start()
            sends += first
        for a in range(n):
            for j, chip in enumerate(chips):
                copy(a, 1 + j, (*chip, c), (x, y, c)).wait_recv()
                passed = copy(a, 4 + j, (*chip, c), sibling)
                passed.start()
                sends.append(passed)
        for a in range(n):
            copy(a, 0, (x, y, 1 - c), (x, y, c)).wait_recv()
            for j, chip in enumerate(chips):
                copy(a, 4 + j, (*chip, 1 - c), (x, y, c)).wait_recv()
        for cp in sends:
            cp.wait_send()
        for cp in local:
            cp.wait()

    return pl.pallas_call(
        body, name="gather_weights",
        out_shape=tuple(jax.ShapeDtypeStruct((N_DEV,) + s.shape, s.dtype) for s in srcs),
        in_specs=[_ANY] * n, out_specs=tuple([_ANY] * n),
        scratch_shapes=[pltpu.SemaphoreType.DMA((7 * n,)), pltpu.SemaphoreType.DMA((7 * n,)),
                        pltpu.SemaphoreType.DMA((n,))],
    )(*srcs)


def _swap_with_sibling(name, src, whole=False):
    def body(src_ref, out_ref, send_sem, recv_sem):
        x, y, c = _mesh_position()
        cp = pltpu.make_async_remote_copy(
            src_ref=src_ref if whole else src_ref.at[1 - c], dst_ref=out_ref, send_sem=send_sem, recv_sem=recv_sem,
            device_id=(x, y, 1 - c), device_id_type=_MESH)
        cp.start()
        cp.wait()

    return pl.pallas_call(
        body, name=name, out_shape=jax.ShapeDtypeStruct(src.shape if whole else src.shape[1:], src.dtype),
        in_specs=[_ANY], out_specs=_ANY,
        scratch_shapes=[pltpu.SemaphoreType.DMA, pltpu.SemaphoreType.DMA],
    )(src)


def _pair_sum(name, g, sib, rows):
    _, _, r_all, cols = g.shape
    assert r_all % rows == 0

    def body(g0_ref, g1_ref, s_ref, o_ref):
        c = lax.axis_index("c")
        mine = jnp.where(c == 0, g0_ref[0, 0].astype(F32), g1_ref[0, 0].astype(F32))
        o_ref[0] = (mine + s_ref[0].astype(F32)).astype(o_ref.dtype)

    return pl.pallas_call(
        body, name=name, out_shape=jax.ShapeDtypeStruct((4, r_all, cols), BF16), grid=(4, r_all // rows),
        in_specs=[pl.BlockSpec((1, 1, rows, cols), lambda k, i: (0, k, i, 0)),
                  pl.BlockSpec((1, 1, rows, cols), lambda k, i: (1, k, i, 0)),
                  pl.BlockSpec((1, rows, cols), lambda k, i: (k, i, 0))],
        out_specs=pl.BlockSpec((1, rows, cols), lambda k, i: (k, i, 0)),
        compiler_params=_params(("parallel", "parallel")),
    )(g, g, sib)


_CHIP_EXCHANGE_SEMS = [pltpu.SemaphoreType.DMA((3,)), pltpu.SemaphoreType.DMA((3,)), pltpu.SemaphoreType.DMA]


def _chip_exchange_copies(src, out, send_sems, recv_sems, loc_sem, whole=False):
    x, y, c = _mesh_position()
    here = 2 * x + y
    copies = [pltpu.make_async_copy(src if whole else src.at[here], out.at[here], loc_sem)]
    for j, chip in enumerate(_other_chips(x, y)):
        copies.append(pltpu.make_async_remote_copy(
            src_ref=src if whole else src.at[2 * chip[0] + chip[1]], dst_ref=out.at[here],
            send_sem=send_sems.at[j], recv_sem=recv_sems.at[j], device_id=(*chip, c), device_id_type=_MESH))
    return copies


def _exchange_between_chips(name, src, whole=False):
    def body(src_ref, out_ref, send_sems, recv_sems, loc_sem):
        copies = _chip_exchange_copies(src_ref, out_ref, send_sems, recv_sems, loc_sem, whole)
        for cp in copies:
            cp.start()
        for cp in copies:
            cp.wait()

    return pl.pallas_call(
        body, name=name, out_shape=jax.ShapeDtypeStruct(((4,) + src.shape) if whole else src.shape, src.dtype),
        in_specs=[_ANY], out_specs=_ANY, scratch_shapes=_CHIP_EXCHANGE_SEMS,
    )(src)


def _matmul(name, a, b, kind, tm, tn, tk, out_dtype=F32):
    if kind == "nn":
        (m, kk), (_, nn_) = a.shape, b.shape
        a_spec = pl.BlockSpec((tm, tk), lambda i, j, k: (i, k))
        b_spec = pl.BlockSpec((tk, tn), lambda i, j, k: (k, j))
        dims = NN
    elif kind == "nt":
        (m, kk), (nn_, _) = a.shape, b.shape
        a_spec = pl.BlockSpec((tm, tk), lambda i, j, k: (i, k))
        b_spec = pl.BlockSpec((tn, tk), lambda i, j, k: (j, k))
        dims = NT
    else:
        (kk, m), (_, nn_) = a.shape, b.shape
        a_spec = pl.BlockSpec((tk, tm), lambda i, j, k: (k, i))
        b_spec = pl.BlockSpec((tk, tn), lambda i, j, k: (k, j))
        dims = TN
    assert m % tm == 0 and nn_ % tn == 0 and kk % tk == 0
    nk = kk // tk

    def body(a_ref, b_ref, o_ref, acc_ref):
        k = pl.program_id(2)
        part = _dot(a_ref[...], b_ref[...], dims)
        if nk == 1:
            o_ref[...] = part.astype(o_ref.dtype)
        else:
            @pl.when(k == 0)
            def _():
                acc_ref[...] = part

            @pl.when(k > 0)
            def _():
                acc_ref[...] += part

            @pl.when(k == nk - 1)
            def _():
                o_ref[...] = acc_ref[...].astype(o_ref.dtype)

    return pl.pallas_call(
        body, name=name, out_shape=jax.ShapeDtypeStruct((m, nn_), out_dtype),
        grid=(m // tm, nn_ // tn, nk),
        in_specs=[a_spec, b_spec], out_specs=pl.BlockSpec((tm, tn), lambda i, j, k: (i, j)),
        scratch_shapes=[pltpu.VMEM((tm, tn) if nk > 1 else (8, 128), F32)],
        compiler_params=_params(("parallel", "parallel", "arbitrary")),
    )(a, b)


def _row_spec(width, col=0):
    return pl.BlockSpec((BLK, width), lambda i: (i, col))


def _const_spec(shape):
    return pl.BlockSpec(shape, lambda i: tuple(0 for _ in shape))


def _prenorm(h_pad, norm_w, wdt, wdt_t):
    def body(h_ref, w_ref, wdt_ref, wdtt_ref, u_ref, dt_ref, dtt_ref):
        xv = h_ref[...]
        r = lax.rsqrt(jnp.mean(xv * xv, axis=-1, keepdims=True) + EPS)
        u = (xv * r * w_ref[...]).astype(BF16)
        u_ref[...] = u
        dt_ref[...] = _dot(u, wdt_ref[...], NN)
        dtt_ref[...] = _dot(wdtt_ref[...], u, NT)

    return pl.pallas_call(
        body, name="prenorm",
        out_shape=(jax.ShapeDtypeStruct((LP, D_MODEL), BF16), jax.ShapeDtypeStruct((LP, N_HEADS), F32),
                   jax.ShapeDtypeStruct((N_HEADS, LP), F32)),
        grid=(NBLK,),
        in_specs=[_row_spec(D_MODEL), _const_spec((1, D_MODEL)), _const_spec((D_MODEL, N_HEADS)),
                  _const_spec((N_HEADS, D_MODEL))],
        out_specs=(_row_spec(D_MODEL), _row_spec(N_HEADS), pl.BlockSpec((N_HEADS, BLK), lambda i: (0, i))),
        compiler_params=_params(("parallel",)),
    )(h_pad, norm_w, wdt, wdt_t)


def _gated_norm(o, g, w):
    a = o * (g * _sigmoid(g))
    r = lax.rsqrt(jnp.mean(a * a, axis=-1, keepdims=True) + EPS)
    return a * r * w


def _ycat(o_sb, proj, o_ssd, sb_w, ssd_w):
    def body(osb_ref, g_ref, ossd_ref, z_ref, sbw_ref, ssdw_ref, y_ref):
        y_ref[:, :SSD_W] = _gated_norm(osb_ref[...], g_ref[...], sbw_ref[...]).astype(BF16)
        y_ref[:, SSD_W:] = _gated_norm(ossd_ref[...], z_ref[...], ssdw_ref[...]).astype(BF16)

    return pl.pallas_call(
        body, name="ycat", out_shape=jax.ShapeDtypeStruct((LP, 2 * SSD_W), BF16), grid=(NBLK,),
        in_specs=[_row_spec(1024), _row_spec(1024, COL_G // 1024), _row_spec(1024),
                  _row_spec(1024, COL_Z // 1024), _const_spec((1, 1024)), _const_spec((1, 1024))],
        out_specs=_row_spec(2 * SSD_W),
        compiler_params=_params(("parallel",)),
    )(o_sb, proj, o_ssd, proj, sb_w, ssd_w)


def _loss_head(h_pad, yo, fnw, target):
    def body(h_ref, yo_ref, w_ref, t_ref, dh2_ref, dh2b_ref, loss_ref, dw_ref):
        i = pl.program_id(0)

        @pl.when(i == 0)
        def _():
            loss_ref[...] = jnp.zeros_like(loss_ref)
            dw_ref[...] = jnp.zeros_like(dw_ref)

        h2 = h_ref[...] + yo_ref[...]
        r = lax.rsqrt(jnp.mean(h2 * h2, axis=-1, keepdims=True) + EPS)
        nrm = h2 * r
        w = w_ref[...]
        live = i > 0
        err = jnp.where(live, nrm * w - t_ref[...], 0.0)
        dout = err * (1.0 / D_MODEL)
        loss_ref[...] += (0.5 / D_MODEL) * _fold_lanes(jnp.sum(err * err, axis=0, keepdims=True))
        dw_ref[...] += jnp.sum(dout * nrm, axis=0, keepdims=True)
        wd = dout * w
        dh2 = r * (wd - nrm * jnp.mean(wd * nrm, axis=-1, keepdims=True))
        dh2_ref[...] = dh2
        dh2b_ref[...] = dh2.astype(BF16)

    return pl.pallas_call(
        body, name="loss_head",
        out_shape=(jax.ShapeDtypeStruct((LP, D_MODEL), F32), jax.ShapeDtypeStruct((LP, D_MODEL), BF16),
                   jax.ShapeDtypeStruct((1, BLK), F32), jax.ShapeDtypeStruct((1, D_MODEL), F32)),
        grid=(NBLK,),
        in_specs=[_row_spec(D_MODEL), _row_spec(D_MODEL), _const_spec((1, D_MODEL)),
                  pl.BlockSpec((BLK, D_MODEL), lambda i: (jnp.maximum(i - 1, 0), 0))],
        out_specs=(_row_spec(D_MODEL), _row_spec(D_MODEL), _const_spec((1, BLK)), _const_spec((1, D_MODEL))),
        compiler_params=_params(("arbitrary",)),
    )(h_pad, yo, fnw, target)


def _fold_lanes(row):
    out = row[:, :BLK]
    for j in range(1, row.shape[1] // BLK):
        out = out + row[:, j * BLK:(j + 1) * BLK]
    return out


def _gated_norm_bwd(dy, o, g, w):
    s = _sigmoid(g)
    sg = g * s
    a = o * sg
    r = lax.rsqrt(jnp.mean(a * a, axis=-1, keepdims=True) + EPS)
    nrm = a * r
    dw = jnp.sum(dy * nrm, axis=0, keepdims=True)
    wd = dy * w
    da = r * (wd - nrm * jnp.mean(wd * nrm, axis=-1, keepdims=True))
    return da * sg, da * o * (s * (1.0 + g * (1.0 - s))), dw


def _ycat_bwd(dycat, o_sb, proj, o_ssd, sb_w, ssd_w):
    def body(dy_ref, osb_ref, g_ref, ossd_ref, z_ref, sbw_ref, ssdw_ref,
             dosb_ref, dg_ref, dossd_ref, dz_ref, dsbw_ref, dssdw_ref):
        @pl.when(pl.program_id(0) == 0)
        def _():
            dsbw_ref[...] = jnp.zeros_like(dsbw_ref)
            dssdw_ref[...] = jnp.zeros_like(dssdw_ref)

        do, dg, dw = _gated_norm_bwd(dy_ref[:, :SSD_W], osb_ref[...], g_ref[...], sbw_ref[...])
        dosb_ref[...] = do
        dg_ref[...] = dg
        dsbw_ref[...] += dw
        do, dg, dw = _gated_norm_bwd(dy_ref[:, SSD_W:], ossd_ref[...], z_ref[...], ssdw_ref[...])
        dossd_ref[...] = do
        dz_ref[...] = dg
        dssdw_ref[...] += dw

    act = jax.ShapeDtypeStruct((LP, 1024), F32)
    vec = jax.ShapeDtypeStruct((1, 1024), F32)
    return pl.pallas_call(
        body, name="ycat_bwd", out_shape=(act, act, act, act, vec, vec), grid=(NBLK,),
        in_specs=[_row_spec(2048), _row_spec(1024), _row_spec(1024, COL_G // 1024), _row_spec(1024),
                  _row_spec(1024, COL_Z // 1024), _const_spec((1, 1024)), _const_spec((1, 1024))],
        out_specs=(_row_spec(1024), _row_spec(1024), _row_spec(1024), _row_spec(1024),
                   _const_spec((1, 1024)), _const_spec((1, 1024))),
        compiler_params=_params(("arbitrary",)),
    )(dycat, o_sb, proj, o_ssd, proj, sb_w, ssd_w)


def _d_u_prenorm_bwd(dproj, w_main, ddt, wdt, h_pad, norm_w, dh2, chip_sums):
    tm, tk = LP // 4, D_MAIN // 4
    nk = D_MAIN // tk

    def body(a_ref, b_ref, ddt_ref, wdt_ref, h_ref, w_ref, dh2_ref, src_ref, dh_ref, dw_ref, out_ref,
             acc_ref, send_sems, recv_sems, loc_sem):
        i, k = pl.program_id(0), pl.program_id(1)

        @pl.when(jnp.logical_and(i == 0, k == 0))
        def _():
            for cp in _chip_exchange_copies(src_ref, out_ref, send_sems, recv_sems, loc_sem):
                cp.start()
            dw_ref[...] = jnp.zeros_like(dw_ref)

        part = _dot(a_ref[...], b_ref[...], NT)

        @pl.when(k == 0)
        def _():
            acc_ref[...] = part

        @pl.when(k > 0)
        def _():
            acc_ref[...] += part

        @pl.when(k == nk - 1)
        def _():
            dut = acc_ref[...] + _dot(ddt_ref[...].astype(BF16), wdt_ref[...], NT)
            xv = h_ref[...]
            r = lax.rsqrt(jnp.mean(xv * xv, axis=-1, keepdims=True) + EPS)
            nrm = xv * r
            dw_ref[...] += jnp.sum(dut * nrm, axis=0, keepdims=True)
            wd = dut * w_ref[...]
            dh_ref[...] = dh2_ref[...] + r * (wd - nrm * jnp.mean(wd * nrm, axis=-1, keepdims=True))

        @pl.when(jnp.logical_and(i == LP // tm - 1, k == nk - 1))
        def _():
            for cp in _chip_exchange_copies(src_ref, out_ref, send_sems, recv_sems, loc_sem):
                cp.wait()

    rows = lambda width: pl.BlockSpec((tm, width), lambda i, k: (i, 0))
    const = lambda shape: pl.BlockSpec(shape, lambda i, k: (0, 0))
    return pl.pallas_call(
        body, name="d_u_prenorm_bwd",
        out_shape=(jax.ShapeDtypeStruct((LP, D_MODEL), F32), jax.ShapeDtypeStruct((1, D_MODEL), F32),
                   jax.ShapeDtypeStruct(chip_sums.shape, chip_sums.dtype)),
        grid=(LP // tm, nk),
        in_specs=[pl.BlockSpec((tm, tk), lambda i, k: (i, k)), pl.BlockSpec((D_MODEL, tk), lambda i, k: (0, k)),
                  rows(N_HEADS), const((D_MODEL, N_HEADS)), rows(D_MODEL), const((1, D_MODEL)), rows(D_MODEL), _ANY],
        out_specs=(rows(D_MODEL), const((1, D_MODEL)), _ANY),
        scratch_shapes=[pltpu.VMEM((tm, D_MODEL), F32)] + _CHIP_EXCHANGE_SEMS,
        compiler_params=_params(("arbitrary", "arbitrary")),
    )(dproj, w_main, ddt, wdt, h_pad, norm_w, dh2, chip_sums)


def _suffix_sum(vals, tri):
    return _dot_x_exact(vals, tri, NN, n=2)


def _sb_tile(z, valid):
    t = jnp.exp(-jnp.abs(z))
    inv = 1.0 / (1.0 + t)
    sp = jnp.maximum(z, 0.0) + jnp.log(1.0 + t)
    sig = jnp.where(z >= 0, inv, t * inv)
    return sig, jnp.where(valid, -sp, 0.0), z - sp


def _sweep(first, step, init, run_slots):
    def alive_of(state):
        top = state[run_slots[0]]
        for s in run_slots[1:]:
            top = jnp.maximum(top, state[s])
        return (jnp.max(top) > SB_DEAD).astype(jnp.int32)

    def cond(carry):
        return jnp.logical_and(carry[0] >= 0, carry[1] > 0)

    def body(carry):
        state = step(carry[0], tuple(carry[2:]))
        return (carry[0] - 1, alive_of(state)) + tuple(state)

    return lax.while_loop(cond, body, (first, alive_of(init)) + tuple(init))[2:]


def _head_masks(x, lane):
    head0 = lane < HEAD_DIM
    return jnp.where(head0, x, 0.0).astype(BF16), jnp.where(head0, 0.0, x).astype(BF16)


def _stage_kv(qi, k_ref, v_ref, kb_ref, vb_ref):
    @pl.when(qi == 0)
    def _():
        kb_ref[:BLK, :] = jnp.zeros((BLK, BLK), BF16)
        vb_ref[:BLK, :] = jnp.zeros((BLK, BLK), BF16)
        kb_ref[BLK:, :] = k_ref[...].astype(BF16)
        vb_ref[BLK:, :] = v_ref[...].astype(BF16)


def _sb_attention_fwd(proj):
    def body(q_ref, k_ref, v_ref, o_ref, ox_ref, kb_ref, vb_ref):
        qi = pl.program_id(1)
        _stage_kv(qi, k_ref, v_ref, kb_ref, vb_ref)
        lane = _iota((BLK, BLK), 1)
        qh = _head_masks(q_ref[...] * SB_SCALE, lane)
        row1 = qi * BLK + _iota((BLK, BLK), 0)

        def tile(h, ks, vs, valid, tri, run):
            z = _dot(qh[h], ks, NT)
            _, lk, lb = _sb_tile(z, valid)
            after = _suffix_sum(lk, tri)
            if run is not None:
                after = after + run
            a = jnp.where(valid, jnp.exp(lb + after), 0.0)
            a_hi = a.astype(BF16)
            return (jnp.sum(lk, axis=1, keepdims=True), _dot(a_hi, vs, NN),
                    _dot((a - a_hi.astype(F32)).astype(BF16), vs, NN))

        off = pl.multiple_of(qi * BLK, BLK)
        ks2 = kb_ref[pl.ds(off, 2 * BLK), :]
        vs2 = vb_ref[pl.ds(off, 2 * BLK), :]
        col2 = (qi - 1) * BLK + _iota((BLK, 2 * BLK), 1)
        row2 = qi * BLK + _iota((BLK, 2 * BLK), 0)
        valid2 = jnp.logical_and(col2 < row2, col2 >= PAD)
        tri2 = (_iota((2 * BLK, 2 * BLK), 0) > _iota((2 * BLK, 2 * BLK), 1)).astype(BF16)
        init = []
        for h in range(2):
            init += list(tile(h, ks2, vs2, valid2, tri2, None))
        tri1 = (_iota((BLK, BLK), 0) > lane).astype(BF16)

        def step(kblk, carry):
            o1 = pl.multiple_of((kblk + 1) * BLK, BLK)
            ks = kb_ref[pl.ds(o1, BLK), :]
            vs = vb_ref[pl.ds(o1, BLK), :]
            col = kblk * BLK + lane
            valid = jnp.logical_and(col < row1, col >= PAD)
            new = []
            for h in range(2):
                run, acc, rest = carry[3 * h], carry[3 * h + 1], carry[3 * h + 2]
                d_run, d_acc, d_rest = tile(h, ks, vs, valid, tri1, run)
                new += [run + d_run, acc + d_acc, rest + d_rest]
            return tuple(new)

        res = _sweep(qi - 2, step, init, (0, 3))
        o = jnp.where(lane < HEAD_DIM, res[1], res[4])
        o_ref[...] = o
        ox_ref[...] = o + jnp.where(lane < HEAD_DIM, res[2], res[5])

    act = jax.ShapeDtypeStruct((LP, 1024), F32)
    return pl.pallas_call(
        body, name="sb_attn_fwd", out_shape=(act, act),
        grid=(N_HEADS // 2, NBLK),
        in_specs=[pl.BlockSpec((BLK, BLK), lambda hp, qi: (qi, COL_Q // BLK + hp)),
                  pl.BlockSpec((LP, BLK), lambda hp, qi: (0, COL_K // BLK + hp)),
                  pl.BlockSpec((LP, BLK), lambda hp, qi: (0, COL_V // BLK + hp))],
        out_specs=(pl.BlockSpec((BLK, BLK), lambda hp, qi: (qi, hp)),
                   pl.BlockSpec((BLK, BLK), lambda hp, qi: (qi, hp))),
        scratch_shapes=[pltpu.VMEM((LP + BLK, BLK), BF16), pltpu.VMEM((LP + BLK, BLK), BF16)],
        compiler_params=_params(("arbitrary", "arbitrary")),
    )(proj, proj, proj)


def _sb_attention_bwd(proj, o_sb, do_sb, chip_sums):
    def body(q_ref, k_ref, v_ref, o_ref, do_ref, src_ref, dq_ref, dk_ref, dv_ref, out_ref,
             kb_ref, vb_ref, dka_ref, dva_ref, send_sems, recv_sems, loc_sem):
        hp, qi = pl.program_id(0), pl.program_id(1)
        _stage_kv(qi, k_ref, v_ref, kb_ref, vb_ref)

        @pl.when(jnp.logical_and(hp == 0, qi == 0))
        def _():
            for cp in _chip_exchange_copies(src_ref, out_ref, send_sems, recv_sems, loc_sem):
                cp.start()

        @pl.when(qi == 0)
        def _():
            dka_ref[...] = jnp.zeros_like(dka_ref)
            dva_ref[...] = jnp.zeros_like(dva_ref)

        lane = _iota((BLK, BLK), 1)
        head0 = lane < HEAD_DIM
        qh = _head_masks(q_ref[...] * SB_SCALE, lane)
        do = do_ref[...]
        doh = _head_masks(do, lane)
        prod = do.astype(BF16).astype(F32) * o_ref[...]
        dsum = (jnp.sum(jnp.where(head0, prod, 0.0), axis=1, keepdims=True),
                jnp.sum(jnp.where(head0, 0.0, prod), axis=1, keepdims=True))
        row1 = qi * BLK + _iota((BLK, BLK), 0)

        def tile(h, ks, vs, valid, tri, run, erun):
            z = _dot(qh[h], ks, NT)
            sig, lk, lb = _sb_tile(z, valid)
            after = _suffix_sum(lk, tri)
            if run is not None:
                after = after + run
            a = jnp.where(valid, jnp.exp(lb + after), 0.0)
            e = a * _dot(doh[h], vs, NT)
            esuf = _suffix_sum(e, tri)
            if erun is not None:
                esuf = esuf + erun
            dz = jnp.where(valid, e - sig * (dsum[h] - esuf), 0.0).astype(BF16)
            return (jnp.sum(lk, axis=1, keepdims=True), jnp.sum(e, axis=1, keepdims=True),
                    _dot(dz, ks, NN), _dot(dz, qh[h], TN), _dot(a.astype(BF16), doh[h], TN))

        off = pl.multiple_of(qi * BLK, BLK)
        ks2 = kb_ref[pl.ds(off, 2 * BLK), :]
        vs2 = vb_ref[pl.ds(off, 2 * BLK), :]
        col2 = (qi - 1) * BLK + _iota((BLK, 2 * BLK), 1)
        row2 = qi * BLK + _iota((BLK, 2 * BLK), 0)
        valid2 = jnp.logical_and(col2 < row2, col2 >= PAD)
        tri2 = (_iota((2 * BLK, 2 * BLK), 0) > _iota((2 * BLK, 2 * BLK), 1)).astype(BF16)
        init = []
        for h in range(2):
            d_run, d_erun, d_q, d_k, d_v = tile(h, ks2, vs2, valid2, tri2, None, None)
            init += [d_run, d_erun, d_q]
            dk2 = d_k if h == 0 else dk2 + d_k
            dv2 = d_v if h == 0 else dv2 + d_v
        dka_ref[pl.ds(off, 2 * BLK), :] += dk2
        dva_ref[pl.ds(off, 2 * BLK), :] += dv2
        tri1 = (_iota((BLK, BLK), 0) > lane).astype(BF16)

        def step(kblk, carry):
            o1 = pl.multiple_of((kblk + 1) * BLK, BLK)
            ks = kb_ref[pl.ds(o1, BLK), :]
            vs = vb_ref[pl.ds(o1, BLK), :]
            col = kblk * BLK + lane
            valid = jnp.logical_and(col < row1, col >= PAD)
            new = []
            dk1 = jnp.zeros((BLK, BLK), F32)
            dv1 = jnp.zeros((BLK, BLK), F32)
            for h in range(2):
                run, erun, dq = carry[3 * h], carry[3 * h + 1], carry[3 * h + 2]
                d_run, d_erun, d_q, d_k, d_v = tile(h, ks, vs, valid, tri1, run, erun)
                new += [run + d_run, erun + d_erun, dq + d_q]
                dk1 = dk1 + d_k
                dv1 = dv1 + d_v
            dka_ref[pl.ds(o1, BLK), :] += dk1
            dva_ref[pl.ds(o1, BLK), :] += dv1
            return tuple(new)

        res = _sweep(qi - 2, step, init, (0, 3))
        dq_ref[...] = jnp.where(head0, res[2], res[5]) * SB_SCALE

        @pl.when(qi == NBLK - 1)
        def _():
            dk_ref[...] = dka_ref[BLK:, :]
            dv_ref[...] = dva_ref[BLK:, :]

        @pl.when(jnp.logical_and(hp == N_HEADS // 2 - 1, qi == NBLK - 1))
        def _():
            for cp in _chip_exchange_copies(src_ref, out_ref, send_sems, recv_sems, loc_sem):
                cp.wait()

    act = jax.ShapeDtypeStruct((LP, 1024), F32)
    blk = lambda col: pl.BlockSpec((BLK, BLK), lambda hp, qi: (qi, col + hp))
    whole = lambda col: pl.BlockSpec((LP, BLK), lambda hp, qi: (0, col + hp))
    return pl.pallas_call(
        body, name="sb_attn_bwd",
        out_shape=(act, act, act, jax.ShapeDtypeStruct(chip_sums.shape, chip_sums.dtype)),
        grid=(N_HEADS // 2, NBLK),
        in_specs=[blk(COL_Q // BLK), whole(COL_K // BLK), whole(COL_V // BLK), blk(0), blk(0), _ANY],
        out_specs=(blk(0), whole(0), whole(0), _ANY),
        scratch_shapes=[pltpu.VMEM((LP + BLK, BLK), BF16), pltpu.VMEM((LP + BLK, BLK), BF16),
                        pltpu.VMEM((LP + BLK, BLK), F32), pltpu.VMEM((LP + BLK, BLK), F32)] + _CHIP_EXCHANGE_SEMS,
        compiler_params=_params(("arbitrary", "arbitrary")),
    )(proj, proj, proj, o_sb, do_sb, chip_sums)


def _conv_pre(x, w_ref, b_ref):
    acc = b_ref[...] + w_ref[3:4, :] * x
    for k in range(3):
        acc = acc + w_ref[k:k + 1, :] * pltpu.roll(x, 3 - k, 0)
    return acc


def _conv_fwd(proj, conv_w, conv_b):
    def body(x_ref, w_ref, b_ref, o_ref):
        xc = _conv_pre(x_ref[...], w_ref, b_ref)
        o_ref[...] = xc * _sigmoid(xc)

    nb = XBC_W // BLK
    return pl.pallas_call(
        body, name="conv_fwd", out_shape=jax.ShapeDtypeStruct((LP, XBC_W), F32), grid=(nb,),
        in_specs=[pl.BlockSpec((LP, BLK), lambda j: (0, COL_XBC // BLK + j)),
                  pl.BlockSpec((4, BLK), lambda j: (0, j)), pl.BlockSpec((1, BLK), lambda j: (0, j))],
        out_specs=pl.BlockSpec((LP, BLK), lambda j: (0, j)),
        compiler_params=_params(("parallel",)),
    )(proj, conv_w, conv_b)


def _conv_bwd(dxa, proj, conv_w, conv_b):
    def body(d_ref, x_ref, w_ref, b_ref, dx_ref, dw_ref, db_ref):
        x = x_ref[...]
        xc = _conv_pre(x, w_ref, b_ref)
        s = _sigmoid(xc)
        live = _iota((LP, BLK), 0) >= PAD
        dxc = jnp.where(live, d_ref[...] * (s * (1.0 + xc * (1.0 - s))), 0.0)
        db_ref[...] = jnp.sum(dxc, axis=0, keepdims=True)
        dx = w_ref[3:4, :] * dxc
        dw_ref[3:4, :] = jnp.sum(dxc * x, axis=0, keepdims=True)
        for k in range(3):
            dw_ref[k:k + 1, :] = jnp.sum(dxc * pltpu.roll(x, 3 - k, 0), axis=0, keepdims=True)
            dx = dx + w_ref[k:k + 1, :] * pltpu.roll(dxc, LP - (3 - k), 0)
        dx_ref[...] = dx

    nb = XBC_W // BLK
    col = lambda j: (0, j)
    return pl.pallas_call(
        body, name="conv_bwd",
        out_shape=(jax.ShapeDtypeStruct((LP, XBC_W), F32), jax.ShapeDtypeStruct((4, XBC_W), F32),
                   jax.ShapeDtypeStruct((1, XBC_W), F32)),
        grid=(nb,),
        in_specs=[pl.BlockSpec((LP, BLK), col), pl.BlockSpec((LP, BLK), lambda j: (0, COL_XBC // BLK + j)),
                  pl.BlockSpec((4, BLK), col), pl.BlockSpec((1, BLK), col)],
        out_specs=(pl.BlockSpec((LP, BLK), col), pl.BlockSpec((4, BLK), col), pl.BlockSpec((1, BLK), col)),
        compiler_params=_params(("parallel",)),
    )(dxa, proj, conv_w, conv_b)


def _ssd_prelude(c, dt_ref, dtt_ref, dtb_ref, dtbt_ref, alog_ref, alogt_ref):
    live = jnp.logical_or(c > 0, _iota((BLK, N_HEADS), 0) >= PAD)
    live_t = jnp.logical_or(c > 0, _iota((N_HEADS, BLK), 1) >= PAD)
    pre = dt_ref[...] + dtb_ref[...]
    pre_t = dtt_ref[...] + dtbt_ref[...]
    dt = jnp.where(live, _softplus(pre), 0.0)
    dt_t = jnp.where(live_t, _softplus(pre_t), 0.0)
    a = -jnp.exp(alog_ref[...])
    a_t = -jnp.exp(alogt_ref[...])
    li = _iota((BLK, BLK), 0)
    si = _iota((BLK, BLK), 1)
    lower = (si <= li).astype(BF16)
    upper = (li <= si).astype(BF16)
    acum = _dot_exact_x(lower, dt * a, NN)
    acum_t = _dot_x_exact(dt_t * a_t, upper, NN)
    return live, pre, dt, a, a_t, acum, acum_t


def _head_expand():
    return (_iota((N_HEADS, SSD_W), 1) // HEAD_DIM == _iota((N_HEADS, SSD_W), 0)).astype(BF16)


def _head_reduce_mat():
    return (_iota((SSD_W, N_HEADS), 0) // HEAD_DIM == _iota((SSD_W, N_HEADS), 1)).astype(BF16)


def _decay_mat(acum, acum_t, h, causal):
    seg = jnp.minimum(acum[:, h:h + 1] - acum_t[h:h + 1, :], 0.0)
    return jnp.where(causal, jnp.exp(seg), 0.0)


def _ssd_specs():
    chunk = lambda width, col=0: pl.BlockSpec((BLK, width), lambda c: (c, col))
    return chunk


def _ssd_fwd(xa, dt_raw, dt_raw_t, dt_bias, dt_bias_t, a_log, a_log_t, d_exp):
    def body(x_ref, b_ref, c_ref, dt_ref, dtt_ref, dtb_ref, dtbt_ref, alog_ref, alogt_ref, dexp_ref,
             y_ref, hs_ref, state_ref):
        c = pl.program_id(0)

        @pl.when(c == 0)
        def _():
            state_ref[...] = jnp.zeros_like(state_ref)

        _, _, dt, _, _, acum, acum_t = _ssd_prelude(c, dt_ref, dtt_ref, dtb_ref, dtbt_ref, alog_ref, alogt_ref)
        expand = _head_expand()
        x = x_ref[...]
        xdt = x * _dot_x_exact(dt, expand)
        exp_a = _dot_x_exact(jnp.exp(acum), expand)
        to_end = _dot_x_exact(jnp.exp(acum[BLK - 1:BLK, :] - acum), expand)
        xdt_b = xdt.astype(BF16)
        xd_b = (xdt * to_end).astype(BF16)
        chunk_decay = jnp.exp(acum_t[:, BLK - 1:BLK])
        hs_ref[0] = state_ref[...]
        lane = _iota((BLK, BLK), 1)
        causal = _iota((BLK, BLK), 0) >= lane
        gw = HEADS_PER_GROUP * HEAD_DIM
        for g in range(N_GROUPS):
            bg = b_ref[:, g * N_STATE:(g + 1) * N_STATE].astype(BF16)
            cg = c_ref[:, g * N_STATE:(g + 1) * N_STATE].astype(BF16)
            cb = _dot(cg, bg, NT)
            hg = state_ref[g * gw:(g + 1) * gw, :]
            ch = _dot(cg, hg.astype(BF16), NT)
            st = _dot(xd_b[:, g * gw:(g + 1) * gw], bg, TN)
            for p in range(HEADS_PER_GROUP // 2):
                h0 = g * HEADS_PER_GROUP + 2 * p
                lo = h0 * HEAD_DIM
                xp = xdt_b[:, lo:lo + BLK]
                w0 = (cb * _decay_mat(acum, acum_t, h0, causal)).astype(BF16)
                w1 = (cb * _decay_mat(acum, acum_t, h0 + 1, causal)).astype(BF16)
                yd = jnp.where(lane < HEAD_DIM, _dot(w0, xp), _dot(w1, xp))
                y_ref[:, lo:lo + BLK] = (yd + ch[:, lo - g * gw:lo - g * gw + BLK] * exp_a[:, lo:lo + BLK]
                                         + x[:, lo:lo + BLK] * dexp_ref[:, lo:lo + BLK])
            for r in range(HEADS_PER_GROUP):
                h = g * HEADS_PER_GROUP + r
                state_ref[h * HEAD_DIM:(h + 1) * HEAD_DIM, :] = (
                    hg[r * HEAD_DIM:(r + 1) * HEAD_DIM, :] * chunk_decay[h:h + 1, :]
                    + st[r * HEAD_DIM:(r + 1) * HEAD_DIM, :])

    chunk = _ssd_specs()
    return pl.pallas_call(
        body, name="ssd_fwd",
        out_shape=(jax.ShapeDtypeStruct((LP, SSD_W), F32), jax.ShapeDtypeStruct((NBLK, SSD_W, N_STATE), F32)),
        grid=(NBLK,),
        in_specs=[chunk(SSD_W), chunk(256, 4), chunk(256, 5), chunk(N_HEADS),
                  pl.BlockSpec((N_HEADS, BLK), lambda c: (0, c)), _const_spec((1, N_HEADS)),
                  _const_spec((N_HEADS, 1)), _const_spec((1, N_HEADS)), _const_spec((N_HEADS, 1)),
                  _const_spec((1, SSD_W))],
        out_specs=(chunk(SSD_W), pl.BlockSpec((1, SSD_W, N_STATE), lambda c: (c, 0, 0))),
        scratch_shapes=[pltpu.VMEM((SSD_W, N_STATE), F32)],
        compiler_params=_params(("arbitrary",)),
    )(xa, xa, xa, dt_raw, dt_raw_t, dt_bias, dt_bias_t, a_log, a_log_t, d_exp)


def _ssd_bwd(xa, dt_raw, dt_raw_t, dt_bias, dt_bias_t, a_log, a_log_t, d_exp, hstart, dy):
    def body(x_ref, b_ref, c_ref, dt_ref, dtt_ref, dtb_ref, dtbt_ref, alog_ref, alogt_ref, dexp_ref,
             hs_ref, dy_ref, dx_ref, db_ref, dc_ref, ddt_ref, dbias_ref, dalog_ref, dd_ref, dstate_ref):
        step = pl.program_id(0)
        c = NBLK - 1 - step

        @pl.when(step == 0)
        def _():
            dstate_ref[...] = jnp.zeros_like(dstate_ref)
            dbias_ref[...] = jnp.zeros_like(dbias_ref)
            dalog_ref[...] = jnp.zeros_like(dalog_ref)
            dd_ref[...] = jnp.zeros_like(dd_ref)

        live, pre, dt, a, a_t, acum, acum_t = _ssd_prelude(c, dt_ref, dtt_ref, dtb_ref, dtbt_ref,
                                                           alog_ref, alogt_ref)
        expand = _head_expand()
        reduce_m = _head_reduce_mat()
        x = x_ref[...]
        dyv = dy_ref[...]
        dt_e = _dot_x_exact(dt, expand)
        xdt = x * dt_e
        exp_acum = jnp.exp(acum)
        exp_a = _dot_x_exact(exp_acum, expand)
        dte = jnp.exp(acum[BLK - 1:BLK, :] - acum)
        to_end = _dot_x_exact(dte, expand)
        xdt_b = xdt.astype(BF16)
        xd_b = (xdt * to_end).astype(BF16)
        chunk_decay = jnp.exp(acum_t[:, BLK - 1:BLK])
        lane = _iota((BLK, BLK), 1)
        head0 = lane < HEAD_DIM
        causal = _iota((BLK, BLK), 0) >= lane
        gw = HEADS_PER_GROUP * HEAD_DIM
        dm = dyv * exp_a
        dm_b = dm.astype(BF16)
        onehot = lambda h: (_iota((1, N_HEADS), 1) == h).astype(F32)
        onehot_t = lambda h: (_iota((N_HEADS, 1), 0) == h).astype(F32)
        dacum = jnp.zeros((BLK, N_HEADS), F32)
        dacum_t = jnp.zeros((N_HEADS, BLK), F32)
        ddt_acc = jnp.zeros((BLK, N_HEADS), F32)
        ddte_acc = jnp.zeros((BLK, N_HEADS), F32)
        dexpa_acc = jnp.zeros((BLK, N_HEADS), F32)
        dskip_acc = jnp.zeros((BLK, N_HEADS), F32)
        head_sum = expand
        for g in range(N_GROUPS):
            bg = b_ref[:, g * N_STATE:(g + 1) * N_STATE].astype(BF16)
            cg = c_ref[:, g * N_STATE:(g + 1) * N_STATE].astype(BF16)
            cb = _dot(cg, bg, NT)
            hg = hs_ref[0, g * gw:(g + 1) * gw, :]
            hg_b = hg.astype(BF16)
            dhe = dstate_ref[g * gw:(g + 1) * gw, :]
            dhe_b = dhe.astype(BF16)
            ch = _dot(cg, hg_b, NT)
            dcg = _dot(dm_b[:, g * gw:(g + 1) * gw], hg_b, NN)
            dhs = _dot(dm_b[:, g * gw:(g + 1) * gw], cg, TN)
            dxd = _dot(bg, dhe_b, NT)
            dbg = _dot(xd_b[:, g * gw:(g + 1) * gw], dhe_b, NN)
            dcb = jnp.zeros((BLK, BLK), F32)
            for p in range(HEADS_PER_GROUP // 2):
                h0 = g * HEADS_PER_GROUP + 2 * p
                lo = h0 * HEAD_DIM
                xp = xdt_b[:, lo:lo + BLK]
                dyp = dyv[:, lo:lo + BLK]
                dyh = (jnp.where(head0, dyp, 0.0).astype(BF16), jnp.where(head0, 0.0, dyp).astype(BF16))
                dxdt_p = jnp.zeros((BLK, BLK), F32)
                for q in range(2):
                    h = h0 + q
                    dec = _decay_mat(acum, acum_t, h, causal)
                    w = cb * dec
                    dw = _dot(dyh[q], xp, NT)
                    t = dw * w
                    dacum = dacum + jnp.sum(t, axis=1, keepdims=True) * onehot(h)
                    dacum_t = dacum_t - jnp.sum(t, axis=0, keepdims=True) * onehot_t(h)
                    dcb = dcb + dw * dec
                    dxdt_p = dxdt_p + _dot(w.astype(BF16), dyh[q], TN)
                sl = slice(lo, lo + BLK)
                gl = slice(lo - g * gw, lo - g * gw + BLK)
                dxdt_p = dxdt_p + dxd[:, gl] * to_end[:, sl]
                dx_ref[:, sl] = dyp * dexp_ref[:, sl] + dxdt_p * dt_e[:, sl]
                red = reduce_m[lo:lo + BLK, :]
                ddt_acc = ddt_acc + _dot_x_exact(dxdt_p * x[:, sl], red)
                ddte_acc = ddte_acc + _dot_x_exact(dxd[:, gl] * xdt[:, sl], red)
                dexpa_acc = dexpa_acc + _dot_x_exact(dyp * ch[:, gl], red)
                dskip_acc = dskip_acc + _dot_x_exact(dyp * x[:, sl], red)
            dcb_b = dcb.astype(BF16)
            dc_ref[:, g * N_STATE:(g + 1) * N_STATE] = dcg + _dot(dcb_b, bg, NN)
            db_ref[:, g * N_STATE:(g + 1) * N_STATE] = dbg + _dot(dcb_b, cg, TN)
            prod = dhe * hg
            per_head = jnp.sum(_dot_exact_x(head_sum[:, g * gw:(g + 1) * gw], prod, NN), axis=1, keepdims=True)
            dacum_t = dacum_t + (per_head * chunk_decay) * (_iota((1, BLK), 1) == BLK - 1).astype(F32)
            for r in range(HEADS_PER_GROUP):
                h = g * HEADS_PER_GROUP + r
                rows = slice(h * HEAD_DIM, (h + 1) * HEAD_DIM)
                dstate_ref[rows, :] = (dhs[r * HEAD_DIM:(r + 1) * HEAD_DIM, :]
                                       + dhe[r * HEAD_DIM:(r + 1) * HEAD_DIM, :] * chunk_decay[h:h + 1, :])
        dacum = dacum + dexpa_acc * exp_acum - ddte_acc * dte
        last_row = (_iota((BLK, 1), 0) == BLK - 1).astype(F32)
        dacum = dacum + last_row * jnp.sum(ddte_acc * dte, axis=0, keepdims=True)
        li = _iota((BLK, BLK), 0)
        si = _iota((BLK, BLK), 1)
        upper = (li <= si).astype(BF16)
        lower = (si <= li).astype(BF16)
        dda = _dot_exact_x(upper, dacum, NN)
        dda_t = _dot_x_exact(dacum_t, lower, NN)
        eye = (_iota((N_HEADS, N_HEADS), 0) == _iota((N_HEADS, N_HEADS), 1)).astype(BF16)
        dda = dda + _dot_x_exact_tn(dda_t, eye)
        ddt = ddt_acc + dda * a
        dalog_ref[...] += jnp.sum(dda * dt, axis=0, keepdims=True) * a
        dd_ref[...] += jnp.sum(dskip_acc, axis=0, keepdims=True)
        ddt_raw = jnp.where(live, ddt * _sigmoid(pre), 0.0)
        ddt_ref[...] = ddt_raw
        dbias_ref[...] += jnp.sum(ddt_raw, axis=0, keepdims=True)

    rev = lambda width, col=0: pl.BlockSpec((BLK, width), lambda s: (NBLK - 1 - s, col))
    vec = jax.ShapeDtypeStruct((1, N_HEADS), F32)
    return pl.pallas_call(
        body, name="ssd_bwd",
        out_shape=(jax.ShapeDtypeStruct((LP, SSD_W), F32), jax.ShapeDtypeStruct((LP, 256), F32),
                   jax.ShapeDtypeStruct((LP, 256), F32), jax.ShapeDtypeStruct((LP, N_HEADS), F32), vec, vec, vec),
        grid=(NBLK,),
        in_specs=[rev(SSD_W), rev(256, 4), rev(256, 5), rev(N_HEADS),
                  pl.BlockSpec((N_HEADS, BLK), lambda s: (0, NBLK - 1 - s)), _const_spec((1, N_HEADS)),
                  _const_spec((N_HEADS, 1)), _const_spec((1, N_HEADS)), _const_spec((N_HEADS, 1)),
                  _const_spec((1, SSD_W)),
                  pl.BlockSpec((1, SSD_W, N_STATE), lambda s: (NBLK - 1 - s, 0, 0)), rev(SSD_W)],
        out_specs=(rev(SSD_W), rev(256), rev(256), rev(N_HEADS), _const_spec((1, N_HEADS)),
                   _const_spec((1, N_HEADS)), _const_spec((1, N_HEADS))),
        scratch_shapes=[pltpu.VMEM((SSD_W, N_STATE), F32)],
        compiler_params=_params(("arbitrary",)),
    )(xa, xa, xa, dt_raw, dt_raw_t, dt_bias, dt_bias_t, a_log, a_log_t, d_exp, hstart, dy)


def _dot_x_exact_tn(x_t, eye):
    out = None
    for p in _split(x_t, 3):
        t = _dot(p, eye, TN)
        out = t if out is None else out + t
    return out


def _adamw(name, parts, w, m, v, rows):
    r_all, cols = w.shape
    assert r_all % rows == 0
    c1 = 1.0 / (1.0 - ADAM_B1 ** ADAM_STEP)
    c2 = 1.0 / (1.0 - ADAM_B2 ** ADAM_STEP)

    def body(p_ref, w_ref, m_ref, v_ref, g_ref, d_ref, mo_ref, vo_ref):
        g = p_ref[0].astype(F32)
        for j in range(1, parts.shape[0]):
            g = g + p_ref[j].astype(F32)
        mn = ADAM_B1 * m_ref[...] + (1.0 - ADAM_B1) * g
        vn = ADAM_B2 * v_ref[...] + (1.0 - ADAM_B2) * (g * g)
        g_ref[...] = g
        mo_ref[...] = mn
        vo_ref[...] = vn
        d_ref[...] = -ADAM_LR * ((mn * c1) / (jnp.sqrt(vn * c2) + ADAM_EPS) + ADAM_WD * w_ref[...])

    spec = pl.BlockSpec((rows, cols), lambda i: (i, 0))
    shp = jax.ShapeDtypeStruct((r_all, cols), F32)
    return pl.pallas_call(
        body, name=name, out_shape=(shp, shp, shp, shp), grid=(r_all // rows,),
        in_specs=[pl.BlockSpec((parts.shape[0], rows, cols), lambda i: (0, i, 0)), spec, spec, spec],
        out_specs=(spec, spec, spec, spec),
        compiler_params=_params(("parallel",)),
    )(parts, w, m, v)


_VECTORS = (("norm_w", 1024, 0), ("conv_b", 1536, 8), ("dt_bias", 16, 20), ("a_log", 16, 21), ("d_skip", 16, 22),
            ("sb_norm_w", 1024, 24), ("ssd_norm_w", 1024, 32), ("final_norm_w", 1024, 40))
_LOSS_ROW = 23
_CONVW_ROW = 48
_META_ROW = 96
_PACK_ROWS = 224
_SMALL_ORDER = tuple(name for name, _, _ in _VECTORS) + ("conv_w", "meta_tokens")


def _pack_small_grads(vectors, loss_row, d_convw, dh):
    def body(*refs):
        vec_refs, (loss_ref, cw_ref, dh_ref, out_ref) = refs[:len(_VECTORS)], refs[len(_VECTORS):]
        out_ref[...] = jnp.zeros_like(out_ref)
        for (_, width, row), ref in zip(_VECTORS, vec_refs):
            if width < BLK:
                out_ref[row:row + 1, :width] = ref[...]
            else:
                for t in range(width // BLK):
                    out_ref[row + t:row + t + 1, :] = ref[:, t * BLK:(t + 1) * BLK]
        out_ref[_LOSS_ROW:_LOSS_ROW + 1, :] = loss_ref[...]
        for k in range(4):
            for t in range(XBC_W // BLK):
                r = _CONVW_ROW + k * (XBC_W // BLK) + t
                out_ref[r:r + 1, :] = cw_ref[k:k + 1, t * BLK:(t + 1) * BLK]
        for i in range(N_META):
            for t in range(D_MODEL // BLK):
                r = _META_ROW + i * (D_MODEL // BLK) + t
                out_ref[r:r + 1, :] = dh_ref[i:i + 1, t * BLK:(t + 1) * BLK]

    full = lambda a: pl.BlockSpec(a.shape, lambda i: tuple(0 for _ in a.shape))
    return pl.pallas_call(
        body, name="pack_small_grads", out_shape=jax.ShapeDtypeStruct((_PACK_ROWS, BLK), F32), grid=(1,),
        in_specs=[full(v) for v in vectors] + [full(loss_row), full(d_convw),
                                               pl.BlockSpec((N_META, D_MODEL), lambda i: (PAD // N_META, 0))],
        out_specs=pl.BlockSpec((_PACK_ROWS, BLK), lambda i: (0, 0)),
        compiler_params=_params(("arbitrary",)),
    )(*vectors, loss_row, d_convw, dh)


def _add(name, a, b):
    def body(a_ref, b_ref, o_ref):
        o_ref[...] = a_ref[...] + b_ref[...]

    spec = pl.BlockSpec(a.shape, lambda i: (0, 0))
    return pl.pallas_call(body, name=name, out_shape=jax.ShapeDtypeStruct(a.shape, a.dtype), grid=(1,),
                          in_specs=[spec, spec], out_specs=spec, compiler_params=_params(("arbitrary",)))(a, b)


def _adamw_small(chip_sums, weights, moms, vels):
    c1 = 1.0 / (1.0 - ADAM_B1 ** ADAM_STEP)
    c2 = 1.0 / (1.0 - ADAM_B2 ** ADAM_STEP)
    n = len(_SMALL_ORDER)

    def body(*refs):
        p_ref = refs[0]
        w_refs, m_refs, v_refs = refs[1:1 + n], refs[1 + n:1 + 2 * n], refs[1 + 2 * n:1 + 3 * n]
        outs = refs[1 + 3 * n:1 + 7 * n]
        loss_ref, g_ref, cw_ref, cws_ref, mt_ref = refs[1 + 7 * n:]
        x, y, c = _mesh_position()
        me = 4 * x + 2 * y + c
        g = p_ref[0]
        for j in range(1, 4):
            g = g + p_ref[j]
        g_ref[...] = g
        loss_ref[...] = g_ref[_LOSS_ROW:_LOSS_ROW + 1, :]

        def update(idx, grad):
            go_ref, d_ref, mo_ref, vo_ref = outs[4 * idx:4 * idx + 4]
            mn = ADAM_B1 * m_refs[idx][...] + (1.0 - ADAM_B1) * grad
            vn = ADAM_B2 * v_refs[idx][...] + (1.0 - ADAM_B2) * (grad * grad)
            go_ref[...] = grad
            mo_ref[...] = mn
            vo_ref[...] = vn
            d_ref[...] = -ADAM_LR * ((mn * c1) / (jnp.sqrt(vn * c2) + ADAM_EPS) + ADAM_WD * w_refs[idx][...])

        for idx, (_, width, row) in enumerate(_VECTORS):
            go_ref = outs[4 * idx]
            if width < BLK:
                grad = g_ref[row:row + 1, :width]
            else:
                for t in range(width // BLK):
                    go_ref[:, t * BLK:(t + 1) * BLK] = g_ref[row + t:row + t + 1, :]
                grad = go_ref[...]
            update(idx, grad)
        cw_ref[...] = jnp.zeros_like(cw_ref)
        for k in range(4):
            for t in range(XBC_W // BLK):
                r = _CONVW_ROW + k * (XBC_W // BLK) + t
                cw_ref[k:k + 1, t * BLK:(t + 1) * BLK] = g_ref[r:r + 1, :]
        for i in range(N_META):
            for t in range(D_MODEL // BLK):
                r = _META_ROW + i * (D_MODEL // BLK) + t
                mt_ref[i:i + 1, t * BLK:(t + 1) * BLK] = g_ref[r:r + 1, :]
        width_cw = XBC_W // N_DEV
        pick_cw = (_iota((XBC_W, width_cw), 0) == me * width_cw + _iota((XBC_W, width_cw), 1)).astype(BF16)
        cws_ref[...] = _dot_x_exact(cw_ref[...], pick_cw)
        update(n - 2, cws_ref[0:4, :])
        pick_mt = (_iota((D_MODEL, BLK), 0) == me * BLK + _iota((D_MODEL, BLK), 1)).astype(BF16)
        update(n - 1, _dot_x_exact(mt_ref[...], pick_mt))

    full = lambda a: pl.BlockSpec(a.shape, lambda i: tuple(0 for _ in a.shape))
    params = list(weights) + list(moms) + list(vels)
    out_shape, out_specs = [], []
    for w in weights:
        for _ in range(4):
            out_shape.append(jax.ShapeDtypeStruct(w.shape, F32))
            out_specs.append(full(w))
    out_shape.append(jax.ShapeDtypeStruct((1, BLK), F32))
    out_specs.append(pl.BlockSpec((1, BLK), lambda i: (0, 0)))
    return pl.pallas_call(
        body, name="adamw_small", out_shape=tuple(out_shape), grid=(1,),
        in_specs=[full(chip_sums)] + [full(a) for a in params], out_specs=tuple(out_specs),
        scratch_shapes=[pltpu.VMEM((_PACK_ROWS, BLK), F32), pltpu.VMEM((8, XBC_W), F32),
                        pltpu.VMEM((8, XBC_W // N_DEV), F32), pltpu.VMEM((N_META, D_MODEL), F32)],
        compiler_params=_params(("arbitrary",)),
    )(chip_sums, *params)


def kernel(x, meta_tokens, norm_w, w_in, conv_w, conv_b, dt_bias, a_log, d_skip, sb_norm_w, ssd_norm_w, w_out, final_norm_w, loss_target, m_meta_tokens, m_norm_w, m_w_in, m_conv_w, m_conv_b, m_dt_bias, m_a_log, m_d_skip, m_sb_norm_w, m_ssd_norm_w, m_w_out, m_final_norm_w, v_meta_tokens, v_norm_w, v_w_in, v_conv_w, v_conv_b, v_dt_bias, v_a_log, v_d_skip, v_sb_norm_w, v_ssd_norm_w, v_w_out, v_final_norm_w):
    small_src = jnp.concatenate([conv_w[0].reshape(6, BLK), meta_tokens, jnp.zeros((2, BLK), F32)], axis=0)
    small_g, w_out_g, w_in_g = _gather_weights([small_src, w_out[0].astype(BF16), w_in[0].astype(BF16)])
    w_in_full = w_in_g.transpose(1, 0, 2).reshape(D_MODEL, D_IN)
    w_main = w_in_full[:, :D_MAIN]
    w_dt = w_in_full[:, D_MAIN:]
    w_out_full = w_out_g.reshape(2 * SSD_W, D_MODEL)
    conv_w_full = small_g[:, :6].reshape(N_DEV, 4, 192).transpose(1, 0, 2).reshape(4, XBC_W)
    meta_full = small_g[:, 6:6 + N_META].transpose(1, 0, 2).reshape(N_META, D_MODEL)
    h_pad = jnp.concatenate([jnp.zeros((PAD, D_MODEL), F32), meta_full, x[0]], axis=0)
    dt_bias_t = dt_bias.reshape(N_HEADS, 1)
    a_log_t = a_log.reshape(N_HEADS, 1)
    d_exp = jnp.repeat(d_skip, HEAD_DIM, axis=1)
    fnw = final_norm_w.reshape(1, D_MODEL)

    u, dt_raw, dt_raw_t = _prenorm(h_pad, norm_w, w_dt, w_dt.T)
    proj = _matmul("in_proj", u, w_main, "nn", 1088, 512, D_MODEL)
    o_sb, o_sb_exact = _sb_attention_fwd(proj)
    xa = _conv_fwd(proj, conv_w_full, conv_b)
    o_ssd, hstart = _ssd_fwd(xa, dt_raw, dt_raw_t, dt_bias, dt_bias_t, a_log, a_log_t, d_exp)
    ycat = _ycat(o_sb, proj, o_ssd, sb_norm_w, ssd_norm_w)
    yo = _matmul("out_proj", ycat, w_out_full, "nn", 1088, 512, 2 * SSD_W)
    dh2, dh2_b, loss_row, d_fnw = _loss_head(h_pad, yo, fnw, loss_target[0])

    g_w_out = _matmul("d_w_out", ycat, dh2_b, "tn", 512, 512, LP, BF16)
    g_w_out = g_w_out.reshape(4, 2, 256, D_MODEL).transpose(1, 0, 2, 3)
    chip_w_out = _pair_sum("pair_sum_w_out", g_w_out, _swap_with_sibling("swap_w_out", g_w_out), 256)
    dycat = _matmul("d_ycat", dh2_b, w_out_full, "nt", 1088, 512, D_MODEL)
    do_sb, dg, do_ssd, dz, d_sbw, d_ssdw = _ycat_bwd(dycat, o_sb, proj, o_ssd, sb_norm_w, ssd_norm_w)
    dq, dk, dv, p_w_out = _sb_attention_bwd(proj, o_sb_exact, do_sb, chip_w_out)
    dxs, dbm, dcm, ddt_raw, d_dtb, d_alog, d_dskip = _ssd_bwd(
        xa, dt_raw, dt_raw_t, dt_bias, dt_bias_t, a_log, a_log_t, d_exp, hstart, do_ssd)
    dxa = jnp.concatenate([dxs, dbm, dcm], axis=1)
    dxbc, d_convw, d_convb = _conv_bwd(dxa, proj, conv_w_full, conv_b)
    dproj = jnp.concatenate([t.astype(BF16) for t in (dq, dk, dv, dg, dz, dxbc)], axis=1)
    g_w_main = _matmul("d_w_in", u, dproj, "tn", 512, 512, LP)
    g_w_dt = _matmul("d_w_dt", u, ddt_raw.astype(BF16), "tn", 512, N_HEADS, LP)
    g_w_in = jnp.concatenate([g_w_main, g_w_dt], axis=1)
    g_w_in = g_w_in.reshape(D_MODEL, 4, 2, D_IN // N_DEV).transpose(2, 1, 0, 3)
    chip_w_in = _pair_sum("pair_sum_w_in", g_w_in, _swap_with_sibling("swap_w_in", g_w_in), 256)
    dh, d_nw, p_w_in = _d_u_prenorm_bwd(dproj, w_main, ddt_raw, w_dt, h_pad, norm_w, dh2, chip_w_in)

    pack = _pack_small_grads([d_nw, d_convb, d_dtb, d_alog, d_dskip, d_sbw, d_ssdw, d_fnw], loss_row, d_convw, dh)
    chip_small = _add("pair_sum_small", pack, _swap_with_sibling("swap_small", pack, whole=True))
    p_small = _exchange_between_chips("exchange_small", chip_small, whole=True)

    res_in = _adamw("adamw_w_in", p_w_in, w_in[0], m_w_in[0], v_w_in[0], 128)
    res_out = _adamw("adamw_w_out", p_w_out, w_out[0], m_w_out[0], v_w_out[0], 128)
    res_small = _adamw_small(
        p_small,
        [norm_w, conv_b, dt_bias, a_log, d_skip, sb_norm_w, ssd_norm_w, fnw, conv_w[0], meta_tokens],
        [m_norm_w, m_conv_b, m_dt_bias, m_a_log, m_d_skip, m_sb_norm_w, m_ssd_norm_w,
         m_final_norm_w.reshape(1, D_MODEL), m_conv_w[0], m_meta_tokens],
        [v_norm_w, v_conv_b, v_dt_bias, v_a_log, v_d_skip, v_sb_norm_w, v_ssd_norm_w,
         v_final_norm_w.reshape(1, D_MODEL), v_conv_w[0], v_meta_tokens])

    loss = jnp.sum(res_small[-1])
    order = ["meta_tokens", "norm_w", "w_in", "conv_w", "conv_b", "dt_bias", "a_log", "d_skip",
             "sb_norm_w", "ssd_norm_w", "w_out", "final_norm_w"]
    outs = [loss, dh[BLK:].reshape(1, SEQ, D_MODEL)]
    for kind in range(4):
        small = {name: res_small[4 * idx + kind] for idx, name in enumerate(_SMALL_ORDER)}
        small["final_norm_w"] = small["final_norm_w"].reshape(D_MODEL)
        small["conv_w"] = small["conv_w"].reshape(1, 4, XBC_W // N_DEV)
        small["w_in"] = res_in[kind].reshape(1, D_MODEL, D_IN // N_DEV)
        small["w_out"] = res_out[kind].reshape(1, 256, D_MODEL)
        outs += [small[name] for name in order]
    return tuple(outs)
```

```python
import functools

import jax
import jax.numpy as jnp
from jax import lax
from jax.experimental import pallas as pl
from jax.experimental.pallas import tpu as pltpu

F32 = jnp.float32
BF16 = jnp.bfloat16

D_MODEL = 1024
SEQ = 2048
N_META = 16
BLK = 128
PAD = BLK - N_META
LP = PAD + N_META + SEQ
NBLK = LP // BLK
N_HEADS = 16
HEAD_DIM = 64
N_GROUPS = 2
HEADS_PER_GROUP = 8
N_STATE = 128
SSD_W = 1024
XBC_W = 1536
D_MAIN = 6656
D_IN = 6672
COL_Q, COL_K, COL_V, COL_G, COL_Z, COL_XBC = 0, 1024, 2048, 3072, 4096, 5120
N_DEV = 8
EPS = 1e-5
SB_SCALE = 0.125
SB_DEAD = -87.4

ADAM_LR = 0.001
ADAM_B1 = 0.9
ADAM_B2 = 0.999
ADAM_EPS = 1e-08
ADAM_WD = 0.01
ADAM_STEP = 10

VMEM_LIMIT = 48 * 1024 * 1024

NN = (((1,), (0,)), ((), ()))
NT = (((1,), (1,)), ((), ()))
TN = (((0,), (0,)), ((), ()))


def _dot(a, b, dims=NN):
    return lax.dot_general(a, b, dims, preferred_element_type=F32)


def _split(x, n):
    parts = []
    r = x
    for i in range(n):
        p = r.astype(BF16)
        parts.append(p)
        if i + 1 < n:
            r = r - p.astype(F32)
    return parts


def _dot_x_exact(x, m, dims=NN, n=3):
    out = None
    for p in _split(x, n):
        t = _dot(p, m, dims)
        out = t if out is None else out + t
    return out


def _dot_exact_x(m, x, dims=NN, n=3):
    out = None
    for p in _split(x, n):
        t = _dot(m, p, dims)
        out = t if out is None else out + t
    return out


def _iota(shape, dim):
    return lax.broadcasted_iota(jnp.int32, shape, dim)


def _softplus(x):
    return jnp.maximum(x, 0.0) + jnp.log(1.0 + jnp.exp(-jnp.abs(x)))


def _sigmoid(x):
    return 1.0 / (1.0 + jnp.exp(-x))


def _params(sem=None):
    return pltpu.CompilerParams(dimension_semantics=sem, vmem_limit_bytes=VMEM_LIMIT)


_ANY = pl.BlockSpec(memory_space=pl.ANY)
_MESH = pl.DeviceIdType.MESH


def _mesh_position():
    return lax.axis_index("x"), lax.axis_index("y"), lax.axis_index("c")


def _other_chips(x, y):
    return [(1 - x, y), (x, 1 - y), (1 - x, 1 - y)]


def _gather_weights(srcs):
    n = len(srcs)

    def body(*refs):
        src, out = refs[:n], refs[n:2 * n]
        send_sems, recv_sems, loc_sems = refs[2 * n:]
        x, y, c = _mesh_position()
        sibling = (x, y, 1 - c)
        chips = _other_chips(x, y)

        def copy(a, k, block, to, from_src=False):
            slot = out[a].at[4 * block[0] + 2 * block[1] + block[2]]
            return pltpu.make_async_remote_copy(
                src_ref=src[a] if from_src else slot, dst_ref=slot,
                send_sem=send_sems.at[7 * a + k], recv_sem=recv_sems.at[7 * a + k],
                device_id=to, device_id_type=_MESH)

        local, sends = [], []
        for a in range(n):
            mine = pltpu.make_async_copy(src[a], out[a].at[4 * x + 2 * y + c], loc_sems.at[a])
            mine.start()
            local.append(mine)
            first = [copy(a, 0, (x, y, c), sibling, True)]
            first += [copy(a, 1 + j, (x, y, c), (*chip, c), True) for j, chip in enumerate(chips)]
            for cp in first:
                cp.start()
            sends += first
        for a in range(n):
            for j, chip in enumerate(chips):
                copy(a, 1 + j, (*chip, c), (x, y, c)).wait_recv()
                passed = copy(a, 4 + j, (*chip, c), sibling)
                passed.start()
                sends.append(passed)
        for a in range(n):
            copy(a, 0, (x, y, 1 - c), (x, y, c)).wait_recv()
            for j, chip in enumerate(chips):
                copy(a, 4 + j, (*chip, 1 - c), (x, y, c)).wait_recv()
        for cp in sends:
            cp.wait_send()
        for cp in local:
            cp.wait()

    return pl.pallas_call(
        body, name="gather_weights",
        out_shape=tuple(jax.ShapeDtypeStruct((N_DEV,) + s.shape, s.dtype) for s in srcs),
        in_specs=[_ANY] * n, out_specs=tuple([_ANY] * n),
        scratch_shapes=[pltpu.SemaphoreType.DMA((7 * n,)), pltpu.SemaphoreType.DMA((7 * n,)),
                        pltpu.SemaphoreType.DMA((n,))],
    )(*srcs)


def _swap_with_sibling(name, src, whole=False):
    n = 1 if whole else src.shape[0]

    def body(src_ref, out_ref, send_sems, recv_sems):
        x, y, c = _mesh_position()
        copies = []
        for k in range(n):
            copies.append(pltpu.make_async_remote_copy(
                src_ref=src_ref if whole else src_ref.at[k, 1 - c], dst_ref=out_ref if whole else out_ref.at[k],
                send_sem=send_sems.at[k], recv_sem=recv_sems.at[k], device_id=(x, y, 1 - c), device_id_type=_MESH))
        for cp in copies:
            cp.start()
        for cp in copies:
            cp.wait()

    shape = src.shape if whole else (src.shape[0],) + src.shape[2:]
    return pl.pallas_call(
        body, name=name, out_shape=jax.ShapeDtypeStruct(shape, src.dtype), in_specs=[_ANY], out_specs=_ANY,
        scratch_shapes=[pltpu.SemaphoreType.DMA((n,)), pltpu.SemaphoreType.DMA((n,))],
    )(src)


def _pair_sum(name, g, sib, rows):
    n, _, r_all, cols = g.shape
    assert r_all % rows == 0

    def body(g0_ref, g1_ref, s_ref, o_ref):
        c = lax.axis_index("c")
        mine = jnp.where(c == 0, g0_ref[0, 0].astype(F32), g1_ref[0, 0].astype(F32))
        o_ref[0] = (mine + s_ref[0].astype(F32)).astype(o_ref.dtype)

    return pl.pallas_call(
        body, name=name, out_shape=jax.ShapeDtypeStruct((n, r_all, cols), BF16), grid=(n, r_all // rows),
        in_specs=[pl.BlockSpec((1, 1, rows, cols), lambda k, i: (k, 0, i, 0)),
                  pl.BlockSpec((1, 1, rows, cols), lambda k, i: (k, 1, i, 0)),
                  pl.BlockSpec((1, rows, cols), lambda k, i: (k, i, 0))],
        out_specs=pl.BlockSpec((1, rows, cols), lambda k, i: (k, i, 0)),
        compiler_params=_params(("parallel", "parallel")),
    )(g, g, sib)


_CHIP_EXCHANGE_SEMS = [pltpu.SemaphoreType.DMA((3,)), pltpu.SemaphoreType.DMA((3,)), pltpu.SemaphoreType.DMA]


def _chip_exchange_copies(src, out, send_sems, recv_sems, loc_sem, whole=False, window=None):
    x, y, c = _mesh_position()
    here = 2 * x + y

    def slot(k):
        if whole:
            return src
        if window is not None:
            return src.at[:, pl.ds(pl.multiple_of(k * window[0], BLK), window[1])]
        return src.at[k]

    copies = [pltpu.make_async_copy(slot(here), out.at[here], loc_sem)]
    for j, chip in enumerate(_other_chips(x, y)):
        copies.append(pltpu.make_async_remote_copy(
            src_ref=slot(2 * chip[0] + chip[1]), dst_ref=out.at[here],
            send_sem=send_sems.at[j], recv_sem=recv_sems.at[j], device_id=(*chip, c), device_id_type=_MESH))
    return copies


def _exchange_between_chips(name, src, whole=False):
    def body(src_ref, out_ref, send_sems, recv_sems, loc_sem):
        copies = _chip_exchange_copies(src_ref, out_ref, send_sems, recv_sems, loc_sem, whole)
        for cp in copies:
            cp.start()
        for cp in copies:
            cp.wait()

    return pl.pallas_call(
        body, name=name, out_shape=jax.ShapeDtypeStruct(((4,) + src.shape) if whole else src.shape, src.dtype),
        in_specs=[_ANY], out_specs=_ANY, scratch_shapes=_CHIP_EXCHANGE_SEMS,
    )(src)


def _matmul(name, a, b, kind, tm, tn, tk, out_dtype=F32):
    if kind == "nn":
        (m, kk), (_, nn_) = a.shape, b.shape
        a_spec = pl.BlockSpec((tm, tk), lambda i, j, k: (i, k))
        b_spec = pl.BlockSpec((tk, tn), lambda i, j, k: (k, j))
        dims = NN
    elif kind == "nt":
        (m, kk), (nn_, _) = a.shape, b.shape
        a_spec = pl.BlockSpec((tm, tk), lambda i, j, k: (i, k))
        b_spec = pl.BlockSpec((tn, tk), lambda i, j, k: (j, k))
        dims = NT
    else:
        (kk, m), (_, nn_) = a.shape, b.shape
        a_spec = pl.BlockSpec((tk, tm), lambda i, j, k: (k, i))
        b_spec = pl.BlockSpec((tk, tn), lambda i, j, k: (k, j))
        dims = TN
    assert m % tm == 0 and nn_ % tn == 0 and kk % tk == 0
    nk = kk // tk

    def body(a_ref, b_ref, o_ref, acc_ref):
        k = pl.program_id(2)
        part = _dot(a_ref[...], b_ref[...], dims)
        if nk == 1:
            o_ref[...] = part.astype(o_ref.dtype)
        else:
            @pl.when(k == 0)
            def _():
                acc_ref[...] = part

            @pl.when(k > 0)
            def _():
                acc_ref[...] += part

            @pl.when(k == nk - 1)
            def _():
                o_ref[...] = acc_ref[...].astype(o_ref.dtype)

    return pl.pallas_call(
        body, name=name, out_shape=jax.ShapeDtypeStruct((m, nn_), out_dtype),
        grid=(m // tm, nn_ // tn, nk),
        in_specs=[a_spec, b_spec], out_specs=pl.BlockSpec((tm, tn), lambda i, j, k: (i, j)),
        scratch_shapes=[pltpu.VMEM((tm, tn) if nk > 1 else (8, 128), F32)],
        compiler_params=_params(("parallel", "parallel", "arbitrary")),
    )(a, b)


def _row_spec(width, col=0):
    return pl.BlockSpec((BLK, width), lambda i: (i, col))


def _const_spec(shape):
    return pl.BlockSpec(shape, lambda i: tuple(0 for _ in shape))


def _prenorm(h_pad, norm_w, wdt, wdt_t):
    def body(h_ref, w_ref, wdt_ref, wdtt_ref, u_ref, dt_ref, dtt_ref):
        xv = h_ref[...]
        r = lax.rsqrt(jnp.mean(xv * xv, axis=-1, keepdims=True) + EPS)
        u = (xv * r * w_ref[...]).astype(BF16)
        u_ref[...] = u
        dt_ref[...] = _dot(u, wdt_ref[...], NN)
        dtt_ref[...] = _dot(wdtt_ref[...], u, NT)

    return pl.pallas_call(
        body, name="prenorm",
        out_shape=(jax.ShapeDtypeStruct((LP, D_MODEL), BF16), jax.ShapeDtypeStruct((LP, N_HEADS), F32),
                   jax.ShapeDtypeStruct((N_HEADS, LP), F32)),
        grid=(NBLK,),
        in_specs=[_row_spec(D_MODEL), _const_spec((1, D_MODEL)), _const_spec((D_MODEL, N_HEADS)),
                  _const_spec((N_HEADS, D_MODEL))],
        out_specs=(_row_spec(D_MODEL), _row_spec(N_HEADS), pl.BlockSpec((N_HEADS, BLK), lambda i: (0, i))),
        compiler_params=_params(("parallel",)),
    )(h_pad, norm_w, wdt, wdt_t)


def _gated_norm(o, g, w):
    a = o * (g * _sigmoid(g))
    r = lax.rsqrt(jnp.mean(a * a, axis=-1, keepdims=True) + EPS)
    return a * r * w


def _ycat(o_sb, proj, o_ssd, sb_w, ssd_w):
    def body(osb_ref, g_ref, ossd_ref, z_ref, sbw_ref, ssdw_ref, y_ref):
        y_ref[:, :SSD_W] = _gated_norm(osb_ref[...], g_ref[...], sbw_ref[...]).astype(BF16)
        y_ref[:, SSD_W:] = _gated_norm(ossd_ref[...], z_ref[...], ssdw_ref[...]).astype(BF16)

    return pl.pallas_call(
        body, name="ycat", out_shape=jax.ShapeDtypeStruct((LP, 2 * SSD_W), BF16), grid=(NBLK,),
        in_specs=[_row_spec(1024), _row_spec(1024, COL_G // 1024), _row_spec(1024),
                  _row_spec(1024, COL_Z // 1024), _const_spec((1, 1024)), _const_spec((1, 1024))],
        out_specs=_row_spec(2 * SSD_W),
        compiler_params=_params(("parallel",)),
    )(o_sb, proj, o_ssd, proj, sb_w, ssd_w)


def _loss_head(h_pad, yo, fnw, target):
    def body(h_ref, yo_ref, w_ref, t_ref, dh2_ref, dh2b_ref, loss_ref, dw_ref):
        i = pl.program_id(0)

        @pl.when(i == 0)
        def _():
            loss_ref[...] = jnp.zeros_like(loss_ref)
            dw_ref[...] = jnp.zeros_like(dw_ref)

        h2 = h_ref[...] + yo_ref[...]
        r = lax.rsqrt(jnp.mean(h2 * h2, axis=-1, keepdims=True) + EPS)
        nrm = h2 * r
        w = w_ref[...]
        live = i > 0
        err = jnp.where(live, nrm * w - t_ref[...], 0.0)
        dout = err * (1.0 / D_MODEL)
        loss_ref[...] += (0.5 / D_MODEL) * _fold_lanes(jnp.sum(err * err, axis=0, keepdims=True))
        dw_ref[...] += jnp.sum(dout * nrm, axis=0, keepdims=True)
        wd = dout * w
        dh2 = r * (wd - nrm * jnp.mean(wd * nrm, axis=-1, keepdims=True))
        dh2_ref[...] = dh2
        dh2b_ref[...] = dh2.astype(BF16)

    return pl.pallas_call(
        body, name="loss_head",
        out_shape=(jax.ShapeDtypeStruct((LP, D_MODEL), F32), jax.ShapeDtypeStruct((LP, D_MODEL), BF16),
                   jax.ShapeDtypeStruct((1, BLK), F32), jax.ShapeDtypeStruct((1, D_MODEL), F32)),
        grid=(NBLK,),
        in_specs=[_row_spec(D_MODEL), _row_spec(D_MODEL), _const_spec((1, D_MODEL)),
                  pl.BlockSpec((BLK, D_MODEL), lambda i: (jnp.maximum(i - 1, 0), 0))],
        out_specs=(_row_spec(D_MODEL), _row_spec(D_MODEL), _const_spec((1, BLK)), _const_spec((1, D_MODEL))),
        compiler_params=_params(("arbitrary",)),
    )(h_pad, yo, fnw, target)


def _fold_lanes(row):
    out = row[:, :BLK]
    for j in range(1, row.shape[1] // BLK):
        out = out + row[:, j * BLK:(j + 1) * BLK]
    return out


def _gated_norm_bwd(dy, o, g, w):
    s = _sigmoid(g)
    sg = g * s
    a = o * sg
    r = lax.rsqrt(jnp.mean(a * a, axis=-1, keepdims=True) + EPS)
    nrm = a * r
    dw = jnp.sum(dy * nrm, axis=0, keepdims=True)
    wd = dy * w
    da = r * (wd - nrm * jnp.mean(wd * nrm, axis=-1, keepdims=True))
    return da * sg, da * o * (s * (1.0 + g * (1.0 - s))), dw


def _ycat_bwd(dycat, o_sb, proj, o_ssd, sb_w, ssd_w):
    def body(dy_ref, osb_ref, g_ref, ossd_ref, z_ref, sbw_ref, ssdw_ref,
             dosb_ref, dg_ref, dossd_ref, dz_ref, dsbw_ref, dssdw_ref):
        @pl.when(pl.program_id(0) == 0)
        def _():
            dsbw_ref[...] = jnp.zeros_like(dsbw_ref)
            dssdw_ref[...] = jnp.zeros_like(dssdw_ref)

        do, dg, dw = _gated_norm_bwd(dy_ref[:, :SSD_W], osb_ref[...], g_ref[...], sbw_ref[...])
        dosb_ref[...] = do
        dg_ref[...] = dg.astype(BF16)
        dsbw_ref[...] += dw
        do, dg, dw = _gated_norm_bwd(dy_ref[:, SSD_W:], ossd_ref[...], z_ref[...], ssdw_ref[...])
        dossd_ref[...] = do
        dz_ref[...] = dg.astype(BF16)
        dssdw_ref[...] += dw

    act = jax.ShapeDtypeStruct((LP, 1024), F32)
    gate = jax.ShapeDtypeStruct((LP, 1024), BF16)
    vec = jax.ShapeDtypeStruct((1, 1024), F32)
    return pl.pallas_call(
        body, name="ycat_bwd", out_shape=(act, gate, act, gate, vec, vec), grid=(NBLK,),
        in_specs=[_row_spec(2048), _row_spec(1024), _row_spec(1024, COL_G // 1024), _row_spec(1024),
                  _row_spec(1024, COL_Z // 1024), _const_spec((1, 1024)), _const_spec((1, 1024))],
        out_specs=(_row_spec(1024), _row_spec(1024), _row_spec(1024), _row_spec(1024),
                   _const_spec((1, 1024)), _const_spec((1, 1024))),
        compiler_params=_params(("arbitrary",)),
    )(dycat, o_sb, proj, o_ssd, proj, sb_w, ssd_w)


_DPROJ_PIECES = (("dq", COL_Q, 1024), ("dk", COL_K, 1024), ("dv", COL_V, 1024), ("dg", COL_G, 1024),
                 ("dz", COL_Z, 1024), ("dxbc", COL_XBC, XBC_W), ("ddt", D_MAIN, BLK))
W_IN_PAD = D_MAIN + BLK
WIN_STRIDE = 13 * BLK
WIN_WIDTH = 14 * BLK


def _d_w_in(u, pieces):
    tm, tk = 256, LP // 4
    nk = LP // tk

    def body(*refs):
        u_ref, piece_refs, (o_ref, acc_ref) = refs[0], refs[1:1 + len(pieces)], refs[1 + len(pieces):]
        k = pl.program_id(1)
        a = u_ref[...]
        for (_, col, width), ref in zip(_DPROJ_PIECES, piece_refs):
            part = _dot(a, ref[...].astype(BF16), TN)

            @pl.when(k == 0)
            def _():
                acc_ref[:, col:col + width] = part

            @pl.when(k > 0)
            def _():
                acc_ref[:, col:col + width] += part

        @pl.when(k == nk - 1)
        def _():
            o_ref[...] = acc_ref[...].astype(BF16)

    return pl.pallas_call(
        body, name="d_w_in", out_shape=jax.ShapeDtypeStruct((D_MODEL, W_IN_PAD), BF16), grid=(D_MODEL // tm, nk),
        in_specs=[pl.BlockSpec((tk, tm), lambda i, k: (k, i))]
        + [pl.BlockSpec((tk, width), lambda i, k: (k, 0)) for _, _, width in _DPROJ_PIECES],
        out_specs=pl.BlockSpec((tm, W_IN_PAD), lambda i, k: (i, 0)),
        scratch_shapes=[pltpu.VMEM((tm, W_IN_PAD), F32)],
        compiler_params=_params(("parallel", "arbitrary")),
    )(u, *pieces)


def _d_u_prenorm_bwd(pieces, w_main, wdt, h_pad, norm_w, dh2, chip_sum):
    tm, tk = LP // 4, 512
    nk = D_MAIN // tk
    main = _DPROJ_PIECES[:-1]
    window = (WIN_STRIDE, WIN_WIDTH)

    def body(*refs):
        piece_refs = refs[:len(main)]
        (b_ref, ddt_ref, wdt_ref, h_ref, w_ref, dh2_ref, src_ref, dh_ref, dw_ref, out_ref,
         acc_ref, send_sems, recv_sems, loc_sem) = refs[len(main):]
        i, k = pl.program_id(0), pl.program_id(1)

        @pl.when(jnp.logical_and(i == 0, k == 0))
        def _():
            for cp in _chip_exchange_copies(src_ref, out_ref, send_sems, recv_sems, loc_sem, window=window):
                cp.start()
            dw_ref[...] = jnp.zeros_like(dw_ref)

        @pl.when(k == 0)
        def _():
            acc_ref[...] = jnp.zeros_like(acc_ref)

        for (_, col, width), ref in zip(main, piece_refs):
            for lo in range(0, width, tk):
                @pl.when(k == (col + lo) // tk)
                def _():
                    acc_ref[...] += _dot(ref[:, lo:lo + tk], b_ref[...], NT)

        @pl.when(k == nk - 1)
        def _():
            dut = acc_ref[...] + _dot(ddt_ref[:, :N_HEADS].astype(BF16), wdt_ref[...], NT)
            xv = h_ref[...]
            r = lax.rsqrt(jnp.mean(xv * xv, axis=-1, keepdims=True) + EPS)
            nrm = xv * r
            dw_ref[...] += jnp.sum(dut * nrm, axis=0, keepdims=True)
            wd = dut * w_ref[...]
            dh_ref[...] = dh2_ref[...] + r * (wd - nrm * jnp.mean(wd * nrm, axis=-1, keepdims=True))

        @pl.when(jnp.logical_and(i == LP // tm - 1, k == nk - 1))
        def _():
            for cp in _chip_exchange_copies(src_ref, out_ref, send_sems, recv_sems, loc_sem, window=window):
                cp.wait()

    rows = lambda width: pl.BlockSpec((tm, width), lambda i, k: (i, 0))
    const = lambda shape: pl.BlockSpec(shape, lambda i, k: (0, 0))
    return pl.pallas_call(
        body, name="d_u_prenorm_bwd",
        out_shape=(jax.ShapeDtypeStruct((LP, D_MODEL), F32), jax.ShapeDtypeStruct((1, D_MODEL), F32),
                   jax.ShapeDtypeStruct((4, chip_sum.shape[0], WIN_WIDTH), chip_sum.dtype)),
        grid=(LP // tm, nk),
        in_specs=[rows(width) for _, _, width in main]
        + [pl.BlockSpec((D_MODEL, tk), lambda i, k: (0, k)), rows(BLK), const((D_MODEL, N_HEADS)), rows(D_MODEL),
           const((1, D_MODEL)), rows(D_MODEL), _ANY],
        out_specs=(rows(D_MODEL), const((1, D_MODEL)), _ANY),
        scratch_shapes=[pltpu.VMEM((tm, D_MODEL), F32)] + _CHIP_EXCHANGE_SEMS,
        compiler_params=_params(("arbitrary", "arbitrary")),
    )(*pieces[:-1], w_main, pieces[-1], wdt, h_pad, norm_w, dh2, chip_sum)


def _suffix_sum(vals, tri):
    return _dot_x_exact(vals, tri, NN, n=2)


def _sb_tile(z, valid):
    t = jnp.exp(-jnp.abs(z))
    inv = 1.0 / (1.0 + t)
    sp = jnp.maximum(z, 0.0) + jnp.log(1.0 + t)
    sig = jnp.where(z >= 0, inv, t * inv)
    return sig, jnp.where(valid, -sp, 0.0), z - sp


def _sweep(first, step, init, run_slots):
    def alive_of(state):
        top = state[run_slots[0]]
        for s in run_slots[1:]:
            top = jnp.maximum(top, state[s])
        return (jnp.max(top) > SB_DEAD).astype(jnp.int32)

    def cond(carry):
        return jnp.logical_and(carry[0] >= 0, carry[1] > 0)

    def body(carry):
        state = step(carry[0], tuple(carry[2:]))
        return (carry[0] - 1, alive_of(state)) + tuple(state)

    return lax.while_loop(cond, body, (first, alive_of(init)) + tuple(init))[2:]


def _head_masks(x, lane):
    head0 = lane < HEAD_DIM
    return jnp.where(head0, x, 0.0).astype(BF16), jnp.where(head0, 0.0, x).astype(BF16)


def _stage_kv(qi, k_ref, v_ref, kb_ref, vb_ref):
    @pl.when(qi == 0)
    def _():
        kb_ref[:BLK, :] = jnp.zeros((BLK, BLK), BF16)
        vb_ref[:BLK, :] = jnp.zeros((BLK, BLK), BF16)
        kb_ref[BLK:, :] = k_ref[...].astype(BF16)
        vb_ref[BLK:, :] = v_ref[...].astype(BF16)


def _sb_attention_fwd(proj):
    def body(q_ref, k_ref, v_ref, o_ref, ox_ref, kb_ref, vb_ref):
        qi = pl.program_id(1)
        _stage_kv(qi, k_ref, v_ref, kb_ref, vb_ref)
        lane = _iota((BLK, BLK), 1)
        qh = _head_masks(q_ref[...] * SB_SCALE, lane)
        row1 = qi * BLK + _iota((BLK, BLK), 0)

        def tile(h, ks, vs, valid, tri, run):
            z = _dot(qh[h], ks, NT)
            _, lk, lb = _sb_tile(z, valid)
            after = _suffix_sum(lk, tri)
            if run is not None:
                after = after + run
            a = jnp.where(valid, jnp.exp(lb + after), 0.0)
            a_hi = a.astype(BF16)
            return (jnp.sum(lk, axis=1, keepdims=True), _dot(a_hi, vs, NN),
                    _dot((a - a_hi.astype(F32)).astype(BF16), vs, NN))

        off = pl.multiple_of(qi * BLK, BLK)
        ks2 = kb_ref[pl.ds(off, 2 * BLK), :]
        vs2 = vb_ref[pl.ds(off, 2 * BLK), :]
        col2 = (qi - 1) * BLK + _iota((BLK, 2 * BLK), 1)
        row2 = qi * BLK + _iota((BLK, 2 * BLK), 0)
        valid2 = jnp.logical_and(col2 < row2, col2 >= PAD)
        tri2 = (_iota((2 * BLK, 2 * BLK), 0) > _iota((2 * BLK, 2 * BLK), 1)).astype(BF16)
        init = []
        for h in range(2):
            init += list(tile(h, ks2, vs2, valid2, tri2, None))
        tri1 = (_iota((BLK, BLK), 0) > lane).astype(BF16)

        def step(kblk, carry):
            o1 = pl.multiple_of((kblk + 1) * BLK, BLK)
            ks = kb_ref[pl.ds(o1, BLK), :]
            vs = vb_ref[pl.ds(o1, BLK), :]
            col = kblk * BLK + lane
            valid = jnp.logical_and(col < row1, col >= PAD)
            new = []
            for h in range(2):
                run, acc, rest = carry[3 * h], carry[3 * h + 1], carry[3 * h + 2]
                d_run, d_acc, d_rest = tile(h, ks, vs, valid, tri1, run)
                new += [run + d_run, acc + d_acc, rest + d_rest]
            return tuple(new)

        res = _sweep(qi - 2, step, init, (0, 3))
        o = jnp.where(lane < HEAD_DIM, res[1], res[4])
        o_ref[...] = o
        ox_ref[...] = o + jnp.where(lane < HEAD_DIM, res[2], res[5])

    act = jax.ShapeDtypeStruct((LP, 1024), F32)
    return pl.pallas_call(
        body, name="sb_attn_fwd", out_shape=(act, act),
        grid=(N_HEADS // 2, NBLK),
        in_specs=[pl.BlockSpec((BLK, BLK), lambda hp, qi: (qi, COL_Q // BLK + hp)),
                  pl.BlockSpec((LP, BLK), lambda hp, qi: (0, COL_K // BLK + hp)),
                  pl.BlockSpec((LP, BLK), lambda hp, qi: (0, COL_V // BLK + hp))],
        out_specs=(pl.BlockSpec((BLK, BLK), lambda hp, qi: (qi, hp)),
                   pl.BlockSpec((BLK, BLK), lambda hp, qi: (qi, hp))),
        scratch_shapes=[pltpu.VMEM((LP + BLK, BLK), BF16), pltpu.VMEM((LP + BLK, BLK), BF16)],
        compiler_params=_params(("arbitrary", "arbitrary")),
    )(proj, proj, proj)


def _sb_attention_bwd(proj, o_sb, do_sb, chip_sums):
    def body(q_ref, k_ref, v_ref, o_ref, do_ref, src_ref, dq_ref, dk_ref, dv_ref, out_ref,
             kb_ref, vb_ref, dka_ref, dva_ref, send_sems, recv_sems, loc_sem):
        hp, qi = pl.program_id(0), pl.program_id(1)
        _stage_kv(qi, k_ref, v_ref, kb_ref, vb_ref)

        @pl.when(jnp.logical_and(hp == 0, qi == 0))
        def _():
            for cp in _chip_exchange_copies(src_ref, out_ref, send_sems, recv_sems, loc_sem):
                cp.start()

        @pl.when(qi == 0)
        def _():
            dka_ref[...] = jnp.zeros_like(dka_ref)
            dva_ref[...] = jnp.zeros_like(dva_ref)

        lane = _iota((BLK, BLK), 1)
        head0 = lane < HEAD_DIM
        qh = _head_masks(q_ref[...] * SB_SCALE, lane)
        do = do_ref[...]
        doh = _head_masks(do, lane)
        prod = do.astype(BF16).astype(F32) * o_ref[...]
        dsum = (jnp.sum(jnp.where(head0, prod, 0.0), axis=1, keepdims=True),
                jnp.sum(jnp.where(head0, 0.0, prod), axis=1, keepdims=True))
        row1 = qi * BLK + _iota((BLK, BLK), 0)

        def tile(h, ks, vs, valid, tri, run, erun):
            z = _dot(qh[h], ks, NT)
            sig, lk, lb = _sb_tile(z, valid)
            after = _suffix_sum(lk, tri)
            if run is not None:
                after = after + run
            a = jnp.where(valid, jnp.exp(lb + after), 0.0)
            e = a * _dot(doh[h], vs, NT)
            esuf = _suffix_sum(e, tri)
            if erun is not None:
                esuf = esuf + erun
            dz = jnp.where(valid, e - sig * (dsum[h] - esuf), 0.0).astype(BF16)
            return (jnp.sum(lk, axis=1, keepdims=True), jnp.sum(e, axis=1, keepdims=True),
                    _dot(dz, ks, NN), _dot(dz, qh[h], TN), _dot(a.astype(BF16), doh[h], TN))

        off = pl.multiple_of(qi * BLK, BLK)
        ks2 = kb_ref[pl.ds(off, 2 * BLK), :]
        vs2 = vb_ref[pl.ds(off, 2 * BLK), :]
        col2 = (qi - 1) * BLK + _iota((BLK, 2 * BLK), 1)
        row2 = qi * BLK + _iota((BLK, 2 * BLK), 0)
        valid2 = jnp.logical_and(col2 < row2, col2 >= PAD)
        tri2 = (_iota((2 * BLK, 2 * BLK), 0) > _iota((2 * BLK, 2 * BLK), 1)).astype(BF16)
        init = []
        for h in range(2):
            d_run, d_erun, d_q, d_k, d_v = tile(h, ks2, vs2, valid2, tri2, None, None)
            init += [d_run, d_erun, d_q]
            dk2 = d_k if h == 0 else dk2 + d_k
            dv2 = d_v if h == 0 else dv2 + d_v
        dka_ref[pl.ds(off, 2 * BLK), :] += dk2
        dva_ref[pl.ds(off, 2 * BLK), :] += dv2
        tri1 = (_iota((BLK, BLK), 0) > lane).astype(BF16)

        def step(kblk, carry):
            o1 = pl.multiple_of((kblk + 1) * BLK, BLK)
            ks = kb_ref[pl.ds(o1, BLK), :]
            vs = vb_ref[pl.ds(o1, BLK), :]
            col = kblk * BLK + lane
            valid = jnp.logical_and(col < row1, col >= PAD)
            new = []
            dk1 = jnp.zeros((BLK, BLK), F32)
            dv1 = jnp.zeros((BLK, BLK), F32)
            for h in range(2):
                run, erun, dq = carry[3 * h], carry[3 * h + 1], carry[3 * h + 2]
                d_run, d_erun, d_q, d_k, d_v = tile(h, ks, vs, valid, tri1, run, erun)
                new += [run + d_run, erun + d_erun, dq + d_q]
                dk1 = dk1 + d_k
                dv1 = dv1 + d_v
            dka_ref[pl.ds(o1, BLK), :] += dk1
            dva_ref[pl.ds(o1, BLK), :] += dv1
            return tuple(new)

        res = _sweep(qi - 2, step, init, (0, 3))
        dq_ref[...] = (jnp.where(head0, res[2], res[5]) * SB_SCALE).astype(BF16)

        @pl.when(qi == NBLK - 1)
        def _():
            dk_ref[...] = dka_ref[BLK:, :].astype(BF16)
            dv_ref[...] = dva_ref[BLK:, :].astype(BF16)

        @pl.when(jnp.logical_and(hp == N_HEADS // 2 - 1, qi == NBLK - 1))
        def _():
            for cp in _chip_exchange_copies(src_ref, out_ref, send_sems, recv_sems, loc_sem):
                cp.wait()

    act = jax.ShapeDtypeStruct((LP, 1024), BF16)
    blk = lambda col: pl.BlockSpec((BLK, BLK), lambda hp, qi: (qi, col + hp))
    whole = lambda col: pl.BlockSpec((LP, BLK), lambda hp, qi: (0, col + hp))
    return pl.pallas_call(
        body, name="sb_attn_bwd",
        out_shape=(act, act, act, jax.ShapeDtypeStruct(chip_sums.shape, chip_sums.dtype)),
        grid=(N_HEADS // 2, NBLK),
        in_specs=[blk(COL_Q // BLK), whole(COL_K // BLK), whole(COL_V // BLK), blk(0), blk(0), _ANY],
        out_specs=(blk(0), whole(0), whole(0), _ANY),
        scratch_shapes=[pltpu.VMEM((LP + BLK, BLK), BF16), pltpu.VMEM((LP + BLK, BLK), BF16),
                        pltpu.VMEM((LP + BLK, BLK), F32), pltpu.VMEM((LP + BLK, BLK), F32)] + _CHIP_EXCHANGE_SEMS,
        compiler_params=_params(("arbitrary", "arbitrary")),
    )(proj, proj, proj, o_sb, do_sb, chip_sums)


def _conv_pre(x, w_ref, b_ref):
    acc = b_ref[...] + w_ref[3:4, :] * x
    for k in range(3):
        acc = acc + w_ref[k:k + 1, :] * pltpu.roll(x, 3 - k, 0)
    return acc


def _conv_fwd(proj, conv_w, conv_b):
    def body(x_ref, w_ref, b_ref, o_ref):
        xc = _conv_pre(x_ref[...], w_ref, b_ref)
        o_ref[...] = xc * _sigmoid(xc)

    nb = XBC_W // BLK
    return pl.pallas_call(
        body, name="conv_fwd", out_shape=jax.ShapeDtypeStruct((LP, XBC_W), F32), grid=(nb,),
        in_specs=[pl.BlockSpec((LP, BLK), lambda j: (0, COL_XBC // BLK + j)),
                  pl.BlockSpec((4, BLK), lambda j: (0, j)), pl.BlockSpec((1, BLK), lambda j: (0, j))],
        out_specs=pl.BlockSpec((LP, BLK), lambda j: (0, j)),
        compiler_params=_params(("parallel",)),
    )(proj, conv_w, conv_b)


def _conv_bwd(dxa, proj, conv_w, conv_b):
    def body(d_ref, x_ref, w_ref, b_ref, dx_ref, dw_ref, db_ref):
        x = x_ref[...]
        xc = _conv_pre(x, w_ref, b_ref)
        s = _sigmoid(xc)
        live = _iota((LP, BLK), 0) >= PAD
        dxc = jnp.where(live, d_ref[...] * (s * (1.0 + xc * (1.0 - s))), 0.0)
        db_ref[...] = jnp.sum(dxc, axis=0, keepdims=True)
        dx = w_ref[3:4, :] * dxc
        dw_ref[3:4, :] = jnp.sum(dxc * x, axis=0, keepdims=True)
        for k in range(3):
            dw_ref[k:k + 1, :] = jnp.sum(dxc * pltpu.roll(x, 3 - k, 0), axis=0, keepdims=True)
            dx = dx + w_ref[k:k + 1, :] * pltpu.roll(dxc, LP - (3 - k), 0)
        dx_ref[...] = dx.astype(BF16)

    nb = XBC_W // BLK
    col = lambda j: (0, j)
    return pl.pallas_call(
        body, name="conv_bwd",
        out_shape=(jax.ShapeDtypeStruct((LP, XBC_W), BF16), jax.ShapeDtypeStruct((4, XBC_W), F32),
                   jax.ShapeDtypeStruct((1, XBC_W), F32)),
        grid=(nb,),
        in_specs=[pl.BlockSpec((LP, BLK), col), pl.BlockSpec((LP, BLK), lambda j: (0, COL_XBC // BLK + j)),
                  pl.BlockSpec((4, BLK), col), pl.BlockSpec((1, BLK), col)],
        out_specs=(pl.BlockSpec((LP, BLK), col), pl.BlockSpec((4, BLK), col), pl.BlockSpec((1, BLK), col)),
        compiler_params=_params(("parallel",)),
    )(dxa, proj, conv_w, conv_b)


def _ssd_prelude(c, dt_ref, dtt_ref, dtb_ref, dtbt_ref, alog_ref, alogt_ref):
    live = jnp.logical_or(c > 0, _iota((BLK, N_HEADS), 0) >= PAD)
    live_t = jnp.logical_or(c > 0, _iota((N_HEADS, BLK), 1) >= PAD)
    pre = dt_ref[...] + dtb_ref[...]
    pre_t = dtt_ref[...] + dtbt_ref[...]
    dt = jnp.where(live, _softplus(pre), 0.0)
    dt_t = jnp.where(live_t, _softplus(pre_t), 0.0)
    a = -jnp.exp(alog_ref[...])
    a_t = -jnp.exp(alogt_ref[...])
    li = _iota((BLK, BLK), 0)
    si = _iota((BLK, BLK), 1)
    lower = (si <= li).astype(BF16)
    upper = (li <= si).astype(BF16)
    acum = _dot_exact_x(lower, dt * a, NN)
    acum_t = _dot_x_exact(dt_t * a_t, upper, NN)
    return live, pre, dt, a, a_t, acum, acum_t


def _head_expand():
    return (_iota((N_HEADS, SSD_W), 1) // HEAD_DIM == _iota((N_HEADS, SSD_W), 0)).astype(BF16)


def _head_reduce_mat():
    return (_iota((SSD_W, N_HEADS), 0) // HEAD_DIM == _iota((SSD_W, N_HEADS), 1)).astype(BF16)


def _decay_mat(acum, acum_t, h, causal):
    seg = jnp.minimum(acum[:, h:h + 1] - acum_t[h:h + 1, :], 0.0)
    return jnp.where(causal, jnp.exp(seg), 0.0)


def _ssd_specs():
    chunk = lambda width, col=0: pl.BlockSpec((BLK, width), lambda c: (c, col))
    return chunk


def _ssd_fwd(xa, dt_raw, dt_raw_t, dt_bias, dt_bias_t, a_log, a_log_t, d_exp):
    def body(x_ref, b_ref, c_ref, dt_ref, dtt_ref, dtb_ref, dtbt_ref, alog_ref, alogt_ref, dexp_ref,
             y_ref, hs_ref, state_ref):
        c = pl.program_id(0)

        @pl.when(c == 0)
        def _():
            state_ref[...] = jnp.zeros_like(state_ref)

        _, _, dt, _, _, acum, acum_t = _ssd_prelude(c, dt_ref, dtt_ref, dtb_ref, dtbt_ref, alog_ref, alogt_ref)
        expand = _head_expand()
        x = x_ref[...]
        xdt = x * _dot_x_exact(dt, expand)
        exp_a = _dot_x_exact(jnp.exp(acum), expand)
        to_end = _dot_x_exact(jnp.exp(acum[BLK - 1:BLK, :] - acum), expand)
        xdt_b = xdt.astype(BF16)
        xd_b = (xdt * to_end).astype(BF16)
        chunk_decay = jnp.exp(acum_t[:, BLK - 1:BLK])
        hs_ref[0] = state_ref[...]
        lane = _iota((BLK, BLK), 1)
        causal = _iota((BLK, BLK), 0) >= lane
        gw = HEADS_PER_GROUP * HEAD_DIM
        for g in range(N_GROUPS):
            bg = b_ref[:, g * N_STATE:(g + 1) * N_STATE].astype(BF16)
            cg = c_ref[:, g * N_STATE:(g + 1) * N_STATE].astype(BF16)
            cb = _dot(cg, bg, NT)
            hg = state_ref[g * gw:(g + 1) * gw, :]
            ch = _dot(cg, hg.astype(BF16), NT)
            st = _dot(xd_b[:, g * gw:(g + 1) * gw], bg, TN)
            for p in range(HEADS_PER_GROUP // 2):
                h0 = g * HEADS_PER_GROUP + 2 * p
                lo = h0 * HEAD_DIM
                xp = xdt_b[:, lo:lo + BLK]
                w0 = (cb * _decay_mat(acum, acum_t, h0, causal)).astype(BF16)
                w1 = (cb * _decay_mat(acum, acum_t, h0 + 1, causal)).astype(BF16)
                yd = jnp.where(lane < HEAD_DIM, _dot(w0, xp), _dot(w1, xp))
                y_ref[:, lo:lo + BLK] = (yd + ch[:, lo - g * gw:lo - g * gw + BLK] * exp_a[:, lo:lo + BLK]
                                         + x[:, lo:lo + BLK] * dexp_ref[:, lo:lo + BLK])
            for r in range(HEADS_PER_GROUP):
                h = g * HEADS_PER_GROUP + r
                state_ref[h * HEAD_DIM:(h + 1) * HEAD_DIM, :] = (
                    hg[r * HEAD_DIM:(r + 1) * HEAD_DIM, :] * chunk_decay[h:h + 1, :]
                    + st[r * HEAD_DIM:(r + 1) * HEAD_DIM, :])

    chunk = _ssd_specs()
    return pl.pallas_call(
        body, name="ssd_fwd",
        out_shape=(jax.ShapeDtypeStruct((LP, SSD_W), F32), jax.ShapeDtypeStruct((NBLK, SSD_W, N_STATE), F32)),
        grid=(NBLK,),
        in_specs=[chunk(SSD_W), chunk(256, 4), chunk(256, 5), chunk(N_HEADS),
                  pl.BlockSpec((N_HEADS, BLK), lambda c: (0, c)), _const_spec((1, N_HEADS)),
                  _const_spec((N_HEADS, 1)), _const_spec((1, N_HEADS)), _const_spec((N_HEADS, 1)),
                  _const_spec((1, SSD_W))],
        out_specs=(chunk(SSD_W), pl.BlockSpec((1, SSD_W, N_STATE), lambda c: (c, 0, 0))),
        scratch_shapes=[pltpu.VMEM((SSD_W, N_STATE), F32)],
        compiler_params=_params(("arbitrary",)),
    )(xa, xa, xa, dt_raw, dt_raw_t, dt_bias, dt_bias_t, a_log, a_log_t, d_exp)


def _ssd_bwd(xa, dt_raw, dt_raw_t, dt_bias, dt_bias_t, a_log, a_log_t, d_exp, hstart, dy):
    def body(x_ref, b_ref, c_ref, dt_ref, dtt_ref, dtb_ref, dtbt_ref, alog_ref, alogt_ref, dexp_ref,
             hs_ref, dy_ref, dxa_ref, ddt_ref, dbias_ref, dalog_ref, dd_ref, dstate_ref):
        step = pl.program_id(0)
        c = NBLK - 1 - step

        @pl.when(step == 0)
        def _():
            dstate_ref[...] = jnp.zeros_like(dstate_ref)
            dbias_ref[...] = jnp.zeros_like(dbias_ref)
            dalog_ref[...] = jnp.zeros_like(dalog_ref)
            dd_ref[...] = jnp.zeros_like(dd_ref)

        live, pre, dt, a, a_t, acum, acum_t = _ssd_prelude(c, dt_ref, dtt_ref, dtb_ref, dtbt_ref,
                                                           alog_ref, alogt_ref)
        expand = _head_expand()
        reduce_m = _head_reduce_mat()
        x = x_ref[...]
        dyv = dy_ref[...]
        dt_e = _dot_x_exact(dt, expand)
        xdt = x * dt_e
        exp_acum = jnp.exp(acum)
        exp_a = _dot_x_exact(exp_acum, expand)
        dte = jnp.exp(acum[BLK - 1:BLK, :] - acum)
        to_end = _dot_x_exact(dte, expand)
        xdt_b = xdt.astype(BF16)
        xd_b = (xdt * to_end).astype(BF16)
        chunk_decay = jnp.exp(acum_t[:, BLK - 1:BLK])
        lane = _iota((BLK, BLK), 1)
        head0 = lane < HEAD_DIM
        causal = _iota((BLK, BLK), 0) >= lane
        gw = HEADS_PER_GROUP * HEAD_DIM
        dm = dyv * exp_a
        dm_b = dm.astype(BF16)
        onehot = lambda h: (_iota((1, N_HEADS), 1) == h).astype(F32)
        onehot_t = lambda h: (_iota((N_HEADS, 1), 0) == h).astype(F32)
        dacum = jnp.zeros((BLK, N_HEADS), F32)
        dacum_t = jnp.zeros((N_HEADS, BLK), F32)
        ddt_acc = jnp.zeros((BLK, N_HEADS), F32)
        ddte_acc = jnp.zeros((BLK, N_HEADS), F32)
        dexpa_acc = jnp.zeros((BLK, N_HEADS), F32)
        dskip_acc = jnp.zeros((BLK, N_HEADS), F32)
        head_sum = expand
        for g in range(N_GROUPS):
            bg = b_ref[:, g * N_STATE:(g + 1) * N_STATE].astype(BF16)
            cg = c_ref[:, g * N_STATE:(g + 1) * N_STATE].astype(BF16)
            cb = _dot(cg, bg, NT)
            hg = hs_ref[0, g * gw:(g + 1) * gw, :]
            hg_b = hg.astype(BF16)
            dhe = dstate_ref[g * gw:(g + 1) * gw, :]
            dhe_b = dhe.astype(BF16)
            ch = _dot(cg, hg_b, NT)
            dcg = _dot(dm_b[:, g * gw:(g + 1) * gw], hg_b, NN)
            dhs = _dot(dm_b[:, g * gw:(g + 1) * gw], cg, TN)
            dxd = _dot(bg, dhe_b, NT)
            dbg = _dot(xd_b[:, g * gw:(g + 1) * gw], dhe_b, NN)
            dcb = jnp.zeros((BLK, BLK), F32)
            for p in range(HEADS_PER_GROUP // 2):
                h0 = g * HEADS_PER_GROUP + 2 * p
                lo = h0 * HEAD_DIM
                xp = xdt_b[:, lo:lo + BLK]
                dyp = dyv[:, lo:lo + BLK]
                dyh = (jnp.where(head0, dyp, 0.0).astype(BF16), jnp.where(head0, 0.0, dyp).astype(BF16))
                dxdt_p = jnp.zeros((BLK, BLK), F32)
                for q in range(2):
                    h = h0 + q
                    dec = _decay_mat(acum, acum_t, h, causal)
                    w = cb * dec
                    dw = _dot(dyh[q], xp, NT)
                    t = dw * w
                    dacum = dacum + jnp.sum(t, axis=1, keepdims=True) * onehot(h)
                    dacum_t = dacum_t - jnp.sum(t, axis=0, keepdims=True) * onehot_t(h)
                    dcb = dcb + dw * dec
                    dxdt_p = dxdt_p + _dot(w.astype(BF16), dyh[q], TN)
                sl = slice(lo, lo + BLK)
                gl = slice(lo - g * gw, lo - g * gw + BLK)
                dxdt_p = dxdt_p + dxd[:, gl] * to_end[:, sl]
                dxa_ref[:, sl] = dyp * dexp_ref[:, sl] + dxdt_p * dt_e[:, sl]
                red = reduce_m[lo:lo + BLK, :]
                ddt_acc = ddt_acc + _dot_x_exact(dxdt_p * x[:, sl], red)
                ddte_acc = ddte_acc + _dot_x_exact(dxd[:, gl] * xdt[:, sl], red)
                dexpa_acc = dexpa_acc + _dot_x_exact(dyp * ch[:, gl], red)
                dskip_acc = dskip_acc + _dot_x_exact(dyp * x[:, sl], red)
            dcb_b = dcb.astype(BF16)
            b_col = SSD_W + g * N_STATE
            c_col = SSD_W + (N_GROUPS + g) * N_STATE
            dxa_ref[:, c_col:c_col + N_STATE] = dcg + _dot(dcb_b, bg, NN)
            dxa_ref[:, b_col:b_col + N_STATE] = dbg + _dot(dcb_b, cg, TN)
            prod = dhe * hg
            per_head = jnp.sum(_dot_exact_x(head_sum[:, g * gw:(g + 1) * gw], prod, NN), axis=1, keepdims=True)
            dacum_t = dacum_t + (per_head * chunk_decay) * (_iota((1, BLK), 1) == BLK - 1).astype(F32)
            for r in range(HEADS_PER_GROUP):
                h = g * HEADS_PER_GROUP + r
                rows = slice(h * HEAD_DIM, (h + 1) * HEAD_DIM)
                dstate_ref[rows, :] = (dhs[r * HEAD_DIM:(r + 1) * HEAD_DIM, :]
                                       + dhe[r * HEAD_DIM:(r + 1) * HEAD_DIM, :] * chunk_decay[h:h + 1, :])
        dacum = dacum + dexpa_acc * exp_acum - ddte_acc * dte
        last_row = (_iota((BLK, 1), 0) == BLK - 1).astype(F32)
        dacum = dacum + last_row * jnp.sum(ddte_acc * dte, axis=0, keepdims=True)
        li = _iota((BLK, BLK), 0)
        si = _iota((BLK, BLK), 1)
        upper = (li <= si).astype(BF16)
        lower = (si <= li).astype(BF16)
        dda = _dot_exact_x(upper, dacum, NN)
        dda_t = _dot_x_exact(dacum_t, lower, NN)
        eye = (_iota((N_HEADS, N_HEADS), 0) == _iota((N_HEADS, N_HEADS), 1)).astype(BF16)
        dda = dda + _dot_x_exact_tn(dda_t, eye)
        ddt = ddt_acc + dda * a
        dalog_ref[...] += jnp.sum(dda * dt, axis=0, keepdims=True) * a
        dd_ref[...] += jnp.sum(dskip_acc, axis=0, keepdims=True)
        ddt_raw = jnp.where(live, ddt * _sigmoid(pre), 0.0)
        ddt_ref[...] = jnp.zeros_like(ddt_ref)
        ddt_ref[:, :N_HEADS] = ddt_raw
        dbias_ref[...] += jnp.sum(ddt_raw, axis=0, keepdims=True)

    rev = lambda width, col=0: pl.BlockSpec((BLK, width), lambda s: (NBLK - 1 - s, col))
    vec = jax.ShapeDtypeStruct((1, N_HEADS), F32)
    return pl.pallas_call(
        body, name="ssd_bwd",
        out_shape=(jax.ShapeDtypeStruct((LP, XBC_W), F32), jax.ShapeDtypeStruct((LP, BLK), F32), vec, vec, vec),
        grid=(NBLK,),
        in_specs=[rev(SSD_W), rev(256, 4), rev(256, 5), rev(N_HEADS),
                  pl.BlockSpec((N_HEADS, BLK), lambda s: (0, NBLK - 1 - s)), _const_spec((1, N_HEADS)),
                  _const_spec((N_HEADS, 1)), _const_spec((1, N_HEADS)), _const_spec((N_HEADS, 1)),
                  _const_spec((1, SSD_W)),
                  pl.BlockSpec((1, SSD_W, N_STATE), lambda s: (NBLK - 1 - s, 0, 0)), rev(SSD_W)],
        out_specs=(rev(XBC_W), rev(BLK), _const_spec((1, N_HEADS)),
                   _const_spec((1, N_HEADS)), _const_spec((1, N_HEADS))),
        scratch_shapes=[pltpu.VMEM((SSD_W, N_STATE), F32)],
        compiler_params=_params(("arbitrary",)),
    )(xa, xa, xa, dt_raw, dt_raw_t, dt_bias, dt_bias_t, a_log, a_log_t, d_exp, hstart, dy)


def _dot_x_exact_tn(x_t, eye):
    out = None
    for p in _split(x_t, 3):
        t = _dot(p, eye, TN)
        out = t if out is None else out + t
    return out


def _adamw(name, parts, w, m, v, rows):
    r_all, cols = w.shape
    assert r_all % rows == 0
    c1 = 1.0 / (1.0 - ADAM_B1 ** ADAM_STEP)
    c2 = 1.0 / (1.0 - ADAM_B2 ** ADAM_STEP)

    def body(p_ref, w_ref, m_ref, v_ref, g_ref, d_ref, mo_ref, vo_ref):
        g = p_ref[0].astype(F32)
        for j in range(1, parts.shape[0]):
            g = g + p_ref[j].astype(F32)
        mn = ADAM_B1 * m_ref[...] + (1.0 - ADAM_B1) * g
        vn = ADAM_B2 * v_ref[...] + (1.0 - ADAM_B2) * (g * g)
        g_ref[...] = g
        mo_ref[...] = mn
        vo_ref[...] = vn
        d_ref[...] = -ADAM_LR * ((mn * c1) / (jnp.sqrt(vn * c2) + ADAM_EPS) + ADAM_WD * w_ref[...])

    spec = pl.BlockSpec((rows, cols), lambda i: (i, 0))
    shp = jax.ShapeDtypeStruct((r_all, cols), F32)
    return pl.pallas_call(
        body, name=name, out_shape=(shp, shp, shp, shp), grid=(r_all // rows,),
        in_specs=[pl.BlockSpec((parts.shape[0], rows, cols), lambda i: (0, i, 0)), spec, spec, spec],
        out_specs=(spec, spec, spec, spec),
        compiler_params=_params(("parallel",)),
    )(parts, w, m, v)


_VECTORS = (("norm_w", 1024, 0), ("conv_b", 1536, 8), ("dt_bias", 16, 20), ("a_log", 16, 21), ("d_skip", 16, 22),
            ("sb_norm_w", 1024, 24), ("ssd_norm_w", 1024, 32), ("final_norm_w", 1024, 40))
_LOSS_ROW = 23
_CONVW_ROW = 48
_META_ROW = 96
_PACK_ROWS = 224
_SMALL_ORDER = tuple(name for name, _, _ in _VECTORS) + ("conv_w", "meta_tokens")


def _pack_small_grads(vectors, loss_row, d_convw, dh):
    def body(*refs):
        vec_refs, (loss_ref, cw_ref, dh_ref, out_ref) = refs[:len(_VECTORS)], refs[len(_VECTORS):]
        out_ref[...] = jnp.zeros_like(out_ref)
        for (_, width, row), ref in zip(_VECTORS, vec_refs):
            if width < BLK:
                out_ref[row:row + 1, :width] = ref[...]
            else:
                for t in range(width // BLK):
                    out_ref[row + t:row + t + 1, :] = ref[:, t * BLK:(t + 1) * BLK]
        out_ref[_LOSS_ROW:_LOSS_ROW + 1, :] = loss_ref[...]
        for k in range(4):
            for t in range(XBC_W // BLK):
                r = _CONVW_ROW + k * (XBC_W // BLK) + t
                out_ref[r:r + 1, :] = cw_ref[k:k + 1, t * BLK:(t + 1) * BLK]
        for i in range(N_META):
            for t in range(D_MODEL // BLK):
                r = _META_ROW + i * (D_MODEL // BLK) + t
                out_ref[r:r + 1, :] = dh_ref[i:i + 1, t * BLK:(t + 1) * BLK]

    full = lambda a: pl.BlockSpec(a.shape, lambda i: tuple(0 for _ in a.shape))
    return pl.pallas_call(
        body, name="pack_small_grads", out_shape=jax.ShapeDtypeStruct((_PACK_ROWS, BLK), F32), grid=(1,),
        in_specs=[full(v) for v in vectors] + [full(loss_row), full(d_convw),
                                               pl.BlockSpec((N_META, D_MODEL), lambda i: (PAD // N_META, 0))],
        out_specs=pl.BlockSpec((_PACK_ROWS, BLK), lambda i: (0, 0)),
        compiler_params=_params(("arbitrary",)),
    )(*vectors, loss_row, d_convw, dh)


def _sum_slots(name, parts, rows):
    n, r_all, cols = parts.shape

    def body(p_ref, o_ref):
        acc = p_ref[0].astype(F32)
        for j in range(1, n):
            acc = acc + p_ref[j].astype(F32)
        o_ref[...] = acc

    return pl.pallas_call(
        body, name=name, out_shape=jax.ShapeDtypeStruct((r_all, cols), F32), grid=(r_all // rows,),
        in_specs=[pl.BlockSpec((n, rows, cols), lambda i: (0, i, 0))],
        out_specs=pl.BlockSpec((rows, cols), lambda i: (i, 0)),
        compiler_params=_params(("parallel",)),
    )(parts)


def _add(name, a, b):
    def body(a_ref, b_ref, o_ref):
        o_ref[...] = a_ref[...] + b_ref[...]

    spec = pl.BlockSpec(a.shape, lambda i: (0, 0))
    return pl.pallas_call(body, name=name, out_shape=jax.ShapeDtypeStruct(a.shape, a.dtype), grid=(1,),
                          in_specs=[spec, spec], out_specs=spec, compiler_params=_params(("arbitrary",)))(a, b)


def _adamw_small(chip_sums, weights, moms, vels):
    c1 = 1.0 / (1.0 - ADAM_B1 ** ADAM_STEP)
    c2 = 1.0 / (1.0 - ADAM_B2 ** ADAM_STEP)
    n = len(_SMALL_ORDER)

    def body(*refs):
        p_ref = refs[0]
        w_refs, m_refs, v_refs = refs[1:1 + n], refs[1 + n:1 + 2 * n], refs[1 + 2 * n:1 + 3 * n]
        outs = refs[1 + 3 * n:1 + 7 * n]
        loss_ref, g_ref, cw_ref, cws_ref, mt_ref = refs[1 + 7 * n:]
        x, y, c = _mesh_position()
        me = 4 * x + 2 * y + c
        g = p_ref[0]
        for j in range(1, 4):
            g = g + p_ref[j]
        g_ref[...] = g
        loss_ref[...] = g_ref[_LOSS_ROW:_LOSS_ROW + 1, :]

        def update(idx, grad):
            go_ref, d_ref, mo_ref, vo_ref = outs[4 * idx:4 * idx + 4]
            mn = ADAM_B1 * m_refs[idx][...] + (1.0 - ADAM_B1) * grad
            vn = ADAM_B2 * v_refs[idx][...] + (1.0 - ADAM_B2) * (grad * grad)
            go_ref[...] = grad
            mo_ref[...] = mn
            vo_ref[...] = vn
            d_ref[...] = -ADAM_LR * ((mn * c1) / (jnp.sqrt(vn * c2) + ADAM_EPS) + ADAM_WD * w_refs[idx][...])

        for idx, (_, width, row) in enumerate(_VECTORS):
            go_ref = outs[4 * idx]
            if width < BLK:
                grad = g_ref[row:row + 1, :width]
            else:
                for t in range(width // BLK):
                    go_ref[:, t * BLK:(t + 1) * BLK] = g_ref[row + t:row + t + 1, :]
                grad = go_ref[...]
            update(idx, grad)
        cw_ref[...] = jnp.zeros_like(cw_ref)
        for k in range(4):
            for t in range(XBC_W // BLK):
                r = _CONVW_ROW + k * (XBC_W // BLK) + t
                cw_ref[k:k + 1, t * BLK:(t + 1) * BLK] = g_ref[r:r + 1, :]
        for i in range(N_META):
            for t in range(D_MODEL // BLK):
                r = _META_ROW + i * (D_MODEL // BLK) + t
                mt_ref[i:i + 1, t * BLK:(t + 1) * BLK] = g_ref[r:r + 1, :]
        width_cw = XBC_W // N_DEV
        pick_cw = (_iota((XBC_W, width_cw), 0) == me * width_cw + _iota((XBC_W, width_cw), 1)).astype(BF16)
        cws_ref[...] = _dot_x_exact(cw_ref[...], pick_cw)
        update(n - 2, cws_ref[0:4, :])
        pick_mt = (_iota((D_MODEL, BLK), 0) == me * BLK + _iota((D_MODEL, BLK), 1)).astype(BF16)
        update(n - 1, _dot_x_exact(mt_ref[...], pick_mt))

    full = lambda a: pl.BlockSpec(a.shape, lambda i: tuple(0 for _ in a.shape))
    params = list(weights) + list(moms) + list(vels)
    out_shape, out_specs = [], []
    for w in weights:
        for _ in range(4):
            out_shape.append(jax.ShapeDtypeStruct(w.shape, F32))
            out_specs.append(full(w))
    out_shape.append(jax.ShapeDtypeStruct((1, BLK), F32))
    out_specs.append(pl.BlockSpec((1, BLK), lambda i: (0, 0)))
    return pl.pallas_call(
        body, name="adamw_small", out_shape=tuple(out_shape), grid=(1,),
        in_specs=[full(chip_sums)] + [full(a) for a in params], out_specs=tuple(out_specs),
        scratch_shapes=[pltpu.VMEM((_PACK_ROWS, BLK), F32), pltpu.VMEM((8, XBC_W), F32),
                        pltpu.VMEM((8, XBC_W // N_DEV), F32), pltpu.VMEM((N_META, D_MODEL), F32)],
        compiler_params=_params(("arbitrary",)),
    )(chip_sums, *params)


def kernel(x, meta_tokens, norm_w, w_in, conv_w, conv_b, dt_bias, a_log, d_skip, sb_norm_w, ssd_norm_w, w_out, final_norm_w, loss_target, m_meta_tokens, m_norm_w, m_w_in, m_conv_w, m_conv_b, m_dt_bias, m_a_log, m_d_skip, m_sb_norm_w, m_ssd_norm_w, m_w_out, m_final_norm_w, v_meta_tokens, v_norm_w, v_w_in, v_conv_w, v_conv_b, v_dt_bias, v_a_log, v_d_skip, v_sb_norm_w, v_ssd_norm_w, v_w_out, v_final_norm_w):
    small_src = jnp.concatenate([conv_w[0].reshape(6, BLK), meta_tokens, jnp.zeros((2, BLK), F32)], axis=0)
    small_g, w_out_g, w_in_g = _gather_weights([small_src, w_out[0].astype(BF16), w_in[0].astype(BF16)])
    w_in_full = w_in_g.transpose(1, 0, 2).reshape(D_MODEL, D_IN)
    w_main = w_in_full[:, :D_MAIN]
    w_dt = w_in_full[:, D_MAIN:]
    w_out_full = w_out_g.reshape(2 * SSD_W, D_MODEL)
    conv_w_full = small_g[:, :6].reshape(N_DEV, 4, 192).transpose(1, 0, 2).reshape(4, XBC_W)
    meta_full = small_g[:, 6:6 + N_META].transpose(1, 0, 2).reshape(N_META, D_MODEL)
    h_pad = jnp.concatenate([jnp.zeros((PAD, D_MODEL), F32), meta_full, x[0]], axis=0)
    dt_bias_t = dt_bias.reshape(N_HEADS, 1)
    a_log_t = a_log.reshape(N_HEADS, 1)
    d_exp = jnp.repeat(d_skip, HEAD_DIM, axis=1)
    fnw = final_norm_w.reshape(1, D_MODEL)

    u, dt_raw, dt_raw_t = _prenorm(h_pad, norm_w, w_dt, w_dt.T)
    proj = _matmul("in_proj", u, w_main, "nn", 1088, 512, D_MODEL)
    o_sb, o_sb_exact = _sb_attention_fwd(proj)
    xa = _conv_fwd(proj, conv_w_full, conv_b)
    o_ssd, hstart = _ssd_fwd(xa, dt_raw, dt_raw_t, dt_bias, dt_bias_t, a_log, a_log_t, d_exp)
    ycat = _ycat(o_sb, proj, o_ssd, sb_norm_w, ssd_norm_w)
    yo = _matmul("out_proj", ycat, w_out_full, "nn", 1088, 512, 2 * SSD_W)
    dh2, dh2_b, loss_row, d_fnw = _loss_head(h_pad, yo, fnw, loss_target[0])

    g_w_out = _matmul("d_w_out", ycat, dh2_b, "tn", 512, 512, LP, BF16).reshape(4, 2, 256, D_MODEL)
    chip_w_out = _pair_sum("pair_sum_w_out", g_w_out, _swap_with_sibling("swap_w_out", g_w_out), 256)
    dycat = _matmul("d_ycat", dh2_b, w_out_full, "nt", 1088, 512, D_MODEL)
    do_sb, dg, do_ssd, dz, d_sbw, d_ssdw = _ycat_bwd(dycat, o_sb, proj, o_ssd, sb_norm_w, ssd_norm_w)
    dq, dk, dv, p_w_out = _sb_attention_bwd(proj, o_sb_exact, do_sb, chip_w_out)
    dxa, ddt_raw, d_dtb, d_alog, d_dskip = _ssd_bwd(
        xa, dt_raw, dt_raw_t, dt_bias, dt_bias_t, a_log, a_log_t, d_exp, hstart, do_ssd)
    dxbc, d_convw, d_convb = _conv_bwd(dxa, proj, conv_w_full, conv_b)
    pieces = [dq, dk, dv, dg, dz, dxbc, ddt_raw]
    g_w_in = _d_w_in(u, pieces).reshape(1, 2, D_MODEL // 2, W_IN_PAD)
    chip_w_in = _pair_sum("pair_sum_w_in", g_w_in, _swap_with_sibling("swap_w_in", g_w_in), 128)
    dh, d_nw, win_parts = _d_u_prenorm_bwd(pieces, w_main, w_dt, h_pad, norm_w, dh2, chip_w_in[0])
    win_mine = _sum_slots("sum_w_in_windows", win_parts, 128)
    win_other = _swap_with_sibling("swap_w_in_window", win_mine, whole=True)
    core = lax.axis_index("c")
    chip = 2 * lax.axis_index("x") + lax.axis_index("y")
    first_col = (D_IN // N_DEV) * (2 * chip + core) - WIN_STRIDE * chip
    cut = lambda w: lax.dynamic_slice(w, (0, first_col), (D_MODEL // 2, D_IN // N_DEV))
    half_mine, half_other = cut(win_mine), cut(win_other)
    p_w_in = jnp.concatenate([jnp.where(core == 0, half_mine, half_other),
                              jnp.where(core == 0, half_other, half_mine)], axis=0)[None]

    pack = _pack_small_grads([d_nw, d_convb, d_dtb, d_alog, d_dskip, d_sbw, d_ssdw, d_fnw], loss_row, d_convw, dh)
    chip_small = _add("pair_sum_small", pack, _swap_with_sibling("swap_small", pack, whole=True))
    p_small = _exchange_between_chips("exchange_small", chip_small, whole=True)

    res_in = _adamw("adamw_w_in", p_w_in, w_in[0], m_w_in[0], v_w_in[0], 128)
    res_out = _adamw("adamw_w_out", p_w_out, w_out[0], m_w_out[0], v_w_out[0], 128)
    res_small = _adamw_small(
        p_small,
        [norm_w, conv_b, dt_bias, a_log, d_skip, sb_norm_w, ssd_norm_w, fnw, conv_w[0], meta_tokens],
        [m_norm_w, m_conv_b, m_dt_bias, m_a_log, m_d_skip, m_sb_norm_w, m_ssd_norm_w,
         m_final_norm_w.reshape(1, D_MODEL), m_conv_w[0], m_meta_tokens],
        [v_norm_w, v_conv_b, v_dt_bias, v_a_log, v_d_skip, v_sb_norm_w, v_ssd_norm_w,
         v_final_norm_w.reshape(1, D_MODEL), v_conv_w[0], v_meta_tokens])

    loss = jnp.sum(res_small[-1])
    order = ["meta_tokens", "norm_w", "w_in", "conv_w", "conv_b", "dt_bias", "a_log", "d_skip",
             "sb_norm_w", "ssd_norm_w", "w_out", "final_norm_w"]
    outs = [loss, dh[BLK:].reshape(1, SEQ, D_MODEL)]
    for kind in range(4):
        small = {name: res_small[4 * idx + kind] for idx, name in enumerate(_SMALL_ORDER)}
        small["final_norm_w"] = small["final_norm_w"].reshape(D_MODEL)
        small["conv_w"] = small["conv_w"].reshape(1, 4, XBC_W // N_DEV)
        small["w_in"] = res_in[kind].reshape(1, D_MODEL, D_IN // N_DEV)
        small["w_out"] = res_out[kind].reshape(1, 256, D_MODEL)
        outs += [small[name] for name in order]
    return tuple(outs)
```

```python
import functools

import jax
import jax.numpy as jnp
from jax import lax
from jax.experimental import pallas as pl
from jax.experimental.pallas import tpu as pltpu

F32 = jnp.float32
BF16 = jnp.bfloat16

D_MODEL = 1024
SEQ = 2048
N_META = 16
BLK = 128
PAD = BLK - N_META
LP = PAD + N_META + SEQ
NBLK = LP // BLK
N_HEADS = 16
HEAD_DIM = 64
N_GROUPS = 2
HEADS_PER_GROUP = 8
N_STATE = 128
SSD_W = 1024
XBC_W = 1536
D_MAIN = 6656
D_IN = 6672
COL_Q, COL_K, COL_V, COL_G, COL_Z, COL_XBC = 0, 1024, 2048, 3072, 4096, 5120
N_DEV = 8
EPS = 1e-5
SB_SCALE = 0.125
SB_DEAD = -87.4
SB_MASKED = -1e30

ADAM_LR = 0.001
ADAM_B1 = 0.9
ADAM_B2 = 0.999
ADAM_EPS = 1e-08
ADAM_WD = 0.01
ADAM_STEP = 10

VMEM_LIMIT = 48 * 1024 * 1024

NN = (((1,), (0,)), ((), ()))
NT = (((1,), (1,)), ((), ()))
TN = (((0,), (0,)), ((), ()))


def _dot(a, b, dims=NN):
    return lax.dot_general(a, b, dims, preferred_element_type=F32)


def _split(x, n):
    parts = []
    r = x
    for i in range(n):
        p = r.astype(BF16)
        parts.append(p)
        if i + 1 < n:
            r = r - p.astype(F32)
    return parts


def _dot_x_exact(x, m, dims=NN, n=3):
    out = None
    for p in _split(x, n):
        t = _dot(p, m, dims)
        out = t if out is None else out + t
    return out


def _dot_exact_x(m, x, dims=NN, n=3):
    out = None
    for p in _split(x, n):
        t = _dot(m, p, dims)
        out = t if out is None else out + t
    return out


def _iota(shape, dim):
    return lax.broadcasted_iota(jnp.int32, shape, dim)


def _softplus(x):
    return jnp.maximum(x, 0.0) + jnp.log(1.0 + jnp.exp(-jnp.abs(x)))


def _sigmoid(x):
    return 1.0 / (1.0 + jnp.exp(-x))


def _params(sem=None):
    return pltpu.CompilerParams(dimension_semantics=sem, vmem_limit_bytes=VMEM_LIMIT)


_ANY = pl.BlockSpec(memory_space=pl.ANY)
_MESH = pl.DeviceIdType.MESH


def _mesh_position():
    return lax.axis_index("x"), lax.axis_index("y"), lax.axis_index("c")


def _other_chips(x, y):
    return [(1 - x, y), (x, 1 - y), (1 - x, 1 - y)]


def _gather_weights(srcs):
    n = len(srcs)

    def body(*refs):
        src, out = refs[:n], refs[n:2 * n]
        send_sems, recv_sems, loc_sems = refs[2 * n:]
        x, y, c = _mesh_position()
        sibling = (x, y, 1 - c)
        chips = _other_chips(x, y)
        relay_from = (jnp.where(c == 0, 1 - x, x), jnp.where(c == 0, y, 1 - y))
        relay_to = (jnp.where(c == 0, x, 1 - x), jnp.where(c == 0, 1 - y, y))

        def copy(a, k, block, to, from_src=False):
            slot = out[a].at[4 * block[0] + 2 * block[1] + block[2]]
            return pltpu.make_async_remote_copy(
                src_ref=src[a] if from_src else slot, dst_ref=slot,
                send_sem=send_sems.at[7 * a + k], recv_sem=recv_sems.at[7 * a + k],
                device_id=to, device_id_type=_MESH)

        local, sends = [], []
        for a in range(n):
            mine = pltpu.make_async_copy(src[a], out[a].at[4 * x + 2 * y + c], loc_sems.at[a])
            mine.start()
            local.append(mine)
            first = [copy(a, 0, (x, y, c), sibling, True)]
            first += [copy(a, 1 + j, (x, y, c), (*chip, c), True) for j, chip in enumerate(chips[:2])]
            for cp in first:
                cp.start()
            sends += first
        for a in range(n):
            for j, chip in enumerate(chips[:2]):
                copy(a, 1 + j, (*chip, c), (x, y, c)).wait_recv()
            later = [copy(a, 3, (*relay_from, c), (*relay_to, c))]
            later += [copy(a, 4 + j, (*chip, c), sibling) for j, chip in enumerate(chips[:2])]
            for cp in later:
                cp.start()
            sends += later
        for a in range(n):
            copy(a, 3, (*chips[2], c), (x, y, c)).wait_recv()
            passed = copy(a, 6, (*chips[2], c), sibling)
            passed.start()
            sends.append(passed)
        for a in range(n):
            copy(a, 0, (x, y, 1 - c), (x, y, c)).wait_recv()
            for j, chip in enumerate(chips):
                copy(a, 4 + j, (*chip, 1 - c), (x, y, c)).wait_recv()
        for cp in sends:
            cp.wait_send()
        for cp in local:
            cp.wait()

    return pl.pallas_call(
        body, name="gather_weights",
        out_shape=tuple(jax.ShapeDtypeStruct((N_DEV,) + s.shape, s.dtype) for s in srcs),
        in_specs=[_ANY] * n, out_specs=tuple([_ANY] * n),
        scratch_shapes=[pltpu.SemaphoreType.DMA((7 * n,)), pltpu.SemaphoreType.DMA((7 * n,)),
                        pltpu.SemaphoreType.DMA((n,))],
    )(*srcs)


_ALL_GATHER_SEMS = [pltpu.SemaphoreType.DMA((7,)), pltpu.SemaphoreType.DMA((7,)), pltpu.SemaphoreType.DMA]


def _all_gather_copies(src, out, send_sems, recv_sems, loc_sem):
    x, y, c = _mesh_position()
    me = 4 * x + 2 * y + c
    copies = [pltpu.make_async_copy(src, out.at[me], loc_sem)]
    for k in range(1, N_DEV):
        peer = (1 - x if k & 4 else x, 1 - y if k & 2 else y, 1 - c if k & 1 else c)
        copies.append(pltpu.make_async_remote_copy(
            src_ref=src, dst_ref=out.at[me], send_sem=send_sems.at[k - 1], recv_sem=recv_sems.at[k - 1],
            device_id=peer, device_id_type=_MESH))
    return copies


def _swap_with_sibling(name, src, whole=False):
    n = 1 if whole else src.shape[0]

    def body(src_ref, out_ref, send_sems, recv_sems):
        x, y, c = _mesh_position()
        copies = []
        for k in range(n):
            copies.append(pltpu.make_async_remote_copy(
                src_ref=src_ref if whole else src_ref.at[k, 1 - c], dst_ref=out_ref if whole else out_ref.at[k],
                send_sem=send_sems.at[k], recv_sem=recv_sems.at[k], device_id=(x, y, 1 - c), device_id_type=_MESH))
        for cp in copies:
            cp.start()
        for cp in copies:
            cp.wait()

    shape = src.shape if whole else (src.shape[0],) + src.shape[2:]
    return pl.pallas_call(
        body, name=name, out_shape=jax.ShapeDtypeStruct(shape, src.dtype), in_specs=[_ANY], out_specs=_ANY,
        scratch_shapes=[pltpu.SemaphoreType.DMA((n,)), pltpu.SemaphoreType.DMA((n,))],
    )(src)


def _pair_sum(name, g, sib, rows):
    n, _, r_all, cols = g.shape
    assert r_all % rows == 0

    def body(g0_ref, g1_ref, s_ref, o_ref):
        c = lax.axis_index("c")
        mine = jnp.where(c == 0, g0_ref[0, 0].astype(F32), g1_ref[0, 0].astype(F32))
        o_ref[0] = (mine + s_ref[0].astype(F32)).astype(o_ref.dtype)

    return pl.pallas_call(
        body, name=name, out_shape=jax.ShapeDtypeStruct((n, r_all, cols), BF16), grid=(n, r_all // rows),
        in_specs=[pl.BlockSpec((1, 1, rows, cols), lambda k, i: (k, 0, i, 0)),
                  pl.BlockSpec((1, 1, rows, cols), lambda k, i: (k, 1, i, 0)),
                  pl.BlockSpec((1, rows, cols), lambda k, i: (k, i, 0))],
        out_specs=pl.BlockSpec((1, rows, cols), lambda k, i: (k, i, 0)),
        compiler_params=_params(("parallel", "parallel")),
    )(g, g, sib)


_CHIP_EXCHANGE_SEMS = [pltpu.SemaphoreType.DMA((3,)), pltpu.SemaphoreType.DMA((3,)), pltpu.SemaphoreType.DMA]


def _chip_exchange_copies(src, out, send_sems, recv_sems, loc_sem, whole=False, window=None):
    x, y, c = _mesh_position()
    here = 2 * x + y

    def slot(k):
        if whole:
            return src
        if window is not None:
            return src.at[:, pl.ds(pl.multiple_of(k * window[0], BLK), window[1])]
        return src.at[k]

    copies = [pltpu.make_async_copy(slot(here), out.at[here], loc_sem)]
    for j, chip in enumerate(_other_chips(x, y)):
        copies.append(pltpu.make_async_remote_copy(
            src_ref=slot(2 * chip[0] + chip[1]), dst_ref=out.at[here],
            send_sem=send_sems.at[j], recv_sem=recv_sems.at[j], device_id=(*chip, c), device_id_type=_MESH))
    return copies


def _exchange_between_chips(name, src, whole=False):
    def body(src_ref, out_ref, send_sems, recv_sems, loc_sem):
        copies = _chip_exchange_copies(src_ref, out_ref, send_sems, recv_sems, loc_sem, whole)
        for cp in copies:
            cp.start()
        for cp in copies:
            cp.wait()

    return pl.pallas_call(
        body, name=name, out_shape=jax.ShapeDtypeStruct(((4,) + src.shape) if whole else src.shape, src.dtype),
        in_specs=[_ANY], out_specs=_ANY, scratch_shapes=_CHIP_EXCHANGE_SEMS,
    )(src)


def _matmul(name, a, b, kind, tm, tn, tk, out_dtype=F32):
    if kind == "nn":
        (m, kk), (_, nn_) = a.shape, b.shape
        a_spec = pl.BlockSpec((tm, tk), lambda i, j, k: (i, k))
        b_spec = pl.BlockSpec((tk, tn), lambda i, j, k: (k, j))
        dims = NN
    elif kind == "nt":
        (m, kk), (nn_, _) = a.shape, b.shape
        a_spec = pl.BlockSpec((tm, tk), lambda i, j, k: (i, k))
        b_spec = pl.BlockSpec((tn, tk), lambda i, j, k: (j, k))
        dims = NT
    else:
        (kk, m), (_, nn_) = a.shape, b.shape
        a_spec = pl.BlockSpec((tk, tm), lambda i, j, k: (k, i))
        b_spec = pl.BlockSpec((tk, tn), lambda i, j, k: (k, j))
        dims = TN
    assert m % tm == 0 and nn_ % tn == 0 and kk % tk == 0
    nk = kk // tk

    def body(a_ref, b_ref, o_ref, acc_ref):
        k = pl.program_id(2)
        part = _dot(a_ref[...], b_ref[...], dims)
        if nk == 1:
            o_ref[...] = part.astype(o_ref.dtype)
        else:
            @pl.when(k == 0)
            def _():
                acc_ref[...] = part

            @pl.when(k > 0)
            def _():
                acc_ref[...] += part

            @pl.when(k == nk - 1)
            def _():
                o_ref[...] = acc_ref[...].astype(o_ref.dtype)

    return pl.pallas_call(
        body, name=name, out_shape=jax.ShapeDtypeStruct((m, nn_), out_dtype),
        grid=(m // tm, nn_ // tn, nk),
        in_specs=[a_spec, b_spec], out_specs=pl.BlockSpec((tm, tn), lambda i, j, k: (i, j)),
        scratch_shapes=[pltpu.VMEM((tm, tn) if nk > 1 else (8, 128), F32)],
        compiler_params=_params(("parallel", "parallel", "arbitrary")),
    )(a, b)


def _row_spec(width, col=0):
    return pl.BlockSpec((BLK, width), lambda i: (i, col))


def _const_spec(shape):
    return pl.BlockSpec(shape, lambda i: tuple(0 for _ in shape))


def _prenorm(h_pad, norm_w, wdt, wdt_t):
    def body(h_ref, w_ref, wdt_ref, wdtt_ref, u_ref, dt_ref, dtt_ref):
        xv = h_ref[...]
        r = lax.rsqrt(jnp.mean(xv * xv, axis=-1, keepdims=True) + EPS)
        u = (xv * r * w_ref[...]).astype(BF16)
        u_ref[...] = u
        dt_ref[...] = _dot(u, wdt_ref[...], NN)
        dtt_ref[...] = _dot(wdtt_ref[...], u, NT)

    return pl.pallas_call(
        body, name="prenorm",
        out_shape=(jax.ShapeDtypeStruct((LP, D_MODEL), BF16), jax.ShapeDtypeStruct((LP, N_HEADS), F32),
                   jax.ShapeDtypeStruct((N_HEADS, LP), F32)),
        grid=(NBLK,),
        in_specs=[_row_spec(D_MODEL), _const_spec((1, D_MODEL)), _const_spec((D_MODEL, N_HEADS)),
                  _const_spec((N_HEADS, D_MODEL))],
        out_specs=(_row_spec(D_MODEL), _row_spec(N_HEADS), pl.BlockSpec((N_HEADS, BLK), lambda i: (0, i))),
        compiler_params=_params(("parallel",)),
    )(h_pad, norm_w, wdt, wdt_t)


def _gated_norm(o, g, w):
    a = o * (g * _sigmoid(g))
    r = lax.rsqrt(jnp.mean(a * a, axis=-1, keepdims=True) + EPS)
    return a * r * w


def _ycat(o_sb, proj, o_ssd, sb_w, ssd_w):
    def body(osb_ref, g_ref, ossd_ref, z_ref, sbw_ref, ssdw_ref, y_ref):
        y_ref[:, :SSD_W] = _gated_norm(osb_ref[...], g_ref[...], sbw_ref[...]).astype(BF16)
        y_ref[:, SSD_W:] = _gated_norm(ossd_ref[...], z_ref[...], ssdw_ref[...]).astype(BF16)

    return pl.pallas_call(
        body, name="ycat", out_shape=jax.ShapeDtypeStruct((LP, 2 * SSD_W), BF16), grid=(NBLK,),
        in_specs=[_row_spec(1024), _row_spec(1024, COL_G // 1024), _row_spec(1024),
                  _row_spec(1024, COL_Z // 1024), _const_spec((1, 1024)), _const_spec((1, 1024))],
        out_specs=_row_spec(2 * SSD_W),
        compiler_params=_params(("parallel",)),
    )(o_sb, proj, o_ssd, proj, sb_w, ssd_w)


def _loss_head(h_pad, yo, fnw, target):
    def body(h_ref, yo_ref, w_ref, t_ref, dh2_ref, dh2b_ref, loss_ref, dw_ref):
        i = pl.program_id(0)

        @pl.when(i == 0)
        def _():
            loss_ref[...] = jnp.zeros_like(loss_ref)
            dw_ref[...] = jnp.zeros_like(dw_ref)

        h2 = h_ref[...] + yo_ref[...]
        r = lax.rsqrt(jnp.mean(h2 * h2, axis=-1, keepdims=True) + EPS)
        nrm = h2 * r
        w = w_ref[...]
        live = i > 0
        err = jnp.where(live, nrm * w - t_ref[...], 0.0)
        dout = err * (1.0 / D_MODEL)
        loss_ref[...] += (0.5 / D_MODEL) * _fold_lanes(jnp.sum(err * err, axis=0, keepdims=True))
        dw_ref[...] += jnp.sum(dout * nrm, axis=0, keepdims=True)
        wd = dout * w
        dh2 = r * (wd - nrm * jnp.mean(wd * nrm, axis=-1, keepdims=True))
        dh2_ref[...] = dh2
        dh2b_ref[...] = dh2.astype(BF16)

    return pl.pallas_call(
        body, name="loss_head",
        out_shape=(jax.ShapeDtypeStruct((LP, D_MODEL), F32), jax.ShapeDtypeStruct((LP, D_MODEL), BF16),
                   jax.ShapeDtypeStruct((1, BLK), F32), jax.ShapeDtypeStruct((1, D_MODEL), F32)),
        grid=(NBLK,),
        in_specs=[_row_spec(D_MODEL), _row_spec(D_MODEL), _const_spec((1, D_MODEL)),
                  pl.BlockSpec((BLK, D_MODEL), lambda i: (jnp.maximum(i - 1, 0), 0))],
        out_specs=(_row_spec(D_MODEL), _row_spec(D_MODEL), _const_spec((1, BLK)), _const_spec((1, D_MODEL))),
        compiler_params=_params(("arbitrary",)),
    )(h_pad, yo, fnw, target)


def _fold_lanes(row):
    out = row[:, :BLK]
    for j in range(1, row.shape[1] // BLK):
        out = out + row[:, j * BLK:(j + 1) * BLK]
    return out


def _gated_norm_bwd(dy, o, g, w):
    s = _sigmoid(g)
    sg = g * s
    a = o * sg
    r = lax.rsqrt(jnp.mean(a * a, axis=-1, keepdims=True) + EPS)
    nrm = a * r
    dw = jnp.sum(dy * nrm, axis=0, keepdims=True)
    wd = dy * w
    da = r * (wd - nrm * jnp.mean(wd * nrm, axis=-1, keepdims=True))
    return da * sg, da * o * (s * (1.0 + g * (1.0 - s))), dw


def _ycat_bwd(dycat, o_sb, proj, o_ssd, sb_w, ssd_w):
    def body(dy_ref, osb_ref, g_ref, ossd_ref, z_ref, sbw_ref, ssdw_ref,
             dosb_ref, dg_ref, dossd_ref, dz_ref, dsbw_ref, dssdw_ref):
        @pl.when(pl.program_id(0) == 0)
        def _():
            dsbw_ref[...] = jnp.zeros_like(dsbw_ref)
            dssdw_ref[...] = jnp.zeros_like(dssdw_ref)

        do, dg, dw = _gated_norm_bwd(dy_ref[:, :SSD_W], osb_ref[...], g_ref[...], sbw_ref[...])
        dosb_ref[...] = do
        dg_ref[...] = dg.astype(BF16)
        dsbw_ref[...] += dw
        do, dg, dw = _gated_norm_bwd(dy_ref[:, SSD_W:], ossd_ref[...], z_ref[...], ssdw_ref[...])
        dossd_ref[...] = do
        dz_ref[...] = dg.astype(BF16)
        dssdw_ref[...] += dw

    act = jax.ShapeDtypeStruct((LP, 1024), F32)
    gate = jax.ShapeDtypeStruct((LP, 1024), BF16)
    vec = jax.ShapeDtypeStruct((1, 1024), F32)
    return pl.pallas_call(
        body, name="ycat_bwd", out_shape=(act, gate, act, gate, vec, vec), grid=(NBLK,),
        in_specs=[_row_spec(2048), _row_spec(1024), _row_spec(1024, COL_G // 1024), _row_spec(1024),
                  _row_spec(1024, COL_Z // 1024), _const_spec((1, 1024)), _const_spec((1, 1024))],
        out_specs=(_row_spec(1024), _row_spec(1024), _row_spec(1024), _row_spec(1024),
                   _const_spec((1, 1024)), _const_spec((1, 1024))),
        compiler_params=_params(("arbitrary",)),
    )(dycat, o_sb, proj, o_ssd, proj, sb_w, ssd_w)


_DPROJ_PIECES = (("dq", COL_Q, 1024), ("dk", COL_K, 1024), ("dv", COL_V, 1024), ("dg", COL_G, 1024),
                 ("dz", COL_Z, 1024), ("dxbc", COL_XBC, XBC_W), ("ddt", D_MAIN, BLK))
W_IN_PAD = D_MAIN + 512
WIN_STRIDE = 13 * BLK
WIN_WIDTH = 14 * BLK


def _d_w_in(u, pieces):
    tm, tn = 512, 512
    nj = W_IN_PAD // tn
    main = _DPROJ_PIECES[:-1]

    def body(*refs):
        u_ref, piece_refs, (ddt_ref, o_ref) = refs[0], refs[1:1 + len(main)], refs[1 + len(main):]
        j = pl.program_id(1)
        a = u_ref[...]
        for (_, col, width), ref in zip(main, piece_refs):
            @pl.when(jnp.logical_and(j >= col // tn, j < (col + width) // tn))
            def _():
                o_ref[...] = _dot(a, ref[...], TN).astype(BF16)

        @pl.when(j == nj - 1)
        def _():
            o_ref[...] = jnp.zeros_like(o_ref)
            o_ref[:, :BLK] = _dot(a, ddt_ref[...].astype(BF16), TN).astype(BF16)

    def piece_spec(col, width):
        return pl.BlockSpec((LP, tn), lambda i, j: (0, jnp.clip(j - col // tn, 0, width // tn - 1)))

    return pl.pallas_call(
        body, name="d_w_in", out_shape=jax.ShapeDtypeStruct((D_MODEL, W_IN_PAD), BF16), grid=(D_MODEL // tm, nj),
        in_specs=[pl.BlockSpec((LP, tm), lambda i, j: (0, i))] + [piece_spec(col, width) for _, col, width in main]
        + [pl.BlockSpec((LP, BLK), lambda i, j: (0, 0))],
        out_specs=pl.BlockSpec((tm, tn), lambda i, j: (i, j)),
        compiler_params=_params(("parallel", "arbitrary")),
    )(u, *pieces)


def _d_u_prenorm_bwd(pieces, w_main, wdt, h_pad, norm_w, dh2, chip_sum):
    tm, tk = LP // 4, 512
    nk = D_MAIN // tk
    main = _DPROJ_PIECES[:-1]
    window = (WIN_STRIDE, WIN_WIDTH)

    def body(*refs):
        piece_refs = refs[:len(main)]
        (b_ref, ddt_ref, wdt_ref, h_ref, w_ref, dh2_ref, src_ref, dh_ref, dw_ref, out_ref,
         acc_ref, send_sems, recv_sems, loc_sem) = refs[len(main):]
        i, k = pl.program_id(0), pl.program_id(1)

        @pl.when(jnp.logical_and(i == 0, k == 0))
        def _():
            for cp in _chip_exchange_copies(src_ref, out_ref, send_sems, recv_sems, loc_sem, window=window):
                cp.start()
            dw_ref[...] = jnp.zeros_like(dw_ref)

        @pl.when(k == 0)
        def _():
            acc_ref[...] = jnp.zeros_like(acc_ref)

        for (_, col, width), ref in zip(main, piece_refs):
            for lo in range(0, width, tk):
                @pl.when(k == (col + lo) // tk)
                def _():
                    acc_ref[...] += _dot(ref[:, lo:lo + tk], b_ref[...], NT)

        @pl.when(k == nk - 1)
        def _():
            dut = acc_ref[...] + _dot(ddt_ref[:, :N_HEADS].astype(BF16), wdt_ref[...], NT)
            xv = h_ref[...]
            r = lax.rsqrt(jnp.mean(xv * xv, axis=-1, keepdims=True) + EPS)
            nrm = xv * r
            dw_ref[...] += jnp.sum(dut * nrm, axis=0, keepdims=True)
            wd = dut * w_ref[...]
            dh_ref[...] = dh2_ref[...] + r * (wd - nrm * jnp.mean(wd * nrm, axis=-1, keepdims=True))

        @pl.when(jnp.logical_and(i == LP // tm - 1, k == nk - 1))
        def _():
            for cp in _chip_exchange_copies(src_ref, out_ref, send_sems, recv_sems, loc_sem, window=window):
                cp.wait()

    rows = lambda width: pl.BlockSpec((tm, width), lambda i, k: (i, 0))
    const = lambda shape: pl.BlockSpec(shape, lambda i, k: (0, 0))
    return pl.pallas_call(
        body, name="d_u_prenorm_bwd",
        out_shape=(jax.ShapeDtypeStruct((LP, D_MODEL), F32), jax.ShapeDtypeStruct((1, D_MODEL), F32),
                   jax.ShapeDtypeStruct((4, chip_sum.shape[0], WIN_WIDTH), chip_sum.dtype)),
        grid=(LP // tm, nk),
        in_specs=[rows(width) for _, _, width in main]
        + [pl.BlockSpec((D_MODEL, tk), lambda i, k: (0, k)), rows(BLK), const((D_MODEL, N_HEADS)), rows(D_MODEL),
           const((1, D_MODEL)), rows(D_MODEL), _ANY],
        out_specs=(rows(D_MODEL), const((1, D_MODEL)), _ANY),
        scratch_shapes=[pltpu.VMEM((tm, D_MODEL), F32)] + _CHIP_EXCHANGE_SEMS,
        compiler_params=_params(("arbitrary", "arbitrary")),
    )(*pieces[:-1], w_main, pieces[-1], wdt, h_pad, norm_w, dh2, chip_sum)


def _suffix_sum(vals, tri):
    return _dot_x_exact(vals, tri, NN, n=2)


def _sb_tile(z):
    t = jnp.exp(-jnp.abs(z))
    inv = 1.0 / (1.0 + t)
    sp = jnp.maximum(z, 0.0) + jnp.log(1.0 + t)
    sig = jnp.where(z >= 0, inv, t * inv)
    return sig, -sp, z - sp


def _sweep(first, step, init, run_slots):
    def alive_of(state):
        top = state[run_slots[0]]
        for s in run_slots[1:]:
            top = jnp.maximum(top, state[s])
        return (jnp.max(top) > SB_DEAD).astype(jnp.int32)

    def cond(carry):
        return jnp.logical_and(carry[0] >= 0, carry[1] > 0)

    def body(carry):
        state = step(carry[0], tuple(carry[2:]))
        return (carry[0] - 1, alive_of(state)) + tuple(state)

    return lax.while_loop(cond, body, (first, alive_of(init)) + tuple(init))[2:]


def _head_masks(x, lane):
    head0 = lane < HEAD_DIM
    return jnp.where(head0, x, 0.0).astype(BF16), jnp.where(head0, 0.0, x).astype(BF16)


def _stage_kv(qi, k_ref, v_ref, kb_ref, vb_ref):
    @pl.when(qi == 0)
    def _():
        kb_ref[:BLK, :] = jnp.zeros((BLK, BLK), BF16)
        vb_ref[:BLK, :] = jnp.zeros((BLK, BLK), BF16)
        kb_ref[BLK:, :] = k_ref[...].astype(BF16)
        vb_ref[BLK:, :] = v_ref[...].astype(BF16)


def _sb_attention_fwd(proj, w_out_shard):
    def body(q_ref, k_ref, v_ref, src_ref, o_ref, ox_ref, out_ref, kb_ref, vb_ref, send_sems, recv_sems, loc_sem):
        hp, qi = pl.program_id(0), pl.program_id(1)
        _stage_kv(qi, k_ref, v_ref, kb_ref, vb_ref)

        @pl.when(jnp.logical_and(hp == 0, qi == 0))
        def _():
            for cp in _all_gather_copies(src_ref, out_ref, send_sems, recv_sems, loc_sem):
                cp.start()

        lane = _iota((BLK, BLK), 1)
        qh = _head_masks(q_ref[...] * SB_SCALE, lane)
        row1 = qi * BLK + _iota((BLK, BLK), 0)

        def tiles(ks, vs, valid, tri, runs):
            heads = range(2)
            parts = [_sb_tile(jnp.where(valid, _dot(qh[h], ks, NT), SB_MASKED)) for h in heads]
            afters = [_suffix_sum(parts[h][1], tri) for h in heads]
            if runs is not None:
                afters = [afters[h] + runs[h] for h in heads]
            avals = [jnp.exp(parts[h][2] + afters[h]) for h in heads]
            his = [avals[h].astype(BF16) for h in heads]
            accs = [_dot(his[h], vs, NN) for h in heads]
            rests = [_dot((avals[h] - his[h].astype(F32)).astype(BF16), vs, NN) for h in heads]
            return [(jnp.sum(parts[h][1], axis=1, keepdims=True), accs[h], rests[h]) for h in heads]

        off = pl.multiple_of(qi * BLK, BLK)
        ks2 = kb_ref[pl.ds(off, 2 * BLK), :]
        vs2 = vb_ref[pl.ds(off, 2 * BLK), :]
        col2 = (qi - 1) * BLK + _iota((BLK, 2 * BLK), 1)
        row2 = qi * BLK + _iota((BLK, 2 * BLK), 0)
        valid2 = jnp.logical_and(col2 < row2, col2 >= PAD)
        tri2 = (_iota((2 * BLK, 2 * BLK), 0) > _iota((2 * BLK, 2 * BLK), 1)).astype(BF16)
        init = [t for head in tiles(ks2, vs2, valid2, tri2, None) for t in head]
        tri1 = (_iota((BLK, BLK), 0) > lane).astype(BF16)

        def step(kblk, carry):
            o1 = pl.multiple_of((kblk + 1) * BLK, BLK)
            ks = kb_ref[pl.ds(o1, BLK), :]
            vs = vb_ref[pl.ds(o1, BLK), :]
            col = kblk * BLK + lane
            valid = jnp.logical_and(col < row1, col >= PAD)
            new = []
            for h, (d_run, d_acc, d_rest) in enumerate(tiles(ks, vs, valid, tri1, (carry[0], carry[3]))):
                new += [carry[3 * h] + d_run, carry[3 * h + 1] + d_acc, carry[3 * h + 2] + d_rest]
            return tuple(new)

        res = _sweep(qi - 2, step, init, (0, 3))
        o = jnp.where(lane < HEAD_DIM, res[1], res[4])
        o_ref[...] = o
        ox_ref[...] = o + jnp.where(lane < HEAD_DIM, res[2], res[5])

        @pl.when(jnp.logical_and(hp == N_HEADS // 2 - 1, qi == NBLK - 1))
        def _():
            for cp in _all_gather_copies(src_ref, out_ref, send_sems, recv_sems, loc_sem):
                cp.wait()

    act = jax.ShapeDtypeStruct((LP, 1024), F32)
    return pl.pallas_call(
        body, name="sb_attn_fwd",
        out_shape=(act, act, jax.ShapeDtypeStruct((N_DEV,) + w_out_shard.shape, w_out_shard.dtype)),
        grid=(N_HEADS // 2, NBLK),
        in_specs=[pl.BlockSpec((BLK, BLK), lambda hp, qi: (qi, COL_Q // BLK + hp)),
                  pl.BlockSpec((LP, BLK), lambda hp, qi: (0, COL_K // BLK + hp)),
                  pl.BlockSpec((LP, BLK), lambda hp, qi: (0, COL_V // BLK + hp)), _ANY],
        out_specs=(pl.BlockSpec((BLK, BLK), lambda hp, qi: (qi, hp)),
                   pl.BlockSpec((BLK, BLK), lambda hp, qi: (qi, hp)), _ANY),
        scratch_shapes=[pltpu.VMEM((LP + BLK, BLK), BF16), pltpu.VMEM((LP + BLK, BLK), BF16)] + _ALL_GATHER_SEMS,
        compiler_params=_params(("arbitrary", "arbitrary")),
    )(proj, proj, proj, w_out_shard)


def _sb_attention_bwd(proj, o_sb, do_sb, chip_sums):
    def body(q_ref, k_ref, v_ref, o_ref, do_ref, src_ref, dq_ref, dk_ref, dv_ref, out_ref,
             kb_ref, vb_ref, dka_ref, dva_ref, send_sems, recv_sems, loc_sem):
        hp, qi = pl.program_id(0), pl.program_id(1)
        _stage_kv(qi, k_ref, v_ref, kb_ref, vb_ref)

        @pl.when(jnp.logical_and(hp == 0, qi == 0))
        def _():
            for cp in _chip_exchange_copies(src_ref, out_ref, send_sems, recv_sems, loc_sem):
                cp.start()

        @pl.when(qi == 0)
        def _():
            dka_ref[...] = jnp.zeros_like(dka_ref)
            dva_ref[...] = jnp.zeros_like(dva_ref)

        lane = _iota((BLK, BLK), 1)
        head0 = lane < HEAD_DIM
        qh = _head_masks(q_ref[...] * SB_SCALE, lane)
        do = do_ref[...]
        doh = _head_masks(do, lane)
        prod = do.astype(BF16).astype(F32) * o_ref[...]
        dsum = (jnp.sum(jnp.where(head0, prod, 0.0), axis=1, keepdims=True),
                jnp.sum(jnp.where(head0, 0.0, prod), axis=1, keepdims=True))
        row1 = qi * BLK + _iota((BLK, BLK), 0)

        def tiles(ks, vs, valid, tri, runs, eruns):
            heads = range(2)
            parts = [_sb_tile(jnp.where(valid, _dot(qh[h], ks, NT), SB_MASKED)) for h in heads]
            afters = [_suffix_sum(parts[h][1], tri) for h in heads]
            if runs is not None:
                afters = [afters[h] + runs[h] for h in heads]
            avals = [jnp.exp(parts[h][2] + afters[h]) for h in heads]
            es = [avals[h] * _dot(doh[h], vs, NT) for h in heads]
            esufs = [_suffix_sum(es[h], tri) for h in heads]
            if eruns is not None:
                esufs = [esufs[h] + eruns[h] for h in heads]
            dzs = [(es[h] - parts[h][0] * (dsum[h] - esufs[h])).astype(BF16) for h in heads]
            return [(jnp.sum(parts[h][1], axis=1, keepdims=True), jnp.sum(es[h], axis=1, keepdims=True),
                     _dot(dzs[h], ks, NN), _dot(dzs[h], qh[h], TN), _dot(avals[h].astype(BF16), doh[h], TN))
                    for h in heads]

        off = pl.multiple_of(qi * BLK, BLK)
        ks2 = kb_ref[pl.ds(off, 2 * BLK), :]
        vs2 = vb_ref[pl.ds(off, 2 * BLK), :]
        col2 = (qi - 1) * BLK + _iota((BLK, 2 * BLK), 1)
        row2 = qi * BLK + _iota((BLK, 2 * BLK), 0)
        valid2 = jnp.logical_and(col2 < row2, col2 >= PAD)
        tri2 = (_iota((2 * BLK, 2 * BLK), 0) > _iota((2 * BLK, 2 * BLK), 1)).astype(BF16)
        init = []
        for h, (d_run, d_erun, d_q, d_k, d_v) in enumerate(tiles(ks2, vs2, valid2, tri2, None, None)):
            init += [d_run, d_erun, d_q]
            dk2 = d_k if h == 0 else dk2 + d_k
            dv2 = d_v if h == 0 else dv2 + d_v
        dka_ref[pl.ds(off, 2 * BLK), :] += dk2
        dva_ref[pl.ds(off, 2 * BLK), :] += dv2
        tri1 = (_iota((BLK, BLK), 0) > lane).astype(BF16)

        def step(kblk, carry):
            o1 = pl.multiple_of((kblk + 1) * BLK, BLK)
            ks = kb_ref[pl.ds(o1, BLK), :]
            vs = vb_ref[pl.ds(o1, BLK), :]
            col = kblk * BLK + lane
            valid = jnp.logical_and(col < row1, col >= PAD)
            new = []
            dk1 = jnp.zeros((BLK, BLK), F32)
            dv1 = jnp.zeros((BLK, BLK), F32)
            both = tiles(ks, vs, valid, tri1, (carry[0], carry[3]), (carry[1], carry[4]))
            for h, (d_run, d_erun, d_q, d_k, d_v) in enumerate(both):
                new += [carry[3 * h] + d_run, carry[3 * h + 1] + d_erun, carry[3 * h + 2] + d_q]
                dk1 = dk1 + d_k
                dv1 = dv1 + d_v
            dka_ref[pl.ds(o1, BLK), :] += dk1
            dva_ref[pl.ds(o1, BLK), :] += dv1
            return tuple(new)

        res = _sweep(qi - 2, step, init, (0, 3))
        dq_ref[...] = (jnp.where(head0, res[2], res[5]) * SB_SCALE).astype(BF16)

        @pl.when(qi == NBLK - 1)
        def _():
            dk_ref[...] = dka_ref[BLK:, :].astype(BF16)
            dv_ref[...] = dva_ref[BLK:, :].astype(BF16)

        @pl.when(jnp.logical_and(hp == N_HEADS // 2 - 1, qi == NBLK - 1))
        def _():
            for cp in _chip_exchange_copies(src_ref, out_ref, send_sems, recv_sems, loc_sem):
                cp.wait()

    act = jax.ShapeDtypeStruct((LP, 1024), BF16)
    blk = lambda col: pl.BlockSpec((BLK, BLK), lambda hp, qi: (qi, col + hp))
    whole = lambda col: pl.BlockSpec((LP, BLK), lambda hp, qi: (0, col + hp))
    return pl.pallas_call(
        body, name="sb_attn_bwd",
        out_shape=(act, act, act, jax.ShapeDtypeStruct(chip_sums.shape, chip_sums.dtype)),
        grid=(N_HEADS // 2, NBLK),
        in_specs=[blk(COL_Q // BLK), whole(COL_K // BLK), whole(COL_V // BLK), blk(0), blk(0), _ANY],
        out_specs=(blk(0), whole(0), whole(0), _ANY),
        scratch_shapes=[pltpu.VMEM((LP + BLK, BLK), BF16), pltpu.VMEM((LP + BLK, BLK), BF16),
                        pltpu.VMEM((LP + BLK, BLK), F32), pltpu.VMEM((LP + BLK, BLK), F32)] + _CHIP_EXCHANGE_SEMS,
        compiler_params=_params(("arbitrary", "arbitrary")),
    )(proj, proj, proj, o_sb, do_sb, chip_sums)


def _conv_pre(x, w_ref, b_ref):
    acc = b_ref[...] + w_ref[3:4, :] * x
    for k in range(3):
        acc = acc + w_ref[k:k + 1, :] * pltpu.roll(x, 3 - k, 0)
    return acc


def _conv_fwd(proj, conv_w, conv_b):
    def body(x_ref, w_ref, b_ref, o_ref):
        xc = _conv_pre(x_ref[...], w_ref, b_ref)
        o_ref[...] = xc * _sigmoid(xc)

    nb = XBC_W // BLK
    return pl.pallas_call(
        body, name="conv_fwd", out_shape=jax.ShapeDtypeStruct((LP, XBC_W), F32), grid=(nb,),
        in_specs=[pl.BlockSpec((LP, BLK), lambda j: (0, COL_XBC // BLK + j)),
                  pl.BlockSpec((4, BLK), lambda j: (0, j)), pl.BlockSpec((1, BLK), lambda j: (0, j))],
        out_specs=pl.BlockSpec((LP, BLK), lambda j: (0, j)),
        compiler_params=_params(("parallel",)),
    )(proj, conv_w, conv_b)


def _conv_bwd(dxa, proj, conv_w, conv_b):
    def body(d_ref, x_ref, w_ref, b_ref, dx_ref, dw_ref, db_ref):
        x = x_ref[...]
        xc = _conv_pre(x, w_ref, b_ref)
        s = _sigmoid(xc)
        live = _iota((LP, BLK), 0) >= PAD
        dxc = jnp.where(live, d_ref[...] * (s * (1.0 + xc * (1.0 - s))), 0.0)
        db_ref[...] = jnp.sum(dxc, axis=0, keepdims=True)
        dx = w_ref[3:4, :] * dxc
        dw_ref[3:4, :] = jnp.sum(dxc * x, axis=0, keepdims=True)
        for k in range(3):
            dw_ref[k:k + 1, :] = jnp.sum(dxc * pltpu.roll(x, 3 - k, 0), axis=0, keepdims=True)
            dx = dx + w_ref[k:k + 1, :] * pltpu.roll(dxc, LP - (3 - k), 0)
        dx_ref[...] = dx.astype(BF16)

    nb = XBC_W // BLK
    col = lambda j: (0, j)
    return pl.pallas_call(
        body, name="conv_bwd",
        out_shape=(jax.ShapeDtypeStruct((LP, XBC_W), BF16), jax.ShapeDtypeStruct((4, XBC_W), F32),
                   jax.ShapeDtypeStruct((1, XBC_W), F32)),
        grid=(nb,),
        in_specs=[pl.BlockSpec((LP, BLK), col), pl.BlockSpec((LP, BLK), lambda j: (0, COL_XBC // BLK + j)),
                  pl.BlockSpec((4, BLK), col), pl.BlockSpec((1, BLK), col)],
        out_specs=(pl.BlockSpec((LP, BLK), col), pl.BlockSpec((4, BLK), col), pl.BlockSpec((1, BLK), col)),
        compiler_params=_params(("parallel",)),
    )(dxa, proj, conv_w, conv_b)


def _ssd_prelude(c, dt_ref, dtt_ref, dtb_ref, dtbt_ref, alog_ref, alogt_ref):
    live = jnp.logical_or(c > 0, _iota((BLK, N_HEADS), 0) >= PAD)
    live_t = jnp.logical_or(c > 0, _iota((N_HEADS, BLK), 1) >= PAD)
    pre = dt_ref[...] + dtb_ref[...]
    pre_t = dtt_ref[...] + dtbt_ref[...]
    dt = jnp.where(live, _softplus(pre), 0.0)
    dt_t = jnp.where(live_t, _softplus(pre_t), 0.0)
    a = -jnp.exp(alog_ref[...])
    a_t = -jnp.exp(alogt_ref[...])
    li = _iota((BLK, BLK), 0)
    si = _iota((BLK, BLK), 1)
    lower = (si <= li).astype(BF16)
    upper = (li <= si).astype(BF16)
    acum = _dot_exact_x(lower, dt * a, NN)
    acum_t = _dot_x_exact(dt_t * a_t, upper, NN)
    return live, pre, dt, a, a_t, acum, acum_t


def _head_expand():
    return (_iota((N_HEADS, SSD_W), 1) // HEAD_DIM == _iota((N_HEADS, SSD_W), 0)).astype(BF16)


def _head_reduce_mat():
    return (_iota((SSD_W, N_HEADS), 0) // HEAD_DIM == _iota((SSD_W, N_HEADS), 1)).astype(BF16)


def _decay_mat(acum, acum_t, h, causal):
    seg = jnp.minimum(acum[:, h:h + 1] - acum_t[h:h + 1, :], 0.0)
    return jnp.where(causal, jnp.exp(seg), 0.0)


def _ssd_specs():
    chunk = lambda width, col=0: pl.BlockSpec((BLK, width), lambda c: (c, col))
    return chunk


def _ssd_fwd(xa, dt_raw, dt_raw_t, dt_bias, dt_bias_t, a_log, a_log_t, d_exp):
    def body(x_ref, b_ref, c_ref, dt_ref, dtt_ref, dtb_ref, dtbt_ref, alog_ref, alogt_ref, dexp_ref,
             y_ref, hs_ref, state_ref):
        c = pl.program_id(0)

        @pl.when(c == 0)
        def _():
            state_ref[...] = jnp.zeros_like(state_ref)

        _, _, dt, _, _, acum, acum_t = _ssd_prelude(c, dt_ref, dtt_ref, dtb_ref, dtbt_ref, alog_ref, alogt_ref)
        expand = _head_expand()
        x = x_ref[...]
        xdt = x * _dot_x_exact(dt, expand)
        exp_a = _dot_x_exact(jnp.exp(acum), expand)
        to_end = _dot_x_exact(jnp.exp(acum[BLK - 1:BLK, :] - acum), expand)
        xdt_b = xdt.astype(BF16)
        xd_b = (xdt * to_end).astype(BF16)
        chunk_decay = jnp.exp(acum_t[:, BLK - 1:BLK])
        hs_ref[0] = state_ref[...]
        lane = _iota((BLK, BLK), 1)
        causal = _iota((BLK, BLK), 0) >= lane
        gw = HEADS_PER_GROUP * HEAD_DIM
        for g in range(N_GROUPS):
            bg = b_ref[:, g * N_STATE:(g + 1) * N_STATE].astype(BF16)
            cg = c_ref[:, g * N_STATE:(g + 1) * N_STATE].astype(BF16)
            cb = _dot(cg, bg, NT)
            hg = state_ref[g * gw:(g + 1) * gw, :]
            ch = _dot(cg, hg.astype(BF16), NT)
            st = _dot(xd_b[:, g * gw:(g + 1) * gw], bg, TN)
            for p in range(HEADS_PER_GROUP // 2):
                h0 = g * HEADS_PER_GROUP + 2 * p
                lo = h0 * HEAD_DIM
                xp = xdt_b[:, lo:lo + BLK]
                w0 = (cb * _decay_mat(acum, acum_t, h0, causal)).astype(BF16)
                w1 = (cb * _decay_mat(acum, acum_t, h0 + 1, causal)).astype(BF16)
                yd = jnp.where(lane < HEAD_DIM, _dot(w0, xp), _dot(w1, xp))
                y_ref[:, lo:lo + BLK] = (yd + ch[:, lo - g * gw:lo - g * gw + BLK] * exp_a[:, lo:lo + BLK]
                                         + x[:, lo:lo + BLK] * dexp_ref[:, lo:lo + BLK])
            for r in range(HEADS_PER_GROUP):
                h = g * HEADS_PER_GROUP + r
                state_ref[h * HEAD_DIM:(h + 1) * HEAD_DIM, :] = (
                    hg[r * HEAD_DIM:(r + 1) * HEAD_DIM, :] * chunk_decay[h:h + 1, :]
                    + st[r * HEAD_DIM:(r + 1) * HEAD_DIM, :])

    chunk = _ssd_specs()
    return pl.pallas_call(
        body, name="ssd_fwd",
        out_shape=(jax.ShapeDtypeStruct((LP, SSD_W), F32), jax.ShapeDtypeStruct((NBLK, SSD_W, N_STATE), F32)),
        grid=(NBLK,),
        in_specs=[chunk(SSD_W), chunk(256, 4), chunk(256, 5), chunk(N_HEADS),
                  pl.BlockSpec((N_HEADS, BLK), lambda c: (0, c)), _const_spec((1, N_HEADS)),
                  _const_spec((N_HEADS, 1)), _const_spec((1, N_HEADS)), _const_spec((N_HEADS, 1)),
                  _const_spec((1, SSD_W))],
        out_specs=(chunk(SSD_W), pl.BlockSpec((1, SSD_W, N_STATE), lambda c: (c, 0, 0))),
        scratch_shapes=[pltpu.VMEM((SSD_W, N_STATE), F32)],
        compiler_params=_params(("arbitrary",)),
    )(xa, xa, xa, dt_raw, dt_raw_t, dt_bias, dt_bias_t, a_log, a_log_t, d_exp)


def _ssd_bwd(xa, dt_raw, dt_raw_t, dt_bias, dt_bias_t, a_log, a_log_t, d_exp, hstart, dy):
    def body(x_ref, b_ref, c_ref, dt_ref, dtt_ref, dtb_ref, dtbt_ref, alog_ref, alogt_ref, dexp_ref,
             hs_ref, dy_ref, dxa_ref, ddt_ref, dbias_ref, dalog_ref, dd_ref, dstate_ref):
        step = pl.program_id(0)
        c = NBLK - 1 - step

        @pl.when(step == 0)
        def _():
            dstate_ref[...] = jnp.zeros_like(dstate_ref)
            dbias_ref[...] = jnp.zeros_like(dbias_ref)
            dalog_ref[...] = jnp.zeros_like(dalog_ref)
            dd_ref[...] = jnp.zeros_like(dd_ref)

        live, pre, dt, a, a_t, acum, acum_t = _ssd_prelude(c, dt_ref, dtt_ref, dtb_ref, dtbt_ref,
                                                           alog_ref, alogt_ref)
        expand = _head_expand()
        reduce_m = _head_reduce_mat()
        x = x_ref[...]
        dyv = dy_ref[...]
        dt_e = _dot_x_exact(dt, expand)
        xdt = x * dt_e
        exp_acum = jnp.exp(acum)
        exp_a = _dot_x_exact(exp_acum, expand)
        dte = jnp.exp(acum[BLK - 1:BLK, :] - acum)
        to_end = _dot_x_exact(dte, expand)
        xdt_b = xdt.astype(BF16)
        xd_b = (xdt * to_end).astype(BF16)
        chunk_decay = jnp.exp(acum_t[:, BLK - 1:BLK])
        lane = _iota((BLK, BLK), 1)
        head0 = lane < HEAD_DIM
        causal = _iota((BLK, BLK), 0) >= lane
        gw = HEADS_PER_GROUP * HEAD_DIM
        dm = dyv * exp_a
        dm_b = dm.astype(BF16)
        onehot = lambda h: (_iota((1, N_HEADS), 1) == h).astype(F32)
        onehot_t = lambda h: (_iota((N_HEADS, 1), 0) == h).astype(F32)
        dacum = jnp.zeros((BLK, N_HEADS), F32)
        dacum_t = jnp.zeros((N_HEADS, BLK), F32)
        ddt_acc = jnp.zeros((BLK, N_HEADS), F32)
        ddte_acc = jnp.zeros((BLK, N_HEADS), F32)
        dexpa_acc = jnp.zeros((BLK, N_HEADS), F32)
        dskip_acc = jnp.zeros((BLK, N_HEADS), F32)
        head_sum = expand
        for g in range(N_GROUPS):
            bg = b_ref[:, g * N_STATE:(g + 1) * N_STATE].astype(BF16)
            cg = c_ref[:, g * N_STATE:(g + 1) * N_STATE].astype(BF16)
            cb = _dot(cg, bg, NT)
            hg = hs_ref[0, g * gw:(g + 1) * gw, :]
            hg_b = hg.astype(BF16)
            dhe = dstate_ref[g * gw:(g + 1) * gw, :]
            dhe_b = dhe.astype(BF16)
            ch = _dot(cg, hg_b, NT)
            dcg = _dot(dm_b[:, g * gw:(g + 1) * gw], hg_b, NN)
            dhs = _dot(dm_b[:, g * gw:(g + 1) * gw], cg, TN)
            dxd = _dot(bg, dhe_b, NT)
            dbg = _dot(xd_b[:, g * gw:(g + 1) * gw], dhe_b, NN)
            dcb = jnp.zeros((BLK, BLK), F32)
            for p in range(HEADS_PER_GROUP // 2):
                h0 = g * HEADS_PER_GROUP + 2 * p
                lo = h0 * HEAD_DIM
                xp = xdt_b[:, lo:lo + BLK]
                dyp = dyv[:, lo:lo + BLK]
                dyh = (jnp.where(head0, dyp, 0.0).astype(BF16), jnp.where(head0, 0.0, dyp).astype(BF16))
                dxdt_p = jnp.zeros((BLK, BLK), F32)
                for q in range(2):
                    h = h0 + q
                    dec = _decay_mat(acum, acum_t, h, causal)
                    w = cb * dec
                    dw = _dot(dyh[q], xp, NT)
                    t = dw * w
                    dacum = dacum + jnp.sum(t, axis=1, keepdims=True) * onehot(h)
                    dacum_t = dacum_t - jnp.sum(t, axis=0, keepdims=True) * onehot_t(h)
                    dcb = dcb + dw * dec
                    dxdt_p = dxdt_p + _dot(w.astype(BF16), dyh[q], TN)
                sl = slice(lo, lo + BLK)
                gl = slice(lo - g * gw, lo - g * gw + BLK)
                dxdt_p = dxdt_p + dxd[:, gl] * to_end[:, sl]
                dxa_ref[:, sl] = dyp * dexp_ref[:, sl] + dxdt_p * dt_e[:, sl]
                red = reduce_m[lo:lo + BLK, :]
                ddt_acc = ddt_acc + _dot_x_exact(dxdt_p * x[:, sl], red)
                ddte_acc = ddte_acc + _dot_x_exact(dxd[:, gl] * xdt[:, sl], red)
                dexpa_acc = dexpa_acc + _dot_x_exact(dyp * ch[:, gl], red)
                dskip_acc = dskip_acc + _dot_x_exact(dyp * x[:, sl], red)
            dcb_b = dcb.astype(BF16)
            b_col = SSD_W + g * N_STATE
            c_col = SSD_W + (N_GROUPS + g) * N_STATE
            dxa_ref[:, c_col:c_col + N_STATE] = dcg + _dot(dcb_b, bg, NN)
            dxa_ref[:, b_col:b_col + N_STATE] = dbg + _dot(dcb_b, cg, TN)
            prod = dhe * hg
            per_head = jnp.sum(_dot_exact_x(head_sum[:, g * gw:(g + 1) * gw], prod, NN), axis=1, keepdims=True)
            dacum_t = dacum_t + (per_head * chunk_decay) * (_iota((1, BLK), 1) == BLK - 1).astype(F32)
            for r in range(HEADS_PER_GROUP):
                h = g * HEADS_PER_GROUP + r
                rows = slice(h * HEAD_DIM, (h + 1) * HEAD_DIM)
                dstate_ref[rows, :] = (dhs[r * HEAD_DIM:(r + 1) * HEAD_DIM, :]
                                       + dhe[r * HEAD_DIM:(r + 1) * HEAD_DIM, :] * chunk_decay[h:h + 1, :])
        dacum = dacum + dexpa_acc * exp_acum - ddte_acc * dte
        last_row = (_iota((BLK, 1), 0) == BLK - 1).astype(F32)
        dacum = dacum + last_row * jnp.sum(ddte_acc * dte, axis=0, keepdims=True)
        li = _iota((BLK, BLK), 0)
        si = _iota((BLK, BLK), 1)
        upper = (li <= si).astype(BF16)
        lower = (si <= li).astype(BF16)
        dda = _dot_exact_x(upper, dacum, NN)
        dda_t = _dot_x_exact(dacum_t, lower, NN)
        eye = (_iota((N_HEADS, N_HEADS), 0) == _iota((N_HEADS, N_HEADS), 1)).astype(BF16)
        dda = dda + _dot_x_exact_tn(dda_t, eye)
        ddt = ddt_acc + dda * a
        dalog_ref[...] += jnp.sum(dda * dt, axis=0, keepdims=True) * a
        dd_ref[...] += jnp.sum(dskip_acc, axis=0, keepdims=True)
        ddt_raw = jnp.where(live, ddt * _sigmoid(pre), 0.0)
        ddt_ref[...] = jnp.zeros_like(ddt_ref)
        ddt_ref[:, :N_HEADS] = ddt_raw
        dbias_ref[...] += jnp.sum(ddt_raw, axis=0, keepdims=True)

    rev = lambda width, col=0: pl.BlockSpec((BLK, width), lambda s: (NBLK - 1 - s, col))
    vec = jax.ShapeDtypeStruct((1, N_HEADS), F32)
    return pl.pallas_call(
        body, name="ssd_bwd",
        out_shape=(jax.ShapeDtypeStruct((LP, XBC_W), F32), jax.ShapeDtypeStruct((LP, BLK), F32), vec, vec, vec),
        grid=(NBLK,),
        in_specs=[rev(SSD_W), rev(256, 4), rev(256, 5), rev(N_HEADS),
                  pl.BlockSpec((N_HEADS, BLK), lambda s: (0, NBLK - 1 - s)), _const_spec((1, N_HEADS)),
                  _const_spec((N_HEADS, 1)), _const_spec((1, N_HEADS)), _const_spec((N_HEADS, 1)),
                  _const_spec((1, SSD_W)),
                  pl.BlockSpec((1, SSD_W, N_STATE), lambda s: (NBLK - 1 - s, 0, 0)), rev(SSD_W)],
        out_specs=(rev(XBC_W), rev(BLK), _const_spec((1, N_HEADS)),
                   _const_spec((1, N_HEADS)), _const_spec((1, N_HEADS))),
        scratch_shapes=[pltpu.VMEM((SSD_W, N_STATE), F32)],
        compiler_params=_params(("arbitrary",)),
    )(xa, xa, xa, dt_raw, dt_raw_t, dt_bias, dt_bias_t, a_log, a_log_t, d_exp, hstart, dy)


def _dot_x_exact_tn(x_t, eye):
    out = None
    for p in _split(x_t, 3):
        t = _dot(p, eye, TN)
        out = t if out is None else out + t
    return out


def _adamw(name, parts, w, m, v, rows):
    r_all, cols = w.shape
    assert r_all % rows == 0
    c1 = 1.0 / (1.0 - ADAM_B1 ** ADAM_STEP)
    c2 = 1.0 / (1.0 - ADAM_B2 ** ADAM_STEP)

    def body(p_ref, w_ref, m_ref, v_ref, g_ref, d_ref, mo_ref, vo_ref):
        g = p_ref[0].astype(F32)
        for j in range(1, parts.shape[0]):
            g = g + p_ref[j].astype(F32)
        mn = ADAM_B1 * m_ref[...] + (1.0 - ADAM_B1) * g
        vn = ADAM_B2 * v_ref[...] + (1.0 - ADAM_B2) * (g * g)
        g_ref[...] = g
        mo_ref[...] = mn
        vo_ref[...] = vn
        d_ref[...] = -ADAM_LR * ((mn * c1) / (jnp.sqrt(vn * c2) + ADAM_EPS) + ADAM_WD * w_ref[...])

    spec = pl.BlockSpec((rows, cols), lambda i: (i, 0))
    shp = jax.ShapeDtypeStruct((r_all, cols), F32)
    return pl.pallas_call(
        body, name=name, out_shape=(shp, shp, shp, shp), grid=(r_all // rows,),
        in_specs=[pl.BlockSpec((parts.shape[0], rows, cols), lambda i: (0, i, 0)), spec, spec, spec],
        out_specs=(spec, spec, spec, spec),
        compiler_params=_params(("parallel",)),
    )(parts, w, m, v)


_VECTORS = (("norm_w", 1024, 0), ("conv_b", 1536, 8), ("dt_bias", 16, 20), ("a_log", 16, 21), ("d_skip", 16, 22),
            ("sb_norm_w", 1024, 24), ("ssd_norm_w", 1024, 32), ("final_norm_w", 1024, 40))
_LOSS_ROW = 23
_CONVW_ROW = 48
_META_ROW = 96
_PACK_ROWS = 224
_SMALL_ORDER = tuple(name for name, _, _ in _VECTORS) + ("conv_w", "meta_tokens")


def _pack_small_grads(vectors, loss_row, d_convw, dh):
    def body(*refs):
        vec_refs, (loss_ref, cw_ref, dh_ref, out_ref) = refs[:len(_VECTORS)], refs[len(_VECTORS):]
        out_ref[...] = jnp.zeros_like(out_ref)
        for (_, width, row), ref in zip(_VECTORS, vec_refs):
            if width < BLK:
                out_ref[row:row + 1, :width] = ref[...]
            else:
                for t in range(width // BLK):
                    out_ref[row + t:row + t + 1, :] = ref[:, t * BLK:(t + 1) * BLK]
        out_ref[_LOSS_ROW:_LOSS_ROW + 1, :] = loss_ref[...]
        for k in range(4):
            for t in range(XBC_W // BLK):
                r = _CONVW_ROW + k * (XBC_W // BLK) + t
                out_ref[r:r + 1, :] = cw_ref[k:k + 1, t * BLK:(t + 1) * BLK]
        for i in range(N_META):
            for t in range(D_MODEL // BLK):
                r = _META_ROW + i * (D_MODEL // BLK) + t
                out_ref[r:r + 1, :] = dh_ref[i:i + 1, t * BLK:(t + 1) * BLK]

    full = lambda a: pl.BlockSpec(a.shape, lambda i: tuple(0 for _ in a.shape))
    return pl.pallas_call(
        body, name="pack_small_grads", out_shape=jax.ShapeDtypeStruct((_PACK_ROWS, BLK), F32), grid=(1,),
        in_specs=[full(v) for v in vectors] + [full(loss_row), full(d_convw),
                                               pl.BlockSpec((N_META, D_MODEL), lambda i: (PAD // N_META, 0))],
        out_specs=pl.BlockSpec((_PACK_ROWS, BLK), lambda i: (0, 0)),
        compiler_params=_params(("arbitrary",)),
    )(*vectors, loss_row, d_convw, dh)


def _sum_slots(name, parts, rows):
    n, r_all, cols = parts.shape

    def body(p_ref, o_ref):
        acc = p_ref[0].astype(F32)
        for j in range(1, n):
            acc = acc + p_ref[j].astype(F32)
        o_ref[...] = acc

    return pl.pallas_call(
        body, name=name, out_shape=jax.ShapeDtypeStruct((r_all, cols), F32), grid=(r_all // rows,),
        in_specs=[pl.BlockSpec((n, rows, cols), lambda i: (0, i, 0))],
        out_specs=pl.BlockSpec((rows, cols), lambda i: (i, 0)),
        compiler_params=_params(("parallel",)),
    )(parts)


def _add(name, a, b):
    def body(a_ref, b_ref, o_ref):
        o_ref[...] = a_ref[...] + b_ref[...]

    spec = pl.BlockSpec(a.shape, lambda i: (0, 0))
    return pl.pallas_call(body, name=name, out_shape=jax.ShapeDtypeStruct(a.shape, a.dtype), grid=(1,),
                          in_specs=[spec, spec], out_specs=spec, compiler_params=_params(("arbitrary",)))(a, b)


def _adamw_small(chip_sums, weights, moms, vels):
    c1 = 1.0 / (1.0 - ADAM_B1 ** ADAM_STEP)
    c2 = 1.0 / (1.0 - ADAM_B2 ** ADAM_STEP)
    n = len(_SMALL_ORDER)

    def body(*refs):
        p_ref = refs[0]
        w_refs, m_refs, v_refs = refs[1:1 + n], refs[1 + n:1 + 2 * n], refs[1 + 2 * n:1 + 3 * n]
        outs = refs[1 + 3 * n:1 + 7 * n]
        loss_ref, g_ref, cw_ref, cws_ref, mt_ref = refs[1 + 7 * n:]
        x, y, c = _mesh_position()
        me = 4 * x + 2 * y + c
        g = p_ref[0]
        for j in range(1, 4):
            g = g + p_ref[j]
        g_ref[...] = g
        loss_ref[...] = g_ref[_LOSS_ROW:_LOSS_ROW + 1, :]

        def update(idx, grad):
            go_ref, d_ref, mo_ref, vo_ref = outs[4 * idx:4 * idx + 4]
            mn = ADAM_B1 * m_refs[idx][...] + (1.0 - ADAM_B1) * grad
            vn = ADAM_B2 * v_refs[idx][...] + (1.0 - ADAM_B2) * (grad * grad)
            go_ref[...] = grad
            mo_ref[...] = mn
            vo_ref[...] = vn
            d_ref[...] = -ADAM_LR * ((mn * c1) / (jnp.sqrt(vn * c2) + ADAM_EPS) + ADAM_WD * w_refs[idx][...])

        for idx, (_, width, row) in enumerate(_VECTORS):
            go_ref = outs[4 * idx]
            if width < BLK:
                grad = g_ref[row:row + 1, :width]
            else:
                for t in range(width // BLK):
                    go_ref[:, t * BLK:(t + 1) * BLK] = g_ref[row + t:row + t + 1, :]
                grad = go_ref[...]
            update(idx, grad)
        cw_ref[...] = jnp.zeros_like(cw_ref)
        for k in range(4):
            for t in range(XBC_W // BLK):
                r = _CONVW_ROW + k * (XBC_W // BLK) + t
                cw_ref[k:k + 1, t * BLK:(t + 1) * BLK] = g_ref[r:r + 1, :]
        for i in range(N_META):
            for t in range(D_MODEL // BLK):
                r = _META_ROW + i * (D_MODEL // BLK) + t
                mt_ref[i:i + 1, t * BLK:(t + 1) * BLK] = g_ref[r:r + 1, :]
        width_cw = XBC_W // N_DEV
        pick_cw = (_iota((XBC_W, width_cw), 0) == me * width_cw + _iota((XBC_W, width_cw), 1)).astype(BF16)
        cws_ref[...] = _dot_x_exact(cw_ref[...], pick_cw)
        update(n - 2, cws_ref[0:4, :])
        pick_mt = (_iota((D_MODEL, BLK), 0) == me * BLK + _iota((D_MODEL, BLK), 1)).astype(BF16)
        update(n - 1, _dot_x_exact(mt_ref[...], pick_mt))

    full = lambda a: pl.BlockSpec(a.shape, lambda i: tuple(0 for _ in a.shape))
    params = list(weights) + list(moms) + list(vels)
    out_shape, out_specs = [], []
    for w in weights:
        for _ in range(4):
            out_shape.append(jax.ShapeDtypeStruct(w.shape, F32))
            out_specs.append(full(w))
    out_shape.append(jax.ShapeDtypeStruct((1, BLK), F32))
    out_specs.append(pl.BlockSpec((1, BLK), lambda i: (0, 0)))
    return pl.pallas_call(
        body, name="adamw_small", out_shape=tuple(out_shape), grid=(1,),
        in_specs=[full(chip_sums)] + [full(a) for a in params], out_specs=tuple(out_specs),
        scratch_shapes=[pltpu.VMEM((_PACK_ROWS, BLK), F32), pltpu.VMEM((8, XBC_W), F32),
                        pltpu.VMEM((8, XBC_W // N_DEV), F32), pltpu.VMEM((N_META, D_MODEL), F32)],
        compiler_params=_params(("arbitrary",)),
    )(chip_sums, *params)


def kernel(x, meta_tokens, norm_w, w_in, conv_w, conv_b, dt_bias, a_log, d_skip, sb_norm_w, ssd_norm_w, w_out, final_norm_w, loss_target, m_meta_tokens, m_norm_w, m_w_in, m_conv_w, m_conv_b, m_dt_bias, m_a_log, m_d_skip, m_sb_norm_w, m_ssd_norm_w, m_w_out, m_final_norm_w, v_meta_tokens, v_norm_w, v_w_in, v_conv_w, v_conv_b, v_dt_bias, v_a_log, v_d_skip, v_sb_norm_w, v_ssd_norm_w, v_w_out, v_final_norm_w):
    small_src = jnp.concatenate([conv_w[0].reshape(6, BLK), meta_tokens, jnp.zeros((2, BLK), F32)], axis=0)
    small_g, w_in_g = _gather_weights([small_src, w_in[0].astype(BF16)])
    w_in_full = w_in_g.transpose(1, 0, 2).reshape(D_MODEL, D_IN)
    w_main = w_in_full[:, :D_MAIN]
    w_dt = w_in_full[:, D_MAIN:]
    conv_w_full = small_g[:, :6].reshape(N_DEV, 4, 192).transpose(1, 0, 2).reshape(4, XBC_W)
    meta_full = small_g[:, 6:6 + N_META].transpose(1, 0, 2).reshape(N_META, D_MODEL)
    h_pad = jnp.concatenate([jnp.zeros((PAD, D_MODEL), F32), meta_full, x[0]], axis=0)
    dt_bias_t = dt_bias.reshape(N_HEADS, 1)
    a_log_t = a_log.reshape(N_HEADS, 1)
    d_exp = jnp.repeat(d_skip, HEAD_DIM, axis=1)
    fnw = final_norm_w.reshape(1, D_MODEL)

    u, dt_raw, dt_raw_t = _prenorm(h_pad, norm_w, w_dt, w_dt.T)
    proj = _matmul("in_proj", u, w_main, "nn", 1088, 512, D_MODEL)
    o_sb, o_sb_exact, w_out_g = _sb_attention_fwd(proj, w_out[0].astype(BF16))
    w_out_full = w_out_g.reshape(2 * SSD_W, D_MODEL)
    xa = _conv_fwd(proj, conv_w_full, conv_b)
    o_ssd, hstart = _ssd_fwd(xa, dt_raw, dt_raw_t, dt_bias, dt_bias_t, a_log, a_log_t, d_exp)
    ycat = _ycat(o_sb, proj, o_ssd, sb_norm_w, ssd_norm_w)
    yo = _matmul("out_proj", ycat, w_out_full, "nn", 1088, 512, 2 * SSD_W)
    dh2, dh2_b, loss_row, d_fnw = _loss_head(h_pad, yo, fnw, loss_target[0])

    g_w_out = _matmul("d_w_out", ycat, dh2_b, "tn", 512, 512, LP, BF16).reshape(4, 2, 256, D_MODEL)
    chip_w_out = _pair_sum("pair_sum_w_out", g_w_out, _swap_with_sibling("swap_w_out", g_w_out), 256)
    dycat = _matmul("d_ycat", dh2_b, w_out_full, "nt", 1088, 512, D_MODEL)
    do_sb, dg, do_ssd, dz, d_sbw, d_ssdw = _ycat_bwd(dycat, o_sb, proj, o_ssd, sb_norm_w, ssd_norm_w)
    dq, dk, dv, p_w_out = _sb_attention_bwd(proj, o_sb_exact, do_sb, chip_w_out)
    dxa, ddt_raw, d_dtb, d_alog, d_dskip = _ssd_bwd(
        xa, dt_raw, dt_raw_t, dt_bias, dt_bias_t, a_log, a_log_t, d_exp, hstart, do_ssd)
    dxbc, d_convw, d_convb = _conv_bwd(dxa, proj, conv_w_full, conv_b)
    pieces = [dq, dk, dv, dg, dz, dxbc, ddt_raw]
    g_w_in = _d_w_in(u, pieces).reshape(1, 2, D_MODEL // 2, W_IN_PAD)
    chip_w_in = _pair_sum("pair_sum_w_in", g_w_in, _swap_with_sibling("swap_w_in", g_w_in), 128)
    dh, d_nw, win_parts = _d_u_prenorm_bwd(pieces, w_main, w_dt, h_pad, norm_w, dh2, chip_w_in[0])
    win_mine = _sum_slots("sum_w_in_windows", win_parts, 128)
    win_other = _swap_with_sibling("swap_w_in_window", win_mine, whole=True)
    core = lax.axis_index("c")
    chip = 2 * lax.axis_index("x") + lax.axis_index("y")
    first_col = (D_IN // N_DEV) * (2 * chip + core) - WIN_STRIDE * chip
    cut = lambda w: lax.dynamic_slice(w, (0, first_col), (D_MODEL // 2, D_IN // N_DEV))
    half_mine, half_other = cut(win_mine), cut(win_other)
    p_w_in = jnp.concatenate([jnp.where(core == 0, half_mine, half_other),
                              jnp.where(core == 0, half_other, half_mine)], axis=0)[None]

    pack = _pack_small_grads([d_nw, d_convb, d_dtb, d_alog, d_dskip, d_sbw, d_ssdw, d_fnw], loss_row, d_convw, dh)
    chip_small = _add("pair_sum_small", pack, _swap_with_sibling("swap_small", pack, whole=True))
    p_small = _exchange_between_chips("exchange_small", chip_small, whole=True)

    res_in = _adamw("adamw_w_in", p_w_in, w_in[0], m_w_in[0], v_w_in[0], 128)
    res_out = _adamw("adamw_w_out", p_w_out, w_out[0], m_w_out[0], v_w_out[0], 128)
    res_small = _adamw_small(
        p_small,
        [norm_w, conv_b, dt_bias, a_log, d_skip, sb_norm_w, ssd_norm_w, fnw, conv_w[0], meta_tokens],
        [m_norm_w, m_conv_b, m_dt_bias, m_a_log, m_d_skip, m_sb_norm_w, m_ssd_norm_w,
         m_final_norm_w.reshape(1, D_MODEL), m_conv_w[0], m_meta_tokens],
        [v_norm_w, v_conv_b, v_dt_bias, v_a_log, v_d_skip, v_sb_norm_w, v_ssd_norm_w,
         v_final_norm_w.reshape(1, D_MODEL), v_conv_w[0], v_meta_tokens])

    loss = jnp.sum(res_small[-1])
    order = ["meta_tokens", "norm_w", "w_in", "conv_w", "conv_b", "dt_bias", "a_log", "d_skip",
             "sb_norm_w", "ssd_norm_w", "w_out", "final_norm_w"]
    outs = [loss, dh[BLK:].reshape(1, SEQ, D_MODEL)]
    for kind in range(4):
        small = {name: res_small[4 * idx + kind] for idx, name in enumerate(_SMALL_ORDER)}
        small["final_norm_w"] = small["final_norm_w"].reshape(D_MODEL)
        small["conv_w"] = small["conv_w"].reshape(1, 4, XBC_W // N_DEV)
        small["w_in"] = res_in[kind].reshape(1, D_MODEL, D_IN // N_DEV)
        small["w_out"] = res_out[kind].reshape(1, 256, D_MODEL)
        outs += [small[name] for name in order]
    return tuple(outs)
```

```python
import functools

import jax
import jax.numpy as jnp
from jax import lax
from jax.experimental import pallas as pl
from jax.experimental.pallas import tpu as pltpu

F32 = jnp.float32
BF16 = jnp.bfloat16

D_MODEL = 1024
SEQ = 2048
N_META = 16
BLK = 128
PAD = BLK - N_META
LP = PAD + N_META + SEQ
NBLK = LP // BLK
N_HEADS = 16
HEAD_DIM = 64
N_GROUPS = 2
HEADS_PER_GROUP = 8
N_STATE = 128
SSD_W = 1024
XBC_W = 1536
D_MAIN = 6656
D_IN = 6672
COL_Q, COL_K, COL_V, COL_G, COL_Z, COL_XBC = 0, 1024, 2048, 3072, 4096, 5120
N_DEV = 8
EPS = 1e-5
SB_SCALE = 0.125
SB_DEAD = -87.4
SB_MASKED = -1e30

ADAM_LR = 0.001
ADAM_B1 = 0.9
ADAM_B2 = 0.999
ADAM_EPS = 1e-08
ADAM_WD = 0.01
ADAM_STEP = 10

VMEM_LIMIT = 48 * 1024 * 1024

NN = (((1,), (0,)), ((), ()))
NT = (((1,), (1,)), ((), ()))
TN = (((0,), (0,)), ((), ()))


def _dot(a, b, dims=NN):
    return lax.dot_general(a, b, dims, preferred_element_type=F32)


def _split(x, n):
    parts = []
    r = x
    for i in range(n):
        p = r.astype(BF16)
        parts.append(p)
        if i + 1 < n:
            r = r - p.astype(F32)
    return parts


def _dot_x_exact(x, m, dims=NN, n=3):
    out = None
    for p in _split(x, n):
        t = _dot(p, m, dims)
        out = t if out is None else out + t
    return out


def _dot_exact_x(m, x, dims=NN, n=3):
    out = None
    for p in _split(x, n):
        t = _dot(m, p, dims)
        out = t if out is None else out + t
    return out


def _iota(shape, dim):
    return lax.broadcasted_iota(jnp.int32, shape, dim)


def _softplus(x):
    return jnp.maximum(x, 0.0) + jnp.log(1.0 + jnp.exp(-jnp.abs(x)))


def _sigmoid(x):
    return 1.0 / (1.0 + jnp.exp(-x))


def _params(sem=None):
    return pltpu.CompilerParams(dimension_semantics=sem, vmem_limit_bytes=VMEM_LIMIT)


_ANY = pl.BlockSpec(memory_space=pl.ANY)
_MESH = pl.DeviceIdType.MESH


def _mesh_position():
    return lax.axis_index("x"), lax.axis_index("y"), lax.axis_index("c")


def _other_chips(x, y):
    return [(1 - x, y), (x, 1 - y), (1 - x, 1 - y)]


def _gather_weights(srcs):
    n = len(srcs)

    def body(*refs):
        src, out = refs[:n], refs[n:2 * n]
        send_sems, recv_sems, loc_sems = refs[2 * n:]
        x, y, c = _mesh_position()
        sibling = (x, y, 1 - c)
        chips = _other_chips(x, y)
        relay_from = (jnp.where(c == 0, 1 - x, x), jnp.where(c == 0, y, 1 - y))
        relay_to = (jnp.where(c == 0, x, 1 - x), jnp.where(c == 0, 1 - y, y))

        def copy(a, k, block, to, from_src=False):
            slot = out[a].at[4 * block[0] + 2 * block[1] + block[2]]
            return pltpu.make_async_remote_copy(
                src_ref=src[a] if from_src else slot, dst_ref=slot,
                send_sem=send_sems.at[7 * a + k], recv_sem=recv_sems.at[7 * a + k],
                device_id=to, device_id_type=_MESH)

        local, sends = [], []
        for a in range(n):
            mine = pltpu.make_async_copy(src[a], out[a].at[4 * x + 2 * y + c], loc_sems.at[a])
            mine.start()
            local.append(mine)
            first = [copy(a, 0, (x, y, c), sibling, True)]
            first += [copy(a, 1 + j, (x, y, c), (*chip, c), True) for j, chip in enumerate(chips[:2])]
            for cp in first:
                cp.start()
            sends += first
        for a in range(n):
            for j, chip in enumerate(chips[:2]):
                copy(a, 1 + j, (*chip, c), (x, y, c)).wait_recv()
            later = [copy(a, 3, (*relay_from, c), (*relay_to, c))]
            later += [copy(a, 4 + j, (*chip, c), sibling) for j, chip in enumerate(chips[:2])]
            for cp in later:
                cp.start()
            sends += later
        for a in range(n):
            copy(a, 3, (*chips[2], c), (x, y, c)).wait_recv()
            passed = copy(a, 6, (*chips[2], c), sibling)
            passed.start()
            sends.append(passed)
        for a in range(n):
            copy(a, 0, (x, y, 1 - c), (x, y, c)).wait_recv()
            for j, chip in enumerate(chips):
                copy(a, 4 + j, (*chip, 1 - c), (x, y, c)).wait_recv()
        for cp in sends:
            cp.wait_send()
        for cp in local:
            cp.wait()

    return pl.pallas_call(
        body, name="gather_weights",
        out_shape=tuple(jax.ShapeDtypeStruct((N_DEV,) + s.shape, s.dtype) for s in srcs),
        in_specs=[_ANY] * n, out_specs=tuple([_ANY] * n),
        scratch_shapes=[pltpu.SemaphoreType.DMA((7 * n,)), pltpu.SemaphoreType.DMA((7 * n,)),
                        pltpu.SemaphoreType.DMA((n,))],
    )(*srcs)


_ALL_GATHER_SEMS = [pltpu.SemaphoreType.DMA((7,)), pltpu.SemaphoreType.DMA((7,)), pltpu.SemaphoreType.DMA]


def _all_gather_copies(src, out, send_sems, recv_sems, loc_sem):
    x, y, c = _mesh_position()
    me = 4 * x + 2 * y + c
    copies = [pltpu.make_async_copy(src, out.at[me], loc_sem)]
    for k in range(1, N_DEV):
        peer = (1 - x if k & 4 else x, 1 - y if k & 2 else y, 1 - c if k & 1 else c)
        copies.append(pltpu.make_async_remote_copy(
            src_ref=src, dst_ref=out.at[me], send_sem=send_sems.at[k - 1], recv_sem=recv_sems.at[k - 1],
            device_id=peer, device_id_type=_MESH))
    return copies


def _swap_with_sibling(name, src, whole=False):
    n = 1 if whole else src.shape[0]

    def body(src_ref, out_ref, send_sems, recv_sems):
        x, y, c = _mesh_position()
        copies = []
        for k in range(n):
            copies.append(pltpu.make_async_remote_copy(
                src_ref=src_ref if whole else src_ref.at[k, 1 - c], dst_ref=out_ref if whole else out_ref.at[k],
                send_sem=send_sems.at[k], recv_sem=recv_sems.at[k], device_id=(x, y, 1 - c), device_id_type=_MESH))
        for cp in copies:
            cp.start()
        for cp in copies:
            cp.wait()

    shape = src.shape if whole else (src.shape[0],) + src.shape[2:]
    return pl.pallas_call(
        body, name=name, out_shape=jax.ShapeDtypeStruct(shape, src.dtype), in_specs=[_ANY], out_specs=_ANY,
        scratch_shapes=[pltpu.SemaphoreType.DMA((n,)), pltpu.SemaphoreType.DMA((n,))],
    )(src)


def _pair_sum(name, g, sib, rows):
    n, _, r_all, cols = g.shape
    assert r_all % rows == 0

    def body(g0_ref, g1_ref, s_ref, o_ref):
        c = lax.axis_index("c")
        mine = jnp.where(c == 0, g0_ref[0, 0].astype(F32), g1_ref[0, 0].astype(F32))
        o_ref[0] = (mine + s_ref[0].astype(F32)).astype(o_ref.dtype)

    return pl.pallas_call(
        body, name=name, out_shape=jax.ShapeDtypeStruct((n, r_all, cols), BF16), grid=(n, r_all // rows),
        in_specs=[pl.BlockSpec((1, 1, rows, cols), lambda k, i: (k, 0, i, 0)),
                  pl.BlockSpec((1, 1, rows, cols), lambda k, i: (k, 1, i, 0)),
                  pl.BlockSpec((1, rows, cols), lambda k, i: (k, i, 0))],
        out_specs=pl.BlockSpec((1, rows, cols), lambda k, i: (k, i, 0)),
        compiler_params=_params(("parallel", "parallel")),
    )(g, g, sib)


_CHIP_EXCHANGE_SEMS = [pltpu.SemaphoreType.DMA((3,)), pltpu.SemaphoreType.DMA((3,)), pltpu.SemaphoreType.DMA]


def _chip_exchange_copies(src, out, send_sems, recv_sems, loc_sem, whole=False, window=None):
    x, y, c = _mesh_position()
    here = 2 * x + y

    def slot(k):
        if whole:
            return src
        if window is not None:
            return src.at[:, pl.ds(pl.multiple_of(k * window[0], BLK), window[1])]
        return src.at[k]

    copies = [pltpu.make_async_copy(slot(here), out.at[here], loc_sem)]
    for j, chip in enumerate(_other_chips(x, y)):
        copies.append(pltpu.make_async_remote_copy(
            src_ref=slot(2 * chip[0] + chip[1]), dst_ref=out.at[here],
            send_sem=send_sems.at[j], recv_sem=recv_sems.at[j], device_id=(*chip, c), device_id_type=_MESH))
    return copies


def _exchange_between_chips(name, src, whole=False):
    def body(src_ref, out_ref, send_sems, recv_sems, loc_sem):
        copies = _chip_exchange_copies(src_ref, out_ref, send_sems, recv_sems, loc_sem, whole)
        for cp in copies:
            cp.start()
        for cp in copies:
            cp.wait()

    return pl.pallas_call(
        body, name=name, out_shape=jax.ShapeDtypeStruct(((4,) + src.shape) if whole else src.shape, src.dtype),
        in_specs=[_ANY], out_specs=_ANY, scratch_shapes=_CHIP_EXCHANGE_SEMS,
    )(src)


def _matmul(name, a, b, kind, tm, tn, tk, out_dtype=F32):
    if kind == "nn":
        (m, kk), (_, nn_) = a.shape, b.shape
        a_spec = pl.BlockSpec((tm, tk), lambda i, j, k: (i, k))
        b_spec = pl.BlockSpec((tk, tn), lambda i, j, k: (k, j))
        dims = NN
    elif kind == "nt":
        (m, kk), (nn_, _) = a.shape, b.shape
        a_spec = pl.BlockSpec((tm, tk), lambda i, j, k: (i, k))
        b_spec = pl.BlockSpec((tn, tk), lambda i, j, k: (j, k))
        dims = NT
    else:
        (kk, m), (_, nn_) = a.shape, b.shape
        a_spec = pl.BlockSpec((tk, tm), lambda i, j, k: (k, i))
        b_spec = pl.BlockSpec((tk, tn), lambda i, j, k: (k, j))
        dims = TN
    assert m % tm == 0 and nn_ % tn == 0 and kk % tk == 0
    nk = kk // tk

    def body(a_ref, b_ref, o_ref, acc_ref):
        k = pl.program_id(2)
        part = _dot(a_ref[...], b_ref[...], dims)
        if nk == 1:
            o_ref[...] = part.astype(o_ref.dtype)
        else:
            @pl.when(k == 0)
            def _():
                acc_ref[...] = part

            @pl.when(k > 0)
            def _():
                acc_ref[...] += part

            @pl.when(k == nk - 1)
            def _():
                o_ref[...] = acc_ref[...].astype(o_ref.dtype)

    return pl.pallas_call(
        body, name=name, out_shape=jax.ShapeDtypeStruct((m, nn_), out_dtype),
        grid=(m // tm, nn_ // tn, nk),
        in_specs=[a_spec, b_spec], out_specs=pl.BlockSpec((tm, tn), lambda i, j, k: (i, j)),
        scratch_shapes=[pltpu.VMEM((tm, tn) if nk > 1 else (8, 128), F32)],
        compiler_params=_params(("parallel", "parallel", "arbitrary")),
    )(a, b)


def _row_spec(width, col=0):
    return pl.BlockSpec((BLK, width), lambda i: (i, col))


def _const_spec(shape):
    return pl.BlockSpec(shape, lambda i: tuple(0 for _ in shape))


def _prenorm(h_pad, norm_w, wdt, wdt_t):
    def body(h_ref, w_ref, wdt_ref, wdtt_ref, u_ref, dt_ref, dtt_ref):
        xv = h_ref[...]
        r = lax.rsqrt(jnp.mean(xv * xv, axis=-1, keepdims=True) + EPS)
        u = (xv * r * w_ref[...]).astype(BF16)
        u_ref[...] = u
        dt_ref[...] = _dot(u, wdt_ref[...], NN)
        dtt_ref[...] = _dot(wdtt_ref[...], u, NT)

    return pl.pallas_call(
        body, name="prenorm",
        out_shape=(jax.ShapeDtypeStruct((LP, D_MODEL), BF16), jax.ShapeDtypeStruct((LP, N_HEADS), F32),
                   jax.ShapeDtypeStruct((N_HEADS, LP), F32)),
        grid=(NBLK,),
        in_specs=[_row_spec(D_MODEL), _const_spec((1, D_MODEL)), _const_spec((D_MODEL, N_HEADS)),
                  _const_spec((N_HEADS, D_MODEL))],
        out_specs=(_row_spec(D_MODEL), _row_spec(N_HEADS), pl.BlockSpec((N_HEADS, BLK), lambda i: (0, i))),
        compiler_params=_params(("parallel",)),
    )(h_pad, norm_w, wdt, wdt_t)


def _gated_norm(o, g, w):
    a = o * (g * _sigmoid(g))
    r = lax.rsqrt(jnp.mean(a * a, axis=-1, keepdims=True) + EPS)
    return a * r * w


def _ycat(o_sb, proj, o_ssd, sb_w, ssd_w):
    def body(osb_ref, g_ref, ossd_ref, z_ref, sbw_ref, ssdw_ref, y_ref):
        y_ref[:, :SSD_W] = _gated_norm(osb_ref[...], g_ref[...], sbw_ref[...]).astype(BF16)
        y_ref[:, SSD_W:] = _gated_norm(ossd_ref[...], z_ref[...], ssdw_ref[...]).astype(BF16)

    return pl.pallas_call(
        body, name="ycat", out_shape=jax.ShapeDtypeStruct((LP, 2 * SSD_W), BF16), grid=(NBLK,),
        in_specs=[_row_spec(1024), _row_spec(1024, COL_G // 1024), _row_spec(1024),
                  _row_spec(1024, COL_Z // 1024), _const_spec((1, 1024)), _const_spec((1, 1024))],
        out_specs=_row_spec(2 * SSD_W),
        compiler_params=_params(("parallel",)),
    )(o_sb, proj, o_ssd, proj, sb_w, ssd_w)


def _loss_head(h_pad, yo, fnw, target):
    def body(h_ref, yo_ref, w_ref, t_ref, dh2_ref, dh2b_ref, loss_ref, dw_ref):
        i = pl.program_id(0)

        @pl.when(i == 0)
        def _():
            loss_ref[...] = jnp.zeros_like(loss_ref)
            dw_ref[...] = jnp.zeros_like(dw_ref)

        h2 = h_ref[...] + yo_ref[...]
        r = lax.rsqrt(jnp.mean(h2 * h2, axis=-1, keepdims=True) + EPS)
        nrm = h2 * r
        w = w_ref[...]
        live = i > 0
        err = jnp.where(live, nrm * w - t_ref[...], 0.0)
        dout = err * (1.0 / D_MODEL)
        loss_ref[...] += (0.5 / D_MODEL) * _fold_lanes(jnp.sum(err * err, axis=0, keepdims=True))
        dw_ref[...] += jnp.sum(dout * nrm, axis=0, keepdims=True)
        wd = dout * w
        dh2 = r * (wd - nrm * jnp.mean(wd * nrm, axis=-1, keepdims=True))
        dh2_ref[...] = dh2
        dh2b_ref[...] = dh2.astype(BF16)

    return pl.pallas_call(
        body, name="loss_head",
        out_shape=(jax.ShapeDtypeStruct((LP, D_MODEL), F32), jax.ShapeDtypeStruct((LP, D_MODEL), BF16),
                   jax.ShapeDtypeStruct((1, BLK), F32), jax.ShapeDtypeStruct((1, D_MODEL), F32)),
        grid=(NBLK,),
        in_specs=[_row_spec(D_MODEL), _row_spec(D_MODEL), _const_spec((1, D_MODEL)),
                  pl.BlockSpec((BLK, D_MODEL), lambda i: (jnp.maximum(i - 1, 0), 0))],
        out_specs=(_row_spec(D_MODEL), _row_spec(D_MODEL), _const_spec((1, BLK)), _const_spec((1, D_MODEL))),
        compiler_params=_params(("arbitrary",)),
    )(h_pad, yo, fnw, target)


def _fold_lanes(row):
    out = row[:, :BLK]
    for j in range(1, row.shape[1] // BLK):
        out = out + row[:, j * BLK:(j + 1) * BLK]
    return out


def _gated_norm_bwd(dy, o, g, w):
    s = _sigmoid(g)
    sg = g * s
    a = o * sg
    r = lax.rsqrt(jnp.mean(a * a, axis=-1, keepdims=True) + EPS)
    nrm = a * r
    dw = jnp.sum(dy * nrm, axis=0, keepdims=True)
    wd = dy * w
    da = r * (wd - nrm * jnp.mean(wd * nrm, axis=-1, keepdims=True))
    return da * sg, da * o * (s * (1.0 + g * (1.0 - s))), dw


def _ycat_bwd(dycat, o_sb, proj, o_ssd, sb_w, ssd_w):
    def body(dy_ref, osb_ref, g_ref, ossd_ref, z_ref, sbw_ref, ssdw_ref,
             dosb_ref, dg_ref, dossd_ref, dz_ref, dsbw_ref, dssdw_ref):
        @pl.when(pl.program_id(0) == 0)
        def _():
            dsbw_ref[...] = jnp.zeros_like(dsbw_ref)
            dssdw_ref[...] = jnp.zeros_like(dssdw_ref)

        do, dg, dw = _gated_norm_bwd(dy_ref[:, :SSD_W], osb_ref[...], g_ref[...], sbw_ref[...])
        dosb_ref[...] = do
        dg_ref[...] = dg.astype(BF16)
        dsbw_ref[...] += dw
        do, dg, dw = _gated_norm_bwd(dy_ref[:, SSD_W:], ossd_ref[...], z_ref[...], ssdw_ref[...])
        dossd_ref[...] = do
        dz_ref[...] = dg.astype(BF16)
        dssdw_ref[...] += dw

    act = jax.ShapeDtypeStruct((LP, 1024), F32)
    gate = jax.ShapeDtypeStruct((LP, 1024), BF16)
    vec = jax.ShapeDtypeStruct((1, 1024), F32)
    return pl.pallas_call(
        body, name="ycat_bwd", out_shape=(act, gate, act, gate, vec, vec), grid=(NBLK,),
        in_specs=[_row_spec(2048), _row_spec(1024), _row_spec(1024, COL_G // 1024), _row_spec(1024),
                  _row_spec(1024, COL_Z // 1024), _const_spec((1, 1024)), _const_spec((1, 1024))],
        out_specs=(_row_spec(1024), _row_spec(1024), _row_spec(1024), _row_spec(1024),
                   _const_spec((1, 1024)), _const_spec((1, 1024))),
        compiler_params=_params(("arbitrary",)),
    )(dycat, o_sb, proj, o_ssd, proj, sb_w, ssd_w)


_DPROJ_PIECES = (("dq", COL_Q, 1024), ("dk", COL_K, 1024), ("dv", COL_V, 1024), ("dg", COL_G, 1024),
                 ("dz", COL_Z, 1024), ("dxbc", COL_XBC, XBC_W), ("ddt", D_MAIN, BLK))
W_IN_PAD = D_MAIN + 512
WIN_STRIDE = 13 * BLK
WIN_WIDTH = 14 * BLK


def _d_w_in(u, pieces):
    tm, tn = 512, 512
    nj = W_IN_PAD // tn
    main = _DPROJ_PIECES[:-1]

    def body(*refs):
        u_ref, piece_refs, (ddt_ref, o_ref) = refs[0], refs[1:1 + len(main)], refs[1 + len(main):]
        j = pl.program_id(1)
        a = u_ref[...]
        for (_, col, width), ref in zip(main, piece_refs):
            @pl.when(jnp.logical_and(j >= col // tn, j < (col + width) // tn))
            def _():
                o_ref[...] = _dot(a, ref[...], TN).astype(BF16)

        @pl.when(j == nj - 1)
        def _():
            o_ref[...] = jnp.zeros_like(o_ref)
            o_ref[:, :BLK] = _dot(a, ddt_ref[...].astype(BF16), TN).astype(BF16)

    def piece_spec(col, width):
        return pl.BlockSpec((LP, tn), lambda i, j: (0, jnp.clip(j - col // tn, 0, width // tn - 1)))

    return pl.pallas_call(
        body, name="d_w_in", out_shape=jax.ShapeDtypeStruct((D_MODEL, W_IN_PAD), BF16), grid=(D_MODEL // tm, nj),
        in_specs=[pl.BlockSpec((LP, tm), lambda i, j: (0, i))] + [piece_spec(col, width) for _, col, width in main]
        + [pl.BlockSpec((LP, BLK), lambda i, j: (0, 0))],
        out_specs=pl.BlockSpec((tm, tn), lambda i, j: (i, j)),
        compiler_params=_params(("parallel", "arbitrary")),
    )(u, *pieces)


def _d_u_prenorm_bwd(pieces, w_main, wdt, h_pad, norm_w, dh2, chip_sum):
    tm, tk = LP // 4, 512
    nk = D_MAIN // tk
    main = _DPROJ_PIECES[:-1]
    window = (WIN_STRIDE, WIN_WIDTH)

    def body(*refs):
        piece_refs = refs[:len(main)]
        (b_ref, ddt_ref, wdt_ref, h_ref, w_ref, dh2_ref, src_ref, dh_ref, dw_ref, out_ref,
         acc_ref, send_sems, recv_sems, loc_sem) = refs[len(main):]
        i, k = pl.program_id(0), pl.program_id(1)

        @pl.when(jnp.logical_and(i == 0, k == 0))
        def _():
            for cp in _chip_exchange_copies(src_ref, out_ref, send_sems, recv_sems, loc_sem, window=window):
                cp.start()
            dw_ref[...] = jnp.zeros_like(dw_ref)

        @pl.when(k == 0)
        def _():
            acc_ref[...] = jnp.zeros_like(acc_ref)

        for (_, col, width), ref in zip(main, piece_refs):
            for lo in range(0, width, tk):
                @pl.when(k == (col + lo) // tk)
                def _():
                    acc_ref[...] += _dot(ref[:, lo:lo + tk], b_ref[...], NT)

        @pl.when(k == nk - 1)
        def _():
            dut = acc_ref[...] + _dot(ddt_ref[:, :N_HEADS].astype(BF16), wdt_ref[...], NT)
            xv = h_ref[...]
            r = lax.rsqrt(jnp.mean(xv * xv, axis=-1, keepdims=True) + EPS)
            nrm = xv * r
            dw_ref[...] += jnp.sum(dut * nrm, axis=0, keepdims=True)
            wd = dut * w_ref[...]
            dh_ref[...] = dh2_ref[...] + r * (wd - nrm * jnp.mean(wd * nrm, axis=-1, keepdims=True))

        @pl.when(jnp.logical_and(i == LP // tm - 1, k == nk - 1))
        def _():
            for cp in _chip_exchange_copies(src_ref, out_ref, send_sems, recv_sems, loc_sem, window=window):
                cp.wait()

    rows = lambda width: pl.BlockSpec((tm, width), lambda i, k: (i, 0))
    const = lambda shape: pl.BlockSpec(shape, lambda i, k: (0, 0))
    return pl.pallas_call(
        body, name="d_u_prenorm_bwd",
        out_shape=(jax.ShapeDtypeStruct((LP, D_MODEL), F32), jax.ShapeDtypeStruct((1, D_MODEL), F32),
                   jax.ShapeDtypeStruct((4, chip_sum.shape[0], WIN_WIDTH), chip_sum.dtype)),
        grid=(LP // tm, nk),
        in_specs=[rows(width) for _, _, width in main]
        + [pl.BlockSpec((D_MODEL, tk), lambda i, k: (0, k)), rows(BLK), const((D_MODEL, N_HEADS)), rows(D_MODEL),
           const((1, D_MODEL)), rows(D_MODEL), _ANY],
        out_specs=(rows(D_MODEL), const((1, D_MODEL)), _ANY),
        scratch_shapes=[pltpu.VMEM((tm, D_MODEL), F32)] + _CHIP_EXCHANGE_SEMS,
        compiler_params=_params(("arbitrary", "arbitrary")),
    )(*pieces[:-1], w_main, pieces[-1], wdt, h_pad, norm_w, dh2, chip_sum)


def _suffix_sum(vals, tri):
    return _dot_x_exact(vals, tri, NN, n=2)


def _sb_tile(z):
    t = jnp.exp(-jnp.abs(z))
    inv = 1.0 / (1.0 + t)
    sp = jnp.maximum(z, 0.0) + jnp.log(1.0 + t)
    sig = jnp.where(z >= 0, inv, t * inv)
    return sig, -sp, z - sp


def _sweep(first, step, init, run_slots):
    def alive_of(state):
        top = state[run_slots[0]]
        for s in run_slots[1:]:
            top = jnp.maximum(top, state[s])
        return (jnp.max(top) > SB_DEAD).astype(jnp.int32)

    def cond(carry):
        return jnp.logical_and(carry[0] >= 0, carry[1] > 0)

    def body(carry):
        state = step(carry[0], tuple(carry[2:]))
        return (carry[0] - 1, alive_of(state)) + tuple(state)

    return lax.while_loop(cond, body, (first, alive_of(init)) + tuple(init))[2:]


def _head_masks(x, lane):
    head0 = lane < HEAD_DIM
    return [jnp.where(head0, x, 0.0).astype(BF16), jnp.where(head0, 0.0, x).astype(BF16)]


PAIRS_PER_STEP = 2
STEP_W = PAIRS_PER_STEP * BLK
STEP_HEADS = 2 * PAIRS_PER_STEP


def _pair_lanes(h):
    lo = (h // 2) * BLK
    return slice(lo, lo + BLK)


def _stage_kv(qi, k_ref, v_ref, kb_ref, vb_ref):
    @pl.when(qi == 0)
    def _():
        kb_ref[:BLK, :] = jnp.zeros((BLK, STEP_W), BF16)
        vb_ref[:BLK, :] = jnp.zeros((BLK, STEP_W), BF16)
        kb_ref[BLK:, :] = k_ref[...].astype(BF16)
        vb_ref[BLK:, :] = v_ref[...].astype(BF16)


def _window_masks(qi):
    col2 = (qi - 1) * BLK + _iota((BLK, 2 * BLK), 1)
    row2 = qi * BLK + _iota((BLK, 2 * BLK), 0)
    valid2 = jnp.logical_and(col2 < row2, col2 >= PAD)
    row1 = qi * BLK + _iota((BLK, BLK), 0)
    lane = _iota((BLK, BLK), 1)

    def valid1(kblk):
        col = kblk * BLK + lane
        return jnp.logical_and(col < row1, col >= PAD)

    return valid2, valid1


def _strict_upper(n):
    return (_iota((n, n), 0) > _iota((n, n), 1)).astype(BF16)


def _sb_attention_fwd(proj, w_out_shard):
    heads = range(STEP_HEADS)
    n_groups = N_HEADS // STEP_HEADS

    def body(q_ref, k_ref, v_ref, src_ref, o_ref, ox_ref, out_ref, kb_ref, vb_ref, send_sems, recv_sems, loc_sem):
        grp, qi = pl.program_id(0), pl.program_id(1)
        _stage_kv(qi, k_ref, v_ref, kb_ref, vb_ref)

        @pl.when(jnp.logical_and(grp == 0, qi == 0))
        def _():
            for cp in _all_gather_copies(src_ref, out_ref, send_sems, recv_sems, loc_sem):
                cp.start()

        lane = _iota((BLK, BLK), 1)
        qh = []
        for p in range(PAIRS_PER_STEP):
            qh += _head_masks(q_ref[:, p * BLK:(p + 1) * BLK] * SB_SCALE, lane)
        valid2, valid1 = _window_masks(qi)

        def tiles(rows, valid, tri, runs):
            ks = [kb_ref[rows, _pair_lanes(h)] for h in heads]
            vs = [vb_ref[rows, _pair_lanes(h)] for h in heads]
            parts = [_sb_tile(jnp.where(valid, _dot(qh[h], ks[h], NT), SB_MASKED)) for h in heads]
            afters = [_suffix_sum(parts[h][1], tri) for h in heads]
            if runs is not None:
                afters = [afters[h] + runs[h] for h in heads]
            avals = [jnp.exp(parts[h][2] + afters[h]) for h in heads]
            his = [avals[h].astype(BF16) for h in heads]
            accs = [_dot(his[h], vs[h], NN) for h in heads]
            rests = [_dot((avals[h] - his[h].astype(F32)).astype(BF16), vs[h], NN) for h in heads]
            return [(jnp.sum(parts[h][1], axis=1, keepdims=True), accs[h], rests[h]) for h in heads]

        win = pl.ds(pl.multiple_of(qi * BLK, BLK), 2 * BLK)
        init = [t for head in tiles(win, valid2, _strict_upper(2 * BLK), None) for t in head]
        tri1 = _strict_upper(BLK)

        def step(kblk, carry):
            rows = pl.ds(pl.multiple_of((kblk + 1) * BLK, BLK), BLK)
            runs = [carry[3 * h] for h in heads]
            new = []
            for h, (d_run, d_acc, d_rest) in enumerate(tiles(rows, valid1(kblk), tri1, runs)):
                new += [carry[3 * h] + d_run, carry[3 * h + 1] + d_acc, carry[3 * h + 2] + d_rest]
            return tuple(new)

        res = _sweep(qi - 2, step, init, tuple(3 * h for h in heads))
        for p in range(PAIRS_PER_STEP):
            o = jnp.where(lane < HEAD_DIM, res[6 * p + 1], res[6 * p + 4])
            o_ref[:, p * BLK:(p + 1) * BLK] = o
            ox_ref[:, p * BLK:(p + 1) * BLK] = o + jnp.where(lane < HEAD_DIM, res[6 * p + 2], res[6 * p + 5])

        @pl.when(jnp.logical_and(grp == n_groups - 1, qi == NBLK - 1))
        def _():
            for cp in _all_gather_copies(src_ref, out_ref, send_sems, recv_sems, loc_sem):
                cp.wait()

    act = jax.ShapeDtypeStruct((LP, 1024), F32)
    blk = lambda col: pl.BlockSpec((BLK, STEP_W), lambda g, qi: (qi, col + g))
    whole = lambda col: pl.BlockSpec((LP, STEP_W), lambda g, qi: (0, col + g))
    return pl.pallas_call(
        body, name="sb_attn_fwd",
        out_shape=(act, act, jax.ShapeDtypeStruct((N_DEV,) + w_out_shard.shape, w_out_shard.dtype)),
        grid=(n_groups, NBLK),
        in_specs=[blk(COL_Q // STEP_W), whole(COL_K // STEP_W), whole(COL_V // STEP_W), _ANY],
        out_specs=(blk(0), blk(0), _ANY),
        scratch_shapes=[pltpu.VMEM((LP + BLK, STEP_W), BF16), pltpu.VMEM((LP + BLK, STEP_W), BF16)] + _ALL_GATHER_SEMS,
        compiler_params=_params(("arbitrary", "arbitrary")),
    )(proj, proj, proj, w_out_shard)


def _sb_attention_bwd(proj, o_sb, do_sb, chip_sums):
    heads = range(STEP_HEADS)
    n_groups = N_HEADS // STEP_HEADS

    def body(q_ref, k_ref, v_ref, o_ref, do_ref, src_ref, dq_ref, dk_ref, dv_ref, out_ref,
             kb_ref, vb_ref, dka_ref, dva_ref, send_sems, recv_sems, loc_sem):
        grp, qi = pl.program_id(0), pl.program_id(1)
        _stage_kv(qi, k_ref, v_ref, kb_ref, vb_ref)

        @pl.when(jnp.logical_and(grp == 0, qi == 0))
        def _():
            for cp in _chip_exchange_copies(src_ref, out_ref, send_sems, recv_sems, loc_sem):
                cp.start()

        @pl.when(qi == 0)
        def _():
            dka_ref[...] = jnp.zeros_like(dka_ref)
            dva_ref[...] = jnp.zeros_like(dva_ref)

        lane = _iota((BLK, BLK), 1)
        head0 = lane < HEAD_DIM
        qh, doh, dsum = [], [], []
        for p in range(PAIRS_PER_STEP):
            lanes = slice(p * BLK, (p + 1) * BLK)
            qh += _head_masks(q_ref[:, lanes] * SB_SCALE, lane)
            do = do_ref[:, lanes]
            doh += _head_masks(do, lane)
            prod = do.astype(BF16).astype(F32) * o_ref[:, lanes]
            dsum += [jnp.sum(jnp.where(head0, prod, 0.0), axis=1, keepdims=True),
                     jnp.sum(jnp.where(head0, 0.0, prod), axis=1, keepdims=True)]
        valid2, valid1 = _window_masks(qi)

        def tiles(rows, valid, tri, runs, eruns):
            ks = [kb_ref[rows, _pair_lanes(h)] for h in heads]
            vs = [vb_ref[rows, _pair_lanes(h)] for h in heads]
            parts = [_sb_tile(jnp.where(valid, _dot(qh[h], ks[h], NT), SB_MASKED)) for h in heads]
            afters = [_suffix_sum(parts[h][1], tri) for h in heads]
            if runs is not None:
                afters = [afters[h] + runs[h] for h in heads]
            avals = [jnp.exp(parts[h][2] + afters[h]) for h in heads]
            es = [avals[h] * _dot(doh[h], vs[h], NT) for h in heads]
            esufs = [_suffix_sum(es[h], tri) for h in heads]
            if eruns is not None:
                esufs = [esufs[h] + eruns[h] for h in heads]
            dzs = [(es[h] - parts[h][0] * (dsum[h] - esufs[h])).astype(BF16) for h in heads]
            dqs = [_dot(dzs[h], ks[h], NN) for h in heads]
            dks = [_dot(dzs[h], qh[h], TN) for h in heads]
            dvs = [_dot(avals[h].astype(BF16), doh[h], TN) for h in heads]
            for p in range(PAIRS_PER_STEP):
                dka_ref[rows, p * BLK:(p + 1) * BLK] += dks[2 * p] + dks[2 * p + 1]
                dva_ref[rows, p * BLK:(p + 1) * BLK] += dvs[2 * p] + dvs[2 * p + 1]
            return [(jnp.sum(parts[h][1], axis=1, keepdims=True), jnp.sum(es[h], axis=1, keepdims=True), dqs[h])
                    for h in heads]

        win = pl.ds(pl.multiple_of(qi * BLK, BLK), 2 * BLK)
        init = [t for head in tiles(win, valid2, _strict_upper(2 * BLK), None, None) for t in head]
        tri1 = _strict_upper(BLK)

        def step(kblk, carry):
            rows = pl.ds(pl.multiple_of((kblk + 1) * BLK, BLK), BLK)
            runs = [carry[3 * h] for h in heads]
            eruns = [carry[3 * h + 1] for h in heads]
            new = []
            for h, (d_run, d_erun, d_q) in enumerate(tiles(rows, valid1(kblk), tri1, runs, eruns)):
                new += [carry[3 * h] + d_run, carry[3 * h + 1] + d_erun, carry[3 * h + 2] + d_q]
            return tuple(new)

        res = _sweep(qi - 2, step, init, tuple(3 * h for h in heads))
        for p in range(PAIRS_PER_STEP):
            dq = jnp.where(head0, res[6 * p + 2], res[6 * p + 5]) * SB_SCALE
            dq_ref[:, p * BLK:(p + 1) * BLK] = dq.astype(BF16)

        @pl.when(qi == NBLK - 1)
        def _():
            dk_ref[...] = dka_ref[BLK:, :].astype(BF16)
            dv_ref[...] = dva_ref[BLK:, :].astype(BF16)

        @pl.when(jnp.logical_and(grp == n_groups - 1, qi == NBLK - 1))
        def _():
            for cp in _chip_exchange_copies(src_ref, out_ref, send_sems, recv_sems, loc_sem):
                cp.wait()

    act = jax.ShapeDtypeStruct((LP, 1024), BF16)
    blk = lambda col: pl.BlockSpec((BLK, STEP_W), lambda g, qi: (qi, col + g))
    whole = lambda col: pl.BlockSpec((LP, STEP_W), lambda g, qi: (0, col + g))
    return pl.pallas_call(
        body, name="sb_attn_bwd",
        out_shape=(act, act, act, jax.ShapeDtypeStruct(chip_sums.shape, chip_sums.dtype)),
        grid=(n_groups, NBLK),
        in_specs=[blk(COL_Q // STEP_W), whole(COL_K // STEP_W), whole(COL_V // STEP_W), blk(0), blk(0), _ANY],
        out_specs=(blk(0), whole(0), whole(0), _ANY),
        scratch_shapes=[pltpu.VMEM((LP + BLK, STEP_W), BF16), pltpu.VMEM((LP + BLK, STEP_W), BF16),
                        pltpu.VMEM((LP + BLK, STEP_W), F32), pltpu.VMEM((LP + BLK, STEP_W), F32)]
        + _CHIP_EXCHANGE_SEMS,
        compiler_params=_params(("arbitrary", "arbitrary")),
    )(proj, proj, proj, o_sb, do_sb, chip_sums)


def _conv_pre(x, w_ref, b_ref):
    acc = b_ref[...] + w_ref[3:4, :] * x
    for k in range(3):
        acc = acc + w_ref[k:k + 1, :] * pltpu.roll(x, 3 - k, 0)
    return acc


def _conv_fwd(proj, conv_w, conv_b):
    def body(x_ref, w_ref, b_ref, o_ref):
        xc = _conv_pre(x_ref[...], w_ref, b_ref)
        o_ref[...] = xc * _sigmoid(xc)

    nb = XBC_W // BLK
    return pl.pallas_call(
        body, name="conv_fwd", out_shape=jax.ShapeDtypeStruct((LP, XBC_W), F32), grid=(nb,),
        in_specs=[pl.BlockSpec((LP, BLK), lambda j: (0, COL_XBC // BLK + j)),
                  pl.BlockSpec((4, BLK), lambda j: (0, j)), pl.BlockSpec((1, BLK), lambda j: (0, j))],
        out_specs=pl.BlockSpec((LP, BLK), lambda j: (0, j)),
        compiler_params=_params(("parallel",)),
    )(proj, conv_w, conv_b)


def _conv_bwd(dxa, proj, conv_w, conv_b):
    def body(d_ref, x_ref, w_ref, b_ref, dx_ref, dw_ref, db_ref):
        x = x_ref[...]
        xc = _conv_pre(x, w_ref, b_ref)
        s = _sigmoid(xc)
        live = _iota((LP, BLK), 0) >= PAD
        dxc = jnp.where(live, d_ref[...] * (s * (1.0 + xc * (1.0 - s))), 0.0)
        db_ref[...] = jnp.sum(dxc, axis=0, keepdims=True)
        dx = w_ref[3:4, :] * dxc
        dw_ref[3:4, :] = jnp.sum(dxc * x, axis=0, keepdims=True)
        for k in range(3):
            dw_ref[k:k + 1, :] = jnp.sum(dxc * pltpu.roll(x, 3 - k, 0), axis=0, keepdims=True)
            dx = dx + w_ref[k:k + 1, :] * pltpu.roll(dxc, LP - (3 - k), 0)
        dx_ref[...] = dx.astype(BF16)

    nb = XBC_W // BLK
    col = lambda j: (0, j)
    return pl.pallas_call(
        body, name="conv_bwd",
        out_shape=(jax.ShapeDtypeStruct((LP, XBC_W), BF16), jax.ShapeDtypeStruct((4, XBC_W), F32),
                   jax.ShapeDtypeStruct((1, XBC_W), F32)),
        grid=(nb,),
        in_specs=[pl.BlockSpec((LP, BLK), col), pl.BlockSpec((LP, BLK), lambda j: (0, COL_XBC // BLK + j)),
                  pl.BlockSpec((4, BLK), col), pl.BlockSpec((1, BLK), col)],
        out_specs=(pl.BlockSpec((LP, BLK), col), pl.BlockSpec((4, BLK), col), pl.BlockSpec((1, BLK), col)),
        compiler_params=_params(("parallel",)),
    )(dxa, proj, conv_w, conv_b)


def _ssd_prelude(c, dt_ref, dtt_ref, dtb_ref, dtbt_ref, alog_ref, alogt_ref):
    live = jnp.logical_or(c > 0, _iota((BLK, N_HEADS), 0) >= PAD)
    live_t = jnp.logical_or(c > 0, _iota((N_HEADS, BLK), 1) >= PAD)
    pre = dt_ref[...] + dtb_ref[...]
    pre_t = dtt_ref[...] + dtbt_ref[...]
    dt = jnp.where(live, _softplus(pre), 0.0)
    dt_t = jnp.where(live_t, _softplus(pre_t), 0.0)
    a = -jnp.exp(alog_ref[...])
    a_t = -jnp.exp(alogt_ref[...])
    li = _iota((BLK, BLK), 0)
    si = _iota((BLK, BLK), 1)
    lower = (si <= li).astype(BF16)
    upper = (li <= si).astype(BF16)
    acum = _dot_exact_x(lower, dt * a, NN)
    acum_t = _dot_x_exact(dt_t * a_t, upper, NN)
    return live, pre, dt, a, a_t, acum, acum_t


def _head_expand():
    return (_iota((N_HEADS, SSD_W), 1) // HEAD_DIM == _iota((N_HEADS, SSD_W), 0)).astype(BF16)


def _head_reduce_mat():
    return (_iota((SSD_W, N_HEADS), 0) // HEAD_DIM == _iota((SSD_W, N_HEADS), 1)).astype(BF16)


def _decay_mat(acum, acum_t, h, causal):
    seg = jnp.minimum(acum[:, h:h + 1] - acum_t[h:h + 1, :], 0.0)
    return jnp.where(causal, jnp.exp(seg), 0.0)


def _ssd_specs():
    chunk = lambda width, col=0: pl.BlockSpec((BLK, width), lambda c: (c, col))
    return chunk


def _ssd_fwd(xa, dt_raw, dt_raw_t, dt_bias, dt_bias_t, a_log, a_log_t, d_exp):
    def body(x_ref, b_ref, c_ref, dt_ref, dtt_ref, dtb_ref, dtbt_ref, alog_ref, alogt_ref, dexp_ref,
             y_ref, hs_ref, state_ref):
        c = pl.program_id(0)

        @pl.when(c == 0)
        def _():
            state_ref[...] = jnp.zeros_like(state_ref)

        _, _, dt, _, _, acum, acum_t = _ssd_prelude(c, dt_ref, dtt_ref, dtb_ref, dtbt_ref, alog_ref, alogt_ref)
        expand = _head_expand()
        x = x_ref[...]
        xdt = x * _dot_x_exact(dt, expand)
        exp_a = _dot_x_exact(jnp.exp(acum), expand)
        to_end = _dot_x_exact(jnp.exp(acum[BLK - 1:BLK, :] - acum), expand)
        xdt_b = xdt.astype(BF16)
        xd_b = (xdt * to_end).astype(BF16)
        chunk_decay = jnp.exp(acum_t[:, BLK - 1:BLK])
        hs_ref[0] = state_ref[...]
        lane = _iota((BLK, BLK), 1)
        causal = _iota((BLK, BLK), 0) >= lane
        gw = HEADS_PER_GROUP * HEAD_DIM
        for g in range(N_GROUPS):
            bg = b_ref[:, g * N_STATE:(g + 1) * N_STATE].astype(BF16)
            cg = c_ref[:, g * N_STATE:(g + 1) * N_STATE].astype(BF16)
            cb = _dot(cg, bg, NT)
            hg = state_ref[g * gw:(g + 1) * gw, :]
            ch = _dot(cg, hg.astype(BF16), NT)
            st = _dot(xd_b[:, g * gw:(g + 1) * gw], bg, TN)
            for p in range(HEADS_PER_GROUP // 2):
                h0 = g * HEADS_PER_GROUP + 2 * p
                lo = h0 * HEAD_DIM
                xp = xdt_b[:, lo:lo + BLK]
                w0 = (cb * _decay_mat(acum, acum_t, h0, causal)).astype(BF16)
                w1 = (cb * _decay_mat(acum, acum_t, h0 + 1, causal)).astype(BF16)
                yd = jnp.where(lane < HEAD_DIM, _dot(w0, xp), _dot(w1, xp))
                y_ref[:, lo:lo + BLK] = (yd + ch[:, lo - g * gw:lo - g * gw + BLK] * exp_a[:, lo:lo + BLK]
                                         + x[:, lo:lo + BLK] * dexp_ref[:, lo:lo + BLK])
            for r in range(HEADS_PER_GROUP):
                h = g * HEADS_PER_GROUP + r
                state_ref[h * HEAD_DIM:(h + 1) * HEAD_DIM, :] = (
                    hg[r * HEAD_DIM:(r + 1) * HEAD_DIM, :] * chunk_decay[h:h + 1, :]
                    + st[r * HEAD_DIM:(r + 1) * HEAD_DIM, :])

    chunk = _ssd_specs()
    return pl.pallas_call(
        body, name="ssd_fwd",
        out_shape=(jax.ShapeDtypeStruct((LP, SSD_W), F32), jax.ShapeDtypeStruct((NBLK, SSD_W, N_STATE), F32)),
        grid=(NBLK,),
        in_specs=[chunk(SSD_W), chunk(256, 4), chunk(256, 5), chunk(N_HEADS),
                  pl.BlockSpec((N_HEADS, BLK), lambda c: (0, c)), _const_spec((1, N_HEADS)),
                  _const_spec((N_HEADS, 1)), _const_spec((1, N_HEADS)), _const_spec((N_HEADS, 1)),
                  _const_spec((1, SSD_W))],
        out_specs=(chunk(SSD_W), pl.BlockSpec((1, SSD_W, N_STATE), lambda c: (c, 0, 0))),
        scratch_shapes=[pltpu.VMEM((SSD_W, N_STATE), F32)],
        compiler_params=_params(("arbitrary",)),
    )(xa, xa, xa, dt_raw, dt_raw_t, dt_bias, dt_bias_t, a_log, a_log_t, d_exp)


def _ssd_bwd(xa, dt_raw, dt_raw_t, dt_bias, dt_bias_t, a_log, a_log_t, d_exp, hstart, dy):
    def body(x_ref, b_ref, c_ref, dt_ref, dtt_ref, dtb_ref, dtbt_ref, alog_ref, alogt_ref, dexp_ref,
             hs_ref, dy_ref, dxa_ref, ddt_ref, dbias_ref, dalog_ref, dd_ref, dstate_ref):
        step = pl.program_id(0)
        c = NBLK - 1 - step

        @pl.when(step == 0)
        def _():
            dstate_ref[...] = jnp.zeros_like(dstate_ref)
            dbias_ref[...] = jnp.zeros_like(dbias_ref)
            dalog_ref[...] = jnp.zeros_like(dalog_ref)
            dd_ref[...] = jnp.zeros_like(dd_ref)

        live, pre, dt, a, a_t, acum, acum_t = _ssd_prelude(c, dt_ref, dtt_ref, dtb_ref, dtbt_ref,
                                                           alog_ref, alogt_ref)
        expand = _head_expand()
        reduce_m = _head_reduce_mat()
        x = x_ref[...]
        dyv = dy_ref[...]
        dt_e = _dot_x_exact(dt, expand)
        xdt = x * dt_e
        exp_acum = jnp.exp(acum)
        exp_a = _dot_x_exact(exp_acum, expand)
        dte = jnp.exp(acum[BLK - 1:BLK, :] - acum)
        to_end = _dot_x_exact(dte, expand)
        xdt_b = xdt.astype(BF16)
        xd_b = (xdt * to_end).astype(BF16)
        chunk_decay = jnp.exp(acum_t[:, BLK - 1:BLK])
        lane = _iota((BLK, BLK), 1)
        head0 = lane < HEAD_DIM
        causal = _iota((BLK, BLK), 0) >= lane
        gw = HEADS_PER_GROUP * HEAD_DIM
        dm = dyv * exp_a
        dm_b = dm.astype(BF16)
        onehot = lambda h: (_iota((1, N_HEADS), 1) == h).astype(F32)
        onehot_t = lambda h: (_iota((N_HEADS, 1), 0) == h).astype(F32)
        dacum = jnp.zeros((BLK, N_HEADS), F32)
        dacum_t = jnp.zeros((N_HEADS, BLK), F32)
        ddt_acc = jnp.zeros((BLK, N_HEADS), F32)
        ddte_acc = jnp.zeros((BLK, N_HEADS), F32)
        dexpa_acc = jnp.zeros((BLK, N_HEADS), F32)
        dskip_acc = jnp.zeros((BLK, N_HEADS), F32)
        head_sum = expand
        for g in range(N_GROUPS):
            bg = b_ref[:, g * N_STATE:(g + 1) * N_STATE].astype(BF16)
            cg = c_ref[:, g * N_STATE:(g + 1) * N_STATE].astype(BF16)
            cb = _dot(cg, bg, NT)
            hg = hs_ref[0, g * gw:(g + 1) * gw, :]
            hg_b = hg.astype(BF16)
            dhe = dstate_ref[g * gw:(g + 1) * gw, :]
            dhe_b = dhe.astype(BF16)
            ch = _dot(cg, hg_b, NT)
            dcg = _dot(dm_b[:, g * gw:(g + 1) * gw], hg_b, NN)
            dhs = _dot(dm_b[:, g * gw:(g + 1) * gw], cg, TN)
            dxd = _dot(bg, dhe_b, NT)
            dbg = _dot(xd_b[:, g * gw:(g + 1) * gw], dhe_b, NN)
            dcb = jnp.zeros((BLK, BLK), F32)
            for p in range(HEADS_PER_GROUP // 2):
                h0 = g * HEADS_PER_GROUP + 2 * p
                lo = h0 * HEAD_DIM
                xp = xdt_b[:, lo:lo + BLK]
                dyp = dyv[:, lo:lo + BLK]
                dyh = (jnp.where(head0, dyp, 0.0).astype(BF16), jnp.where(head0, 0.0, dyp).astype(BF16))
                dxdt_p = jnp.zeros((BLK, BLK), F32)
                for q in range(2):
                    h = h0 + q
                    dec = _decay_mat(acum, acum_t, h, causal)
                    w = cb * dec
                    dw = _dot(dyh[q], xp, NT)
                    t = dw * w
                    dacum = dacum + jnp.sum(t, axis=1, keepdims=True) * onehot(h)
                    dacum_t = dacum_t - jnp.sum(t, axis=0, keepdims=True) * onehot_t(h)
                    dcb = dcb + dw * dec
                    dxdt_p = dxdt_p + _dot(w.astype(BF16), dyh[q], TN)
                sl = slice(lo, lo + BLK)
                gl = slice(lo - g * gw, lo - g * gw + BLK)
                dxdt_p = dxdt_p + dxd[:, gl] * to_end[:, sl]
                dxa_ref[:, sl] = dyp * dexp_ref[:, sl] + dxdt_p * dt_e[:, sl]
                red = reduce_m[lo:lo + BLK, :]
                ddt_acc = ddt_acc + _dot_x_exact(dxdt_p * x[:, sl], red)
                ddte_acc = ddte_acc + _dot_x_exact(dxd[:, gl] * xdt[:, sl], red)
                dexpa_acc = dexpa_acc + _dot_x_exact(dyp * ch[:, gl], red)
                dskip_acc = dskip_acc + _dot_x_exact(dyp * x[:, sl], red)
            dcb_b = dcb.astype(BF16)
            b_col = SSD_W + g * N_STATE
            c_col = SSD_W + (N_GROUPS + g) * N_STATE
            dxa_ref[:, c_col:c_col + N_STATE] = dcg + _dot(dcb_b, bg, NN)
            dxa_ref[:, b_col:b_col + N_STATE] = dbg + _dot(dcb_b, cg, TN)
            prod = dhe * hg
            per_head = jnp.sum(_dot_exact_x(head_sum[:, g * gw:(g + 1) * gw], prod, NN), axis=1, keepdims=True)
            dacum_t = dacum_t + (per_head * chunk_decay) * (_iota((1, BLK), 1) == BLK - 1).astype(F32)
            for r in range(HEADS_PER_GROUP):
                h = g * HEADS_PER_GROUP + r
                rows = slice(h * HEAD_DIM, (h + 1) * HEAD_DIM)
                dstate_ref[rows, :] = (dhs[r * HEAD_DIM:(r + 1) * HEAD_DIM, :]
                                       + dhe[r * HEAD_DIM:(r + 1) * HEAD_DIM, :] * chunk_decay[h:h + 1, :])
        dacum = dacum + dexpa_acc * exp_acum - ddte_acc * dte
        last_row = (_iota((BLK, 1), 0) == BLK - 1).astype(F32)
        dacum = dacum + last_row * jnp.sum(ddte_acc * dte, axis=0, keepdims=True)
        li = _iota((BLK, BLK), 0)
        si = _iota((BLK, BLK), 1)
        upper = (li <= si).astype(BF16)
        lower = (si <= li).astype(BF16)
        dda = _dot_exact_x(upper, dacum, NN)
        dda_t = _dot_x_exact(dacum_t, lower, NN)
        eye = (_iota((N_HEADS, N_HEADS), 0) == _iota((N_HEADS, N_HEADS), 1)).astype(BF16)
        dda = dda + _dot_x_exact_tn(dda_t, eye)
        ddt = ddt_acc + dda * a
        dalog_ref[...] += jnp.sum(dda * dt, axis=0, keepdims=True) * a
        dd_ref[...] += jnp.sum(dskip_acc, axis=0, keepdims=True)
        ddt_raw = jnp.where(live, ddt * _sigmoid(pre), 0.0)
        ddt_ref[...] = jnp.zeros_like(ddt_ref)
        ddt_ref[:, :N_HEADS] = ddt_raw
        dbias_ref[...] += jnp.sum(ddt_raw, axis=0, keepdims=True)

    rev = lambda width, col=0: pl.BlockSpec((BLK, width), lambda s: (NBLK - 1 - s, col))
    vec = jax.ShapeDtypeStruct((1, N_HEADS), F32)
    return pl.pallas_call(
        body, name="ssd_bwd",
        out_shape=(jax.ShapeDtypeStruct((LP, XBC_W), F32), jax.ShapeDtypeStruct((LP, BLK), F32), vec, vec, vec),
        grid=(NBLK,),
        in_specs=[rev(SSD_W), rev(256, 4), rev(256, 5), rev(N_HEADS),
                  pl.BlockSpec((N_HEADS, BLK), lambda s: (0, NBLK - 1 - s)), _const_spec((1, N_HEADS)),
                  _const_spec((N_HEADS, 1)), _const_spec((1, N_HEADS)), _const_spec((N_HEADS, 1)),
                  _const_spec((1, SSD_W)),
                  pl.BlockSpec((1, SSD_W, N_STATE), lambda s: (NBLK - 1 - s, 0, 0)), rev(SSD_W)],
        out_specs=(rev(XBC_W), rev(BLK), _const_spec((1, N_HEADS)),
                   _const_spec((1, N_HEADS)), _const_spec((1, N_HEADS))),
        scratch_shapes=[pltpu.VMEM((SSD_W, N_STATE), F32)],
        compiler_params=_params(("arbitrary",)),
    )(xa, xa, xa, dt_raw, dt_raw_t, dt_bias, dt_bias_t, a_log, a_log_t, d_exp, hstart, dy)


def _dot_x_exact_tn(x_t, eye):
    out = None
    for p in _split(x_t, 3):
        t = _dot(p, eye, TN)
        out = t if out is None else out + t
    return out


def _adamw(name, parts, w, m, v, rows):
    r_all, cols = w.shape
    assert r_all % rows == 0
    c1 = 1.0 / (1.0 - ADAM_B1 ** ADAM_STEP)
    c2 = 1.0 / (1.0 - ADAM_B2 ** ADAM_STEP)

    def body(p_ref, w_ref, m_ref, v_ref, g_ref, d_ref, mo_ref, vo_ref):
        g = p_ref[0].astype(F32)
        for j in range(1, parts.shape[0]):
            g = g + p_ref[j].astype(F32)
        mn = ADAM_B1 * m_ref[...] + (1.0 - ADAM_B1) * g
        vn = ADAM_B2 * v_ref[...] + (1.0 - ADAM_B2) * (g * g)
        g_ref[...] = g
        mo_ref[...] = mn
        vo_ref[...] = vn
        d_ref[...] = -ADAM_LR * ((mn * c1) / (jnp.sqrt(vn * c2) + ADAM_EPS) + ADAM_WD * w_ref[...])

    spec = pl.BlockSpec((rows, cols), lambda i: (i, 0))
    shp = jax.ShapeDtypeStruct((r_all, cols), F32)
    return pl.pallas_call(
        body, name=name, out_shape=(shp, shp, shp, shp), grid=(r_all // rows,),
        in_specs=[pl.BlockSpec((parts.shape[0], rows, cols), lambda i: (0, i, 0)), spec, spec, spec],
        out_specs=(spec, spec, spec, spec),
        compiler_params=_params(("parallel",)),
    )(parts, w, m, v)


_VECTORS = (("norm_w", 1024, 0), ("conv_b", 1536, 8), ("dt_bias", 16, 20), ("a_log", 16, 21), ("d_skip", 16, 22),
            ("sb_norm_w", 1024, 24), ("ssd_norm_w", 1024, 32), ("final_norm_w", 1024, 40))
_LOSS_ROW = 23
_CONVW_ROW = 48
_META_ROW = 96
_PACK_ROWS = 224
_SMALL_ORDER = tuple(name for name, _, _ in _VECTORS) + ("conv_w", "meta_tokens")


def _pack_small_grads(vectors, loss_row, d_convw, dh):
    def body(*refs):
        vec_refs, (loss_ref, cw_ref, dh_ref, out_ref) = refs[:len(_VECTORS)], refs[len(_VECTORS):]
        out_ref[...] = jnp.zeros_like(out_ref)
        for (_, width, row), ref in zip(_VECTORS, vec_refs):
            if width < BLK:
                out_ref[row:row + 1, :width] = ref[...]
            else:
                for t in range(width // BLK):
                    out_ref[row + t:row + t + 1, :] = ref[:, t * BLK:(t + 1) * BLK]
        out_ref[_LOSS_ROW:_LOSS_ROW + 1, :] = loss_ref[...]
        for k in range(4):
            for t in range(XBC_W // BLK):
                r = _CONVW_ROW + k * (XBC_W // BLK) + t
                out_ref[r:r + 1, :] = cw_ref[k:k + 1, t * BLK:(t + 1) * BLK]
        for i in range(N_META):
            for t in range(D_MODEL // BLK):
                r = _META_ROW + i * (D_MODEL // BLK) + t
                out_ref[r:r + 1, :] = dh_ref[i:i + 1, t * BLK:(t + 1) * BLK]

    full = lambda a: pl.BlockSpec(a.shape, lambda i: tuple(0 for _ in a.shape))
    return pl.pallas_call(
        body, name="pack_small_grads", out_shape=jax.ShapeDtypeStruct((_PACK_ROWS, BLK), F32), grid=(1,),
        in_specs=[full(v) for v in vectors] + [full(loss_row), full(d_convw),
                                               pl.BlockSpec((N_META, D_MODEL), lambda i: (PAD // N_META, 0))],
        out_specs=pl.BlockSpec((_PACK_ROWS, BLK), lambda i: (0, 0)),
        compiler_params=_params(("arbitrary",)),
    )(*vectors, loss_row, d_convw, dh)


def _sum_slots(name, parts, rows):
    n, r_all, cols = parts.shape

    def body(p_ref, o_ref):
        acc = p_ref[0].astype(F32)
        for j in range(1, n):
            acc = acc + p_ref[j].astype(F32)
        o_ref[...] = acc

    return pl.pallas_call(
        body, name=name, out_shape=jax.ShapeDtypeStruct((r_all, cols), F32), grid=(r_all // rows,),
        in_specs=[pl.BlockSpec((n, rows, cols), lambda i: (0, i, 0))],
        out_specs=pl.BlockSpec((rows, cols), lambda i: (i, 0)),
        compiler_params=_params(("parallel",)),
    )(parts)


def _add(name, a, b):
    def body(a_ref, b_ref, o_ref):
        o_ref[...] = a_ref[...] + b_ref[...]

    spec = pl.BlockSpec(a.shape, lambda i: (0, 0))
    return pl.pallas_call(body, name=name, out_shape=jax.ShapeDtypeStruct(a.shape, a.dtype), grid=(1,),
                          in_specs=[spec, spec], out_specs=spec, compiler_params=_params(("arbitrary",)))(a, b)


def _adamw_small(chip_sums, weights, moms, vels):
    c1 = 1.0 / (1.0 - ADAM_B1 ** ADAM_STEP)
    c2 = 1.0 / (1.0 - ADAM_B2 ** ADAM_STEP)
    n = len(_SMALL_ORDER)

    def body(*refs):
        p_ref = refs[0]
        w_refs, m_refs, v_refs = refs[1:1 + n], refs[1 + n:1 + 2 * n], refs[1 + 2 * n:1 + 3 * n]
        outs = refs[1 + 3 * n:1 + 7 * n]
        loss_ref, g_ref, cw_ref, cws_ref, mt_ref = refs[1 + 7 * n:]
        x, y, c = _mesh_position()
        me = 4 * x + 2 * y + c
        g = p_ref[0]
        for j in range(1, 4):
            g = g + p_ref[j]
        g_ref[...] = g
        loss_ref[...] = g_ref[_LOSS_ROW:_LOSS_ROW + 1, :]

        def update(idx, grad):
            go_ref, d_ref, mo_ref, vo_ref = outs[4 * idx:4 * idx + 4]
            mn = ADAM_B1 * m_refs[idx][...] + (1.0 - ADAM_B1) * grad
            vn = ADAM_B2 * v_refs[idx][...] + (1.0 - ADAM_B2) * (grad * grad)
            go_ref[...] = grad
            mo_ref[...] = mn
            vo_ref[...] = vn
            d_ref[...] = -ADAM_LR * ((mn * c1) / (jnp.sqrt(vn * c2) + ADAM_EPS) + ADAM_WD * w_refs[idx][...])

        for idx, (_, width, row) in enumerate(_VECTORS):
            go_ref = outs[4 * idx]
            if width < BLK:
                grad = g_ref[row:row + 1, :width]
            else:
                for t in range(width // BLK):
                    go_ref[:, t * BLK:(t + 1) * BLK] = g_ref[row + t:row + t + 1, :]
                grad = go_ref[...]
            update(idx, grad)
        cw_ref[...] = jnp.zeros_like(cw_ref)
        for k in range(4):
            for t in range(XBC_W // BLK):
                r = _CONVW_ROW + k * (XBC_W // BLK) + t
                cw_ref[k:k + 1, t * BLK:(t + 1) * BLK] = g_ref[r:r + 1, :]
        for i in range(N_META):
            for t in range(D_MODEL // BLK):
                r = _META_ROW + i * (D_MODEL // BLK) + t
                mt_ref[i:i + 1, t * BLK:(t + 1) * BLK] = g_ref[r:r + 1, :]
        width_cw = XBC_W // N_DEV
        pick_cw = (_iota((XBC_W, width_cw), 0) == me * width_cw + _iota((XBC_W, width_cw), 1)).astype(BF16)
        cws_ref[...] = _dot_x_exact(cw_ref[...], pick_cw)
        update(n - 2, cws_ref[0:4, :])
        pick_mt = (_iota((D_MODEL, BLK), 0) == me * BLK + _iota((D_MODEL, BLK), 1)).astype(BF16)
        update(n - 1, _dot_x_exact(mt_ref[...], pick_mt))

    full = lambda a: pl.BlockSpec(a.shape, lambda i: tuple(0 for _ in a.shape))
    params = list(weights) + list(moms) + list(vels)
    out_shape, out_specs = [], []
    for w in weights:
        for _ in range(4):
            out_shape.append(jax.ShapeDtypeStruct(w.shape, F32))
            out_specs.append(full(w))
    out_shape.append(jax.ShapeDtypeStruct((1, BLK), F32))
    out_specs.append(pl.BlockSpec((1, BLK), lambda i: (0, 0)))
    return pl.pallas_call(
        body, name="adamw_small", out_shape=tuple(out_shape), grid=(1,),
        in_specs=[full(chip_sums)] + [full(a) for a in params], out_specs=tuple(out_specs),
        scratch_shapes=[pltpu.VMEM((_PACK_ROWS, BLK), F32), pltpu.VMEM((8, XBC_W), F32),
                        pltpu.VMEM((8, XBC_W // N_DEV), F32), pltpu.VMEM((N_META, D_MODEL), F32)],
        compiler_params=_params(("arbitrary",)),
    )(chip_sums, *params)


def kernel(x, meta_tokens, norm_w, w_in, conv_w, conv_b, dt_bias, a_log, d_skip, sb_norm_w, ssd_norm_w, w_out, final_norm_w, loss_target, m_meta_tokens, m_norm_w, m_w_in, m_conv_w, m_conv_b, m_dt_bias, m_a_log, m_d_skip, m_sb_norm_w, m_ssd_norm_w, m_w_out, m_final_norm_w, v_meta_tokens, v_norm_w, v_w_in, v_conv_w, v_conv_b, v_dt_bias, v_a_log, v_d_skip, v_sb_norm_w, v_ssd_norm_w, v_w_out, v_final_norm_w):
    small_src = jnp.concatenate([conv_w[0].reshape(6, BLK), meta_tokens, jnp.zeros((2, BLK), F32)], axis=0)
    small_g, w_in_g = _gather_weights([small_src, w_in[0].astype(BF16)])
    w_in_full = w_in_g.transpose(1, 0, 2).reshape(D_MODEL, D_IN)
    w_main = w_in_full[:, :D_MAIN]
    w_dt = w_in_full[:, D_MAIN:]
    conv_w_full = small_g[:, :6].reshape(N_DEV, 4, 192).transpose(1, 0, 2).reshape(4, XBC_W)
    meta_full = small_g[:, 6:6 + N_META].transpose(1, 0, 2).reshape(N_META, D_MODEL)
    h_pad = jnp.concatenate([jnp.zeros((PAD, D_MODEL), F32), meta_full, x[0]], axis=0)
    dt_bias_t = dt_bias.reshape(N_HEADS, 1)
    a_log_t = a_log.reshape(N_HEADS, 1)
    d_exp = jnp.repeat(d_skip, HEAD_DIM, axis=1)
    fnw = final_norm_w.reshape(1, D_MODEL)

    u, dt_raw, dt_raw_t = _prenorm(h_pad, norm_w, w_dt, w_dt.T)
    proj = _matmul("in_proj", u, w_main, "nn", 1088, 512, D_MODEL)
    o_sb, o_sb_exact, w_out_g = _sb_attention_fwd(proj, w_out[0].astype(BF16))
    w_out_full = w_out_g.reshape(2 * SSD_W, D_MODEL)
    xa = _conv_fwd(proj, conv_w_full, conv_b)
    o_ssd, hstart = _ssd_fwd(xa, dt_raw, dt_raw_t, dt_bias, dt_bias_t, a_log, a_log_t, d_exp)
    ycat = _ycat(o_sb, proj, o_ssd, sb_norm_w, ssd_norm_w)
    yo = _matmul("out_proj", ycat, w_out_full, "nn", 1088, 512, 2 * SSD_W)
    dh2, dh2_b, loss_row, d_fnw = _loss_head(h_pad, yo, fnw, loss_target[0])

    g_w_out = _matmul("d_w_out", ycat, dh2_b, "tn", 512, 512, LP, BF16).reshape(4, 2, 256, D_MODEL)
    chip_w_out = _pair_sum("pair_sum_w_out", g_w_out, _swap_with_sibling("swap_w_out", g_w_out), 256)
    dycat = _matmul("d_ycat", dh2_b, w_out_full, "nt", 1088, 512, D_MODEL)
    do_sb, dg, do_ssd, dz, d_sbw, d_ssdw = _ycat_bwd(dycat, o_sb, proj, o_ssd, sb_norm_w, ssd_norm_w)
    dq, dk, dv, p_w_out = _sb_attention_bwd(proj, o_sb_exact, do_sb, chip_w_out)
    dxa, ddt_raw, d_dtb, d_alog, d_dskip = _ssd_bwd(
        xa, dt_raw, dt_raw_t, dt_bias, dt_bias_t, a_log, a_log_t, d_exp, hstart, do_ssd)
    dxbc, d_convw, d_convb = _conv_bwd(dxa, proj, conv_w_full, conv_b)
    pieces = [dq, dk, dv, dg, dz, dxbc, ddt_raw]
    g_w_in = _d_w_in(u, pieces).reshape(1, 2, D_MODEL // 2, W_IN_PAD)
    chip_w_in = _pair_sum("pair_sum_w_in", g_w_in, _swap_with_sibling("swap_w_in", g_w_in), 128)
    dh, d_nw, win_parts = _d_u_prenorm_bwd(pieces, w_main, w_dt, h_pad, norm_w, dh2, chip_w_in[0])
    win_mine = _sum_slots("sum_w_in_windows", win_parts, 128)
    win_other = _swap_with_sibling("swap_w_in_window", win_mine, whole=True)
    core = lax.axis_index("c")
    chip = 2 * lax.axis_index("x") + lax.axis_index("y")
    first_col = (D_IN // N_DEV) * (2 * chip + core) - WIN_STRIDE * chip
    cut = lambda w: lax.dynamic_slice(w, (0, first_col), (D_MODEL // 2, D_IN // N_DEV))
    half_mine, half_other = cut(win_mine), cut(win_other)
    p_w_in = jnp.concatenate([jnp.where(core == 0, half_mine, half_other),
                              jnp.where(core == 0, half_other, half_mine)], axis=0)[None]

    pack = _pack_small_grads([d_nw, d_convb, d_dtb, d_alog, d_dskip, d_sbw, d_ssdw, d_fnw], loss_row, d_convw, dh)
    chip_small = _add("pair_sum_small", pack, _swap_with_sibling("swap_small", pack, whole=True))
    p_small = _exchange_between_chips("exchange_small", chip_small, whole=True)

    res_in = _adamw("adamw_w_in", p_w_in, w_in[0], m_w_in[0], v_w_in[0], 128)
    res_out = _adamw("adamw_w_out", p_w_out, w_out[0], m_w_out[0], v_w_out[0], 128)
    res_small = _adamw_small(
        p_small,
        [norm_w, conv_b, dt_bias, a_log, d_skip, sb_norm_w, ssd_norm_w, fnw, conv_w[0], meta_tokens],
        [m_norm_w, m_conv_b, m_dt_bias, m_a_log, m_d_skip, m_sb_norm_w, m_ssd_norm_w,
         m_final_norm_w.reshape(1, D_MODEL), m_conv_w[0], m_meta_tokens],
        [v_norm_w, v_conv_b, v_dt_bias, v_a_log, v_d_skip, v_sb_norm_w, v_ssd_norm_w,
         v_final_norm_w.reshape(1, D_MODEL), v_conv_w[0], v_meta_tokens])

    loss = jnp.sum(res_small[-1])
    order = ["meta_tokens", "norm_w", "w_in", "conv_w", "conv_b", "dt_bias", "a_log", "d_skip",
             "sb_norm_w", "ssd_norm_w", "w_out", "final_norm_w"]
    outs = [loss, dh[BLK:].reshape(1, SEQ, D_MODEL)]
    for kind in range(4):
        small = {name: res_small[4 * idx + kind] for idx, name in enumerate(_SMALL_ORDER)}
        small["final_norm_w"] = small["final_norm_w"].reshape(D_MODEL)
        small["conv_w"] = small["conv_w"].reshape(1, 4, XBC_W // N_DEV)
        small["w_in"] = res_in[kind].reshape(1, D_MODEL, D_IN // N_DEV)
        small["w_out"] = res_out[kind].reshape(1, 256, D_MODEL)
        outs += [small[name] for name in order]
    return tuple(outs)
```

```python
import functools

import jax
import jax.numpy as jnp
from jax import lax
from jax.experimental import pallas as pl
from jax.experimental.pallas import tpu as pltpu

F32 = jnp.float32
BF16 = jnp.bfloat16

D_MODEL = 1024
SEQ = 2048
N_META = 16
BLK = 128
PAD = BLK - N_META
LP = PAD + N_META + SEQ
NBLK = LP // BLK
N_HEADS = 16
HEAD_DIM = 64
N_GROUPS = 2
HEADS_PER_GROUP = 8
N_STATE = 128
SSD_W = 1024
XBC_W = 1536
D_MAIN = 6656
D_IN = 6672
COL_Q, COL_K, COL_V, COL_G, COL_Z, COL_XBC = 0, 1024, 2048, 3072, 4096, 5120
N_DEV = 8
EPS = 1e-5
SB_SCALE = 0.125
SB_DEAD = -87.4
SB_MASKED = -1e30

ADAM_LR = 0.001
ADAM_B1 = 0.9
ADAM_B2 = 0.999
ADAM_EPS = 1e-08
ADAM_WD = 0.01
ADAM_STEP = 10

VMEM_LIMIT = 48 * 1024 * 1024
ATTN_VMEM_LIMIT = 56 * 1024 * 1024

NN = (((1,), (0,)), ((), ()))
NT = (((1,), (1,)), ((), ()))
TN = (((0,), (0,)), ((), ()))


def _dot(a, b, dims=NN):
    return lax.dot_general(a, b, dims, preferred_element_type=F32)


def _split(x, n):
    parts = []
    r = x
    for i in range(n):
        p = r.astype(BF16)
        parts.append(p)
        if i + 1 < n:
            r = r - p.astype(F32)
    return parts


def _dot_x_exact(x, m, dims=NN, n=3):
    out = None
    for p in _split(x, n):
        t = _dot(p, m, dims)
        out = t if out is None else out + t
    return out


def _dot_exact_x(m, x, dims=NN, n=3):
    out = None
    for p in _split(x, n):
        t = _dot(m, p, dims)
        out = t if out is None else out + t
    return out


def _iota(shape, dim):
    return lax.broadcasted_iota(jnp.int32, shape, dim)


def _softplus(x):
    return jnp.maximum(x, 0.0) + jnp.log(1.0 + jnp.exp(-jnp.abs(x)))


def _sigmoid(x):
    return 1.0 / (1.0 + jnp.exp(-x))


def _params(sem=None, vmem=None):
    return pltpu.CompilerParams(dimension_semantics=sem, vmem_limit_bytes=vmem or VMEM_LIMIT)


_ANY = pl.BlockSpec(memory_space=pl.ANY)
_MESH = pl.DeviceIdType.MESH


def _mesh_position():
    return lax.axis_index("x"), lax.axis_index("y"), lax.axis_index("c")


def _other_chips(x, y):
    return [(1 - x, y), (x, 1 - y), (1 - x, 1 - y)]


def _gather_weights(srcs):
    n = len(srcs)

    def body(*refs):
        src, out = refs[:n], refs[n:2 * n]
        send_sems, recv_sems, loc_sems = refs[2 * n:]
        x, y, c = _mesh_position()
        sibling = (x, y, 1 - c)
        chips = _other_chips(x, y)
        relay_from = (jnp.where(c == 0, 1 - x, x), jnp.where(c == 0, y, 1 - y))
        relay_to = (jnp.where(c == 0, x, 1 - x), jnp.where(c == 0, 1 - y, y))

        def copy(a, k, block, to, from_src=False):
            slot = out[a].at[4 * block[0] + 2 * block[1] + block[2]]
            return pltpu.make_async_remote_copy(
                src_ref=src[a] if from_src else slot, dst_ref=slot,
                send_sem=send_sems.at[7 * a + k], recv_sem=recv_sems.at[7 * a + k],
                device_id=to, device_id_type=_MESH)

        local, sends = [], []
        for a in range(n):
            mine = pltpu.make_async_copy(src[a], out[a].at[4 * x + 2 * y + c], loc_sems.at[a])
            mine.start()
            local.append(mine)
            first = [copy(a, 0, (x, y, c), sibling, True)]
            first += [copy(a, 1 + j, (x, y, c), (*chip, c), True) for j, chip in enumerate(chips[:2])]
            for cp in first:
                cp.start()
            sends += first
        for a in range(n):
            for j, chip in enumerate(chips[:2]):
                copy(a, 1 + j, (*chip, c), (x, y, c)).wait_recv()
            later = [copy(a, 3, (*relay_from, c), (*relay_to, c))]
            later += [copy(a, 4 + j, (*chip, c), sibling) for j, chip in enumerate(chips[:2])]
            for cp in later:
                cp.start()
            sends += later
        for a in range(n):
            copy(a, 3, (*chips[2], c), (x, y, c)).wait_recv()
            passed = copy(a, 6, (*chips[2], c), sibling)
            passed.start()
            sends.append(passed)
        for a in range(n):
            copy(a, 0, (x, y, 1 - c), (x, y, c)).wait_recv()
            for j, chip in enumerate(chips):
                copy(a, 4 + j, (*chip, 1 - c), (x, y, c)).wait_recv()
        for cp in sends:
            cp.wait_send()
        for cp in local:
            cp.wait()

    return pl.pallas_call(
        body, name="gather_weights",
        out_shape=tuple(jax.ShapeDtypeStruct((N_DEV,) + s.shape, s.dtype) for s in srcs),
        in_specs=[_ANY] * n, out_specs=tuple([_ANY] * n),
        scratch_shapes=[pltpu.SemaphoreType.DMA((7 * n,)), pltpu.SemaphoreType.DMA((7 * n,)),
                        pltpu.SemaphoreType.DMA((n,))],
    )(*srcs)


_ALL_GATHER_SEMS = [pltpu.SemaphoreType.DMA((7,)), pltpu.SemaphoreType.DMA((7,)), pltpu.SemaphoreType.DMA]


def _all_gather_copies(src, out, send_sems, recv_sems, loc_sem):
    x, y, c = _mesh_position()
    me = 4 * x + 2 * y + c
    copies = [pltpu.make_async_copy(src, out.at[me], loc_sem)]
    for k in range(1, N_DEV):
        peer = (1 - x if k & 4 else x, 1 - y if k & 2 else y, 1 - c if k & 1 else c)
        copies.append(pltpu.make_async_remote_copy(
            src_ref=src, dst_ref=out.at[me], send_sem=send_sems.at[k - 1], recv_sem=recv_sems.at[k - 1],
            device_id=peer, device_id_type=_MESH))
    return copies


def _swap_with_sibling(name, src, whole=False):
    n = 1 if whole else src.shape[0]

    def body(src_ref, out_ref, send_sems, recv_sems):
        x, y, c = _mesh_position()
        copies = []
        for k in range(n):
            copies.append(pltpu.make_async_remote_copy(
                src_ref=src_ref if whole else src_ref.at[k, 1 - c], dst_ref=out_ref if whole else out_ref.at[k],
                send_sem=send_sems.at[k], recv_sem=recv_sems.at[k], device_id=(x, y, 1 - c), device_id_type=_MESH))
        for cp in copies:
            cp.start()
        for cp in copies:
            cp.wait()

    shape = src.shape if whole else (src.shape[0],) + src.shape[2:]
    return pl.pallas_call(
        body, name=name, out_shape=jax.ShapeDtypeStruct(shape, src.dtype), in_specs=[_ANY], out_specs=_ANY,
        scratch_shapes=[pltpu.SemaphoreType.DMA((n,)), pltpu.SemaphoreType.DMA((n,))],
    )(src)


def _pair_sum(name, g, sib, rows):
    n, _, r_all, cols = g.shape
    assert r_all % rows == 0

    def body(g0_ref, g1_ref, s_ref, o_ref):
        c = lax.axis_index("c")
        mine = jnp.where(c == 0, g0_ref[0, 0].astype(F32), g1_ref[0, 0].astype(F32))
        o_ref[0] = (mine + s_ref[0].astype(F32)).astype(o_ref.dtype)

    return pl.pallas_call(
        body, name=name, out_shape=jax.ShapeDtypeStruct((n, r_all, cols), BF16), grid=(n, r_all // rows),
        in_specs=[pl.BlockSpec((1, 1, rows, cols), lambda k, i: (k, 0, i, 0)),
                  pl.BlockSpec((1, 1, rows, cols), lambda k, i: (k, 1, i, 0)),
                  pl.BlockSpec((1, rows, cols), lambda k, i: (k, i, 0))],
        out_specs=pl.BlockSpec((1, rows, cols), lambda k, i: (k, i, 0)),
        compiler_params=_params(("parallel", "parallel")),
    )(g, g, sib)


_CHIP_EXCHANGE_SEMS = [pltpu.SemaphoreType.DMA((3,)), pltpu.SemaphoreType.DMA((3,)), pltpu.SemaphoreType.DMA]


def _chip_exchange_copies(src, out, send_sems, recv_sems, loc_sem, whole=False, window=None):
    x, y, c = _mesh_position()
    here = 2 * x + y

    def slot(k):
        if whole:
            return src
        if window is not None:
            return src.at[:, pl.ds(pl.multiple_of(k * window[0], BLK), window[1])]
        return src.at[k]

    copies = [pltpu.make_async_copy(slot(here), out.at[here], loc_sem)]
    for j, chip in enumerate(_other_chips(x, y)):
        copies.append(pltpu.make_async_remote_copy(
            src_ref=slot(2 * chip[0] + chip[1]), dst_ref=out.at[here],
            send_sem=send_sems.at[j], recv_sem=recv_sems.at[j], device_id=(*chip, c), device_id_type=_MESH))
    return copies


def _exchange_between_chips(name, src, whole=False):
    def body(src_ref, out_ref, send_sems, recv_sems, loc_sem):
        copies = _chip_exchange_copies(src_ref, out_ref, send_sems, recv_sems, loc_sem, whole)
        for cp in copies:
            cp.start()
        for cp in copies:
            cp.wait()

    return pl.pallas_call(
        body, name=name, out_shape=jax.ShapeDtypeStruct(((4,) + src.shape) if whole else src.shape, src.dtype),
        in_specs=[_ANY], out_specs=_ANY, scratch_shapes=_CHIP_EXCHANGE_SEMS,
    )(src)


def _matmul(name, a, b, kind, tm, tn, tk, out_dtype=F32, n_cols=None):
    if kind == "nn":
        (m, kk), nn_ = a.shape, (n_cols or b.shape[1])
        a_spec = pl.BlockSpec((tm, tk), lambda i, j, k: (i, k))
        b_spec = pl.BlockSpec((tk, tn), lambda i, j, k: (k, j))
        dims = NN
    elif kind == "nt":
        (m, kk), (nn_, _) = a.shape, b.shape
        a_spec = pl.BlockSpec((tm, tk), lambda i, j, k: (i, k))
        b_spec = pl.BlockSpec((tn, tk), lambda i, j, k: (j, k))
        dims = NT
    else:
        (kk, m), (_, nn_) = a.shape, b.shape
        a_spec = pl.BlockSpec((tk, tm), lambda i, j, k: (k, i))
        b_spec = pl.BlockSpec((tk, tn), lambda i, j, k: (k, j))
        dims = TN
    assert m % tm == 0 and nn_ % tn == 0 and kk % tk == 0
    nk = kk // tk

    def body(a_ref, b_ref, o_ref, acc_ref):
        k = pl.program_id(2)
        part = _dot(a_ref[...], b_ref[...], dims)
        if nk == 1:
            o_ref[...] = part.astype(o_ref.dtype)
        else:
            @pl.when(k == 0)
            def _():
                acc_ref[...] = part

            @pl.when(k > 0)
            def _():
                acc_ref[...] += part

            @pl.when(k == nk - 1)
            def _():
                o_ref[...] = acc_ref[...].astype(o_ref.dtype)

    return pl.pallas_call(
        body, name=name, out_shape=jax.ShapeDtypeStruct((m, nn_), out_dtype),
        grid=(m // tm, nn_ // tn, nk),
        in_specs=[a_spec, b_spec], out_specs=pl.BlockSpec((tm, tn), lambda i, j, k: (i, j)),
        scratch_shapes=[pltpu.VMEM((tm, tn) if nk > 1 else (8, 128), F32)],
        compiler_params=_params(("parallel", "parallel", "arbitrary")),
    )(a, b)


def _row_spec(width, col=0):
    return pl.BlockSpec((BLK, width), lambda i: (i, col))


def _const_spec(shape):
    return pl.BlockSpec(shape, lambda i: tuple(0 for _ in shape))


def _prenorm(h_pad, norm_w, wdt, wdt_t):
    def body(h_ref, w_ref, wdt_ref, wdtt_ref, u_ref, dt_ref, dtt_ref):
        xv = h_ref[...]
        r = lax.rsqrt(jnp.mean(xv * xv, axis=-1, keepdims=True) + EPS)
        u = (xv * r * w_ref[...]).astype(BF16)
        u_ref[...] = u
        dt_ref[...] = _dot(u, wdt_ref[...], NN)
        dtt_ref[...] = _dot(wdtt_ref[...], u, NT)

    return pl.pallas_call(
        body, name="prenorm",
        out_shape=(jax.ShapeDtypeStruct((LP, D_MODEL), BF16), jax.ShapeDtypeStruct((LP, N_HEADS), F32),
                   jax.ShapeDtypeStruct((N_HEADS, LP), F32)),
        grid=(NBLK,),
        in_specs=[_row_spec(D_MODEL), _const_spec((1, D_MODEL)), _const_spec((D_MODEL, N_HEADS)),
                  _const_spec((N_HEADS, D_MODEL))],
        out_specs=(_row_spec(D_MODEL), _row_spec(N_HEADS), pl.BlockSpec((N_HEADS, BLK), lambda i: (0, i))),
        compiler_params=_params(("parallel",)),
    )(h_pad, norm_w, wdt, wdt_t)


def _gated_norm(o, g, w):
    a = o * (g * _sigmoid(g))
    r = lax.rsqrt(jnp.mean(a * a, axis=-1, keepdims=True) + EPS)
    return a * r * w


def _ycat(o_sb, proj, o_ssd, sb_w, ssd_w):
    def body(osb_ref, g_ref, ossd_ref, z_ref, sbw_ref, ssdw_ref, y_ref):
        y_ref[:, :SSD_W] = _gated_norm(osb_ref[...], g_ref[...], sbw_ref[...]).astype(BF16)
        y_ref[:, SSD_W:] = _gated_norm(ossd_ref[...], z_ref[...], ssdw_ref[...]).astype(BF16)

    return pl.pallas_call(
        body, name="ycat", out_shape=jax.ShapeDtypeStruct((LP, 2 * SSD_W), BF16), grid=(NBLK,),
        in_specs=[_row_spec(1024), _row_spec(1024, COL_G // 1024), _row_spec(1024),
                  _row_spec(1024, COL_Z // 1024), _const_spec((1, 1024)), _const_spec((1, 1024))],
        out_specs=_row_spec(2 * SSD_W),
        compiler_params=_params(("parallel",)),
    )(o_sb, proj, o_ssd, proj, sb_w, ssd_w)


def _loss_head(h_pad, yo, fnw, target):
    def body(h_ref, yo_ref, w_ref, t_ref, dh2_ref, dh2b_ref, loss_ref, dw_ref):
        i = pl.program_id(0)

        @pl.when(i == 0)
        def _():
            loss_ref[...] = jnp.zeros_like(loss_ref)
            dw_ref[...] = jnp.zeros_like(dw_ref)

        h2 = h_ref[...] + yo_ref[...]
        r = lax.rsqrt(jnp.mean(h2 * h2, axis=-1, keepdims=True) + EPS)
        nrm = h2 * r
        w = w_ref[...]
        live = i > 0
        err = jnp.where(live, nrm * w - t_ref[...], 0.0)
        dout = err * (1.0 / D_MODEL)
        loss_ref[...] += (0.5 / D_MODEL) * _fold_lanes(jnp.sum(err * err, axis=0, keepdims=True))
        dw_ref[...] += jnp.sum(dout * nrm, axis=0, keepdims=True)
        wd = dout * w
        dh2 = r * (wd - nrm * jnp.mean(wd * nrm, axis=-1, keepdims=True))
        dh2_ref[...] = dh2
        dh2b_ref[...] = dh2.astype(BF16)

    return pl.pallas_call(
        body, name="loss_head",
        out_shape=(jax.ShapeDtypeStruct((LP, D_MODEL), F32), jax.ShapeDtypeStruct((LP, D_MODEL), BF16),
                   jax.ShapeDtypeStruct((1, BLK), F32), jax.ShapeDtypeStruct((1, D_MODEL), F32)),
        grid=(NBLK,),
        in_specs=[_row_spec(D_MODEL), _row_spec(D_MODEL), _const_spec((1, D_MODEL)),
                  pl.BlockSpec((BLK, D_MODEL), lambda i: (jnp.maximum(i - 1, 0), 0))],
        out_specs=(_row_spec(D_MODEL), _row_spec(D_MODEL), _const_spec((1, BLK)), _const_spec((1, D_MODEL))),
        compiler_params=_params(("arbitrary",)),
    )(h_pad, yo, fnw, target)


def _fold_lanes(row):
    out = row[:, :BLK]
    for j in range(1, row.shape[1] // BLK):
        out = out + row[:, j * BLK:(j + 1) * BLK]
    return out


def _gated_norm_bwd(dy, o, g, w):
    s = _sigmoid(g)
    sg = g * s
    a = o * sg
    r = lax.rsqrt(jnp.mean(a * a, axis=-1, keepdims=True) + EPS)
    nrm = a * r
    dw = jnp.sum(dy * nrm, axis=0, keepdims=True)
    wd = dy * w
    da = r * (wd - nrm * jnp.mean(wd * nrm, axis=-1, keepdims=True))
    return da * sg, da * o * (s * (1.0 + g * (1.0 - s))), dw


def _ycat_bwd(dycat, o_sb, proj, o_ssd, sb_w, ssd_w):
    def body(dy_ref, osb_ref, g_ref, ossd_ref, z_ref, sbw_ref, ssdw_ref,
             dosb_ref, dg_ref, dossd_ref, dz_ref, dsbw_ref, dssdw_ref):
        @pl.when(pl.program_id(0) == 0)
        def _():
            dsbw_ref[...] = jnp.zeros_like(dsbw_ref)
            dssdw_ref[...] = jnp.zeros_like(dssdw_ref)

        do, dg, dw = _gated_norm_bwd(dy_ref[:, :SSD_W], osb_ref[...], g_ref[...], sbw_ref[...])
        dosb_ref[...] = do
        dg_ref[...] = dg.astype(BF16)
        dsbw_ref[...] += dw
        do, dg, dw = _gated_norm_bwd(dy_ref[:, SSD_W:], ossd_ref[...], z_ref[...], ssdw_ref[...])
        dossd_ref[...] = do
        dz_ref[...] = dg.astype(BF16)
        dssdw_ref[...] += dw

    act = jax.ShapeDtypeStruct((LP, 1024), F32)
    gate = jax.ShapeDtypeStruct((LP, 1024), BF16)
    vec = jax.ShapeDtypeStruct((1, 1024), F32)
    return pl.pallas_call(
        body, name="ycat_bwd", out_shape=(act, gate, act, gate, vec, vec), grid=(NBLK,),
        in_specs=[_row_spec(2048), _row_spec(1024), _row_spec(1024, COL_G // 1024), _row_spec(1024),
                  _row_spec(1024, COL_Z // 1024), _const_spec((1, 1024)), _const_spec((1, 1024))],
        out_specs=(_row_spec(1024), _row_spec(1024), _row_spec(1024), _row_spec(1024),
                   _const_spec((1, 1024)), _const_spec((1, 1024))),
        compiler_params=_params(("arbitrary",)),
    )(dycat, o_sb, proj, o_ssd, proj, sb_w, ssd_w)


_DPROJ_PIECES = (("dq", COL_Q, 1024), ("dk", COL_K, 1024), ("dv", COL_V, 1024), ("dg", COL_G, 1024),
                 ("dz", COL_Z, 1024), ("dxbc", COL_XBC, XBC_W), ("ddt", D_MAIN, BLK))
W_IN_PAD = D_MAIN + 512
WIN_STRIDE = 13 * BLK
WIN_WIDTH = 14 * BLK


def _d_w_in(u, pieces):
    tm, tn = 512, 512
    nj = W_IN_PAD // tn
    main = _DPROJ_PIECES[:-1]

    def body(*refs):
        u_ref, piece_refs, (ddt_ref, o_ref) = refs[0], refs[1:1 + len(main)], refs[1 + len(main):]
        j = pl.program_id(1)
        a = u_ref[...]
        for (_, col, width), ref in zip(main, piece_refs):
            @pl.when(jnp.logical_and(j >= col // tn, j < (col + width) // tn))
            def _():
                o_ref[...] = _dot(a, ref[...], TN).astype(BF16)

        @pl.when(j == nj - 1)
        def _():
            o_ref[...] = jnp.zeros_like(o_ref)
            o_ref[:, :BLK] = _dot(a, ddt_ref[...].astype(BF16), TN).astype(BF16)

    def piece_spec(col, width):
        return pl.BlockSpec((LP, tn), lambda i, j: (0, jnp.clip(j - col // tn, 0, width // tn - 1)))

    return pl.pallas_call(
        body, name="d_w_in", out_shape=jax.ShapeDtypeStruct((D_MODEL, W_IN_PAD), BF16), grid=(D_MODEL // tm, nj),
        in_specs=[pl.BlockSpec((LP, tm), lambda i, j: (0, i))] + [piece_spec(col, width) for _, col, width in main]
        + [pl.BlockSpec((LP, BLK), lambda i, j: (0, 0))],
        out_specs=pl.BlockSpec((tm, tn), lambda i, j: (i, j)),
        compiler_params=_params(("parallel", "arbitrary")),
    )(u, *pieces)


def _d_u_prenorm_bwd(pieces, w_main, wdt, h_pad, norm_w, dh2, chip_sum):
    tm, tk = LP // 4, 512
    nk = D_MAIN // tk
    main = _DPROJ_PIECES[:-1]
    window = (WIN_STRIDE, WIN_WIDTH)

    def body(*refs):
        piece_refs = refs[:len(main)]
        (b_ref, ddt_ref, wdt_ref, h_ref, w_ref, dh2_ref, src_ref, dh_ref, dw_ref, out_ref,
         acc_ref, send_sems, recv_sems, loc_sem) = refs[len(main):]
        i, k = pl.program_id(0), pl.program_id(1)

        @pl.when(jnp.logical_and(i == 0, k == 0))
        def _():
            for cp in _chip_exchange_copies(src_ref, out_ref, send_sems, recv_sems, loc_sem, window=window):
                cp.start()
            dw_ref[...] = jnp.zeros_like(dw_ref)

        @pl.when(k == 0)
        def _():
            acc_ref[...] = jnp.zeros_like(acc_ref)

        for (_, col, width), ref in zip(main, piece_refs):
            for lo in range(0, width, tk):
                @pl.when(k == (col + lo) // tk)
                def _():
                    acc_ref[...] += _dot(ref[:, lo:lo + tk], b_ref[...], NT)

        @pl.when(k == nk - 1)
        def _():
            dut = acc_ref[...] + _dot(ddt_ref[:, :N_HEADS].astype(BF16), wdt_ref[...], NT)
            xv = h_ref[...]
            r = lax.rsqrt(jnp.mean(xv * xv, axis=-1, keepdims=True) + EPS)
            nrm = xv * r
            dw_ref[...] += jnp.sum(dut * nrm, axis=0, keepdims=True)
            wd = dut * w_ref[...]
            dh_ref[...] = dh2_ref[...] + r * (wd - nrm * jnp.mean(wd * nrm, axis=-1, keepdims=True))

        @pl.when(jnp.logical_and(i == LP // tm - 1, k == nk - 1))
        def _():
            for cp in _chip_exchange_copies(src_ref, out_ref, send_sems, recv_sems, loc_sem, window=window):
                cp.wait()

    rows = lambda width: pl.BlockSpec((tm, width), lambda i, k: (i, 0))
    const = lambda shape: pl.BlockSpec(shape, lambda i, k: (0, 0))
    return pl.pallas_call(
        body, name="d_u_prenorm_bwd",
        out_shape=(jax.ShapeDtypeStruct((LP, D_MODEL), F32), jax.ShapeDtypeStruct((1, D_MODEL), F32),
                   jax.ShapeDtypeStruct((4, chip_sum.shape[0], WIN_WIDTH), chip_sum.dtype)),
        grid=(LP // tm, nk),
        in_specs=[rows(width) for _, _, width in main]
        + [pl.BlockSpec((D_MODEL, tk), lambda i, k: (0, k)), rows(BLK), const((D_MODEL, N_HEADS)), rows(D_MODEL),
           const((1, D_MODEL)), rows(D_MODEL), _ANY],
        out_specs=(rows(D_MODEL), const((1, D_MODEL)), _ANY),
        scratch_shapes=[pltpu.VMEM((tm, D_MODEL), F32)] + _CHIP_EXCHANGE_SEMS,
        compiler_params=_params(("arbitrary", "arbitrary")),
    )(*pieces[:-1], w_main, pieces[-1], wdt, h_pad, norm_w, dh2, chip_sum)


def _suffix_sum(vals, tri):
    return _dot_x_exact(vals, tri, NN, n=2)


def _sb_tile(z):
    t = jnp.exp(-jnp.abs(z))
    inv = 1.0 / (1.0 + t)
    sp = jnp.maximum(z, 0.0) + jnp.log(1.0 + t)
    sig = jnp.where(z >= 0, inv, t * inv)
    return sig, -sp, z - sp


def _sweep(first, step, init, run_slots):
    def alive_of(state):
        top = state[run_slots[0]]
        for s in run_slots[1:]:
            top = jnp.maximum(top, state[s])
        return (jnp.max(top) > SB_DEAD).astype(jnp.int32)

    def cond(carry):
        return jnp.logical_and(carry[0] >= 0, carry[1] > 0)

    def body(carry):
        state = step(carry[0], tuple(carry[2:]))
        return (carry[0] - 1, alive_of(state)) + tuple(state)

    return lax.while_loop(cond, body, (first, alive_of(init)) + tuple(init))[2:]


def _head_masks(x, lane):
    head0 = lane < HEAD_DIM
    return [jnp.where(head0, x, 0.0).astype(BF16), jnp.where(head0, 0.0, x).astype(BF16)]


PAIRS_PER_STEP = 4
STEP_W = PAIRS_PER_STEP * BLK
STEP_HEADS = 2 * PAIRS_PER_STEP


def _pair_lanes(h):
    lo = (h // 2) * BLK
    return slice(lo, lo + BLK)


def _stage_kv(qi, k_ref, v_ref, kb_ref, vb_ref):
    @pl.when(qi == 0)
    def _():
        kb_ref[:BLK, :] = jnp.zeros((BLK, STEP_W), BF16)
        vb_ref[:BLK, :] = jnp.zeros((BLK, STEP_W), BF16)
        kb_ref[BLK:, :] = k_ref[...].astype(BF16)
        vb_ref[BLK:, :] = v_ref[...].astype(BF16)


def _window_masks(qi):
    col2 = (qi - 1) * BLK + _iota((BLK, 2 * BLK), 1)
    row2 = qi * BLK + _iota((BLK, 2 * BLK), 0)
    valid2 = jnp.logical_and(col2 < row2, col2 >= PAD)
    row1 = qi * BLK + _iota((BLK, BLK), 0)
    lane = _iota((BLK, BLK), 1)

    def valid1(kblk):
        col = kblk * BLK + lane
        return jnp.logical_and(col < row1, col >= PAD)

    return valid2, valid1


def _strict_upper(n):
    return (_iota((n, n), 0) > _iota((n, n), 1)).astype(BF16)


def _sb_attention_fwd(proj, w_out_shard):
    heads = range(STEP_HEADS)
    n_groups = N_HEADS // STEP_HEADS

    def body(q_ref, k_ref, v_ref, src_ref, o_ref, ox_ref, out_ref, kb_ref, vb_ref, send_sems, recv_sems, loc_sem):
        grp, qi = pl.program_id(0), pl.program_id(1)
        _stage_kv(qi, k_ref, v_ref, kb_ref, vb_ref)

        @pl.when(jnp.logical_and(grp == 0, qi == 0))
        def _():
            for cp in _all_gather_copies(src_ref, out_ref, send_sems, recv_sems, loc_sem):
                cp.start()

        lane = _iota((BLK, BLK), 1)
        qh = []
        for p in range(PAIRS_PER_STEP):
            qh += _head_masks(q_ref[:, p * BLK:(p + 1) * BLK] * SB_SCALE, lane)
        valid2, valid1 = _window_masks(qi)

        def tiles(rows, valid, tri, runs):
            ks = [kb_ref[rows, _pair_lanes(h)] for h in heads]
            vs = [vb_ref[rows, _pair_lanes(h)] for h in heads]
            parts = [_sb_tile(jnp.where(valid, _dot(qh[h], ks[h], NT), SB_MASKED)) for h in heads]
            afters = [_suffix_sum(parts[h][1], tri) for h in heads]
            if runs is not None:
                afters = [afters[h] + runs[h] for h in heads]
            avals = [jnp.exp(parts[h][2] + afters[h]) for h in heads]
            his = [avals[h].astype(BF16) for h in heads]
            accs = [_dot(his[h], vs[h], NN) for h in heads]
            rests = [_dot((avals[h] - his[h].astype(F32)).astype(BF16), vs[h], NN) for h in heads]
            return [(jnp.sum(parts[h][1], axis=1, keepdims=True), accs[h], rests[h]) for h in heads]

        win = pl.ds(pl.multiple_of(qi * BLK, BLK), 2 * BLK)
        init = [t for head in tiles(win, valid2, _strict_upper(2 * BLK), None) for t in head]
        tri1 = _strict_upper(BLK)

        def step(kblk, carry):
            rows = pl.ds(pl.multiple_of((kblk + 1) * BLK, BLK), BLK)
            runs = [carry[3 * h] for h in heads]
            new = []
            for h, (d_run, d_acc, d_rest) in enumerate(tiles(rows, valid1(kblk), tri1, runs)):
                new += [carry[3 * h] + d_run, carry[3 * h + 1] + d_acc, carry[3 * h + 2] + d_rest]
            return tuple(new)

        res = _sweep(qi - 2, step, init, tuple(3 * h for h in heads))
        for p in range(PAIRS_PER_STEP):
            o = jnp.where(lane < HEAD_DIM, res[6 * p + 1], res[6 * p + 4])
            o_ref[:, p * BLK:(p + 1) * BLK] = o
            ox_ref[:, p * BLK:(p + 1) * BLK] = o + jnp.where(lane < HEAD_DIM, res[6 * p + 2], res[6 * p + 5])

        @pl.when(jnp.logical_and(grp == n_groups - 1, qi == NBLK - 1))
        def _():
            for cp in _all_gather_copies(src_ref, out_ref, send_sems, recv_sems, loc_sem):
                cp.wait()

    act = jax.ShapeDtypeStruct((LP, 1024), F32)
    blk = lambda col: pl.BlockSpec((BLK, STEP_W), lambda g, qi: (qi, col + g))
    whole = lambda col: pl.BlockSpec((LP, STEP_W), lambda g, qi: (0, col + g), pipeline_mode=pl.Buffered(1))
    return pl.pallas_call(
        body, name="sb_attn_fwd",
        out_shape=(act, act, jax.ShapeDtypeStruct((N_DEV,) + w_out_shard.shape, w_out_shard.dtype)),
        grid=(n_groups, NBLK),
        in_specs=[blk(COL_Q // STEP_W), whole(COL_K // STEP_W), whole(COL_V // STEP_W), _ANY],
        out_specs=(blk(0), blk(0), _ANY),
        scratch_shapes=[pltpu.VMEM((LP + BLK, STEP_W), BF16), pltpu.VMEM((LP + BLK, STEP_W), BF16)] + _ALL_GATHER_SEMS,
        compiler_params=_params(("arbitrary", "arbitrary")),
    )(proj, proj, proj, w_out_shard)


def _sb_attention_bwd(proj, o_sb, do_sb, chip_sums):
    heads = range(STEP_HEADS)
    n_groups = N_HEADS // STEP_HEADS

    def body(q_ref, k_ref, v_ref, o_ref, do_ref, src_ref, dq_ref, dk_ref, dv_ref, out_ref,
             kb_ref, vb_ref, dka_ref, dva_ref, send_sems, recv_sems, loc_sem):
        grp, qi = pl.program_id(0), pl.program_id(1)
        _stage_kv(qi, k_ref, v_ref, kb_ref, vb_ref)

        @pl.when(jnp.logical_and(grp == 0, qi == 0))
        def _():
            for cp in _chip_exchange_copies(src_ref, out_ref, send_sems, recv_sems, loc_sem):
                cp.start()

        @pl.when(qi == 0)
        def _():
            dka_ref[...] = jnp.zeros_like(dka_ref)
            dva_ref[...] = jnp.zeros_like(dva_ref)

        lane = _iota((BLK, BLK), 1)
        head0 = lane < HEAD_DIM
        qh, doh, dsum = [], [], []
        for p in range(PAIRS_PER_STEP):
            lanes = slice(p * BLK, (p + 1) * BLK)
            qh += _head_masks(q_ref[:, lanes] * SB_SCALE, lane)
            do = do_ref[:, lanes]
            doh += _head_masks(do, lane)
            prod = do.astype(BF16).astype(F32) * o_ref[:, lanes]
            dsum += [jnp.sum(jnp.where(head0, prod, 0.0), axis=1, keepdims=True),
                     jnp.sum(jnp.where(head0, 0.0, prod), axis=1, keepdims=True)]
        valid2, valid1 = _window_masks(qi)

        def tiles(rows, valid, tri, runs, eruns):
            ks = [kb_ref[rows, _pair_lanes(h)] for h in heads]
            vs = [vb_ref[rows, _pair_lanes(h)] for h in heads]
            parts = [_sb_tile(jnp.where(valid, _dot(qh[h], ks[h], NT), SB_MASKED)) for h in heads]
            afters = [_suffix_sum(parts[h][1], tri) for h in heads]
            if runs is not None:
                afters = [afters[h] + runs[h] for h in heads]
            avals = [jnp.exp(parts[h][2] + afters[h]) for h in heads]
            es = [avals[h] * _dot(doh[h], vs[h], NT) for h in heads]
            esufs = [_suffix_sum(es[h], tri) for h in heads]
            if eruns is not None:
                esufs = [esufs[h] + eruns[h] for h in heads]
            dzs = [(es[h] - parts[h][0] * (dsum[h] - esufs[h])).astype(BF16) for h in heads]
            dqs = [_dot(dzs[h], ks[h], NN) for h in heads]
            dks = [_dot(dzs[h], qh[h], TN) for h in heads]
            dvs = [_dot(avals[h].astype(BF16), doh[h], TN) for h in heads]
            for p in range(PAIRS_PER_STEP):
                dka_ref[rows, p * BLK:(p + 1) * BLK] += dks[2 * p] + dks[2 * p + 1]
                dva_ref[rows, p * BLK:(p + 1) * BLK] += dvs[2 * p] + dvs[2 * p + 1]
            return [(jnp.sum(parts[h][1], axis=1, keepdims=True), jnp.sum(es[h], axis=1, keepdims=True), dqs[h])
                    for h in heads]

        win = pl.ds(pl.multiple_of(qi * BLK, BLK), 2 * BLK)
        init = [t for head in tiles(win, valid2, _strict_upper(2 * BLK), None, None) for t in head]
        tri1 = _strict_upper(BLK)

        def step(kblk, carry):
            rows = pl.ds(pl.multiple_of((kblk + 1) * BLK, BLK), BLK)
            runs = [carry[3 * h] for h in heads]
            eruns = [carry[3 * h + 1] for h in heads]
            new = []
            for h, (d_run, d_erun, d_q) in enumerate(tiles(rows, valid1(kblk), tri1, runs, eruns)):
                new += [carry[3 * h] + d_run, carry[3 * h + 1] + d_erun, carry[3 * h + 2] + d_q]
            return tuple(new)

        res = _sweep(qi - 2, step, init, tuple(3 * h for h in heads))
        for p in range(PAIRS_PER_STEP):
            dq = jnp.where(head0, res[6 * p + 2], res[6 * p + 5]) * SB_SCALE
            dq_ref[:, p * BLK:(p + 1) * BLK] = dq.astype(BF16)

        @pl.when(qi == NBLK - 1)
        def _():
            dk_ref[...] = dka_ref[BLK:, :].astype(BF16)
            dv_ref[...] = dva_ref[BLK:, :].astype(BF16)

        @pl.when(jnp.logical_and(grp == n_groups - 1, qi == NBLK - 1))
        def _():
            for cp in _chip_exchange_copies(src_ref, out_ref, send_sems, recv_sems, loc_sem):
                cp.wait()

    act = jax.ShapeDtypeStruct((LP, 1024), BF16)
    blk = lambda col: pl.BlockSpec((BLK, STEP_W), lambda g, qi: (qi, col + g))
    whole = lambda col: pl.BlockSpec((LP, STEP_W), lambda g, qi: (0, col + g))
    once = lambda col: pl.BlockSpec((LP, STEP_W), lambda g, qi: (0, col + g), pipeline_mode=pl.Buffered(1))
    return pl.pallas_call(
        body, name="sb_attn_bwd",
        out_shape=(act, act, act, jax.ShapeDtypeStruct(chip_sums.shape, chip_sums.dtype)),
        grid=(n_groups, NBLK),
        in_specs=[blk(COL_Q // STEP_W), once(COL_K // STEP_W), once(COL_V // STEP_W), blk(0), blk(0), _ANY],
        out_specs=(blk(0), whole(0), whole(0), _ANY),
        scratch_shapes=[pltpu.VMEM((LP + BLK, STEP_W), BF16), pltpu.VMEM((LP + BLK, STEP_W), BF16),
                        pltpu.VMEM((LP + BLK, STEP_W), F32), pltpu.VMEM((LP + BLK, STEP_W), F32)]
        + _CHIP_EXCHANGE_SEMS,
        compiler_params=_params(("arbitrary", "arbitrary"), ATTN_VMEM_LIMIT),
    )(proj, proj, proj, o_sb, do_sb, chip_sums)


def _conv_pre(x, w_ref, b_ref):
    acc = b_ref[...] + w_ref[3:4, :] * x
    for k in range(3):
        acc = acc + w_ref[k:k + 1, :] * pltpu.roll(x, 3 - k, 0)
    return acc


def _conv_fwd(proj, conv_w, conv_b):
    def body(x_ref, w_ref, b_ref, o_ref):
        xc = _conv_pre(x_ref[...], w_ref, b_ref)
        o_ref[...] = xc * _sigmoid(xc)

    nb = XBC_W // BLK
    return pl.pallas_call(
        body, name="conv_fwd", out_shape=jax.ShapeDtypeStruct((LP, XBC_W), F32), grid=(nb,),
        in_specs=[pl.BlockSpec((LP, BLK), lambda j: (0, COL_XBC // BLK + j)),
                  pl.BlockSpec((4, BLK), lambda j: (0, j)), pl.BlockSpec((1, BLK), lambda j: (0, j))],
        out_specs=pl.BlockSpec((LP, BLK), lambda j: (0, j)),
        compiler_params=_params(("parallel",)),
    )(proj, conv_w, conv_b)


def _conv_bwd(dxa, proj, conv_w, conv_b):
    def body(d_ref, x_ref, w_ref, b_ref, dx_ref, dw_ref, db_ref):
        x = x_ref[...]
        xc = _conv_pre(x, w_ref, b_ref)
        s = _sigmoid(xc)
        live = _iota((LP, BLK), 0) >= PAD
        dxc = jnp.where(live, d_ref[...] * (s * (1.0 + xc * (1.0 - s))), 0.0)
        db_ref[...] = jnp.sum(dxc, axis=0, keepdims=True)
        dx = w_ref[3:4, :] * dxc
        dw_ref[3:4, :] = jnp.sum(dxc * x, axis=0, keepdims=True)
        for k in range(3):
            dw_ref[k:k + 1, :] = jnp.sum(dxc * pltpu.roll(x, 3 - k, 0), axis=0, keepdims=True)
            dx = dx + w_ref[k:k + 1, :] * pltpu.roll(dxc, LP - (3 - k), 0)
        dx_ref[...] = dx.astype(BF16)

    nb = XBC_W // BLK
    col = lambda j: (0, j)
    return pl.pallas_call(
        body, name="conv_bwd",
        out_shape=(jax.ShapeDtypeStruct((LP, XBC_W), BF16), jax.ShapeDtypeStruct((4, XBC_W), F32),
                   jax.ShapeDtypeStruct((1, XBC_W), F32)),
        grid=(nb,),
        in_specs=[pl.BlockSpec((LP, BLK), col), pl.BlockSpec((LP, BLK), lambda j: (0, COL_XBC // BLK + j)),
                  pl.BlockSpec((4, BLK), col), pl.BlockSpec((1, BLK), col)],
        out_specs=(pl.BlockSpec((LP, BLK), col), pl.BlockSpec((4, BLK), col), pl.BlockSpec((1, BLK), col)),
        compiler_params=_params(("parallel",)),
    )(dxa, proj, conv_w, conv_b)


def _ssd_prelude(c, dt_ref, dtt_ref, dtb_ref, dtbt_ref, alog_ref, alogt_ref):
    live = jnp.logical_or(c > 0, _iota((BLK, N_HEADS), 0) >= PAD)
    live_t = jnp.logical_or(c > 0, _iota((N_HEADS, BLK), 1) >= PAD)
    pre = dt_ref[...] + dtb_ref[...]
    pre_t = dtt_ref[...] + dtbt_ref[...]
    dt = jnp.where(live, _softplus(pre), 0.0)
    dt_t = jnp.where(live_t, _softplus(pre_t), 0.0)
    a = -jnp.exp(alog_ref[...])
    a_t = -jnp.exp(alogt_ref[...])
    li = _iota((BLK, BLK), 0)
    si = _iota((BLK, BLK), 1)
    lower = (si <= li).astype(BF16)
    upper = (li <= si).astype(BF16)
    acum = _dot_exact_x(lower, dt * a, NN)
    acum_t = _dot_x_exact(dt_t * a_t, upper, NN)
    return live, pre, dt, a, a_t, acum, acum_t


def _head_expand():
    return (_iota((N_HEADS, SSD_W), 1) // HEAD_DIM == _iota((N_HEADS, SSD_W), 0)).astype(BF16)


def _head_reduce_mat():
    return (_iota((SSD_W, N_HEADS), 0) // HEAD_DIM == _iota((SSD_W, N_HEADS), 1)).astype(BF16)


def _decay_mat(acum, acum_t, h, causal):
    seg = jnp.minimum(acum[:, h:h + 1] - acum_t[h:h + 1, :], 0.0)
    return jnp.where(causal, jnp.exp(seg), 0.0)


def _ssd_specs():
    chunk = lambda width, col=0: pl.BlockSpec((BLK, width), lambda c: (c, col))
    return chunk


def _ssd_fwd(xa, dt_raw, dt_raw_t, dt_bias, dt_bias_t, a_log, a_log_t, d_exp):
    def body(x_ref, b_ref, c_ref, dt_ref, dtt_ref, dtb_ref, dtbt_ref, alog_ref, alogt_ref, dexp_ref,
             y_ref, hs_ref, state_ref):
        c = pl.program_id(0)

        @pl.when(c == 0)
        def _():
            state_ref[...] = jnp.zeros_like(state_ref)

        _, _, dt, _, _, acum, acum_t = _ssd_prelude(c, dt_ref, dtt_ref, dtb_ref, dtbt_ref, alog_ref, alogt_ref)
        expand = _head_expand()
        x = x_ref[...]
        xdt = x * _dot_x_exact(dt, expand, n=2)
        exp_a = _dot_x_exact(jnp.exp(acum), expand, n=2)
        to_end = _dot_x_exact(jnp.exp(acum[BLK - 1:BLK, :] - acum), expand, n=2)
        xdt_b = xdt.astype(BF16)
        xd_b = (xdt * to_end).astype(BF16)
        chunk_decay = jnp.exp(acum_t[:, BLK - 1:BLK])
        hs_ref[0] = state_ref[...]
        lane = _iota((BLK, BLK), 1)
        causal = _iota((BLK, BLK), 0) >= lane
        gw = HEADS_PER_GROUP * HEAD_DIM
        for g in range(N_GROUPS):
            bg = b_ref[:, g * N_STATE:(g + 1) * N_STATE].astype(BF16)
            cg = c_ref[:, g * N_STATE:(g + 1) * N_STATE].astype(BF16)
            cb = _dot(cg, bg, NT)
            hg = state_ref[g * gw:(g + 1) * gw, :]
            ch = _dot(cg, hg.astype(BF16), NT)
            st = _dot(xd_b[:, g * gw:(g + 1) * gw], bg, TN)
            for p in range(HEADS_PER_GROUP // 2):
                h0 = g * HEADS_PER_GROUP + 2 * p
                lo = h0 * HEAD_DIM
                xp = xdt_b[:, lo:lo + BLK]
                w0 = (cb * _decay_mat(acum, acum_t, h0, causal)).astype(BF16)
                w1 = (cb * _decay_mat(acum, acum_t, h0 + 1, causal)).astype(BF16)
                yd = jnp.where(lane < HEAD_DIM, _dot(w0, xp), _dot(w1, xp))
                y_ref[:, lo:lo + BLK] = (yd + ch[:, lo - g * gw:lo - g * gw + BLK] * exp_a[:, lo:lo + BLK]
                                         + x[:, lo:lo + BLK] * dexp_ref[:, lo:lo + BLK])
            for r in range(HEADS_PER_GROUP):
                h = g * HEADS_PER_GROUP + r
                state_ref[h * HEAD_DIM:(h + 1) * HEAD_DIM, :] = (
                    hg[r * HEAD_DIM:(r + 1) * HEAD_DIM, :] * chunk_decay[h:h + 1, :]
                    + st[r * HEAD_DIM:(r + 1) * HEAD_DIM, :])

    chunk = _ssd_specs()
    return pl.pallas_call(
        body, name="ssd_fwd",
        out_shape=(jax.ShapeDtypeStruct((LP, SSD_W), F32), jax.ShapeDtypeStruct((NBLK, SSD_W, N_STATE), F32)),
        grid=(NBLK,),
        in_specs=[chunk(SSD_W), chunk(256, 4), chunk(256, 5), chunk(N_HEADS),
                  pl.BlockSpec((N_HEADS, BLK), lambda c: (0, c)), _const_spec((1, N_HEADS)),
                  _const_spec((N_HEADS, 1)), _const_spec((1, N_HEADS)), _const_spec((N_HEADS, 1)),
                  _const_spec((1, SSD_W))],
        out_specs=(chunk(SSD_W), pl.BlockSpec((1, SSD_W, N_STATE), lambda c: (c, 0, 0))),
        scratch_shapes=[pltpu.VMEM((SSD_W, N_STATE), F32)],
        compiler_params=_params(("arbitrary",)),
    )(xa, xa, xa, dt_raw, dt_raw_t, dt_bias, dt_bias_t, a_log, a_log_t, d_exp)


def _ssd_bwd(xa, dt_raw, dt_raw_t, dt_bias, dt_bias_t, a_log, a_log_t, d_exp, hstart, dy):
    def body(x_ref, b_ref, c_ref, dt_ref, dtt_ref, dtb_ref, dtbt_ref, alog_ref, alogt_ref, dexp_ref,
             hs_ref, dy_ref, dxa_ref, ddt_ref, dbias_ref, dalog_ref, dd_ref, dstate_ref):
        step = pl.program_id(0)
        c = NBLK - 1 - step

        @pl.when(step == 0)
        def _():
            dstate_ref[...] = jnp.zeros_like(dstate_ref)
            dbias_ref[...] = jnp.zeros_like(dbias_ref)
            dalog_ref[...] = jnp.zeros_like(dalog_ref)
            dd_ref[...] = jnp.zeros_like(dd_ref)

        live, pre, dt, a, a_t, acum, acum_t = _ssd_prelude(c, dt_ref, dtt_ref, dtb_ref, dtbt_ref,
                                                           alog_ref, alogt_ref)
        expand = _head_expand()
        reduce_m = _head_reduce_mat()
        x = x_ref[...]
        dyv = dy_ref[...]
        dt_e = _dot_x_exact(dt, expand, n=2)
        xdt = x * dt_e
        exp_acum = jnp.exp(acum)
        exp_a = _dot_x_exact(exp_acum, expand, n=2)
        dte = jnp.exp(acum[BLK - 1:BLK, :] - acum)
        to_end = _dot_x_exact(dte, expand, n=2)
        xdt_b = xdt.astype(BF16)
        xd_b = (xdt * to_end).astype(BF16)
        chunk_decay = jnp.exp(acum_t[:, BLK - 1:BLK])
        lane = _iota((BLK, BLK), 1)
        head0 = lane < HEAD_DIM
        causal = _iota((BLK, BLK), 0) >= lane
        gw = HEADS_PER_GROUP * HEAD_DIM
        dm = dyv * exp_a
        dm_b = dm.astype(BF16)
        onehot = lambda h: (_iota((1, N_HEADS), 1) == h).astype(F32)
        onehot_t = lambda h: (_iota((N_HEADS, 1), 0) == h).astype(F32)
        dacum = jnp.zeros((BLK, N_HEADS), F32)
        dacum_t = jnp.zeros((N_HEADS, BLK), F32)
        head_sums = lambda prod, rows: _dot_x_exact(prod, reduce_m[rows, :], n=2)
        dskip_acc = head_sums(dyv * x, slice(0, SSD_W))
        ddt_acc = jnp.zeros((BLK, N_HEADS), F32)
        ddte_acc = jnp.zeros((BLK, N_HEADS), F32)
        dexpa_acc = jnp.zeros((BLK, N_HEADS), F32)
        head_sum = expand
        for g in range(N_GROUPS):
            bg = b_ref[:, g * N_STATE:(g + 1) * N_STATE].astype(BF16)
            cg = c_ref[:, g * N_STATE:(g + 1) * N_STATE].astype(BF16)
            cb = _dot(cg, bg, NT)
            hg = hs_ref[0, g * gw:(g + 1) * gw, :]
            hg_b = hg.astype(BF16)
            dhe = dstate_ref[g * gw:(g + 1) * gw, :]
            dhe_b = dhe.astype(BF16)
            ch = _dot(cg, hg_b, NT)
            dcg = _dot(dm_b[:, g * gw:(g + 1) * gw], hg_b, NN)
            dhs = _dot(dm_b[:, g * gw:(g + 1) * gw], cg, TN)
            dxd = _dot(bg, dhe_b, NT)
            dbg = _dot(xd_b[:, g * gw:(g + 1) * gw], dhe_b, NN)
            dcb = jnp.zeros((BLK, BLK), F32)
            dxdt_g = []
            for p in range(HEADS_PER_GROUP // 2):
                h0 = g * HEADS_PER_GROUP + 2 * p
                lo = h0 * HEAD_DIM
                xp = xdt_b[:, lo:lo + BLK]
                dyp = dyv[:, lo:lo + BLK]
                dyh = (jnp.where(head0, dyp, 0.0).astype(BF16), jnp.where(head0, 0.0, dyp).astype(BF16))
                dxdt_p = jnp.zeros((BLK, BLK), F32)
                for q in range(2):
                    h = h0 + q
                    dec = _decay_mat(acum, acum_t, h, causal)
                    w = cb * dec
                    dw = _dot(dyh[q], xp, NT)
                    t = dw * w
                    dacum = dacum + jnp.sum(t, axis=1, keepdims=True) * onehot(h)
                    dacum_t = dacum_t - jnp.sum(t, axis=0, keepdims=True) * onehot_t(h)
                    dcb = dcb + dw * dec
                    dxdt_p = dxdt_p + _dot(w.astype(BF16), dyh[q], TN)
                sl = slice(lo, lo + BLK)
                gl = slice(lo - g * gw, lo - g * gw + BLK)
                dxdt_p = dxdt_p + dxd[:, gl] * to_end[:, sl]
                dxa_ref[:, sl] = dyp * dexp_ref[:, sl] + dxdt_p * dt_e[:, sl]
                dxdt_g.append(dxdt_p)
            cols = slice(g * gw, (g + 1) * gw)
            ddt_acc = ddt_acc + head_sums(jnp.concatenate(dxdt_g, axis=1) * x[:, cols], cols)
            ddte_acc = ddte_acc + head_sums(dxd * xdt[:, cols], cols)
            dexpa_acc = dexpa_acc + head_sums(dyv[:, cols] * ch, cols)
            dcb_b = dcb.astype(BF16)
            b_col = SSD_W + g * N_STATE
            c_col = SSD_W + (N_GROUPS + g) * N_STATE
            dxa_ref[:, c_col:c_col + N_STATE] = dcg + _dot(dcb_b, bg, NN)
            dxa_ref[:, b_col:b_col + N_STATE] = dbg + _dot(dcb_b, cg, TN)
            prod = dhe * hg
            per_head = jnp.sum(_dot_exact_x(head_sum[:, g * gw:(g + 1) * gw], prod, NN), axis=1, keepdims=True)
            dacum_t = dacum_t + (per_head * chunk_decay) * (_iota((1, BLK), 1) == BLK - 1).astype(F32)
            for r in range(HEADS_PER_GROUP):
                h = g * HEADS_PER_GROUP + r
                rows = slice(h * HEAD_DIM, (h + 1) * HEAD_DIM)
                dstate_ref[rows, :] = (dhs[r * HEAD_DIM:(r + 1) * HEAD_DIM, :]
                                       + dhe[r * HEAD_DIM:(r + 1) * HEAD_DIM, :] * chunk_decay[h:h + 1, :])
        dacum = dacum + dexpa_acc * exp_acum - ddte_acc * dte
        last_row = (_iota((BLK, 1), 0) == BLK - 1).astype(F32)
        dacum = dacum + last_row * jnp.sum(ddte_acc * dte, axis=0, keepdims=True)
        li = _iota((BLK, BLK), 0)
        si = _iota((BLK, BLK), 1)
        upper = (li <= si).astype(BF16)
        lower = (si <= li).astype(BF16)
        dda = _dot_exact_x(upper, dacum, NN)
        dda_t = _dot_x_exact(dacum_t, lower, NN)
        eye = (_iota((N_HEADS, N_HEADS), 0) == _iota((N_HEADS, N_HEADS), 1)).astype(BF16)
        dda = dda + _dot_x_exact_tn(dda_t, eye)
        ddt = ddt_acc + dda * a
        dalog_ref[...] += jnp.sum(dda * dt, axis=0, keepdims=True) * a
        dd_ref[...] += jnp.sum(dskip_acc, axis=0, keepdims=True)
        ddt_raw = jnp.where(live, ddt * _sigmoid(pre), 0.0)
        ddt_ref[...] = jnp.zeros_like(ddt_ref)
        ddt_ref[:, :N_HEADS] = ddt_raw
        dbias_ref[...] += jnp.sum(ddt_raw, axis=0, keepdims=True)

    rev = lambda width, col=0: pl.BlockSpec((BLK, width), lambda s: (NBLK - 1 - s, col))
    vec = jax.ShapeDtypeStruct((1, N_HEADS), F32)
    return pl.pallas_call(
        body, name="ssd_bwd",
        out_shape=(jax.ShapeDtypeStruct((LP, XBC_W), F32), jax.ShapeDtypeStruct((LP, BLK), F32), vec, vec, vec),
        grid=(NBLK,),
        in_specs=[rev(SSD_W), rev(256, 4), rev(256, 5), rev(N_HEADS),
                  pl.BlockSpec((N_HEADS, BLK), lambda s: (0, NBLK - 1 - s)), _const_spec((1, N_HEADS)),
                  _const_spec((N_HEADS, 1)), _const_spec((1, N_HEADS)), _const_spec((N_HEADS, 1)),
                  _const_spec((1, SSD_W)),
                  pl.BlockSpec((1, SSD_W, N_STATE), lambda s: (NBLK - 1 - s, 0, 0)), rev(SSD_W)],
        out_specs=(rev(XBC_W), rev(BLK), _const_spec((1, N_HEADS)),
                   _const_spec((1, N_HEADS)), _const_spec((1, N_HEADS))),
        scratch_shapes=[pltpu.VMEM((SSD_W, N_STATE), F32)],
        compiler_params=_params(("arbitrary",)),
    )(xa, xa, xa, dt_raw, dt_raw_t, dt_bias, dt_bias_t, a_log, a_log_t, d_exp, hstart, dy)


def _dot_x_exact_tn(x_t, eye):
    out = None
    for p in _split(x_t, 3):
        t = _dot(p, eye, TN)
        out = t if out is None else out + t
    return out


def _adamw(name, parts, w, m, v, rows):
    r_all, cols = w.shape
    assert r_all % rows == 0
    c1 = 1.0 / (1.0 - ADAM_B1 ** ADAM_STEP)
    c2 = 1.0 / (1.0 - ADAM_B2 ** ADAM_STEP)

    def body(p_ref, w_ref, m_ref, v_ref, g_ref, d_ref, mo_ref, vo_ref):
        g = p_ref[0].astype(F32)
        for j in range(1, parts.shape[0]):
            g = g + p_ref[j].astype(F32)
        mn = ADAM_B1 * m_ref[...] + (1.0 - ADAM_B1) * g
        vn = ADAM_B2 * v_ref[...] + (1.0 - ADAM_B2) * (g * g)
        g_ref[...] = g
        mo_ref[...] = mn
        vo_ref[...] = vn
        d_ref[...] = -ADAM_LR * ((mn * c1) / (jnp.sqrt(vn * c2) + ADAM_EPS) + ADAM_WD * w_ref[...])

    spec = pl.BlockSpec((rows, cols), lambda i: (i, 0))
    shp = jax.ShapeDtypeStruct((r_all, cols), F32)
    return pl.pallas_call(
        body, name=name, out_shape=(shp, shp, shp, shp), grid=(r_all // rows,),
        in_specs=[pl.BlockSpec((parts.shape[0], rows, cols), lambda i: (0, i, 0)), spec, spec, spec],
        out_specs=(spec, spec, spec, spec),
        compiler_params=_params(("parallel",)),
    )(parts, w, m, v)


_VECTORS = (("norm_w", 1024, 0), ("conv_b", 1536, 8), ("dt_bias", 16, 20), ("a_log", 16, 21), ("d_skip", 16, 22),
            ("sb_norm_w", 1024, 24), ("ssd_norm_w", 1024, 32), ("final_norm_w", 1024, 40))
_LOSS_ROW = 23
_CONVW_ROW = 48
_META_ROW = 96
_PACK_ROWS = 224
_SMALL_ORDER = tuple(name for name, _, _ in _VECTORS) + ("conv_w", "meta_tokens")


def _pack_small_grads(vectors, loss_row, d_convw, dh):
    def body(*refs):
        vec_refs, (loss_ref, cw_ref, dh_ref, out_ref) = refs[:len(_VECTORS)], refs[len(_VECTORS):]
        out_ref[...] = jnp.zeros_like(out_ref)
        for (_, width, row), ref in zip(_VECTORS, vec_refs):
            if width < BLK:
                out_ref[row:row + 1, :width] = ref[...]
            else:
                for t in range(width // BLK):
                    out_ref[row + t:row + t + 1, :] = ref[:, t * BLK:(t + 1) * BLK]
        out_ref[_LOSS_ROW:_LOSS_ROW + 1, :] = loss_ref[...]
        for k in range(4):
            for t in range(XBC_W // BLK):
                r = _CONVW_ROW + k * (XBC_W // BLK) + t
                out_ref[r:r + 1, :] = cw_ref[k:k + 1, t * BLK:(t + 1) * BLK]
        for i in range(N_META):
            for t in range(D_MODEL // BLK):
                r = _META_ROW + i * (D_MODEL // BLK) + t
                out_ref[r:r + 1, :] = dh_ref[i:i + 1, t * BLK:(t + 1) * BLK]

    full = lambda a: pl.BlockSpec(a.shape, lambda i: tuple(0 for _ in a.shape))
    return pl.pallas_call(
        body, name="pack_small_grads", out_shape=jax.ShapeDtypeStruct((_PACK_ROWS, BLK), F32), grid=(1,),
        in_specs=[full(v) for v in vectors] + [full(loss_row), full(d_convw),
                                               pl.BlockSpec((N_META, D_MODEL), lambda i: (PAD // N_META, 0))],
        out_specs=pl.BlockSpec((_PACK_ROWS, BLK), lambda i: (0, 0)),
        compiler_params=_params(("arbitrary",)),
    )(*vectors, loss_row, d_convw, dh)


def _sum_slots(name, parts, rows):
    n, r_all, cols = parts.shape

    def body(p_ref, o_ref):
        acc = p_ref[0].astype(F32)
        for j in range(1, n):
            acc = acc + p_ref[j].astype(F32)
        o_ref[...] = acc

    return pl.pallas_call(
        body, name=name, out_shape=jax.ShapeDtypeStruct((r_all, cols), F32), grid=(r_all // rows,),
        in_specs=[pl.BlockSpec((n, rows, cols), lambda i: (0, i, 0))],
        out_specs=pl.BlockSpec((rows, cols), lambda i: (i, 0)),
        compiler_params=_params(("parallel",)),
    )(parts)


def _add(name, a, b):
    def body(a_ref, b_ref, o_ref):
        o_ref[...] = a_ref[...] + b_ref[...]

    spec = pl.BlockSpec(a.shape, lambda i: (0, 0))
    return pl.pallas_call(body, name=name, out_shape=jax.ShapeDtypeStruct(a.shape, a.dtype), grid=(1,),
                          in_specs=[spec, spec], out_specs=spec, compiler_params=_params(("arbitrary",)))(a, b)


def _adamw_small(chip_sums, weights, moms, vels):
    c1 = 1.0 / (1.0 - ADAM_B1 ** ADAM_STEP)
    c2 = 1.0 / (1.0 - ADAM_B2 ** ADAM_STEP)
    n = len(_SMALL_ORDER)

    def body(*refs):
        p_ref = refs[0]
        w_refs, m_refs, v_refs = refs[1:1 + n], refs[1 + n:1 + 2 * n], refs[1 + 2 * n:1 + 3 * n]
        outs = refs[1 + 3 * n:1 + 7 * n]
        loss_ref, g_ref, cw_ref, cws_ref, mt_ref = refs[1 + 7 * n:]
        x, y, c = _mesh_position()
        me = 4 * x + 2 * y + c
        g = p_ref[0]
        for j in range(1, 4):
            g = g + p_ref[j]
        g_ref[...] = g
        loss_ref[...] = g_ref[_LOSS_ROW:_LOSS_ROW + 1, :]

        def update(idx, grad):
            go_ref, d_ref, mo_ref, vo_ref = outs[4 * idx:4 * idx + 4]
            mn = ADAM_B1 * m_refs[idx][...] + (1.0 - ADAM_B1) * grad
            vn = ADAM_B2 * v_refs[idx][...] + (1.0 - ADAM_B2) * (grad * grad)
            go_ref[...] = grad
            mo_ref[...] = mn
            vo_ref[...] = vn
            d_ref[...] = -ADAM_LR * ((mn * c1) / (jnp.sqrt(vn * c2) + ADAM_EPS) + ADAM_WD * w_refs[idx][...])

        for idx, (_, width, row) in enumerate(_VECTORS):
            go_ref = outs[4 * idx]
            if width < BLK:
                grad = g_ref[row:row + 1, :width]
            else:
                for t in range(width // BLK):
                    go_ref[:, t * BLK:(t + 1) * BLK] = g_ref[row + t:row + t + 1, :]
                grad = go_ref[...]
            update(idx, grad)
        cw_ref[...] = jnp.zeros_like(cw_ref)
        for k in range(4):
            for t in range(XBC_W // BLK):
                r = _CONVW_ROW + k * (XBC_W // BLK) + t
                cw_ref[k:k + 1, t * BLK:(t + 1) * BLK] = g_ref[r:r + 1, :]
        for i in range(N_META):
            for t in range(D_MODEL // BLK):
                r = _META_ROW + i * (D_MODEL // BLK) + t
                mt_ref[i:i + 1, t * BLK:(t + 1) * BLK] = g_ref[r:r + 1, :]
        width_cw = XBC_W // N_DEV
        pick_cw = (_iota((XBC_W, width_cw), 0) == me * width_cw + _iota((XBC_W, width_cw), 1)).astype(BF16)
        cws_ref[...] = _dot_x_exact(cw_ref[...], pick_cw)
        update(n - 2, cws_ref[0:4, :])
        pick_mt = (_iota((D_MODEL, BLK), 0) == me * BLK + _iota((D_MODEL, BLK), 1)).astype(BF16)
        update(n - 1, _dot_x_exact(mt_ref[...], pick_mt))

    full = lambda a: pl.BlockSpec(a.shape, lambda i: tuple(0 for _ in a.shape))
    params = list(weights) + list(moms) + list(vels)
    out_shape, out_specs = [], []
    for w in weights:
        for _ in range(4):
            out_shape.append(jax.ShapeDtypeStruct(w.shape, F32))
            out_specs.append(full(w))
    out_shape.append(jax.ShapeDtypeStruct((1, BLK), F32))
    out_specs.append(pl.BlockSpec((1, BLK), lambda i: (0, 0)))
    return pl.pallas_call(
        body, name="adamw_small", out_shape=tuple(out_shape), grid=(1,),
        in_specs=[full(chip_sums)] + [full(a) for a in params], out_specs=tuple(out_specs),
        scratch_shapes=[pltpu.VMEM((_PACK_ROWS, BLK), F32), pltpu.VMEM((8, XBC_W), F32),
                        pltpu.VMEM((8, XBC_W // N_DEV), F32), pltpu.VMEM((N_META, D_MODEL), F32)],
        compiler_params=_params(("arbitrary",)),
    )(chip_sums, *params)


def kernel(x, meta_tokens, norm_w, w_in, conv_w, conv_b, dt_bias, a_log, d_skip, sb_norm_w, ssd_norm_w, w_out, final_norm_w, loss_target, m_meta_tokens, m_norm_w, m_w_in, m_conv_w, m_conv_b, m_dt_bias, m_a_log, m_d_skip, m_sb_norm_w, m_ssd_norm_w, m_w_out, m_final_norm_w, v_meta_tokens, v_norm_w, v_w_in, v_conv_w, v_conv_b, v_dt_bias, v_a_log, v_d_skip, v_sb_norm_w, v_ssd_norm_w, v_w_out, v_final_norm_w):
    small_src = jnp.concatenate([conv_w[0].reshape(6, BLK), meta_tokens, jnp.zeros((2, BLK), F32)], axis=0)
    small_g, w_in_g = _gather_weights([small_src, w_in[0].astype(BF16)])
    w_in_full = w_in_g.transpose(1, 0, 2).reshape(D_MODEL, D_IN)
    w_dt = w_in_full[:, D_MAIN:]
    conv_w_full = small_g[:, :6].reshape(N_DEV, 4, 192).transpose(1, 0, 2).reshape(4, XBC_W)
    meta_full = small_g[:, 6:6 + N_META].transpose(1, 0, 2).reshape(N_META, D_MODEL)
    h_pad = jnp.concatenate([jnp.zeros((PAD, D_MODEL), F32), meta_full, x[0]], axis=0)
    dt_bias_t = dt_bias.reshape(N_HEADS, 1)
    a_log_t = a_log.reshape(N_HEADS, 1)
    d_exp = jnp.repeat(d_skip, HEAD_DIM, axis=1)
    fnw = final_norm_w.reshape(1, D_MODEL)

    u, dt_raw, dt_raw_t = _prenorm(h_pad, norm_w, w_dt, w_dt.T)
    proj = _matmul("in_proj", u, w_in_full, "nn", 1088, 512, D_MODEL, n_cols=D_MAIN)
    o_sb, o_sb_exact, w_out_g = _sb_attention_fwd(proj, w_out[0].astype(BF16))
    w_out_full = w_out_g.reshape(2 * SSD_W, D_MODEL)
    xa = _conv_fwd(proj, conv_w_full, conv_b)
    o_ssd, hstart = _ssd_fwd(xa, dt_raw, dt_raw_t, dt_bias, dt_bias_t, a_log, a_log_t, d_exp)
    ycat = _ycat(o_sb, proj, o_ssd, sb_norm_w, ssd_norm_w)
    yo = _matmul("out_proj", ycat, w_out_full, "nn", 1088, 512, 2 * SSD_W)
    dh2, dh2_b, loss_row, d_fnw = _loss_head(h_pad, yo, fnw, loss_target[0])

    g_w_out = _matmul("d_w_out", ycat, dh2_b, "tn", 512, 512, LP, BF16).reshape(4, 2, 256, D_MODEL)
    chip_w_out = _pair_sum("pair_sum_w_out", g_w_out, _swap_with_sibling("swap_w_out", g_w_out), 256)
    dycat = _matmul("d_ycat", dh2_b, w_out_full, "nt", 1088, 512, D_MODEL)
    do_sb, dg, do_ssd, dz, d_sbw, d_ssdw = _ycat_bwd(dycat, o_sb, proj, o_ssd, sb_norm_w, ssd_norm_w)
    dq, dk, dv, p_w_out = _sb_attention_bwd(proj, o_sb_exact, do_sb, chip_w_out)
    dxa, ddt_raw, d_dtb, d_alog, d_dskip = _ssd_bwd(
        xa, dt_raw, dt_raw_t, dt_bias, dt_bias_t, a_log, a_log_t, d_exp, hstart, do_ssd)
    dxbc, d_convw, d_convb = _conv_bwd(dxa, proj, conv_w_full, conv_b)
    pieces = [dq, dk, dv, dg, dz, dxbc, ddt_raw]
    g_w_in = _d_w_in(u, pieces).reshape(1, 2, D_MODEL // 2, W_IN_PAD)
    chip_w_in = _pair_sum("pair_sum_w_in", g_w_in, _swap_with_sibling("swap_w_in", g_w_in), 128)
    dh, d_nw, win_parts = _d_u_prenorm_bwd(pieces, w_in_full, w_dt, h_pad, norm_w, dh2, chip_w_in[0])
    win_mine = _sum_slots("sum_w_in_windows", win_parts, 128)
    win_other = _swap_with_sibling("swap_w_in_window", win_mine, whole=True)
    core = lax.axis_index("c")
    chip = 2 * lax.axis_index("x") + lax.axis_index("y")
    first_col = (D_IN // N_DEV) * (2 * chip + core) - WIN_STRIDE * chip
    cut = lambda w: lax.dynamic_slice(w, (0, first_col), (D_MODEL // 2, D_IN // N_DEV))
    half_mine, half_other = cut(win_mine), cut(win_other)
    p_w_in = jnp.concatenate([jnp.where(core == 0, half_mine, half_other),
                              jnp.where(core == 0, half_other, half_mine)], axis=0)[None]

    pack = _pack_small_grads([d_nw, d_convb, d_dtb, d_alog, d_dskip, d_sbw, d_ssdw, d_fnw], loss_row, d_convw, dh)
    chip_small = _add("pair_sum_small", pack, _swap_with_sibling("swap_small", pack, whole=True))
    p_small = _exchange_between_chips("exchange_small", chip_small, whole=True)

    res_in = _adamw("adamw_w_in", p_w_in, w_in[0], m_w_in[0], v_w_in[0], 128)
    res_out = _adamw("adamw_w_out", p_w_out, w_out[0], m_w_out[0], v_w_out[0], 128)
    res_small = _adamw_small(
        p_small,
        [norm_w, conv_b, dt_bias, a_log, d_skip, sb_norm_w, ssd_norm_w, fnw, conv_w[0], meta_tokens],
        [m_norm_w, m_conv_b, m_dt_bias, m_a_log, m_d_skip, m_sb_norm_w, m_ssd_norm_w,
         m_final_norm_w.reshape(1, D_MODEL), m_conv_w[0], m_meta_tokens],
        [v_norm_w, v_conv_b, v_dt_bias, v_a_log, v_d_skip, v_sb_norm_w, v_ssd_norm_w,
         v_final_norm_w.reshape(1, D_MODEL), v_conv_w[0], v_meta_tokens])

    loss = jnp.sum(res_small[-1])
    order = ["meta_tokens", "norm_w", "w_in", "conv_w", "conv_b", "dt_bias", "a_log", "d_skip",
             "sb_norm_w", "ssd_norm_w", "w_out", "final_norm_w"]
    outs = [loss, dh[BLK:].reshape(1, SEQ, D_MODEL)]
    for kind in range(4):
        small = {name: res_small[4 * idx + kind] for idx, name in enumerate(_SMALL_ORDER)}
        small["final_norm_w"] = small["final_norm_w"].reshape(D_MODEL)
        small["conv_w"] = small["conv_w"].reshape(1, 4, XBC_W // N_DEV)
        small["w_in"] = res_in[kind].reshape(1, D_MODEL, D_IN // N_DEV)
        small["w_out"] = res_out[kind].reshape(1, 256, D_MODEL)
        outs += [small[name] for name in order]
    return tuple(outs)
```

```python
import jax
import jax.numpy as jnp
from jax import lax
from jax.experimental import pallas as pl
from jax.experimental.pallas import tpu as pltpu

F32 = jnp.float32
BF16 = jnp.bfloat16

D_MODEL = 1024
SEQ = 2048
N_META = 16
BLK = 128
PAD = BLK - N_META
LP = PAD + N_META + SEQ
NBLK = LP // BLK
N_HEADS = 16
HEAD_DIM = 64
N_GROUPS = 2
HEADS_PER_GROUP = 8
N_STATE = 128
SSD_W = 1024
XBC_W = 1536
D_MAIN = 6656
D_IN = 6672
COL_Q, COL_K, COL_V, COL_G, COL_Z, COL_XBC = 0, 1024, 2048, 3072, 4096, 5120
N_DEV = 8
EPS = 1e-5
SB_SCALE = 0.125
SB_DEAD = -87.4
SB_MASKED = -1e30

ADAM_LR = 0.001
ADAM_B1 = 0.9
ADAM_B2 = 0.999
ADAM_EPS = 1e-08
ADAM_WD = 0.01
ADAM_STEP = 10

VMEM_LIMIT = 48 * 1024 * 1024
ATTN_VMEM_LIMIT = 56 * 1024 * 1024

NN = (((1,), (0,)), ((), ()))
NT = (((1,), (1,)), ((), ()))
TN = (((0,), (0,)), ((), ()))


def _dot(a, b, dims=NN):
    return lax.dot_general(a, b, dims, preferred_element_type=F32)


def _split(x, n):
    parts = []
    r = x
    for i in range(n):
        p = r.astype(BF16)
        parts.append(p)
        if i + 1 < n:
            r = r - p.astype(F32)
    return parts


def _dot_x_exact(x, m, dims=NN, n=3):
    out = None
    for p in _split(x, n):
        t = _dot(p, m, dims)
        out = t if out is None else out + t
    return out


def _dot_exact_x(m, x, dims=NN, n=3):
    out = None
    for p in _split(x, n):
        t = _dot(m, p, dims)
        out = t if out is None else out + t
    return out


def _iota(shape, dim):
    return lax.broadcasted_iota(jnp.int32, shape, dim)


def _softplus(x):
    return jnp.maximum(x, 0.0) + jnp.log(1.0 + jnp.exp(-jnp.abs(x)))


def _sigmoid(x):
    return 1.0 / (1.0 + jnp.exp(-x))


def _params(sem=None, vmem=None):
    return pltpu.CompilerParams(dimension_semantics=sem, vmem_limit_bytes=vmem or VMEM_LIMIT)


_ANY = pl.BlockSpec(memory_space=pl.ANY)
_MESH = pl.DeviceIdType.MESH


def _mesh_position():
    return lax.axis_index("x"), lax.axis_index("y"), lax.axis_index("c")


def _other_chips(x, y):
    return [(1 - x, y), (x, 1 - y), (1 - x, 1 - y)]


def _gather_weights(srcs):
    n = len(srcs)

    def body(*refs):
        src, out = refs[:n], refs[n:2 * n]
        send_sems, recv_sems, loc_sems = refs[2 * n:]
        x, y, c = _mesh_position()
        sibling = (x, y, 1 - c)
        chips = _other_chips(x, y)
        relay_from = (jnp.where(c == 0, 1 - x, x), jnp.where(c == 0, y, 1 - y))
        relay_to = (jnp.where(c == 0, x, 1 - x), jnp.where(c == 0, 1 - y, y))

        def copy(a, k, block, to, from_src=False):
            slot = out[a].at[4 * block[0] + 2 * block[1] + block[2]]
            return pltpu.make_async_remote_copy(
                src_ref=src[a] if from_src else slot, dst_ref=slot,
                send_sem=send_sems.at[7 * a + k], recv_sem=recv_sems.at[7 * a + k],
                device_id=to, device_id_type=_MESH)

        local, sends = [], []
        for a in range(n):
            mine = pltpu.make_async_copy(src[a], out[a].at[4 * x + 2 * y + c], loc_sems.at[a])
            mine.start()
            local.append(mine)
            first = [copy(a, 0, (x, y, c), sibling, True)]
            first += [copy(a, 1 + j, (x, y, c), (*chip, c), True) for j, chip in enumerate(chips[:2])]
            for cp in first:
                cp.start()
            sends += first
        for a in range(n):
            for j, chip in enumerate(chips[:2]):
                copy(a, 1 + j, (*chip, c), (x, y, c)).wait_recv()
            later = [copy(a, 3, (*relay_from, c), (*relay_to, c))]
            later += [copy(a, 4 + j, (*chip, c), sibling) for j, chip in enumerate(chips[:2])]
            for cp in later:
                cp.start()
            sends += later
        for a in range(n):
            copy(a, 3, (*chips[2], c), (x, y, c)).wait_recv()
            passed = copy(a, 6, (*chips[2], c), sibling)
            passed.start()
            sends.append(passed)
        for a in range(n):
            copy(a, 0, (x, y, 1 - c), (x, y, c)).wait_recv()
            for j, chip in enumerate(chips):
                copy(a, 4 + j, (*chip, 1 - c), (x, y, c)).wait_recv()
        for cp in sends:
            cp.wait_send()
        for cp in local:
            cp.wait()

    return pl.pallas_call(
        body, name="gather_weights",
        out_shape=tuple(jax.ShapeDtypeStruct((N_DEV,) + s.shape, s.dtype) for s in srcs),
        in_specs=[_ANY] * n, out_specs=tuple([_ANY] * n),
        scratch_shapes=[pltpu.SemaphoreType.DMA((7 * n,)), pltpu.SemaphoreType.DMA((7 * n,)),
                        pltpu.SemaphoreType.DMA((n,))],
    )(*srcs)


_ALL_GATHER_SEMS = [pltpu.SemaphoreType.DMA((7,)), pltpu.SemaphoreType.DMA((7,)), pltpu.SemaphoreType.DMA]


def _all_gather_copies(src, out, send_sems, recv_sems, loc_sem):
    x, y, c = _mesh_position()
    me = 4 * x + 2 * y + c
    copies = [pltpu.make_async_copy(src, out.at[me], loc_sem)]
    for k in range(1, N_DEV):
        peer = (1 - x if k & 4 else x, 1 - y if k & 2 else y, 1 - c if k & 1 else c)
        copies.append(pltpu.make_async_remote_copy(
            src_ref=src, dst_ref=out.at[me], send_sem=send_sems.at[k - 1], recv_sem=recv_sems.at[k - 1],
            device_id=peer, device_id_type=_MESH))
    return copies


def _sibling_swap_copies(src, out, send_sems, recv_sems, n, whole=False):
    x, y, c = _mesh_position()
    return [pltpu.make_async_remote_copy(
        src_ref=src if whole else src.at[k, 1 - c], dst_ref=out if whole else out.at[k],
        send_sem=send_sems.at[k], recv_sem=recv_sems.at[k], device_id=(x, y, 1 - c), device_id_type=_MESH)
        for k in range(n)]


def _swap_with_sibling(name, src, whole=False):
    n = 1 if whole else src.shape[0]

    def body(src_ref, out_ref, send_sems, recv_sems):
        copies = _sibling_swap_copies(src_ref, out_ref, send_sems, recv_sems, n, whole)
        for cp in copies:
            cp.start()
        for cp in copies:
            cp.wait()

    shape = src.shape if whole else (src.shape[0],) + src.shape[2:]
    return pl.pallas_call(
        body, name=name, out_shape=jax.ShapeDtypeStruct(shape, src.dtype), in_specs=[_ANY], out_specs=_ANY,
        scratch_shapes=[pltpu.SemaphoreType.DMA((n,)), pltpu.SemaphoreType.DMA((n,))],
    )(src)


def _pair_sum(name, g, sib, rows):
    n, _, r_all, cols = g.shape
    assert r_all % rows == 0

    def body(g0_ref, g1_ref, s_ref, o_ref):
        c = lax.axis_index("c")
        mine = jnp.where(c == 0, g0_ref[0, 0].astype(F32), g1_ref[0, 0].astype(F32))
        o_ref[0] = (mine + s_ref[0].astype(F32)).astype(o_ref.dtype)

    return pl.pallas_call(
        body, name=name, out_shape=jax.ShapeDtypeStruct((n, r_all, cols), BF16), grid=(n, r_all // rows),
        in_specs=[pl.BlockSpec((1, 1, rows, cols), lambda k, i: (k, 0, i, 0)),
                  pl.BlockSpec((1, 1, rows, cols), lambda k, i: (k, 1, i, 0)),
                  pl.BlockSpec((1, rows, cols), lambda k, i: (k, i, 0))],
        out_specs=pl.BlockSpec((1, rows, cols), lambda k, i: (k, i, 0)),
        compiler_params=_params(("parallel", "parallel")),
    )(g, g, sib)


_CHIP_EXCHANGE_SEMS = [pltpu.SemaphoreType.DMA((3,)), pltpu.SemaphoreType.DMA((3,)), pltpu.SemaphoreType.DMA]


def _chip_exchange_copies(src, out, send_sems, recv_sems, loc_sem, window=None):
    x, y, c = _mesh_position()
    here = 2 * x + y

    def slot(k):
        if window is not None:
            return src.at[:, pl.ds(pl.multiple_of(k * window[0], BLK), window[1])]
        return src.at[k]

    copies = [pltpu.make_async_copy(slot(here), out.at[here], loc_sem)]
    for j, chip in enumerate(_other_chips(x, y)):
        copies.append(pltpu.make_async_remote_copy(
            src_ref=slot(2 * chip[0] + chip[1]), dst_ref=out.at[here],
            send_sem=send_sems.at[j], recv_sem=recv_sems.at[j], device_id=(*chip, c), device_id_type=_MESH))
    return copies


def _matmul(name, a, b, kind, tm, tn, tk, out_dtype=F32, n_cols=None):
    if kind == "nn":
        (m, kk), nn_ = a.shape, (n_cols or b.shape[1])
        a_spec = pl.BlockSpec((tm, tk), lambda i, j, k: (i, k))
        b_spec = pl.BlockSpec((tk, tn), lambda i, j, k: (k, j))
        dims = NN
    else:
        (kk, m), (_, nn_) = a.shape, b.shape
        a_spec = pl.BlockSpec((tk, tm), lambda i, j, k: (k, i))
        b_spec = pl.BlockSpec((tk, tn), lambda i, j, k: (k, j))
        dims = TN
    assert m % tm == 0 and nn_ % tn == 0 and kk % tk == 0
    nk = kk // tk

    def body(a_ref, b_ref, o_ref, acc_ref):
        k = pl.program_id(2)
        part = _dot(a_ref[...], b_ref[...], dims)
        if nk == 1:
            o_ref[...] = part.astype(o_ref.dtype)
        else:
            @pl.when(k == 0)
            def _():
                acc_ref[...] = part

            @pl.when(k > 0)
            def _():
                acc_ref[...] += part

            @pl.when(k == nk - 1)
            def _():
                o_ref[...] = acc_ref[...].astype(o_ref.dtype)

    return pl.pallas_call(
        body, name=name, out_shape=jax.ShapeDtypeStruct((m, nn_), out_dtype),
        grid=(m // tm, nn_ // tn, nk),
        in_specs=[a_spec, b_spec], out_specs=pl.BlockSpec((tm, tn), lambda i, j, k: (i, j)),
        scratch_shapes=[pltpu.VMEM((tm, tn) if nk > 1 else (8, 128), F32)],
        compiler_params=_params(("parallel", "parallel", "arbitrary")),
    )(a, b)


def _row_spec(width, col=0):
    return pl.BlockSpec((BLK, width), lambda i: (i, col))


def _const_spec(shape):
    return pl.BlockSpec(shape, lambda i: tuple(0 for _ in shape))


def _prenorm(h_pad, norm_w, wdt, wdt_t):
    def body(h_ref, w_ref, wdt_ref, wdtt_ref, u_ref, ut_ref, dt_ref, dtt_ref):
        xv = h_ref[...]
        r = lax.rsqrt(jnp.mean(xv * xv, axis=-1, keepdims=True) + EPS)
        uf = xv * r * w_ref[...]
        u = uf.astype(BF16)
        u_ref[...] = u
        ut_ref[...] = uf.T.astype(BF16)
        dt_ref[...] = _dot(u, wdt_ref[...], NN)
        dtt_ref[...] = _dot(wdtt_ref[...], u, NT)

    return pl.pallas_call(
        body, name="prenorm",
        out_shape=(jax.ShapeDtypeStruct((LP, D_MODEL), BF16), jax.ShapeDtypeStruct((D_MODEL, LP), BF16),
                   jax.ShapeDtypeStruct((LP, N_HEADS), F32), jax.ShapeDtypeStruct((N_HEADS, LP), F32)),
        grid=(NBLK,),
        in_specs=[_row_spec(D_MODEL), _const_spec((1, D_MODEL)), _const_spec((D_MODEL, N_HEADS)),
                  _const_spec((N_HEADS, D_MODEL))],
        out_specs=(_row_spec(D_MODEL), pl.BlockSpec((D_MODEL, BLK), lambda i: (0, i)), _row_spec(N_HEADS),
                   pl.BlockSpec((N_HEADS, BLK), lambda i: (0, i))),
        compiler_params=_params(("parallel",)),
    )(h_pad, norm_w, wdt, wdt_t)


def _gated_norm(o, g, w):
    a = o * (g * _sigmoid(g))
    r = lax.rsqrt(jnp.mean(a * a, axis=-1, keepdims=True) + EPS)
    return a * r * w


def _ycat(o_sb, proj, o_ssd, sb_w, ssd_w):
    def body(osb_ref, g_ref, ossd_ref, z_ref, sbw_ref, ssdw_ref, y_ref):
        y_ref[:, :SSD_W] = _gated_norm(osb_ref[...], g_ref[...], sbw_ref[...]).astype(BF16)
        y_ref[:, SSD_W:] = _gated_norm(ossd_ref[...], z_ref[...], ssdw_ref[...]).astype(BF16)

    return pl.pallas_call(
        body, name="ycat", out_shape=jax.ShapeDtypeStruct((LP, 2 * SSD_W), BF16), grid=(NBLK,),
        in_specs=[_row_spec(1024), _row_spec(1024, COL_G // 1024), _row_spec(1024),
                  _row_spec(1024, COL_Z // 1024), _const_spec((1, 1024)), _const_spec((1, 1024))],
        out_specs=_row_spec(2 * SSD_W),
        compiler_params=_params(("parallel",)),
    )(o_sb, proj, o_ssd, proj, sb_w, ssd_w)


def _loss_head(h_pad, yo, fnw, target):
    def body(h_ref, yo_ref, w_ref, t_ref, dh2_ref, dh2b_ref, loss_ref, dw_ref):
        i = pl.program_id(0)

        @pl.when(i == 0)
        def _():
            loss_ref[...] = jnp.zeros_like(loss_ref)
            dw_ref[...] = jnp.zeros_like(dw_ref)

        h2 = h_ref[...] + yo_ref[...]
        r = lax.rsqrt(jnp.mean(h2 * h2, axis=-1, keepdims=True) + EPS)
        nrm = h2 * r
        w = w_ref[...]
        live = i > 0
        err = jnp.where(live, nrm * w - t_ref[...], 0.0)
        dout = err * (1.0 / D_MODEL)
        loss_ref[...] += (0.5 / D_MODEL) * _fold_lanes(jnp.sum(err * err, axis=0, keepdims=True))
        dw_ref[...] += jnp.sum(dout * nrm, axis=0, keepdims=True)
        wd = dout * w
        dh2 = r * (wd - nrm * jnp.mean(wd * nrm, axis=-1, keepdims=True))
        dh2_ref[...] = dh2
        dh2b_ref[...] = dh2.astype(BF16)

    return pl.pallas_call(
        body, name="loss_head",
        out_shape=(jax.ShapeDtypeStruct((LP, D_MODEL), F32), jax.ShapeDtypeStruct((LP, D_MODEL), BF16),
                   jax.ShapeDtypeStruct((1, BLK), F32), jax.ShapeDtypeStruct((1, D_MODEL), F32)),
        grid=(NBLK,),
        in_specs=[_row_spec(D_MODEL), _row_spec(D_MODEL), _const_spec((1, D_MODEL)),
                  pl.BlockSpec((BLK, D_MODEL), lambda i: (jnp.maximum(i - 1, 0), 0))],
        out_specs=(_row_spec(D_MODEL), _row_spec(D_MODEL), _const_spec((1, BLK)), _const_spec((1, D_MODEL))),
        compiler_params=_params(("arbitrary",)),
    )(h_pad, yo, fnw, target)


def _fold_lanes(row):
    out = row[:, :BLK]
    for j in range(1, row.shape[1] // BLK):
        out = out + row[:, j * BLK:(j + 1) * BLK]
    return out


def _gated_norm_bwd(dy, o, g, w):
    s = _sigmoid(g)
    sg = g * s
    a = o * sg
    r = lax.rsqrt(jnp.mean(a * a, axis=-1, keepdims=True) + EPS)
    nrm = a * r
    dw = jnp.sum(dy * nrm, axis=0, keepdims=True)
    wd = dy * w
    da = r * (wd - nrm * jnp.mean(wd * nrm, axis=-1, keepdims=True))
    return da * sg, da * o * (s * (1.0 + g * (1.0 - s))), dw


def _ycat_bwd(dh2_b, w_out_full, o_sb, proj, o_ssd, sb_w, ssd_w, g_w_out):
    tm = LP // 8
    n_slots = g_w_out.shape[0]

    def body(a_ref, w_ref, osb_ref, g_ref, ossd_ref, z_ref, sbw_ref, ssdw_ref, src_ref,
             dosb_ref, dg_ref, dossd_ref, dz_ref, dsbw_ref, dssdw_ref, sib_ref, send_sems, recv_sems):
        i = pl.program_id(0)

        @pl.when(i == 0)
        def _():
            for cp in _sibling_swap_copies(src_ref, sib_ref, send_sems, recv_sems, n_slots):
                cp.start()
            dsbw_ref[...] = jnp.zeros_like(dsbw_ref)
            dssdw_ref[...] = jnp.zeros_like(dssdw_ref)

        dy = _dot(a_ref[...], w_ref[...], NT)
        do, dg, dw = _gated_norm_bwd(dy[:, :SSD_W], osb_ref[...], g_ref[...], sbw_ref[...])
        dosb_ref[...] = do
        dg_ref[...] = dg.astype(BF16)
        dsbw_ref[...] += dw
        do, dg, dw = _gated_norm_bwd(dy[:, SSD_W:], ossd_ref[...], z_ref[...], ssdw_ref[...])
        dossd_ref[...] = do
        dz_ref[...] = dg.astype(BF16)
        dssdw_ref[...] += dw

        @pl.when(i == LP // tm - 1)
        def _():
            for cp in _sibling_swap_copies(src_ref, sib_ref, send_sems, recv_sems, n_slots):
                cp.wait()

    act = jax.ShapeDtypeStruct((LP, 1024), F32)
    gate = jax.ShapeDtypeStruct((LP, 1024), BF16)
    vec = jax.ShapeDtypeStruct((1, 1024), F32)
    rows = lambda col=0: pl.BlockSpec((tm, 1024), lambda i: (i, col))
    return pl.pallas_call(
        body, name="ycat_bwd",
        out_shape=(act, gate, act, gate, vec, vec,
                   jax.ShapeDtypeStruct((n_slots,) + g_w_out.shape[2:], g_w_out.dtype)),
        grid=(LP // tm,),
        in_specs=[rows(), _const_spec((2 * SSD_W, D_MODEL)), rows(), rows(COL_G // 1024), rows(),
                  rows(COL_Z // 1024), _const_spec((1, 1024)), _const_spec((1, 1024)), _ANY],
        out_specs=(rows(), rows(), rows(), rows(), _const_spec((1, 1024)), _const_spec((1, 1024)), _ANY),
        scratch_shapes=[pltpu.SemaphoreType.DMA((n_slots,)), pltpu.SemaphoreType.DMA((n_slots,))],
        compiler_params=_params(("arbitrary",)),
    )(dh2_b, w_out_full, o_sb, proj, o_ssd, proj, sb_w, ssd_w, g_w_out)


_DPROJ_PIECES = (("dq", COL_Q, 1024), ("dk", COL_K, 1024), ("dv", COL_V, 1024), ("dg", COL_G, 1024),
                 ("dz", COL_Z, 1024), ("dxbc", COL_XBC, XBC_W), ("ddt", D_MAIN, BLK))
W_IN_PAD = D_MAIN + 512
WIN_STRIDE = 13 * BLK
WIN_WIDTH = 14 * BLK


def _d_w_in(u_t, pieces):
    tn = 512
    nj = W_IN_PAD // tn
    main = _DPROJ_PIECES[:-1]

    def body(*refs):
        u_ref, piece_refs, (ddt_ref, o_ref) = refs[0], refs[1:1 + len(main)], refs[1 + len(main):]
        j = pl.program_id(1)
        a = u_ref[...]
        for (_, col, width), ref in zip(main, piece_refs):
            @pl.when(jnp.logical_and(j >= col // tn, j < (col + width) // tn))
            def _():
                o_ref[...] = _dot(a, ref[...], NN).astype(BF16)

        @pl.when(j == nj - 1)
        def _():
            o_ref[...] = jnp.zeros_like(o_ref)
            o_ref[:, :BLK] = _dot(a, ddt_ref[...].astype(BF16), NN).astype(BF16)

    def piece_spec(col, width):
        return pl.BlockSpec((LP, tn), lambda i, j: (0, jnp.clip(j - col // tn, 0, width // tn - 1)))

    return pl.pallas_call(
        body, name="d_w_in", out_shape=jax.ShapeDtypeStruct((D_MODEL, W_IN_PAD), BF16), grid=(1, nj),
        in_specs=[pl.BlockSpec((D_MODEL, LP), lambda i, j: (0, 0), pipeline_mode=pl.Buffered(1))]
        + [piece_spec(col, width) for _, col, width in main]
        + [pl.BlockSpec((LP, BLK), lambda i, j: (0, 0))],
        out_specs=pl.BlockSpec((D_MODEL, tn), lambda i, j: (0, j)),
        compiler_params=_params(("parallel", "arbitrary")),
    )(u_t, *pieces)


def _d_u_prenorm_bwd(pieces, w_main, wdt, h_pad, norm_w, dh2, chip_sum):
    tm, tk = LP // 4, 512
    nk = D_MAIN // tk
    main = _DPROJ_PIECES[:-1]
    window = (WIN_STRIDE, WIN_WIDTH)

    def body(*refs):
        piece_refs = refs[:len(main)]
        (b_ref, ddt_ref, wdt_ref, h_ref, w_ref, dh2_ref, src_ref, dh_ref, dw_ref, out_ref,
         acc_ref, send_sems, recv_sems, loc_sem) = refs[len(main):]
        i, k = pl.program_id(0), pl.program_id(1)

        @pl.when(jnp.logical_and(i == 0, k == 0))
        def _():
            for cp in _chip_exchange_copies(src_ref, out_ref, send_sems, recv_sems, loc_sem, window=window):
                cp.start()
            dw_ref[...] = jnp.zeros_like(dw_ref)

        @pl.when(k == 0)
        def _():
            acc_ref[...] = jnp.zeros_like(acc_ref)

        for (_, col, width), ref in zip(main, piece_refs):
            for lo in range(0, width, tk):
                @pl.when(k == (col + lo) // tk)
                def _():
                    acc_ref[...] += _dot(ref[:, lo:lo + tk], b_ref[...], NT)

        @pl.when(k == nk - 1)
        def _():
            dut = acc_ref[...] + _dot(ddt_ref[:, :N_HEADS].astype(BF16), wdt_ref[...], NT)
            xv = h_ref[...]
            r = lax.rsqrt(jnp.mean(xv * xv, axis=-1, keepdims=True) + EPS)
            nrm = xv * r
            dw_ref[...] += jnp.sum(dut * nrm, axis=0, keepdims=True)
            wd = dut * w_ref[...]
            dh_ref[...] = dh2_ref[...] + r * (wd - nrm * jnp.mean(wd * nrm, axis=-1, keepdims=True))

        @pl.when(jnp.logical_and(i == LP // tm - 1, k == nk - 1))
        def _():
            for cp in _chip_exchange_copies(src_ref, out_ref, send_sems, recv_sems, loc_sem, window=window):
                cp.wait()

    rows = lambda width: pl.BlockSpec((tm, width), lambda i, k: (i, 0))
    const = lambda shape: pl.BlockSpec(shape, lambda i, k: (0, 0))
    return pl.pallas_call(
        body, name="d_u_prenorm_bwd",
        out_shape=(jax.ShapeDtypeStruct((LP, D_MODEL), F32), jax.ShapeDtypeStruct((1, D_MODEL), F32),
                   jax.ShapeDtypeStruct((4, chip_sum.shape[0], WIN_WIDTH), chip_sum.dtype)),
        grid=(LP // tm, nk),
        in_specs=[rows(width) for _, _, width in main]
        + [pl.BlockSpec((D_MODEL, tk), lambda i, k: (0, k)), rows(BLK), const((D_MODEL, N_HEADS)), rows(D_MODEL),
           const((1, D_MODEL)), rows(D_MODEL), _ANY],
        out_specs=(rows(D_MODEL), const((1, D_MODEL)), _ANY),
        scratch_shapes=[pltpu.VMEM((tm, D_MODEL), F32)] + _CHIP_EXCHANGE_SEMS,
        compiler_params=_params(("arbitrary", "arbitrary")),
    )(*pieces[:-1], w_main, pieces[-1], wdt, h_pad, norm_w, dh2, chip_sum)


def _suffix_sum(vals, tri):
    return _dot_x_exact(vals, tri, NN, n=2)


def _sb_tile(z):
    t = jnp.exp(-jnp.abs(z))
    inv = 1.0 / (1.0 + t)
    sp = jnp.maximum(z, 0.0) + jnp.log(1.0 + t)
    sig = jnp.where(z >= 0, inv, t * inv)
    return sig, -sp, z - sp


def _sweep(first, step, init, run_slots):
    def alive_of(state):
        top = state[run_slots[0]]
        for s in run_slots[1:]:
            top = jnp.maximum(top, state[s])
        return (jnp.max(top) > SB_DEAD).astype(jnp.int32)

    def cond(carry):
        return jnp.logical_and(carry[0] >= 0, carry[1] > 0)

    def body(carry):
        state = step(carry[0], tuple(carry[2:]))
        return (carry[0] - 1, alive_of(state)) + tuple(state)

    return lax.while_loop(cond, body, (first, alive_of(init)) + tuple(init))[2:]


def _head_masks(x, lane):
    head0 = lane < HEAD_DIM
    return [jnp.where(head0, x, 0.0).astype(BF16), jnp.where(head0, 0.0, x).astype(BF16)]


PAIRS_PER_STEP = 4
STEP_W = PAIRS_PER_STEP * BLK
STEP_HEADS = 2 * PAIRS_PER_STEP


def _pair_lanes(h):
    lo = (h // 2) * BLK
    return slice(lo, lo + BLK)


def _stage_kv(qi, k_ref, v_ref, kb_ref, vb_ref):
    @pl.when(qi == 0)
    def _():
        kb_ref[:BLK, :] = jnp.zeros((BLK, STEP_W), BF16)
        vb_ref[:BLK, :] = jnp.zeros((BLK, STEP_W), BF16)
        kb_ref[BLK:, :] = k_ref[...].astype(BF16)
        vb_ref[BLK:, :] = v_ref[...].astype(BF16)


def _window_masks(qi):
    col2 = (qi - 1) * BLK + _iota((BLK, 2 * BLK), 1)
    row2 = qi * BLK + _iota((BLK, 2 * BLK), 0)
    valid2 = jnp.logical_and(col2 < row2, col2 >= PAD)
    row1 = qi * BLK + _iota((BLK, BLK), 0)
    lane = _iota((BLK, BLK), 1)

    def valid1(kblk):
        col = kblk * BLK + lane
        return jnp.logical_and(col < row1, col >= PAD)

    return valid2, valid1


def _strict_upper(n):
    return (_iota((n, n), 0) > _iota((n, n), 1)).astype(BF16)


def _sb_attention_fwd(proj, w_out_shard):
    heads = range(STEP_HEADS)
    n_groups = N_HEADS // STEP_HEADS

    def body(q_ref, k_ref, v_ref, src_ref, o_ref, ox_ref, out_ref, kb_ref, vb_ref, send_sems, recv_sems, loc_sem):
        grp, qi = pl.program_id(0), pl.program_id(1)
        _stage_kv(qi, k_ref, v_ref, kb_ref, vb_ref)

        @pl.when(jnp.logical_and(grp == 0, qi == 0))
        def _():
            for cp in _all_gather_copies(src_ref, out_ref, send_sems, recv_sems, loc_sem):
                cp.start()

        lane = _iota((BLK, BLK), 1)
        qh = []
        for p in range(PAIRS_PER_STEP):
            qh += _head_masks(q_ref[:, p * BLK:(p + 1) * BLK] * SB_SCALE, lane)
        valid2, valid1 = _window_masks(qi)

        def tiles(rows, valid, tri, runs):
            ks = [kb_ref[rows, _pair_lanes(h)] for h in heads]
            vs = [vb_ref[rows, _pair_lanes(h)] for h in heads]
            parts = [_sb_tile(jnp.where(valid, _dot(qh[h], ks[h], NT), SB_MASKED)) for h in heads]
            afters = [_suffix_sum(parts[h][1], tri) for h in heads]
            if runs is not None:
                afters = [afters[h] + runs[h] for h in heads]
            avals = [jnp.exp(parts[h][2] + afters[h]) for h in heads]
            his = [avals[h].astype(BF16) for h in heads]
            accs = [_dot(his[h], vs[h], NN) for h in heads]
            rests = [_dot((avals[h] - his[h].astype(F32)).astype(BF16), vs[h], NN) for h in heads]
            return [(jnp.sum(parts[h][1], axis=1, keepdims=True), accs[h], rests[h]) for h in heads]

        win = pl.ds(pl.multiple_of(qi * BLK, BLK), 2 * BLK)
        init = [t for head in tiles(win, valid2, _strict_upper(2 * BLK), None) for t in head]
        tri1 = _strict_upper(BLK)

        def step(kblk, carry):
            rows = pl.ds(pl.multiple_of((kblk + 1) * BLK, BLK), BLK)
            runs = [carry[3 * h] for h in heads]
            new = []
            for h, (d_run, d_acc, d_rest) in enumerate(tiles(rows, valid1(kblk), tri1, runs)):
                new += [carry[3 * h] + d_run, carry[3 * h + 1] + d_acc, carry[3 * h + 2] + d_rest]
            return tuple(new)

        res = _sweep(qi - 2, step, init, tuple(3 * h for h in heads))
        for p in range(PAIRS_PER_STEP):
            o = jnp.where(lane < HEAD_DIM, res[6 * p + 1], res[6 * p + 4])
            o_ref[:, p * BLK:(p + 1) * BLK] = o
            ox_ref[:, p * BLK:(p + 1) * BLK] = o + jnp.where(lane < HEAD_DIM, res[6 * p + 2], res[6 * p + 5])

        @pl.when(jnp.logical_and(grp == n_groups - 1, qi == NBLK - 1))
        def _():
            for cp in _all_gather_copies(src_ref, out_ref, send_sems, recv_sems, loc_sem):
                cp.wait()

    act = jax.ShapeDtypeStruct((LP, 1024), F32)
    blk = lambda col: pl.BlockSpec((BLK, STEP_W), lambda g, qi: (qi, col + g))
    whole = lambda col: pl.BlockSpec((LP, STEP_W), lambda g, qi: (0, col + g), pipeline_mode=pl.Buffered(1))
    return pl.pallas_call(
        body, name="sb_attn_fwd",
        out_shape=(act, act, jax.ShapeDtypeStruct((N_DEV,) + w_out_shard.shape, w_out_shard.dtype)),
        grid=(n_groups, NBLK),
        in_specs=[blk(COL_Q // STEP_W), whole(COL_K // STEP_W), whole(COL_V // STEP_W), _ANY],
        out_specs=(blk(0), blk(0), _ANY),
        scratch_shapes=[pltpu.VMEM((LP + BLK, STEP_W), BF16), pltpu.VMEM((LP + BLK, STEP_W), BF16)] + _ALL_GATHER_SEMS,
        compiler_params=_params(("arbitrary", "arbitrary")),
    )(proj, proj, proj, w_out_shard)


def _sb_attention_bwd(proj, o_sb, do_sb, chip_sums):
    heads = range(STEP_HEADS)
    n_groups = N_HEADS // STEP_HEADS

    def body(q_ref, k_ref, v_ref, o_ref, do_ref, src_ref, dq_ref, dk_ref, dv_ref, out_ref,
             kb_ref, vb_ref, dka_ref, dva_ref, send_sems, recv_sems, loc_sem):
        grp, qi = pl.program_id(0), pl.program_id(1)
        _stage_kv(qi, k_ref, v_ref, kb_ref, vb_ref)

        @pl.when(jnp.logical_and(grp == 0, qi == 0))
        def _():
            for cp in _chip_exchange_copies(src_ref, out_ref, send_sems, recv_sems, loc_sem):
                cp.start()

        @pl.when(qi == 0)
        def _():
            dka_ref[...] = jnp.zeros_like(dka_ref)
            dva_ref[...] = jnp.zeros_like(dva_ref)

        lane = _iota((BLK, BLK), 1)
        head0 = lane < HEAD_DIM
        qh, doh, dsum = [], [], []
        for p in range(PAIRS_PER_STEP):
            lanes = slice(p * BLK, (p + 1) * BLK)
            qh += _head_masks(q_ref[:, lanes] * SB_SCALE, lane)
            do = do_ref[:, lanes]
            doh += _head_masks(do, lane)
            prod = do.astype(BF16).astype(F32) * o_ref[:, lanes]
            dsum += [jnp.sum(jnp.where(head0, prod, 0.0), axis=1, keepdims=True),
                     jnp.sum(jnp.where(head0, 0.0, prod), axis=1, keepdims=True)]
        valid2, valid1 = _window_masks(qi)

        def tiles(rows, valid, tri, runs, eruns):
            ks = [kb_ref[rows, _pair_lanes(h)] for h in heads]
            vs = [vb_ref[rows, _pair_lanes(h)] for h in heads]
            parts = [_sb_tile(jnp.where(valid, _dot(qh[h], ks[h], NT), SB_MASKED)) for h in heads]
            afters = [_suffix_sum(parts[h][1], tri) for h in heads]
            if runs is not None:
                afters = [afters[h] + runs[h] for h in heads]
            avals = [jnp.exp(parts[h][2] + afters[h]) for h in heads]
            es = [avals[h] * _dot(doh[h], vs[h], NT) for h in heads]
            esufs = [_suffix_sum(es[h], tri) for h in heads]
            if eruns is not None:
                esufs = [esufs[h] + eruns[h] for h in heads]
            dzs = [(es[h] - parts[h][0] * (dsum[h] - esufs[h])).astype(BF16) for h in heads]
            dqs = [_dot(dzs[h], ks[h], NN) for h in heads]
            dks = [_dot(dzs[h], qh[h], TN) for h in heads]
            dvs = [_dot(avals[h].astype(BF16), doh[h], TN) for h in heads]
            for p in range(PAIRS_PER_STEP):
                dka_ref[rows, p * BLK:(p + 1) * BLK] += dks[2 * p] + dks[2 * p + 1]
                dva_ref[rows, p * BLK:(p + 1) * BLK] += dvs[2 * p] + dvs[2 * p + 1]
            return [(jnp.sum(parts[h][1], axis=1, keepdims=True), jnp.sum(es[h], axis=1, keepdims=True), dqs[h])
                    for h in heads]

        win = pl.ds(pl.multiple_of(qi * BLK, BLK), 2 * BLK)
        init = [t for head in tiles(win, valid2, _strict_upper(2 * BLK), None, None) for t in head]
        tri1 = _strict_upper(BLK)

        def step(kblk, carry):
            rows = pl.ds(pl.multiple_of((kblk + 1) * BLK, BLK), BLK)
            runs = [carry[3 * h] for h in heads]
            eruns = [carry[3 * h + 1] for h in heads]
            new = []
            for h, (d_run, d_erun, d_q) in enumerate(tiles(rows, valid1(kblk), tri1, runs, eruns)):
                new += [carry[3 * h] + d_run, carry[3 * h + 1] + d_erun, carry[3 * h + 2] + d_q]
            return tuple(new)

        res = _sweep(qi - 2, step, init, tuple(3 * h for h in heads))
        for p in range(PAIRS_PER_STEP):
            dq = jnp.where(head0, res[6 * p + 2], res[6 * p + 5]) * SB_SCALE
            dq_ref[:, p * BLK:(p + 1) * BLK] = dq.astype(BF16)

        @pl.when(qi == NBLK - 1)
        def _():
            dk_ref[...] = dka_ref[BLK:, :].astype(BF16)
            dv_ref[...] = dva_ref[BLK:, :].astype(BF16)

        @pl.when(jnp.logical_and(grp == n_groups - 1, qi == NBLK - 1))
        def _():
            for cp in _chip_exchange_copies(src_ref, out_ref, send_sems, recv_sems, loc_sem):
                cp.wait()

    act = jax.ShapeDtypeStruct((LP, 1024), BF16)
    blk = lambda col: pl.BlockSpec((BLK, STEP_W), lambda g, qi: (qi, col + g))
    whole = lambda col: pl.BlockSpec((LP, STEP_W), lambda g, qi: (0, col + g))
    once = lambda col: pl.BlockSpec((LP, STEP_W), lambda g, qi: (0, col + g), pipeline_mode=pl.Buffered(1))
    return pl.pallas_call(
        body, name="sb_attn_bwd",
        out_shape=(act, act, act, jax.ShapeDtypeStruct(chip_sums.shape, chip_sums.dtype)),
        grid=(n_groups, NBLK),
        in_specs=[blk(COL_Q // STEP_W), once(COL_K // STEP_W), once(COL_V // STEP_W), blk(0), blk(0), _ANY],
        out_specs=(blk(0), whole(0), whole(0), _ANY),
        scratch_shapes=[pltpu.VMEM((LP + BLK, STEP_W), BF16), pltpu.VMEM((LP + BLK, STEP_W), BF16),
                        pltpu.VMEM((LP + BLK, STEP_W), F32), pltpu.VMEM((LP + BLK, STEP_W), F32)]
        + _CHIP_EXCHANGE_SEMS,
        compiler_params=_params(("arbitrary", "arbitrary"), ATTN_VMEM_LIMIT),
    )(proj, proj, proj, o_sb, do_sb, chip_sums)


def _conv_pre(x, w_ref, b_ref):
    acc = b_ref[...] + w_ref[3:4, :] * x
    for k in range(3):
        acc = acc + w_ref[k:k + 1, :] * pltpu.roll(x, 3 - k, 0)
    return acc


def _conv_fwd(proj, conv_w, conv_b):
    def body(x_ref, w_ref, b_ref, o_ref):
        xc = _conv_pre(x_ref[...], w_ref, b_ref)
        o_ref[...] = xc * _sigmoid(xc)

    nb = XBC_W // BLK
    return pl.pallas_call(
        body, name="conv_fwd", out_shape=jax.ShapeDtypeStruct((LP, XBC_W), F32), grid=(nb,),
        in_specs=[pl.BlockSpec((LP, BLK), lambda j: (0, COL_XBC // BLK + j)),
                  pl.BlockSpec((4, BLK), lambda j: (0, j)), pl.BlockSpec((1, BLK), lambda j: (0, j))],
        out_specs=pl.BlockSpec((LP, BLK), lambda j: (0, j)),
        compiler_params=_params(("parallel",)),
    )(proj, conv_w, conv_b)


def _conv_bwd(dxa, proj, conv_w, conv_b):
    def body(d_ref, x_ref, w_ref, b_ref, dx_ref, dw_ref, db_ref):
        x = x_ref[...]
        xc = _conv_pre(x, w_ref, b_ref)
        s = _sigmoid(xc)
        live = _iota((LP, BLK), 0) >= PAD
        dxc = jnp.where(live, d_ref[...] * (s * (1.0 + xc * (1.0 - s))), 0.0)
        db_ref[...] = jnp.sum(dxc, axis=0, keepdims=True)
        dx = w_ref[3:4, :] * dxc
        dw_ref[3:4, :] = jnp.sum(dxc * x, axis=0, keepdims=True)
        for k in range(3):
            dw_ref[k:k + 1, :] = jnp.sum(dxc * pltpu.roll(x, 3 - k, 0), axis=0, keepdims=True)
            dx = dx + w_ref[k:k + 1, :] * pltpu.roll(dxc, LP - (3 - k), 0)
        dx_ref[...] = dx.astype(BF16)

    nb = XBC_W // BLK
    col = lambda j: (0, j)
    return pl.pallas_call(
        body, name="conv_bwd",
        out_shape=(jax.ShapeDtypeStruct((LP, XBC_W), BF16), jax.ShapeDtypeStruct((4, XBC_W), F32),
                   jax.ShapeDtypeStruct((1, XBC_W), F32)),
        grid=(nb,),
        in_specs=[pl.BlockSpec((LP, BLK), col), pl.BlockSpec((LP, BLK), lambda j: (0, COL_XBC // BLK + j)),
                  pl.BlockSpec((4, BLK), col), pl.BlockSpec((1, BLK), col)],
        out_specs=(pl.BlockSpec((LP, BLK), col), pl.BlockSpec((4, BLK), col), pl.BlockSpec((1, BLK), col)),
        compiler_params=_params(("parallel",)),
    )(dxa, proj, conv_w, conv_b)


def _ssd_prelude(c, dt_ref, dtt_ref, dtb_ref, dtbt_ref, alog_ref, alogt_ref):
    live = jnp.logical_or(c > 0, _iota((BLK, N_HEADS), 0) >= PAD)
    live_t = jnp.logical_or(c > 0, _iota((N_HEADS, BLK), 1) >= PAD)
    pre = dt_ref[...] + dtb_ref[...]
    pre_t = dtt_ref[...] + dtbt_ref[...]
    dt = jnp.where(live, _softplus(pre), 0.0)
    dt_t = jnp.where(live_t, _softplus(pre_t), 0.0)
    a = -jnp.exp(alog_ref[...])
    a_t = -jnp.exp(alogt_ref[...])
    li = _iota((BLK, BLK), 0)
    si = _iota((BLK, BLK), 1)
    lower = (si <= li).astype(BF16)
    upper = (li <= si).astype(BF16)
    acum = _dot_exact_x(lower, dt * a, NN)
    acum_t = _dot_x_exact(dt_t * a_t, upper, NN)
    return live, pre, dt, a, a_t, acum, acum_t


def _head_expand():
    return (_iota((N_HEADS, SSD_W), 1) // HEAD_DIM == _iota((N_HEADS, SSD_W), 0)).astype(BF16)


def _head_reduce_mat():
    return (_iota((SSD_W, N_HEADS), 0) // HEAD_DIM == _iota((SSD_W, N_HEADS), 1)).astype(BF16)


def _decay_mat(acum, acum_t, h, causal):
    seg = jnp.minimum(acum[:, h:h + 1] - acum_t[h:h + 1, :], 0.0)
    return jnp.where(causal, jnp.exp(seg), 0.0)


def _ssd_fwd(xa, dt_raw, dt_raw_t, dt_bias, dt_bias_t, a_log, a_log_t, d_exp):
    def body(x_ref, b_ref, c_ref, dt_ref, dtt_ref, dtb_ref, dtbt_ref, alog_ref, alogt_ref, dexp_ref,
             y_ref, hs_ref, state_ref):
        c = pl.program_id(0)

        @pl.when(c == 0)
        def _():
            state_ref[...] = jnp.zeros_like(state_ref)

        _, _, dt, _, _, acum, acum_t = _ssd_prelude(c, dt_ref, dtt_ref, dtb_ref, dtbt_ref, alog_ref, alogt_ref)
        expand = _head_expand()
        x = x_ref[...]
        xdt = x * _dot_x_exact(dt, expand, n=2)
        exp_a = _dot_x_exact(jnp.exp(acum), expand, n=2)
        to_end = _dot_x_exact(jnp.exp(acum[BLK - 1:BLK, :] - acum), expand, n=2)
        xdt_b = xdt.astype(BF16)
        xd_b = (xdt * to_end).astype(BF16)
        chunk_decay = jnp.exp(acum_t[:, BLK - 1:BLK])
        hs_ref[0] = state_ref[...]
        lane = _iota((BLK, BLK), 1)
        causal = _iota((BLK, BLK), 0) >= lane
        gw = HEADS_PER_GROUP * HEAD_DIM
        for g in range(N_GROUPS):
            bg = b_ref[:, g * N_STATE:(g + 1) * N_STATE].astype(BF16)
            cg = c_ref[:, g * N_STATE:(g + 1) * N_STATE].astype(BF16)
            cb = _dot(cg, bg, NT)
            hg = state_ref[g * gw:(g + 1) * gw, :]
            ch = _dot(cg, hg.astype(BF16), NT)
            st = _dot(xd_b[:, g * gw:(g + 1) * gw], bg, TN)
            for p in range(HEADS_PER_GROUP // 2):
                h0 = g * HEADS_PER_GROUP + 2 * p
                lo = h0 * HEAD_DIM
                xp = xdt_b[:, lo:lo + BLK]
                w0 = (cb * _decay_mat(acum, acum_t, h0, causal)).astype(BF16)
                w1 = (cb * _decay_mat(acum, acum_t, h0 + 1, causal)).astype(BF16)
                yd = jnp.where(lane < HEAD_DIM, _dot(w0, xp), _dot(w1, xp))
                y_ref[:, lo:lo + BLK] = (yd + ch[:, lo - g * gw:lo - g * gw + BLK] * exp_a[:, lo:lo + BLK]
                                         + x[:, lo:lo + BLK] * dexp_ref[:, lo:lo + BLK])
            for r in range(HEADS_PER_GROUP):
                h = g * HEADS_PER_GROUP + r
                state_ref[h * HEAD_DIM:(h + 1) * HEAD_DIM, :] = (
                    hg[r * HEAD_DIM:(r + 1) * HEAD_DIM, :] * chunk_decay[h:h + 1, :]
                    + st[r * HEAD_DIM:(r + 1) * HEAD_DIM, :])

    chunk = lambda width, col=0: pl.BlockSpec((BLK, width), lambda c: (c, col))
    return pl.pallas_call(
        body, name="ssd_fwd",
        out_shape=(jax.ShapeDtypeStruct((LP, SSD_W), F32), jax.ShapeDtypeStruct((NBLK, SSD_W, N_STATE), F32)),
        grid=(NBLK,),
        in_specs=[chunk(SSD_W), chunk(256, 4), chunk(256, 5), chunk(N_HEADS),
                  pl.BlockSpec((N_HEADS, BLK), lambda c: (0, c)), _const_spec((1, N_HEADS)),
                  _const_spec((N_HEADS, 1)), _const_spec((1, N_HEADS)), _const_spec((N_HEADS, 1)),
                  _const_spec((1, SSD_W))],
        out_specs=(chunk(SSD_W), pl.BlockSpec((1, SSD_W, N_STATE), lambda c: (c, 0, 0))),
        scratch_shapes=[pltpu.VMEM((SSD_W, N_STATE), F32)],
        compiler_params=_params(("arbitrary",)),
    )(xa, xa, xa, dt_raw, dt_raw_t, dt_bias, dt_bias_t, a_log, a_log_t, d_exp)


def _ssd_bwd(xa, dt_raw, dt_raw_t, dt_bias, dt_bias_t, a_log, a_log_t, d_exp, hstart, dy):
    def body(x_ref, b_ref, c_ref, dt_ref, dtt_ref, dtb_ref, dtbt_ref, alog_ref, alogt_ref, dexp_ref,
             hs_ref, dy_ref, dxa_ref, ddt_ref, dbias_ref, dalog_ref, dd_ref, dstate_ref):
        step = pl.program_id(0)
        c = NBLK - 1 - step

        @pl.when(step == 0)
        def _():
            dstate_ref[...] = jnp.zeros_like(dstate_ref)
            dbias_ref[...] = jnp.zeros_like(dbias_ref)
            dalog_ref[...] = jnp.zeros_like(dalog_ref)
            dd_ref[...] = jnp.zeros_like(dd_ref)

        live, pre, dt, a, a_t, acum, acum_t = _ssd_prelude(c, dt_ref, dtt_ref, dtb_ref, dtbt_ref,
                                                           alog_ref, alogt_ref)
        expand = _head_expand()
        reduce_m = _head_reduce_mat()
        x = x_ref[...]
        dyv = dy_ref[...]
        dt_e = _dot_x_exact(dt, expand, n=2)
        xdt = x * dt_e
        exp_acum = jnp.exp(acum)
        exp_a = _dot_x_exact(exp_acum, expand, n=2)
        dte = jnp.exp(acum[BLK - 1:BLK, :] - acum)
        to_end = _dot_x_exact(dte, expand, n=2)
        xdt_b = xdt.astype(BF16)
        xd_b = (xdt * to_end).astype(BF16)
        chunk_decay = jnp.exp(acum_t[:, BLK - 1:BLK])
        lane = _iota((BLK, BLK), 1)
        head0 = lane < HEAD_DIM
        causal = _iota((BLK, BLK), 0) >= lane
        gw = HEADS_PER_GROUP * HEAD_DIM
        dm = dyv * exp_a
        dm_b = dm.astype(BF16)
        onehot = lambda h: (_iota((1, N_HEADS), 1) == h).astype(F32)
        onehot_t = lambda h: (_iota((N_HEADS, 1), 0) == h).astype(F32)
        dacum = jnp.zeros((BLK, N_HEADS), F32)
        dacum_t = jnp.zeros((N_HEADS, BLK), F32)
        head_sums = lambda prod, rows: _dot_x_exact(prod, reduce_m[rows, :], n=2)
        dskip_acc = head_sums(dyv * x, slice(0, SSD_W))
        ddt_acc = jnp.zeros((BLK, N_HEADS), F32)
        ddte_acc = jnp.zeros((BLK, N_HEADS), F32)
        dexpa_acc = jnp.zeros((BLK, N_HEADS), F32)
        head_sum = expand
        for g in range(N_GROUPS):
            bg = b_ref[:, g * N_STATE:(g + 1) * N_STATE].astype(BF16)
            cg = c_ref[:, g * N_STATE:(g + 1) * N_STATE].astype(BF16)
            cb = _dot(cg, bg, NT)
            hg = hs_ref[0, g * gw:(g + 1) * gw, :]
            hg_b = hg.astype(BF16)
            dhe = dstate_ref[g * gw:(g + 1) * gw, :]
            dhe_b = dhe.astype(BF16)
            ch = _dot(cg, hg_b, NT)
            dcg = _dot(dm_b[:, g * gw:(g + 1) * gw], hg_b, NN)
            dhs = _dot(dm_b[:, g * gw:(g + 1) * gw], cg, TN)
            dxd = _dot(bg, dhe_b, NT)
            dbg = _dot(xd_b[:, g * gw:(g + 1) * gw], dhe_b, NN)
            dcb = jnp.zeros((BLK, BLK), F32)
            dxdt_g = []
            for p in range(HEADS_PER_GROUP // 2):
                h0 = g * HEADS_PER_GROUP + 2 * p
                lo = h0 * HEAD_DIM
                xp = xdt_b[:, lo:lo + BLK]
                dyp = dyv[:, lo:lo + BLK]
                dyh = (jnp.where(head0, dyp, 0.0).astype(BF16), jnp.where(head0, 0.0, dyp).astype(BF16))
                dxdt_p = jnp.zeros((BLK, BLK), F32)
                for q in range(2):
                    h = h0 + q
                    dec = _decay_mat(acum, acum_t, h, causal)
                    w = cb * dec
                    dw = _dot(dyh[q], xp, NT)
                    t = dw * w
                    dacum = dacum + jnp.sum(t, axis=1, keepdims=True) * onehot(h)
                    dacum_t = dacum_t - jnp.sum(t, axis=0, keepdims=True) * onehot_t(h)
                    dcb = dcb + dw * dec
                    dxdt_p = dxdt_p + _dot(w.astype(BF16), dyh[q], TN)
                sl = slice(lo, lo + BLK)
                gl = slice(lo - g * gw, lo - g * gw + BLK)
                dxdt_p = dxdt_p + dxd[:, gl] * to_end[:, sl]
                dxa_ref[:, sl] = dyp * dexp_ref[:, sl] + dxdt_p * dt_e[:, sl]
                dxdt_g.append(dxdt_p)
            cols = slice(g * gw, (g + 1) * gw)
            ddt_acc = ddt_acc + head_sums(jnp.concatenate(dxdt_g, axis=1) * x[:, cols], cols)
            ddte_acc = ddte_acc + head_sums(dxd * xdt[:, cols], cols)
            dexpa_acc = dexpa_acc + head_sums(dyv[:, cols] * ch, cols)
            dcb_b = dcb.astype(BF16)
            b_col = SSD_W + g * N_STATE
            c_col = SSD_W + (N_GROUPS + g) * N_STATE
            dxa_ref[:, c_col:c_col + N_STATE] = dcg + _dot(dcb_b, bg, NN)
            dxa_ref[:, b_col:b_col + N_STATE] = dbg + _dot(dcb_b, cg, TN)
            prod = dhe * hg
            per_head = jnp.sum(_dot_exact_x(head_sum[:, g * gw:(g + 1) * gw], prod, NN), axis=1, keepdims=True)
            dacum_t = dacum_t + (per_head * chunk_decay) * (_iota((1, BLK), 1) == BLK - 1).astype(F32)
            for r in range(HEADS_PER_GROUP):
                h = g * HEADS_PER_GROUP + r
                rows = slice(h * HEAD_DIM, (h + 1) * HEAD_DIM)
                dstate_ref[rows, :] = (dhs[r * HEAD_DIM:(r + 1) * HEAD_DIM, :]
                                       + dhe[r * HEAD_DIM:(r + 1) * HEAD_DIM, :] * chunk_decay[h:h + 1, :])
        dacum = dacum + dexpa_acc * exp_acum - ddte_acc * dte
        last_row = (_iota((BLK, 1), 0) == BLK - 1).astype(F32)
        dacum = dacum + last_row * jnp.sum(ddte_acc * dte, axis=0, keepdims=True)
        li = _iota((BLK, BLK), 0)
        si = _iota((BLK, BLK), 1)
        upper = (li <= si).astype(BF16)
        lower = (si <= li).astype(BF16)
        dda = _dot_exact_x(upper, dacum, NN)
        dda_t = _dot_x_exact(dacum_t, lower, NN)
        eye = (_iota((N_HEADS, N_HEADS), 0) == _iota((N_HEADS, N_HEADS), 1)).astype(BF16)
        dda = dda + _dot_x_exact_tn(dda_t, eye)
        ddt = ddt_acc + dda * a
        dalog_ref[...] += jnp.sum(dda * dt, axis=0, keepdims=True) * a
        dd_ref[...] += jnp.sum(dskip_acc, axis=0, keepdims=True)
        ddt_raw = jnp.where(live, ddt * _sigmoid(pre), 0.0)
        ddt_ref[...] = jnp.zeros_like(ddt_ref)
        ddt_ref[:, :N_HEADS] = ddt_raw
        dbias_ref[...] += jnp.sum(ddt_raw, axis=0, keepdims=True)

    rev = lambda width, col=0: pl.BlockSpec((BLK, width), lambda s: (NBLK - 1 - s, col))
    vec = jax.ShapeDtypeStruct((1, N_HEADS), F32)
    return pl.pallas_call(
        body, name="ssd_bwd",
        out_shape=(jax.ShapeDtypeStruct((LP, XBC_W), F32), jax.ShapeDtypeStruct((LP, BLK), F32), vec, vec, vec),
        grid=(NBLK,),
        in_specs=[rev(SSD_W), rev(256, 4), rev(256, 5), rev(N_HEADS),
                  pl.BlockSpec((N_HEADS, BLK), lambda s: (0, NBLK - 1 - s)), _const_spec((1, N_HEADS)),
                  _const_spec((N_HEADS, 1)), _const_spec((1, N_HEADS)), _const_spec((N_HEADS, 1)),
                  _const_spec((1, SSD_W)),
                  pl.BlockSpec((1, SSD_W, N_STATE), lambda s: (NBLK - 1 - s, 0, 0)), rev(SSD_W)],
        out_specs=(rev(XBC_W), rev(BLK), _const_spec((1, N_HEADS)),
                   _const_spec((1, N_HEADS)), _const_spec((1, N_HEADS))),
        scratch_shapes=[pltpu.VMEM((SSD_W, N_STATE), F32)],
        compiler_params=_params(("arbitrary",)),
    )(xa, xa, xa, dt_raw, dt_raw_t, dt_bias, dt_bias_t, a_log, a_log_t, d_exp, hstart, dy)


def _dot_x_exact_tn(x_t, eye):
    out = None
    for p in _split(x_t, 3):
        t = _dot(p, eye, TN)
        out = t if out is None else out + t
    return out


def _adamw(name, parts, w, m, v, rows):
    r_all, cols = w.shape
    assert r_all % rows == 0
    c1 = 1.0 / (1.0 - ADAM_B1 ** ADAM_STEP)
    c2 = 1.0 / (1.0 - ADAM_B2 ** ADAM_STEP)

    def body(p_ref, w_ref, m_ref, v_ref, g_ref, d_ref, mo_ref, vo_ref):
        g = p_ref[0].astype(F32)
        for j in range(1, parts.shape[0]):
            g = g + p_ref[j].astype(F32)
        mn = ADAM_B1 * m_ref[...] + (1.0 - ADAM_B1) * g
        vn = ADAM_B2 * v_ref[...] + (1.0 - ADAM_B2) * (g * g)
        g_ref[...] = g
        mo_ref[...] = mn
        vo_ref[...] = vn
        d_ref[...] = -ADAM_LR * ((mn * c1) / (jnp.sqrt(vn * c2) + ADAM_EPS) + ADAM_WD * w_ref[...])

    spec = pl.BlockSpec((rows, cols), lambda i: (i, 0))
    shp = jax.ShapeDtypeStruct((r_all, cols), F32)
    return pl.pallas_call(
        body, name=name, out_shape=(shp, shp, shp, shp), grid=(r_all // rows,),
        in_specs=[pl.BlockSpec((parts.shape[0], rows, cols), lambda i: (0, i, 0)), spec, spec, spec],
        out_specs=(spec, spec, spec, spec),
        compiler_params=_params(("parallel",)),
    )(parts, w, m, v)


_VECTORS = (("norm_w", 1024, 0), ("conv_b", 1536, 8), ("dt_bias", 16, 20), ("a_log", 16, 21), ("d_skip", 16, 22),
            ("sb_norm_w", 1024, 24), ("ssd_norm_w", 1024, 32), ("final_norm_w", 1024, 40))
_LOSS_ROW = 23
_CONVW_ROW = 48
_META_ROW = 96
_PACK_ROWS = 224
_SMALL_ORDER = tuple(name for name, _, _ in _VECTORS) + ("conv_w", "meta_tokens")


def _pack_small_grads(vectors, loss_row, d_convw, dh):
    def body(*refs):
        vec_refs, (loss_ref, cw_ref, dh_ref, out_ref) = refs[:len(_VECTORS)], refs[len(_VECTORS):]
        out_ref[...] = jnp.zeros_like(out_ref)
        for (_, width, row), ref in zip(_VECTORS, vec_refs):
            if width < BLK:
                out_ref[row:row + 1, :width] = ref[...]
            else:
                for t in range(width // BLK):
                    out_ref[row + t:row + t + 1, :] = ref[:, t * BLK:(t + 1) * BLK]
        out_ref[_LOSS_ROW:_LOSS_ROW + 1, :] = loss_ref[...]
        for k in range(4):
            for t in range(XBC_W // BLK):
                r = _CONVW_ROW + k * (XBC_W // BLK) + t
                out_ref[r:r + 1, :] = cw_ref[k:k + 1, t * BLK:(t + 1) * BLK]
        for i in range(N_META):
            for t in range(D_MODEL // BLK):
                r = _META_ROW + i * (D_MODEL // BLK) + t
                out_ref[r:r + 1, :] = dh_ref[i:i + 1, t * BLK:(t + 1) * BLK]

    full = lambda a: pl.BlockSpec(a.shape, lambda i: tuple(0 for _ in a.shape))
    return pl.pallas_call(
        body, name="pack_small_grads", out_shape=jax.ShapeDtypeStruct((_PACK_ROWS, BLK), F32), grid=(1,),
        in_specs=[full(v) for v in vectors] + [full(loss_row), full(d_convw),
                                               pl.BlockSpec((N_META, D_MODEL), lambda i: (PAD // N_META, 0))],
        out_specs=pl.BlockSpec((_PACK_ROWS, BLK), lambda i: (0, 0)),
        compiler_params=_params(("arbitrary",)),
    )(*vectors, loss_row, d_convw, dh)


def _sum_slots(name, parts, rows):
    n, r_all, cols = parts.shape

    def body(p_ref, o_ref):
        acc = p_ref[0].astype(F32)
        for j in range(1, n):
            acc = acc + p_ref[j].astype(F32)
        o_ref[...] = acc

    return pl.pallas_call(
        body, name=name, out_shape=jax.ShapeDtypeStruct((r_all, cols), F32), grid=(r_all // rows,),
        in_specs=[pl.BlockSpec((n, rows, cols), lambda i: (0, i, 0))],
        out_specs=pl.BlockSpec((rows, cols), lambda i: (i, 0)),
        compiler_params=_params(("parallel",)),
    )(parts)


def _adamw_small(pack, weights, moms, vels):
    c1 = 1.0 / (1.0 - ADAM_B1 ** ADAM_STEP)
    c2 = 1.0 / (1.0 - ADAM_B2 ** ADAM_STEP)
    n = len(_SMALL_ORDER)

    def body(*refs):
        p_ref = refs[0]
        w_refs, m_refs, v_refs = refs[1:1 + n], refs[1 + n:1 + 2 * n], refs[1 + 2 * n:1 + 3 * n]
        outs = refs[1 + 3 * n:1 + 7 * n]
        loss_ref, g_ref, cw_ref, cws_ref, mt_ref, all_ref, send_sems, recv_sems, loc_sem = refs[1 + 7 * n:]
        x, y, c = _mesh_position()
        me = 4 * x + 2 * y + c
        copies = _all_gather_copies(p_ref, all_ref, send_sems, recv_sems, loc_sem)
        for cp in copies:
            cp.start()
        for cp in copies:
            cp.wait()
        g = all_ref[0]
        for j in range(1, N_DEV):
            g = g + all_ref[j]
        g_ref[...] = g
        loss_ref[...] = g_ref[_LOSS_ROW:_LOSS_ROW + 1, :]

        def update(idx, grad):
            go_ref, d_ref, mo_ref, vo_ref = outs[4 * idx:4 * idx + 4]
            mn = ADAM_B1 * m_refs[idx][...] + (1.0 - ADAM_B1) * grad
            vn = ADAM_B2 * v_refs[idx][...] + (1.0 - ADAM_B2) * (grad * grad)
            go_ref[...] = grad
            mo_ref[...] = mn
            vo_ref[...] = vn
            d_ref[...] = -ADAM_LR * ((mn * c1) / (jnp.sqrt(vn * c2) + ADAM_EPS) + ADAM_WD * w_refs[idx][...])

        for idx, (_, width, row) in enumerate(_VECTORS):
            go_ref = outs[4 * idx]
            if width < BLK:
                grad = g_ref[row:row + 1, :width]
            else:
                for t in range(width // BLK):
                    go_ref[:, t * BLK:(t + 1) * BLK] = g_ref[row + t:row + t + 1, :]
                grad = go_ref[...]
            update(idx, grad)
        cw_ref[...] = jnp.zeros_like(cw_ref)
        for k in range(4):
            for t in range(XBC_W // BLK):
                r = _CONVW_ROW + k * (XBC_W // BLK) + t
                cw_ref[k:k + 1, t * BLK:(t + 1) * BLK] = g_ref[r:r + 1, :]
        for i in range(N_META):
            for t in range(D_MODEL // BLK):
                r = _META_ROW + i * (D_MODEL // BLK) + t
                mt_ref[i:i + 1, t * BLK:(t + 1) * BLK] = g_ref[r:r + 1, :]
        width_cw = XBC_W // N_DEV
        pick_cw = (_iota((XBC_W, width_cw), 0) == me * width_cw + _iota((XBC_W, width_cw), 1)).astype(BF16)
        cws_ref[...] = _dot_x_exact(cw_ref[...], pick_cw)
        update(n - 2, cws_ref[0:4, :])
        pick_mt = (_iota((D_MODEL, BLK), 0) == me * BLK + _iota((D_MODEL, BLK), 1)).astype(BF16)
        update(n - 1, _dot_x_exact(mt_ref[...], pick_mt))

    full = lambda a: pl.BlockSpec(a.shape, lambda i: tuple(0 for _ in a.shape))
    params = list(weights) + list(moms) + list(vels)
    out_shape, out_specs = [], []
    for w in weights:
        for _ in range(4):
            out_shape.append(jax.ShapeDtypeStruct(w.shape, F32))
            out_specs.append(full(w))
    out_shape.append(jax.ShapeDtypeStruct((1, BLK), F32))
    out_specs.append(pl.BlockSpec((1, BLK), lambda i: (0, 0)))
    return pl.pallas_call(
        body, name="adamw_small", out_shape=tuple(out_shape), grid=(1,),
        in_specs=[full(pack)] + [full(a) for a in params], out_specs=tuple(out_specs),
        scratch_shapes=[pltpu.VMEM((_PACK_ROWS, BLK), F32), pltpu.VMEM((8, XBC_W), F32),
                        pltpu.VMEM((8, XBC_W // N_DEV), F32), pltpu.VMEM((N_META, D_MODEL), F32),
                        pltpu.VMEM((N_DEV, _PACK_ROWS, BLK), F32)] + _ALL_GATHER_SEMS,
        compiler_params=_params(("arbitrary",)),
    )(pack, *params)


def kernel(x, meta_tokens, norm_w, w_in, conv_w, conv_b, dt_bias, a_log, d_skip, sb_norm_w, ssd_norm_w, w_out, final_norm_w, loss_target, m_meta_tokens, m_norm_w, m_w_in, m_conv_w, m_conv_b, m_dt_bias, m_a_log, m_d_skip, m_sb_norm_w, m_ssd_norm_w, m_w_out, m_final_norm_w, v_meta_tokens, v_norm_w, v_w_in, v_conv_w, v_conv_b, v_dt_bias, v_a_log, v_d_skip, v_sb_norm_w, v_ssd_norm_w, v_w_out, v_final_norm_w):
    small_src = jnp.concatenate([conv_w[0].reshape(6, BLK), meta_tokens, jnp.zeros((2, BLK), F32)], axis=0)
    small_g, w_in_g = _gather_weights([small_src, w_in[0].astype(BF16)])
    w_in_full = w_in_g.transpose(1, 0, 2).reshape(D_MODEL, D_IN)
    w_dt = w_in_full[:, D_MAIN:]
    conv_w_full = small_g[:, :6].reshape(N_DEV, 4, 192).transpose(1, 0, 2).reshape(4, XBC_W)
    meta_full = small_g[:, 6:6 + N_META].transpose(1, 0, 2).reshape(N_META, D_MODEL)
    h_pad = jnp.concatenate([jnp.zeros((PAD, D_MODEL), F32), meta_full, x[0]], axis=0)
    dt_bias_t = dt_bias.reshape(N_HEADS, 1)
    a_log_t = a_log.reshape(N_HEADS, 1)
    d_exp = jnp.repeat(d_skip, HEAD_DIM, axis=1)
    fnw = final_norm_w.reshape(1, D_MODEL)

    u, u_t, dt_raw, dt_raw_t = _prenorm(h_pad, norm_w, w_dt, w_dt.T)
    proj = _matmul("in_proj", u, w_in_full, "nn", LP, 512, D_MODEL, n_cols=D_MAIN)
    o_sb, o_sb_exact, w_out_g = _sb_attention_fwd(proj, w_out[0].astype(BF16))
    w_out_full = w_out_g.reshape(2 * SSD_W, D_MODEL)
    xa = _conv_fwd(proj, conv_w_full, conv_b)
    o_ssd, hstart = _ssd_fwd(xa, dt_raw, dt_raw_t, dt_bias, dt_bias_t, a_log, a_log_t, d_exp)
    ycat = _ycat(o_sb, proj, o_ssd, sb_norm_w, ssd_norm_w)
    yo = _matmul("out_proj", ycat, w_out_full, "nn", LP // 2, 512, 2 * SSD_W)
    dh2, dh2_b, loss_row, d_fnw = _loss_head(h_pad, yo, fnw, loss_target[0])

    g_w_out = _matmul("d_w_out", ycat, dh2_b, "tn", 512, 512, LP, BF16).reshape(4, 2, 256, D_MODEL)
    do_sb, dg, do_ssd, dz, d_sbw, d_ssdw, sib_w_out = _ycat_bwd(
        dh2_b, w_out_full, o_sb, proj, o_ssd, sb_norm_w, ssd_norm_w, g_w_out)
    chip_w_out = _pair_sum("pair_sum_w_out", g_w_out, sib_w_out, 256)
    dq, dk, dv, p_w_out = _sb_attention_bwd(proj, o_sb_exact, do_sb, chip_w_out)
    dxa, ddt_raw, d_dtb, d_alog, d_dskip = _ssd_bwd(
        xa, dt_raw, dt_raw_t, dt_bias, dt_bias_t, a_log, a_log_t, d_exp, hstart, do_ssd)
    dxbc, d_convw, d_convb = _conv_bwd(dxa, proj, conv_w_full, conv_b)
    pieces = [dq, dk, dv, dg, dz, dxbc, ddt_raw]
    g_w_in = _d_w_in(u_t, pieces).reshape(1, 2, D_MODEL // 2, W_IN_PAD)
    chip_w_in = _pair_sum("pair_sum_w_in", g_w_in, _swap_with_sibling("swap_w_in", g_w_in), 128)
    dh, d_nw, win_parts = _d_u_prenorm_bwd(pieces, w_in_full, w_dt, h_pad, norm_w, dh2, chip_w_in[0])
    win_mine = _sum_slots("sum_w_in_windows", win_parts, 128)
    win_other = _swap_with_sibling("swap_w_in_window", win_mine, whole=True)
    core = lax.axis_index("c")
    chip = 2 * lax.axis_index("x") + lax.axis_index("y")
    first_col = (D_IN // N_DEV) * (2 * chip + core) - WIN_STRIDE * chip
    cut = lambda w: lax.dynamic_slice(w, (0, first_col), (D_MODEL // 2, D_IN // N_DEV))
    half_mine, half_other = cut(win_mine), cut(win_other)
    p_w_in = jnp.concatenate([jnp.where(core == 0, half_mine, half_other),
                              jnp.where(core == 0, half_other, half_mine)], axis=0)[None]

    pack = _pack_small_grads([d_nw, d_convb, d_dtb, d_alog, d_dskip, d_sbw, d_ssdw, d_fnw], loss_row, d_convw, dh)

    res_in = _adamw("adamw_w_in", p_w_in, w_in[0], m_w_in[0], v_w_in[0], 128)
    res_out = _adamw("adamw_w_out", p_w_out, w_out[0], m_w_out[0], v_w_out[0], 128)
    res_small = _adamw_small(
        pack,
        [norm_w, conv_b, dt_bias, a_log, d_skip, sb_norm_w, ssd_norm_w, fnw, conv_w[0], meta_tokens],
        [m_norm_w, m_conv_b, m_dt_bias, m_a_log, m_d_skip, m_sb_norm_w, m_ssd_norm_w,
         m_final_norm_w.reshape(1, D_MODEL), m_conv_w[0], m_meta_tokens],
        [v_norm_w, v_conv_b, v_dt_bias, v_a_log, v_d_skip, v_sb_norm_w, v_ssd_norm_w,
         v_final_norm_w.reshape(1, D_MODEL), v_conv_w[0], v_meta_tokens])

    loss = jnp.sum(res_small[-1])
    order = ["meta_tokens", "norm_w", "w_in", "conv_w", "conv_b", "dt_bias", "a_log", "d_skip",
             "sb_norm_w", "ssd_norm_w", "w_out", "final_norm_w"]
    outs = [loss, dh[BLK:].reshape(1, SEQ, D_MODEL)]
    for kind in range(4):
        small = {name: res_small[4 * idx + kind] for idx, name in enumerate(_SMALL_ORDER)}
        small["final_norm_w"] = small["final_norm_w"].reshape(D_MODEL)
        small["conv_w"] = small["conv_w"].reshape(1, 4, XBC_W // N_DEV)
        small["w_in"] = res_in[kind].reshape(1, D_MODEL, D_IN // N_DEV)
        small["w_out"] = res_out[kind].reshape(1, 256, D_MODEL)
        outs += [small[name] for name in order]
    return tuple(outs)
```

```python
import jax
import jax.numpy as jnp
from jax import lax
from jax.experimental import pallas as pl
from jax.experimental.pallas import tpu as pltpu

F32 = jnp.float32
BF16 = jnp.bfloat16

D_MODEL = 1024
SEQ = 2048
N_META = 16
BLK = 128
PAD = BLK - N_META
LP = PAD + N_META + SEQ
NBLK = LP // BLK
N_HEADS = 16
HEAD_DIM = 64
N_GROUPS = 2
HEADS_PER_GROUP = 8
N_STATE = 128
SSD_W = 1024
XBC_W = 1536
D_MAIN = 6656
D_IN = 6672
COL_Q, COL_K, COL_V, COL_G, COL_Z, COL_XBC = 0, 1024, 2048, 3072, 4096, 5120
N_DEV = 8
EPS = 1e-5
SB_SCALE = 0.125
SB_DEAD = -87.4
SB_MASKED = -1e30

ADAM_LR = 0.001
ADAM_B1 = 0.9
ADAM_B2 = 0.999
ADAM_EPS = 1e-08
ADAM_WD = 0.01
ADAM_STEP = 10

VMEM_LIMIT = 48 * 1024 * 1024
ATTN_VMEM_LIMIT = 56 * 1024 * 1024

NN = (((1,), (0,)), ((), ()))
NT = (((1,), (1,)), ((), ()))
TN = (((0,), (0,)), ((), ()))


def _dot(a, b, dims=NN):
    return lax.dot_general(a, b, dims, preferred_element_type=F32)


def _split(x, n):
    parts = []
    r = x
    for i in range(n):
        p = r.astype(BF16)
        parts.append(p)
        if i + 1 < n:
            r = r - p.astype(F32)
    return parts


def _dot_x_exact(x, m, dims=NN, n=3):
    out = None
    for p in _split(x, n):
        t = _dot(p, m, dims)
        out = t if out is None else out + t
    return out


def _dot_exact_x(m, x, dims=NN, n=3):
    out = None
    for p in _split(x, n):
        t = _dot(m, p, dims)
        out = t if out is None else out + t
    return out


def _iota(shape, dim):
    return lax.broadcasted_iota(jnp.int32, shape, dim)


def _softplus(x):
    return jnp.maximum(x, 0.0) + jnp.log(1.0 + jnp.exp(-jnp.abs(x)))


def _sigmoid(x):
    return 1.0 / (1.0 + jnp.exp(-x))


def _params(sem=None, vmem=None):
    return pltpu.CompilerParams(dimension_semantics=sem, vmem_limit_bytes=vmem or VMEM_LIMIT)


_ANY = pl.BlockSpec(memory_space=pl.ANY)
_MESH = pl.DeviceIdType.MESH


def _mesh_position():
    return lax.axis_index("x"), lax.axis_index("y"), lax.axis_index("c")


def _other_chips(x, y):
    return [(1 - x, y), (x, 1 - y), (1 - x, 1 - y)]


def _gather_weights(srcs):
    n = len(srcs)

    def body(*refs):
        src, out = refs[:n], refs[n:2 * n]
        send_sems, recv_sems, loc_sems = refs[2 * n:]
        x, y, c = _mesh_position()
        sibling = (x, y, 1 - c)
        chips = _other_chips(x, y)
        relay_from = (jnp.where(c == 0, 1 - x, x), jnp.where(c == 0, y, 1 - y))
        relay_to = (jnp.where(c == 0, x, 1 - x), jnp.where(c == 0, 1 - y, y))

        def copy(a, k, block, to, from_src=False):
            slot = out[a].at[4 * block[0] + 2 * block[1] + block[2]]
            return pltpu.make_async_remote_copy(
                src_ref=src[a] if from_src else slot, dst_ref=slot,
                send_sem=send_sems.at[7 * a + k], recv_sem=recv_sems.at[7 * a + k],
                device_id=to, device_id_type=_MESH)

        local, sends = [], []
        for a in range(n):
            mine = pltpu.make_async_copy(src[a], out[a].at[4 * x + 2 * y + c], loc_sems.at[a])
            mine.start()
            local.append(mine)
            first = [copy(a, 0, (x, y, c), sibling, True)]
            first += [copy(a, 1 + j, (x, y, c), (*chip, c), True) for j, chip in enumerate(chips[:2])]
            for cp in first:
                cp.start()
            sends += first
        for a in range(n):
            for j, chip in enumerate(chips[:2]):
                copy(a, 1 + j, (*chip, c), (x, y, c)).wait_recv()
            later = [copy(a, 3, (*relay_from, c), (*relay_to, c))]
            later += [copy(a, 4 + j, (*chip, c), sibling) for j, chip in enumerate(chips[:2])]
            for cp in later:
                cp.start()
            sends += later
        for a in range(n):
            copy(a, 3, (*chips[2], c), (x, y, c)).wait_recv()
            passed = copy(a, 6, (*chips[2], c), sibling)
            passed.start()
            sends.append(passed)
        for a in range(n):
            copy(a, 0, (x, y, 1 - c), (x, y, c)).wait_recv()
            for j, chip in enumerate(chips):
                copy(a, 4 + j, (*chip, 1 - c), (x, y, c)).wait_recv()
        for cp in sends:
            cp.wait_send()
        for cp in local:
            cp.wait()

    return pl.pallas_call(
        body, name="gather_weights",
        out_shape=tuple(jax.ShapeDtypeStruct((N_DEV,) + s.shape, s.dtype) for s in srcs),
        in_specs=[_ANY] * n, out_specs=tuple([_ANY] * n),
        scratch_shapes=[pltpu.SemaphoreType.DMA((7 * n,)), pltpu.SemaphoreType.DMA((7 * n,)),
                        pltpu.SemaphoreType.DMA((n,))],
    )(*srcs)


_ALL_GATHER_SEMS = [pltpu.SemaphoreType.DMA((7,)), pltpu.SemaphoreType.DMA((7,)), pltpu.SemaphoreType.DMA]


def _all_gather_copies(src, out, send_sems, recv_sems, loc_sem):
    x, y, c = _mesh_position()
    me = 4 * x + 2 * y + c
    copies = [pltpu.make_async_copy(src, out.at[me], loc_sem)]
    for k in range(1, N_DEV):
        peer = (1 - x if k & 4 else x, 1 - y if k & 2 else y, 1 - c if k & 1 else c)
        copies.append(pltpu.make_async_remote_copy(
            src_ref=src, dst_ref=out.at[me], send_sem=send_sems.at[k - 1], recv_sem=recv_sems.at[k - 1],
            device_id=peer, device_id_type=_MESH))
    return copies


def _sibling_swap_copies(src, out, send_sems, recv_sems, n, whole=False):
    x, y, c = _mesh_position()
    return [pltpu.make_async_remote_copy(
        src_ref=src if whole else src.at[k, 1 - c], dst_ref=out if whole else out.at[k],
        send_sem=send_sems.at[k], recv_sem=recv_sems.at[k], device_id=(x, y, 1 - c), device_id_type=_MESH)
        for k in range(n)]


def _swap_with_sibling(name, src, whole=False):
    n = 1 if whole else src.shape[0]

    def body(src_ref, out_ref, send_sems, recv_sems):
        copies = _sibling_swap_copies(src_ref, out_ref, send_sems, recv_sems, n, whole)
        for cp in copies:
            cp.start()
        for cp in copies:
            cp.wait()

    shape = src.shape if whole else (src.shape[0],) + src.shape[2:]
    return pl.pallas_call(
        body, name=name, out_shape=jax.ShapeDtypeStruct(shape, src.dtype), in_specs=[_ANY], out_specs=_ANY,
        scratch_shapes=[pltpu.SemaphoreType.DMA((n,)), pltpu.SemaphoreType.DMA((n,))],
    )(src)


def _pair_sum(name, g, sib, rows):
    n, _, r_all, cols = g.shape
    assert r_all % rows == 0

    def body(g0_ref, g1_ref, s_ref, o_ref):
        c = lax.axis_index("c")
        mine = jnp.where(c == 0, g0_ref[0, 0].astype(F32), g1_ref[0, 0].astype(F32))
        o_ref[0] = (mine + s_ref[0].astype(F32)).astype(o_ref.dtype)

    return pl.pallas_call(
        body, name=name, out_shape=jax.ShapeDtypeStruct((n, r_all, cols), BF16), grid=(n, r_all // rows),
        in_specs=[pl.BlockSpec((1, 1, rows, cols), lambda k, i: (k, 0, i, 0)),
                  pl.BlockSpec((1, 1, rows, cols), lambda k, i: (k, 1, i, 0)),
                  pl.BlockSpec((1, rows, cols), lambda k, i: (k, i, 0))],
        out_specs=pl.BlockSpec((1, rows, cols), lambda k, i: (k, i, 0)),
        compiler_params=_params(("parallel", "parallel")),
    )(g, g, sib)


_CHIP_EXCHANGE_SEMS = [pltpu.SemaphoreType.DMA((3,)), pltpu.SemaphoreType.DMA((3,)), pltpu.SemaphoreType.DMA]


def _chip_exchange_copies(src, out, send_sems, recv_sems, loc_sem, window=None):
    x, y, c = _mesh_position()
    here = 2 * x + y

    def slot(k):
        if window is not None:
            return src.at[:, pl.ds(pl.multiple_of(k * window[0], BLK), window[1])]
        return src.at[k]

    copies = [pltpu.make_async_copy(slot(here), out.at[here], loc_sem)]
    for j, chip in enumerate(_other_chips(x, y)):
        copies.append(pltpu.make_async_remote_copy(
            src_ref=slot(2 * chip[0] + chip[1]), dst_ref=out.at[here],
            send_sem=send_sems.at[j], recv_sem=recv_sems.at[j], device_id=(*chip, c), device_id_type=_MESH))
    return copies


def _matmul(name, a, b, kind, tm, tn, tk, out_dtype=F32, n_cols=None):
    if kind == "nn":
        (m, kk), nn_ = a.shape, (n_cols or b.shape[1])
        a_spec = pl.BlockSpec((tm, tk), lambda i, j, k: (i, k))
        b_spec = pl.BlockSpec((tk, tn), lambda i, j, k: (k, j))
        dims = NN
    else:
        (kk, m), (_, nn_) = a.shape, b.shape
        a_spec = pl.BlockSpec((tk, tm), lambda i, j, k: (k, i))
        b_spec = pl.BlockSpec((tk, tn), lambda i, j, k: (k, j))
        dims = TN
    assert m % tm == 0 and nn_ % tn == 0 and kk % tk == 0
    nk = kk // tk

    def body(a_ref, b_ref, o_ref, acc_ref):
        k = pl.program_id(2)
        part = _dot(a_ref[...], b_ref[...], dims)
        if nk == 1:
            o_ref[...] = part.astype(o_ref.dtype)
        else:
            @pl.when(k == 0)
            def _():
                acc_ref[...] = part

            @pl.when(k > 0)
            def _():
                acc_ref[...] += part

            @pl.when(k == nk - 1)
            def _():
                o_ref[...] = acc_ref[...].astype(o_ref.dtype)

    return pl.pallas_call(
        body, name=name, out_shape=jax.ShapeDtypeStruct((m, nn_), out_dtype),
        grid=(m // tm, nn_ // tn, nk),
        in_specs=[a_spec, b_spec], out_specs=pl.BlockSpec((tm, tn), lambda i, j, k: (i, j)),
        scratch_shapes=[pltpu.VMEM((tm, tn) if nk > 1 else (8, 128), F32)],
        compiler_params=_params(("parallel", "parallel", "arbitrary")),
    )(a, b)


def _row_spec(width, col=0):
    return pl.BlockSpec((BLK, width), lambda i: (i, col))


def _const_spec(shape):
    return pl.BlockSpec(shape, lambda i: tuple(0 for _ in shape))


def _prenorm(h_pad, norm_w, wdt, wdt_t):
    def body(h_ref, w_ref, wdt_ref, wdtt_ref, u_ref, ut_ref, dt_ref, dtt_ref):
        xv = h_ref[...]
        r = lax.rsqrt(jnp.mean(xv * xv, axis=-1, keepdims=True) + EPS)
        uf = xv * r * w_ref[...]
        u = uf.astype(BF16)
        u_ref[...] = u
        ut_ref[...] = uf.T.astype(BF16)
        dt_ref[...] = _dot(u, wdt_ref[...], NN)
        dtt_ref[...] = _dot(wdtt_ref[...], u, NT)

    return pl.pallas_call(
        body, name="prenorm",
        out_shape=(jax.ShapeDtypeStruct((LP, D_MODEL), BF16), jax.ShapeDtypeStruct((D_MODEL, LP), BF16),
                   jax.ShapeDtypeStruct((LP, N_HEADS), F32), jax.ShapeDtypeStruct((N_HEADS, LP), F32)),
        grid=(NBLK,),
        in_specs=[_row_spec(D_MODEL), _const_spec((1, D_MODEL)), _const_spec((D_MODEL, N_HEADS)),
                  _const_spec((N_HEADS, D_MODEL))],
        out_specs=(_row_spec(D_MODEL), pl.BlockSpec((D_MODEL, BLK), lambda i: (0, i)), _row_spec(N_HEADS),
                   pl.BlockSpec((N_HEADS, BLK), lambda i: (0, i))),
        compiler_params=_params(("parallel",)),
    )(h_pad, norm_w, wdt, wdt_t)


def _gated_norm(o, g, w):
    a = o * (g * _sigmoid(g))
    r = lax.rsqrt(jnp.mean(a * a, axis=-1, keepdims=True) + EPS)
    return a * r * w


def _out_head(o_sb, proj, o_ssd, sb_w, ssd_w, w_out_full, h_pad, fnw, target):
    tm = LP // 8

    def body(osb_ref, g_ref, ossd_ref, z_ref, sbw_ref, ssdw_ref, wout_ref, h_ref, w_ref, t_hbm,
             y_ref, dh2_ref, dh2b_ref, loss_ref, dw_ref, t_ref, t_sem):
        i = pl.program_id(0)

        @pl.when(i == 0)
        def _():
            loss_ref[...] = jnp.zeros_like(loss_ref)
            dw_ref[...] = jnp.zeros_like(dw_ref)
            t_ref[:BLK, :] = jnp.zeros((BLK, D_MODEL), F32)
            first = pltpu.make_async_copy(t_hbm.at[pl.ds(0, tm - BLK)], t_ref.at[pl.ds(BLK, tm - BLK)], t_sem)
            first.start()
            first.wait()

        @pl.when(i > 0)
        def _():
            rest = pltpu.make_async_copy(t_hbm.at[pl.ds(pl.multiple_of(i * tm - BLK, 8), tm)], t_ref, t_sem)
            rest.start()
            rest.wait()

        y_ref[:, :SSD_W] = _gated_norm(osb_ref[...], g_ref[...], sbw_ref[...]).astype(BF16)
        y_ref[:, SSD_W:] = _gated_norm(ossd_ref[...], z_ref[...], ssdw_ref[...]).astype(BF16)
        h2 = h_ref[...] + _dot(y_ref[...], wout_ref[...], NN)
        r = lax.rsqrt(jnp.mean(h2 * h2, axis=-1, keepdims=True) + EPS)
        nrm = h2 * r
        w = w_ref[...]
        live = i * tm + _iota((tm, D_MODEL), 0) >= BLK
        err = jnp.where(live, nrm * w - t_ref[...], 0.0)
        dout = err * (1.0 / D_MODEL)
        loss_ref[...] += (0.5 / D_MODEL) * _fold_lanes(jnp.sum(err * err, axis=0, keepdims=True))
        dw_ref[...] += jnp.sum(dout * nrm, axis=0, keepdims=True)
        wd = dout * w
        dh2 = r * (wd - nrm * jnp.mean(wd * nrm, axis=-1, keepdims=True))
        dh2_ref[...] = dh2
        dh2b_ref[...] = dh2.astype(BF16)

    rows = lambda width, col=0: pl.BlockSpec((tm, width), lambda i: (i, col))
    return pl.pallas_call(
        body, name="out_head",
        out_shape=(jax.ShapeDtypeStruct((LP, 2 * SSD_W), BF16), jax.ShapeDtypeStruct((LP, D_MODEL), F32),
                   jax.ShapeDtypeStruct((LP, D_MODEL), BF16), jax.ShapeDtypeStruct((1, BLK), F32),
                   jax.ShapeDtypeStruct((1, D_MODEL), F32)),
        grid=(LP // tm,),
        in_specs=[rows(1024), rows(1024, COL_G // 1024), rows(1024), rows(1024, COL_Z // 1024),
                  _const_spec((1, 1024)), _const_spec((1, 1024)),
                  pl.BlockSpec((2 * SSD_W, D_MODEL), lambda i: (0, 0), pipeline_mode=pl.Buffered(1)),
                  rows(D_MODEL), _const_spec((1, D_MODEL)), _ANY],
        out_specs=(rows(2 * SSD_W), rows(D_MODEL), rows(D_MODEL), _const_spec((1, BLK)), _const_spec((1, D_MODEL))),
        scratch_shapes=[pltpu.VMEM((tm, D_MODEL), F32), pltpu.SemaphoreType.DMA],
        compiler_params=_params(("arbitrary",)),
    )(o_sb, proj, o_ssd, proj, sb_w, ssd_w, w_out_full, h_pad, fnw, target)


def _fold_lanes(row):
    out = row[:, :BLK]
    for j in range(1, row.shape[1] // BLK):
        out = out + row[:, j * BLK:(j + 1) * BLK]
    return out


def _gated_norm_bwd(dy, o, g, w):
    s = _sigmoid(g)
    sg = g * s
    a = o * sg
    r = lax.rsqrt(jnp.mean(a * a, axis=-1, keepdims=True) + EPS)
    nrm = a * r
    dw = jnp.sum(dy * nrm, axis=0, keepdims=True)
    wd = dy * w
    da = r * (wd - nrm * jnp.mean(wd * nrm, axis=-1, keepdims=True))
    return da * sg, da * o * (s * (1.0 + g * (1.0 - s))), dw


def _ycat_bwd(dh2_b, w_out_full, o_sb, proj, o_ssd, sb_w, ssd_w, g_w_out):
    tm = LP // 8
    n_slots = g_w_out.shape[0]

    def body(a_ref, w_ref, osb_ref, g_ref, ossd_ref, z_ref, sbw_ref, ssdw_ref, src_ref,
             dosb_ref, dg_ref, dossd_ref, dz_ref, dsbw_ref, dssdw_ref, sib_ref, send_sems, recv_sems):
        i = pl.program_id(0)

        @pl.when(i == 0)
        def _():
            for cp in _sibling_swap_copies(src_ref, sib_ref, send_sems, recv_sems, n_slots):
                cp.start()
            dsbw_ref[...] = jnp.zeros_like(dsbw_ref)
            dssdw_ref[...] = jnp.zeros_like(dssdw_ref)

        dy = _dot(a_ref[...], w_ref[...], NT)
        do, dg, dw = _gated_norm_bwd(dy[:, :SSD_W], osb_ref[...], g_ref[...], sbw_ref[...])
        dosb_ref[...] = do
        dg_ref[...] = dg.astype(BF16)
        dsbw_ref[...] += dw
        do, dg, dw = _gated_norm_bwd(dy[:, SSD_W:], ossd_ref[...], z_ref[...], ssdw_ref[...])
        dossd_ref[...] = do
        dz_ref[...] = dg.astype(BF16)
        dssdw_ref[...] += dw

        @pl.when(i == LP // tm - 1)
        def _():
            for cp in _sibling_swap_copies(src_ref, sib_ref, send_sems, recv_sems, n_slots):
                cp.wait()

    act = jax.ShapeDtypeStruct((LP, 1024), F32)
    gate = jax.ShapeDtypeStruct((LP, 1024), BF16)
    vec = jax.ShapeDtypeStruct((1, 1024), F32)
    rows = lambda col=0: pl.BlockSpec((tm, 1024), lambda i: (i, col))
    return pl.pallas_call(
        body, name="ycat_bwd",
        out_shape=(act, gate, act, gate, vec, vec,
                   jax.ShapeDtypeStruct((n_slots,) + g_w_out.shape[2:], g_w_out.dtype)),
        grid=(LP // tm,),
        in_specs=[rows(), _const_spec((2 * SSD_W, D_MODEL)), rows(), rows(COL_G // 1024), rows(),
                  rows(COL_Z // 1024), _const_spec((1, 1024)), _const_spec((1, 1024)), _ANY],
        out_specs=(rows(), rows(), rows(), rows(), _const_spec((1, 1024)), _const_spec((1, 1024)), _ANY),
        scratch_shapes=[pltpu.SemaphoreType.DMA((n_slots,)), pltpu.SemaphoreType.DMA((n_slots,))],
        compiler_params=_params(("arbitrary",)),
    )(dh2_b, w_out_full, o_sb, proj, o_ssd, proj, sb_w, ssd_w, g_w_out)


_DPROJ_PIECES = (("dq", COL_Q, 1024), ("dk", COL_K, 1024), ("dv", COL_V, 1024), ("dg", COL_G, 1024),
                 ("dz", COL_Z, 1024), ("dxbc", COL_XBC, XBC_W), ("ddt", D_MAIN, BLK))
W_IN_PAD = D_MAIN + 512
WIN_STRIDE = 13 * BLK
WIN_WIDTH = 14 * BLK


def _d_w_in(u_t, pieces):
    tn = 512
    nj = W_IN_PAD // tn
    main = _DPROJ_PIECES[:-1]

    def body(*refs):
        u_ref, piece_refs, (ddt_ref, o_ref) = refs[0], refs[1:1 + len(main)], refs[1 + len(main):]
        j = pl.program_id(1)
        a = u_ref[...]
        for (_, col, width), ref in zip(main, piece_refs):
            @pl.when(jnp.logical_and(j >= col // tn, j < (col + width) // tn))
            def _():
                o_ref[...] = _dot(a, ref[...], NN).astype(BF16)

        @pl.when(j == nj - 1)
        def _():
            o_ref[...] = jnp.zeros_like(o_ref)
            o_ref[:, :BLK] = _dot(a, ddt_ref[...].astype(BF16), NN).astype(BF16)

    def piece_spec(col, width):
        return pl.BlockSpec((LP, tn), lambda i, j: (0, jnp.clip(j - col // tn, 0, width // tn - 1)))

    return pl.pallas_call(
        body, name="d_w_in", out_shape=jax.ShapeDtypeStruct((D_MODEL, W_IN_PAD), BF16), grid=(1, nj),
        in_specs=[pl.BlockSpec((D_MODEL, LP), lambda i, j: (0, 0), pipeline_mode=pl.Buffered(1))]
        + [piece_spec(col, width) for _, col, width in main]
        + [pl.BlockSpec((LP, BLK), lambda i, j: (0, 0))],
        out_specs=pl.BlockSpec((D_MODEL, tn), lambda i, j: (0, j)),
        compiler_params=_params(("parallel", "arbitrary")),
    )(u_t, *pieces)


def _d_u_prenorm_bwd(pieces, w_main, wdt, h_pad, norm_w, dh2, chip_sum):
    tm, tk = LP // 4, 512
    nk = D_MAIN // tk
    main = _DPROJ_PIECES[:-1]
    window = (WIN_STRIDE, WIN_WIDTH)

    def body(*refs):
        piece_refs = refs[:len(main)]
        (b_ref, ddt_ref, wdt_ref, h_ref, w_ref, dh2_ref, src_ref, dh_ref, dw_ref, out_ref,
         acc_ref, send_sems, recv_sems, loc_sem) = refs[len(main):]
        i, k = pl.program_id(0), pl.program_id(1)

        @pl.when(jnp.logical_and(i == 0, k == 0))
        def _():
            for cp in _chip_exchange_copies(src_ref, out_ref, send_sems, recv_sems, loc_sem, window=window):
                cp.start()
            dw_ref[...] = jnp.zeros_like(dw_ref)

        @pl.when(k == 0)
        def _():
            acc_ref[...] = jnp.zeros_like(acc_ref)

        for (_, col, width), ref in zip(main, piece_refs):
            for lo in range(0, width, tk):
                @pl.when(k == (col + lo) // tk)
                def _():
                    acc_ref[...] += _dot(ref[:, lo:lo + tk], b_ref[...], NT)

        @pl.when(k == nk - 1)
        def _():
            dut = acc_ref[...] + _dot(ddt_ref[:, :N_HEADS].astype(BF16), wdt_ref[...], NT)
            xv = h_ref[...]
            r = lax.rsqrt(jnp.mean(xv * xv, axis=-1, keepdims=True) + EPS)
            nrm = xv * r
            dw_ref[...] += jnp.sum(dut * nrm, axis=0, keepdims=True)
            wd = dut * w_ref[...]
            dh_ref[...] = dh2_ref[...] + r * (wd - nrm * jnp.mean(wd * nrm, axis=-1, keepdims=True))

        @pl.when(jnp.logical_and(i == LP // tm - 1, k == nk - 1))
        def _():
            for cp in _chip_exchange_copies(src_ref, out_ref, send_sems, recv_sems, loc_sem, window=window):
                cp.wait()

    rows = lambda width: pl.BlockSpec((tm, width), lambda i, k: (i, 0))
    const = lambda shape: pl.BlockSpec(shape, lambda i, k: (0, 0))
    return pl.pallas_call(
        body, name="d_u_prenorm_bwd",
        out_shape=(jax.ShapeDtypeStruct((LP, D_MODEL), F32), jax.ShapeDtypeStruct((1, D_MODEL), F32),
                   jax.ShapeDtypeStruct((4, chip_sum.shape[0], WIN_WIDTH), chip_sum.dtype)),
        grid=(LP // tm, nk),
        in_specs=[rows(width) for _, _, width in main]
        + [pl.BlockSpec((D_MODEL, tk), lambda i, k: (0, k)), rows(BLK), const((D_MODEL, N_HEADS)), rows(D_MODEL),
           const((1, D_MODEL)), rows(D_MODEL), _ANY],
        out_specs=(rows(D_MODEL), const((1, D_MODEL)), _ANY),
        scratch_shapes=[pltpu.VMEM((tm, D_MODEL), F32)] + _CHIP_EXCHANGE_SEMS,
        compiler_params=_params(("arbitrary", "arbitrary")),
    )(*pieces[:-1], w_main, pieces[-1], wdt, h_pad, norm_w, dh2, chip_sum)


def _suffix_sum(vals, tri):
    return _dot_x_exact(vals, tri, NN, n=2)


def _sb_tile(z):
    t = jnp.exp(-jnp.abs(z))
    inv = 1.0 / (1.0 + t)
    sp = jnp.maximum(z, 0.0) + jnp.log(1.0 + t)
    sig = jnp.where(z >= 0, inv, t * inv)
    return sig, -sp, z - sp


def _sweep(first, step, init, run_slots):
    def alive_of(state):
        top = state[run_slots[0]]
        for s in run_slots[1:]:
            top = jnp.maximum(top, state[s])
        return (jnp.max(top) > SB_DEAD).astype(jnp.int32)

    def cond(carry):
        return jnp.logical_and(carry[0] >= 0, carry[1] > 0)

    def body(carry):
        state = step(carry[0], tuple(carry[2:]))
        return (carry[0] - 1, alive_of(state)) + tuple(state)

    return lax.while_loop(cond, body, (first, alive_of(init)) + tuple(init))[2:]


def _head_masks(x, lane):
    head0 = lane < HEAD_DIM
    return [jnp.where(head0, x, 0.0).astype(BF16), jnp.where(head0, 0.0, x).astype(BF16)]


PAIRS_PER_STEP = 4
STEP_W = PAIRS_PER_STEP * BLK
STEP_HEADS = 2 * PAIRS_PER_STEP


def _pair_lanes(h):
    lo = (h // 2) * BLK
    return slice(lo, lo + BLK)


def _stage_kv(qi, k_ref, v_ref, kb_ref, vb_ref):
    @pl.when(qi == 0)
    def _():
        kb_ref[:BLK, :] = jnp.zeros((BLK, STEP_W), BF16)
        vb_ref[:BLK, :] = jnp.zeros((BLK, STEP_W), BF16)
        kb_ref[BLK:, :] = k_ref[...].astype(BF16)
        vb_ref[BLK:, :] = v_ref[...].astype(BF16)


def _window_masks(qi):
    col2 = (qi - 1) * BLK + _iota((BLK, 2 * BLK), 1)
    row2 = qi * BLK + _iota((BLK, 2 * BLK), 0)
    valid2 = jnp.logical_and(col2 < row2, col2 >= PAD)
    row1 = qi * BLK + _iota((BLK, BLK), 0)
    lane = _iota((BLK, BLK), 1)

    def valid1(kblk):
        col = kblk * BLK + lane
        return jnp.logical_and(col < row1, col >= PAD)

    return valid2, valid1


def _strict_upper(n):
    return (_iota((n, n), 0) > _iota((n, n), 1)).astype(BF16)


def _sb_attention_fwd(proj, w_out_shard):
    heads = range(STEP_HEADS)
    n_groups = N_HEADS // STEP_HEADS

    def body(q_ref, k_ref, v_ref, src_ref, o_ref, ox_ref, out_ref, kb_ref, vb_ref, send_sems, recv_sems, loc_sem):
        grp, qi = pl.program_id(0), pl.program_id(1)
        _stage_kv(qi, k_ref, v_ref, kb_ref, vb_ref)

        @pl.when(jnp.logical_and(grp == 0, qi == 0))
        def _():
            for cp in _all_gather_copies(src_ref, out_ref, send_sems, recv_sems, loc_sem):
                cp.start()

        lane = _iota((BLK, BLK), 1)
        qh = []
        for p in range(PAIRS_PER_STEP):
            qh += _head_masks(q_ref[:, p * BLK:(p + 1) * BLK] * SB_SCALE, lane)
        valid2, valid1 = _window_masks(qi)

        def tiles(rows, valid, tri, runs):
            ks = [kb_ref[rows, _pair_lanes(h)] for h in heads]
            vs = [vb_ref[rows, _pair_lanes(h)] for h in heads]
            parts = [_sb_tile(jnp.where(valid, _dot(qh[h], ks[h], NT), SB_MASKED)) for h in heads]
            afters = [_suffix_sum(parts[h][1], tri) for h in heads]
            if runs is not None:
                afters = [afters[h] + runs[h] for h in heads]
            avals = [jnp.exp(parts[h][2] + afters[h]) for h in heads]
            his = [avals[h].astype(BF16) for h in heads]
            accs = [_dot(his[h], vs[h], NN) for h in heads]
            rests = [_dot((avals[h] - his[h].astype(F32)).astype(BF16), vs[h], NN) for h in heads]
            return [(jnp.sum(parts[h][1], axis=1, keepdims=True), accs[h], rests[h]) for h in heads]

        win = pl.ds(pl.multiple_of(qi * BLK, BLK), 2 * BLK)
        init = [t for head in tiles(win, valid2, _strict_upper(2 * BLK), None) for t in head]
        tri1 = _strict_upper(BLK)

        def step(kblk, carry):
            rows = pl.ds(pl.multiple_of((kblk + 1) * BLK, BLK), BLK)
            runs = [carry[3 * h] for h in heads]
            new = []
            for h, (d_run, d_acc, d_rest) in enumerate(tiles(rows, valid1(kblk), tri1, runs)):
                new += [carry[3 * h] + d_run, carry[3 * h + 1] + d_acc, carry[3 * h + 2] + d_rest]
            return tuple(new)

        res = _sweep(qi - 2, step, init, tuple(3 * h for h in heads))
        for p in range(PAIRS_PER_STEP):
            o = jnp.where(lane < HEAD_DIM, res[6 * p + 1], res[6 * p + 4])
            o_ref[:, p * BLK:(p + 1) * BLK] = o
            ox_ref[:, p * BLK:(p + 1) * BLK] = o + jnp.where(lane < HEAD_DIM, res[6 * p + 2], res[6 * p + 5])

        @pl.when(jnp.logical_and(grp == n_groups - 1, qi == NBLK - 1))
        def _():
            for cp in _all_gather_copies(src_ref, out_ref, send_sems, recv_sems, loc_sem):
                cp.wait()

    act = jax.ShapeDtypeStruct((LP, 1024), F32)
    blk = lambda col: pl.BlockSpec((BLK, STEP_W), lambda g, qi: (qi, col + g))
    whole = lambda col: pl.BlockSpec((LP, STEP_W), lambda g, qi: (0, col + g), pipeline_mode=pl.Buffered(1))
    return pl.pallas_call(
        body, name="sb_attn_fwd",
        out_shape=(act, act, jax.ShapeDtypeStruct((N_DEV,) + w_out_shard.shape, w_out_shard.dtype)),
        grid=(n_groups, NBLK),
        in_specs=[blk(COL_Q // STEP_W), whole(COL_K // STEP_W), whole(COL_V // STEP_W), _ANY],
        out_specs=(blk(0), blk(0), _ANY),
        scratch_shapes=[pltpu.VMEM((LP + BLK, STEP_W), BF16), pltpu.VMEM((LP + BLK, STEP_W), BF16)] + _ALL_GATHER_SEMS,
        compiler_params=_params(("arbitrary", "arbitrary")),
    )(proj, proj, proj, w_out_shard)


def _sb_attention_bwd(proj, o_sb, do_sb, chip_sums):
    heads = range(STEP_HEADS)
    n_groups = N_HEADS // STEP_HEADS

    def body(q_ref, k_ref, v_ref, o_ref, do_ref, src_ref, dq_ref, dk_ref, dv_ref, out_ref,
             kb_ref, vb_ref, dka_ref, dva_ref, send_sems, recv_sems, loc_sem):
        grp, qi = pl.program_id(0), pl.program_id(1)
        _stage_kv(qi, k_ref, v_ref, kb_ref, vb_ref)

        @pl.when(jnp.logical_and(grp == 0, qi == 0))
        def _():
            for cp in _chip_exchange_copies(src_ref, out_ref, send_sems, recv_sems, loc_sem):
                cp.start()

        @pl.when(qi == 0)
        def _():
            dka_ref[...] = jnp.zeros_like(dka_ref)
            dva_ref[...] = jnp.zeros_like(dva_ref)

        lane = _iota((BLK, BLK), 1)
        head0 = lane < HEAD_DIM
        qh, doh, dsum = [], [], []
        for p in range(PAIRS_PER_STEP):
            lanes = slice(p * BLK, (p + 1) * BLK)
            qh += _head_masks(q_ref[:, lanes] * SB_SCALE, lane)
            do = do_ref[:, lanes]
            doh += _head_masks(do, lane)
            prod = do.astype(BF16).astype(F32) * o_ref[:, lanes]
            dsum += [jnp.sum(jnp.where(head0, prod, 0.0), axis=1, keepdims=True),
                     jnp.sum(jnp.where(head0, 0.0, prod), axis=1, keepdims=True)]
        valid2, valid1 = _window_masks(qi)

        def tiles(rows, valid, tri, runs, eruns):
            ks = [kb_ref[rows, _pair_lanes(h)] for h in heads]
            vs = [vb_ref[rows, _pair_lanes(h)] for h in heads]
            parts = [_sb_tile(jnp.where(valid, _dot(qh[h], ks[h], NT), SB_MASKED)) for h in heads]
            afters = [_suffix_sum(parts[h][1], tri) for h in heads]
            if runs is not None:
                afters = [afters[h] + runs[h] for h in heads]
            avals = [jnp.exp(parts[h][2] + afters[h]) for h in heads]
            es = [avals[h] * _dot(doh[h], vs[h], NT) for h in heads]
            esufs = [_suffix_sum(es[h], tri) for h in heads]
            if eruns is not None:
                esufs = [esufs[h] + eruns[h] for h in heads]
            dzs = [(es[h] - parts[h][0] * (dsum[h] - esufs[h])).astype(BF16) for h in heads]
            dqs = [_dot(dzs[h], ks[h], NN) for h in heads]
            dks = [_dot(dzs[h], qh[h], TN) for h in heads]
            dvs = [_dot(avals[h].astype(BF16), doh[h], TN) for h in heads]
            for p in range(PAIRS_PER_STEP):
                dka_ref[rows, p * BLK:(p + 1) * BLK] += dks[2 * p] + dks[2 * p + 1]
                dva_ref[rows, p * BLK:(p + 1) * BLK] += dvs[2 * p] + dvs[2 * p + 1]
            return [(jnp.sum(parts[h][1], axis=1, keepdims=True), jnp.sum(es[h], axis=1, keepdims=True), dqs[h])
                    for h in heads]

        win = pl.ds(pl.multiple_of(qi * BLK, BLK), 2 * BLK)
        init = [t for head in tiles(win, valid2, _strict_upper(2 * BLK), None, None) for t in head]
        tri1 = _strict_upper(BLK)

        def step(kblk, carry):
            rows = pl.ds(pl.multiple_of((kblk + 1) * BLK, BLK), BLK)
            runs = [carry[3 * h] for h in heads]
            eruns = [carry[3 * h + 1] for h in heads]
            new = []
            for h, (d_run, d_erun, d_q) in enumerate(tiles(rows, valid1(kblk), tri1, runs, eruns)):
                new += [carry[3 * h] + d_run, carry[3 * h + 1] + d_erun, carry[3 * h + 2] + d_q]
            return tuple(new)

        res = _sweep(qi - 2, step, init, tuple(3 * h for h in heads))
        for p in range(PAIRS_PER_STEP):
            dq = jnp.where(head0, res[6 * p + 2], res[6 * p + 5]) * SB_SCALE
            dq_ref[:, p * BLK:(p + 1) * BLK] = dq.astype(BF16)

        @pl.when(qi == NBLK - 1)
        def _():
            dk_ref[...] = dka_ref[BLK:, :].astype(BF16)
            dv_ref[...] = dva_ref[BLK:, :].astype(BF16)

        @pl.when(jnp.logical_and(grp == n_groups - 1, qi == NBLK - 1))
        def _():
            for cp in _chip_exchange_copies(src_ref, out_ref, send_sems, recv_sems, loc_sem):
                cp.wait()

    act = jax.ShapeDtypeStruct((LP, 1024), BF16)
    blk = lambda col: pl.BlockSpec((BLK, STEP_W), lambda g, qi: (qi, col + g))
    whole = lambda col: pl.BlockSpec((LP, STEP_W), lambda g, qi: (0, col + g))
    once = lambda col: pl.BlockSpec((LP, STEP_W), lambda g, qi: (0, col + g), pipeline_mode=pl.Buffered(1))
    return pl.pallas_call(
        body, name="sb_attn_bwd",
        out_shape=(act, act, act, jax.ShapeDtypeStruct(chip_sums.shape, chip_sums.dtype)),
        grid=(n_groups, NBLK),
        in_specs=[blk(COL_Q // STEP_W), once(COL_K // STEP_W), once(COL_V // STEP_W), blk(0), blk(0), _ANY],
        out_specs=(blk(0), whole(0), whole(0), _ANY),
        scratch_shapes=[pltpu.VMEM((LP + BLK, STEP_W), BF16), pltpu.VMEM((LP + BLK, STEP_W), BF16),
                        pltpu.VMEM((LP + BLK, STEP_W), F32), pltpu.VMEM((LP + BLK, STEP_W), F32)]
        + _CHIP_EXCHANGE_SEMS,
        compiler_params=_params(("arbitrary", "arbitrary"), ATTN_VMEM_LIMIT),
    )(proj, proj, proj, o_sb, do_sb, chip_sums)


def _conv_pre(x, w_ref, b_ref):
    acc = b_ref[...] + w_ref[3:4, :] * x
    for k in range(3):
        acc = acc + w_ref[k:k + 1, :] * pltpu.roll(x, 3 - k, 0)
    return acc


def _conv_fwd(proj, conv_w, conv_b):
    def body(x_ref, w_ref, b_ref, o_ref):
        xc = _conv_pre(x_ref[...], w_ref, b_ref)
        o_ref[...] = xc * _sigmoid(xc)

    nb = XBC_W // BLK
    return pl.pallas_call(
        body, name="conv_fwd", out_shape=jax.ShapeDtypeStruct((LP, XBC_W), F32), grid=(nb,),
        in_specs=[pl.BlockSpec((LP, BLK), lambda j: (0, COL_XBC // BLK + j)),
                  pl.BlockSpec((4, BLK), lambda j: (0, j)), pl.BlockSpec((1, BLK), lambda j: (0, j))],
        out_specs=pl.BlockSpec((LP, BLK), lambda j: (0, j)),
        compiler_params=_params(("parallel",)),
    )(proj, conv_w, conv_b)


def _conv_bwd(dxa, proj, conv_w, conv_b):
    def body(d_ref, x_ref, w_ref, b_ref, dx_ref, dw_ref, db_ref):
        x = x_ref[...]
        xc = _conv_pre(x, w_ref, b_ref)
        s = _sigmoid(xc)
        live = _iota((LP, BLK), 0) >= PAD
        dxc = jnp.where(live, d_ref[...] * (s * (1.0 + xc * (1.0 - s))), 0.0)
        db_ref[...] = jnp.sum(dxc, axis=0, keepdims=True)
        dx = w_ref[3:4, :] * dxc
        dw_ref[3:4, :] = jnp.sum(dxc * x, axis=0, keepdims=True)
        for k in range(3):
            dw_ref[k:k + 1, :] = jnp.sum(dxc * pltpu.roll(x, 3 - k, 0), axis=0, keepdims=True)
            dx = dx + w_ref[k:k + 1, :] * pltpu.roll(dxc, LP - (3 - k), 0)
        dx_ref[...] = dx.astype(BF16)

    nb = XBC_W // BLK
    col = lambda j: (0, j)
    return pl.pallas_call(
        body, name="conv_bwd",
        out_shape=(jax.ShapeDtypeStruct((LP, XBC_W), BF16), jax.ShapeDtypeStruct((4, XBC_W), F32),
                   jax.ShapeDtypeStruct((1, XBC_W), F32)),
        grid=(nb,),
        in_specs=[pl.BlockSpec((LP, BLK), col), pl.BlockSpec((LP, BLK), lambda j: (0, COL_XBC // BLK + j)),
                  pl.BlockSpec((4, BLK), col), pl.BlockSpec((1, BLK), col)],
        out_specs=(pl.BlockSpec((LP, BLK), col), pl.BlockSpec((4, BLK), col), pl.BlockSpec((1, BLK), col)),
        compiler_params=_params(("parallel",)),
    )(dxa, proj, conv_w, conv_b)


def _ssd_prelude(c, dt_ref, dtt_ref, dtb_ref, dtbt_ref, alog_ref, alogt_ref):
    live = jnp.logical_or(c > 0, _iota((BLK, N_HEADS), 0) >= PAD)
    live_t = jnp.logical_or(c > 0, _iota((N_HEADS, BLK), 1) >= PAD)
    pre = dt_ref[...] + dtb_ref[...]
    pre_t = dtt_ref[...] + dtbt_ref[...]
    dt = jnp.where(live, _softplus(pre), 0.0)
    dt_t = jnp.where(live_t, _softplus(pre_t), 0.0)
    a = -jnp.exp(alog_ref[...])
    a_t = -jnp.exp(alogt_ref[...])
    li = _iota((BLK, BLK), 0)
    si = _iota((BLK, BLK), 1)
    lower = (si <= li).astype(BF16)
    upper = (li <= si).astype(BF16)
    acum = _dot_exact_x(lower, dt * a, NN)
    acum_t = _dot_x_exact(dt_t * a_t, upper, NN)
    return live, pre, dt, a, a_t, acum, acum_t


def _head_expand():
    return (_iota((N_HEADS, SSD_W), 1) // HEAD_DIM == _iota((N_HEADS, SSD_W), 0)).astype(BF16)


def _head_reduce_mat():
    return (_iota((SSD_W, N_HEADS), 0) // HEAD_DIM == _iota((SSD_W, N_HEADS), 1)).astype(BF16)


def _decay_mat(acum, acum_t, h, causal):
    seg = jnp.minimum(acum[:, h:h + 1] - acum_t[h:h + 1, :], 0.0)
    return jnp.where(causal, jnp.exp(seg), 0.0)


def _ssd_fwd(xa, dt_raw, dt_raw_t, dt_bias, dt_bias_t, a_log, a_log_t, d_exp):
    def body(x_ref, b_ref, c_ref, dt_ref, dtt_ref, dtb_ref, dtbt_ref, alog_ref, alogt_ref, dexp_ref,
             y_ref, hs_ref, state_ref):
        c = pl.program_id(0)

        @pl.when(c == 0)
        def _():
            state_ref[...] = jnp.zeros_like(state_ref)

        _, _, dt, _, _, acum, acum_t = _ssd_prelude(c, dt_ref, dtt_ref, dtb_ref, dtbt_ref, alog_ref, alogt_ref)
        expand = _head_expand()
        x = x_ref[...]
        xdt = x * _dot_x_exact(dt, expand, n=2)
        exp_a = _dot_x_exact(jnp.exp(acum), expand, n=2)
        to_end = _dot_x_exact(jnp.exp(acum[BLK - 1:BLK, :] - acum), expand, n=2)
        xdt_b = xdt.astype(BF16)
        xd_b = (xdt * to_end).astype(BF16)
        chunk_decay = jnp.exp(acum_t[:, BLK - 1:BLK])
        hs_ref[0] = state_ref[...]
        lane = _iota((BLK, BLK), 1)
        causal = _iota((BLK, BLK), 0) >= lane
        gw = HEADS_PER_GROUP * HEAD_DIM
        for g in range(N_GROUPS):
            bg = b_ref[:, g * N_STATE:(g + 1) * N_STATE].astype(BF16)
            cg = c_ref[:, g * N_STATE:(g + 1) * N_STATE].astype(BF16)
            cb = _dot(cg, bg, NT)
            hg = state_ref[g * gw:(g + 1) * gw, :]
            ch = _dot(cg, hg.astype(BF16), NT)
            st = _dot(xd_b[:, g * gw:(g + 1) * gw], bg, TN)
            for p in range(HEADS_PER_GROUP // 2):
                h0 = g * HEADS_PER_GROUP + 2 * p
                lo = h0 * HEAD_DIM
                xp = xdt_b[:, lo:lo + BLK]
                w0 = (cb * _decay_mat(acum, acum_t, h0, causal)).astype(BF16)
                w1 = (cb * _decay_mat(acum, acum_t, h0 + 1, causal)).astype(BF16)
                yd = jnp.where(lane < HEAD_DIM, _dot(w0, xp), _dot(w1, xp))
                y_ref[:, lo:lo + BLK] = (yd + ch[:, lo - g * gw:lo - g * gw + BLK] * exp_a[:, lo:lo + BLK]
                                         + x[:, lo:lo + BLK] * dexp_ref[:, lo:lo + BLK])
            for r in range(HEADS_PER_GROUP):
                h = g * HEADS_PER_GROUP + r
                state_ref[h * HEAD_DIM:(h + 1) * HEAD_DIM, :] = (
                    hg[r * HEAD_DIM:(r + 1) * HEAD_DIM, :] * chunk_decay[h:h + 1, :]
                    + st[r * HEAD_DIM:(r + 1) * HEAD_DIM, :])

    chunk = lambda width, col=0: pl.BlockSpec((BLK, width), lambda c: (c, col))
    return pl.pallas_call(
        body, name="ssd_fwd",
        out_shape=(jax.ShapeDtypeStruct((LP, SSD_W), F32), jax.ShapeDtypeStruct((NBLK, SSD_W, N_STATE), F32)),
        grid=(NBLK,),
        in_specs=[chunk(SSD_W), chunk(256, 4), chunk(256, 5), chunk(N_HEADS),
                  pl.BlockSpec((N_HEADS, BLK), lambda c: (0, c)), _const_spec((1, N_HEADS)),
                  _const_spec((N_HEADS, 1)), _const_spec((1, N_HEADS)), _const_spec((N_HEADS, 1)),
                  _const_spec((1, SSD_W))],
        out_specs=(chunk(SSD_W), pl.BlockSpec((1, SSD_W, N_STATE), lambda c: (c, 0, 0))),
        scratch_shapes=[pltpu.VMEM((SSD_W, N_STATE), F32)],
        compiler_params=_params(("arbitrary",)),
    )(xa, xa, xa, dt_raw, dt_raw_t, dt_bias, dt_bias_t, a_log, a_log_t, d_exp)


def _ssd_bwd(xa, dt_raw, dt_raw_t, dt_bias, dt_bias_t, a_log, a_log_t, d_exp, hstart, dy):
    def body(x_ref, b_ref, c_ref, dt_ref, dtt_ref, dtb_ref, dtbt_ref, alog_ref, alogt_ref, dexp_ref,
             hs_ref, dy_ref, dxa_ref, ddt_ref, dbias_ref, dalog_ref, dd_ref, dstate_ref):
        step = pl.program_id(0)
        c = NBLK - 1 - step

        @pl.when(step == 0)
        def _():
            dstate_ref[...] = jnp.zeros_like(dstate_ref)
            dbias_ref[...] = jnp.zeros_like(dbias_ref)
            dalog_ref[...] = jnp.zeros_like(dalog_ref)
            dd_ref[...] = jnp.zeros_like(dd_ref)

        live, pre, dt, a, a_t, acum, acum_t = _ssd_prelude(c, dt_ref, dtt_ref, dtb_ref, dtbt_ref,
                                                           alog_ref, alogt_ref)
        expand = _head_expand()
        reduce_m = _head_reduce_mat()
        x = x_ref[...]
        dyv = dy_ref[...]
        dt_e = _dot_x_exact(dt, expand, n=2)
        xdt = x * dt_e
        exp_acum = jnp.exp(acum)
        exp_a = _dot_x_exact(exp_acum, expand, n=2)
        dte = jnp.exp(acum[BLK - 1:BLK, :] - acum)
        to_end = _dot_x_exact(dte, expand, n=2)
        xdt_b = xdt.astype(BF16)
        xd_b = (xdt * to_end).astype(BF16)
        chunk_decay = jnp.exp(acum_t[:, BLK - 1:BLK])
        lane = _iota((BLK, BLK), 1)
        head0 = lane < HEAD_DIM
        causal = _iota((BLK, BLK), 0) >= lane
        gw = HEADS_PER_GROUP * HEAD_DIM
        dm = dyv * exp_a
        dm_b = dm.astype(BF16)
        onehot = lambda h: (_iota((1, N_HEADS), 1) == h).astype(F32)
        onehot_t = lambda h: (_iota((N_HEADS, 1), 0) == h).astype(F32)
        dacum = jnp.zeros((BLK, N_HEADS), F32)
        dacum_t = jnp.zeros((N_HEADS, BLK), F32)
        head_sums = lambda prod, rows: _dot_x_exact(prod, reduce_m[rows, :], n=2)
        dskip_acc = head_sums(dyv * x, slice(0, SSD_W))
        ddt_acc = jnp.zeros((BLK, N_HEADS), F32)
        ddte_acc = jnp.zeros((BLK, N_HEADS), F32)
        dexpa_acc = jnp.zeros((BLK, N_HEADS), F32)
        head_sum = expand
        for g in range(N_GROUPS):
            bg = b_ref[:, g * N_STATE:(g + 1) * N_STATE].astype(BF16)
            cg = c_ref[:, g * N_STATE:(g + 1) * N_STATE].astype(BF16)
            cb = _dot(cg, bg, NT)
            hg = hs_ref[0, g * gw:(g + 1) * gw, :]
            hg_b = hg.astype(BF16)
            dhe = dstate_ref[g * gw:(g + 1) * gw, :]
            dhe_b = dhe.astype(BF16)
            ch = _dot(cg, hg_b, NT)
            dcg = _dot(dm_b[:, g * gw:(g + 1) * gw], hg_b, NN)
            dhs = _dot(dm_b[:, g * gw:(g + 1) * gw], cg, TN)
            dxd = _dot(bg, dhe_b, NT)
            dbg = _dot(xd_b[:, g * gw:(g + 1) * gw], dhe_b, NN)
            dcb = jnp.zeros((BLK, BLK), F32)
            dxdt_g = []
            for p in range(HEADS_PER_GROUP // 2):
                h0 = g * HEADS_PER_GROUP + 2 * p
                lo = h0 * HEAD_DIM
                xp = xdt_b[:, lo:lo + BLK]
                dyp = dyv[:, lo:lo + BLK]
                dyh = (jnp.where(head0, dyp, 0.0).astype(BF16), jnp.where(head0, 0.0, dyp).astype(BF16))
                dxdt_p = jnp.zeros((BLK, BLK), F32)
                for q in range(2):
                    h = h0 + q
                    dec = _decay_mat(acum, acum_t, h, causal)
                    w = cb * dec
                    dw = _dot(dyh[q], xp, NT)
                    t = dw * w
                    dacum = dacum + jnp.sum(t, axis=1, keepdims=True) * onehot(h)
                    dacum_t = dacum_t - jnp.sum(t, axis=0, keepdims=True) * onehot_t(h)
                    dcb = dcb + dw * dec
                    dxdt_p = dxdt_p + _dot(w.astype(BF16), dyh[q], TN)
                sl = slice(lo, lo + BLK)
                gl = slice(lo - g * gw, lo - g * gw + BLK)
                dxdt_p = dxdt_p + dxd[:, gl] * to_end[:, sl]
                dxa_ref[:, sl] = dyp * dexp_ref[:, sl] + dxdt_p * dt_e[:, sl]
                dxdt_g.append(dxdt_p)
            cols = slice(g * gw, (g + 1) * gw)
            ddt_acc = ddt_acc + head_sums(jnp.concatenate(dxdt_g, axis=1) * x[:, cols], cols)
            ddte_acc = ddte_acc + head_sums(dxd * xdt[:, cols], cols)
            dexpa_acc = dexpa_acc + head_sums(dyv[:, cols] * ch, cols)
            dcb_b = dcb.astype(BF16)
            b_col = SSD_W + g * N_STATE
            c_col = SSD_W + (N_GROUPS + g) * N_STATE
            dxa_ref[:, c_col:c_col + N_STATE] = dcg + _dot(dcb_b, bg, NN)
            dxa_ref[:, b_col:b_col + N_STATE] = dbg + _dot(dcb_b, cg, TN)
            prod = dhe * hg
            per_head = jnp.sum(_dot_exact_x(head_sum[:, g * gw:(g + 1) * gw], prod, NN), axis=1, keepdims=True)
            dacum_t = dacum_t + (per_head * chunk_decay) * (_iota((1, BLK), 1) == BLK - 1).astype(F32)
            for r in range(HEADS_PER_GROUP):
                h = g * HEADS_PER_GROUP + r
                rows = slice(h * HEAD_DIM, (h + 1) * HEAD_DIM)
                dstate_ref[rows, :] = (dhs[r * HEAD_DIM:(r + 1) * HEAD_DIM, :]
                                       + dhe[r * HEAD_DIM:(r + 1) * HEAD_DIM, :] * chunk_decay[h:h + 1, :])
        dacum = dacum + dexpa_acc * exp_acum - ddte_acc * dte
        last_row = (_iota((BLK, 1), 0) == BLK - 1).astype(F32)
        dacum = dacum + last_row * jnp.sum(ddte_acc * dte, axis=0, keepdims=True)
        li = _iota((BLK, BLK), 0)
        si = _iota((BLK, BLK), 1)
        upper = (li <= si).astype(BF16)
        lower = (si <= li).astype(BF16)
        dda = _dot_exact_x(upper, dacum, NN)
        dda_t = _dot_x_exact(dacum_t, lower, NN)
        eye = (_iota((N_HEADS, N_HEADS), 0) == _iota((N_HEADS, N_HEADS), 1)).astype(BF16)
        dda = dda + _dot_x_exact_tn(dda_t, eye)
        ddt = ddt_acc + dda * a
        dalog_ref[...] += jnp.sum(dda * dt, axis=0, keepdims=True) * a
        dd_ref[...] += jnp.sum(dskip_acc, axis=0, keepdims=True)
        ddt_raw = jnp.where(live, ddt * _sigmoid(pre), 0.0)
        ddt_ref[...] = jnp.zeros_like(ddt_ref)
        ddt_ref[:, :N_HEADS] = ddt_raw
        dbias_ref[...] += jnp.sum(ddt_raw, axis=0, keepdims=True)

    rev = lambda width, col=0: pl.BlockSpec((BLK, width), lambda s: (NBLK - 1 - s, col))
    vec = jax.ShapeDtypeStruct((1, N_HEADS), F32)
    return pl.pallas_call(
        body, name="ssd_bwd",
        out_shape=(jax.ShapeDtypeStruct((LP, XBC_W), F32), jax.ShapeDtypeStruct((LP, BLK), F32), vec, vec, vec),
        grid=(NBLK,),
        in_specs=[rev(SSD_W), rev(256, 4), rev(256, 5), rev(N_HEADS),
                  pl.BlockSpec((N_HEADS, BLK), lambda s: (0, NBLK - 1 - s)), _const_spec((1, N_HEADS)),
                  _const_spec((N_HEADS, 1)), _const_spec((1, N_HEADS)), _const_spec((N_HEADS, 1)),
                  _const_spec((1, SSD_W)),
                  pl.BlockSpec((1, SSD_W, N_STATE), lambda s: (NBLK - 1 - s, 0, 0)), rev(SSD_W)],
        out_specs=(rev(XBC_W), rev(BLK), _const_spec((1, N_HEADS)),
                   _const_spec((1, N_HEADS)), _const_spec((1, N_HEADS))),
        scratch_shapes=[pltpu.VMEM((SSD_W, N_STATE), F32)],
        compiler_params=_params(("arbitrary",)),
    )(xa, xa, xa, dt_raw, dt_raw_t, dt_bias, dt_bias_t, a_log, a_log_t, d_exp, hstart, dy)


def _dot_x_exact_tn(x_t, eye):
    out = None
    for p in _split(x_t, 3):
        t = _dot(p, eye, TN)
        out = t if out is None else out + t
    return out


def _adamw(name, parts, w, m, v, rows):
    r_all, cols = w.shape
    assert r_all % rows == 0
    c1 = 1.0 / (1.0 - ADAM_B1 ** ADAM_STEP)
    c2 = 1.0 / (1.0 - ADAM_B2 ** ADAM_STEP)

    def body(p_ref, w_ref, m_ref, v_ref, g_ref, d_ref, mo_ref, vo_ref):
        g = p_ref[0].astype(F32)
        for j in range(1, parts.shape[0]):
            g = g + p_ref[j].astype(F32)
        mn = ADAM_B1 * m_ref[...] + (1.0 - ADAM_B1) * g
        vn = ADAM_B2 * v_ref[...] + (1.0 - ADAM_B2) * (g * g)
        g_ref[...] = g
        mo_ref[...] = mn
        vo_ref[...] = vn
        d_ref[...] = -ADAM_LR * ((mn * c1) / (jnp.sqrt(vn * c2) + ADAM_EPS) + ADAM_WD * w_ref[...])

    spec = pl.BlockSpec((rows, cols), lambda i: (i, 0))
    shp = jax.ShapeDtypeStruct((r_all, cols), F32)
    return pl.pallas_call(
        body, name=name, out_shape=(shp, shp, shp, shp), grid=(r_all // rows,),
        in_specs=[pl.BlockSpec((parts.shape[0], rows, cols), lambda i: (0, i, 0)), spec, spec, spec],
        out_specs=(spec, spec, spec, spec),
        compiler_params=_params(("parallel",)),
    )(parts, w, m, v)


_VECTORS = (("norm_w", 1024, 0), ("conv_b", 1536, 8), ("dt_bias", 16, 20), ("a_log", 16, 21), ("d_skip", 16, 22),
            ("sb_norm_w", 1024, 24), ("ssd_norm_w", 1024, 32), ("final_norm_w", 1024, 40))
_LOSS_ROW = 23
_CONVW_ROW = 48
_META_ROW = 96
_PACK_ROWS = 224
_SMALL_ORDER = tuple(name for name, _, _ in _VECTORS) + ("conv_w", "meta_tokens")


def _pack_small_grads(vectors, loss_row, d_convw, dh):
    def body(*refs):
        vec_refs, (loss_ref, cw_ref, dh_ref, out_ref) = refs[:len(_VECTORS)], refs[len(_VECTORS):]
        out_ref[...] = jnp.zeros_like(out_ref)
        for (_, width, row), ref in zip(_VECTORS, vec_refs):
            if width < BLK:
                out_ref[row:row + 1, :width] = ref[...]
            else:
                for t in range(width // BLK):
                    out_ref[row + t:row + t + 1, :] = ref[:, t * BLK:(t + 1) * BLK]
        out_ref[_LOSS_ROW:_LOSS_ROW + 1, :] = loss_ref[...]
        for k in range(4):
            for t in range(XBC_W // BLK):
                r = _CONVW_ROW + k * (XBC_W // BLK) + t
                out_ref[r:r + 1, :] = cw_ref[k:k + 1, t * BLK:(t + 1) * BLK]
        for i in range(N_META):
            for t in range(D_MODEL // BLK):
                r = _META_ROW + i * (D_MODEL // BLK) + t
                out_ref[r:r + 1, :] = dh_ref[i:i + 1, t * BLK:(t + 1) * BLK]

    full = lambda a: pl.BlockSpec(a.shape, lambda i: tuple(0 for _ in a.shape))
    return pl.pallas_call(
        body, name="pack_small_grads", out_shape=jax.ShapeDtypeStruct((_PACK_ROWS, BLK), F32), grid=(1,),
        in_specs=[full(v) for v in vectors] + [full(loss_row), full(d_convw),
                                               pl.BlockSpec((N_META, D_MODEL), lambda i: (PAD // N_META, 0))],
        out_specs=pl.BlockSpec((_PACK_ROWS, BLK), lambda i: (0, 0)),
        compiler_params=_params(("arbitrary",)),
    )(*vectors, loss_row, d_convw, dh)


def _sum_slots(name, parts, rows):
    n, r_all, cols = parts.shape

    def body(p_ref, o_ref):
        acc = p_ref[0].astype(F32)
        for j in range(1, n):
            acc = acc + p_ref[j].astype(F32)
        o_ref[...] = acc

    return pl.pallas_call(
        body, name=name, out_shape=jax.ShapeDtypeStruct((r_all, cols), F32), grid=(r_all // rows,),
        in_specs=[pl.BlockSpec((n, rows, cols), lambda i: (0, i, 0))],
        out_specs=pl.BlockSpec((rows, cols), lambda i: (i, 0)),
        compiler_params=_params(("parallel",)),
    )(parts)


def _adamw_small(pack, weights, moms, vels):
    c1 = 1.0 / (1.0 - ADAM_B1 ** ADAM_STEP)
    c2 = 1.0 / (1.0 - ADAM_B2 ** ADAM_STEP)
    n = len(_SMALL_ORDER)

    def body(*refs):
        p_ref = refs[0]
        w_refs, m_refs, v_refs = refs[1:1 + n], refs[1 + n:1 + 2 * n], refs[1 + 2 * n:1 + 3 * n]
        outs = refs[1 + 3 * n:1 + 7 * n]
        loss_ref, g_ref, cw_ref, cws_ref, mt_ref, all_ref, send_sems, recv_sems, loc_sem = refs[1 + 7 * n:]
        x, y, c = _mesh_position()
        me = 4 * x + 2 * y + c
        copies = _all_gather_copies(p_ref, all_ref, send_sems, recv_sems, loc_sem)
        for cp in copies:
            cp.start()
        for cp in copies:
            cp.wait()
        g = all_ref[0]
        for j in range(1, N_DEV):
            g = g + all_ref[j]
        g_ref[...] = g
        loss_ref[...] = g_ref[_LOSS_ROW:_LOSS_ROW + 1, :]

        def update(idx, grad):
            go_ref, d_ref, mo_ref, vo_ref = outs[4 * idx:4 * idx + 4]
            mn = ADAM_B1 * m_refs[idx][...] + (1.0 - ADAM_B1) * grad
            vn = ADAM_B2 * v_refs[idx][...] + (1.0 - ADAM_B2) * (grad * grad)
            go_ref[...] = grad
            mo_ref[...] = mn
            vo_ref[...] = vn
            d_ref[...] = -ADAM_LR * ((mn * c1) / (jnp.sqrt(vn * c2) + ADAM_EPS) + ADAM_WD * w_refs[idx][...])

        for idx, (_, width, row) in enumerate(_VECTORS):
            go_ref = outs[4 * idx]
            if width < BLK:
                grad = g_ref[row:row + 1, :width]
            else:
                for t in range(width // BLK):
                    go_ref[:, t * BLK:(t + 1) * BLK] = g_ref[row + t:row + t + 1, :]
                grad = go_ref[...]
            update(idx, grad)
        cw_ref[...] = jnp.zeros_like(cw_ref)
        for k in range(4):
            for t in range(XBC_W // BLK):
                r = _CONVW_ROW + k * (XBC_W // BLK) + t
                cw_ref[k:k + 1, t * BLK:(t + 1) * BLK] = g_ref[r:r + 1, :]
        for i in range(N_META):
            for t in range(D_MODEL // BLK):
                r = _META_ROW + i * (D_MODEL // BLK) + t
                mt_ref[i:i + 1, t * BLK:(t + 1) * BLK] = g_ref[r:r + 1, :]
        width_cw = XBC_W // N_DEV
        pick_cw = (_iota((XBC_W, width_cw), 0) == me * width_cw + _iota((XBC_W, width_cw), 1)).astype(BF16)
        cws_ref[...] = _dot_x_exact(cw_ref[...], pick_cw)
        update(n - 2, cws_ref[0:4, :])
        pick_mt = (_iota((D_MODEL, BLK), 0) == me * BLK + _iota((D_MODEL, BLK), 1)).astype(BF16)
        update(n - 1, _dot_x_exact(mt_ref[...], pick_mt))

    full = lambda a: pl.BlockSpec(a.shape, lambda i: tuple(0 for _ in a.shape))
    params = list(weights) + list(moms) + list(vels)
    out_shape, out_specs = [], []
    for w in weights:
        for _ in range(4):
            out_shape.append(jax.ShapeDtypeStruct(w.shape, F32))
            out_specs.append(full(w))
    out_shape.append(jax.ShapeDtypeStruct((1, BLK), F32))
    out_specs.append(pl.BlockSpec((1, BLK), lambda i: (0, 0)))
    return pl.pallas_call(
        body, name="adamw_small", out_shape=tuple(out_shape), grid=(1,),
        in_specs=[full(pack)] + [full(a) for a in params], out_specs=tuple(out_specs),
        scratch_shapes=[pltpu.VMEM((_PACK_ROWS, BLK), F32), pltpu.VMEM((8, XBC_W), F32),
                        pltpu.VMEM((8, XBC_W // N_DEV), F32), pltpu.VMEM((N_META, D_MODEL), F32),
                        pltpu.VMEM((N_DEV, _PACK_ROWS, BLK), F32)] + _ALL_GATHER_SEMS,
        compiler_params=_params(("arbitrary",)),
    )(pack, *params)


def kernel(x, meta_tokens, norm_w, w_in, conv_w, conv_b, dt_bias, a_log, d_skip, sb_norm_w, ssd_norm_w, w_out, final_norm_w, loss_target, m_meta_tokens, m_norm_w, m_w_in, m_conv_w, m_conv_b, m_dt_bias, m_a_log, m_d_skip, m_sb_norm_w, m_ssd_norm_w, m_w_out, m_final_norm_w, v_meta_tokens, v_norm_w, v_w_in, v_conv_w, v_conv_b, v_dt_bias, v_a_log, v_d_skip, v_sb_norm_w, v_ssd_norm_w, v_w_out, v_final_norm_w):
    small_src = jnp.concatenate([conv_w[0].reshape(6, BLK), meta_tokens, jnp.zeros((2, BLK), F32)], axis=0)
    small_g, w_in_g = _gather_weights([small_src, w_in[0].astype(BF16)])
    w_in_full = w_in_g.transpose(1, 0, 2).reshape(D_MODEL, D_IN)
    w_dt = w_in_full[:, D_MAIN:]
    conv_w_full = small_g[:, :6].reshape(N_DEV, 4, 192).transpose(1, 0, 2).reshape(4, XBC_W)
    meta_full = small_g[:, 6:6 + N_META].transpose(1, 0, 2).reshape(N_META, D_MODEL)
    h_pad = jnp.concatenate([jnp.zeros((PAD, D_MODEL), F32), meta_full, x[0]], axis=0)
    dt_bias_t = dt_bias.reshape(N_HEADS, 1)
    a_log_t = a_log.reshape(N_HEADS, 1)
    d_exp = jnp.repeat(d_skip, HEAD_DIM, axis=1)
    fnw = final_norm_w.reshape(1, D_MODEL)

    u, u_t, dt_raw, dt_raw_t = _prenorm(h_pad, norm_w, w_dt, w_dt.T)
    proj = _matmul("in_proj", u, w_in_full, "nn", LP, 512, D_MODEL, n_cols=D_MAIN)
    o_sb, o_sb_exact, w_out_g = _sb_attention_fwd(proj, w_out[0].astype(BF16))
    w_out_full = w_out_g.reshape(2 * SSD_W, D_MODEL)
    xa = _conv_fwd(proj, conv_w_full, conv_b)
    o_ssd, hstart = _ssd_fwd(xa, dt_raw, dt_raw_t, dt_bias, dt_bias_t, a_log, a_log_t, d_exp)
    ycat, dh2, dh2_b, loss_row, d_fnw = _out_head(
        o_sb, proj, o_ssd, sb_norm_w, ssd_norm_w, w_out_full, h_pad, fnw, loss_target[0])

    g_w_out = _matmul("d_w_out", ycat, dh2_b, "tn", 512, 512, LP, BF16).reshape(4, 2, 256, D_MODEL)
    do_sb, dg, do_ssd, dz, d_sbw, d_ssdw, sib_w_out = _ycat_bwd(
        dh2_b, w_out_full, o_sb, proj, o_ssd, sb_norm_w, ssd_norm_w, g_w_out)
    chip_w_out = _pair_sum("pair_sum_w_out", g_w_out, sib_w_out, 256)
    dq, dk, dv, p_w_out = _sb_attention_bwd(proj, o_sb_exact, do_sb, chip_w_out)
    dxa, ddt_raw, d_dtb, d_alog, d_dskip = _ssd_bwd(
        xa, dt_raw, dt_raw_t, dt_bias, dt_bias_t, a_log, a_log_t, d_exp, hstart, do_ssd)
    dxbc, d_convw, d_convb = _conv_bwd(dxa, proj, conv_w_full, conv_b)
    pieces = [dq, dk, dv, dg, dz, dxbc, ddt_raw]
    g_w_in = _d_w_in(u_t, pieces).reshape(1, 2, D_MODEL // 2, W_IN_PAD)
    chip_w_in = _pair_sum("pair_sum_w_in", g_w_in, _swap_with_sibling("swap_w_in", g_w_in), 128)
    dh, d_nw, win_parts = _d_u_prenorm_bwd(pieces, w_in_full, w_dt, h_pad, norm_w, dh2, chip_w_in[0])
    win_mine = _sum_slots("sum_w_in_windows", win_parts, 128)
    win_other = _swap_with_sibling("swap_w_in_window", win_mine, whole=True)
    core = lax.axis_index("c")
    chip = 2 * lax.axis_index("x") + lax.axis_index("y")
    first_col = (D_IN // N_DEV) * (2 * chip + core) - WIN_STRIDE * chip
    cut = lambda w: lax.dynamic_slice(w, (0, first_col), (D_MODEL // 2, D_IN // N_DEV))
    half_mine, half_other = cut(win_mine), cut(win_other)
    p_w_in = jnp.concatenate([jnp.where(core == 0, half_mine, half_other),
                              jnp.where(core == 0, half_other, half_mine)], axis=0)[None]

    pack = _pack_small_grads([d_nw, d_convb, d_dtb, d_alog, d_dskip, d_sbw, d_ssdw, d_fnw], loss_row, d_convw, dh)

    res_in = _adamw("adamw_w_in", p_w_in, w_in[0], m_w_in[0], v_w_in[0], 128)
    res_out = _adamw("adamw_w_out", p_w_out, w_out[0], m_w_out[0], v_w_out[0], 128)
    res_small = _adamw_small(
        pack,
        [norm_w, conv_b, dt_bias, a_log, d_skip, sb_norm_w, ssd_norm_w, fnw, conv_w[0], meta_tokens],
        [m_norm_w, m_conv_b, m_dt_bias, m_a_log, m_d_skip, m_sb_norm_w, m_ssd_norm_w,
         m_final_norm_w.reshape(1, D_MODEL), m_conv_w[0], m_meta_tokens],
        [v_norm_w, v_conv_b, v_dt_bias, v_a_log, v_d_skip, v_sb_norm_w, v_ssd_norm_w,
         v_final_norm_w.reshape(1, D_MODEL), v_conv_w[0], v_meta_tokens])

    loss = jnp.sum(res_small[-1])
    order = ["meta_tokens", "norm_w", "w_in", "conv_w", "conv_b", "dt_bias", "a_log", "d_skip",
             "sb_norm_w", "ssd_norm_w", "w_out", "final_norm_w"]
    outs = [loss, dh[BLK:].reshape(1, SEQ, D_MODEL)]
    for kind in range(4):
        small = {name: res_small[4 * idx + kind] for idx, name in enumerate(_SMALL_ORDER)}
        small["final_norm_w"] = small["final_norm_w"].reshape(D_MODEL)
        small["conv_w"] = small["conv_w"].reshape(1, 4, XBC_W // N_DEV)
        small["w_in"] = res_in[kind].reshape(1, D_MODEL, D_IN // N_DEV)
        small["w_out"] = res_out[kind].reshape(1, 256, D_MODEL)
        outs += [small[name] for name in order]
    return tuple(outs)
```

```python
import jax
import jax.numpy as jnp
from jax import lax
from jax.experimental import pallas as pl
from jax.experimental.pallas import tpu as pltpu

F32 = jnp.float32
BF16 = jnp.bfloat16

D_MODEL = 1024
SEQ = 2048
N_META = 16
BLK = 128
PAD = BLK - N_META
LP = PAD + N_META + SEQ
NBLK = LP // BLK
N_HEADS = 16
HEAD_DIM = 64
N_GROUPS = 2
HEADS_PER_GROUP = 8
N_STATE = 128
SSD_W = 1024
XBC_W = 1536
D_MAIN = 6656
D_IN = 6672
COL_Q, COL_K, COL_V, COL_G, COL_Z, COL_XBC = 0, 1024, 2048, 3072, 4096, 5120
N_DEV = 8
EPS = 1e-5
SB_SCALE = 0.125
SB_DEAD = -87.4
SB_MASKED = -1e30

ADAM_LR = 0.001
ADAM_B1 = 0.9
ADAM_B2 = 0.999
ADAM_EPS = 1e-08
ADAM_WD = 0.01
ADAM_STEP = 10

VMEM_LIMIT = 48 * 1024 * 1024
BIG_VMEM_LIMIT = 56 * 1024 * 1024

NN = (((1,), (0,)), ((), ()))
NT = (((1,), (1,)), ((), ()))
TN = (((0,), (0,)), ((), ()))


def _dot(a, b, dims=NN):
    return lax.dot_general(a, b, dims, preferred_element_type=F32)


def _split(x, n):
    parts = []
    r = x
    for i in range(n):
        p = r.astype(BF16)
        parts.append(p)
        if i + 1 < n:
            r = r - p.astype(F32)
    return parts


def _dot_x_exact(x, m, dims=NN, n=3):
    out = None
    for p in _split(x, n):
        t = _dot(p, m, dims)
        out = t if out is None else out + t
    return out


def _dot_exact_x(m, x, dims=NN, n=3):
    out = None
    for p in _split(x, n):
        t = _dot(m, p, dims)
        out = t if out is None else out + t
    return out


def _iota(shape, dim):
    return lax.broadcasted_iota(jnp.int32, shape, dim)


def _softplus(x):
    return jnp.maximum(x, 0.0) + jnp.log(1.0 + jnp.exp(-jnp.abs(x)))


def _sigmoid(x):
    return 1.0 / (1.0 + jnp.exp(-x))


def _params(sem=None, vmem=None):
    return pltpu.CompilerParams(dimension_semantics=sem, vmem_limit_bytes=vmem or VMEM_LIMIT)


_ANY = pl.BlockSpec(memory_space=pl.ANY)
_MESH = pl.DeviceIdType.MESH


def _mesh_position():
    return lax.axis_index("x"), lax.axis_index("y"), lax.axis_index("c")


def _other_chips(x, y):
    return [(1 - x, y), (x, 1 - y), (1 - x, 1 - y)]


def _gather_weights(srcs):
    n = len(srcs)

    def body(*refs):
        src, out = refs[:n], refs[n:2 * n]
        send_sems, recv_sems, loc_sems = refs[2 * n:]
        x, y, c = _mesh_position()
        sibling = (x, y, 1 - c)
        chips = _other_chips(x, y)
        relay_from = (jnp.where(c == 0, 1 - x, x), jnp.where(c == 0, y, 1 - y))
        relay_to = (jnp.where(c == 0, x, 1 - x), jnp.where(c == 0, 1 - y, y))

        def copy(a, k, block, to, from_src=False):
            slot = out[a].at[4 * block[0] + 2 * block[1] + block[2]]
            return pltpu.make_async_remote_copy(
                src_ref=src[a] if from_src else slot, dst_ref=slot,
                send_sem=send_sems.at[7 * a + k], recv_sem=recv_sems.at[7 * a + k],
                device_id=to, device_id_type=_MESH)

        local, sends = [], []
        for a in range(n):
            mine = pltpu.make_async_copy(src[a], out[a].at[4 * x + 2 * y + c], loc_sems.at[a])
            mine.start()
            local.append(mine)
            first = [copy(a, 0, (x, y, c), sibling, True)]
            first += [copy(a, 1 + j, (x, y, c), (*chip, c), True) for j, chip in enumerate(chips[:2])]
            for cp in first:
                cp.start()
            sends += first
        for a in range(n):
            for j, chip in enumerate(chips[:2]):
                copy(a, 1 + j, (*chip, c), (x, y, c)).wait_recv()
            later = [copy(a, 3, (*relay_from, c), (*relay_to, c))]
            later += [copy(a, 4 + j, (*chip, c), sibling) for j, chip in enumerate(chips[:2])]
            for cp in later:
                cp.start()
            sends += later
        for a in range(n):
            copy(a, 3, (*chips[2], c), (x, y, c)).wait_recv()
            passed = copy(a, 6, (*chips[2], c), sibling)
            passed.start()
            sends.append(passed)
        for a in range(n):
            copy(a, 0, (x, y, 1 - c), (x, y, c)).wait_recv()
            for j, chip in enumerate(chips):
                copy(a, 4 + j, (*chip, 1 - c), (x, y, c)).wait_recv()
        for cp in sends:
            cp.wait_send()
        for cp in local:
            cp.wait()

    return pl.pallas_call(
        body, name="gather_weights",
        out_shape=tuple(jax.ShapeDtypeStruct((N_DEV,) + s.shape, s.dtype) for s in srcs),
        in_specs=[_ANY] * n, out_specs=tuple([_ANY] * n),
        scratch_shapes=[pltpu.SemaphoreType.DMA((7 * n,)), pltpu.SemaphoreType.DMA((7 * n,)),
                        pltpu.SemaphoreType.DMA((n,))],
    )(*srcs)


_ALL_GATHER_SEMS = [pltpu.SemaphoreType.DMA((7,)), pltpu.SemaphoreType.DMA((7,)), pltpu.SemaphoreType.DMA]


def _all_gather_copies(src, out, send_sems, recv_sems, loc_sem):
    x, y, c = _mesh_position()
    me = 4 * x + 2 * y + c
    copies = [pltpu.make_async_copy(src, out.at[me], loc_sem)]
    for k in range(1, N_DEV):
        peer = (1 - x if k & 4 else x, 1 - y if k & 2 else y, 1 - c if k & 1 else c)
        copies.append(pltpu.make_async_remote_copy(
            src_ref=src, dst_ref=out.at[me], send_sem=send_sems.at[k - 1], recv_sem=recv_sems.at[k - 1],
            device_id=peer, device_id_type=_MESH))
    return copies


def _sibling_swap_copies(src, out, send_sems, recv_sems, n, whole=False):
    x, y, c = _mesh_position()
    return [pltpu.make_async_remote_copy(
        src_ref=src if whole else src.at[k, 1 - c], dst_ref=out if whole else out.at[k],
        send_sem=send_sems.at[k], recv_sem=recv_sems.at[k], device_id=(x, y, 1 - c), device_id_type=_MESH)
        for k in range(n)]


def _swap_with_sibling(name, src, whole=False):
    n = 1 if whole else src.shape[0]

    def body(src_ref, out_ref, send_sems, recv_sems):
        copies = _sibling_swap_copies(src_ref, out_ref, send_sems, recv_sems, n, whole)
        for cp in copies:
            cp.start()
        for cp in copies:
            cp.wait()

    shape = src.shape if whole else (src.shape[0],) + src.shape[2:]
    return pl.pallas_call(
        body, name=name, out_shape=jax.ShapeDtypeStruct(shape, src.dtype), in_specs=[_ANY], out_specs=_ANY,
        scratch_shapes=[pltpu.SemaphoreType.DMA((n,)), pltpu.SemaphoreType.DMA((n,))],
    )(src)


def _pair_sum(name, g, sib, rows):
    n, _, r_all, cols = g.shape
    assert r_all % rows == 0

    def body(g0_ref, g1_ref, s_ref, o_ref):
        c = lax.axis_index("c")
        mine = jnp.where(c == 0, g0_ref[0, 0].astype(F32), g1_ref[0, 0].astype(F32))
        o_ref[0] = (mine + s_ref[0].astype(F32)).astype(o_ref.dtype)

    return pl.pallas_call(
        body, name=name, out_shape=jax.ShapeDtypeStruct((n, r_all, cols), BF16), grid=(n, r_all // rows),
        in_specs=[pl.BlockSpec((1, 1, rows, cols), lambda k, i: (k, 0, i, 0)),
                  pl.BlockSpec((1, 1, rows, cols), lambda k, i: (k, 1, i, 0)),
                  pl.BlockSpec((1, rows, cols), lambda k, i: (k, i, 0))],
        out_specs=pl.BlockSpec((1, rows, cols), lambda k, i: (k, i, 0)),
        compiler_params=_params(("parallel", "parallel")),
    )(g, g, sib)


_CHIP_EXCHANGE_SEMS = [pltpu.SemaphoreType.DMA((3,)), pltpu.SemaphoreType.DMA((3,)), pltpu.SemaphoreType.DMA]


def _chip_exchange_copies(src, out, send_sems, recv_sems, loc_sem, window=None):
    x, y, c = _mesh_position()
    here = 2 * x + y

    def slot(k):
        if window is not None:
            return src.at[:, pl.ds(pl.multiple_of(k * window[0], BLK), window[1])]
        return src.at[k]

    copies = [pltpu.make_async_copy(slot(here), out.at[here], loc_sem)]
    for j, chip in enumerate(_other_chips(x, y)):
        copies.append(pltpu.make_async_remote_copy(
            src_ref=slot(2 * chip[0] + chip[1]), dst_ref=out.at[here],
            send_sem=send_sems.at[j], recv_sem=recv_sems.at[j], device_id=(*chip, c), device_id_type=_MESH))
    return copies


def _matmul(name, a, b, kind, tm, tn, tk, out_dtype=F32, n_cols=None):
    if kind == "nn":
        (m, kk), nn_ = a.shape, (n_cols or b.shape[1])
        a_spec = pl.BlockSpec((tm, tk), lambda i, j, k: (i, k))
        b_spec = pl.BlockSpec((tk, tn), lambda i, j, k: (k, j))
        dims = NN
    else:
        (kk, m), (_, nn_) = a.shape, b.shape
        a_spec = pl.BlockSpec((tk, tm), lambda i, j, k: (k, i))
        b_spec = pl.BlockSpec((tk, tn), lambda i, j, k: (k, j))
        dims = TN
    assert m % tm == 0 and nn_ % tn == 0 and kk % tk == 0
    nk = kk // tk

    def body(a_ref, b_ref, o_ref, acc_ref):
        k = pl.program_id(2)
        part = _dot(a_ref[...], b_ref[...], dims)
        if nk == 1:
            o_ref[...] = part.astype(o_ref.dtype)
        else:
            @pl.when(k == 0)
            def _():
                acc_ref[...] = part

            @pl.when(k > 0)
            def _():
                acc_ref[...] += part

            @pl.when(k == nk - 1)
            def _():
                o_ref[...] = acc_ref[...].astype(o_ref.dtype)

    return pl.pallas_call(
        body, name=name, out_shape=jax.ShapeDtypeStruct((m, nn_), out_dtype),
        grid=(m // tm, nn_ // tn, nk),
        in_specs=[a_spec, b_spec], out_specs=pl.BlockSpec((tm, tn), lambda i, j, k: (i, j)),
        scratch_shapes=[pltpu.VMEM((tm, tn) if nk > 1 else (8, 128), F32)],
        compiler_params=_params(("parallel", "parallel", "arbitrary")),
    )(a, b)


def _row_spec(width, col=0):
    return pl.BlockSpec((BLK, width), lambda i: (i, col))


def _const_spec(shape):
    return pl.BlockSpec(shape, lambda i: tuple(0 for _ in shape))


def _prenorm(h_pad, norm_w, wdt, wdt_t):
    def body(h_ref, w_ref, wdt_ref, wdtt_ref, u_ref, ut_ref, dt_ref, dtt_ref):
        xv = h_ref[...]
        r = lax.rsqrt(jnp.mean(xv * xv, axis=-1, keepdims=True) + EPS)
        uf = xv * r * w_ref[...]
        u = uf.astype(BF16)
        u_ref[...] = u
        ut_ref[...] = uf.T.astype(BF16)
        dt_ref[...] = _dot(u, wdt_ref[...], NN)
        dtt_ref[...] = _dot(wdtt_ref[...], u, NT)

    return pl.pallas_call(
        body, name="prenorm",
        out_shape=(jax.ShapeDtypeStruct((LP, D_MODEL), BF16), jax.ShapeDtypeStruct((D_MODEL, LP), BF16),
                   jax.ShapeDtypeStruct((LP, N_HEADS), F32), jax.ShapeDtypeStruct((N_HEADS, LP), F32)),
        grid=(NBLK,),
        in_specs=[_row_spec(D_MODEL), _const_spec((1, D_MODEL)), _const_spec((D_MODEL, N_HEADS)),
                  _const_spec((N_HEADS, D_MODEL))],
        out_specs=(_row_spec(D_MODEL), pl.BlockSpec((D_MODEL, BLK), lambda i: (0, i)), _row_spec(N_HEADS),
                   pl.BlockSpec((N_HEADS, BLK), lambda i: (0, i))),
        compiler_params=_params(("parallel",)),
    )(h_pad, norm_w, wdt, wdt_t)


def _gated_norm(o, g, w):
    a = o * (g * _sigmoid(g))
    r = lax.rsqrt(jnp.mean(a * a, axis=-1, keepdims=True) + EPS)
    return a * r * w


def _out_head(o_sb, proj, o_ssd, sb_w, ssd_w, w_out_full, h_pad, fnw, target):
    tm = LP // 8

    def body(osb_ref, g_ref, ossd_ref, z_ref, sbw_ref, ssdw_ref, wout_ref, h_ref, w_ref, t_hbm,
             y_ref, dh2_ref, dh2b_ref, loss_ref, dw_ref, t_ref, t_sem):
        i = pl.program_id(0)

        @pl.when(i == 0)
        def _():
            loss_ref[...] = jnp.zeros_like(loss_ref)
            dw_ref[...] = jnp.zeros_like(dw_ref)
            t_ref[:BLK, :] = jnp.zeros((BLK, D_MODEL), F32)
            first = pltpu.make_async_copy(t_hbm.at[pl.ds(0, tm - BLK)], t_ref.at[pl.ds(BLK, tm - BLK)], t_sem)
            first.start()
            first.wait()

        @pl.when(i > 0)
        def _():
            rest = pltpu.make_async_copy(t_hbm.at[pl.ds(pl.multiple_of(i * tm - BLK, 8), tm)], t_ref, t_sem)
            rest.start()
            rest.wait()

        y_ref[:, :SSD_W] = _gated_norm(osb_ref[...], g_ref[...], sbw_ref[...]).astype(BF16)
        y_ref[:, SSD_W:] = _gated_norm(ossd_ref[...], z_ref[...], ssdw_ref[...]).astype(BF16)
        h2 = h_ref[...] + _dot(y_ref[...], wout_ref[...], NN)
        r = lax.rsqrt(jnp.mean(h2 * h2, axis=-1, keepdims=True) + EPS)
        nrm = h2 * r
        w = w_ref[...]
        live = i * tm + _iota((tm, D_MODEL), 0) >= BLK
        err = jnp.where(live, nrm * w - t_ref[...], 0.0)
        dout = err * (1.0 / D_MODEL)
        loss_ref[...] += (0.5 / D_MODEL) * _fold_lanes(jnp.sum(err * err, axis=0, keepdims=True))
        dw_ref[...] += jnp.sum(dout * nrm, axis=0, keepdims=True)
        wd = dout * w
        dh2 = r * (wd - nrm * jnp.mean(wd * nrm, axis=-1, keepdims=True))
        dh2_ref[...] = dh2
        dh2b_ref[...] = dh2.astype(BF16)

    rows = lambda width, col=0: pl.BlockSpec((tm, width), lambda i: (i, col))
    return pl.pallas_call(
        body, name="out_head",
        out_shape=(jax.ShapeDtypeStruct((LP, 2 * SSD_W), BF16), jax.ShapeDtypeStruct((LP, D_MODEL), F32),
                   jax.ShapeDtypeStruct((LP, D_MODEL), BF16), jax.ShapeDtypeStruct((1, BLK), F32),
                   jax.ShapeDtypeStruct((1, D_MODEL), F32)),
        grid=(LP // tm,),
        in_specs=[rows(1024), rows(1024, COL_G // 1024), rows(1024), rows(1024, COL_Z // 1024),
                  _const_spec((1, 1024)), _const_spec((1, 1024)),
                  pl.BlockSpec((2 * SSD_W, D_MODEL), lambda i: (0, 0), pipeline_mode=pl.Buffered(1)),
                  rows(D_MODEL), _const_spec((1, D_MODEL)), _ANY],
        out_specs=(rows(2 * SSD_W), rows(D_MODEL), rows(D_MODEL), _const_spec((1, BLK)), _const_spec((1, D_MODEL))),
        scratch_shapes=[pltpu.VMEM((tm, D_MODEL), F32), pltpu.SemaphoreType.DMA],
        compiler_params=_params(("arbitrary",)),
    )(o_sb, proj, o_ssd, proj, sb_w, ssd_w, w_out_full, h_pad, fnw, target)


def _fold_lanes(row):
    out = row[:, :BLK]
    for j in range(1, row.shape[1] // BLK):
        out = out + row[:, j * BLK:(j + 1) * BLK]
    return out


def _gated_norm_bwd(dy, o, g, w):
    s = _sigmoid(g)
    sg = g * s
    a = o * sg
    r = lax.rsqrt(jnp.mean(a * a, axis=-1, keepdims=True) + EPS)
    nrm = a * r
    dw = jnp.sum(dy * nrm, axis=0, keepdims=True)
    wd = dy * w
    da = r * (wd - nrm * jnp.mean(wd * nrm, axis=-1, keepdims=True))
    return da * sg, da * o * (s * (1.0 + g * (1.0 - s))), dw


def _ycat_bwd(dh2_b, w_out_full, o_sb, proj, o_ssd, sb_w, ssd_w, g_w_out):
    tm = LP // 8
    n_slots = g_w_out.shape[0]

    def body(a_ref, w_ref, osb_ref, g_ref, ossd_ref, z_ref, sbw_ref, ssdw_ref, src_ref,
             dosb_ref, dg_ref, dossd_ref, dz_ref, dsbw_ref, dssdw_ref, sib_ref, send_sems, recv_sems):
        i = pl.program_id(0)

        @pl.when(i == 0)
        def _():
            for cp in _sibling_swap_copies(src_ref, sib_ref, send_sems, recv_sems, n_slots):
                cp.start()
            dsbw_ref[...] = jnp.zeros_like(dsbw_ref)
            dssdw_ref[...] = jnp.zeros_like(dssdw_ref)

        dy = _dot(a_ref[...], w_ref[...], NT)
        do, dg, dw = _gated_norm_bwd(dy[:, :SSD_W], osb_ref[...], g_ref[...], sbw_ref[...])
        dosb_ref[...] = do
        dg_ref[...] = dg.astype(BF16)
        dsbw_ref[...] += dw
        do, dg, dw = _gated_norm_bwd(dy[:, SSD_W:], ossd_ref[...], z_ref[...], ssdw_ref[...])
        dossd_ref[...] = do
        dz_ref[...] = dg.astype(BF16)
        dssdw_ref[...] += dw

        @pl.when(i == LP // tm - 1)
        def _():
            for cp in _sibling_swap_copies(src_ref, sib_ref, send_sems, recv_sems, n_slots):
                cp.wait()

    act = jax.ShapeDtypeStruct((LP, 1024), F32)
    gate = jax.ShapeDtypeStruct((LP, 1024), BF16)
    vec = jax.ShapeDtypeStruct((1, 1024), F32)
    rows = lambda col=0: pl.BlockSpec((tm, 1024), lambda i: (i, col))
    return pl.pallas_call(
        body, name="ycat_bwd",
        out_shape=(act, gate, act, gate, vec, vec,
                   jax.ShapeDtypeStruct((n_slots,) + g_w_out.shape[2:], g_w_out.dtype)),
        grid=(LP // tm,),
        in_specs=[rows(), _const_spec((2 * SSD_W, D_MODEL)), rows(), rows(COL_G // 1024), rows(),
                  rows(COL_Z // 1024), _const_spec((1, 1024)), _const_spec((1, 1024)), _ANY],
        out_specs=(rows(), rows(), rows(), rows(), _const_spec((1, 1024)), _const_spec((1, 1024)), _ANY),
        scratch_shapes=[pltpu.SemaphoreType.DMA((n_slots,)), pltpu.SemaphoreType.DMA((n_slots,))],
        compiler_params=_params(("arbitrary",)),
    )(dh2_b, w_out_full, o_sb, proj, o_ssd, proj, sb_w, ssd_w, g_w_out)


_DPROJ_PIECES = (("dq", COL_Q, 1024), ("dk", COL_K, 1024), ("dv", COL_V, 1024), ("dg", COL_G, 1024),
                 ("dz", COL_Z, 1024), ("dxbc", COL_XBC, XBC_W), ("ddt", D_MAIN, BLK))
W_IN_PAD = D_MAIN + 512
WIN_STRIDE = 13 * BLK
WIN_WIDTH = 14 * BLK


def _d_w_in(u_t, pieces):
    tn = 512
    nj = W_IN_PAD // tn
    half = D_MODEL // 2
    main = _DPROJ_PIECES[:-1]

    def body(*refs):
        u_ref, piece_refs = refs[0], refs[1:1 + len(main)]
        ddt_ref, o_ref, sib_ref, stage_ref, send_sems, recv_sems = refs[1 + len(main):]
        j = pl.program_id(1)
        x, y, c = _mesh_position()
        a = u_ref[...]

        def copy(blk):
            return pltpu.make_async_remote_copy(
                src_ref=stage_ref.at[blk], dst_ref=sib_ref.at[:, pl.ds(pl.multiple_of(blk * tn, tn), tn)],
                send_sem=send_sems.at[blk], recv_sem=recv_sems.at[blk],
                device_id=(x, y, 1 - c), device_id_type=_MESH)

        def emit(res):
            top, bottom = res[:half], res[half:]
            o_ref[...] = jnp.where(c == 0, top, bottom)
            stage_ref[j] = jnp.where(c == 0, bottom, top)
            copy(j).start()

        for (_, col, width), ref in zip(main, piece_refs):
            @pl.when(jnp.logical_and(j >= col // tn, j < (col + width) // tn))
            def _():
                emit(_dot(a, ref[...], NN).astype(BF16))

        @pl.when(j == nj - 1)
        def _():
            tail = _dot(a, ddt_ref[...].astype(BF16), NN).astype(BF16)
            emit(jnp.concatenate([tail, jnp.zeros((D_MODEL, tn - BLK), BF16)], axis=1))
            for blk in range(nj):
                copy(blk).wait()

    def piece_spec(col, width):
        return pl.BlockSpec((LP, tn), lambda i, j: (0, jnp.clip(j - col // tn, 0, width // tn - 1)))

    shape = jax.ShapeDtypeStruct((half, W_IN_PAD), BF16)
    return pl.pallas_call(
        body, name="d_w_in", out_shape=(shape, shape), grid=(1, nj),
        in_specs=[pl.BlockSpec((D_MODEL, LP), lambda i, j: (0, 0), pipeline_mode=pl.Buffered(1))]
        + [piece_spec(col, width) for _, col, width in main]
        + [pl.BlockSpec((LP, BLK), lambda i, j: (0, 0))],
        out_specs=(pl.BlockSpec((half, tn), lambda i, j: (0, j)), _ANY),
        scratch_shapes=[pltpu.VMEM((nj, half, tn), BF16), pltpu.SemaphoreType.DMA((nj,)),
                        pltpu.SemaphoreType.DMA((nj,))],
        compiler_params=_params(("arbitrary", "arbitrary"), BIG_VMEM_LIMIT),
    )(u_t, *pieces)


def _add_halves(name, mine, sib, rows):
    r_all, cols = mine.shape

    def body(a_ref, b_ref, o_ref):
        o_ref[...] = (a_ref[...].astype(F32) + b_ref[...].astype(F32)).astype(BF16)

    spec = pl.BlockSpec((rows, cols), lambda i: (i, 0))
    return pl.pallas_call(
        body, name=name, out_shape=jax.ShapeDtypeStruct((r_all, cols), BF16), grid=(r_all // rows,),
        in_specs=[spec, spec], out_specs=spec, compiler_params=_params(("parallel",)),
    )(mine, sib)


def _d_u_prenorm_bwd(pieces, w_main, wdt, h_pad, norm_w, dh2, chip_sum):
    tm, tk = LP // 4, 512
    nk = D_MAIN // tk
    main = _DPROJ_PIECES[:-1]
    window = (WIN_STRIDE, WIN_WIDTH)

    def body(*refs):
        piece_refs = refs[:len(main)]
        (b_ref, ddt_ref, wdt_ref, h_ref, w_ref, dh2_ref, src_ref, dh_ref, dw_ref, out_ref,
         acc_ref, send_sems, recv_sems, loc_sem) = refs[len(main):]
        i, k = pl.program_id(0), pl.program_id(1)

        @pl.when(jnp.logical_and(i == 0, k == 0))
        def _():
            for cp in _chip_exchange_copies(src_ref, out_ref, send_sems, recv_sems, loc_sem, window=window):
                cp.start()
            dw_ref[...] = jnp.zeros_like(dw_ref)

        @pl.when(k == 0)
        def _():
            acc_ref[...] = jnp.zeros_like(acc_ref)

        for (_, col, width), ref in zip(main, piece_refs):
            for lo in range(0, width, tk):
                @pl.when(k == (col + lo) // tk)
                def _():
                    acc_ref[...] += _dot(ref[:, lo:lo + tk], b_ref[...], NT)

        @pl.when(k == nk - 1)
        def _():
            dut = acc_ref[...] + _dot(ddt_ref[:, :N_HEADS].astype(BF16), wdt_ref[...], NT)
            xv = h_ref[...]
            r = lax.rsqrt(jnp.mean(xv * xv, axis=-1, keepdims=True) + EPS)
            nrm = xv * r
            dw_ref[...] += jnp.sum(dut * nrm, axis=0, keepdims=True)
            wd = dut * w_ref[...]
            dh_ref[...] = dh2_ref[...] + r * (wd - nrm * jnp.mean(wd * nrm, axis=-1, keepdims=True))

        @pl.when(jnp.logical_and(i == LP // tm - 1, k == nk - 1))
        def _():
            for cp in _chip_exchange_copies(src_ref, out_ref, send_sems, recv_sems, loc_sem, window=window):
                cp.wait()

    rows = lambda width: pl.BlockSpec((tm, width), lambda i, k: (i, 0))
    const = lambda shape: pl.BlockSpec(shape, lambda i, k: (0, 0))
    return pl.pallas_call(
        body, name="d_u_prenorm_bwd",
        out_shape=(jax.ShapeDtypeStruct((LP, D_MODEL), F32), jax.ShapeDtypeStruct((1, D_MODEL), F32),
                   jax.ShapeDtypeStruct((4, chip_sum.shape[0], WIN_WIDTH), chip_sum.dtype)),
        grid=(LP // tm, nk),
        in_specs=[rows(width) for _, _, width in main]
        + [pl.BlockSpec((D_MODEL, tk), lambda i, k: (0, k)), rows(BLK), const((D_MODEL, N_HEADS)), rows(D_MODEL),
           const((1, D_MODEL)), rows(D_MODEL), _ANY],
        out_specs=(rows(D_MODEL), const((1, D_MODEL)), _ANY),
        scratch_shapes=[pltpu.VMEM((tm, D_MODEL), F32)] + _CHIP_EXCHANGE_SEMS,
        compiler_params=_params(("arbitrary", "arbitrary")),
    )(*pieces[:-1], w_main, pieces[-1], wdt, h_pad, norm_w, dh2, chip_sum)


def _suffix_sum(vals, tri):
    return _dot_x_exact(vals, tri, NN, n=2)


def _sb_tile(z):
    t = jnp.exp(-jnp.abs(z))
    inv = 1.0 / (1.0 + t)
    sp = jnp.maximum(z, 0.0) + jnp.log(1.0 + t)
    sig = jnp.where(z >= 0, inv, t * inv)
    return sig, -sp, z - sp


def _sweep(first, step, init, run_slots):
    def alive_of(state):
        top = state[run_slots[0]]
        for s in run_slots[1:]:
            top = jnp.maximum(top, state[s])
        return (jnp.max(top) > SB_DEAD).astype(jnp.int32)

    def cond(carry):
        return jnp.logical_and(carry[0] >= 0, carry[1] > 0)

    def body(carry):
        state = step(carry[0], tuple(carry[2:]))
        return (carry[0] - 1, alive_of(state)) + tuple(state)

    return lax.while_loop(cond, body, (first, alive_of(init)) + tuple(init))[2:]


def _head_masks(x, lane):
    head0 = lane < HEAD_DIM
    return [jnp.where(head0, x, 0.0).astype(BF16), jnp.where(head0, 0.0, x).astype(BF16)]


PAIRS_PER_STEP = 4
STEP_W = PAIRS_PER_STEP * BLK
STEP_HEADS = 2 * PAIRS_PER_STEP


def _pair_lanes(h):
    lo = (h // 2) * BLK
    return slice(lo, lo + BLK)


def _stage_kv(qi, k_ref, v_ref, kb_ref, vb_ref):
    @pl.when(qi == 0)
    def _():
        kb_ref[:BLK, :] = jnp.zeros((BLK, STEP_W), BF16)
        vb_ref[:BLK, :] = jnp.zeros((BLK, STEP_W), BF16)
        kb_ref[BLK:, :] = k_ref[...].astype(BF16)
        vb_ref[BLK:, :] = v_ref[...].astype(BF16)


def _window_masks(qi):
    col2 = (qi - 1) * BLK + _iota((BLK, 2 * BLK), 1)
    row2 = qi * BLK + _iota((BLK, 2 * BLK), 0)
    valid2 = jnp.logical_and(col2 < row2, col2 >= PAD)
    row1 = qi * BLK + _iota((BLK, BLK), 0)
    lane = _iota((BLK, BLK), 1)

    def valid1(kblk):
        col = kblk * BLK + lane
        return jnp.logical_and(col < row1, col >= PAD)

    return valid2, valid1


def _strict_upper(n):
    return (_iota((n, n), 0) > _iota((n, n), 1)).astype(BF16)


def _sb_attention_fwd(proj, w_out_shard):
    heads = range(STEP_HEADS)
    n_groups = N_HEADS // STEP_HEADS

    def body(q_ref, k_ref, v_ref, src_ref, o_ref, ox_ref, out_ref, kb_ref, vb_ref, send_sems, recv_sems, loc_sem):
        grp, qi = pl.program_id(0), pl.program_id(1)
        _stage_kv(qi, k_ref, v_ref, kb_ref, vb_ref)

        @pl.when(jnp.logical_and(grp == 0, qi == 0))
        def _():
            for cp in _all_gather_copies(src_ref, out_ref, send_sems, recv_sems, loc_sem):
                cp.start()

        lane = _iota((BLK, BLK), 1)
        qh = []
        for p in range(PAIRS_PER_STEP):
            qh += _head_masks(q_ref[:, p * BLK:(p + 1) * BLK] * SB_SCALE, lane)
        valid2, valid1 = _window_masks(qi)

        def tiles(rows, valid, tri, runs):
            ks = [kb_ref[rows, _pair_lanes(h)] for h in heads]
            vs = [vb_ref[rows, _pair_lanes(h)] for h in heads]
            parts = [_sb_tile(jnp.where(valid, _dot(qh[h], ks[h], NT), SB_MASKED)) for h in heads]
            afters = [_suffix_sum(parts[h][1], tri) for h in heads]
            if runs is not None:
                afters = [afters[h] + runs[h] for h in heads]
            avals = [jnp.exp(parts[h][2] + afters[h]) for h in heads]
            his = [avals[h].astype(BF16) for h in heads]
            accs = [_dot(his[h], vs[h], NN) for h in heads]
            rests = [_dot((avals[h] - his[h].astype(F32)).astype(BF16), vs[h], NN) for h in heads]
            return [(jnp.sum(parts[h][1], axis=1, keepdims=True), accs[h], rests[h]) for h in heads]

        win = pl.ds(pl.multiple_of(qi * BLK, BLK), 2 * BLK)
        init = [t for head in tiles(win, valid2, _strict_upper(2 * BLK), None) for t in head]
        tri1 = _strict_upper(BLK)

        def step(kblk, carry):
            rows = pl.ds(pl.multiple_of((kblk + 1) * BLK, BLK), BLK)
            runs = [carry[3 * h] for h in heads]
            new = []
            for h, (d_run, d_acc, d_rest) in enumerate(tiles(rows, valid1(kblk), tri1, runs)):
                new += [carry[3 * h] + d_run, carry[3 * h + 1] + d_acc, carry[3 * h + 2] + d_rest]
            return tuple(new)

        res = _sweep(qi - 2, step, init, tuple(3 * h for h in heads))
        for p in range(PAIRS_PER_STEP):
            o = jnp.where(lane < HEAD_DIM, res[6 * p + 1], res[6 * p + 4])
            o_ref[:, p * BLK:(p + 1) * BLK] = o
            ox_ref[:, p * BLK:(p + 1) * BLK] = o + jnp.where(lane < HEAD_DIM, res[6 * p + 2], res[6 * p + 5])

        @pl.when(jnp.logical_and(grp == n_groups - 1, qi == NBLK - 1))
        def _():
            for cp in _all_gather_copies(src_ref, out_ref, send_sems, recv_sems, loc_sem):
                cp.wait()

    act = jax.ShapeDtypeStruct((LP, 1024), F32)
    blk = lambda col: pl.BlockSpec((BLK, STEP_W), lambda g, qi: (qi, col + g))
    whole = lambda col: pl.BlockSpec((LP, STEP_W), lambda g, qi: (0, col + g), pipeline_mode=pl.Buffered(1))
    return pl.pallas_call(
        body, name="sb_attn_fwd",
        out_shape=(act, act, jax.ShapeDtypeStruct((N_DEV,) + w_out_shard.shape, w_out_shard.dtype)),
        grid=(n_groups, NBLK),
        in_specs=[blk(COL_Q // STEP_W), whole(COL_K // STEP_W), whole(COL_V // STEP_W), _ANY],
        out_specs=(blk(0), blk(0), _ANY),
        scratch_shapes=[pltpu.VMEM((LP + BLK, STEP_W), BF16), pltpu.VMEM((LP + BLK, STEP_W), BF16)] + _ALL_GATHER_SEMS,
        compiler_params=_params(("arbitrary", "arbitrary")),
    )(proj, proj, proj, w_out_shard)


def _sb_attention_bwd(proj, o_sb, do_sb, chip_sums):
    heads = range(STEP_HEADS)
    n_groups = N_HEADS // STEP_HEADS

    def body(q_ref, k_ref, v_ref, o_ref, do_ref, src_ref, dq_ref, dk_ref, dv_ref, out_ref,
             kb_ref, vb_ref, dka_ref, dva_ref, send_sems, recv_sems, loc_sem):
        grp, qi = pl.program_id(0), pl.program_id(1)
        _stage_kv(qi, k_ref, v_ref, kb_ref, vb_ref)

        @pl.when(jnp.logical_and(grp == 0, qi == 0))
        def _():
            for cp in _chip_exchange_copies(src_ref, out_ref, send_sems, recv_sems, loc_sem):
                cp.start()

        @pl.when(qi == 0)
        def _():
            dka_ref[...] = jnp.zeros_like(dka_ref)
            dva_ref[...] = jnp.zeros_like(dva_ref)

        lane = _iota((BLK, BLK), 1)
        head0 = lane < HEAD_DIM
        qh, doh, dsum = [], [], []
        for p in range(PAIRS_PER_STEP):
            lanes = slice(p * BLK, (p + 1) * BLK)
            qh += _head_masks(q_ref[:, lanes] * SB_SCALE, lane)
            do = do_ref[:, lanes]
            doh += _head_masks(do, lane)
            prod = do.astype(BF16).astype(F32) * o_ref[:, lanes]
            dsum += [jnp.sum(jnp.where(head0, prod, 0.0), axis=1, keepdims=True),
                     jnp.sum(jnp.where(head0, 0.0, prod), axis=1, keepdims=True)]
        valid2, valid1 = _window_masks(qi)

        def tiles(rows, valid, tri, runs, eruns):
            ks = [kb_ref[rows, _pair_lanes(h)] for h in heads]
            vs = [vb_ref[rows, _pair_lanes(h)] for h in heads]
            parts = [_sb_tile(jnp.where(valid, _dot(qh[h], ks[h], NT), SB_MASKED)) for h in heads]
            afters = [_suffix_sum(parts[h][1], tri) for h in heads]
            if runs is not None:
                afters = [afters[h] + runs[h] for h in heads]
            avals = [jnp.exp(parts[h][2] + afters[h]) for h in heads]
            es = [avals[h] * _dot(doh[h], vs[h], NT) for h in heads]
            esufs = [_suffix_sum(es[h], tri) for h in heads]
            if eruns is not None:
                esufs = [esufs[h] + eruns[h] for h in heads]
            dzs = [(es[h] - parts[h][0] * (dsum[h] - esufs[h])).astype(BF16) for h in heads]
            dqs = [_dot(dzs[h], ks[h], NN) for h in heads]
            dks = [_dot(dzs[h], qh[h], TN) for h in heads]
            dvs = [_dot(avals[h].astype(BF16), doh[h], TN) for h in heads]
            for p in range(PAIRS_PER_STEP):
                dka_ref[rows, p * BLK:(p + 1) * BLK] += dks[2 * p] + dks[2 * p + 1]
                dva_ref[rows, p * BLK:(p + 1) * BLK] += dvs[2 * p] + dvs[2 * p + 1]
            return [(jnp.sum(parts[h][1], axis=1, keepdims=True), jnp.sum(es[h], axis=1, keepdims=True), dqs[h])
                    for h in heads]

        win = pl.ds(pl.multiple_of(qi * BLK, BLK), 2 * BLK)
        init = [t for head in tiles(win, valid2, _strict_upper(2 * BLK), None, None) for t in head]
        tri1 = _strict_upper(BLK)

        def step(kblk, carry):
            rows = pl.ds(pl.multiple_of((kblk + 1) * BLK, BLK), BLK)
            runs = [carry[3 * h] for h in heads]
            eruns = [carry[3 * h + 1] for h in heads]
            new = []
            for h, (d_run, d_erun, d_q) in enumerate(tiles(rows, valid1(kblk), tri1, runs, eruns)):
                new += [carry[3 * h] + d_run, carry[3 * h + 1] + d_erun, carry[3 * h + 2] + d_q]
            return tuple(new)

        res = _sweep(qi - 2, step, init, tuple(3 * h for h in heads))
        for p in range(PAIRS_PER_STEP):
            dq = jnp.where(head0, res[6 * p + 2], res[6 * p + 5]) * SB_SCALE
            dq_ref[:, p * BLK:(p + 1) * BLK] = dq.astype(BF16)

        @pl.when(qi == NBLK - 1)
        def _():
            dk_ref[...] = dka_ref[BLK:, :].astype(BF16)
            dv_ref[...] = dva_ref[BLK:, :].astype(BF16)

        @pl.when(jnp.logical_and(grp == n_groups - 1, qi == NBLK - 1))
        def _():
            for cp in _chip_exchange_copies(src_ref, out_ref, send_sems, recv_sems, loc_sem):
                cp.wait()

    act = jax.ShapeDtypeStruct((LP, 1024), BF16)
    blk = lambda col: pl.BlockSpec((BLK, STEP_W), lambda g, qi: (qi, col + g))
    whole = lambda col: pl.BlockSpec((LP, STEP_W), lambda g, qi: (0, col + g))
    once = lambda col: pl.BlockSpec((LP, STEP_W), lambda g, qi: (0, col + g), pipeline_mode=pl.Buffered(1))
    return pl.pallas_call(
        body, name="sb_attn_bwd",
        out_shape=(act, act, act, jax.ShapeDtypeStruct(chip_sums.shape, chip_sums.dtype)),
        grid=(n_groups, NBLK),
        in_specs=[blk(COL_Q // STEP_W), once(COL_K // STEP_W), once(COL_V // STEP_W), blk(0), blk(0), _ANY],
        out_specs=(blk(0), whole(0), whole(0), _ANY),
        scratch_shapes=[pltpu.VMEM((LP + BLK, STEP_W), BF16), pltpu.VMEM((LP + BLK, STEP_W), BF16),
                        pltpu.VMEM((LP + BLK, STEP_W), F32), pltpu.VMEM((LP + BLK, STEP_W), F32)]
        + _CHIP_EXCHANGE_SEMS,
        compiler_params=_params(("arbitrary", "arbitrary"), BIG_VMEM_LIMIT),
    )(proj, proj, proj, o_sb, do_sb, chip_sums)


def _conv_pre(x, w_ref, b_ref):
    acc = b_ref[...] + w_ref[3:4, :] * x
    for k in range(3):
        acc = acc + w_ref[k:k + 1, :] * pltpu.roll(x, 3 - k, 0)
    return acc


def _conv_fwd(proj, conv_w, conv_b):
    def body(x_ref, w_ref, b_ref, o_ref):
        xc = _conv_pre(x_ref[...], w_ref, b_ref)
        o_ref[...] = xc * _sigmoid(xc)

    nb = XBC_W // BLK
    return pl.pallas_call(
        body, name="conv_fwd", out_shape=jax.ShapeDtypeStruct((LP, XBC_W), F32), grid=(nb,),
        in_specs=[pl.BlockSpec((LP, BLK), lambda j: (0, COL_XBC // BLK + j)),
                  pl.BlockSpec((4, BLK), lambda j: (0, j)), pl.BlockSpec((1, BLK), lambda j: (0, j))],
        out_specs=pl.BlockSpec((LP, BLK), lambda j: (0, j)),
        compiler_params=_params(("parallel",)),
    )(proj, conv_w, conv_b)


def _conv_bwd(dxa, proj, conv_w, conv_b):
    def body(d_ref, x_ref, w_ref, b_ref, dx_ref, dw_ref, db_ref):
        x = x_ref[...]
        xc = _conv_pre(x, w_ref, b_ref)
        s = _sigmoid(xc)
        live = _iota((LP, BLK), 0) >= PAD
        dxc = jnp.where(live, d_ref[...] * (s * (1.0 + xc * (1.0 - s))), 0.0)
        db_ref[...] = jnp.sum(dxc, axis=0, keepdims=True)
        dx = w_ref[3:4, :] * dxc
        dw_ref[3:4, :] = jnp.sum(dxc * x, axis=0, keepdims=True)
        for k in range(3):
            dw_ref[k:k + 1, :] = jnp.sum(dxc * pltpu.roll(x, 3 - k, 0), axis=0, keepdims=True)
            dx = dx + w_ref[k:k + 1, :] * pltpu.roll(dxc, LP - (3 - k), 0)
        dx_ref[...] = dx.astype(BF16)

    nb = XBC_W // BLK
    col = lambda j: (0, j)
    return pl.pallas_call(
        body, name="conv_bwd",
        out_shape=(jax.ShapeDtypeStruct((LP, XBC_W), BF16), jax.ShapeDtypeStruct((4, XBC_W), F32),
                   jax.ShapeDtypeStruct((1, XBC_W), F32)),
        grid=(nb,),
        in_specs=[pl.BlockSpec((LP, BLK), col), pl.BlockSpec((LP, BLK), lambda j: (0, COL_XBC // BLK + j)),
                  pl.BlockSpec((4, BLK), col), pl.BlockSpec((1, BLK), col)],
        out_specs=(pl.BlockSpec((LP, BLK), col), pl.BlockSpec((4, BLK), col), pl.BlockSpec((1, BLK), col)),
        compiler_params=_params(("parallel",)),
    )(dxa, proj, conv_w, conv_b)


def _ssd_prelude(c, dt_ref, dtt_ref, dtb_ref, dtbt_ref, alog_ref, alogt_ref):
    live = jnp.logical_or(c > 0, _iota((BLK, N_HEADS), 0) >= PAD)
    live_t = jnp.logical_or(c > 0, _iota((N_HEADS, BLK), 1) >= PAD)
    pre = dt_ref[...] + dtb_ref[...]
    pre_t = dtt_ref[...] + dtbt_ref[...]
    dt = jnp.where(live, _softplus(pre), 0.0)
    dt_t = jnp.where(live_t, _softplus(pre_t), 0.0)
    a = -jnp.exp(alog_ref[...])
    a_t = -jnp.exp(alogt_ref[...])
    li = _iota((BLK, BLK), 0)
    si = _iota((BLK, BLK), 1)
    lower = (si <= li).astype(BF16)
    upper = (li <= si).astype(BF16)
    acum = _dot_exact_x(lower, dt * a, NN)
    acum_t = _dot_x_exact(dt_t * a_t, upper, NN)
    return live, pre, dt, a, a_t, acum, acum_t


def _head_expand():
    return (_iota((N_HEADS, SSD_W), 1) // HEAD_DIM == _iota((N_HEADS, SSD_W), 0)).astype(BF16)


def _head_reduce_mat():
    return (_iota((SSD_W, N_HEADS), 0) // HEAD_DIM == _iota((SSD_W, N_HEADS), 1)).astype(BF16)


def _decay_mat(acum, acum_t, h, causal):
    seg = jnp.minimum(acum[:, h:h + 1] - acum_t[h:h + 1, :], 0.0)
    return jnp.where(causal, jnp.exp(seg), 0.0)


def _ssd_fwd(xa, dt_raw, dt_raw_t, dt_bias, dt_bias_t, a_log, a_log_t, d_exp):
    def body(x_ref, b_ref, c_ref, dt_ref, dtt_ref, dtb_ref, dtbt_ref, alog_ref, alogt_ref, dexp_ref,
             y_ref, hs_ref, state_ref):
        c = pl.program_id(0)

        @pl.when(c == 0)
        def _():
            state_ref[...] = jnp.zeros_like(state_ref)

        _, _, dt, _, _, acum, acum_t = _ssd_prelude(c, dt_ref, dtt_ref, dtb_ref, dtbt_ref, alog_ref, alogt_ref)
        expand = _head_expand()
        x = x_ref[...]
        xdt = x * _dot_x_exact(dt, expand, n=2)
        exp_a = _dot_x_exact(jnp.exp(acum), expand, n=2)
        to_end = _dot_x_exact(jnp.exp(acum[BLK - 1:BLK, :] - acum), expand, n=2)
        xdt_b = xdt.astype(BF16)
        xd_b = (xdt * to_end).astype(BF16)
        chunk_decay = jnp.exp(acum_t[:, BLK - 1:BLK])
        hs_ref[0] = state_ref[...]
        lane = _iota((BLK, BLK), 1)
        causal = _iota((BLK, BLK), 0) >= lane
        gw = HEADS_PER_GROUP * HEAD_DIM
        for g in range(N_GROUPS):
            bg = b_ref[:, g * N_STATE:(g + 1) * N_STATE].astype(BF16)
            cg = c_ref[:, g * N_STATE:(g + 1) * N_STATE].astype(BF16)
            cb = _dot(cg, bg, NT)
            hg = state_ref[g * gw:(g + 1) * gw, :]
            ch = _dot(cg, hg.astype(BF16), NT)
            st = _dot(xd_b[:, g * gw:(g + 1) * gw], bg, TN)
            for p in range(HEADS_PER_GROUP // 2):
                h0 = g * HEADS_PER_GROUP + 2 * p
                lo = h0 * HEAD_DIM
                xp = xdt_b[:, lo:lo + BLK]
                w0 = (cb * _decay_mat(acum, acum_t, h0, causal)).astype(BF16)
                w1 = (cb * _decay_mat(acum, acum_t, h0 + 1, causal)).astype(BF16)
                yd = jnp.where(lane < HEAD_DIM, _dot(w0, xp), _dot(w1, xp))
                y_ref[:, lo:lo + BLK] = (yd + ch[:, lo - g * gw:lo - g * gw + BLK] * exp_a[:, lo:lo + BLK]
                                         + x[:, lo:lo + BLK] * dexp_ref[:, lo:lo + BLK])
            for r in range(HEADS_PER_GROUP):
                h = g * HEADS_PER_GROUP + r
                state_ref[h * HEAD_DIM:(h + 1) * HEAD_DIM, :] = (
                    hg[r * HEAD_DIM:(r + 1) * HEAD_DIM, :] * chunk_decay[h:h + 1, :]
                    + st[r * HEAD_DIM:(r + 1) * HEAD_DIM, :])

    chunk = lambda width, col=0: pl.BlockSpec((BLK, width), lambda c: (c, col))
    return pl.pallas_call(
        body, name="ssd_fwd",
        out_shape=(jax.ShapeDtypeStruct((LP, SSD_W), F32), jax.ShapeDtypeStruct((NBLK, SSD_W, N_STATE), F32)),
        grid=(NBLK,),
        in_specs=[chunk(SSD_W), chunk(256, 4), chunk(256, 5), chunk(N_HEADS),
                  pl.BlockSpec((N_HEADS, BLK), lambda c: (0, c)), _const_spec((1, N_HEADS)),
                  _const_spec((N_HEADS, 1)), _const_spec((1, N_HEADS)), _const_spec((N_HEADS, 1)),
                  _const_spec((1, SSD_W))],
        out_specs=(chunk(SSD_W), pl.BlockSpec((1, SSD_W, N_STATE), lambda c: (c, 0, 0))),
        scratch_shapes=[pltpu.VMEM((SSD_W, N_STATE), F32)],
        compiler_params=_params(("arbitrary",)),
    )(xa, xa, xa, dt_raw, dt_raw_t, dt_bias, dt_bias_t, a_log, a_log_t, d_exp)


def _ssd_bwd(xa, dt_raw, dt_raw_t, dt_bias, dt_bias_t, a_log, a_log_t, d_exp, hstart, dy):
    def body(x_ref, b_ref, c_ref, dt_ref, dtt_ref, dtb_ref, dtbt_ref, alog_ref, alogt_ref, dexp_ref,
             hs_ref, dy_ref, dxa_ref, ddt_ref, dbias_ref, dalog_ref, dd_ref, dstate_ref):
        step = pl.program_id(0)
        c = NBLK - 1 - step

        @pl.when(step == 0)
        def _():
            dstate_ref[...] = jnp.zeros_like(dstate_ref)
            dbias_ref[...] = jnp.zeros_like(dbias_ref)
            dalog_ref[...] = jnp.zeros_like(dalog_ref)
            dd_ref[...] = jnp.zeros_like(dd_ref)

        live, pre, dt, a, a_t, acum, acum_t = _ssd_prelude(c, dt_ref, dtt_ref, dtb_ref, dtbt_ref,
                                                           alog_ref, alogt_ref)
        expand = _head_expand()
        reduce_m = _head_reduce_mat()
        x = x_ref[...]
        dyv = dy_ref[...]
        dt_e = _dot_x_exact(dt, expand, n=2)
        xdt = x * dt_e
        exp_acum = jnp.exp(acum)
        exp_a = _dot_x_exact(exp_acum, expand, n=2)
        dte = jnp.exp(acum[BLK - 1:BLK, :] - acum)
        to_end = _dot_x_exact(dte, expand, n=2)
        xdt_b = xdt.astype(BF16)
        xd_b = (xdt * to_end).astype(BF16)
        chunk_decay = jnp.exp(acum_t[:, BLK - 1:BLK])
        lane = _iota((BLK, BLK), 1)
        head0 = lane < HEAD_DIM
        causal = _iota((BLK, BLK), 0) >= lane
        gw = HEADS_PER_GROUP * HEAD_DIM
        dm = dyv * exp_a
        dm_b = dm.astype(BF16)
        onehot = lambda h: (_iota((1, N_HEADS), 1) == h).astype(F32)
        onehot_t = lambda h: (_iota((N_HEADS, 1), 0) == h).astype(F32)
        dacum = jnp.zeros((BLK, N_HEADS), F32)
        dacum_t = jnp.zeros((N_HEADS, BLK), F32)
        head_sums = lambda prod, rows: _dot_x_exact(prod, reduce_m[rows, :], n=2)
        dskip_acc = head_sums(dyv * x, slice(0, SSD_W))
        ddt_acc = jnp.zeros((BLK, N_HEADS), F32)
        ddte_acc = jnp.zeros((BLK, N_HEADS), F32)
        dexpa_acc = jnp.zeros((BLK, N_HEADS), F32)
        head_sum = expand
        for g in range(N_GROUPS):
            bg = b_ref[:, g * N_STATE:(g + 1) * N_STATE].astype(BF16)
            cg = c_ref[:, g * N_STATE:(g + 1) * N_STATE].astype(BF16)
            cb = _dot(cg, bg, NT)
            hg = hs_ref[0, g * gw:(g + 1) * gw, :]
            hg_b = hg.astype(BF16)
            dhe = dstate_ref[g * gw:(g + 1) * gw, :]
            dhe_b = dhe.astype(BF16)
            ch = _dot(cg, hg_b, NT)
            dcg = _dot(dm_b[:, g * gw:(g + 1) * gw], hg_b, NN)
            dhs = _dot(dm_b[:, g * gw:(g + 1) * gw], cg, TN)
            dxd = _dot(bg, dhe_b, NT)
            dbg = _dot(xd_b[:, g * gw:(g + 1) * gw], dhe_b, NN)
            dcb = jnp.zeros((BLK, BLK), F32)
            dxdt_g = []
            for p in range(HEADS_PER_GROUP // 2):
                h0 = g * HEADS_PER_GROUP + 2 * p
                lo = h0 * HEAD_DIM
                xp = xdt_b[:, lo:lo + BLK]
                dyp = dyv[:, lo:lo + BLK]
                dyh = (jnp.where(head0, dyp, 0.0).astype(BF16), jnp.where(head0, 0.0, dyp).astype(BF16))
                dxdt_p = jnp.zeros((BLK, BLK), F32)
                for q in range(2):
                    h = h0 + q
                    dec = _decay_mat(acum, acum_t, h, causal)
                    w = cb * dec
                    dw = _dot(dyh[q], xp, NT)
                    t = dw * w
                    dacum = dacum + jnp.sum(t, axis=1, keepdims=True) * onehot(h)
                    dacum_t = dacum_t - jnp.sum(t, axis=0, keepdims=True) * onehot_t(h)
                    dcb = dcb + dw * dec
                    dxdt_p = dxdt_p + _dot(w.astype(BF16), dyh[q], TN)
                sl = slice(lo, lo + BLK)
                gl = slice(lo - g * gw, lo - g * gw + BLK)
                dxdt_p = dxdt_p + dxd[:, gl] * to_end[:, sl]
                dxa_ref[:, sl] = dyp * dexp_ref[:, sl] + dxdt_p * dt_e[:, sl]
                dxdt_g.append(dxdt_p)
            cols = slice(g * gw, (g + 1) * gw)
            ddt_acc = ddt_acc + head_sums(jnp.concatenate(dxdt_g, axis=1) * x[:, cols], cols)
            ddte_acc = ddte_acc + head_sums(dxd * xdt[:, cols], cols)
            dexpa_acc = dexpa_acc + head_sums(dyv[:, cols] * ch, cols)
            dcb_b = dcb.astype(BF16)
            b_col = SSD_W + g * N_STATE
            c_col = SSD_W + (N_GROUPS + g) * N_STATE
            dxa_ref[:, c_col:c_col + N_STATE] = dcg + _dot(dcb_b, bg, NN)
            dxa_ref[:, b_col:b_col + N_STATE] = dbg + _dot(dcb_b, cg, TN)
            prod = dhe * hg
            per_head = jnp.sum(_dot_exact_x(head_sum[:, g * gw:(g + 1) * gw], prod, NN), axis=1, keepdims=True)
            dacum_t = dacum_t + (per_head * chunk_decay) * (_iota((1, BLK), 1) == BLK - 1).astype(F32)
            for r in range(HEADS_PER_GROUP):
                h = g * HEADS_PER_GROUP + r
                rows = slice(h * HEAD_DIM, (h + 1) * HEAD_DIM)
                dstate_ref[rows, :] = (dhs[r * HEAD_DIM:(r + 1) * HEAD_DIM, :]
                                       + dhe[r * HEAD_DIM:(r + 1) * HEAD_DIM, :] * chunk_decay[h:h + 1, :])
        dacum = dacum + dexpa_acc * exp_acum - ddte_acc * dte
        last_row = (_iota((BLK, 1), 0) == BLK - 1).astype(F32)
        dacum = dacum + last_row * jnp.sum(ddte_acc * dte, axis=0, keepdims=True)
        li = _iota((BLK, BLK), 0)
        si = _iota((BLK, BLK), 1)
        upper = (li <= si).astype(BF16)
        lower = (si <= li).astype(BF16)
        dda = _dot_exact_x(upper, dacum, NN)
        dda_t = _dot_x_exact(dacum_t, lower, NN)
        eye = (_iota((N_HEADS, N_HEADS), 0) == _iota((N_HEADS, N_HEADS), 1)).astype(BF16)
        dda = dda + _dot_x_exact_tn(dda_t, eye)
        ddt = ddt_acc + dda * a
        dalog_ref[...] += jnp.sum(dda * dt, axis=0, keepdims=True) * a
        dd_ref[...] += jnp.sum(dskip_acc, axis=0, keepdims=True)
        ddt_raw = jnp.where(live, ddt * _sigmoid(pre), 0.0)
        ddt_ref[...] = jnp.zeros_like(ddt_ref)
        ddt_ref[:, :N_HEADS] = ddt_raw
        dbias_ref[...] += jnp.sum(ddt_raw, axis=0, keepdims=True)

    rev = lambda width, col=0: pl.BlockSpec((BLK, width), lambda s: (NBLK - 1 - s, col))
    vec = jax.ShapeDtypeStruct((1, N_HEADS), F32)
    return pl.pallas_call(
        body, name="ssd_bwd",
        out_shape=(jax.ShapeDtypeStruct((LP, XBC_W), F32), jax.ShapeDtypeStruct((LP, BLK), F32), vec, vec, vec),
        grid=(NBLK,),
        in_specs=[rev(SSD_W), rev(256, 4), rev(256, 5), rev(N_HEADS),
                  pl.BlockSpec((N_HEADS, BLK), lambda s: (0, NBLK - 1 - s)), _const_spec((1, N_HEADS)),
                  _const_spec((N_HEADS, 1)), _const_spec((1, N_HEADS)), _const_spec((N_HEADS, 1)),
                  _const_spec((1, SSD_W)),
                  pl.BlockSpec((1, SSD_W, N_STATE), lambda s: (NBLK - 1 - s, 0, 0)), rev(SSD_W)],
        out_specs=(rev(XBC_W), rev(BLK), _const_spec((1, N_HEADS)),
                   _const_spec((1, N_HEADS)), _const_spec((1, N_HEADS))),
        scratch_shapes=[pltpu.VMEM((SSD_W, N_STATE), F32)],
        compiler_params=_params(("arbitrary",)),
    )(xa, xa, xa, dt_raw, dt_raw_t, dt_bias, dt_bias_t, a_log, a_log_t, d_exp, hstart, dy)


def _dot_x_exact_tn(x_t, eye):
    out = None
    for p in _split(x_t, 3):
        t = _dot(p, eye, TN)
        out = t if out is None else out + t
    return out


def _adamw(name, parts, w, m, v, rows):
    r_all, cols = w.shape
    assert r_all % rows == 0
    c1 = 1.0 / (1.0 - ADAM_B1 ** ADAM_STEP)
    c2 = 1.0 / (1.0 - ADAM_B2 ** ADAM_STEP)

    def body(p_ref, w_ref, m_ref, v_ref, g_ref, d_ref, mo_ref, vo_ref):
        g = p_ref[0].astype(F32)
        for j in range(1, parts.shape[0]):
            g = g + p_ref[j].astype(F32)
        mn = ADAM_B1 * m_ref[...] + (1.0 - ADAM_B1) * g
        vn = ADAM_B2 * v_ref[...] + (1.0 - ADAM_B2) * (g * g)
        g_ref[...] = g
        mo_ref[...] = mn
        vo_ref[...] = vn
        d_ref[...] = -ADAM_LR * ((mn * c1) / (jnp.sqrt(vn * c2) + ADAM_EPS) + ADAM_WD * w_ref[...])

    spec = pl.BlockSpec((rows, cols), lambda i: (i, 0))
    shp = jax.ShapeDtypeStruct((r_all, cols), F32)
    return pl.pallas_call(
        body, name=name, out_shape=(shp, shp, shp, shp), grid=(r_all // rows,),
        in_specs=[pl.BlockSpec((parts.shape[0], rows, cols), lambda i: (0, i, 0)), spec, spec, spec],
        out_specs=(spec, spec, spec, spec),
        compiler_params=_params(("parallel",)),
    )(parts, w, m, v)


_VECTORS = (("norm_w", 1024, 0), ("conv_b", 1536, 8), ("dt_bias", 16, 20), ("a_log", 16, 21), ("d_skip", 16, 22),
            ("sb_norm_w", 1024, 24), ("ssd_norm_w", 1024, 32), ("final_norm_w", 1024, 40))
_LOSS_ROW = 23
_CONVW_ROW = 48
_META_ROW = 96
_PACK_ROWS = 224
_SMALL_ORDER = tuple(name for name, _, _ in _VECTORS) + ("conv_w", "meta_tokens")


def _pack_small_grads(vectors, loss_row, d_convw, dh):
    def body(*refs):
        vec_refs, (loss_ref, cw_ref, dh_ref, out_ref) = refs[:len(_VECTORS)], refs[len(_VECTORS):]
        out_ref[...] = jnp.zeros_like(out_ref)
        for (_, width, row), ref in zip(_VECTORS, vec_refs):
            if width < BLK:
                out_ref[row:row + 1, :width] = ref[...]
            else:
                for t in range(width // BLK):
                    out_ref[row + t:row + t + 1, :] = ref[:, t * BLK:(t + 1) * BLK]
        out_ref[_LOSS_ROW:_LOSS_ROW + 1, :] = loss_ref[...]
        for k in range(4):
            for t in range(XBC_W // BLK):
                r = _CONVW_ROW + k * (XBC_W // BLK) + t
                out_ref[r:r + 1, :] = cw_ref[k:k + 1, t * BLK:(t + 1) * BLK]
        for i in range(N_META):
            for t in range(D_MODEL // BLK):
                r = _META_ROW + i * (D_MODEL // BLK) + t
                out_ref[r:r + 1, :] = dh_ref[i:i + 1, t * BLK:(t + 1) * BLK]

    full = lambda a: pl.BlockSpec(a.shape, lambda i: tuple(0 for _ in a.shape))
    return pl.pallas_call(
        body, name="pack_small_grads", out_shape=jax.ShapeDtypeStruct((_PACK_ROWS, BLK), F32), grid=(1,),
        in_specs=[full(v) for v in vectors] + [full(loss_row), full(d_convw),
                                               pl.BlockSpec((N_META, D_MODEL), lambda i: (PAD // N_META, 0))],
        out_specs=pl.BlockSpec((_PACK_ROWS, BLK), lambda i: (0, 0)),
        compiler_params=_params(("arbitrary",)),
    )(*vectors, loss_row, d_convw, dh)


def _sum_slots(name, parts, rows):
    n, r_all, cols = parts.shape

    def body(p_ref, o_ref):
        acc = p_ref[0].astype(F32)
        for j in range(1, n):
            acc = acc + p_ref[j].astype(F32)
        o_ref[...] = acc

    return pl.pallas_call(
        body, name=name, out_shape=jax.ShapeDtypeStruct((r_all, cols), F32), grid=(r_all // rows,),
        in_specs=[pl.BlockSpec((n, rows, cols), lambda i: (0, i, 0))],
        out_specs=pl.BlockSpec((rows, cols), lambda i: (i, 0)),
        compiler_params=_params(("parallel",)),
    )(parts)


def _adamw_small(pack, weights, moms, vels):
    c1 = 1.0 / (1.0 - ADAM_B1 ** ADAM_STEP)
    c2 = 1.0 / (1.0 - ADAM_B2 ** ADAM_STEP)
    n = len(_SMALL_ORDER)

    def body(*refs):
        p_ref = refs[0]
        w_refs, m_refs, v_refs = refs[1:1 + n], refs[1 + n:1 + 2 * n], refs[1 + 2 * n:1 + 3 * n]
        outs = refs[1 + 3 * n:1 + 7 * n]
        loss_ref, g_ref, cw_ref, cws_ref, mt_ref, all_ref, send_sems, recv_sems, loc_sem = refs[1 + 7 * n:]
        x, y, c = _mesh_position()
        me = 4 * x + 2 * y + c
        copies = _all_gather_copies(p_ref, all_ref, send_sems, recv_sems, loc_sem)
        for cp in copies:
            cp.start()
        for cp in copies:
            cp.wait()
        g = all_ref[0]
        for j in range(1, N_DEV):
            g = g + all_ref[j]
        g_ref[...] = g
        loss_ref[...] = g_ref[_LOSS_ROW:_LOSS_ROW + 1, :]

        def update(idx, grad):
            go_ref, d_ref, mo_ref, vo_ref = outs[4 * idx:4 * idx + 4]
            mn = ADAM_B1 * m_refs[idx][...] + (1.0 - ADAM_B1) * grad
            vn = ADAM_B2 * v_refs[idx][...] + (1.0 - ADAM_B2) * (grad * grad)
            go_ref[...] = grad
            mo_ref[...] = mn
            vo_ref[...] = vn
            d_ref[...] = -ADAM_LR * ((mn * c1) / (jnp.sqrt(vn * c2) + ADAM_EPS) + ADAM_WD * w_refs[idx][...])

        for idx, (_, width, row) in enumerate(_VECTORS):
            go_ref = outs[4 * idx]
            if width < BLK:
                grad = g_ref[row:row + 1, :width]
            else:
                for t in range(width // BLK):
                    go_ref[:, t * BLK:(t + 1) * BLK] = g_ref[row + t:row + t + 1, :]
                grad = go_ref[...]
            update(idx, grad)
        cw_ref[...] = jnp.zeros_like(cw_ref)
        for k in range(4):
            for t in range(XBC_W // BLK):
                r = _CONVW_ROW + k * (XBC_W // BLK) + t
                cw_ref[k:k + 1, t * BLK:(t + 1) * BLK] = g_ref[r:r + 1, :]
        for i in range(N_META):
            for t in range(D_MODEL // BLK):
                r = _META_ROW + i * (D_MODEL // BLK) + t
                mt_ref[i:i + 1, t * BLK:(t + 1) * BLK] = g_ref[r:r + 1, :]
        width_cw = XBC_W // N_DEV
        pick_cw = (_iota((XBC_W, width_cw), 0) == me * width_cw + _iota((XBC_W, width_cw), 1)).astype(BF16)
        cws_ref[...] = _dot_x_exact(cw_ref[...], pick_cw)
        update(n - 2, cws_ref[0:4, :])
        pick_mt = (_iota((D_MODEL, BLK), 0) == me * BLK + _iota((D_MODEL, BLK), 1)).astype(BF16)
        update(n - 1, _dot_x_exact(mt_ref[...], pick_mt))

    full = lambda a: pl.BlockSpec(a.shape, lambda i: tuple(0 for _ in a.shape))
    params = list(weights) + list(moms) + list(vels)
    out_shape, out_specs = [], []
    for w in weights:
        for _ in range(4):
            out_shape.append(jax.ShapeDtypeStruct(w.shape, F32))
            out_specs.append(full(w))
    out_shape.append(jax.ShapeDtypeStruct((1, BLK), F32))
    out_specs.append(pl.BlockSpec((1, BLK), lambda i: (0, 0)))
    return pl.pallas_call(
        body, name="adamw_small", out_shape=tuple(out_shape), grid=(1,),
        in_specs=[full(pack)] + [full(a) for a in params], out_specs=tuple(out_specs),
        scratch_shapes=[pltpu.VMEM((_PACK_ROWS, BLK), F32), pltpu.VMEM((8, XBC_W), F32),
                        pltpu.VMEM((8, XBC_W // N_DEV), F32), pltpu.VMEM((N_META, D_MODEL), F32),
                        pltpu.VMEM((N_DEV, _PACK_ROWS, BLK), F32)] + _ALL_GATHER_SEMS,
        compiler_params=_params(("arbitrary",)),
    )(pack, *params)


def kernel(x, meta_tokens, norm_w, w_in, conv_w, conv_b, dt_bias, a_log, d_skip, sb_norm_w, ssd_norm_w, w_out, final_norm_w, loss_target, m_meta_tokens, m_norm_w, m_w_in, m_conv_w, m_conv_b, m_dt_bias, m_a_log, m_d_skip, m_sb_norm_w, m_ssd_norm_w, m_w_out, m_final_norm_w, v_meta_tokens, v_norm_w, v_w_in, v_conv_w, v_conv_b, v_dt_bias, v_a_log, v_d_skip, v_sb_norm_w, v_ssd_norm_w, v_w_out, v_final_norm_w):
    small_src = jnp.concatenate([conv_w[0].reshape(6, BLK), meta_tokens, jnp.zeros((2, BLK), F32)], axis=0)
    small_g, w_in_g = _gather_weights([small_src, w_in[0].astype(BF16)])
    w_in_full = w_in_g.transpose(1, 0, 2).reshape(D_MODEL, D_IN)
    w_dt = w_in_full[:, D_MAIN:]
    conv_w_full = small_g[:, :6].reshape(N_DEV, 4, 192).transpose(1, 0, 2).reshape(4, XBC_W)
    meta_full = small_g[:, 6:6 + N_META].transpose(1, 0, 2).reshape(N_META, D_MODEL)
    h_pad = jnp.concatenate([jnp.zeros((PAD, D_MODEL), F32), meta_full, x[0]], axis=0)
    dt_bias_t = dt_bias.reshape(N_HEADS, 1)
    a_log_t = a_log.reshape(N_HEADS, 1)
    d_exp = jnp.repeat(d_skip, HEAD_DIM, axis=1)
    fnw = final_norm_w.reshape(1, D_MODEL)

    u, u_t, dt_raw, dt_raw_t = _prenorm(h_pad, norm_w, w_dt, w_dt.T)
    proj = _matmul("in_proj", u, w_in_full, "nn", LP, 512, D_MODEL, n_cols=D_MAIN)
    o_sb, o_sb_exact, w_out_g = _sb_attention_fwd(proj, w_out[0].astype(BF16))
    w_out_full = w_out_g.reshape(2 * SSD_W, D_MODEL)
    xa = _conv_fwd(proj, conv_w_full, conv_b)
    o_ssd, hstart = _ssd_fwd(xa, dt_raw, dt_raw_t, dt_bias, dt_bias_t, a_log, a_log_t, d_exp)
    ycat, dh2, dh2_b, loss_row, d_fnw = _out_head(
        o_sb, proj, o_ssd, sb_norm_w, ssd_norm_w, w_out_full, h_pad, fnw, loss_target[0])

    g_w_out = _matmul("d_w_out", ycat, dh2_b, "tn", 512, 512, LP, BF16).reshape(4, 2, 256, D_MODEL)
    do_sb, dg, do_ssd, dz, d_sbw, d_ssdw, sib_w_out = _ycat_bwd(
        dh2_b, w_out_full, o_sb, proj, o_ssd, sb_norm_w, ssd_norm_w, g_w_out)
    chip_w_out = _pair_sum("pair_sum_w_out", g_w_out, sib_w_out, 256)
    dq, dk, dv, p_w_out = _sb_attention_bwd(proj, o_sb_exact, do_sb, chip_w_out)
    dxa, ddt_raw, d_dtb, d_alog, d_dskip = _ssd_bwd(
        xa, dt_raw, dt_raw_t, dt_bias, dt_bias_t, a_log, a_log_t, d_exp, hstart, do_ssd)
    dxbc, d_convw, d_convb = _conv_bwd(dxa, proj, conv_w_full, conv_b)
    pieces = [dq, dk, dv, dg, dz, dxbc, ddt_raw]
    g_w_in_mine, g_w_in_sib = _d_w_in(u_t, pieces)
    chip_w_in = _add_halves("pair_sum_w_in", g_w_in_mine, g_w_in_sib, 128)
    dh, d_nw, win_parts = _d_u_prenorm_bwd(pieces, w_in_full, w_dt, h_pad, norm_w, dh2, chip_w_in)
    win_mine = _sum_slots("sum_w_in_windows", win_parts, 128)
    win_other = _swap_with_sibling("swap_w_in_window", win_mine, whole=True)
    core = lax.axis_index("c")
    chip = 2 * lax.axis_index("x") + lax.axis_index("y")
    first_col = (D_IN // N_DEV) * (2 * chip + core) - WIN_STRIDE * chip
    cut = lambda w: lax.dynamic_slice(w, (0, first_col), (D_MODEL // 2, D_IN // N_DEV))
    half_mine, half_other = cut(win_mine), cut(win_other)
    p_w_in = jnp.concatenate([jnp.where(core == 0, half_mine, half_other),
                              jnp.where(core == 0, half_other, half_mine)], axis=0)[None]

    pack = _pack_small_grads([d_nw, d_convb, d_dtb, d_alog, d_dskip, d_sbw, d_ssdw, d_fnw], loss_row, d_convw, dh)

    res_in = _adamw("adamw_w_in", p_w_in, w_in[0], m_w_in[0], v_w_in[0], 128)
    res_out = _adamw("adamw_w_out", p_w_out, w_out[0], m_w_out[0], v_w_out[0], 128)
    res_small = _adamw_small(
        pack,
        [norm_w, conv_b, dt_bias, a_log, d_skip, sb_norm_w, ssd_norm_w, fnw, conv_w[0], meta_tokens],
        [m_norm_w, m_conv_b, m_dt_bias, m_a_log, m_d_skip, m_sb_norm_w, m_ssd_norm_w,
         m_final_norm_w.reshape(1, D_MODEL), m_conv_w[0], m_meta_tokens],
        [v_norm_w, v_conv_b, v_dt_bias, v_a_log, v_d_skip, v_sb_norm_w, v_ssd_norm_w,
         v_final_norm_w.reshape(1, D_MODEL), v_conv_w[0], v_meta_tokens])

    loss = jnp.sum(res_small[-1])
    order = ["meta_tokens", "norm_w", "w_in", "conv_w", "conv_b", "dt_bias", "a_log", "d_skip",
             "sb_norm_w", "ssd_norm_w", "w_out", "final_norm_w"]
    outs = [loss, dh[BLK:].reshape(1, SEQ, D_MODEL)]
    for kind in range(4):
        small = {name: res_small[4 * idx + kind] for idx, name in enumerate(_SMALL_ORDER)}
        small["final_norm_w"] = small["final_norm_w"].reshape(D_MODEL)
        small["conv_w"] = small["conv_w"].reshape(1, 4, XBC_W // N_DEV)
        small["w_in"] = res_in[kind].reshape(1, D_MODEL, D_IN // N_DEV)
        small["w_out"] = res_out[kind].reshape(1, 256, D_MODEL)
        outs += [small[name] for name in order]
    return tuple(outs)
```

```python
import jax
import jax.numpy as jnp
from jax import lax
from jax.experimental import pallas as pl
from jax.experimental.pallas import tpu as pltpu

F32 = jnp.float32
BF16 = jnp.bfloat16

D_MODEL = 1024
SEQ = 2048
N_META = 16
BLK = 128
PAD = BLK - N_META
LP = PAD + N_META + SEQ
NBLK = LP // BLK
N_HEADS = 16
HEAD_DIM = 64
N_GROUPS = 2
HEADS_PER_GROUP = 8
N_STATE = 128
SSD_W = 1024
XBC_W = 1536
D_MAIN = 6656
D_IN = 6672
COL_Q, COL_K, COL_V, COL_G, COL_Z, COL_XBC = 0, 1024, 2048, 3072, 4096, 5120
N_DEV = 8
EPS = 1e-5
SB_SCALE = 0.125
SB_DEAD = -87.4
SB_MASKED = -1e30

ADAM_LR = 0.001
ADAM_B1 = 0.9
ADAM_B2 = 0.999
ADAM_EPS = 1e-08
ADAM_WD = 0.01
ADAM_STEP = 10

VMEM_LIMIT = 48 * 1024 * 1024
BIG_VMEM_LIMIT = 56 * 1024 * 1024

NN = (((1,), (0,)), ((), ()))
NT = (((1,), (1,)), ((), ()))
TN = (((0,), (0,)), ((), ()))


def _dot(a, b, dims=NN):
    return lax.dot_general(a, b, dims, preferred_element_type=F32)


def _split(x, n):
    parts = []
    r = x
    for i in range(n):
        p = r.astype(BF16)
        parts.append(p)
        if i + 1 < n:
            r = r - p.astype(F32)
    return parts


def _dot_x_exact(x, m, dims=NN, n=3):
    out = None
    for p in _split(x, n):
        t = _dot(p, m, dims)
        out = t if out is None else out + t
    return out


def _dot_exact_x(m, x, dims=NN, n=3):
    out = None
    for p in _split(x, n):
        t = _dot(m, p, dims)
        out = t if out is None else out + t
    return out


def _iota(shape, dim):
    return lax.broadcasted_iota(jnp.int32, shape, dim)


def _softplus(x):
    return jnp.maximum(x, 0.0) + jnp.log(1.0 + jnp.exp(-jnp.abs(x)))


def _sigmoid(x):
    return 1.0 / (1.0 + jnp.exp(-x))


def _params(sem=None, vmem=None):
    return pltpu.CompilerParams(dimension_semantics=sem, vmem_limit_bytes=vmem or VMEM_LIMIT)


_ANY = pl.BlockSpec(memory_space=pl.ANY)
_MESH = pl.DeviceIdType.MESH


def _mesh_position():
    return lax.axis_index("x"), lax.axis_index("y"), lax.axis_index("c")


def _other_chips(x, y):
    return [(1 - x, y), (x, 1 - y), (1 - x, 1 - y)]


def _gather_weights(srcs):
    n = len(srcs)

    def body(*refs):
        src, out = refs[:n], refs[n:2 * n]
        send_sems, recv_sems, loc_sems = refs[2 * n:]
        x, y, c = _mesh_position()
        sibling = (x, y, 1 - c)
        chips = _other_chips(x, y)
        relay_from = (jnp.where(c == 0, 1 - x, x), jnp.where(c == 0, y, 1 - y))
        relay_to = (jnp.where(c == 0, x, 1 - x), jnp.where(c == 0, 1 - y, y))

        def copy(a, k, block, to, from_src=False):
            slot = out[a].at[4 * block[0] + 2 * block[1] + block[2]]
            return pltpu.make_async_remote_copy(
                src_ref=src[a] if from_src else slot, dst_ref=slot,
                send_sem=send_sems.at[7 * a + k], recv_sem=recv_sems.at[7 * a + k],
                device_id=to, device_id_type=_MESH)

        local, sends = [], []
        for a in range(n):
            mine = pltpu.make_async_copy(src[a], out[a].at[4 * x + 2 * y + c], loc_sems.at[a])
            mine.start()
            local.append(mine)
            first = [copy(a, 0, (x, y, c), sibling, True)]
            first += [copy(a, 1 + j, (x, y, c), (*chip, c), True) for j, chip in enumerate(chips[:2])]
            for cp in first:
                cp.start()
            sends += first
        for a in range(n):
            for j, chip in enumerate(chips[:2]):
                copy(a, 1 + j, (*chip, c), (x, y, c)).wait_recv()
            later = [copy(a, 3, (*relay_from, c), (*relay_to, c))]
            later += [copy(a, 4 + j, (*chip, c), sibling) for j, chip in enumerate(chips[:2])]
            for cp in later:
                cp.start()
            sends += later
        for a in range(n):
            copy(a, 3, (*chips[2], c), (x, y, c)).wait_recv()
            passed = copy(a, 6, (*chips[2], c), sibling)
            passed.start()
            sends.append(passed)
        for a in range(n):
            copy(a, 0, (x, y, 1 - c), (x, y, c)).wait_recv()
            for j, chip in enumerate(chips):
                copy(a, 4 + j, (*chip, 1 - c), (x, y, c)).wait_recv()
        for cp in sends:
            cp.wait_send()
        for cp in local:
            cp.wait()

    return pl.pallas_call(
        body, name="gather_weights",
        out_shape=tuple(jax.ShapeDtypeStruct((N_DEV,) + s.shape, s.dtype) for s in srcs),
        in_specs=[_ANY] * n, out_specs=tuple([_ANY] * n),
        scratch_shapes=[pltpu.SemaphoreType.DMA((7 * n,)), pltpu.SemaphoreType.DMA((7 * n,)),
                        pltpu.SemaphoreType.DMA((n,))],
    )(*srcs)


_ALL_GATHER_SEMS = [pltpu.SemaphoreType.DMA((7,)), pltpu.SemaphoreType.DMA((7,)), pltpu.SemaphoreType.DMA]


def _all_gather_copies(src, out, send_sems, recv_sems, loc_sem):
    x, y, c = _mesh_position()
    me = 4 * x + 2 * y + c
    copies = [pltpu.make_async_copy(src, out.at[me], loc_sem)]
    for k in range(1, N_DEV):
        peer = (1 - x if k & 4 else x, 1 - y if k & 2 else y, 1 - c if k & 1 else c)
        copies.append(pltpu.make_async_remote_copy(
            src_ref=src, dst_ref=out.at[me], send_sem=send_sems.at[k - 1], recv_sem=recv_sems.at[k - 1],
            device_id=peer, device_id_type=_MESH))
    return copies


def _sibling_swap_copies(src, out, send_sems, recv_sems, n, whole=False):
    x, y, c = _mesh_position()
    return [pltpu.make_async_remote_copy(
        src_ref=src if whole else src.at[k, 1 - c], dst_ref=out if whole else out.at[k],
        send_sem=send_sems.at[k], recv_sem=recv_sems.at[k], device_id=(x, y, 1 - c), device_id_type=_MESH)
        for k in range(n)]


def _swap_with_sibling(name, src, whole=False):
    n = 1 if whole else src.shape[0]

    def body(src_ref, out_ref, send_sems, recv_sems):
        copies = _sibling_swap_copies(src_ref, out_ref, send_sems, recv_sems, n, whole)
        for cp in copies:
            cp.start()
        for cp in copies:
            cp.wait()

    shape = src.shape if whole else (src.shape[0],) + src.shape[2:]
    return pl.pallas_call(
        body, name=name, out_shape=jax.ShapeDtypeStruct(shape, src.dtype), in_specs=[_ANY], out_specs=_ANY,
        scratch_shapes=[pltpu.SemaphoreType.DMA((n,)), pltpu.SemaphoreType.DMA((n,))],
    )(src)


def _pair_sum(name, g, sib, rows):
    n, _, r_all, cols = g.shape
    assert r_all % rows == 0

    def body(g0_ref, g1_ref, s_ref, o_ref):
        c = lax.axis_index("c")
        mine = jnp.where(c == 0, g0_ref[0, 0].astype(F32), g1_ref[0, 0].astype(F32))
        o_ref[0] = (mine + s_ref[0].astype(F32)).astype(o_ref.dtype)

    return pl.pallas_call(
        body, name=name, out_shape=jax.ShapeDtypeStruct((n, r_all, cols), BF16), grid=(n, r_all // rows),
        in_specs=[pl.BlockSpec((1, 1, rows, cols), lambda k, i: (k, 0, i, 0)),
                  pl.BlockSpec((1, 1, rows, cols), lambda k, i: (k, 1, i, 0)),
                  pl.BlockSpec((1, rows, cols), lambda k, i: (k, i, 0))],
        out_specs=pl.BlockSpec((1, rows, cols), lambda k, i: (k, i, 0)),
        compiler_params=_params(("parallel", "parallel")),
    )(g, g, sib)


_CHIP_EXCHANGE_SEMS = [pltpu.SemaphoreType.DMA((3,)), pltpu.SemaphoreType.DMA((3,)), pltpu.SemaphoreType.DMA]


def _chip_exchange_copies(src, out, send_sems, recv_sems, loc_sem, window=None):
    x, y, c = _mesh_position()
    here = 2 * x + y

    def slot(k):
        if window is not None:
            return src.at[:, pl.ds(pl.multiple_of(k * window[0], BLK), window[1])]
        return src.at[k]

    copies = [pltpu.make_async_copy(slot(here), out.at[here], loc_sem)]
    for j, chip in enumerate(_other_chips(x, y)):
        copies.append(pltpu.make_async_remote_copy(
            src_ref=slot(2 * chip[0] + chip[1]), dst_ref=out.at[here],
            send_sem=send_sems.at[j], recv_sem=recv_sems.at[j], device_id=(*chip, c), device_id_type=_MESH))
    return copies


def _matmul(name, a, b, kind, tm, tn, tk, out_dtype=F32, n_cols=None):
    if kind == "nn":
        (m, kk), nn_ = a.shape, (n_cols or b.shape[1])
        a_spec = pl.BlockSpec((tm, tk), lambda i, j, k: (i, k))
        b_spec = pl.BlockSpec((tk, tn), lambda i, j, k: (k, j))
        dims = NN
    else:
        (kk, m), (_, nn_) = a.shape, b.shape
        a_spec = pl.BlockSpec((tk, tm), lambda i, j, k: (k, i))
        b_spec = pl.BlockSpec((tk, tn), lambda i, j, k: (k, j))
        dims = TN
    assert m % tm == 0 and nn_ % tn == 0 and kk % tk == 0
    nk = kk // tk

    def body(a_ref, b_ref, o_ref, acc_ref):
        k = pl.program_id(2)
        part = _dot(a_ref[...], b_ref[...], dims)
        if nk == 1:
            o_ref[...] = part.astype(o_ref.dtype)
        else:
            @pl.when(k == 0)
            def _():
                acc_ref[...] = part

            @pl.when(k > 0)
            def _():
                acc_ref[...] += part

            @pl.when(k == nk - 1)
            def _():
                o_ref[...] = acc_ref[...].astype(o_ref.dtype)

    return pl.pallas_call(
        body, name=name, out_shape=jax.ShapeDtypeStruct((m, nn_), out_dtype),
        grid=(m // tm, nn_ // tn, nk),
        in_specs=[a_spec, b_spec], out_specs=pl.BlockSpec((tm, tn), lambda i, j, k: (i, j)),
        scratch_shapes=[pltpu.VMEM((tm, tn) if nk > 1 else (8, 128), F32)],
        compiler_params=_params(("parallel", "parallel", "arbitrary")),
    )(a, b)


def _row_spec(width, col=0):
    return pl.BlockSpec((BLK, width), lambda i: (i, col))


def _const_spec(shape):
    return pl.BlockSpec(shape, lambda i: tuple(0 for _ in shape))


def _prenorm(h_pad, norm_w, wdt, wdt_t):
    def body(h_ref, w_ref, wdt_ref, wdtt_ref, u_ref, ut_ref, dt_ref, dtt_ref):
        xv = h_ref[...]
        r = lax.rsqrt(jnp.mean(xv * xv, axis=-1, keepdims=True) + EPS)
        uf = xv * r * w_ref[...]
        u = uf.astype(BF16)
        u_ref[...] = u
        ut_ref[...] = uf.T.astype(BF16)
        dt_ref[...] = _dot(u, wdt_ref[...], NN)
        dtt_ref[...] = _dot(wdtt_ref[...], u, NT)

    return pl.pallas_call(
        body, name="prenorm",
        out_shape=(jax.ShapeDtypeStruct((LP, D_MODEL), BF16), jax.ShapeDtypeStruct((D_MODEL, LP), BF16),
                   jax.ShapeDtypeStruct((LP, N_HEADS), F32), jax.ShapeDtypeStruct((N_HEADS, LP), F32)),
        grid=(NBLK,),
        in_specs=[_row_spec(D_MODEL), _const_spec((1, D_MODEL)), _const_spec((D_MODEL, N_HEADS)),
                  _const_spec((N_HEADS, D_MODEL))],
        out_specs=(_row_spec(D_MODEL), pl.BlockSpec((D_MODEL, BLK), lambda i: (0, i)), _row_spec(N_HEADS),
                   pl.BlockSpec((N_HEADS, BLK), lambda i: (0, i))),
        compiler_params=_params(("parallel",)),
    )(h_pad, norm_w, wdt, wdt_t)


def _gated_norm(o, g, w):
    a = o * (g * _sigmoid(g))
    r = lax.rsqrt(jnp.mean(a * a, axis=-1, keepdims=True) + EPS)
    return a * r * w


def _out_head(o_sb, proj, o_ssd, sb_w, ssd_w, w_out_full, h_pad, fnw, target):
    tm = LP // 8

    def body(osb_ref, g_ref, ossd_ref, z_ref, sbw_ref, ssdw_ref, wout_ref, h_ref, w_ref, t_hbm,
             y_ref, dh2_ref, dh2b_ref, loss_ref, dw_ref, t_ref, t_sem):
        i = pl.program_id(0)

        @pl.when(i == 0)
        def _():
            loss_ref[...] = jnp.zeros_like(loss_ref)
            dw_ref[...] = jnp.zeros_like(dw_ref)
            t_ref[:BLK, :] = jnp.zeros((BLK, D_MODEL), F32)
            first = pltpu.make_async_copy(t_hbm.at[pl.ds(0, tm - BLK)], t_ref.at[pl.ds(BLK, tm - BLK)], t_sem)
            first.start()
            first.wait()

        @pl.when(i > 0)
        def _():
            rest = pltpu.make_async_copy(t_hbm.at[pl.ds(pl.multiple_of(i * tm - BLK, 8), tm)], t_ref, t_sem)
            rest.start()
            rest.wait()

        y_ref[:, :SSD_W] = _gated_norm(osb_ref[...], g_ref[...], sbw_ref[...]).astype(BF16)
        y_ref[:, SSD_W:] = _gated_norm(ossd_ref[...], z_ref[...], ssdw_ref[...]).astype(BF16)
        h2 = h_ref[...] + _dot(y_ref[...], wout_ref[...], NN)
        r = lax.rsqrt(jnp.mean(h2 * h2, axis=-1, keepdims=True) + EPS)
        nrm = h2 * r
        w = w_ref[...]
        live = i * tm + _iota((tm, D_MODEL), 0) >= BLK
        err = jnp.where(live, nrm * w - t_ref[...], 0.0)
        dout = err * (1.0 / D_MODEL)
        loss_ref[...] += (0.5 / D_MODEL) * _fold_lanes(jnp.sum(err * err, axis=0, keepdims=True))
        dw_ref[...] += jnp.sum(dout * nrm, axis=0, keepdims=True)
        wd = dout * w
        dh2 = r * (wd - nrm * jnp.mean(wd * nrm, axis=-1, keepdims=True))
        dh2_ref[...] = dh2
        dh2b_ref[...] = dh2.astype(BF16)

    rows = lambda width, col=0: pl.BlockSpec((tm, width), lambda i: (i, col))
    return pl.pallas_call(
        body, name="out_head",
        out_shape=(jax.ShapeDtypeStruct((LP, 2 * SSD_W), BF16), jax.ShapeDtypeStruct((LP, D_MODEL), F32),
                   jax.ShapeDtypeStruct((LP, D_MODEL), BF16), jax.ShapeDtypeStruct((1, BLK), F32),
                   jax.ShapeDtypeStruct((1, D_MODEL), F32)),
        grid=(LP // tm,),
        in_specs=[rows(1024), rows(1024, COL_G // 1024), rows(1024), rows(1024, COL_Z // 1024),
                  _const_spec((1, 1024)), _const_spec((1, 1024)),
                  pl.BlockSpec((2 * SSD_W, D_MODEL), lambda i: (0, 0), pipeline_mode=pl.Buffered(1)),
                  rows(D_MODEL), _const_spec((1, D_MODEL)), _ANY],
        out_specs=(rows(2 * SSD_W), rows(D_MODEL), rows(D_MODEL), _const_spec((1, BLK)), _const_spec((1, D_MODEL))),
        scratch_shapes=[pltpu.VMEM((tm, D_MODEL), F32), pltpu.SemaphoreType.DMA],
        compiler_params=_params(("arbitrary",)),
    )(o_sb, proj, o_ssd, proj, sb_w, ssd_w, w_out_full, h_pad, fnw, target)


def _fold_lanes(row):
    out = row[:, :BLK]
    for j in range(1, row.shape[1] // BLK):
        out = out + row[:, j * BLK:(j + 1) * BLK]
    return out


def _gated_norm_bwd(dy, o, g, w):
    s = _sigmoid(g)
    sg = g * s
    a = o * sg
    r = lax.rsqrt(jnp.mean(a * a, axis=-1, keepdims=True) + EPS)
    nrm = a * r
    dw = jnp.sum(dy * nrm, axis=0, keepdims=True)
    wd = dy * w
    da = r * (wd - nrm * jnp.mean(wd * nrm, axis=-1, keepdims=True))
    return da * sg, da * o * (s * (1.0 + g * (1.0 - s))), dw


def _ycat_bwd(dh2_b, w_out_full, o_sb, proj, o_ssd, sb_w, ssd_w, g_w_out):
    tm = LP // 8
    n_slots = g_w_out.shape[0]

    def body(a_ref, w_ref, osb_ref, g_ref, ossd_ref, z_ref, sbw_ref, ssdw_ref, src_ref,
             dosb_ref, dg_ref, dossd_ref, dz_ref, dsbw_ref, dssdw_ref, sib_ref, send_sems, recv_sems):
        i = pl.program_id(0)

        @pl.when(i == 0)
        def _():
            for cp in _sibling_swap_copies(src_ref, sib_ref, send_sems, recv_sems, n_slots):
                cp.start()
            dsbw_ref[...] = jnp.zeros_like(dsbw_ref)
            dssdw_ref[...] = jnp.zeros_like(dssdw_ref)

        dy = _dot(a_ref[...], w_ref[...], NT)
        do, dg, dw = _gated_norm_bwd(dy[:, :SSD_W], osb_ref[...], g_ref[...], sbw_ref[...])
        dosb_ref[...] = do
        dg_ref[...] = dg.astype(BF16)
        dsbw_ref[...] += dw
        do, dg, dw = _gated_norm_bwd(dy[:, SSD_W:], ossd_ref[...], z_ref[...], ssdw_ref[...])
        dossd_ref[...] = do
        dz_ref[...] = dg.astype(BF16)
        dssdw_ref[...] += dw

        @pl.when(i == LP // tm - 1)
        def _():
            for cp in _sibling_swap_copies(src_ref, sib_ref, send_sems, recv_sems, n_slots):
                cp.wait()

    act = jax.ShapeDtypeStruct((LP, 1024), F32)
    gate = jax.ShapeDtypeStruct((LP, 1024), BF16)
    vec = jax.ShapeDtypeStruct((1, 1024), F32)
    rows = lambda col=0: pl.BlockSpec((tm, 1024), lambda i: (i, col))
    return pl.pallas_call(
        body, name="ycat_bwd",
        out_shape=(act, gate, act, gate, vec, vec,
                   jax.ShapeDtypeStruct((n_slots,) + g_w_out.shape[2:], g_w_out.dtype)),
        grid=(LP // tm,),
        in_specs=[rows(), _const_spec((2 * SSD_W, D_MODEL)), rows(), rows(COL_G // 1024), rows(),
                  rows(COL_Z // 1024), _const_spec((1, 1024)), _const_spec((1, 1024)), _ANY],
        out_specs=(rows(), rows(), rows(), rows(), _const_spec((1, 1024)), _const_spec((1, 1024)), _ANY),
        scratch_shapes=[pltpu.SemaphoreType.DMA((n_slots,)), pltpu.SemaphoreType.DMA((n_slots,))],
        compiler_params=_params(("arbitrary",)),
    )(dh2_b, w_out_full, o_sb, proj, o_ssd, proj, sb_w, ssd_w, g_w_out)


_DPROJ_PIECES = (("dq", COL_Q, 1024), ("dk", COL_K, 1024), ("dv", COL_V, 1024), ("dg", COL_G, 1024),
                 ("dz", COL_Z, 1024), ("dxbc", COL_XBC, XBC_W), ("ddt", D_MAIN, BLK))
W_IN_PAD = D_MAIN + 512
WIN_STRIDE = 13 * BLK
WIN_WIDTH = 14 * BLK


def _d_w_in(u_t, pieces):
    tn = 512
    nj = W_IN_PAD // tn
    half = D_MODEL // 2
    main = _DPROJ_PIECES[:-1]

    def body(*refs):
        u_ref, piece_refs = refs[0], refs[1:1 + len(main)]
        ddt_ref, o_ref, sib_ref, stage_ref, send_sems, recv_sems = refs[1 + len(main):]
        j = pl.program_id(1)
        x, y, c = _mesh_position()
        a = u_ref[...]

        def copy(blk):
            return pltpu.make_async_remote_copy(
                src_ref=stage_ref.at[blk], dst_ref=sib_ref.at[:, pl.ds(pl.multiple_of(blk * tn, tn), tn)],
                send_sem=send_sems.at[blk], recv_sem=recv_sems.at[blk],
                device_id=(x, y, 1 - c), device_id_type=_MESH)

        def emit(res):
            top, bottom = res[:half], res[half:]
            o_ref[...] = jnp.where(c == 0, top, bottom)
            stage_ref[j] = jnp.where(c == 0, bottom, top)
            copy(j).start()

        for (_, col, width), ref in zip(main, piece_refs):
            @pl.when(jnp.logical_and(j >= col // tn, j < (col + width) // tn))
            def _():
                emit(_dot(a, ref[...], NN).astype(BF16))

        @pl.when(j == nj - 1)
        def _():
            tail = _dot(a, ddt_ref[...].astype(BF16), NN).astype(BF16)
            emit(jnp.concatenate([tail, jnp.zeros((D_MODEL, tn - BLK), BF16)], axis=1))
            for blk in range(nj):
                copy(blk).wait()

    def piece_spec(col, width):
        return pl.BlockSpec((LP, tn), lambda i, j: (0, jnp.clip(j - col // tn, 0, width // tn - 1)))

    shape = jax.ShapeDtypeStruct((half, W_IN_PAD), BF16)
    return pl.pallas_call(
        body, name="d_w_in", out_shape=(shape, shape), grid=(1, nj),
        in_specs=[pl.BlockSpec((D_MODEL, LP), lambda i, j: (0, 0), pipeline_mode=pl.Buffered(1))]
        + [piece_spec(col, width) for _, col, width in main]
        + [pl.BlockSpec((LP, BLK), lambda i, j: (0, 0))],
        out_specs=(pl.BlockSpec((half, tn), lambda i, j: (0, j)), _ANY),
        scratch_shapes=[pltpu.VMEM((nj, half, tn), BF16), pltpu.SemaphoreType.DMA((nj,)),
                        pltpu.SemaphoreType.DMA((nj,))],
        compiler_params=_params(("arbitrary", "arbitrary"), BIG_VMEM_LIMIT),
    )(u_t, *pieces)


def _add_halves(name, mine, sib, rows):
    r_all, cols = mine.shape

    def body(a_ref, b_ref, o_ref):
        o_ref[...] = (a_ref[...].astype(F32) + b_ref[...].astype(F32)).astype(BF16)

    spec = pl.BlockSpec((rows, cols), lambda i: (i, 0))
    return pl.pallas_call(
        body, name=name, out_shape=jax.ShapeDtypeStruct((r_all, cols), BF16), grid=(r_all // rows,),
        in_specs=[spec, spec], out_specs=spec, compiler_params=_params(("parallel",)),
    )(mine, sib)


def _d_u_prenorm_bwd(pieces, w_main, wdt, h_pad, norm_w, dh2, chip_sum):
    tm, tk = LP // 4, 512
    nk = D_MAIN // tk
    main = _DPROJ_PIECES[:-1]
    window = (WIN_STRIDE, WIN_WIDTH)

    def body(*refs):
        piece_refs = refs[:len(main)]
        (b_ref, ddt_ref, wdt_ref, h_ref, w_ref, dh2_ref, src_ref, dh_ref, dw_ref, out_ref,
         acc_ref, send_sems, recv_sems, loc_sem) = refs[len(main):]
        i, k = pl.program_id(0), pl.program_id(1)

        @pl.when(jnp.logical_and(i == 0, k == 0))
        def _():
            for cp in _chip_exchange_copies(src_ref, out_ref, send_sems, recv_sems, loc_sem, window=window):
                cp.start()
            dw_ref[...] = jnp.zeros_like(dw_ref)

        @pl.when(k == 0)
        def _():
            acc_ref[...] = jnp.zeros_like(acc_ref)

        for (_, col, width), ref in zip(main, piece_refs):
            for lo in range(0, width, tk):
                @pl.when(k == (col + lo) // tk)
                def _():
                    acc_ref[...] += _dot(ref[:, lo:lo + tk], b_ref[...], NT)

        @pl.when(k == nk - 1)
        def _():
            dut = acc_ref[...] + _dot(ddt_ref[:, :N_HEADS].astype(BF16), wdt_ref[...], NT)
            xv = h_ref[...]
            r = lax.rsqrt(jnp.mean(xv * xv, axis=-1, keepdims=True) + EPS)
            nrm = xv * r
            dw_ref[...] += jnp.sum(dut * nrm, axis=0, keepdims=True)
            wd = dut * w_ref[...]
            dh_ref[...] = dh2_ref[...] + r * (wd - nrm * jnp.mean(wd * nrm, axis=-1, keepdims=True))

        @pl.when(jnp.logical_and(i == LP // tm - 1, k == nk - 1))
        def _():
            for cp in _chip_exchange_copies(src_ref, out_ref, send_sems, recv_sems, loc_sem, window=window):
                cp.wait()

    rows = lambda width: pl.BlockSpec((tm, width), lambda i, k: (i, 0))
    const = lambda shape: pl.BlockSpec(shape, lambda i, k: (0, 0))
    return pl.pallas_call(
        body, name="d_u_prenorm_bwd",
        out_shape=(jax.ShapeDtypeStruct((LP, D_MODEL), F32), jax.ShapeDtypeStruct((1, D_MODEL), F32),
                   jax.ShapeDtypeStruct((4, chip_sum.shape[0], WIN_WIDTH), chip_sum.dtype)),
        grid=(LP // tm, nk),
        in_specs=[rows(width) for _, _, width in main]
        + [pl.BlockSpec((D_MODEL, tk), lambda i, k: (0, k)), rows(BLK), const((D_MODEL, N_HEADS)), rows(D_MODEL),
           const((1, D_MODEL)), rows(D_MODEL), _ANY],
        out_specs=(rows(D_MODEL), const((1, D_MODEL)), _ANY),
        scratch_shapes=[pltpu.VMEM((tm, D_MODEL), F32)] + _CHIP_EXCHANGE_SEMS,
        compiler_params=_params(("arbitrary", "arbitrary")),
    )(*pieces[:-1], w_main, pieces[-1], wdt, h_pad, norm_w, dh2, chip_sum)


def _suffix_sum(vals, tri):
    return _dot_x_exact(vals, tri, NN, n=2)


def _sb_tile(z):
    t = jnp.exp(-jnp.abs(z))
    inv = 1.0 / (1.0 + t)
    sp = jnp.maximum(z, 0.0) + jnp.log(1.0 + t)
    sig = jnp.where(z >= 0, inv, t * inv)
    return sig, -sp, z - sp


def _sweep(first, step, init, run_slots):
    def alive_of(state):
        top = state[run_slots[0]]
        for s in run_slots[1:]:
            top = jnp.maximum(top, state[s])
        return (jnp.max(top) > SB_DEAD).astype(jnp.int32)

    def cond(carry):
        return jnp.logical_and(carry[0] >= 0, carry[1] > 0)

    def body(carry):
        state = step(carry[0], tuple(carry[2:]))
        return (carry[0] - 1, alive_of(state)) + tuple(state)

    return lax.while_loop(cond, body, (first, alive_of(init)) + tuple(init))[2:]


def _head_masks(x, lane):
    head0 = lane < HEAD_DIM
    return [jnp.where(head0, x, 0.0).astype(BF16), jnp.where(head0, 0.0, x).astype(BF16)]


PAIRS_PER_STEP = 4
STEP_W = PAIRS_PER_STEP * BLK
STEP_HEADS = 2 * PAIRS_PER_STEP


def _pair_lanes(h):
    lo = (h // 2) * BLK
    return slice(lo, lo + BLK)


def _stage_kv(qi, k_ref, v_ref, kb_ref, vb_ref):
    @pl.when(qi == 0)
    def _():
        kb_ref[:BLK, :] = jnp.zeros((BLK, STEP_W), BF16)
        vb_ref[:BLK, :] = jnp.zeros((BLK, STEP_W), BF16)
        kb_ref[BLK:, :] = k_ref[...].astype(BF16)
        vb_ref[BLK:, :] = v_ref[...].astype(BF16)


def _window_masks(qi):
    col2 = (qi - 1) * BLK + _iota((BLK, 2 * BLK), 1)
    row2 = qi * BLK + _iota((BLK, 2 * BLK), 0)
    valid2 = jnp.logical_and(col2 < row2, col2 >= PAD)
    row1 = qi * BLK + _iota((BLK, BLK), 0)
    lane = _iota((BLK, BLK), 1)

    def valid1(kblk):
        col = kblk * BLK + lane
        return jnp.logical_and(col < row1, col >= PAD)

    return valid2, valid1


def _strict_upper(n):
    return (_iota((n, n), 0) > _iota((n, n), 1)).astype(BF16)


def _sb_attention_fwd(proj, w_out_shard):
    heads = range(STEP_HEADS)
    n_groups = N_HEADS // STEP_HEADS

    def body(q_ref, k_ref, v_ref, src_ref, o_ref, ox_ref, out_ref, kb_ref, vb_ref, send_sems, recv_sems, loc_sem):
        grp, qi = pl.program_id(0), pl.program_id(1)
        _stage_kv(qi, k_ref, v_ref, kb_ref, vb_ref)

        @pl.when(jnp.logical_and(grp == 0, qi == 0))
        def _():
            for cp in _all_gather_copies(src_ref, out_ref, send_sems, recv_sems, loc_sem):
                cp.start()

        lane = _iota((BLK, BLK), 1)
        qh = []
        for p in range(PAIRS_PER_STEP):
            qh += _head_masks(q_ref[:, p * BLK:(p + 1) * BLK] * SB_SCALE, lane)
        valid2, valid1 = _window_masks(qi)

        def tiles(rows, valid, tri, runs):
            ks = [kb_ref[rows, _pair_lanes(h)] for h in heads]
            vs = [vb_ref[rows, _pair_lanes(h)] for h in heads]
            parts = [_sb_tile(jnp.where(valid, _dot(qh[h], ks[h], NT), SB_MASKED)) for h in heads]
            afters = [_suffix_sum(parts[h][1], tri) for h in heads]
            if runs is not None:
                afters = [afters[h] + runs[h] for h in heads]
            avals = [jnp.exp(parts[h][2] + afters[h]) for h in heads]
            his = [avals[h].astype(BF16) for h in heads]
            accs = [_dot(his[h], vs[h], NN) for h in heads]
            rests = [_dot((avals[h] - his[h].astype(F32)).astype(BF16), vs[h], NN) for h in heads]
            return [(jnp.sum(parts[h][1], axis=1, keepdims=True), accs[h], rests[h]) for h in heads]

        win = pl.ds(pl.multiple_of(qi * BLK, BLK), 2 * BLK)
        init = [t for head in tiles(win, valid2, _strict_upper(2 * BLK), None) for t in head]
        tri1 = _strict_upper(BLK)

        def step(kblk, carry):
            rows = pl.ds(pl.multiple_of((kblk + 1) * BLK, BLK), BLK)
            runs = [carry[3 * h] for h in heads]
            new = []
            for h, (d_run, d_acc, d_rest) in enumerate(tiles(rows, valid1(kblk), tri1, runs)):
                new += [carry[3 * h] + d_run, carry[3 * h + 1] + d_acc, carry[3 * h + 2] + d_rest]
            return tuple(new)

        res = _sweep(qi - 2, step, init, tuple(3 * h for h in heads))
        for p in range(PAIRS_PER_STEP):
            o = jnp.where(lane < HEAD_DIM, res[6 * p + 1], res[6 * p + 4])
            o_ref[:, p * BLK:(p + 1) * BLK] = o
            ox_ref[:, p * BLK:(p + 1) * BLK] = o + jnp.where(lane < HEAD_DIM, res[6 * p + 2], res[6 * p + 5])

        @pl.when(jnp.logical_and(grp == n_groups - 1, qi == NBLK - 1))
        def _():
            for cp in _all_gather_copies(src_ref, out_ref, send_sems, recv_sems, loc_sem):
                cp.wait()

    act = jax.ShapeDtypeStruct((LP, 1024), F32)
    blk = lambda col: pl.BlockSpec((BLK, STEP_W), lambda g, qi: (qi, col + g))
    whole = lambda col: pl.BlockSpec((LP, STEP_W), lambda g, qi: (0, col + g), pipeline_mode=pl.Buffered(1))
    return pl.pallas_call(
        body, name="sb_attn_fwd",
        out_shape=(act, act, jax.ShapeDtypeStruct((N_DEV,) + w_out_shard.shape, w_out_shard.dtype)),
        grid=(n_groups, NBLK),
        in_specs=[blk(COL_Q // STEP_W), whole(COL_K // STEP_W), whole(COL_V // STEP_W), _ANY],
        out_specs=(blk(0), blk(0), _ANY),
        scratch_shapes=[pltpu.VMEM((LP + BLK, STEP_W), BF16), pltpu.VMEM((LP + BLK, STEP_W), BF16)] + _ALL_GATHER_SEMS,
        compiler_params=_params(("arbitrary", "arbitrary")),
    )(proj, proj, proj, w_out_shard)


def _sb_attention_bwd(proj, o_sb, do_sb, chip_sums):
    heads = range(STEP_HEADS)
    n_groups = N_HEADS // STEP_HEADS

    def body(q_ref, k_ref, v_ref, o_ref, do_ref, src_ref, dq_ref, dk_ref, dv_ref, out_ref,
             kb_ref, vb_ref, dka_ref, dva_ref, send_sems, recv_sems, loc_sem):
        grp, qi = pl.program_id(0), pl.program_id(1)
        _stage_kv(qi, k_ref, v_ref, kb_ref, vb_ref)

        @pl.when(jnp.logical_and(grp == 0, qi == 0))
        def _():
            for cp in _chip_exchange_copies(src_ref, out_ref, send_sems, recv_sems, loc_sem):
                cp.start()

        @pl.when(qi == 0)
        def _():
            dka_ref[...] = jnp.zeros_like(dka_ref)
            dva_ref[...] = jnp.zeros_like(dva_ref)

        lane = _iota((BLK, BLK), 1)
        head0 = lane < HEAD_DIM
        qh, doh, dsum = [], [], []
        for p in range(PAIRS_PER_STEP):
            lanes = slice(p * BLK, (p + 1) * BLK)
            qh += _head_masks(q_ref[:, lanes] * SB_SCALE, lane)
            do = do_ref[:, lanes]
            doh += _head_masks(do, lane)
            prod = do.astype(BF16).astype(F32) * o_ref[:, lanes]
            dsum += [jnp.sum(jnp.where(head0, prod, 0.0), axis=1, keepdims=True),
                     jnp.sum(jnp.where(head0, 0.0, prod), axis=1, keepdims=True)]
        valid2, valid1 = _window_masks(qi)

        def tiles(rows, valid, tri, runs, eruns):
            ks = [kb_ref[rows, _pair_lanes(h)] for h in heads]
            vs = [vb_ref[rows, _pair_lanes(h)] for h in heads]
            parts = [_sb_tile(jnp.where(valid, _dot(qh[h], ks[h], NT), SB_MASKED)) for h in heads]
            afters = [_suffix_sum(parts[h][1], tri) for h in heads]
            if runs is not None:
                afters = [afters[h] + runs[h] for h in heads]
            avals = [jnp.exp(parts[h][2] + afters[h]) for h in heads]
            es = [avals[h] * _dot(doh[h], vs[h], NT) for h in heads]
            esufs = [_suffix_sum(es[h], tri) for h in heads]
            if eruns is not None:
                esufs = [esufs[h] + eruns[h] for h in heads]
            dzs = [(es[h] - parts[h][0] * (dsum[h] - esufs[h])).astype(BF16) for h in heads]
            dqs = [_dot(dzs[h], ks[h], NN) for h in heads]
            dks = [_dot(dzs[h], qh[h], TN) for h in heads]
            dvs = [_dot(avals[h].astype(BF16), doh[h], TN) for h in heads]
            for p in range(PAIRS_PER_STEP):
                dka_ref[rows, p * BLK:(p + 1) * BLK] += dks[2 * p] + dks[2 * p + 1]
                dva_ref[rows, p * BLK:(p + 1) * BLK] += dvs[2 * p] + dvs[2 * p + 1]
            return [(jnp.sum(parts[h][1], axis=1, keepdims=True), jnp.sum(es[h], axis=1, keepdims=True), dqs[h])
                    for h in heads]

        win = pl.ds(pl.multiple_of(qi * BLK, BLK), 2 * BLK)
        init = [t for head in tiles(win, valid2, _strict_upper(2 * BLK), None, None) for t in head]
        tri1 = _strict_upper(BLK)

        def step(kblk, carry):
            rows = pl.ds(pl.multiple_of((kblk + 1) * BLK, BLK), BLK)
            runs = [carry[3 * h] for h in heads]
            eruns = [carry[3 * h + 1] for h in heads]
            new = []
            for h, (d_run, d_erun, d_q) in enumerate(tiles(rows, valid1(kblk), tri1, runs, eruns)):
                new += [carry[3 * h] + d_run, carry[3 * h + 1] + d_erun, carry[3 * h + 2] + d_q]
            return tuple(new)

        res = _sweep(qi - 2, step, init, tuple(3 * h for h in heads))
        for p in range(PAIRS_PER_STEP):
            dq = jnp.where(head0, res[6 * p + 2], res[6 * p + 5]) * SB_SCALE
            dq_ref[:, p * BLK:(p + 1) * BLK] = dq.astype(BF16)

        @pl.when(qi == NBLK - 1)
        def _():
            dk_ref[...] = dka_ref[BLK:, :].astype(BF16)
            dv_ref[...] = dva_ref[BLK:, :].astype(BF16)

        @pl.when(jnp.logical_and(grp == n_groups - 1, qi == NBLK - 1))
        def _():
            for cp in _chip_exchange_copies(src_ref, out_ref, send_sems, recv_sems, loc_sem):
                cp.wait()

    act = jax.ShapeDtypeStruct((LP, 1024), BF16)
    blk = lambda col: pl.BlockSpec((BLK, STEP_W), lambda g, qi: (qi, col + g))
    whole = lambda col: pl.BlockSpec((LP, STEP_W), lambda g, qi: (0, col + g))
    once = lambda col: pl.BlockSpec((LP, STEP_W), lambda g, qi: (0, col + g), pipeline_mode=pl.Buffered(1))
    return pl.pallas_call(
        body, name="sb_attn_bwd",
        out_shape=(act, act, act, jax.ShapeDtypeStruct(chip_sums.shape, chip_sums.dtype)),
        grid=(n_groups, NBLK),
        in_specs=[blk(COL_Q // STEP_W), once(COL_K // STEP_W), once(COL_V // STEP_W), blk(0), blk(0), _ANY],
        out_specs=(blk(0), whole(0), whole(0), _ANY),
        scratch_shapes=[pltpu.VMEM((LP + BLK, STEP_W), BF16), pltpu.VMEM((LP + BLK, STEP_W), BF16),
                        pltpu.VMEM((LP + BLK, STEP_W), F32), pltpu.VMEM((LP + BLK, STEP_W), F32)]
        + _CHIP_EXCHANGE_SEMS,
        compiler_params=_params(("arbitrary", "arbitrary"), BIG_VMEM_LIMIT),
    )(proj, proj, proj, o_sb, do_sb, chip_sums)


def _conv_pre(x, w_ref, b_ref):
    acc = b_ref[...] + w_ref[3:4, :] * x
    for k in range(3):
        acc = acc + w_ref[k:k + 1, :] * pltpu.roll(x, 3 - k, 0)
    return acc


CONV_BLK = 256


def _conv_fwd(proj, conv_w, conv_b):
    def body(x_ref, w_ref, b_ref, o_ref):
        xc = _conv_pre(x_ref[...], w_ref, b_ref)
        o_ref[...] = xc * _sigmoid(xc)

    nb = XBC_W // CONV_BLK
    return pl.pallas_call(
        body, name="conv_fwd", out_shape=jax.ShapeDtypeStruct((LP, XBC_W), F32), grid=(nb,),
        in_specs=[pl.BlockSpec((LP, CONV_BLK), lambda j: (0, COL_XBC // CONV_BLK + j)),
                  pl.BlockSpec((4, CONV_BLK), lambda j: (0, j)), pl.BlockSpec((1, CONV_BLK), lambda j: (0, j))],
        out_specs=pl.BlockSpec((LP, CONV_BLK), lambda j: (0, j)),
        compiler_params=_params(("parallel",)),
    )(proj, conv_w, conv_b)


def _conv_bwd(dxa, proj, conv_w, conv_b):
    def body(d_ref, x_ref, w_ref, b_ref, dx_ref, dw_ref, db_ref):
        x = x_ref[...]
        xc = _conv_pre(x, w_ref, b_ref)
        s = _sigmoid(xc)
        live = _iota((LP, CONV_BLK), 0) >= PAD
        dxc = jnp.where(live, d_ref[...] * (s * (1.0 + xc * (1.0 - s))), 0.0)
        db_ref[...] = jnp.sum(dxc, axis=0, keepdims=True)
        dx = w_ref[3:4, :] * dxc
        dw_ref[3:4, :] = jnp.sum(dxc * x, axis=0, keepdims=True)
        for k in range(3):
            dw_ref[k:k + 1, :] = jnp.sum(dxc * pltpu.roll(x, 3 - k, 0), axis=0, keepdims=True)
            dx = dx + w_ref[k:k + 1, :] * pltpu.roll(dxc, LP - (3 - k), 0)
        dx_ref[...] = dx.astype(BF16)

    nb = XBC_W // CONV_BLK
    col = lambda j: (0, j)
    wide = pl.BlockSpec((LP, CONV_BLK), col)
    return pl.pallas_call(
        body, name="conv_bwd",
        out_shape=(jax.ShapeDtypeStruct((LP, XBC_W), BF16), jax.ShapeDtypeStruct((4, XBC_W), F32),
                   jax.ShapeDtypeStruct((1, XBC_W), F32)),
        grid=(nb,),
        in_specs=[wide, pl.BlockSpec((LP, CONV_BLK), lambda j: (0, COL_XBC // CONV_BLK + j)),
                  pl.BlockSpec((4, CONV_BLK), col), pl.BlockSpec((1, CONV_BLK), col)],
        out_specs=(wide, pl.BlockSpec((4, CONV_BLK), col), pl.BlockSpec((1, CONV_BLK), col)),
        compiler_params=_params(("parallel",)),
    )(dxa, proj, conv_w, conv_b)


def _ssd_prelude(c, dt_ref, dtt_ref, dtb_ref, dtbt_ref, alog_ref, alogt_ref):
    live = jnp.logical_or(c > 0, _iota((BLK, N_HEADS), 0) >= PAD)
    live_t = jnp.logical_or(c > 0, _iota((N_HEADS, BLK), 1) >= PAD)
    pre = dt_ref[...] + dtb_ref[...]
    pre_t = dtt_ref[...] + dtbt_ref[...]
    dt = jnp.where(live, _softplus(pre), 0.0)
    dt_t = jnp.where(live_t, _softplus(pre_t), 0.0)
    a = -jnp.exp(alog_ref[...])
    a_t = -jnp.exp(alogt_ref[...])
    li = _iota((BLK, BLK), 0)
    si = _iota((BLK, BLK), 1)
    lower = (si <= li).astype(BF16)
    upper = (li <= si).astype(BF16)
    acum = _dot_exact_x(lower, dt * a, NN)
    acum_t = _dot_x_exact(dt_t * a_t, upper, NN)
    return live, pre, dt, a, a_t, acum, acum_t


def _head_expand():
    return (_iota((N_HEADS, SSD_W), 1) // HEAD_DIM == _iota((N_HEADS, SSD_W), 0)).astype(BF16)


def _head_reduce_mat():
    return (_iota((SSD_W, N_HEADS), 0) // HEAD_DIM == _iota((SSD_W, N_HEADS), 1)).astype(BF16)


def _decay_mat(acum, acum_t, h, causal):
    seg = jnp.minimum(acum[:, h:h + 1] - acum_t[h:h + 1, :], 0.0)
    return jnp.where(causal, jnp.exp(seg), 0.0)


def _ssd_fwd(xa, dt_raw, dt_raw_t, dt_bias, dt_bias_t, a_log, a_log_t, d_exp):
    def body(x_ref, b_ref, c_ref, dt_ref, dtt_ref, dtb_ref, dtbt_ref, alog_ref, alogt_ref, dexp_ref,
             y_ref, hs_ref, state_ref):
        c = pl.program_id(0)

        @pl.when(c == 0)
        def _():
            state_ref[...] = jnp.zeros_like(state_ref)

        _, _, dt, _, _, acum, acum_t = _ssd_prelude(c, dt_ref, dtt_ref, dtb_ref, dtbt_ref, alog_ref, alogt_ref)
        expand = _head_expand()
        x = x_ref[...]
        xdt = x * _dot_x_exact(dt, expand, n=2)
        exp_a = _dot_x_exact(jnp.exp(acum), expand, n=2)
        to_end = _dot_x_exact(jnp.exp(acum[BLK - 1:BLK, :] - acum), expand, n=2)
        xdt_b = xdt.astype(BF16)
        xd_b = (xdt * to_end).astype(BF16)
        chunk_decay = jnp.exp(acum_t[:, BLK - 1:BLK])
        hs_ref[0] = state_ref[...]
        lane = _iota((BLK, BLK), 1)
        causal = _iota((BLK, BLK), 0) >= lane
        gw = HEADS_PER_GROUP * HEAD_DIM
        for g in range(N_GROUPS):
            bg = b_ref[:, g * N_STATE:(g + 1) * N_STATE].astype(BF16)
            cg = c_ref[:, g * N_STATE:(g + 1) * N_STATE].astype(BF16)
            cb = _dot(cg, bg, NT)
            hg = state_ref[g * gw:(g + 1) * gw, :]
            ch = _dot(cg, hg.astype(BF16), NT)
            st = _dot(xd_b[:, g * gw:(g + 1) * gw], bg, TN)
            for p in range(HEADS_PER_GROUP // 2):
                h0 = g * HEADS_PER_GROUP + 2 * p
                lo = h0 * HEAD_DIM
                xp = xdt_b[:, lo:lo + BLK]
                w0 = (cb * _decay_mat(acum, acum_t, h0, causal)).astype(BF16)
                w1 = (cb * _decay_mat(acum, acum_t, h0 + 1, causal)).astype(BF16)
                yd = jnp.where(lane < HEAD_DIM, _dot(w0, xp), _dot(w1, xp))
                y_ref[:, lo:lo + BLK] = (yd + ch[:, lo - g * gw:lo - g * gw + BLK] * exp_a[:, lo:lo + BLK]
                                         + x[:, lo:lo + BLK] * dexp_ref[:, lo:lo + BLK])
            for r in range(HEADS_PER_GROUP):
                h = g * HEADS_PER_GROUP + r
                state_ref[h * HEAD_DIM:(h + 1) * HEAD_DIM, :] = (
                    hg[r * HEAD_DIM:(r + 1) * HEAD_DIM, :] * chunk_decay[h:h + 1, :]
                    + st[r * HEAD_DIM:(r + 1) * HEAD_DIM, :])

    chunk = lambda width, col=0: pl.BlockSpec((BLK, width), lambda c: (c, col))
    return pl.pallas_call(
        body, name="ssd_fwd",
        out_shape=(jax.ShapeDtypeStruct((LP, SSD_W), F32), jax.ShapeDtypeStruct((NBLK, SSD_W, N_STATE), F32)),
        grid=(NBLK,),
        in_specs=[chunk(SSD_W), chunk(256, 4), chunk(256, 5), chunk(N_HEADS),
                  pl.BlockSpec((N_HEADS, BLK), lambda c: (0, c)), _const_spec((1, N_HEADS)),
                  _const_spec((N_HEADS, 1)), _const_spec((1, N_HEADS)), _const_spec((N_HEADS, 1)),
                  _const_spec((1, SSD_W))],
        out_specs=(chunk(SSD_W), pl.BlockSpec((1, SSD_W, N_STATE), lambda c: (c, 0, 0))),
        scratch_shapes=[pltpu.VMEM((SSD_W, N_STATE), F32)],
        compiler_params=_params(("arbitrary",)),
    )(xa, xa, xa, dt_raw, dt_raw_t, dt_bias, dt_bias_t, a_log, a_log_t, d_exp)


def _ssd_bwd(xa, dt_raw, dt_raw_t, dt_bias, dt_bias_t, a_log, a_log_t, d_exp, hstart, dy):
    def body(x_ref, b_ref, c_ref, dt_ref, dtt_ref, dtb_ref, dtbt_ref, alog_ref, alogt_ref, dexp_ref,
             hs_ref, dy_ref, dxa_ref, ddt_ref, dbias_ref, dalog_ref, dd_ref, dstate_ref):
        step = pl.program_id(0)
        c = NBLK - 1 - step

        @pl.when(step == 0)
        def _():
            dstate_ref[...] = jnp.zeros_like(dstate_ref)
            dbias_ref[...] = jnp.zeros_like(dbias_ref)
            dalog_ref[...] = jnp.zeros_like(dalog_ref)
            dd_ref[...] = jnp.zeros_like(dd_ref)

        live, pre, dt, a, a_t, acum, acum_t = _ssd_prelude(c, dt_ref, dtt_ref, dtb_ref, dtbt_ref,
                                                           alog_ref, alogt_ref)
        expand = _head_expand()
        reduce_m = _head_reduce_mat()
        x = x_ref[...]
        dyv = dy_ref[...]
        dt_e = _dot_x_exact(dt, expand, n=2)
        xdt = x * dt_e
        exp_acum = jnp.exp(acum)
        exp_a = _dot_x_exact(exp_acum, expand, n=2)
        dte = jnp.exp(acum[BLK - 1:BLK, :] - acum)
        to_end = _dot_x_exact(dte, expand, n=2)
        xdt_b = xdt.astype(BF16)
        xd_b = (xdt * to_end).astype(BF16)
        chunk_decay = jnp.exp(acum_t[:, BLK - 1:BLK])
        lane = _iota((BLK, BLK), 1)
        head0 = lane < HEAD_DIM
        causal = _iota((BLK, BLK), 0) >= lane
        gw = HEADS_PER_GROUP * HEAD_DIM
        dm = dyv * exp_a
        dm_b = dm.astype(BF16)
        onehot = lambda h: (_iota((1, N_HEADS), 1) == h).astype(F32)
        onehot_t = lambda h: (_iota((N_HEADS, 1), 0) == h).astype(F32)
        dacum = jnp.zeros((BLK, N_HEADS), F32)
        dacum_t = jnp.zeros((N_HEADS, BLK), F32)
        head_sums = lambda prod, rows: _dot_x_exact(prod, reduce_m[rows, :], n=2)
        dskip_acc = head_sums(dyv * x, slice(0, SSD_W))
        ddt_acc = jnp.zeros((BLK, N_HEADS), F32)
        ddte_acc = jnp.zeros((BLK, N_HEADS), F32)
        dexpa_acc = jnp.zeros((BLK, N_HEADS), F32)
        head_sum = expand
        for g in range(N_GROUPS):
            bg = b_ref[:, g * N_STATE:(g + 1) * N_STATE].astype(BF16)
            cg = c_ref[:, g * N_STATE:(g + 1) * N_STATE].astype(BF16)
            cb = _dot(cg, bg, NT)
            hg = hs_ref[0, g * gw:(g + 1) * gw, :]
            hg_b = hg.astype(BF16)
            dhe = dstate_ref[g * gw:(g + 1) * gw, :]
            dhe_b = dhe.astype(BF16)
            ch = _dot(cg, hg_b, NT)
            dcg = _dot(dm_b[:, g * gw:(g + 1) * gw], hg_b, NN)
            dhs = _dot(dm_b[:, g * gw:(g + 1) * gw], cg, TN)
            dxd = _dot(bg, dhe_b, NT)
            dbg = _dot(xd_b[:, g * gw:(g + 1) * gw], dhe_b, NN)
            dcb = jnp.zeros((BLK, BLK), F32)
            dxdt_g = []
            for p in range(HEADS_PER_GROUP // 2):
                h0 = g * HEADS_PER_GROUP + 2 * p
                lo = h0 * HEAD_DIM
                xp = xdt_b[:, lo:lo + BLK]
                dyp = dyv[:, lo:lo + BLK]
                dyh = (jnp.where(head0, dyp, 0.0).astype(BF16), jnp.where(head0, 0.0, dyp).astype(BF16))
                dxdt_p = jnp.zeros((BLK, BLK), F32)
                for q in range(2):
                    h = h0 + q
                    dec = _decay_mat(acum, acum_t, h, causal)
                    w = cb * dec
                    dw = _dot(dyh[q], xp, NT)
                    t = dw * w
                    dacum = dacum + jnp.sum(t, axis=1, keepdims=True) * onehot(h)
                    dacum_t = dacum_t - jnp.sum(t, axis=0, keepdims=True) * onehot_t(h)
                    dcb = dcb + dw * dec
                    dxdt_p = dxdt_p + _dot(w.astype(BF16), dyh[q], TN)
                sl = slice(lo, lo + BLK)
                gl = slice(lo - g * gw, lo - g * gw + BLK)
                dxdt_p = dxdt_p + dxd[:, gl] * to_end[:, sl]
                dxa_ref[:, sl] = dyp * dexp_ref[:, sl] + dxdt_p * dt_e[:, sl]
                dxdt_g.append(dxdt_p)
            cols = slice(g * gw, (g + 1) * gw)
            ddt_acc = ddt_acc + head_sums(jnp.concatenate(dxdt_g, axis=1) * x[:, cols], cols)
            ddte_acc = ddte_acc + head_sums(dxd * xdt[:, cols], cols)
            dexpa_acc = dexpa_acc + head_sums(dyv[:, cols] * ch, cols)
            dcb_b = dcb.astype(BF16)
            b_col = SSD_W + g * N_STATE
            c_col = SSD_W + (N_GROUPS + g) * N_STATE
            dxa_ref[:, c_col:c_col + N_STATE] = dcg + _dot(dcb_b, bg, NN)
            dxa_ref[:, b_col:b_col + N_STATE] = dbg + _dot(dcb_b, cg, TN)
            prod = dhe * hg
            per_head = jnp.sum(_dot_exact_x(head_sum[:, g * gw:(g + 1) * gw], prod, NN), axis=1, keepdims=True)
            dacum_t = dacum_t + (per_head * chunk_decay) * (_iota((1, BLK), 1) == BLK - 1).astype(F32)
            for r in range(HEADS_PER_GROUP):
                h = g * HEADS_PER_GROUP + r
                rows = slice(h * HEAD_DIM, (h + 1) * HEAD_DIM)
                dstate_ref[rows, :] = (dhs[r * HEAD_DIM:(r + 1) * HEAD_DIM, :]
                                       + dhe[r * HEAD_DIM:(r + 1) * HEAD_DIM, :] * chunk_decay[h:h + 1, :])
        dacum = dacum + dexpa_acc * exp_acum - ddte_acc * dte
        last_row = (_iota((BLK, 1), 0) == BLK - 1).astype(F32)
        dacum = dacum + last_row * jnp.sum(ddte_acc * dte, axis=0, keepdims=True)
        li = _iota((BLK, BLK), 0)
        si = _iota((BLK, BLK), 1)
        upper = (li <= si).astype(BF16)
        lower = (si <= li).astype(BF16)
        dda = _dot_exact_x(upper, dacum, NN)
        dda_t = _dot_x_exact(dacum_t, lower, NN)
        eye = (_iota((N_HEADS, N_HEADS), 0) == _iota((N_HEADS, N_HEADS), 1)).astype(BF16)
        dda = dda + _dot_x_exact_tn(dda_t, eye)
        ddt = ddt_acc + dda * a
        dalog_ref[...] += jnp.sum(dda * dt, axis=0, keepdims=True) * a
        dd_ref[...] += jnp.sum(dskip_acc, axis=0, keepdims=True)
        ddt_raw = jnp.where(live, ddt * _sigmoid(pre), 0.0)
        ddt_ref[...] = jnp.zeros_like(ddt_ref)
        ddt_ref[:, :N_HEADS] = ddt_raw
        dbias_ref[...] += jnp.sum(ddt_raw, axis=0, keepdims=True)

    rev = lambda width, col=0: pl.BlockSpec((BLK, width), lambda s: (NBLK - 1 - s, col))
    vec = jax.ShapeDtypeStruct((1, N_HEADS), F32)
    return pl.pallas_call(
        body, name="ssd_bwd",
        out_shape=(jax.ShapeDtypeStruct((LP, XBC_W), F32), jax.ShapeDtypeStruct((LP, BLK), F32), vec, vec, vec),
        grid=(NBLK,),
        in_specs=[rev(SSD_W), rev(256, 4), rev(256, 5), rev(N_HEADS),
                  pl.BlockSpec((N_HEADS, BLK), lambda s: (0, NBLK - 1 - s)), _const_spec((1, N_HEADS)),
                  _const_spec((N_HEADS, 1)), _const_spec((1, N_HEADS)), _const_spec((N_HEADS, 1)),
                  _const_spec((1, SSD_W)),
                  pl.BlockSpec((1, SSD_W, N_STATE), lambda s: (NBLK - 1 - s, 0, 0)), rev(SSD_W)],
        out_specs=(rev(XBC_W), rev(BLK), _const_spec((1, N_HEADS)),
                   _const_spec((1, N_HEADS)), _const_spec((1, N_HEADS))),
        scratch_shapes=[pltpu.VMEM((SSD_W, N_STATE), F32)],
        compiler_params=_params(("arbitrary",)),
    )(xa, xa, xa, dt_raw, dt_raw_t, dt_bias, dt_bias_t, a_log, a_log_t, d_exp, hstart, dy)


def _dot_x_exact_tn(x_t, eye):
    out = None
    for p in _split(x_t, 3):
        t = _dot(p, eye, TN)
        out = t if out is None else out + t
    return out


def _adamw(name, parts, w, m, v, rows):
    r_all, cols = w.shape
    assert r_all % rows == 0
    c1 = 1.0 / (1.0 - ADAM_B1 ** ADAM_STEP)
    c2 = 1.0 / (1.0 - ADAM_B2 ** ADAM_STEP)

    def body(p_ref, w_ref, m_ref, v_ref, g_ref, d_ref, mo_ref, vo_ref):
        g = p_ref[0].astype(F32)
        for j in range(1, parts.shape[0]):
            g = g + p_ref[j].astype(F32)
        mn = ADAM_B1 * m_ref[...] + (1.0 - ADAM_B1) * g
        vn = ADAM_B2 * v_ref[...] + (1.0 - ADAM_B2) * (g * g)
        g_ref[...] = g
        mo_ref[...] = mn
        vo_ref[...] = vn
        d_ref[...] = -ADAM_LR * ((mn * c1) / (jnp.sqrt(vn * c2) + ADAM_EPS) + ADAM_WD * w_ref[...])

    spec = pl.BlockSpec((rows, cols), lambda i: (i, 0))
    shp = jax.ShapeDtypeStruct((r_all, cols), F32)
    return pl.pallas_call(
        body, name=name, out_shape=(shp, shp, shp, shp), grid=(r_all // rows,),
        in_specs=[pl.BlockSpec((parts.shape[0], rows, cols), lambda i: (0, i, 0)), spec, spec, spec],
        out_specs=(spec, spec, spec, spec),
        compiler_params=_params(("parallel",)),
    )(parts, w, m, v)


_VECTORS = (("norm_w", 1024, 0), ("conv_b", 1536, 8), ("dt_bias", 16, 20), ("a_log", 16, 21), ("d_skip", 16, 22),
            ("sb_norm_w", 1024, 24), ("ssd_norm_w", 1024, 32), ("final_norm_w", 1024, 40))
_LOSS_ROW = 23
_CONVW_ROW = 48
_META_ROW = 96
_PACK_ROWS = 224
_SMALL_ORDER = tuple(name for name, _, _ in _VECTORS) + ("conv_w", "meta_tokens")


def _pack_small_grads(vectors, loss_row, d_convw, dh):
    def body(*refs):
        vec_refs, (loss_ref, cw_ref, dh_ref, out_ref) = refs[:len(_VECTORS)], refs[len(_VECTORS):]
        out_ref[...] = jnp.zeros_like(out_ref)
        for (_, width, row), ref in zip(_VECTORS, vec_refs):
            if width < BLK:
                out_ref[row:row + 1, :width] = ref[...]
            else:
                for t in range(width // BLK):
                    out_ref[row + t:row + t + 1, :] = ref[:, t * BLK:(t + 1) * BLK]
        out_ref[_LOSS_ROW:_LOSS_ROW + 1, :] = loss_ref[...]
        for k in range(4):
            for t in range(XBC_W // BLK):
                r = _CONVW_ROW + k * (XBC_W // BLK) + t
                out_ref[r:r + 1, :] = cw_ref[k:k + 1, t * BLK:(t + 1) * BLK]
        for i in range(N_META):
            for t in range(D_MODEL // BLK):
                r = _META_ROW + i * (D_MODEL // BLK) + t
                out_ref[r:r + 1, :] = dh_ref[i:i + 1, t * BLK:(t + 1) * BLK]

    full = lambda a: pl.BlockSpec(a.shape, lambda i: tuple(0 for _ in a.shape))
    return pl.pallas_call(
        body, name="pack_small_grads", out_shape=jax.ShapeDtypeStruct((_PACK_ROWS, BLK), F32), grid=(1,),
        in_specs=[full(v) for v in vectors] + [full(loss_row), full(d_convw),
                                               pl.BlockSpec((N_META, D_MODEL), lambda i: (PAD // N_META, 0))],
        out_specs=pl.BlockSpec((_PACK_ROWS, BLK), lambda i: (0, 0)),
        compiler_params=_params(("arbitrary",)),
    )(*vectors, loss_row, d_convw, dh)


def _sum_slots(name, parts, rows):
    n, r_all, cols = parts.shape

    def body(p_ref, o_ref):
        acc = p_ref[0].astype(F32)
        for j in range(1, n):
            acc = acc + p_ref[j].astype(F32)
        o_ref[...] = acc

    return pl.pallas_call(
        body, name=name, out_shape=jax.ShapeDtypeStruct((r_all, cols), F32), grid=(r_all // rows,),
        in_specs=[pl.BlockSpec((n, rows, cols), lambda i: (0, i, 0))],
        out_specs=pl.BlockSpec((rows, cols), lambda i: (i, 0)),
        compiler_params=_params(("parallel",)),
    )(parts)


def _adamw_small(pack, weights, moms, vels):
    c1 = 1.0 / (1.0 - ADAM_B1 ** ADAM_STEP)
    c2 = 1.0 / (1.0 - ADAM_B2 ** ADAM_STEP)
    n = len(_SMALL_ORDER)

    def body(*refs):
        p_ref = refs[0]
        w_refs, m_refs, v_refs = refs[1:1 + n], refs[1 + n:1 + 2 * n], refs[1 + 2 * n:1 + 3 * n]
        outs = refs[1 + 3 * n:1 + 7 * n]
        loss_ref, g_ref, cw_ref, cws_ref, mt_ref, all_ref, send_sems, recv_sems, loc_sem = refs[1 + 7 * n:]
        x, y, c = _mesh_position()
        me = 4 * x + 2 * y + c
        copies = _all_gather_copies(p_ref, all_ref, send_sems, recv_sems, loc_sem)
        for cp in copies:
            cp.start()
        for cp in copies:
            cp.wait()
        g = all_ref[0]
        for j in range(1, N_DEV):
            g = g + all_ref[j]
        g_ref[...] = g
        loss_ref[...] = g_ref[_LOSS_ROW:_LOSS_ROW + 1, :]

        def update(idx, grad):
            go_ref, d_ref, mo_ref, vo_ref = outs[4 * idx:4 * idx + 4]
            mn = ADAM_B1 * m_refs[idx][...] + (1.0 - ADAM_B1) * grad
            vn = ADAM_B2 * v_refs[idx][...] + (1.0 - ADAM_B2) * (grad * grad)
            go_ref[...] = grad
            mo_ref[...] = mn
            vo_ref[...] = vn
            d_ref[...] = -ADAM_LR * ((mn * c1) / (jnp.sqrt(vn * c2) + ADAM_EPS) + ADAM_WD * w_refs[idx][...])

        for idx, (_, width, row) in enumerate(_VECTORS):
            go_ref = outs[4 * idx]
            if width < BLK:
                grad = g_ref[row:row + 1, :width]
            else:
                for t in range(width // BLK):
                    go_ref[:, t * BLK:(t + 1) * BLK] = g_ref[row + t:row + t + 1, :]
                grad = go_ref[...]
            update(idx, grad)
        cw_ref[...] = jnp.zeros_like(cw_ref)
        for k in range(4):
            for t in range(XBC_W // BLK):
                r = _CONVW_ROW + k * (XBC_W // BLK) + t
                cw_ref[k:k + 1, t * BLK:(t + 1) * BLK] = g_ref[r:r + 1, :]
        for i in range(N_META):
            for t in range(D_MODEL // BLK):
                r = _META_ROW + i * (D_MODEL // BLK) + t
                mt_ref[i:i + 1, t * BLK:(t + 1) * BLK] = g_ref[r:r + 1, :]
        width_cw = XBC_W // N_DEV
        pick_cw = (_iota((XBC_W, width_cw), 0) == me * width_cw + _iota((XBC_W, width_cw), 1)).astype(BF16)
        cws_ref[...] = _dot_x_exact(cw_ref[...], pick_cw)
        update(n - 2, cws_ref[0:4, :])
        pick_mt = (_iota((D_MODEL, BLK), 0) == me * BLK + _iota((D_MODEL, BLK), 1)).astype(BF16)
        update(n - 1, _dot_x_exact(mt_ref[...], pick_mt))

    full = lambda a: pl.BlockSpec(a.shape, lambda i: tuple(0 for _ in a.shape))
    params = list(weights) + list(moms) + list(vels)
    out_shape, out_specs = [], []
    for w in weights:
        for _ in range(4):
            out_shape.append(jax.ShapeDtypeStruct(w.shape, F32))
            out_specs.append(full(w))
    out_shape.append(jax.ShapeDtypeStruct((1, BLK), F32))
    out_specs.append(pl.BlockSpec((1, BLK), lambda i: (0, 0)))
    return pl.pallas_call(
        body, name="adamw_small", out_shape=tuple(out_shape), grid=(1,),
        in_specs=[full(pack)] + [full(a) for a in params], out_specs=tuple(out_specs),
        scratch_shapes=[pltpu.VMEM((_PACK_ROWS, BLK), F32), pltpu.VMEM((8, XBC_W), F32),
                        pltpu.VMEM((8, XBC_W // N_DEV), F32), pltpu.VMEM((N_META, D_MODEL), F32),
                        pltpu.VMEM((N_DEV, _PACK_ROWS, BLK), F32)] + _ALL_GATHER_SEMS,
        compiler_params=_params(("arbitrary",)),
    )(pack, *params)


def kernel(x, meta_tokens, norm_w, w_in, conv_w, conv_b, dt_bias, a_log, d_skip, sb_norm_w, ssd_norm_w, w_out, final_norm_w, loss_target, m_meta_tokens, m_norm_w, m_w_in, m_conv_w, m_conv_b, m_dt_bias, m_a_log, m_d_skip, m_sb_norm_w, m_ssd_norm_w, m_w_out, m_final_norm_w, v_meta_tokens, v_norm_w, v_w_in, v_conv_w, v_conv_b, v_dt_bias, v_a_log, v_d_skip, v_sb_norm_w, v_ssd_norm_w, v_w_out, v_final_norm_w):
    small_src = jnp.concatenate([conv_w[0].reshape(6, BLK), meta_tokens, jnp.zeros((2, BLK), F32)], axis=0)
    small_g, w_in_g = _gather_weights([small_src, w_in[0].astype(BF16)])
    w_in_full = w_in_g.transpose(1, 0, 2).reshape(D_MODEL, D_IN)
    w_dt = w_in_full[:, D_MAIN:]
    conv_w_full = small_g[:, :6].reshape(N_DEV, 4, 192).transpose(1, 0, 2).reshape(4, XBC_W)
    meta_full = small_g[:, 6:6 + N_META].transpose(1, 0, 2).reshape(N_META, D_MODEL)
    h_pad = jnp.concatenate([jnp.zeros((PAD, D_MODEL), F32), meta_full, x[0]], axis=0)
    dt_bias_t = dt_bias.reshape(N_HEADS, 1)
    a_log_t = a_log.reshape(N_HEADS, 1)
    d_exp = jnp.repeat(d_skip, HEAD_DIM, axis=1)
    fnw = final_norm_w.reshape(1, D_MODEL)

    u, u_t, dt_raw, dt_raw_t = _prenorm(h_pad, norm_w, w_dt, w_dt.T)
    proj = _matmul("in_proj", u, w_in_full, "nn", LP, 512, D_MODEL, n_cols=D_MAIN)
    o_sb, o_sb_exact, w_out_g = _sb_attention_fwd(proj, w_out[0].astype(BF16))
    w_out_full = w_out_g.reshape(2 * SSD_W, D_MODEL)
    xa = _conv_fwd(proj, conv_w_full, conv_b)
    o_ssd, hstart = _ssd_fwd(xa, dt_raw, dt_raw_t, dt_bias, dt_bias_t, a_log, a_log_t, d_exp)
    ycat, dh2, dh2_b, loss_row, d_fnw = _out_head(
        o_sb, proj, o_ssd, sb_norm_w, ssd_norm_w, w_out_full, h_pad, fnw, loss_target[0])

    g_w_out = _matmul("d_w_out", ycat, dh2_b, "tn", 512, 512, LP, BF16).reshape(4, 2, 256, D_MODEL)
    do_sb, dg, do_ssd, dz, d_sbw, d_ssdw, sib_w_out = _ycat_bwd(
        dh2_b, w_out_full, o_sb, proj, o_ssd, sb_norm_w, ssd_norm_w, g_w_out)
    chip_w_out = _pair_sum("pair_sum_w_out", g_w_out, sib_w_out, 256)
    dq, dk, dv, p_w_out = _sb_attention_bwd(proj, o_sb_exact, do_sb, chip_w_out)
    dxa, ddt_raw, d_dtb, d_alog, d_dskip = _ssd_bwd(
        xa, dt_raw, dt_raw_t, dt_bias, dt_bias_t, a_log, a_log_t, d_exp, hstart, do_ssd)
    dxbc, d_convw, d_convb = _conv_bwd(dxa, proj, conv_w_full, conv_b)
    pieces = [dq, dk, dv, dg, dz, dxbc, ddt_raw]
    g_w_in_mine, g_w_in_sib = _d_w_in(u_t, pieces)
    chip_w_in = _add_halves("pair_sum_w_in", g_w_in_mine, g_w_in_sib, 128)
    dh, d_nw, win_parts = _d_u_prenorm_bwd(pieces, w_in_full, w_dt, h_pad, norm_w, dh2, chip_w_in)
    win_mine = _sum_slots("sum_w_in_windows", win_parts, 128)
    win_other = _swap_with_sibling("swap_w_in_window", win_mine, whole=True)
    core = lax.axis_index("c")
    chip = 2 * lax.axis_index("x") + lax.axis_index("y")
    first_col = (D_IN // N_DEV) * (2 * chip + core) - WIN_STRIDE * chip
    cut = lambda w: lax.dynamic_slice(w, (0, first_col), (D_MODEL // 2, D_IN // N_DEV))
    half_mine, half_other = cut(win_mine), cut(win_other)
    p_w_in = jnp.concatenate([jnp.where(core == 0, half_mine, half_other),
                              jnp.where(core == 0, half_other, half_mine)], axis=0)[None]

    pack = _pack_small_grads([d_nw, d_convb, d_dtb, d_alog, d_dskip, d_sbw, d_ssdw, d_fnw], loss_row, d_convw, dh)

    res_in = _adamw("adamw_w_in", p_w_in, w_in[0], m_w_in[0], v_w_in[0], 128)
    res_out = _adamw("adamw_w_out", p_w_out, w_out[0], m_w_out[0], v_w_out[0], 128)
    res_small = _adamw_small(
        pack,
        [norm_w, conv_b, dt_bias, a_log, d_skip, sb_norm_w, ssd_norm_w, fnw, conv_w[0], meta_tokens],
        [m_norm_w, m_conv_b, m_dt_bias, m_a_log, m_d_skip, m_sb_norm_w, m_ssd_norm_w,
         m_final_norm_w.reshape(1, D_MODEL), m_conv_w[0], m_meta_tokens],
        [v_norm_w, v_conv_b, v_dt_bias, v_a_log, v_d_skip, v_sb_norm_w, v_ssd_norm_w,
         v_final_norm_w.reshape(1, D_MODEL), v_conv_w[0], v_meta_tokens])

    loss = jnp.sum(res_small[-1])
    order = ["meta_tokens", "norm_w", "w_in", "conv_w", "conv_b", "dt_bias", "a_log", "d_skip",
             "sb_norm_w", "ssd_norm_w", "w_out", "final_norm_w"]
    outs = [loss, dh[BLK:].reshape(1, SEQ, D_MODEL)]
    for kind in range(4):
        small = {name: res_small[4 * idx + kind] for idx, name in enumerate(_SMALL_ORDER)}
        small["final_norm_w"] = small["final_norm_w"].reshape(D_MODEL)
        small["conv_w"] = small["conv_w"].reshape(1, 4, XBC_W // N_DEV)
        small["w_in"] = res_in[kind].reshape(1, D_MODEL, D_IN // N_DEV)
        small["w_out"] = res_out[kind].reshape(1, 256, D_MODEL)
        outs += [small[name] for name in order]
    return tuple(outs)
```

```python
import jax
import jax.numpy as jnp
from jax import lax
from jax.experimental import pallas as pl
from jax.experimental.pallas import tpu as pltpu

F32 = jnp.float32
BF16 = jnp.bfloat16

D_MODEL = 1024
SEQ = 2048
N_META = 16
BLK = 128
PAD = BLK - N_META
LP = PAD + N_META + SEQ
NBLK = LP // BLK
N_HEADS = 16
HEAD_DIM = 64
N_GROUPS = 2
HEADS_PER_GROUP = 8
N_STATE = 128
SSD_W = 1024
XBC_W = 1536
D_MAIN = 6656
D_IN = 6672
COL_Q, COL_K, COL_V, COL_G, COL_Z, COL_XBC = 0, 1024, 2048, 3072, 4096, 5120
N_DEV = 8
EPS = 1e-5
SB_SCALE = 0.125
SB_DEAD = -87.4
SB_MASKED = -1e30

ADAM_LR = 0.001
ADAM_B1 = 0.9
ADAM_B2 = 0.999
ADAM_EPS = 1e-08
ADAM_WD = 0.01
ADAM_STEP = 10

VMEM_LIMIT = 48 * 1024 * 1024
BIG_VMEM_LIMIT = 56 * 1024 * 1024

NN = (((1,), (0,)), ((), ()))
NT = (((1,), (1,)), ((), ()))
TN = (((0,), (0,)), ((), ()))


def _dot(a, b, dims=NN):
    return lax.dot_general(a, b, dims, preferred_element_type=F32)


def _split(x, n):
    parts = []
    r = x
    for i in range(n):
        p = r.astype(BF16)
        parts.append(p)
        if i + 1 < n:
            r = r - p.astype(F32)
    return parts


def _dot_x_exact(x, m, dims=NN, n=3):
    out = None
    for p in _split(x, n):
        t = _dot(p, m, dims)
        out = t if out is None else out + t
    return out


def _dot_exact_x(m, x, dims=NN, n=3):
    out = None
    for p in _split(x, n):
        t = _dot(m, p, dims)
        out = t if out is None else out + t
    return out


def _iota(shape, dim):
    return lax.broadcasted_iota(jnp.int32, shape, dim)


def _softplus(x):
    return jnp.maximum(x, 0.0) + jnp.log(1.0 + jnp.exp(-jnp.abs(x)))


def _sigmoid(x):
    return 1.0 / (1.0 + jnp.exp(-x))


def _params(sem=None, vmem=None):
    return pltpu.CompilerParams(dimension_semantics=sem, vmem_limit_bytes=vmem or VMEM_LIMIT)


_ANY = pl.BlockSpec(memory_space=pl.ANY)
_MESH = pl.DeviceIdType.MESH


def _mesh_position():
    return lax.axis_index("x"), lax.axis_index("y"), lax.axis_index("c")


def _other_chips(x, y):
    return [(1 - x, y), (x, 1 - y), (1 - x, 1 - y)]


def _gather_weights(srcs):
    n = len(srcs)

    def body(*refs):
        src, out = refs[:n], refs[n:2 * n]
        send_sems, recv_sems, loc_sems = refs[2 * n:]
        x, y, c = _mesh_position()
        sibling = (x, y, 1 - c)
        chips = _other_chips(x, y)
        relay_from = (jnp.where(c == 0, 1 - x, x), jnp.where(c == 0, y, 1 - y))
        relay_to = (jnp.where(c == 0, x, 1 - x), jnp.where(c == 0, 1 - y, y))

        def copy(a, k, block, to, from_src=False):
            slot = out[a].at[4 * block[0] + 2 * block[1] + block[2]]
            return pltpu.make_async_remote_copy(
                src_ref=src[a] if from_src else slot, dst_ref=slot,
                send_sem=send_sems.at[7 * a + k], recv_sem=recv_sems.at[7 * a + k],
                device_id=to, device_id_type=_MESH)

        local, sends = [], []
        for a in range(n):
            mine = pltpu.make_async_copy(src[a], out[a].at[4 * x + 2 * y + c], loc_sems.at[a])
            mine.start()
            local.append(mine)
            first = [copy(a, 0, (x, y, c), sibling, True)]
            first += [copy(a, 1 + j, (x, y, c), (*chip, c), True) for j, chip in enumerate(chips[:2])]
            for cp in first:
                cp.start()
            sends += first
        for a in range(n):
            for j, chip in enumerate(chips[:2]):
                copy(a, 1 + j, (*chip, c), (x, y, c)).wait_recv()
            later = [copy(a, 3, (*relay_from, c), (*relay_to, c))]
            later += [copy(a, 4 + j, (*chip, c), sibling) for j, chip in enumerate(chips[:2])]
            for cp in later:
                cp.start()
            sends += later
        for a in range(n):
            copy(a, 3, (*chips[2], c), (x, y, c)).wait_recv()
            passed = copy(a, 6, (*chips[2], c), sibling)
            passed.start()
            sends.append(passed)
        for a in range(n):
            copy(a, 0, (x, y, 1 - c), (x, y, c)).wait_recv()
            for j, chip in enumerate(chips):
                copy(a, 4 + j, (*chip, 1 - c), (x, y, c)).wait_recv()
        for cp in sends:
            cp.wait_send()
        for cp in local:
            cp.wait()

    return pl.pallas_call(
        body, name="gather_weights",
        out_shape=tuple(jax.ShapeDtypeStruct((N_DEV,) + s.shape, s.dtype) for s in srcs),
        in_specs=[_ANY] * n, out_specs=tuple([_ANY] * n),
        scratch_shapes=[pltpu.SemaphoreType.DMA((7 * n,)), pltpu.SemaphoreType.DMA((7 * n,)),
                        pltpu.SemaphoreType.DMA((n,))],
    )(*srcs)


_ALL_GATHER_SEMS = [pltpu.SemaphoreType.DMA((7,)), pltpu.SemaphoreType.DMA((7,)), pltpu.SemaphoreType.DMA]


def _all_gather_copies(src, out, send_sems, recv_sems, loc_sem):
    x, y, c = _mesh_position()
    me = 4 * x + 2 * y + c
    copies = [pltpu.make_async_copy(src, out.at[me], loc_sem)]
    for k in range(1, N_DEV):
        peer = (1 - x if k & 4 else x, 1 - y if k & 2 else y, 1 - c if k & 1 else c)
        copies.append(pltpu.make_async_remote_copy(
            src_ref=src, dst_ref=out.at[me], send_sem=send_sems.at[k - 1], recv_sem=recv_sems.at[k - 1],
            device_id=peer, device_id_type=_MESH))
    return copies


def _sibling_swap_copies(src, out, send_sems, recv_sems, n):
    x, y, c = _mesh_position()
    return [pltpu.make_async_remote_copy(
        src_ref=src.at[k, 1 - c], dst_ref=out.at[k],
        send_sem=send_sems.at[k], recv_sem=recv_sems.at[k], device_id=(x, y, 1 - c), device_id_type=_MESH)
        for k in range(n)]


def _pair_sum(name, g, sib, rows):
    n, _, r_all, cols = g.shape
    assert r_all % rows == 0

    def body(g0_ref, g1_ref, s_ref, o_ref):
        c = lax.axis_index("c")
        mine = jnp.where(c == 0, g0_ref[0, 0].astype(F32), g1_ref[0, 0].astype(F32))
        o_ref[0] = (mine + s_ref[0].astype(F32)).astype(o_ref.dtype)

    return pl.pallas_call(
        body, name=name, out_shape=jax.ShapeDtypeStruct((n, r_all, cols), BF16), grid=(n, r_all // rows),
        in_specs=[pl.BlockSpec((1, 1, rows, cols), lambda k, i: (k, 0, i, 0)),
                  pl.BlockSpec((1, 1, rows, cols), lambda k, i: (k, 1, i, 0)),
                  pl.BlockSpec((1, rows, cols), lambda k, i: (k, i, 0))],
        out_specs=pl.BlockSpec((1, rows, cols), lambda k, i: (k, i, 0)),
        compiler_params=_params(("parallel", "parallel")),
    )(g, g, sib)


_CHIP_EXCHANGE_SEMS = [pltpu.SemaphoreType.DMA((3,)), pltpu.SemaphoreType.DMA((3,)), pltpu.SemaphoreType.DMA]


def _chip_exchange_copies(src, out, send_sems, recv_sems, loc_sem, window=None):
    x, y, c = _mesh_position()
    here = 2 * x + y

    def slot(k):
        if window is not None:
            return src.at[:, pl.ds(pl.multiple_of(k * window[0], BLK), window[1])]
        return src.at[k]

    copies = [pltpu.make_async_copy(slot(here), out.at[here], loc_sem)]
    for j, chip in enumerate(_other_chips(x, y)):
        copies.append(pltpu.make_async_remote_copy(
            src_ref=slot(2 * chip[0] + chip[1]), dst_ref=out.at[here],
            send_sem=send_sems.at[j], recv_sem=recv_sems.at[j], device_id=(*chip, c), device_id_type=_MESH))
    return copies


def _matmul(name, a, b, kind, tm, tn, tk, out_dtype=F32, n_cols=None):
    if kind == "nn":
        (m, kk), nn_ = a.shape, (n_cols or b.shape[1])
        a_spec = pl.BlockSpec((tm, tk), lambda i, j, k: (i, k))
        b_spec = pl.BlockSpec((tk, tn), lambda i, j, k: (k, j))
        dims = NN
    else:
        (kk, m), (_, nn_) = a.shape, b.shape
        a_spec = pl.BlockSpec((tk, tm), lambda i, j, k: (k, i))
        b_spec = pl.BlockSpec((tk, tn), lambda i, j, k: (k, j))
        dims = TN
    assert m % tm == 0 and nn_ % tn == 0 and kk % tk == 0
    nk = kk // tk

    def body(a_ref, b_ref, o_ref, acc_ref):
        k = pl.program_id(2)
        part = _dot(a_ref[...], b_ref[...], dims)
        if nk == 1:
            o_ref[...] = part.astype(o_ref.dtype)
        else:
            @pl.when(k == 0)
            def _():
                acc_ref[...] = part

            @pl.when(k > 0)
            def _():
                acc_ref[...] += part

            @pl.when(k == nk - 1)
            def _():
                o_ref[...] = acc_ref[...].astype(o_ref.dtype)

    return pl.pallas_call(
        body, name=name, out_shape=jax.ShapeDtypeStruct((m, nn_), out_dtype),
        grid=(m // tm, nn_ // tn, nk),
        in_specs=[a_spec, b_spec], out_specs=pl.BlockSpec((tm, tn), lambda i, j, k: (i, j)),
        scratch_shapes=[pltpu.VMEM((tm, tn) if nk > 1 else (8, 128), F32)],
        compiler_params=_params(("parallel", "parallel", "arbitrary")),
    )(a, b)


def _row_spec(width, col=0):
    return pl.BlockSpec((BLK, width), lambda i: (i, col))


def _const_spec(shape):
    return pl.BlockSpec(shape, lambda i: tuple(0 for _ in shape))


def _prenorm(h_pad, norm_w, wdt, wdt_t):
    def body(h_ref, w_ref, wdt_ref, wdtt_ref, u_ref, ut_ref, dt_ref, dtt_ref):
        xv = h_ref[...]
        r = lax.rsqrt(jnp.mean(xv * xv, axis=-1, keepdims=True) + EPS)
        uf = xv * r * w_ref[...]
        u = uf.astype(BF16)
        u_ref[...] = u
        ut_ref[...] = uf.T.astype(BF16)
        dt_ref[...] = _dot(u, wdt_ref[...], NN)
        dtt_ref[...] = _dot(wdtt_ref[...], u, NT)

    return pl.pallas_call(
        body, name="prenorm",
        out_shape=(jax.ShapeDtypeStruct((LP, D_MODEL), BF16), jax.ShapeDtypeStruct((D_MODEL, LP), BF16),
                   jax.ShapeDtypeStruct((LP, N_HEADS), F32), jax.ShapeDtypeStruct((N_HEADS, LP), F32)),
        grid=(NBLK,),
        in_specs=[_row_spec(D_MODEL), _const_spec((1, D_MODEL)), _const_spec((D_MODEL, N_HEADS)),
                  _const_spec((N_HEADS, D_MODEL))],
        out_specs=(_row_spec(D_MODEL), pl.BlockSpec((D_MODEL, BLK), lambda i: (0, i)), _row_spec(N_HEADS),
                   pl.BlockSpec((N_HEADS, BLK), lambda i: (0, i))),
        compiler_params=_params(("parallel",)),
    )(h_pad, norm_w, wdt, wdt_t)


def _gated_norm(o, g, w):
    a = o * (g * _sigmoid(g))
    r = lax.rsqrt(jnp.mean(a * a, axis=-1, keepdims=True) + EPS)
    return a * r * w


def _out_head(o_sb, proj, o_ssd, sb_w, ssd_w, w_out_full, h_pad, fnw, target):
    tm = LP // 8

    def body(osb_ref, g_ref, ossd_ref, z_ref, sbw_ref, ssdw_ref, wout_ref, h_ref, w_ref, t_hbm,
             y_ref, dh2_ref, dh2b_ref, loss_ref, dw_ref, t_ref, t_sem):
        i = pl.program_id(0)

        @pl.when(i == 0)
        def _():
            loss_ref[...] = jnp.zeros_like(loss_ref)
            dw_ref[...] = jnp.zeros_like(dw_ref)
            t_ref[:BLK, :] = jnp.zeros((BLK, D_MODEL), F32)
            first = pltpu.make_async_copy(t_hbm.at[pl.ds(0, tm - BLK)], t_ref.at[pl.ds(BLK, tm - BLK)], t_sem)
            first.start()
            first.wait()

        @pl.when(i > 0)
        def _():
            rest = pltpu.make_async_copy(t_hbm.at[pl.ds(pl.multiple_of(i * tm - BLK, 8), tm)], t_ref, t_sem)
            rest.start()
            rest.wait()

        y_ref[:, :SSD_W] = _gated_norm(osb_ref[...], g_ref[...], sbw_ref[...]).astype(BF16)
        y_ref[:, SSD_W:] = _gated_norm(ossd_ref[...], z_ref[...], ssdw_ref[...]).astype(BF16)
        h2 = h_ref[...] + _dot(y_ref[...], wout_ref[...], NN)
        r = lax.rsqrt(jnp.mean(h2 * h2, axis=-1, keepdims=True) + EPS)
        nrm = h2 * r
        w = w_ref[...]
        live = i * tm + _iota((tm, D_MODEL), 0) >= BLK
        err = jnp.where(live, nrm * w - t_ref[...], 0.0)
        dout = err * (1.0 / D_MODEL)
        loss_ref[...] += (0.5 / D_MODEL) * _fold_lanes(jnp.sum(err * err, axis=0, keepdims=True))
        dw_ref[...] += jnp.sum(dout * nrm, axis=0, keepdims=True)
        wd = dout * w
        dh2 = r * (wd - nrm * jnp.mean(wd * nrm, axis=-1, keepdims=True))
        dh2_ref[...] = dh2
        dh2b_ref[...] = dh2.astype(BF16)

    rows = lambda width, col=0: pl.BlockSpec((tm, width), lambda i: (i, col))
    return pl.pallas_call(
        body, name="out_head",
        out_shape=(jax.ShapeDtypeStruct((LP, 2 * SSD_W), BF16), jax.ShapeDtypeStruct((LP, D_MODEL), F32),
                   jax.ShapeDtypeStruct((LP, D_MODEL), BF16), jax.ShapeDtypeStruct((1, BLK), F32),
                   jax.ShapeDtypeStruct((1, D_MODEL), F32)),
        grid=(LP // tm,),
        in_specs=[rows(1024), rows(1024, COL_G // 1024), rows(1024), rows(1024, COL_Z // 1024),
                  _const_spec((1, 1024)), _const_spec((1, 1024)),
                  pl.BlockSpec((2 * SSD_W, D_MODEL), lambda i: (0, 0), pipeline_mode=pl.Buffered(1)),
                  rows(D_MODEL), _const_spec((1, D_MODEL)), _ANY],
        out_specs=(rows(2 * SSD_W), rows(D_MODEL), rows(D_MODEL), _const_spec((1, BLK)), _const_spec((1, D_MODEL))),
        scratch_shapes=[pltpu.VMEM((tm, D_MODEL), F32), pltpu.SemaphoreType.DMA],
        compiler_params=_params(("arbitrary",)),
    )(o_sb, proj, o_ssd, proj, sb_w, ssd_w, w_out_full, h_pad, fnw, target)


def _fold_lanes(row):
    out = row[:, :BLK]
    for j in range(1, row.shape[1] // BLK):
        out = out + row[:, j * BLK:(j + 1) * BLK]
    return out


def _gated_norm_bwd(dy, o, g, w):
    s = _sigmoid(g)
    sg = g * s
    a = o * sg
    r = lax.rsqrt(jnp.mean(a * a, axis=-1, keepdims=True) + EPS)
    nrm = a * r
    dw = jnp.sum(dy * nrm, axis=0, keepdims=True)
    wd = dy * w
    da = r * (wd - nrm * jnp.mean(wd * nrm, axis=-1, keepdims=True))
    return da * sg, da * o * (s * (1.0 + g * (1.0 - s))), dw


def _ycat_bwd(dh2_b, w_out_full, o_sb, proj, o_ssd, sb_w, ssd_w, g_w_out):
    tm = LP // 8
    n_slots = g_w_out.shape[0]

    def body(a_ref, w_ref, osb_ref, g_ref, ossd_ref, z_ref, sbw_ref, ssdw_ref, src_ref,
             dosb_ref, dg_ref, dossd_ref, dz_ref, dsbw_ref, dssdw_ref, sib_ref, send_sems, recv_sems):
        i = pl.program_id(0)

        @pl.when(i == 0)
        def _():
            for cp in _sibling_swap_copies(src_ref, sib_ref, send_sems, recv_sems, n_slots):
                cp.start()
            dsbw_ref[...] = jnp.zeros_like(dsbw_ref)
            dssdw_ref[...] = jnp.zeros_like(dssdw_ref)

        dy = _dot(a_ref[...], w_ref[...], NT)
        do, dg, dw = _gated_norm_bwd(dy[:, :SSD_W], osb_ref[...], g_ref[...], sbw_ref[...])
        dosb_ref[...] = do
        dg_ref[...] = dg.astype(BF16)
        dsbw_ref[...] += dw
        do, dg, dw = _gated_norm_bwd(dy[:, SSD_W:], ossd_ref[...], z_ref[...], ssdw_ref[...])
        dossd_ref[...] = do
        dz_ref[...] = dg.astype(BF16)
        dssdw_ref[...] += dw

        @pl.when(i == LP // tm - 1)
        def _():
            for cp in _sibling_swap_copies(src_ref, sib_ref, send_sems, recv_sems, n_slots):
                cp.wait()

    act = jax.ShapeDtypeStruct((LP, 1024), F32)
    gate = jax.ShapeDtypeStruct((LP, 1024), BF16)
    vec = jax.ShapeDtypeStruct((1, 1024), F32)
    rows = lambda col=0: pl.BlockSpec((tm, 1024), lambda i: (i, col))
    return pl.pallas_call(
        body, name="ycat_bwd",
        out_shape=(act, gate, act, gate, vec, vec,
                   jax.ShapeDtypeStruct((n_slots,) + g_w_out.shape[2:], g_w_out.dtype)),
        grid=(LP // tm,),
        in_specs=[rows(), _const_spec((2 * SSD_W, D_MODEL)), rows(), rows(COL_G // 1024), rows(),
                  rows(COL_Z // 1024), _const_spec((1, 1024)), _const_spec((1, 1024)), _ANY],
        out_specs=(rows(), rows(), rows(), rows(), _const_spec((1, 1024)), _const_spec((1, 1024)), _ANY),
        scratch_shapes=[pltpu.SemaphoreType.DMA((n_slots,)), pltpu.SemaphoreType.DMA((n_slots,))],
        compiler_params=_params(("arbitrary",)),
    )(dh2_b, w_out_full, o_sb, proj, o_ssd, proj, sb_w, ssd_w, g_w_out)


_DPROJ_PIECES = (("dq", COL_Q, 1024), ("dk", COL_K, 1024), ("dv", COL_V, 1024), ("dg", COL_G, 1024),
                 ("dz", COL_Z, 1024), ("dxbc", COL_XBC, XBC_W), ("ddt", D_MAIN, BLK))
W_IN_PAD = D_MAIN + 512
WIN_STRIDE = 13 * BLK
WIN_WIDTH = 14 * BLK


def _d_w_in(u_t, pieces):
    tn = 512
    nj = W_IN_PAD // tn
    half = D_MODEL // 2
    main = _DPROJ_PIECES[:-1]

    def body(*refs):
        u_ref, piece_refs = refs[0], refs[1:1 + len(main)]
        ddt_ref, o_ref, sib_ref, stage_ref, send_sems, recv_sems = refs[1 + len(main):]
        j = pl.program_id(1)
        x, y, c = _mesh_position()
        a = u_ref[...]

        def copy(blk):
            return pltpu.make_async_remote_copy(
                src_ref=stage_ref.at[blk], dst_ref=sib_ref.at[:, pl.ds(pl.multiple_of(blk * tn, tn), tn)],
                send_sem=send_sems.at[blk], recv_sem=recv_sems.at[blk],
                device_id=(x, y, 1 - c), device_id_type=_MESH)

        def emit(res):
            top, bottom = res[:half], res[half:]
            o_ref[...] = jnp.where(c == 0, top, bottom)
            stage_ref[j] = jnp.where(c == 0, bottom, top)
            copy(j).start()

        for (_, col, width), ref in zip(main, piece_refs):
            @pl.when(jnp.logical_and(j >= col // tn, j < (col + width) // tn))
            def _():
                emit(_dot(a, ref[...], NN).astype(BF16))

        @pl.when(j == nj - 1)
        def _():
            tail = _dot(a, ddt_ref[...].astype(BF16), NN).astype(BF16)
            emit(jnp.concatenate([tail, jnp.zeros((D_MODEL, tn - BLK), BF16)], axis=1))
            for blk in range(nj):
                copy(blk).wait()

    def piece_spec(col, width):
        return pl.BlockSpec((LP, tn), lambda i, j: (0, jnp.clip(j - col // tn, 0, width // tn - 1)))

    shape = jax.ShapeDtypeStruct((half, W_IN_PAD), BF16)
    return pl.pallas_call(
        body, name="d_w_in", out_shape=(shape, shape), grid=(1, nj),
        in_specs=[pl.BlockSpec((D_MODEL, LP), lambda i, j: (0, 0), pipeline_mode=pl.Buffered(1))]
        + [piece_spec(col, width) for _, col, width in main]
        + [pl.BlockSpec((LP, BLK), lambda i, j: (0, 0))],
        out_specs=(pl.BlockSpec((half, tn), lambda i, j: (0, j)), _ANY),
        scratch_shapes=[pltpu.VMEM((nj, half, tn), BF16), pltpu.SemaphoreType.DMA((nj,)),
                        pltpu.SemaphoreType.DMA((nj,))],
        compiler_params=_params(("arbitrary", "arbitrary"), BIG_VMEM_LIMIT),
    )(u_t, *pieces)


def _add_halves(name, mine, sib, rows):
    r_all, cols = mine.shape

    def body(a_ref, b_ref, o_ref):
        o_ref[...] = (a_ref[...].astype(F32) + b_ref[...].astype(F32)).astype(BF16)

    spec = pl.BlockSpec((rows, cols), lambda i: (i, 0))
    return pl.pallas_call(
        body, name=name, out_shape=jax.ShapeDtypeStruct((r_all, cols), BF16), grid=(r_all // rows,),
        in_specs=[spec, spec], out_specs=spec, compiler_params=_params(("parallel",)),
    )(mine, sib)


def _d_u_prenorm_bwd(pieces, w_main, wdt, h_pad, norm_w, dh2, chip_sum):
    tm, tk = LP // 4, 512
    nk = D_MAIN // tk
    main = _DPROJ_PIECES[:-1]
    window = (WIN_STRIDE, WIN_WIDTH)

    def body(*refs):
        piece_refs = refs[:len(main)]
        (b_ref, ddt_ref, wdt_ref, h_ref, w_ref, dh2_ref, src_ref, dh_ref, dw_ref, out_ref,
         acc_ref, send_sems, recv_sems, loc_sem) = refs[len(main):]
        i, k = pl.program_id(0), pl.program_id(1)

        @pl.when(jnp.logical_and(i == 0, k == 0))
        def _():
            for cp in _chip_exchange_copies(src_ref, out_ref, send_sems, recv_sems, loc_sem, window=window):
                cp.start()
            dw_ref[...] = jnp.zeros_like(dw_ref)

        @pl.when(k == 0)
        def _():
            acc_ref[...] = jnp.zeros_like(acc_ref)

        for (_, col, width), ref in zip(main, piece_refs):
            for lo in range(0, width, tk):
                @pl.when(k == (col + lo) // tk)
                def _():
                    acc_ref[...] += _dot(ref[:, lo:lo + tk], b_ref[...], NT)

        @pl.when(k == nk - 1)
        def _():
            dut = acc_ref[...] + _dot(ddt_ref[:, :N_HEADS].astype(BF16), wdt_ref[...], NT)
            xv = h_ref[...]
            r = lax.rsqrt(jnp.mean(xv * xv, axis=-1, keepdims=True) + EPS)
            nrm = xv * r
            dw_ref[...] += jnp.sum(dut * nrm, axis=0, keepdims=True)
            wd = dut * w_ref[...]
            dh_ref[...] = dh2_ref[...] + r * (wd - nrm * jnp.mean(wd * nrm, axis=-1, keepdims=True))

        @pl.when(jnp.logical_and(i == LP // tm - 1, k == nk - 1))
        def _():
            for cp in _chip_exchange_copies(src_ref, out_ref, send_sems, recv_sems, loc_sem, window=window):
                cp.wait()

    rows = lambda width: pl.BlockSpec((tm, width), lambda i, k: (i, 0))
    const = lambda shape: pl.BlockSpec(shape, lambda i, k: (0, 0))
    return pl.pallas_call(
        body, name="d_u_prenorm_bwd",
        out_shape=(jax.ShapeDtypeStruct((LP, D_MODEL), F32), jax.ShapeDtypeStruct((1, D_MODEL), F32),
                   jax.ShapeDtypeStruct((4, chip_sum.shape[0], WIN_WIDTH), chip_sum.dtype)),
        grid=(LP // tm, nk),
        in_specs=[rows(width) for _, _, width in main]
        + [pl.BlockSpec((D_MODEL, tk), lambda i, k: (0, k)), rows(BLK), const((D_MODEL, N_HEADS)), rows(D_MODEL),
           const((1, D_MODEL)), rows(D_MODEL), _ANY],
        out_specs=(rows(D_MODEL), const((1, D_MODEL)), _ANY),
        scratch_shapes=[pltpu.VMEM((tm, D_MODEL), F32)] + _CHIP_EXCHANGE_SEMS,
        compiler_params=_params(("arbitrary", "arbitrary")),
    )(*pieces[:-1], w_main, pieces[-1], wdt, h_pad, norm_w, dh2, chip_sum)


def _suffix_sum(vals, tri):
    return _dot_x_exact(vals, tri, NN, n=2)


def _sb_tile(z):
    t = jnp.exp(-jnp.abs(z))
    inv = 1.0 / (1.0 + t)
    sp = jnp.maximum(z, 0.0) + jnp.log(1.0 + t)
    sig = jnp.where(z >= 0, inv, t * inv)
    return sig, -sp, z - sp


def _sweep(first, step, init, run_slots):
    def alive_of(state):
        top = state[run_slots[0]]
        for s in run_slots[1:]:
            top = jnp.maximum(top, state[s])
        return (jnp.max(top) > SB_DEAD).astype(jnp.int32)

    def cond(carry):
        return jnp.logical_and(carry[0] >= 0, carry[1] > 0)

    def body(carry):
        state = step(carry[0], tuple(carry[2:]))
        return (carry[0] - 1, alive_of(state)) + tuple(state)

    return lax.while_loop(cond, body, (first, alive_of(init)) + tuple(init))[2:]


def _head_masks(x, lane):
    head0 = lane < HEAD_DIM
    return [jnp.where(head0, x, 0.0).astype(BF16), jnp.where(head0, 0.0, x).astype(BF16)]


PAIRS_PER_STEP = 4
STEP_W = PAIRS_PER_STEP * BLK
STEP_HEADS = 2 * PAIRS_PER_STEP


def _pair_lanes(h):
    lo = (h // 2) * BLK
    return slice(lo, lo + BLK)


def _stage_kv(qi, k_ref, v_ref, kb_ref, vb_ref):
    @pl.when(qi == 0)
    def _():
        kb_ref[:BLK, :] = jnp.zeros((BLK, STEP_W), BF16)
        vb_ref[:BLK, :] = jnp.zeros((BLK, STEP_W), BF16)
        kb_ref[BLK:, :] = k_ref[...].astype(BF16)
        vb_ref[BLK:, :] = v_ref[...].astype(BF16)


def _window_masks(qi):
    col2 = (qi - 1) * BLK + _iota((BLK, 2 * BLK), 1)
    row2 = qi * BLK + _iota((BLK, 2 * BLK), 0)
    valid2 = jnp.logical_and(col2 < row2, col2 >= PAD)
    row1 = qi * BLK + _iota((BLK, BLK), 0)
    lane = _iota((BLK, BLK), 1)

    def valid1(kblk):
        col = kblk * BLK + lane
        return jnp.logical_and(col < row1, col >= PAD)

    return valid2, valid1


def _strict_upper(n):
    return (_iota((n, n), 0) > _iota((n, n), 1)).astype(BF16)


def _sb_attention_fwd(proj, w_out_shard):
    heads = range(STEP_HEADS)
    n_groups = N_HEADS // STEP_HEADS

    def body(q_ref, k_ref, v_ref, src_ref, o_ref, ox_ref, out_ref, kb_ref, vb_ref, send_sems, recv_sems, loc_sem):
        grp, qi = pl.program_id(0), pl.program_id(1)
        _stage_kv(qi, k_ref, v_ref, kb_ref, vb_ref)

        @pl.when(jnp.logical_and(grp == 0, qi == 0))
        def _():
            for cp in _all_gather_copies(src_ref, out_ref, send_sems, recv_sems, loc_sem):
                cp.start()

        lane = _iota((BLK, BLK), 1)
        qh = []
        for p in range(PAIRS_PER_STEP):
            qh += _head_masks(q_ref[:, p * BLK:(p + 1) * BLK] * SB_SCALE, lane)
        valid2, valid1 = _window_masks(qi)

        def tiles(rows, valid, tri, runs):
            ks = [kb_ref[rows, _pair_lanes(h)] for h in heads]
            vs = [vb_ref[rows, _pair_lanes(h)] for h in heads]
            parts = [_sb_tile(jnp.where(valid, _dot(qh[h], ks[h], NT), SB_MASKED)) for h in heads]
            afters = [_suffix_sum(parts[h][1], tri) for h in heads]
            if runs is not None:
                afters = [afters[h] + runs[h] for h in heads]
            avals = [jnp.exp(parts[h][2] + afters[h]) for h in heads]
            his = [avals[h].astype(BF16) for h in heads]
            accs = [_dot(his[h], vs[h], NN) for h in heads]
            rests = [_dot((avals[h] - his[h].astype(F32)).astype(BF16), vs[h], NN) for h in heads]
            return [(jnp.sum(parts[h][1], axis=1, keepdims=True), accs[h], rests[h]) for h in heads]

        win = pl.ds(pl.multiple_of(qi * BLK, BLK), 2 * BLK)
        init = [t for head in tiles(win, valid2, _strict_upper(2 * BLK), None) for t in head]
        tri1 = _strict_upper(BLK)

        def step(kblk, carry):
            rows = pl.ds(pl.multiple_of((kblk + 1) * BLK, BLK), BLK)
            runs = [carry[3 * h] for h in heads]
            new = []
            for h, (d_run, d_acc, d_rest) in enumerate(tiles(rows, valid1(kblk), tri1, runs)):
                new += [carry[3 * h] + d_run, carry[3 * h + 1] + d_acc, carry[3 * h + 2] + d_rest]
            return tuple(new)

        res = _sweep(qi - 2, step, init, tuple(3 * h for h in heads))
        for p in range(PAIRS_PER_STEP):
            o = jnp.where(lane < HEAD_DIM, res[6 * p + 1], res[6 * p + 4])
            o_ref[:, p * BLK:(p + 1) * BLK] = o
            ox_ref[:, p * BLK:(p + 1) * BLK] = o + jnp.where(lane < HEAD_DIM, res[6 * p + 2], res[6 * p + 5])

        @pl.when(jnp.logical_and(grp == n_groups - 1, qi == NBLK - 1))
        def _():
            for cp in _all_gather_copies(src_ref, out_ref, send_sems, recv_sems, loc_sem):
                cp.wait()

    act = jax.ShapeDtypeStruct((LP, 1024), F32)
    blk = lambda col: pl.BlockSpec((BLK, STEP_W), lambda g, qi: (qi, col + g))
    whole = lambda col: pl.BlockSpec((LP, STEP_W), lambda g, qi: (0, col + g), pipeline_mode=pl.Buffered(1))
    return pl.pallas_call(
        body, name="sb_attn_fwd",
        out_shape=(act, act, jax.ShapeDtypeStruct((N_DEV,) + w_out_shard.shape, w_out_shard.dtype)),
        grid=(n_groups, NBLK),
        in_specs=[blk(COL_Q // STEP_W), whole(COL_K // STEP_W), whole(COL_V // STEP_W), _ANY],
        out_specs=(blk(0), blk(0), _ANY),
        scratch_shapes=[pltpu.VMEM((LP + BLK, STEP_W), BF16), pltpu.VMEM((LP + BLK, STEP_W), BF16)] + _ALL_GATHER_SEMS,
        compiler_params=_params(("arbitrary", "arbitrary")),
    )(proj, proj, proj, w_out_shard)


def _sb_attention_bwd(proj, o_sb, do_sb, chip_sums):
    heads = range(STEP_HEADS)
    n_groups = N_HEADS // STEP_HEADS

    def body(q_ref, k_ref, v_ref, o_ref, do_ref, src_ref, dq_ref, dk_ref, dv_ref, out_ref,
             kb_ref, vb_ref, dka_ref, dva_ref, send_sems, recv_sems, loc_sem):
        grp, qi = pl.program_id(0), pl.program_id(1)
        _stage_kv(qi, k_ref, v_ref, kb_ref, vb_ref)

        @pl.when(jnp.logical_and(grp == 0, qi == 0))
        def _():
            for cp in _chip_exchange_copies(src_ref, out_ref, send_sems, recv_sems, loc_sem):
                cp.start()

        @pl.when(qi == 0)
        def _():
            dka_ref[...] = jnp.zeros_like(dka_ref)
            dva_ref[...] = jnp.zeros_like(dva_ref)

        lane = _iota((BLK, BLK), 1)
        head0 = lane < HEAD_DIM
        qh, doh, dsum = [], [], []
        for p in range(PAIRS_PER_STEP):
            lanes = slice(p * BLK, (p + 1) * BLK)
            qh += _head_masks(q_ref[:, lanes] * SB_SCALE, lane)
            do = do_ref[:, lanes]
            doh += _head_masks(do, lane)
            prod = do.astype(BF16).astype(F32) * o_ref[:, lanes]
            dsum += [jnp.sum(jnp.where(head0, prod, 0.0), axis=1, keepdims=True),
                     jnp.sum(jnp.where(head0, 0.0, prod), axis=1, keepdims=True)]
        valid2, valid1 = _window_masks(qi)

        def tiles(rows, valid, tri, runs, eruns):
            ks = [kb_ref[rows, _pair_lanes(h)] for h in heads]
            vs = [vb_ref[rows, _pair_lanes(h)] for h in heads]
            parts = [_sb_tile(jnp.where(valid, _dot(qh[h], ks[h], NT), SB_MASKED)) for h in heads]
            afters = [_suffix_sum(parts[h][1], tri) for h in heads]
            if runs is not None:
                afters = [afters[h] + runs[h] for h in heads]
            avals = [jnp.exp(parts[h][2] + afters[h]) for h in heads]
            es = [avals[h] * _dot(doh[h], vs[h], NT) for h in heads]
            esufs = [_suffix_sum(es[h], tri) for h in heads]
            if eruns is not None:
                esufs = [esufs[h] + eruns[h] for h in heads]
            dzs = [(es[h] - parts[h][0] * (dsum[h] - esufs[h])).astype(BF16) for h in heads]
            dqs = [_dot(dzs[h], ks[h], NN) for h in heads]
            dks = [_dot(dzs[h], qh[h], TN) for h in heads]
            dvs = [_dot(avals[h].astype(BF16), doh[h], TN) for h in heads]
            for p in range(PAIRS_PER_STEP):
                dka_ref[rows, p * BLK:(p + 1) * BLK] += dks[2 * p] + dks[2 * p + 1]
                dva_ref[rows, p * BLK:(p + 1) * BLK] += dvs[2 * p] + dvs[2 * p + 1]
            return [(jnp.sum(parts[h][1], axis=1, keepdims=True), jnp.sum(es[h], axis=1, keepdims=True), dqs[h])
                    for h in heads]

        win = pl.ds(pl.multiple_of(qi * BLK, BLK), 2 * BLK)
        init = [t for head in tiles(win, valid2, _strict_upper(2 * BLK), None, None) for t in head]
        tri1 = _strict_upper(BLK)

        def step(kblk, carry):
            rows = pl.ds(pl.multiple_of((kblk + 1) * BLK, BLK), BLK)
            runs = [carry[3 * h] for h in heads]
            eruns = [carry[3 * h + 1] for h in heads]
            new = []
            for h, (d_run, d_erun, d_q) in enumerate(tiles(rows, valid1(kblk), tri1, runs, eruns)):
                new += [carry[3 * h] + d_run, carry[3 * h + 1] + d_erun, carry[3 * h + 2] + d_q]
            return tuple(new)

        res = _sweep(qi - 2, step, init, tuple(3 * h for h in heads))
        for p in range(PAIRS_PER_STEP):
            dq = jnp.where(head0, res[6 * p + 2], res[6 * p + 5]) * SB_SCALE
            dq_ref[:, p * BLK:(p + 1) * BLK] = dq.astype(BF16)

        @pl.when(qi == NBLK - 1)
        def _():
            dk_ref[...] = dka_ref[BLK:, :].astype(BF16)
            dv_ref[...] = dva_ref[BLK:, :].astype(BF16)

        @pl.when(jnp.logical_and(grp == n_groups - 1, qi == NBLK - 1))
        def _():
            for cp in _chip_exchange_copies(src_ref, out_ref, send_sems, recv_sems, loc_sem):
                cp.wait()

    act = jax.ShapeDtypeStruct((LP, 1024), BF16)
    blk = lambda col: pl.BlockSpec((BLK, STEP_W), lambda g, qi: (qi, col + g))
    whole = lambda col: pl.BlockSpec((LP, STEP_W), lambda g, qi: (0, col + g))
    once = lambda col: pl.BlockSpec((LP, STEP_W), lambda g, qi: (0, col + g), pipeline_mode=pl.Buffered(1))
    return pl.pallas_call(
        body, name="sb_attn_bwd",
        out_shape=(act, act, act, jax.ShapeDtypeStruct(chip_sums.shape, chip_sums.dtype)),
        grid=(n_groups, NBLK),
        in_specs=[blk(COL_Q // STEP_W), once(COL_K // STEP_W), once(COL_V // STEP_W), blk(0), blk(0), _ANY],
        out_specs=(blk(0), whole(0), whole(0), _ANY),
        scratch_shapes=[pltpu.VMEM((LP + BLK, STEP_W), BF16), pltpu.VMEM((LP + BLK, STEP_W), BF16),
                        pltpu.VMEM((LP + BLK, STEP_W), F32), pltpu.VMEM((LP + BLK, STEP_W), F32)]
        + _CHIP_EXCHANGE_SEMS,
        compiler_params=_params(("arbitrary", "arbitrary"), BIG_VMEM_LIMIT),
    )(proj, proj, proj, o_sb, do_sb, chip_sums)


def _conv_pre(x, w_ref, b_ref):
    acc = b_ref[...] + w_ref[3:4, :] * x
    for k in range(3):
        acc = acc + w_ref[k:k + 1, :] * pltpu.roll(x, 3 - k, 0)
    return acc


CONV_BLK = 256


def _conv_fwd(proj, conv_w, conv_b):
    def body(x_ref, w_ref, b_ref, o_ref):
        xc = _conv_pre(x_ref[...], w_ref, b_ref)
        o_ref[...] = xc * _sigmoid(xc)

    nb = XBC_W // CONV_BLK
    return pl.pallas_call(
        body, name="conv_fwd", out_shape=jax.ShapeDtypeStruct((LP, XBC_W), F32), grid=(nb,),
        in_specs=[pl.BlockSpec((LP, CONV_BLK), lambda j: (0, COL_XBC // CONV_BLK + j)),
                  pl.BlockSpec((4, CONV_BLK), lambda j: (0, j)), pl.BlockSpec((1, CONV_BLK), lambda j: (0, j))],
        out_specs=pl.BlockSpec((LP, CONV_BLK), lambda j: (0, j)),
        compiler_params=_params(("parallel",)),
    )(proj, conv_w, conv_b)


def _conv_bwd(dxa, proj, conv_w, conv_b):
    def body(d_ref, x_ref, w_ref, b_ref, dx_ref, dw_ref, db_ref):
        x = x_ref[...]
        xc = _conv_pre(x, w_ref, b_ref)
        s = _sigmoid(xc)
        live = _iota((LP, CONV_BLK), 0) >= PAD
        dxc = jnp.where(live, d_ref[...] * (s * (1.0 + xc * (1.0 - s))), 0.0)
        db_ref[...] = jnp.sum(dxc, axis=0, keepdims=True)
        dx = w_ref[3:4, :] * dxc
        dw_ref[3:4, :] = jnp.sum(dxc * x, axis=0, keepdims=True)
        for k in range(3):
            dw_ref[k:k + 1, :] = jnp.sum(dxc * pltpu.roll(x, 3 - k, 0), axis=0, keepdims=True)
            dx = dx + w_ref[k:k + 1, :] * pltpu.roll(dxc, LP - (3 - k), 0)
        dx_ref[...] = dx.astype(BF16)

    nb = XBC_W // CONV_BLK
    col = lambda j: (0, j)
    wide = pl.BlockSpec((LP, CONV_BLK), col)
    return pl.pallas_call(
        body, name="conv_bwd",
        out_shape=(jax.ShapeDtypeStruct((LP, XBC_W), BF16), jax.ShapeDtypeStruct((4, XBC_W), F32),
                   jax.ShapeDtypeStruct((1, XBC_W), F32)),
        grid=(nb,),
        in_specs=[wide, pl.BlockSpec((LP, CONV_BLK), lambda j: (0, COL_XBC // CONV_BLK + j)),
                  pl.BlockSpec((4, CONV_BLK), col), pl.BlockSpec((1, CONV_BLK), col)],
        out_specs=(wide, pl.BlockSpec((4, CONV_BLK), col), pl.BlockSpec((1, CONV_BLK), col)),
        compiler_params=_params(("parallel",)),
    )(dxa, proj, conv_w, conv_b)


def _ssd_prelude(c, dt_ref, dtt_ref, dtb_ref, dtbt_ref, alog_ref, alogt_ref):
    live = jnp.logical_or(c > 0, _iota((BLK, N_HEADS), 0) >= PAD)
    live_t = jnp.logical_or(c > 0, _iota((N_HEADS, BLK), 1) >= PAD)
    pre = dt_ref[...] + dtb_ref[...]
    pre_t = dtt_ref[...] + dtbt_ref[...]
    dt = jnp.where(live, _softplus(pre), 0.0)
    dt_t = jnp.where(live_t, _softplus(pre_t), 0.0)
    a = -jnp.exp(alog_ref[...])
    a_t = -jnp.exp(alogt_ref[...])
    li = _iota((BLK, BLK), 0)
    si = _iota((BLK, BLK), 1)
    lower = (si <= li).astype(BF16)
    upper = (li <= si).astype(BF16)
    acum = _dot_exact_x(lower, dt * a, NN)
    acum_t = _dot_x_exact(dt_t * a_t, upper, NN)
    return live, pre, dt, a, a_t, acum, acum_t


def _head_expand():
    return (_iota((N_HEADS, SSD_W), 1) // HEAD_DIM == _iota((N_HEADS, SSD_W), 0)).astype(BF16)


def _head_reduce_mat():
    return (_iota((SSD_W, N_HEADS), 0) // HEAD_DIM == _iota((SSD_W, N_HEADS), 1)).astype(BF16)


def _decay_mat(acum, acum_t, h, causal):
    seg = jnp.minimum(acum[:, h:h + 1] - acum_t[h:h + 1, :], 0.0)
    return jnp.where(causal, jnp.exp(seg), 0.0)


def _ssd_fwd(xa, dt_raw, dt_raw_t, dt_bias, dt_bias_t, a_log, a_log_t, d_exp):
    def body(x_ref, b_ref, c_ref, dt_ref, dtt_ref, dtb_ref, dtbt_ref, alog_ref, alogt_ref, dexp_ref,
             y_ref, hs_ref, state_ref):
        c = pl.program_id(0)

        @pl.when(c == 0)
        def _():
            state_ref[...] = jnp.zeros_like(state_ref)

        _, _, dt, _, _, acum, acum_t = _ssd_prelude(c, dt_ref, dtt_ref, dtb_ref, dtbt_ref, alog_ref, alogt_ref)
        expand = _head_expand()
        x = x_ref[...]
        xdt = x * _dot_x_exact(dt, expand, n=2)
        exp_a = _dot_x_exact(jnp.exp(acum), expand, n=2)
        to_end = _dot_x_exact(jnp.exp(acum[BLK - 1:BLK, :] - acum), expand, n=2)
        xdt_b = xdt.astype(BF16)
        xd_b = (xdt * to_end).astype(BF16)
        chunk_decay = jnp.exp(acum_t[:, BLK - 1:BLK])
        hs_ref[0] = state_ref[...]
        lane = _iota((BLK, BLK), 1)
        causal = _iota((BLK, BLK), 0) >= lane
        gw = HEADS_PER_GROUP * HEAD_DIM
        for g in range(N_GROUPS):
            bg = b_ref[:, g * N_STATE:(g + 1) * N_STATE].astype(BF16)
            cg = c_ref[:, g * N_STATE:(g + 1) * N_STATE].astype(BF16)
            cb = _dot(cg, bg, NT)
            hg = state_ref[g * gw:(g + 1) * gw, :]
            ch = _dot(cg, hg.astype(BF16), NT)
            st = _dot(xd_b[:, g * gw:(g + 1) * gw], bg, TN)
            for p in range(HEADS_PER_GROUP // 2):
                h0 = g * HEADS_PER_GROUP + 2 * p
                lo = h0 * HEAD_DIM
                xp = xdt_b[:, lo:lo + BLK]
                w0 = (cb * _decay_mat(acum, acum_t, h0, causal)).astype(BF16)
                w1 = (cb * _decay_mat(acum, acum_t, h0 + 1, causal)).astype(BF16)
                yd = jnp.where(lane < HEAD_DIM, _dot(w0, xp), _dot(w1, xp))
                y_ref[:, lo:lo + BLK] = (yd + ch[:, lo - g * gw:lo - g * gw + BLK] * exp_a[:, lo:lo + BLK]
                                         + x[:, lo:lo + BLK] * dexp_ref[:, lo:lo + BLK])
            for r in range(HEADS_PER_GROUP):
                h = g * HEADS_PER_GROUP + r
                state_ref[h * HEAD_DIM:(h + 1) * HEAD_DIM, :] = (
                    hg[r * HEAD_DIM:(r + 1) * HEAD_DIM, :] * chunk_decay[h:h + 1, :]
                    + st[r * HEAD_DIM:(r + 1) * HEAD_DIM, :])

    chunk = lambda width, col=0: pl.BlockSpec((BLK, width), lambda c: (c, col))
    return pl.pallas_call(
        body, name="ssd_fwd",
        out_shape=(jax.ShapeDtypeStruct((LP, SSD_W), F32), jax.ShapeDtypeStruct((NBLK, SSD_W, N_STATE), F32)),
        grid=(NBLK,),
        in_specs=[chunk(SSD_W), chunk(256, 4), chunk(256, 5), chunk(N_HEADS),
                  pl.BlockSpec((N_HEADS, BLK), lambda c: (0, c)), _const_spec((1, N_HEADS)),
                  _const_spec((N_HEADS, 1)), _const_spec((1, N_HEADS)), _const_spec((N_HEADS, 1)),
                  _const_spec((1, SSD_W))],
        out_specs=(chunk(SSD_W), pl.BlockSpec((1, SSD_W, N_STATE), lambda c: (c, 0, 0))),
        scratch_shapes=[pltpu.VMEM((SSD_W, N_STATE), F32)],
        compiler_params=_params(("arbitrary",)),
    )(xa, xa, xa, dt_raw, dt_raw_t, dt_bias, dt_bias_t, a_log, a_log_t, d_exp)


def _ssd_bwd(xa, dt_raw, dt_raw_t, dt_bias, dt_bias_t, a_log, a_log_t, d_exp, hstart, dy):
    def body(x_ref, b_ref, c_ref, dt_ref, dtt_ref, dtb_ref, dtbt_ref, alog_ref, alogt_ref, dexp_ref,
             hs_ref, dy_ref, dxa_ref, ddt_ref, dbias_ref, dalog_ref, dd_ref, dstate_ref):
        step = pl.program_id(0)
        c = NBLK - 1 - step

        @pl.when(step == 0)
        def _():
            dstate_ref[...] = jnp.zeros_like(dstate_ref)
            dbias_ref[...] = jnp.zeros_like(dbias_ref)
            dalog_ref[...] = jnp.zeros_like(dalog_ref)
            dd_ref[...] = jnp.zeros_like(dd_ref)

        live, pre, dt, a, a_t, acum, acum_t = _ssd_prelude(c, dt_ref, dtt_ref, dtb_ref, dtbt_ref,
                                                           alog_ref, alogt_ref)
        expand = _head_expand()
        reduce_m = _head_reduce_mat()
        x = x_ref[...]
        dyv = dy_ref[...]
        dt_e = _dot_x_exact(dt, expand, n=2)
        xdt = x * dt_e
        exp_acum = jnp.exp(acum)
        exp_a = _dot_x_exact(exp_acum, expand, n=2)
        dte = jnp.exp(acum[BLK - 1:BLK, :] - acum)
        to_end = _dot_x_exact(dte, expand, n=2)
        xdt_b = xdt.astype(BF16)
        xd_b = (xdt * to_end).astype(BF16)
        chunk_decay = jnp.exp(acum_t[:, BLK - 1:BLK])
        lane = _iota((BLK, BLK), 1)
        head0 = lane < HEAD_DIM
        causal = _iota((BLK, BLK), 0) >= lane
        gw = HEADS_PER_GROUP * HEAD_DIM
        dm = dyv * exp_a
        dm_b = dm.astype(BF16)
        onehot = lambda h: (_iota((1, N_HEADS), 1) == h).astype(F32)
        onehot_t = lambda h: (_iota((N_HEADS, 1), 0) == h).astype(F32)
        dacum = jnp.zeros((BLK, N_HEADS), F32)
        dacum_t = jnp.zeros((N_HEADS, BLK), F32)
        head_sums = lambda prod, rows: _dot_x_exact(prod, reduce_m[rows, :], n=2)
        dskip_acc = head_sums(dyv * x, slice(0, SSD_W))
        ddt_acc = jnp.zeros((BLK, N_HEADS), F32)
        ddte_acc = jnp.zeros((BLK, N_HEADS), F32)
        dexpa_acc = jnp.zeros((BLK, N_HEADS), F32)
        head_sum = expand
        for g in range(N_GROUPS):
            bg = b_ref[:, g * N_STATE:(g + 1) * N_STATE].astype(BF16)
            cg = c_ref[:, g * N_STATE:(g + 1) * N_STATE].astype(BF16)
            cb = _dot(cg, bg, NT)
            hg = hs_ref[0, g * gw:(g + 1) * gw, :]
            hg_b = hg.astype(BF16)
            dhe = dstate_ref[g * gw:(g + 1) * gw, :]
            dhe_b = dhe.astype(BF16)
            ch = _dot(cg, hg_b, NT)
            dcg = _dot(dm_b[:, g * gw:(g + 1) * gw], hg_b, NN)
            dhs = _dot(dm_b[:, g * gw:(g + 1) * gw], cg, TN)
            dxd = _dot(bg, dhe_b, NT)
            dbg = _dot(xd_b[:, g * gw:(g + 1) * gw], dhe_b, NN)
            dcb = jnp.zeros((BLK, BLK), F32)
            dxdt_g = []
            for p in range(HEADS_PER_GROUP // 2):
                h0 = g * HEADS_PER_GROUP + 2 * p
                lo = h0 * HEAD_DIM
                xp = xdt_b[:, lo:lo + BLK]
                dyp = dyv[:, lo:lo + BLK]
                dyh = (jnp.where(head0, dyp, 0.0).astype(BF16), jnp.where(head0, 0.0, dyp).astype(BF16))
                dxdt_p = jnp.zeros((BLK, BLK), F32)
                for q in range(2):
                    h = h0 + q
                    dec = _decay_mat(acum, acum_t, h, causal)
                    w = cb * dec
                    dw = _dot(dyh[q], xp, NT)
                    t = dw * w
                    dacum = dacum + jnp.sum(t, axis=1, keepdims=True) * onehot(h)
                    dacum_t = dacum_t - jnp.sum(t, axis=0, keepdims=True) * onehot_t(h)
                    dcb = dcb + dw * dec
                    dxdt_p = dxdt_p + _dot(w.astype(BF16), dyh[q], TN)
                sl = slice(lo, lo + BLK)
                gl = slice(lo - g * gw, lo - g * gw + BLK)
                dxdt_p = dxdt_p + dxd[:, gl] * to_end[:, sl]
                dxa_ref[:, sl] = dyp * dexp_ref[:, sl] + dxdt_p * dt_e[:, sl]
                dxdt_g.append(dxdt_p)
            cols = slice(g * gw, (g + 1) * gw)
            ddt_acc = ddt_acc + head_sums(jnp.concatenate(dxdt_g, axis=1) * x[:, cols], cols)
            ddte_acc = ddte_acc + head_sums(dxd * xdt[:, cols], cols)
            dexpa_acc = dexpa_acc + head_sums(dyv[:, cols] * ch, cols)
            dcb_b = dcb.astype(BF16)
            b_col = SSD_W + g * N_STATE
            c_col = SSD_W + (N_GROUPS + g) * N_STATE
            dxa_ref[:, c_col:c_col + N_STATE] = dcg + _dot(dcb_b, bg, NN)
            dxa_ref[:, b_col:b_col + N_STATE] = dbg + _dot(dcb_b, cg, TN)
            prod = dhe * hg
            per_head = jnp.sum(_dot_exact_x(head_sum[:, g * gw:(g + 1) * gw], prod, NN), axis=1, keepdims=True)
            dacum_t = dacum_t + (per_head * chunk_decay) * (_iota((1, BLK), 1) == BLK - 1).astype(F32)
            for r in range(HEADS_PER_GROUP):
                h = g * HEADS_PER_GROUP + r
                rows = slice(h * HEAD_DIM, (h + 1) * HEAD_DIM)
                dstate_ref[rows, :] = (dhs[r * HEAD_DIM:(r + 1) * HEAD_DIM, :]
                                       + dhe[r * HEAD_DIM:(r + 1) * HEAD_DIM, :] * chunk_decay[h:h + 1, :])
        dacum = dacum + dexpa_acc * exp_acum - ddte_acc * dte
        last_row = (_iota((BLK, 1), 0) == BLK - 1).astype(F32)
        dacum = dacum + last_row * jnp.sum(ddte_acc * dte, axis=0, keepdims=True)
        li = _iota((BLK, BLK), 0)
        si = _iota((BLK, BLK), 1)
        upper = (li <= si).astype(BF16)
        lower = (si <= li).astype(BF16)
        dda = _dot_exact_x(upper, dacum, NN)
        dda_t = _dot_x_exact(dacum_t, lower, NN)
        eye = (_iota((N_HEADS, N_HEADS), 0) == _iota((N_HEADS, N_HEADS), 1)).astype(BF16)
        dda = dda + _dot_x_exact_tn(dda_t, eye)
        ddt = ddt_acc + dda * a
        dalog_ref[...] += jnp.sum(dda * dt, axis=0, keepdims=True) * a
        dd_ref[...] += jnp.sum(dskip_acc, axis=0, keepdims=True)
        ddt_raw = jnp.where(live, ddt * _sigmoid(pre), 0.0)
        ddt_ref[...] = jnp.zeros_like(ddt_ref)
        ddt_ref[:, :N_HEADS] = ddt_raw
        dbias_ref[...] += jnp.sum(ddt_raw, axis=0, keepdims=True)

    rev = lambda width, col=0: pl.BlockSpec((BLK, width), lambda s: (NBLK - 1 - s, col))
    vec = jax.ShapeDtypeStruct((1, N_HEADS), F32)
    return pl.pallas_call(
        body, name="ssd_bwd",
        out_shape=(jax.ShapeDtypeStruct((LP, XBC_W), F32), jax.ShapeDtypeStruct((LP, BLK), F32), vec, vec, vec),
        grid=(NBLK,),
        in_specs=[rev(SSD_W), rev(256, 4), rev(256, 5), rev(N_HEADS),
                  pl.BlockSpec((N_HEADS, BLK), lambda s: (0, NBLK - 1 - s)), _const_spec((1, N_HEADS)),
                  _const_spec((N_HEADS, 1)), _const_spec((1, N_HEADS)), _const_spec((N_HEADS, 1)),
                  _const_spec((1, SSD_W)),
                  pl.BlockSpec((1, SSD_W, N_STATE), lambda s: (NBLK - 1 - s, 0, 0)), rev(SSD_W)],
        out_specs=(rev(XBC_W), rev(BLK), _const_spec((1, N_HEADS)),
                   _const_spec((1, N_HEADS)), _const_spec((1, N_HEADS))),
        scratch_shapes=[pltpu.VMEM((SSD_W, N_STATE), F32)],
        compiler_params=_params(("arbitrary",)),
    )(xa, xa, xa, dt_raw, dt_raw_t, dt_bias, dt_bias_t, a_log, a_log_t, d_exp, hstart, dy)


def _dot_x_exact_tn(x_t, eye):
    out = None
    for p in _split(x_t, 3):
        t = _dot(p, eye, TN)
        out = t if out is None else out + t
    return out


def _adamw(name, parts, w, m, v, rows):
    r_all, cols = w.shape
    assert r_all % rows == 0
    c1 = 1.0 / (1.0 - ADAM_B1 ** ADAM_STEP)
    c2 = 1.0 / (1.0 - ADAM_B2 ** ADAM_STEP)

    def body(p_ref, w_ref, m_ref, v_ref, g_ref, d_ref, mo_ref, vo_ref):
        g = p_ref[0].astype(F32)
        for j in range(1, parts.shape[0]):
            g = g + p_ref[j].astype(F32)
        mn = ADAM_B1 * m_ref[...] + (1.0 - ADAM_B1) * g
        vn = ADAM_B2 * v_ref[...] + (1.0 - ADAM_B2) * (g * g)
        g_ref[...] = g
        mo_ref[...] = mn
        vo_ref[...] = vn
        d_ref[...] = -ADAM_LR * ((mn * c1) / (jnp.sqrt(vn * c2) + ADAM_EPS) + ADAM_WD * w_ref[...])

    spec = pl.BlockSpec((rows, cols), lambda i: (i, 0))
    shp = jax.ShapeDtypeStruct((r_all, cols), F32)
    return pl.pallas_call(
        body, name=name, out_shape=(shp, shp, shp, shp), grid=(r_all // rows,),
        in_specs=[pl.BlockSpec((parts.shape[0], rows, cols), lambda i: (0, i, 0)), spec, spec, spec],
        out_specs=(spec, spec, spec, spec),
        compiler_params=_params(("parallel",)),
    )(parts, w, m, v)


_VECTORS = (("norm_w", 1024, 0), ("conv_b", 1536, 8), ("dt_bias", 16, 20), ("a_log", 16, 21), ("d_skip", 16, 22),
            ("sb_norm_w", 1024, 24), ("ssd_norm_w", 1024, 32), ("final_norm_w", 1024, 40))
_LOSS_ROW = 23
_CONVW_ROW = 48
_META_ROW = 96
_PACK_ROWS = 224
_SMALL_ORDER = tuple(name for name, _, _ in _VECTORS) + ("conv_w", "meta_tokens")


def _pack_small_grads(vectors, loss_row, d_convw, dh):
    def body(*refs):
        vec_refs, (loss_ref, cw_ref, dh_ref, out_ref) = refs[:len(_VECTORS)], refs[len(_VECTORS):]
        out_ref[...] = jnp.zeros_like(out_ref)
        for (_, width, row), ref in zip(_VECTORS, vec_refs):
            if width < BLK:
                out_ref[row:row + 1, :width] = ref[...]
            else:
                for t in range(width // BLK):
                    out_ref[row + t:row + t + 1, :] = ref[:, t * BLK:(t + 1) * BLK]
        out_ref[_LOSS_ROW:_LOSS_ROW + 1, :] = loss_ref[...]
        for k in range(4):
            for t in range(XBC_W // BLK):
                r = _CONVW_ROW + k * (XBC_W // BLK) + t
                out_ref[r:r + 1, :] = cw_ref[k:k + 1, t * BLK:(t + 1) * BLK]
        for i in range(N_META):
            for t in range(D_MODEL // BLK):
                r = _META_ROW + i * (D_MODEL // BLK) + t
                out_ref[r:r + 1, :] = dh_ref[i:i + 1, t * BLK:(t + 1) * BLK]

    full = lambda a: pl.BlockSpec(a.shape, lambda i: tuple(0 for _ in a.shape))
    return pl.pallas_call(
        body, name="pack_small_grads", out_shape=jax.ShapeDtypeStruct((_PACK_ROWS, BLK), F32), grid=(1,),
        in_specs=[full(v) for v in vectors] + [full(loss_row), full(d_convw),
                                               pl.BlockSpec((N_META, D_MODEL), lambda i: (PAD // N_META, 0))],
        out_specs=pl.BlockSpec((_PACK_ROWS, BLK), lambda i: (0, 0)),
        compiler_params=_params(("arbitrary",)),
    )(*vectors, loss_row, d_convw, dh)


def _sum_slots_and_swap(name, parts, rows):
    n, r_all, cols = parts.shape
    steps = r_all // rows

    def body(p_ref, o_ref, sib_ref, stage_ref, send_sems, recv_sems):
        i = pl.program_id(0)
        x, y, c = _mesh_position()

        def copy(blk):
            return pltpu.make_async_remote_copy(
                src_ref=stage_ref.at[blk], dst_ref=sib_ref.at[pl.ds(pl.multiple_of(blk * rows, rows), rows)],
                send_sem=send_sems.at[blk], recv_sem=recv_sems.at[blk],
                device_id=(x, y, 1 - c), device_id_type=_MESH)

        acc = p_ref[0].astype(F32)
        for j in range(1, n):
            acc = acc + p_ref[j].astype(F32)
        o_ref[...] = acc
        stage_ref[i] = acc
        copy(i).start()

        @pl.when(i == steps - 1)
        def _():
            for blk in range(steps):
                copy(blk).wait()

    shape = jax.ShapeDtypeStruct((r_all, cols), F32)
    return pl.pallas_call(
        body, name=name, out_shape=(shape, shape), grid=(steps,),
        in_specs=[pl.BlockSpec((n, rows, cols), lambda i: (0, i, 0))],
        out_specs=(pl.BlockSpec((rows, cols), lambda i: (i, 0)), _ANY),
        scratch_shapes=[pltpu.VMEM((steps, rows, cols), F32), pltpu.SemaphoreType.DMA((steps,)),
                        pltpu.SemaphoreType.DMA((steps,))],
        compiler_params=_params(("arbitrary",)),
    )(parts)


def _adamw_small(pack, weights, moms, vels):
    c1 = 1.0 / (1.0 - ADAM_B1 ** ADAM_STEP)
    c2 = 1.0 / (1.0 - ADAM_B2 ** ADAM_STEP)
    n = len(_SMALL_ORDER)

    def body(*refs):
        p_ref = refs[0]
        w_refs, m_refs, v_refs = refs[1:1 + n], refs[1 + n:1 + 2 * n], refs[1 + 2 * n:1 + 3 * n]
        outs = refs[1 + 3 * n:1 + 7 * n]
        loss_ref, g_ref, cw_ref, cws_ref, mt_ref, all_ref, send_sems, recv_sems, loc_sem = refs[1 + 7 * n:]
        x, y, c = _mesh_position()
        me = 4 * x + 2 * y + c
        copies = _all_gather_copies(p_ref, all_ref, send_sems, recv_sems, loc_sem)
        for cp in copies:
            cp.start()
        for cp in copies:
            cp.wait()
        g = all_ref[0]
        for j in range(1, N_DEV):
            g = g + all_ref[j]
        g_ref[...] = g
        loss_ref[...] = g_ref[_LOSS_ROW:_LOSS_ROW + 1, :]

        def update(idx, grad):
            go_ref, d_ref, mo_ref, vo_ref = outs[4 * idx:4 * idx + 4]
            mn = ADAM_B1 * m_refs[idx][...] + (1.0 - ADAM_B1) * grad
            vn = ADAM_B2 * v_refs[idx][...] + (1.0 - ADAM_B2) * (grad * grad)
            go_ref[...] = grad
            mo_ref[...] = mn
            vo_ref[...] = vn
            d_ref[...] = -ADAM_LR * ((mn * c1) / (jnp.sqrt(vn * c2) + ADAM_EPS) + ADAM_WD * w_refs[idx][...])

        for idx, (_, width, row) in enumerate(_VECTORS):
            go_ref = outs[4 * idx]
            if width < BLK:
                grad = g_ref[row:row + 1, :width]
            else:
                for t in range(width // BLK):
                    go_ref[:, t * BLK:(t + 1) * BLK] = g_ref[row + t:row + t + 1, :]
                grad = go_ref[...]
            update(idx, grad)
        cw_ref[...] = jnp.zeros_like(cw_ref)
        for k in range(4):
            for t in range(XBC_W // BLK):
                r = _CONVW_ROW + k * (XBC_W // BLK) + t
                cw_ref[k:k + 1, t * BLK:(t + 1) * BLK] = g_ref[r:r + 1, :]
        for i in range(N_META):
            for t in range(D_MODEL // BLK):
                r = _META_ROW + i * (D_MODEL // BLK) + t
                mt_ref[i:i + 1, t * BLK:(t + 1) * BLK] = g_ref[r:r + 1, :]
        width_cw = XBC_W // N_DEV
        pick_cw = (_iota((XBC_W, width_cw), 0) == me * width_cw + _iota((XBC_W, width_cw), 1)).astype(BF16)
        cws_ref[...] = _dot_x_exact(cw_ref[...], pick_cw)
        update(n - 2, cws_ref[0:4, :])
        pick_mt = (_iota((D_MODEL, BLK), 0) == me * BLK + _iota((D_MODEL, BLK), 1)).astype(BF16)
        update(n - 1, _dot_x_exact(mt_ref[...], pick_mt))

    full = lambda a: pl.BlockSpec(a.shape, lambda i: tuple(0 for _ in a.shape))
    params = list(weights) + list(moms) + list(vels)
    out_shape, out_specs = [], []
    for w in weights:
        for _ in range(4):
            out_shape.append(jax.ShapeDtypeStruct(w.shape, F32))
            out_specs.append(full(w))
    out_shape.append(jax.ShapeDtypeStruct((1, BLK), F32))
    out_specs.append(pl.BlockSpec((1, BLK), lambda i: (0, 0)))
    return pl.pallas_call(
        body, name="adamw_small", out_shape=tuple(out_shape), grid=(1,),
        in_specs=[full(pack)] + [full(a) for a in params], out_specs=tuple(out_specs),
        scratch_shapes=[pltpu.VMEM((_PACK_ROWS, BLK), F32), pltpu.VMEM((8, XBC_W), F32),
                        pltpu.VMEM((8, XBC_W // N_DEV), F32), pltpu.VMEM((N_META, D_MODEL), F32),
                        pltpu.VMEM((N_DEV, _PACK_ROWS, BLK), F32)] + _ALL_GATHER_SEMS,
        compiler_params=_params(("arbitrary",)),
    )(pack, *params)


def kernel(x, meta_tokens, norm_w, w_in, conv_w, conv_b, dt_bias, a_log, d_skip, sb_norm_w, ssd_norm_w, w_out, final_norm_w, loss_target, m_meta_tokens, m_norm_w, m_w_in, m_conv_w, m_conv_b, m_dt_bias, m_a_log, m_d_skip, m_sb_norm_w, m_ssd_norm_w, m_w_out, m_final_norm_w, v_meta_tokens, v_norm_w, v_w_in, v_conv_w, v_conv_b, v_dt_bias, v_a_log, v_d_skip, v_sb_norm_w, v_ssd_norm_w, v_w_out, v_final_norm_w):
    small_src = jnp.concatenate([conv_w[0].reshape(6, BLK), meta_tokens, jnp.zeros((2, BLK), F32)], axis=0)
    small_g, w_in_g = _gather_weights([small_src, w_in[0].astype(BF16)])
    w_in_full = w_in_g.transpose(1, 0, 2).reshape(D_MODEL, D_IN)
    w_dt = w_in_full[:, D_MAIN:]
    conv_w_full = small_g[:, :6].reshape(N_DEV, 4, 192).transpose(1, 0, 2).reshape(4, XBC_W)
    meta_full = small_g[:, 6:6 + N_META].transpose(1, 0, 2).reshape(N_META, D_MODEL)
    h_pad = jnp.concatenate([jnp.zeros((PAD, D_MODEL), F32), meta_full, x[0]], axis=0)
    dt_bias_t = dt_bias.reshape(N_HEADS, 1)
    a_log_t = a_log.reshape(N_HEADS, 1)
    d_exp = jnp.repeat(d_skip, HEAD_DIM, axis=1)
    fnw = final_norm_w.reshape(1, D_MODEL)

    u, u_t, dt_raw, dt_raw_t = _prenorm(h_pad, norm_w, w_dt, w_dt.T)
    proj = _matmul("in_proj", u, w_in_full, "nn", LP, 512, D_MODEL, n_cols=D_MAIN)
    o_sb, o_sb_exact, w_out_g = _sb_attention_fwd(proj, w_out[0].astype(BF16))
    w_out_full = w_out_g.reshape(2 * SSD_W, D_MODEL)
    xa = _conv_fwd(proj, conv_w_full, conv_b)
    o_ssd, hstart = _ssd_fwd(xa, dt_raw, dt_raw_t, dt_bias, dt_bias_t, a_log, a_log_t, d_exp)
    ycat, dh2, dh2_b, loss_row, d_fnw = _out_head(
        o_sb, proj, o_ssd, sb_norm_w, ssd_norm_w, w_out_full, h_pad, fnw, loss_target[0])

    g_w_out = _matmul("d_w_out", ycat, dh2_b, "tn", 512, 512, LP, BF16).reshape(4, 2, 256, D_MODEL)
    do_sb, dg, do_ssd, dz, d_sbw, d_ssdw, sib_w_out = _ycat_bwd(
        dh2_b, w_out_full, o_sb, proj, o_ssd, sb_norm_w, ssd_norm_w, g_w_out)
    chip_w_out = _pair_sum("pair_sum_w_out", g_w_out, sib_w_out, 256)
    dq, dk, dv, p_w_out = _sb_attention_bwd(proj, o_sb_exact, do_sb, chip_w_out)
    dxa, ddt_raw, d_dtb, d_alog, d_dskip = _ssd_bwd(
        xa, dt_raw, dt_raw_t, dt_bias, dt_bias_t, a_log, a_log_t, d_exp, hstart, do_ssd)
    dxbc, d_convw, d_convb = _conv_bwd(dxa, proj, conv_w_full, conv_b)
    pieces = [dq, dk, dv, dg, dz, dxbc, ddt_raw]
    g_w_in_mine, g_w_in_sib = _d_w_in(u_t, pieces)
    chip_w_in = _add_halves("pair_sum_w_in", g_w_in_mine, g_w_in_sib, 128)
    dh, d_nw, win_parts = _d_u_prenorm_bwd(pieces, w_in_full, w_dt, h_pad, norm_w, dh2, chip_w_in)
    win_mine, win_other = _sum_slots_and_swap("sum_w_in_windows", win_parts, 128)
    core = lax.axis_index("c")
    chip = 2 * lax.axis_index("x") + lax.axis_index("y")
    first_col = (D_IN // N_DEV) * (2 * chip + core) - WIN_STRIDE * chip
    cut = lambda w: lax.dynamic_slice(w, (0, first_col), (D_MODEL // 2, D_IN // N_DEV))
    half_mine, half_other = cut(win_mine), cut(win_other)
    p_w_in = jnp.concatenate([jnp.where(core == 0, half_mine, half_other),
                              jnp.where(core == 0, half_other, half_mine)], axis=0)[None]

    pack = _pack_small_grads([d_nw, d_convb, d_dtb, d_alog, d_dskip, d_sbw, d_ssdw, d_fnw], loss_row, d_convw, dh)

    res_in = _adamw("adamw_w_in", p_w_in, w_in[0], m_w_in[0], v_w_in[0], 128)
    res_out = _adamw("adamw_w_out", p_w_out, w_out[0], m_w_out[0], v_w_out[0], 128)
    res_small = _adamw_small(
        pack,
        [norm_w, conv_b, dt_bias, a_log, d_skip, sb_norm_w, ssd_norm_w, fnw, conv_w[0], meta_tokens],
        [m_norm_w, m_conv_b, m_dt_bias, m_a_log, m_d_skip, m_sb_norm_w, m_ssd_norm_w,
         m_final_norm_w.reshape(1, D_MODEL), m_conv_w[0], m_meta_tokens],
        [v_norm_w, v_conv_b, v_dt_bias, v_a_log, v_d_skip, v_sb_norm_w, v_ssd_norm_w,
         v_final_norm_w.reshape(1, D_MODEL), v_conv_w[0], v_meta_tokens])

    loss = jnp.sum(res_small[-1])
    order = ["meta_tokens", "norm_w", "w_in", "conv_w", "conv_b", "dt_bias", "a_log", "d_skip",
             "sb_norm_w", "ssd_norm_w", "w_out", "final_norm_w"]
    outs = [loss, dh[BLK:].reshape(1, SEQ, D_MODEL)]
    for kind in range(4):
        small = {name: res_small[4 * idx + kind] for idx, name in enumerate(_SMALL_ORDER)}
        small["final_norm_w"] = small["final_norm_w"].reshape(D_MODEL)
        small["conv_w"] = small["conv_w"].reshape(1, 4, XBC_W // N_DEV)
        small["w_in"] = res_in[kind].reshape(1, D_MODEL, D_IN // N_DEV)
        small["w_out"] = res_out[kind].reshape(1, 256, D_MODEL)
        outs += [small[name] for name in order]
    return tuple(outs)
```

```python
import jax
import jax.numpy as jnp
from jax import lax
from jax.experimental import pallas as pl
from jax.experimental.pallas import tpu as pltpu

F32 = jnp.float32
BF16 = jnp.bfloat16

D_MODEL = 1024
SEQ = 2048
N_META = 16
BLK = 128
PAD = BLK - N_META
LP = PAD + N_META + SEQ
NBLK = LP // BLK
N_HEADS = 16
HEAD_DIM = 64
N_GROUPS = 2
HEADS_PER_GROUP = 8
N_STATE = 128
SSD_W = 1024
XBC_W = 1536
D_MAIN = 6656
D_IN = 6672
COL_Q, COL_K, COL_V, COL_G, COL_Z, COL_XBC = 0, 1024, 2048, 3072, 4096, 5120
N_DEV = 8
EPS = 1e-5
SB_SCALE = 0.125
SB_DEAD = -87.4
SB_MASKED = -1e30

ADAM_LR = 0.001
ADAM_B1 = 0.9
ADAM_B2 = 0.999
ADAM_EPS = 1e-08
ADAM_WD = 0.01
ADAM_STEP = 10

VMEM_LIMIT = 48 * 1024 * 1024
BIG_VMEM_LIMIT = 56 * 1024 * 1024

NN = (((1,), (0,)), ((), ()))
NT = (((1,), (1,)), ((), ()))
TN = (((0,), (0,)), ((), ()))


def _dot(a, b, dims=NN):
    return lax.dot_general(a, b, dims, preferred_element_type=F32)


def _split(x, n):
    parts = []
    r = x
    for i in range(n):
        p = r.astype(BF16)
        parts.append(p)
        if i + 1 < n:
            r = r - p.astype(F32)
    return parts


def _dot_x_exact(x, m, dims=NN, n=3):
    out = None
    for p in _split(x, n):
        t = _dot(p, m, dims)
        out = t if out is None else out + t
    return out


def _dot_exact_x(m, x, dims=NN, n=3):
    out = None
    for p in _split(x, n):
        t = _dot(m, p, dims)
        out = t if out is None else out + t
    return out


def _iota(shape, dim):
    return lax.broadcasted_iota(jnp.int32, shape, dim)


def _softplus(x):
    return jnp.maximum(x, 0.0) + jnp.log(1.0 + jnp.exp(-jnp.abs(x)))


def _sigmoid(x):
    return 1.0 / (1.0 + jnp.exp(-x))


def _params(sem=None, vmem=None):
    return pltpu.CompilerParams(dimension_semantics=sem, vmem_limit_bytes=vmem or VMEM_LIMIT)


_ANY = pl.BlockSpec(memory_space=pl.ANY)
_MESH = pl.DeviceIdType.MESH


def _mesh_position():
    return lax.axis_index("x"), lax.axis_index("y"), lax.axis_index("c")


def _other_chips(x, y):
    return [(1 - x, y), (x, 1 - y), (1 - x, 1 - y)]


def _gather_weights(srcs):
    n = len(srcs)

    def body(*refs):
        src, out = refs[:n], refs[n:2 * n]
        send_sems, recv_sems, loc_sems = refs[2 * n:]
        x, y, c = _mesh_position()
        sibling = (x, y, 1 - c)
        chips = _other_chips(x, y)
        relay_from = (jnp.where(c == 0, 1 - x, x), jnp.where(c == 0, y, 1 - y))
        relay_to = (jnp.where(c == 0, x, 1 - x), jnp.where(c == 0, 1 - y, y))

        def copy(a, k, block, to, from_src=False):
            slot = out[a].at[4 * block[0] + 2 * block[1] + block[2]]
            return pltpu.make_async_remote_copy(
                src_ref=src[a] if from_src else slot, dst_ref=slot,
                send_sem=send_sems.at[7 * a + k], recv_sem=recv_sems.at[7 * a + k],
                device_id=to, device_id_type=_MESH)

        local, sends = [], []
        for a in range(n):
            mine = pltpu.make_async_copy(src[a], out[a].at[4 * x + 2 * y + c], loc_sems.at[a])
            mine.start()
            local.append(mine)
            first = [copy(a, 0, (x, y, c), sibling, True)]
            first += [copy(a, 1 + j, (x, y, c), (*chip, c), True) for j, chip in enumerate(chips[:2])]
            for cp in first:
                cp.start()
            sends += first
        for a in range(n):
            for j, chip in enumerate(chips[:2]):
                copy(a, 1 + j, (*chip, c), (x, y, c)).wait_recv()
            later = [copy(a, 3, (*relay_from, c), (*relay_to, c))]
            later += [copy(a, 4 + j, (*chip, c), sibling) for j, chip in enumerate(chips[:2])]
            for cp in later:
                cp.start()
            sends += later
        for a in range(n):
            copy(a, 3, (*chips[2], c), (x, y, c)).wait_recv()
            passed = copy(a, 6, (*chips[2], c), sibling)
            passed.start()
            sends.append(passed)
        for a in range(n):
            copy(a, 0, (x, y, 1 - c), (x, y, c)).wait_recv()
            for j, chip in enumerate(chips):
                copy(a, 4 + j, (*chip, 1 - c), (x, y, c)).wait_recv()
        for cp in sends:
            cp.wait_send()
        for cp in local:
            cp.wait()

    return pl.pallas_call(
        body, name="gather_weights",
        out_shape=tuple(jax.ShapeDtypeStruct((N_DEV,) + s.shape, s.dtype) for s in srcs),
        in_specs=[_ANY] * n, out_specs=tuple([_ANY] * n),
        scratch_shapes=[pltpu.SemaphoreType.DMA((7 * n,)), pltpu.SemaphoreType.DMA((7 * n,)),
                        pltpu.SemaphoreType.DMA((n,))],
    )(*srcs)


_ALL_GATHER_SEMS = [pltpu.SemaphoreType.DMA((7,)), pltpu.SemaphoreType.DMA((7,)), pltpu.SemaphoreType.DMA]


def _all_gather_copies(src, out, send_sems, recv_sems, loc_sem):
    x, y, c = _mesh_position()
    me = 4 * x + 2 * y + c
    copies = [pltpu.make_async_copy(src, out.at[me], loc_sem)]
    for k in range(1, N_DEV):
        peer = (1 - x if k & 4 else x, 1 - y if k & 2 else y, 1 - c if k & 1 else c)
        copies.append(pltpu.make_async_remote_copy(
            src_ref=src, dst_ref=out.at[me], send_sem=send_sems.at[k - 1], recv_sem=recv_sems.at[k - 1],
            device_id=peer, device_id_type=_MESH))
    return copies


def _sibling_swap_copies(src, out, send_sems, recv_sems, n):
    x, y, c = _mesh_position()
    return [pltpu.make_async_remote_copy(
        src_ref=src.at[k, 1 - c], dst_ref=out.at[k],
        send_sem=send_sems.at[k], recv_sem=recv_sems.at[k], device_id=(x, y, 1 - c), device_id_type=_MESH)
        for k in range(n)]


def _pair_sum(name, g, sib, rows):
    n, _, r_all, cols = g.shape
    assert r_all % rows == 0

    def body(g0_ref, g1_ref, s_ref, o_ref):
        c = lax.axis_index("c")
        mine = jnp.where(c == 0, g0_ref[0, 0].astype(F32), g1_ref[0, 0].astype(F32))
        o_ref[0] = (mine + s_ref[0].astype(F32)).astype(o_ref.dtype)

    return pl.pallas_call(
        body, name=name, out_shape=jax.ShapeDtypeStruct((n, r_all, cols), BF16), grid=(n, r_all // rows),
        in_specs=[pl.BlockSpec((1, 1, rows, cols), lambda k, i: (k, 0, i, 0)),
                  pl.BlockSpec((1, 1, rows, cols), lambda k, i: (k, 1, i, 0)),
                  pl.BlockSpec((1, rows, cols), lambda k, i: (k, i, 0))],
        out_specs=pl.BlockSpec((1, rows, cols), lambda k, i: (k, i, 0)),
        compiler_params=_params(("parallel", "parallel")),
    )(g, g, sib)


_CHIP_EXCHANGE_SEMS = [pltpu.SemaphoreType.DMA((3,)), pltpu.SemaphoreType.DMA((3,)), pltpu.SemaphoreType.DMA]


def _chip_exchange_copies(src, out, send_sems, recv_sems, loc_sem, window=None):
    x, y, c = _mesh_position()
    here = 2 * x + y

    def slot(k):
        if window is not None:
            return src.at[:, pl.ds(pl.multiple_of(k * window[0], BLK), window[1])]
        return src.at[k]

    copies = [pltpu.make_async_copy(slot(here), out.at[here], loc_sem)]
    for j, chip in enumerate(_other_chips(x, y)):
        copies.append(pltpu.make_async_remote_copy(
            src_ref=slot(2 * chip[0] + chip[1]), dst_ref=out.at[here],
            send_sem=send_sems.at[j], recv_sem=recv_sems.at[j], device_id=(*chip, c), device_id_type=_MESH))
    return copies


def _matmul(name, a, b, kind, tm, tn, tk, out_dtype=F32, n_cols=None):
    if kind == "nn":
        (m, kk), nn_ = a.shape, (n_cols or b.shape[1])
        a_spec = pl.BlockSpec((tm, tk), lambda i, j, k: (i, k))
        b_spec = pl.BlockSpec((tk, tn), lambda i, j, k: (k, j))
        dims = NN
    else:
        (kk, m), (_, nn_) = a.shape, b.shape
        a_spec = pl.BlockSpec((tk, tm), lambda i, j, k: (k, i))
        b_spec = pl.BlockSpec((tk, tn), lambda i, j, k: (k, j))
        dims = TN
    assert m % tm == 0 and nn_ % tn == 0 and kk % tk == 0
    nk = kk // tk

    def body(a_ref, b_ref, o_ref, acc_ref):
        k = pl.program_id(2)
        part = _dot(a_ref[...], b_ref[...], dims)
        if nk == 1:
            o_ref[...] = part.astype(o_ref.dtype)
        else:
            @pl.when(k == 0)
            def _():
                acc_ref[...] = part

            @pl.when(k > 0)
            def _():
                acc_ref[...] += part

            @pl.when(k == nk - 1)
            def _():
                o_ref[...] = acc_ref[...].astype(o_ref.dtype)

    return pl.pallas_call(
        body, name=name, out_shape=jax.ShapeDtypeStruct((m, nn_), out_dtype),
        grid=(m // tm, nn_ // tn, nk),
        in_specs=[a_spec, b_spec], out_specs=pl.BlockSpec((tm, tn), lambda i, j, k: (i, j)),
        scratch_shapes=[pltpu.VMEM((tm, tn) if nk > 1 else (8, 128), F32)],
        compiler_params=_params(("parallel", "parallel", "arbitrary")),
    )(a, b)


def _row_spec(width, col=0):
    return pl.BlockSpec((BLK, width), lambda i: (i, col))


def _const_spec(shape):
    return pl.BlockSpec(shape, lambda i: tuple(0 for _ in shape))


def _prenorm(h_pad, norm_w, wdt, wdt_t):
    def body(h_ref, w_ref, wdt_ref, wdtt_ref, u_ref, ut_ref, dt_ref, dtt_ref):
        xv = h_ref[...]
        r = lax.rsqrt(jnp.mean(xv * xv, axis=-1, keepdims=True) + EPS)
        uf = xv * r * w_ref[...]
        u = uf.astype(BF16)
        u_ref[...] = u
        ut_ref[...] = uf.T.astype(BF16)
        dt_ref[...] = _dot(u, wdt_ref[...], NN)
        dtt_ref[...] = _dot(wdtt_ref[...], u, NT)

    return pl.pallas_call(
        body, name="prenorm",
        out_shape=(jax.ShapeDtypeStruct((LP, D_MODEL), BF16), jax.ShapeDtypeStruct((D_MODEL, LP), BF16),
                   jax.ShapeDtypeStruct((LP, N_HEADS), F32), jax.ShapeDtypeStruct((N_HEADS, LP), F32)),
        grid=(NBLK,),
        in_specs=[_row_spec(D_MODEL), _const_spec((1, D_MODEL)), _const_spec((D_MODEL, N_HEADS)),
                  _const_spec((N_HEADS, D_MODEL))],
        out_specs=(_row_spec(D_MODEL), pl.BlockSpec((D_MODEL, BLK), lambda i: (0, i)), _row_spec(N_HEADS),
                   pl.BlockSpec((N_HEADS, BLK), lambda i: (0, i))),
        compiler_params=_params(("parallel",)),
    )(h_pad, norm_w, wdt, wdt_t)


def _gated_norm(o, g, w):
    a = o * (g * _sigmoid(g))
    r = lax.rsqrt(jnp.mean(a * a, axis=-1, keepdims=True) + EPS)
    return a * r * w


def _out_head(o_sb, proj, o_ssd, sb_w, ssd_w, w_out_full, h_pad, fnw, target):
    tm = LP // 8

    def body(osb_ref, g_ref, ossd_ref, z_ref, sbw_ref, ssdw_ref, wout_ref, h_ref, w_ref, t_hbm,
             y_ref, dh2_ref, dh2b_ref, loss_ref, dw_ref, t_ref, t_sem):
        i = pl.program_id(0)

        @pl.when(i == 0)
        def _():
            loss_ref[...] = jnp.zeros_like(loss_ref)
            dw_ref[...] = jnp.zeros_like(dw_ref)
            t_ref[:BLK, :] = jnp.zeros((BLK, D_MODEL), F32)
            first = pltpu.make_async_copy(t_hbm.at[pl.ds(0, tm - BLK)], t_ref.at[pl.ds(BLK, tm - BLK)], t_sem)
            first.start()
            first.wait()

        @pl.when(i > 0)
        def _():
            rest = pltpu.make_async_copy(t_hbm.at[pl.ds(pl.multiple_of(i * tm - BLK, 8), tm)], t_ref, t_sem)
            rest.start()
            rest.wait()

        y_ref[:, :SSD_W] = _gated_norm(osb_ref[...], g_ref[...], sbw_ref[...]).astype(BF16)
        y_ref[:, SSD_W:] = _gated_norm(ossd_ref[...], z_ref[...], ssdw_ref[...]).astype(BF16)
        h2 = h_ref[...] + _dot(y_ref[...], wout_ref[...], NN)
        r = lax.rsqrt(jnp.mean(h2 * h2, axis=-1, keepdims=True) + EPS)
        nrm = h2 * r
        w = w_ref[...]
        live = i * tm + _iota((tm, D_MODEL), 0) >= BLK
        err = jnp.where(live, nrm * w - t_ref[...], 0.0)
        dout = err * (1.0 / D_MODEL)
        loss_ref[...] += (0.5 / D_MODEL) * _fold_lanes(jnp.sum(err * err, axis=0, keepdims=True))
        dw_ref[...] += jnp.sum(dout * nrm, axis=0, keepdims=True)
        wd = dout * w
        dh2 = r * (wd - nrm * jnp.mean(wd * nrm, axis=-1, keepdims=True))
        dh2_ref[...] = dh2
        dh2b_ref[...] = dh2.astype(BF16)

    rows = lambda width, col=0: pl.BlockSpec((tm, width), lambda i: (i, col))
    return pl.pallas_call(
        body, name="out_head",
        out_shape=(jax.ShapeDtypeStruct((LP, 2 * SSD_W), BF16), jax.ShapeDtypeStruct((LP, D_MODEL), F32),
                   jax.ShapeDtypeStruct((LP, D_MODEL), BF16), jax.ShapeDtypeStruct((1, BLK), F32),
                   jax.ShapeDtypeStruct((1, D_MODEL), F32)),
        grid=(LP // tm,),
        in_specs=[rows(1024), rows(1024, COL_G // 1024), rows(1024), rows(1024, COL_Z // 1024),
                  _const_spec((1, 1024)), _const_spec((1, 1024)),
                  pl.BlockSpec((2 * SSD_W, D_MODEL), lambda i: (0, 0), pipeline_mode=pl.Buffered(1)),
                  rows(D_MODEL), _const_spec((1, D_MODEL)), _ANY],
        out_specs=(rows(2 * SSD_W), rows(D_MODEL), rows(D_MODEL), _const_spec((1, BLK)), _const_spec((1, D_MODEL))),
        scratch_shapes=[pltpu.VMEM((tm, D_MODEL), F32), pltpu.SemaphoreType.DMA],
        compiler_params=_params(("arbitrary",)),
    )(o_sb, proj, o_ssd, proj, sb_w, ssd_w, w_out_full, h_pad, fnw, target)


def _fold_lanes(row):
    out = row[:, :BLK]
    for j in range(1, row.shape[1] // BLK):
        out = out + row[:, j * BLK:(j + 1) * BLK]
    return out


def _gated_norm_bwd(dy, o, g, w):
    s = _sigmoid(g)
    sg = g * s
    a = o * sg
    r = lax.rsqrt(jnp.mean(a * a, axis=-1, keepdims=True) + EPS)
    nrm = a * r
    dw = jnp.sum(dy * nrm, axis=0, keepdims=True)
    wd = dy * w
    da = r * (wd - nrm * jnp.mean(wd * nrm, axis=-1, keepdims=True))
    return da * sg, da * o * (s * (1.0 + g * (1.0 - s))), dw


def _ycat_bwd(dh2_b, w_out_full, o_sb, proj, o_ssd, sb_w, ssd_w, g_w_out):
    tm = LP // 8
    n_slots = g_w_out.shape[0]

    def body(a_ref, w_ref, osb_ref, g_ref, ossd_ref, z_ref, sbw_ref, ssdw_ref, src_ref,
             dosb_ref, dg_ref, dossd_ref, dz_ref, dsbw_ref, dssdw_ref, sib_ref, send_sems, recv_sems):
        i = pl.program_id(0)

        @pl.when(i == 0)
        def _():
            for cp in _sibling_swap_copies(src_ref, sib_ref, send_sems, recv_sems, n_slots):
                cp.start()
            dsbw_ref[...] = jnp.zeros_like(dsbw_ref)
            dssdw_ref[...] = jnp.zeros_like(dssdw_ref)

        dy = _dot(a_ref[...], w_ref[...], NT)
        do, dg, dw = _gated_norm_bwd(dy[:, :SSD_W], osb_ref[...], g_ref[...], sbw_ref[...])
        dosb_ref[...] = do
        dg_ref[...] = dg.astype(BF16)
        dsbw_ref[...] += dw
        do, dg, dw = _gated_norm_bwd(dy[:, SSD_W:], ossd_ref[...], z_ref[...], ssdw_ref[...])
        dossd_ref[...] = do
        dz_ref[...] = dg.astype(BF16)
        dssdw_ref[...] += dw

        @pl.when(i == LP // tm - 1)
        def _():
            for cp in _sibling_swap_copies(src_ref, sib_ref, send_sems, recv_sems, n_slots):
                cp.wait()

    act = jax.ShapeDtypeStruct((LP, 1024), F32)
    gate = jax.ShapeDtypeStruct((LP, 1024), BF16)
    vec = jax.ShapeDtypeStruct((1, 1024), F32)
    rows = lambda col=0: pl.BlockSpec((tm, 1024), lambda i: (i, col))
    return pl.pallas_call(
        body, name="ycat_bwd",
        out_shape=(act, gate, act, gate, vec, vec,
                   jax.ShapeDtypeStruct((n_slots,) + g_w_out.shape[2:], g_w_out.dtype)),
        grid=(LP // tm,),
        in_specs=[rows(), _const_spec((2 * SSD_W, D_MODEL)), rows(), rows(COL_G // 1024), rows(),
                  rows(COL_Z // 1024), _const_spec((1, 1024)), _const_spec((1, 1024)), _ANY],
        out_specs=(rows(), rows(), rows(), rows(), _const_spec((1, 1024)), _const_spec((1, 1024)), _ANY),
        scratch_shapes=[pltpu.SemaphoreType.DMA((n_slots,)), pltpu.SemaphoreType.DMA((n_slots,))],
        compiler_params=_params(("arbitrary",)),
    )(dh2_b, w_out_full, o_sb, proj, o_ssd, proj, sb_w, ssd_w, g_w_out)


_DPROJ_PIECES = (("dq", COL_Q, 1024), ("dk", COL_K, 1024), ("dv", COL_V, 1024), ("dg", COL_G, 1024),
                 ("dz", COL_Z, 1024), ("dxbc", COL_XBC, XBC_W), ("ddt", D_MAIN, BLK))
W_IN_PAD = D_MAIN + 512
WIN_STRIDE = 13 * BLK
WIN_WIDTH = 14 * BLK


def _d_w_in(u_t, pieces):
    tn = 512
    nj = W_IN_PAD // tn
    half = D_MODEL // 2
    main = _DPROJ_PIECES[:-1]

    def body(*refs):
        u_ref, piece_refs = refs[0], refs[1:1 + len(main)]
        ddt_ref, o_ref, sib_ref, stage_ref, send_sems, recv_sems = refs[1 + len(main):]
        j = pl.program_id(1)
        x, y, c = _mesh_position()
        a = u_ref[...]

        def copy(blk):
            return pltpu.make_async_remote_copy(
                src_ref=stage_ref.at[blk], dst_ref=sib_ref.at[:, pl.ds(pl.multiple_of(blk * tn, tn), tn)],
                send_sem=send_sems.at[blk], recv_sem=recv_sems.at[blk],
                device_id=(x, y, 1 - c), device_id_type=_MESH)

        def emit(res):
            top, bottom = res[:half], res[half:]
            o_ref[...] = jnp.where(c == 0, top, bottom)
            stage_ref[j] = jnp.where(c == 0, bottom, top)
            copy(j).start()

        for (_, col, width), ref in zip(main, piece_refs):
            @pl.when(jnp.logical_and(j >= col // tn, j < (col + width) // tn))
            def _():
                emit(_dot(a, ref[...], NN).astype(BF16))

        @pl.when(j == nj - 1)
        def _():
            tail = _dot(a, ddt_ref[...].astype(BF16), NN).astype(BF16)
            emit(jnp.concatenate([tail, jnp.zeros((D_MODEL, tn - BLK), BF16)], axis=1))
            for blk in range(nj):
                copy(blk).wait()

    def piece_spec(col, width):
        return pl.BlockSpec((LP, tn), lambda i, j: (0, jnp.clip(j - col // tn, 0, width // tn - 1)))

    shape = jax.ShapeDtypeStruct((half, W_IN_PAD), BF16)
    return pl.pallas_call(
        body, name="d_w_in", out_shape=(shape, shape), grid=(1, nj),
        in_specs=[pl.BlockSpec((D_MODEL, LP), lambda i, j: (0, 0), pipeline_mode=pl.Buffered(1))]
        + [piece_spec(col, width) for _, col, width in main]
        + [pl.BlockSpec((LP, BLK), lambda i, j: (0, 0))],
        out_specs=(pl.BlockSpec((half, tn), lambda i, j: (0, j)), _ANY),
        scratch_shapes=[pltpu.VMEM((nj, half, tn), BF16), pltpu.SemaphoreType.DMA((nj,)),
                        pltpu.SemaphoreType.DMA((nj,))],
        compiler_params=_params(("arbitrary", "arbitrary"), BIG_VMEM_LIMIT),
    )(u_t, *pieces)


def _add_halves(name, mine, sib, rows):
    r_all, cols = mine.shape

    def body(a_ref, b_ref, o_ref):
        o_ref[...] = (a_ref[...].astype(F32) + b_ref[...].astype(F32)).astype(BF16)

    spec = pl.BlockSpec((rows, cols), lambda i: (i, 0))
    return pl.pallas_call(
        body, name=name, out_shape=jax.ShapeDtypeStruct((r_all, cols), BF16), grid=(r_all // rows,),
        in_specs=[spec, spec], out_specs=spec, compiler_params=_params(("parallel",)),
    )(mine, sib)


def _d_u_prenorm_bwd(pieces, w_main, wdt, h_pad, norm_w, dh2, chip_sum):
    tm, tk = LP // 4, 512
    nk = D_MAIN // tk
    main = _DPROJ_PIECES[:-1]
    window = (WIN_STRIDE, WIN_WIDTH)

    def body(*refs):
        piece_refs = refs[:len(main)]
        (b_ref, ddt_ref, wdt_ref, h_ref, w_ref, dh2_ref, src_ref, dh_ref, dw_ref, out_ref,
         acc_ref, send_sems, recv_sems, loc_sem) = refs[len(main):]
        i, k = pl.program_id(0), pl.program_id(1)

        @pl.when(jnp.logical_and(i == 0, k == 0))
        def _():
            for cp in _chip_exchange_copies(src_ref, out_ref, send_sems, recv_sems, loc_sem, window=window):
                cp.start()
            dw_ref[...] = jnp.zeros_like(dw_ref)

        @pl.when(k == 0)
        def _():
            acc_ref[...] = jnp.zeros_like(acc_ref)

        for (_, col, width), ref in zip(main, piece_refs):
            for lo in range(0, width, tk):
                @pl.when(k == (col + lo) // tk)
                def _():
                    acc_ref[...] += _dot(ref[:, lo:lo + tk], b_ref[...], NT)

        @pl.when(k == nk - 1)
        def _():
            dut = acc_ref[...] + _dot(ddt_ref[:, :N_HEADS].astype(BF16), wdt_ref[...], NT)
            xv = h_ref[...]
            r = lax.rsqrt(jnp.mean(xv * xv, axis=-1, keepdims=True) + EPS)
            nrm = xv * r
            dw_ref[...] += jnp.sum(dut * nrm, axis=0, keepdims=True)
            wd = dut * w_ref[...]
            dh_ref[...] = dh2_ref[...] + r * (wd - nrm * jnp.mean(wd * nrm, axis=-1, keepdims=True))

        @pl.when(jnp.logical_and(i == LP // tm - 1, k == nk - 1))
        def _():
            for cp in _chip_exchange_copies(src_ref, out_ref, send_sems, recv_sems, loc_sem, window=window):
                cp.wait()

    rows = lambda width: pl.BlockSpec((tm, width), lambda i, k: (i, 0))
    const = lambda shape: pl.BlockSpec(shape, lambda i, k: (0, 0))
    return pl.pallas_call(
        body, name="d_u_prenorm_bwd",
        out_shape=(jax.ShapeDtypeStruct((LP, D_MODEL), F32), jax.ShapeDtypeStruct((1, D_MODEL), F32),
                   jax.ShapeDtypeStruct((4, chip_sum.shape[0], WIN_WIDTH), chip_sum.dtype)),
        grid=(LP // tm, nk),
        in_specs=[rows(width) for _, _, width in main]
        + [pl.BlockSpec((D_MODEL, tk), lambda i, k: (0, k)), rows(BLK), const((D_MODEL, N_HEADS)), rows(D_MODEL),
           const((1, D_MODEL)), rows(D_MODEL), _ANY],
        out_specs=(rows(D_MODEL), const((1, D_MODEL)), _ANY),
        scratch_shapes=[pltpu.VMEM((tm, D_MODEL), F32)] + _CHIP_EXCHANGE_SEMS,
        compiler_params=_params(("arbitrary", "arbitrary")),
    )(*pieces[:-1], w_main, pieces[-1], wdt, h_pad, norm_w, dh2, chip_sum)


def _suffix_sum(vals, tri):
    return _dot_x_exact(vals, tri, NN, n=2)


def _sb_tile(z):
    sp = jnp.maximum(z, 0.0) + jnp.log(1.0 + jnp.exp(-jnp.abs(z)))
    return 1.0 - jnp.exp(-sp), -sp, z - sp


def _sweep(first, step, init, run_slots):
    def alive_of(state):
        top = state[run_slots[0]]
        for s in run_slots[1:]:
            top = jnp.maximum(top, state[s])
        return (jnp.max(top) > SB_DEAD).astype(jnp.int32)

    def cond(carry):
        return jnp.logical_and(carry[0] >= 0, carry[1] > 0)

    def body(carry):
        state = step(carry[0], tuple(carry[2:]))
        return (carry[0] - 1, alive_of(state)) + tuple(state)

    return lax.while_loop(cond, body, (first, alive_of(init)) + tuple(init))[2:]


def _head_masks(x, lane):
    head0 = lane < HEAD_DIM
    return [jnp.where(head0, x, 0.0).astype(BF16), jnp.where(head0, 0.0, x).astype(BF16)]


PAIRS_PER_STEP = 4
STEP_W = PAIRS_PER_STEP * BLK
STEP_HEADS = 2 * PAIRS_PER_STEP


def _pair_lanes(h):
    lo = (h // 2) * BLK
    return slice(lo, lo + BLK)


def _stage_kv(qi, k_ref, v_ref, kb_ref, vb_ref):
    @pl.when(qi == 0)
    def _():
        kb_ref[:BLK, :] = jnp.zeros((BLK, STEP_W), BF16)
        vb_ref[:BLK, :] = jnp.zeros((BLK, STEP_W), BF16)
        kb_ref[BLK:, :] = k_ref[...].astype(BF16)
        vb_ref[BLK:, :] = v_ref[...].astype(BF16)


def _window_masks(qi):
    col2 = (qi - 1) * BLK + _iota((BLK, 2 * BLK), 1)
    row2 = qi * BLK + _iota((BLK, 2 * BLK), 0)
    valid2 = jnp.logical_and(col2 < row2, col2 >= PAD)
    row1 = qi * BLK + _iota((BLK, BLK), 0)
    lane = _iota((BLK, BLK), 1)

    def valid1(kblk):
        col = kblk * BLK + lane
        return jnp.logical_and(col < row1, col >= PAD)

    return valid2, valid1


def _strict_upper(n):
    return (_iota((n, n), 0) > _iota((n, n), 1)).astype(BF16)


def _sb_attention_fwd(proj, w_out_shard):
    heads = range(STEP_HEADS)
    n_groups = N_HEADS // STEP_HEADS

    def body(q_ref, k_ref, v_ref, src_ref, o_ref, ox_ref, out_ref, kb_ref, vb_ref, send_sems, recv_sems, loc_sem):
        grp, qi = pl.program_id(0), pl.program_id(1)
        _stage_kv(qi, k_ref, v_ref, kb_ref, vb_ref)

        @pl.when(jnp.logical_and(grp == 0, qi == 0))
        def _():
            for cp in _all_gather_copies(src_ref, out_ref, send_sems, recv_sems, loc_sem):
                cp.start()

        lane = _iota((BLK, BLK), 1)
        qh = []
        for p in range(PAIRS_PER_STEP):
            qh += _head_masks(q_ref[:, p * BLK:(p + 1) * BLK] * SB_SCALE, lane)
        valid2, valid1 = _window_masks(qi)

        def tiles(rows, valid, tri, runs):
            ks = [kb_ref[rows, _pair_lanes(h)] for h in heads]
            vs = [vb_ref[rows, _pair_lanes(h)] for h in heads]
            parts = [_sb_tile(jnp.where(valid, _dot(qh[h], ks[h], NT), SB_MASKED)) for h in heads]
            afters = [_suffix_sum(parts[h][1], tri) for h in heads]
            if runs is not None:
                afters = [afters[h] + runs[h] for h in heads]
            avals = [jnp.exp(parts[h][2] + afters[h]) for h in heads]
            his = [avals[h].astype(BF16) for h in heads]
            accs = [_dot(his[h], vs[h], NN) for h in heads]
            rests = [_dot((avals[h] - his[h].astype(F32)).astype(BF16), vs[h], NN) for h in heads]
            return [(jnp.sum(parts[h][1], axis=1, keepdims=True), accs[h], rests[h]) for h in heads]

        win = pl.ds(pl.multiple_of(qi * BLK, BLK), 2 * BLK)
        init = [t for head in tiles(win, valid2, _strict_upper(2 * BLK), None) for t in head]
        tri1 = _strict_upper(BLK)

        def step(kblk, carry):
            rows = pl.ds(pl.multiple_of((kblk + 1) * BLK, BLK), BLK)
            runs = [carry[3 * h] for h in heads]
            new = []
            for h, (d_run, d_acc, d_rest) in enumerate(tiles(rows, valid1(kblk), tri1, runs)):
                new += [carry[3 * h] + d_run, carry[3 * h + 1] + d_acc, carry[3 * h + 2] + d_rest]
            return tuple(new)

        res = _sweep(qi - 2, step, init, tuple(3 * h for h in heads))
        for p in range(PAIRS_PER_STEP):
            o = jnp.where(lane < HEAD_DIM, res[6 * p + 1], res[6 * p + 4])
            o_ref[:, p * BLK:(p + 1) * BLK] = o
            ox_ref[:, p * BLK:(p + 1) * BLK] = o + jnp.where(lane < HEAD_DIM, res[6 * p + 2], res[6 * p + 5])

        @pl.when(jnp.logical_and(grp == n_groups - 1, qi == NBLK - 1))
        def _():
            for cp in _all_gather_copies(src_ref, out_ref, send_sems, recv_sems, loc_sem):
                cp.wait()

    act = jax.ShapeDtypeStruct((LP, 1024), F32)
    blk = lambda col: pl.BlockSpec((BLK, STEP_W), lambda g, qi: (qi, col + g))
    whole = lambda col: pl.BlockSpec((LP, STEP_W), lambda g, qi: (0, col + g), pipeline_mode=pl.Buffered(1))
    return pl.pallas_call(
        body, name="sb_attn_fwd",
        out_shape=(act, act, jax.ShapeDtypeStruct((N_DEV,) + w_out_shard.shape, w_out_shard.dtype)),
        grid=(n_groups, NBLK),
        in_specs=[blk(COL_Q // STEP_W), whole(COL_K // STEP_W), whole(COL_V // STEP_W), _ANY],
        out_specs=(blk(0), blk(0), _ANY),
        scratch_shapes=[pltpu.VMEM((LP + BLK, STEP_W), BF16), pltpu.VMEM((LP + BLK, STEP_W), BF16)] + _ALL_GATHER_SEMS,
        compiler_params=_params(("arbitrary", "arbitrary")),
    )(proj, proj, proj, w_out_shard)


def _sb_attention_bwd(proj, o_sb, do_sb, chip_sums):
    heads = range(STEP_HEADS)
    n_groups = N_HEADS // STEP_HEADS

    def body(q_ref, k_ref, v_ref, o_ref, do_ref, src_ref, dq_ref, dk_ref, dv_ref, out_ref,
             kb_ref, vb_ref, dka_ref, dva_ref, send_sems, recv_sems, loc_sem):
        grp, qi = pl.program_id(0), pl.program_id(1)
        _stage_kv(qi, k_ref, v_ref, kb_ref, vb_ref)

        @pl.when(jnp.logical_and(grp == 0, qi == 0))
        def _():
            for cp in _chip_exchange_copies(src_ref, out_ref, send_sems, recv_sems, loc_sem):
                cp.start()

        @pl.when(qi == 0)
        def _():
            dka_ref[...] = jnp.zeros_like(dka_ref)
            dva_ref[...] = jnp.zeros_like(dva_ref)

        lane = _iota((BLK, BLK), 1)
        head0 = lane < HEAD_DIM
        qh, doh, dsum = [], [], []
        for p in range(PAIRS_PER_STEP):
            lanes = slice(p * BLK, (p + 1) * BLK)
            qh += _head_masks(q_ref[:, lanes] * SB_SCALE, lane)
            do = do_ref[:, lanes]
            doh += _head_masks(do, lane)
            prod = do.astype(BF16).astype(F32) * o_ref[:, lanes]
            dsum += [jnp.sum(jnp.where(head0, prod, 0.0), axis=1, keepdims=True),
                     jnp.sum(jnp.where(head0, 0.0, prod), axis=1, keepdims=True)]
        valid2, valid1 = _window_masks(qi)

        def tiles(rows, valid, tri, runs, eruns):
            ks = [kb_ref[rows, _pair_lanes(h)] for h in heads]
            vs = [vb_ref[rows, _pair_lanes(h)] for h in heads]
            parts = [_sb_tile(jnp.where(valid, _dot(qh[h], ks[h], NT), SB_MASKED)) for h in heads]
            afters = [_suffix_sum(parts[h][1], tri) for h in heads]
            if runs is not None:
                afters = [afters[h] + runs[h] for h in heads]
            avals = [jnp.exp(parts[h][2] + afters[h]) for h in heads]
            es = [avals[h] * _dot(doh[h], vs[h], NT) for h in heads]
            esufs = [_suffix_sum(es[h], tri) for h in heads]
            if eruns is not None:
                esufs = [esufs[h] + eruns[h] for h in heads]
            dzs = [(es[h] - parts[h][0] * (dsum[h] - esufs[h])).astype(BF16) for h in heads]
            dqs = [_dot(dzs[h], ks[h], NN) for h in heads]
            dks = [_dot(dzs[h], qh[h], TN) for h in heads]
            dvs = [_dot(avals[h].astype(BF16), doh[h], TN) for h in heads]
            for p in range(PAIRS_PER_STEP):
                dka_ref[rows, p * BLK:(p + 1) * BLK] += dks[2 * p] + dks[2 * p + 1]
                dva_ref[rows, p * BLK:(p + 1) * BLK] += dvs[2 * p] + dvs[2 * p + 1]
            return [(jnp.sum(parts[h][1], axis=1, keepdims=True), jnp.sum(es[h], axis=1, keepdims=True), dqs[h])
                    for h in heads]

        win = pl.ds(pl.multiple_of(qi * BLK, BLK), 2 * BLK)
        init = [t for head in tiles(win, valid2, _strict_upper(2 * BLK), None, None) for t in head]
        tri1 = _strict_upper(BLK)

        def step(kblk, carry):
            rows = pl.ds(pl.multiple_of((kblk + 1) * BLK, BLK), BLK)
            runs = [carry[3 * h] for h in heads]
            eruns = [carry[3 * h + 1] for h in heads]
            new = []
            for h, (d_run, d_erun, d_q) in enumerate(tiles(rows, valid1(kblk), tri1, runs, eruns)):
                new += [carry[3 * h] + d_run, carry[3 * h + 1] + d_erun, carry[3 * h + 2] + d_q]
            return tuple(new)

        res = _sweep(qi - 2, step, init, tuple(3 * h for h in heads))
        for p in range(PAIRS_PER_STEP):
            dq = jnp.where(head0, res[6 * p + 2], res[6 * p + 5]) * SB_SCALE
            dq_ref[:, p * BLK:(p + 1) * BLK] = dq.astype(BF16)

        @pl.when(qi == NBLK - 1)
        def _():
            dk_ref[...] = dka_ref[BLK:, :].astype(BF16)
            dv_ref[...] = dva_ref[BLK:, :].astype(BF16)

        @pl.when(jnp.logical_and(grp == n_groups - 1, qi == NBLK - 1))
        def _():
            for cp in _chip_exchange_copies(src_ref, out_ref, send_sems, recv_sems, loc_sem):
                cp.wait()

    act = jax.ShapeDtypeStruct((LP, 1024), BF16)
    blk = lambda col: pl.BlockSpec((BLK, STEP_W), lambda g, qi: (qi, col + g))
    whole = lambda col: pl.BlockSpec((LP, STEP_W), lambda g, qi: (0, col + g))
    once = lambda col: pl.BlockSpec((LP, STEP_W), lambda g, qi: (0, col + g), pipeline_mode=pl.Buffered(1))
    return pl.pallas_call(
        body, name="sb_attn_bwd",
        out_shape=(act, act, act, jax.ShapeDtypeStruct(chip_sums.shape, chip_sums.dtype)),
        grid=(n_groups, NBLK),
        in_specs=[blk(COL_Q // STEP_W), once(COL_K // STEP_W), once(COL_V // STEP_W), blk(0), blk(0), _ANY],
        out_specs=(blk(0), whole(0), whole(0), _ANY),
        scratch_shapes=[pltpu.VMEM((LP + BLK, STEP_W), BF16), pltpu.VMEM((LP + BLK, STEP_W), BF16),
                        pltpu.VMEM((LP + BLK, STEP_W), F32), pltpu.VMEM((LP + BLK, STEP_W), F32)]
        + _CHIP_EXCHANGE_SEMS,
        compiler_params=_params(("arbitrary", "arbitrary"), BIG_VMEM_LIMIT),
    )(proj, proj, proj, o_sb, do_sb, chip_sums)


def _conv_pre(x, w_ref, b_ref):
    acc = b_ref[...] + w_ref[3:4, :] * x
    for k in range(3):
        acc = acc + w_ref[k:k + 1, :] * pltpu.roll(x, 3 - k, 0)
    return acc


CONV_BLK = 256


def _conv_fwd(proj, conv_w, conv_b):
    def body(x_ref, w_ref, b_ref, o_ref):
        xc = _conv_pre(x_ref[...], w_ref, b_ref)
        o_ref[...] = xc * _sigmoid(xc)

    nb = XBC_W // CONV_BLK
    return pl.pallas_call(
        body, name="conv_fwd", out_shape=jax.ShapeDtypeStruct((LP, XBC_W), F32), grid=(nb,),
        in_specs=[pl.BlockSpec((LP, CONV_BLK), lambda j: (0, COL_XBC // CONV_BLK + j)),
                  pl.BlockSpec((4, CONV_BLK), lambda j: (0, j)), pl.BlockSpec((1, CONV_BLK), lambda j: (0, j))],
        out_specs=pl.BlockSpec((LP, CONV_BLK), lambda j: (0, j)),
        compiler_params=_params(("parallel",)),
    )(proj, conv_w, conv_b)


def _conv_bwd(dxa, proj, conv_w, conv_b):
    def body(d_ref, x_ref, w_ref, b_ref, dx_ref, dw_ref, db_ref):
        x = x_ref[...]
        xc = _conv_pre(x, w_ref, b_ref)
        s = _sigmoid(xc)
        live = _iota((LP, CONV_BLK), 0) >= PAD
        dxc = jnp.where(live, d_ref[...] * (s * (1.0 + xc * (1.0 - s))), 0.0)
        db_ref[...] = jnp.sum(dxc, axis=0, keepdims=True)
        dx = w_ref[3:4, :] * dxc
        dw_ref[3:4, :] = jnp.sum(dxc * x, axis=0, keepdims=True)
        for k in range(3):
            dw_ref[k:k + 1, :] = jnp.sum(dxc * pltpu.roll(x, 3 - k, 0), axis=0, keepdims=True)
            dx = dx + w_ref[k:k + 1, :] * pltpu.roll(dxc, LP - (3 - k), 0)
        dx_ref[...] = dx.astype(BF16)

    nb = XBC_W // CONV_BLK
    col = lambda j: (0, j)
    wide = pl.BlockSpec((LP, CONV_BLK), col)
    return pl.pallas_call(
        body, name="conv_bwd",
        out_shape=(jax.ShapeDtypeStruct((LP, XBC_W), BF16), jax.ShapeDtypeStruct((4, XBC_W), F32),
                   jax.ShapeDtypeStruct((1, XBC_W), F32)),
        grid=(nb,),
        in_specs=[wide, pl.BlockSpec((LP, CONV_BLK), lambda j: (0, COL_XBC // CONV_BLK + j)),
                  pl.BlockSpec((4, CONV_BLK), col), pl.BlockSpec((1, CONV_BLK), col)],
        out_specs=(wide, pl.BlockSpec((4, CONV_BLK), col), pl.BlockSpec((1, CONV_BLK), col)),
        compiler_params=_params(("parallel",)),
    )(dxa, proj, conv_w, conv_b)


def _ssd_prelude(c, dt_ref, dtt_ref, dtb_ref, dtbt_ref, alog_ref, alogt_ref):
    live = jnp.logical_or(c > 0, _iota((BLK, N_HEADS), 0) >= PAD)
    live_t = jnp.logical_or(c > 0, _iota((N_HEADS, BLK), 1) >= PAD)
    pre = dt_ref[...] + dtb_ref[...]
    pre_t = dtt_ref[...] + dtbt_ref[...]
    dt = jnp.where(live, _softplus(pre), 0.0)
    dt_t = jnp.where(live_t, _softplus(pre_t), 0.0)
    a = -jnp.exp(alog_ref[...])
    a_t = -jnp.exp(alogt_ref[...])
    li = _iota((BLK, BLK), 0)
    si = _iota((BLK, BLK), 1)
    lower = (si <= li).astype(BF16)
    upper = (li <= si).astype(BF16)
    acum = _dot_exact_x(lower, dt * a, NN)
    acum_t = _dot_x_exact(dt_t * a_t, upper, NN)
    return live, pre, dt, a, a_t, acum, acum_t


def _head_expand():
    return (_iota((N_HEADS, SSD_W), 1) // HEAD_DIM == _iota((N_HEADS, SSD_W), 0)).astype(BF16)


def _head_reduce_mat():
    return (_iota((SSD_W, N_HEADS), 0) // HEAD_DIM == _iota((SSD_W, N_HEADS), 1)).astype(BF16)


def _decay_mat(acum, acum_t, h, causal):
    seg = jnp.minimum(acum[:, h:h + 1] - acum_t[h:h + 1, :], 0.0)
    return jnp.where(causal, jnp.exp(seg), 0.0)


def _ssd_fwd(xa, dt_raw, dt_raw_t, dt_bias, dt_bias_t, a_log, a_log_t, d_exp):
    def body(x_ref, b_ref, c_ref, dt_ref, dtt_ref, dtb_ref, dtbt_ref, alog_ref, alogt_ref, dexp_ref,
             y_ref, hs_ref, state_ref):
        c = pl.program_id(0)

        @pl.when(c == 0)
        def _():
            state_ref[...] = jnp.zeros_like(state_ref)

        _, _, dt, _, _, acum, acum_t = _ssd_prelude(c, dt_ref, dtt_ref, dtb_ref, dtbt_ref, alog_ref, alogt_ref)
        expand = _head_expand()
        x = x_ref[...]
        xdt = x * _dot_x_exact(dt, expand, n=2)
        exp_a = _dot_x_exact(jnp.exp(acum), expand, n=2)
        to_end = _dot_x_exact(jnp.exp(acum[BLK - 1:BLK, :] - acum), expand, n=2)
        xdt_b = xdt.astype(BF16)
        xd_b = (xdt * to_end).astype(BF16)
        chunk_decay = jnp.exp(acum_t[:, BLK - 1:BLK])
        hs_ref[0] = state_ref[...]
        lane = _iota((BLK, BLK), 1)
        causal = _iota((BLK, BLK), 0) >= lane
        gw = HEADS_PER_GROUP * HEAD_DIM
        for g in range(N_GROUPS):
            bg = b_ref[:, g * N_STATE:(g + 1) * N_STATE].astype(BF16)
            cg = c_ref[:, g * N_STATE:(g + 1) * N_STATE].astype(BF16)
            cb = _dot(cg, bg, NT)
            hg = state_ref[g * gw:(g + 1) * gw, :]
            ch = _dot(cg, hg.astype(BF16), NT)
            st = _dot(xd_b[:, g * gw:(g + 1) * gw], bg, TN)
            for p in range(HEADS_PER_GROUP // 2):
                h0 = g * HEADS_PER_GROUP + 2 * p
                lo = h0 * HEAD_DIM
                xp = xdt_b[:, lo:lo + BLK]
                w0 = (cb * _decay_mat(acum, acum_t, h0, causal)).astype(BF16)
                w1 = (cb * _decay_mat(acum, acum_t, h0 + 1, causal)).astype(BF16)
                yd = jnp.where(lane < HEAD_DIM, _dot(w0, xp), _dot(w1, xp))
                y_ref[:, lo:lo + BLK] = (yd + ch[:, lo - g * gw:lo - g * gw + BLK] * exp_a[:, lo:lo + BLK]
                                         + x[:, lo:lo + BLK] * dexp_ref[:, lo:lo + BLK])
            for r in range(HEADS_PER_GROUP):
                h = g * HEADS_PER_GROUP + r
                state_ref[h * HEAD_DIM:(h + 1) * HEAD_DIM, :] = (
                    hg[r * HEAD_DIM:(r + 1) * HEAD_DIM, :] * chunk_decay[h:h + 1, :]
                    + st[r * HEAD_DIM:(r + 1) * HEAD_DIM, :])

    chunk = lambda width, col=0: pl.BlockSpec((BLK, width), lambda c: (c, col))
    return pl.pallas_call(
        body, name="ssd_fwd",
        out_shape=(jax.ShapeDtypeStruct((LP, SSD_W), F32), jax.ShapeDtypeStruct((NBLK, SSD_W, N_STATE), F32)),
        grid=(NBLK,),
        in_specs=[chunk(SSD_W), chunk(256, 4), chunk(256, 5), chunk(N_HEADS),
                  pl.BlockSpec((N_HEADS, BLK), lambda c: (0, c)), _const_spec((1, N_HEADS)),
                  _const_spec((N_HEADS, 1)), _const_spec((1, N_HEADS)), _const_spec((N_HEADS, 1)),
                  _const_spec((1, SSD_W))],
        out_specs=(chunk(SSD_W), pl.BlockSpec((1, SSD_W, N_STATE), lambda c: (c, 0, 0))),
        scratch_shapes=[pltpu.VMEM((SSD_W, N_STATE), F32)],
        compiler_params=_params(("arbitrary",)),
    )(xa, xa, xa, dt_raw, dt_raw_t, dt_bias, dt_bias_t, a_log, a_log_t, d_exp)


def _ssd_bwd(xa, dt_raw, dt_raw_t, dt_bias, dt_bias_t, a_log, a_log_t, d_exp, hstart, dy):
    def body(x_ref, b_ref, c_ref, dt_ref, dtt_ref, dtb_ref, dtbt_ref, alog_ref, alogt_ref, dexp_ref,
             hs_ref, dy_ref, dxa_ref, ddt_ref, dbias_ref, dalog_ref, dd_ref, dstate_ref):
        step = pl.program_id(0)
        c = NBLK - 1 - step

        @pl.when(step == 0)
        def _():
            dstate_ref[...] = jnp.zeros_like(dstate_ref)
            dbias_ref[...] = jnp.zeros_like(dbias_ref)
            dalog_ref[...] = jnp.zeros_like(dalog_ref)
            dd_ref[...] = jnp.zeros_like(dd_ref)

        live, pre, dt, a, a_t, acum, acum_t = _ssd_prelude(c, dt_ref, dtt_ref, dtb_ref, dtbt_ref,
                                                           alog_ref, alogt_ref)
        expand = _head_expand()
        reduce_m = _head_reduce_mat()
        x = x_ref[...]
        dyv = dy_ref[...]
        dt_e = _dot_x_exact(dt, expand, n=2)
        xdt = x * dt_e
        exp_acum = jnp.exp(acum)
        exp_a = _dot_x_exact(exp_acum, expand, n=2)
        dte = jnp.exp(acum[BLK - 1:BLK, :] - acum)
        to_end = _dot_x_exact(dte, expand, n=2)
        xdt_b = xdt.astype(BF16)
        xd_b = (xdt * to_end).astype(BF16)
        chunk_decay = jnp.exp(acum_t[:, BLK - 1:BLK])
        lane = _iota((BLK, BLK), 1)
        head0 = lane < HEAD_DIM
        causal = _iota((BLK, BLK), 0) >= lane
        gw = HEADS_PER_GROUP * HEAD_DIM
        dm = dyv * exp_a
        dm_b = dm.astype(BF16)
        onehot = lambda h: (_iota((1, N_HEADS), 1) == h).astype(F32)
        onehot_t = lambda h: (_iota((N_HEADS, 1), 0) == h).astype(F32)
        dacum = jnp.zeros((BLK, N_HEADS), F32)
        dacum_t = jnp.zeros((N_HEADS, BLK), F32)
        head_sums = lambda prod, rows: _dot_x_exact(prod, reduce_m[rows, :], n=2)
        dskip_acc = head_sums(dyv * x, slice(0, SSD_W))
        ddt_acc = jnp.zeros((BLK, N_HEADS), F32)
        ddte_acc = jnp.zeros((BLK, N_HEADS), F32)
        dexpa_acc = jnp.zeros((BLK, N_HEADS), F32)
        head_sum = expand
        for g in range(N_GROUPS):
            bg = b_ref[:, g * N_STATE:(g + 1) * N_STATE].astype(BF16)
            cg = c_ref[:, g * N_STATE:(g + 1) * N_STATE].astype(BF16)
            cb = _dot(cg, bg, NT)
            hg = hs_ref[0, g * gw:(g + 1) * gw, :]
            hg_b = hg.astype(BF16)
            dhe = dstate_ref[g * gw:(g + 1) * gw, :]
            dhe_b = dhe.astype(BF16)
            ch = _dot(cg, hg_b, NT)
            dcg = _dot(dm_b[:, g * gw:(g + 1) * gw], hg_b, NN)
            dhs = _dot(dm_b[:, g * gw:(g + 1) * gw], cg, TN)
            dxd = _dot(bg, dhe_b, NT)
            dbg = _dot(xd_b[:, g * gw:(g + 1) * gw], dhe_b, NN)
            dcb = jnp.zeros((BLK, BLK), F32)
            dxdt_g = []
            for p in range(HEADS_PER_GROUP // 2):
                h0 = g * HEADS_PER_GROUP + 2 * p
                lo = h0 * HEAD_DIM
                xp = xdt_b[:, lo:lo + BLK]
                dyp = dyv[:, lo:lo + BLK]
                dyh = (jnp.where(head0, dyp, 0.0).astype(BF16), jnp.where(head0, 0.0, dyp).astype(BF16))
                dxdt_p = jnp.zeros((BLK, BLK), F32)
                for q in range(2):
                    h = h0 + q
                    dec = _decay_mat(acum, acum_t, h, causal)
                    w = cb * dec
                    dw = _dot(dyh[q], xp, NT)
                    t = dw * w
                    dacum = dacum + jnp.sum(t, axis=1, keepdims=True) * onehot(h)
                    dacum_t = dacum_t - jnp.sum(t, axis=0, keepdims=True) * onehot_t(h)
                    dcb = dcb + dw * dec
                    dxdt_p = dxdt_p + _dot(w.astype(BF16), dyh[q], TN)
                sl = slice(lo, lo + BLK)
                gl = slice(lo - g * gw, lo - g * gw + BLK)
                dxdt_p = dxdt_p + dxd[:, gl] * to_end[:, sl]
                dxa_ref[:, sl] = dyp * dexp_ref[:, sl] + dxdt_p * dt_e[:, sl]
                dxdt_g.append(dxdt_p)
            cols = slice(g * gw, (g + 1) * gw)
            ddt_acc = ddt_acc + head_sums(jnp.concatenate(dxdt_g, axis=1) * x[:, cols], cols)
            ddte_acc = ddte_acc + head_sums(dxd * xdt[:, cols], cols)
            dexpa_acc = dexpa_acc + head_sums(dyv[:, cols] * ch, cols)
            dcb_b = dcb.astype(BF16)
            b_col = SSD_W + g * N_STATE
            c_col = SSD_W + (N_GROUPS + g) * N_STATE
            dxa_ref[:, c_col:c_col + N_STATE] = dcg + _dot(dcb_b, bg, NN)
            dxa_ref[:, b_col:b_col + N_STATE] = dbg + _dot(dcb_b, cg, TN)
            prod = dhe * hg
            per_head = jnp.sum(_dot_exact_x(head_sum[:, g * gw:(g + 1) * gw], prod, NN), axis=1, keepdims=True)
            dacum_t = dacum_t + (per_head * chunk_decay) * (_iota((1, BLK), 1) == BLK - 1).astype(F32)
            for r in range(HEADS_PER_GROUP):
                h = g * HEADS_PER_GROUP + r
                rows = slice(h * HEAD_DIM, (h + 1) * HEAD_DIM)
                dstate_ref[rows, :] = (dhs[r * HEAD_DIM:(r + 1) * HEAD_DIM, :]
                                       + dhe[r * HEAD_DIM:(r + 1) * HEAD_DIM, :] * chunk_decay[h:h + 1, :])
        dacum = dacum + dexpa_acc * exp_acum - ddte_acc * dte
        last_row = (_iota((BLK, 1), 0) == BLK - 1).astype(F32)
        dacum = dacum + last_row * jnp.sum(ddte_acc * dte, axis=0, keepdims=True)
        li = _iota((BLK, BLK), 0)
        si = _iota((BLK, BLK), 1)
        upper = (li <= si).astype(BF16)
        lower = (si <= li).astype(BF16)
        dda = _dot_exact_x(upper, dacum, NN)
        dda_t = _dot_x_exact(dacum_t, lower, NN)
        eye = (_iota((N_HEADS, N_HEADS), 0) == _iota((N_HEADS, N_HEADS), 1)).astype(BF16)
        dda = dda + _dot_x_exact_tn(dda_t, eye)
        ddt = ddt_acc + dda * a
        dalog_ref[...] += jnp.sum(dda * dt, axis=0, keepdims=True) * a
        dd_ref[...] += jnp.sum(dskip_acc, axis=0, keepdims=True)
        ddt_raw = jnp.where(live, ddt * _sigmoid(pre), 0.0)
        ddt_ref[...] = jnp.zeros_like(ddt_ref)
        ddt_ref[:, :N_HEADS] = ddt_raw
        dbias_ref[...] += jnp.sum(ddt_raw, axis=0, keepdims=True)

    rev = lambda width, col=0: pl.BlockSpec((BLK, width), lambda s: (NBLK - 1 - s, col))
    vec = jax.ShapeDtypeStruct((1, N_HEADS), F32)
    return pl.pallas_call(
        body, name="ssd_bwd",
        out_shape=(jax.ShapeDtypeStruct((LP, XBC_W), F32), jax.ShapeDtypeStruct((LP, BLK), F32), vec, vec, vec),
        grid=(NBLK,),
        in_specs=[rev(SSD_W), rev(256, 4), rev(256, 5), rev(N_HEADS),
                  pl.BlockSpec((N_HEADS, BLK), lambda s: (0, NBLK - 1 - s)), _const_spec((1, N_HEADS)),
                  _const_spec((N_HEADS, 1)), _const_spec((1, N_HEADS)), _const_spec((N_HEADS, 1)),
                  _const_spec((1, SSD_W)),
                  pl.BlockSpec((1, SSD_W, N_STATE), lambda s: (NBLK - 1 - s, 0, 0)), rev(SSD_W)],
        out_specs=(rev(XBC_W), rev(BLK), _const_spec((1, N_HEADS)),
                   _const_spec((1, N_HEADS)), _const_spec((1, N_HEADS))),
        scratch_shapes=[pltpu.VMEM((SSD_W, N_STATE), F32)],
        compiler_params=_params(("arbitrary",)),
    )(xa, xa, xa, dt_raw, dt_raw_t, dt_bias, dt_bias_t, a_log, a_log_t, d_exp, hstart, dy)


def _dot_x_exact_tn(x_t, eye):
    out = None
    for p in _split(x_t, 3):
        t = _dot(p, eye, TN)
        out = t if out is None else out + t
    return out


def _adamw(name, parts, w, m, v, rows):
    r_all, cols = w.shape
    assert r_all % rows == 0
    c1 = 1.0 / (1.0 - ADAM_B1 ** ADAM_STEP)
    c2 = 1.0 / (1.0 - ADAM_B2 ** ADAM_STEP)

    def body(p_ref, w_ref, m_ref, v_ref, g_ref, d_ref, mo_ref, vo_ref):
        g = p_ref[0].astype(F32)
        for j in range(1, parts.shape[0]):
            g = g + p_ref[j].astype(F32)
        mn = ADAM_B1 * m_ref[...] + (1.0 - ADAM_B1) * g
        vn = ADAM_B2 * v_ref[...] + (1.0 - ADAM_B2) * (g * g)
        g_ref[...] = g
        mo_ref[...] = mn
        vo_ref[...] = vn
        d_ref[...] = -ADAM_LR * ((mn * c1) / (jnp.sqrt(vn * c2) + ADAM_EPS) + ADAM_WD * w_ref[...])

    spec = pl.BlockSpec((rows, cols), lambda i: (i, 0))
    shp = jax.ShapeDtypeStruct((r_all, cols), F32)
    return pl.pallas_call(
        body, name=name, out_shape=(shp, shp, shp, shp), grid=(r_all // rows,),
        in_specs=[pl.BlockSpec((parts.shape[0], rows, cols), lambda i: (0, i, 0)), spec, spec, spec],
        out_specs=(spec, spec, spec, spec),
        compiler_params=_params(("parallel",)),
    )(parts, w, m, v)


_VECTORS = (("norm_w", 1024, 0), ("conv_b", 1536, 8), ("dt_bias", 16, 20), ("a_log", 16, 21), ("d_skip", 16, 22),
            ("sb_norm_w", 1024, 24), ("ssd_norm_w", 1024, 32), ("final_norm_w", 1024, 40))
_LOSS_ROW = 23
_CONVW_ROW = 48
_META_ROW = 96
_PACK_ROWS = 224
_SMALL_ORDER = tuple(name for name, _, _ in _VECTORS) + ("conv_w", "meta_tokens")


def _pack_small_grads(vectors, loss_row, d_convw, dh):
    def body(*refs):
        vec_refs, (loss_ref, cw_ref, dh_ref, out_ref) = refs[:len(_VECTORS)], refs[len(_VECTORS):]
        out_ref[...] = jnp.zeros_like(out_ref)
        for (_, width, row), ref in zip(_VECTORS, vec_refs):
            if width < BLK:
                out_ref[row:row + 1, :width] = ref[...]
            else:
                for t in range(width // BLK):
                    out_ref[row + t:row + t + 1, :] = ref[:, t * BLK:(t + 1) * BLK]
        out_ref[_LOSS_ROW:_LOSS_ROW + 1, :] = loss_ref[...]
        for k in range(4):
            for t in range(XBC_W // BLK):
                r = _CONVW_ROW + k * (XBC_W // BLK) + t
                out_ref[r:r + 1, :] = cw_ref[k:k + 1, t * BLK:(t + 1) * BLK]
        for i in range(N_META):
            for t in range(D_MODEL // BLK):
                r = _META_ROW + i * (D_MODEL // BLK) + t
                out_ref[r:r + 1, :] = dh_ref[i:i + 1, t * BLK:(t + 1) * BLK]

    full = lambda a: pl.BlockSpec(a.shape, lambda i: tuple(0 for _ in a.shape))
    return pl.pallas_call(
        body, name="pack_small_grads", out_shape=jax.ShapeDtypeStruct((_PACK_ROWS, BLK), F32), grid=(1,),
        in_specs=[full(v) for v in vectors] + [full(loss_row), full(d_convw),
                                               pl.BlockSpec((N_META, D_MODEL), lambda i: (PAD // N_META, 0))],
        out_specs=pl.BlockSpec((_PACK_ROWS, BLK), lambda i: (0, 0)),
        compiler_params=_params(("arbitrary",)),
    )(*vectors, loss_row, d_convw, dh)


def _sum_slots_and_swap(name, parts, rows):
    n, r_all, cols = parts.shape
    steps = r_all // rows

    def body(p_ref, o_ref, sib_ref, stage_ref, send_sems, recv_sems):
        i = pl.program_id(0)
        x, y, c = _mesh_position()

        def copy(blk):
            return pltpu.make_async_remote_copy(
                src_ref=stage_ref.at[blk], dst_ref=sib_ref.at[pl.ds(pl.multiple_of(blk * rows, rows), rows)],
                send_sem=send_sems.at[blk], recv_sem=recv_sems.at[blk],
                device_id=(x, y, 1 - c), device_id_type=_MESH)

        acc = p_ref[0].astype(F32)
        for j in range(1, n):
            acc = acc + p_ref[j].astype(F32)
        o_ref[...] = acc
        stage_ref[i] = acc
        copy(i).start()

        @pl.when(i == steps - 1)
        def _():
            for blk in range(steps):
                copy(blk).wait()

    shape = jax.ShapeDtypeStruct((r_all, cols), F32)
    return pl.pallas_call(
        body, name=name, out_shape=(shape, shape), grid=(steps,),
        in_specs=[pl.BlockSpec((n, rows, cols), lambda i: (0, i, 0))],
        out_specs=(pl.BlockSpec((rows, cols), lambda i: (i, 0)), _ANY),
        scratch_shapes=[pltpu.VMEM((steps, rows, cols), F32), pltpu.SemaphoreType.DMA((steps,)),
                        pltpu.SemaphoreType.DMA((steps,))],
        compiler_params=_params(("arbitrary",)),
    )(parts)


def _adamw_small(pack, weights, moms, vels):
    c1 = 1.0 / (1.0 - ADAM_B1 ** ADAM_STEP)
    c2 = 1.0 / (1.0 - ADAM_B2 ** ADAM_STEP)
    n = len(_SMALL_ORDER)

    def body(*refs):
        p_ref = refs[0]
        w_refs, m_refs, v_refs = refs[1:1 + n], refs[1 + n:1 + 2 * n], refs[1 + 2 * n:1 + 3 * n]
        outs = refs[1 + 3 * n:1 + 7 * n]
        loss_ref, g_ref, cw_ref, cws_ref, mt_ref, all_ref, send_sems, recv_sems, loc_sem = refs[1 + 7 * n:]
        x, y, c = _mesh_position()
        me = 4 * x + 2 * y + c
        copies = _all_gather_copies(p_ref, all_ref, send_sems, recv_sems, loc_sem)
        for cp in copies:
            cp.start()
        for cp in copies:
            cp.wait()
        g = all_ref[0]
        for j in range(1, N_DEV):
            g = g + all_ref[j]
        g_ref[...] = g
        loss_ref[...] = g_ref[_LOSS_ROW:_LOSS_ROW + 1, :]

        def update(idx, grad):
            go_ref, d_ref, mo_ref, vo_ref = outs[4 * idx:4 * idx + 4]
            mn = ADAM_B1 * m_refs[idx][...] + (1.0 - ADAM_B1) * grad
            vn = ADAM_B2 * v_refs[idx][...] + (1.0 - ADAM_B2) * (grad * grad)
            go_ref[...] = grad
            mo_ref[...] = mn
            vo_ref[...] = vn
            d_ref[...] = -ADAM_LR * ((mn * c1) / (jnp.sqrt(vn * c2) + ADAM_EPS) + ADAM_WD * w_refs[idx][...])

        for idx, (_, width, row) in enumerate(_VECTORS):
            go_ref = outs[4 * idx]
            if width < BLK:
                grad = g_ref[row:row + 1, :width]
            else:
                for t in range(width // BLK):
                    go_ref[:, t * BLK:(t + 1) * BLK] = g_ref[row + t:row + t + 1, :]
                grad = go_ref[...]
            update(idx, grad)
        cw_ref[...] = jnp.zeros_like(cw_ref)
        for k in range(4):
            for t in range(XBC_W // BLK):
                r = _CONVW_ROW + k * (XBC_W // BLK) + t
                cw_ref[k:k + 1, t * BLK:(t + 1) * BLK] = g_ref[r:r + 1, :]
        for i in range(N_META):
            for t in range(D_MODEL // BLK):
                r = _META_ROW + i * (D_MODEL // BLK) + t
                mt_ref[i:i + 1, t * BLK:(t + 1) * BLK] = g_ref[r:r + 1, :]
        width_cw = XBC_W // N_DEV
        pick_cw = (_iota((XBC_W, width_cw), 0) == me * width_cw + _iota((XBC_W, width_cw), 1)).astype(BF16)
        cws_ref[...] = _dot_x_exact(cw_ref[...], pick_cw)
        update(n - 2, cws_ref[0:4, :])
        pick_mt = (_iota((D_MODEL, BLK), 0) == me * BLK + _iota((D_MODEL, BLK), 1)).astype(BF16)
        update(n - 1, _dot_x_exact(mt_ref[...], pick_mt))

    full = lambda a: pl.BlockSpec(a.shape, lambda i: tuple(0 for _ in a.shape))
    params = list(weights) + list(moms) + list(vels)
    out_shape, out_specs = [], []
    for w in weights:
        for _ in range(4):
            out_shape.append(jax.ShapeDtypeStruct(w.shape, F32))
            out_specs.append(full(w))
    out_shape.append(jax.ShapeDtypeStruct((1, BLK), F32))
    out_specs.append(pl.BlockSpec((1, BLK), lambda i: (0, 0)))
    return pl.pallas_call(
        body, name="adamw_small", out_shape=tuple(out_shape), grid=(1,),
        in_specs=[full(pack)] + [full(a) for a in params], out_specs=tuple(out_specs),
        scratch_shapes=[pltpu.VMEM((_PACK_ROWS, BLK), F32), pltpu.VMEM((8, XBC_W), F32),
                        pltpu.VMEM((8, XBC_W // N_DEV), F32), pltpu.VMEM((N_META, D_MODEL), F32),
                        pltpu.VMEM((N_DEV, _PACK_ROWS, BLK), F32)] + _ALL_GATHER_SEMS,
        compiler_params=_params(("arbitrary",)),
    )(pack, *params)


def kernel(x, meta_tokens, norm_w, w_in, conv_w, conv_b, dt_bias, a_log, d_skip, sb_norm_w, ssd_norm_w, w_out, final_norm_w, loss_target, m_meta_tokens, m_norm_w, m_w_in, m_conv_w, m_conv_b, m_dt_bias, m_a_log, m_d_skip, m_sb_norm_w, m_ssd_norm_w, m_w_out, m_final_norm_w, v_meta_tokens, v_norm_w, v_w_in, v_conv_w, v_conv_b, v_dt_bias, v_a_log, v_d_skip, v_sb_norm_w, v_ssd_norm_w, v_w_out, v_final_norm_w):
    small_src = jnp.concatenate([conv_w[0].reshape(6, BLK), meta_tokens, jnp.zeros((2, BLK), F32)], axis=0)
    small_g, w_in_g = _gather_weights([small_src, w_in[0].astype(BF16)])
    w_in_full = w_in_g.transpose(1, 0, 2).reshape(D_MODEL, D_IN)
    w_dt = w_in_full[:, D_MAIN:]
    conv_w_full = small_g[:, :6].reshape(N_DEV, 4, 192).transpose(1, 0, 2).reshape(4, XBC_W)
    meta_full = small_g[:, 6:6 + N_META].transpose(1, 0, 2).reshape(N_META, D_MODEL)
    h_pad = jnp.concatenate([jnp.zeros((PAD, D_MODEL), F32), meta_full, x[0]], axis=0)
    dt_bias_t = dt_bias.reshape(N_HEADS, 1)
    a_log_t = a_log.reshape(N_HEADS, 1)
    d_exp = jnp.repeat(d_skip, HEAD_DIM, axis=1)
    fnw = final_norm_w.reshape(1, D_MODEL)

    u, u_t, dt_raw, dt_raw_t = _prenorm(h_pad, norm_w, w_dt, w_dt.T)
    proj = _matmul("in_proj", u, w_in_full, "nn", LP, 512, D_MODEL, n_cols=D_MAIN)
    o_sb, o_sb_exact, w_out_g = _sb_attention_fwd(proj, w_out[0].astype(BF16))
    w_out_full = w_out_g.reshape(2 * SSD_W, D_MODEL)
    xa = _conv_fwd(proj, conv_w_full, conv_b)
    o_ssd, hstart = _ssd_fwd(xa, dt_raw, dt_raw_t, dt_bias, dt_bias_t, a_log, a_log_t, d_exp)
    ycat, dh2, dh2_b, loss_row, d_fnw = _out_head(
        o_sb, proj, o_ssd, sb_norm_w, ssd_norm_w, w_out_full, h_pad, fnw, loss_target[0])

    g_w_out = _matmul("d_w_out", ycat, dh2_b, "tn", 512, 512, LP, BF16).reshape(4, 2, 256, D_MODEL)
    do_sb, dg, do_ssd, dz, d_sbw, d_ssdw, sib_w_out = _ycat_bwd(
        dh2_b, w_out_full, o_sb, proj, o_ssd, sb_norm_w, ssd_norm_w, g_w_out)
    chip_w_out = _pair_sum("pair_sum_w_out", g_w_out, sib_w_out, 256)
    dq, dk, dv, p_w_out = _sb_attention_bwd(proj, o_sb_exact, do_sb, chip_w_out)
    dxa, ddt_raw, d_dtb, d_alog, d_dskip = _ssd_bwd(
        xa, dt_raw, dt_raw_t, dt_bias, dt_bias_t, a_log, a_log_t, d_exp, hstart, do_ssd)
    dxbc, d_convw, d_convb = _conv_bwd(dxa, proj, conv_w_full, conv_b)
    pieces = [dq, dk, dv, dg, dz, dxbc, ddt_raw]
    g_w_in_mine, g_w_in_sib = _d_w_in(u_t, pieces)
    chip_w_in = _add_halves("pair_sum_w_in", g_w_in_mine, g_w_in_sib, 128)
    dh, d_nw, win_parts = _d_u_prenorm_bwd(pieces, w_in_full, w_dt, h_pad, norm_w, dh2, chip_w_in)
    win_mine, win_other = _sum_slots_and_swap("sum_w_in_windows", win_parts, 128)
    core = lax.axis_index("c")
    chip = 2 * lax.axis_index("x") + lax.axis_index("y")
    first_col = (D_IN // N_DEV) * (2 * chip + core) - WIN_STRIDE * chip
    cut = lambda w: lax.dynamic_slice(w, (0, first_col), (D_MODEL // 2, D_IN // N_DEV))
    half_mine, half_other = cut(win_mine), cut(win_other)
    p_w_in = jnp.concatenate([jnp.where(core == 0, half_mine, half_other),
                              jnp.where(core == 0, half_other, half_mine)], axis=0)[None]

    pack = _pack_small_grads([d_nw, d_convb, d_dtb, d_alog, d_dskip, d_sbw, d_ssdw, d_fnw], loss_row, d_convw, dh)

    res_in = _adamw("adamw_w_in", p_w_in, w_in[0], m_w_in[0], v_w_in[0], 128)
    res_out = _adamw("adamw_w_out", p_w_out, w_out[0], m_w_out[0], v_w_out[0], 128)
    res_small = _adamw_small(
        pack,
        [norm_w, conv_b, dt_bias, a_log, d_skip, sb_norm_w, ssd_norm_w, fnw, conv_w[0], meta_tokens],
        [m_norm_w, m_conv_b, m_dt_bias, m_a_log, m_d_skip, m_sb_norm_w, m_ssd_norm_w,
         m_final_norm_w.reshape(1, D_MODEL), m_conv_w[0], m_meta_tokens],
        [v_norm_w, v_conv_b, v_dt_bias, v_a_log, v_d_skip, v_sb_norm_w, v_ssd_norm_w,
         v_final_norm_w.reshape(1, D_MODEL), v_conv_w[0], v_meta_tokens])

    loss = jnp.sum(res_small[-1])
    order = ["meta_tokens", "norm_w", "w_in", "conv_w", "conv_b", "dt_bias", "a_log", "d_skip",
             "sb_norm_w", "ssd_norm_w", "w_out", "final_norm_w"]
    outs = [loss, dh[BLK:].reshape(1, SEQ, D_MODEL)]
    for kind in range(4):
        small = {name: res_small[4 * idx + kind] for idx, name in enumerate(_SMALL_ORDER)}
        small["final_norm_w"] = small["final_norm_w"].reshape(D_MODEL)
        small["conv_w"] = small["conv_w"].reshape(1, 4, XBC_W // N_DEV)
        small["w_in"] = res_in[kind].reshape(1, D_MODEL, D_IN // N_DEV)
        small["w_out"] = res_out[kind].reshape(1, 256, D_MODEL)
        outs += [small[name] for name in order]
    return tuple(outs)
```
